```python
import math
import jax, jax.numpy as jnp
from jax import lax
import numpy as np

D_MODEL = 1024
BATCH = 8
SEQ = 2048
DEPTH = 4

GRID_W = 64
CTX_LEN = 256
MIX_WIDTH = D_MODEL
S5_WIDTH = (3 * MIX_WIDTH) // 4
FNET_WIDTH = MIX_WIDTH - S5_WIDTH
S5_H = 16
S5_GROUPS = S5_WIDTH // S5_H
S5_P = 64
FNET_GROUPS = 4
FNET_GW = FNET_WIDTH // FNET_GROUPS
POOL_WIDTH = MIX_WIDTH // 2
CONV_WIDTH = MIX_WIDTH - POOL_WIDTH
POOL_WINDOWS = (2, 4, 8, 16)
POOL_GROUPS = len(POOL_WINDOWS)
POOL_GW = POOL_WIDTH // POOL_GROUPS
CONV_K = 3
N_EVEN = (DEPTH + 1) // 2
N_ODD = DEPTH // 2
EVEN_IN = S5_WIDTH + FNET_WIDTH + MIX_WIDTH
ODD_IN = POOL_WIDTH + 3 * CONV_WIDTH + MIX_WIDTH
STEP_MIN = 1e-3
STEP_MAX = 1e-1
EPS = 1e-6
POS_BASE = 10000.0

kernel_name = 'hybrid_s5_fnet_pool_shortconv_prefix'


def rmsnorm(x, g):
    xf = x.astype(jnp.float32)
    y = xf * lax.rsqrt(jnp.mean(xf * xf, axis=-1, keepdims=True) + EPS)
    return (y * g.astype(jnp.float32)).astype(x.dtype)


def ada_params(cond, w, b):
    m = jax.nn.silu(cond) @ w + b
    return jnp.split(m, 3, axis=-1)


def sincos_2d(n_tok, dim):
    rows = n_tok // GRID_W
    r = jnp.arange(rows, dtype=jnp.float32)
    col = jnp.arange(GRID_W, dtype=jnp.float32)
    rr, cc = jnp.meshgrid(r, col, indexing='ij')
    rr = rr.reshape(-1, 1)
    cc = cc.reshape(-1, 1)
    quarter = dim // 4
    omega = POS_BASE ** (-jnp.arange(quarter, dtype=jnp.float32) / quarter)
    return jnp.concatenate([jnp.sin(rr * omega), jnp.cos(rr * omega),
                            jnp.sin(cc * omega), jnp.cos(cc * omega)], axis=-1)


def _linear_recurrence(e1, e2):
    a1, b1 = e1
    a2, b2 = e2
    return a2 * a1, a2 * b1 + b2


def s5_direction(u, lam_re, lam_im, log_step, b_re, b_im, c_re, c_im):
    f32 = jnp.float32
    lam = lax.complex(lam_re.astype(f32), lam_im.astype(f32))
    step = jnp.exp(log_step.astype(f32))[:, None]
    lam_bar = jnp.exp(lam * step)
    b_bar = ((lam_bar - 1.0) / lam)[..., None] * lax.complex(b_re.astype(f32), b_im.astype(f32))
    bu = lax.complex(jnp.einsum('btgh,gph->btgp', u, b_bar.real),
                     jnp.einsum('btgh,gph->btgp', u, b_bar.imag))
    a = jnp.broadcast_to(lam_bar, bu.shape)
    _, s = lax.associative_scan(_linear_recurrence, (a, bu), axis=1)
    return (jnp.einsum('btgp,ghp->btgh', s.real, c_re.astype(f32))
            - jnp.einsum('btgp,ghp->btgh', s.imag, c_im.astype(f32)))


def s5_bidir(u_lat, u_ctx, need_ctx, lam_re, lam_im, log_step, b_re, b_im, c_re, c_im, d_skip):
    bsz, t_lat, _ = u_lat.shape
    t_ctx = u_ctx.shape[1]
    ul = u_lat.astype(jnp.float32).reshape(bsz, t_lat, S5_GROUPS, S5_H)
    uc = u_ctx.astype(jnp.float32).reshape(bsz, t_ctx, S5_GROUPS, S5_H)
    yf = s5_direction(jnp.concatenate([uc, ul], axis=1),
                      lam_re[0], lam_im[0], log_step[0], b_re[0], b_im[0], c_re[0], c_im[0])
    yb = s5_direction(jnp.concatenate([jnp.flip(uc, 1), jnp.flip(ul, 1)], axis=1),
                      lam_re[1], lam_im[1], log_step[1], b_re[1], b_im[1], c_re[1], c_im[1])
    d = d_skip.astype(jnp.float32)
    y_lat = yf[:, t_ctx:] + jnp.flip(yb[:, t_ctx:], 1) + d * ul
    y_lat = y_lat.reshape(bsz, t_lat, S5_WIDTH)
    if not need_ctx:
        return y_lat, None
    y_ctx = yf[:, :t_ctx] + jnp.flip(yb[:, :t_ctx], 1) + d * uc
    return y_lat, y_ctx.reshape(bsz, t_ctx, S5_WIDTH)


def fnet_mix(u, f_w):
    bsz, t, _ = u.shape
    g = u.astype(jnp.float32).reshape(bsz, t, FNET_GROUPS, FNET_GW).transpose(0, 2, 1, 3)
    f = jnp.fft.fft2(g, axes=(-2, -1), norm='ortho').real
    y = jnp.einsum('bgtc,gcd->btgd', f, f_w.astype(jnp.float32))
    return y.reshape(bsz, t, FNET_WIDTH)


def even_branch_out(ya, ub, z, glu_w, glu_b, f_w, w_out):
    ya = jax.nn.gelu(ya)
    ya = ya * jax.nn.sigmoid(ya @ glu_w.astype(jnp.float32) + glu_b.astype(jnp.float32))
    yb = fnet_mix(ub, f_w)
    y = jnp.concatenate([ya, yb], axis=-1).astype(z.dtype) * jax.nn.silu(z)
    return y @ w_out


def even_mixer(hl, hc, need_ctx, w_in, w_out, lam_re, lam_im, log_step, b_re, b_im,
               c_re, c_im, d_skip, glu_w, glu_b, f_w):
    pl = hl @ w_in
    pc = hc @ w_in if need_ctx else hc @ w_in[:, :S5_WIDTH]
    ya_l, ya_c = s5_bidir(pl[..., :S5_WIDTH], pc[..., :S5_WIDTH], need_ctx,
                          lam_re, lam_im, log_step, b_re, b_im, c_re, c_im, d_skip)
    out_l = even_branch_out(ya_l, pl[..., S5_WIDTH:MIX_WIDTH], pl[..., MIX_WIDTH:],
                            glu_w, glu_b, f_w, w_out)
    if not need_ctx:
        return out_l, None
    out_c = even_branch_out(ya_c, pc[..., S5_WIDTH:MIX_WIDTH], pc[..., MIX_WIDTH:],
                            glu_w, glu_b, f_w, w_out)
    return out_l, out_c


def centered_mean(v, window):
    bsz, t, ch = v.shape
    s = jnp.concatenate([jnp.zeros((bsz, 1, ch), v.dtype), jnp.cumsum(v, axis=1)], axis=1)
    pos = jnp.arange(t)
    lo = jnp.clip(pos - window // 2, 0, t)
    hi = jnp.clip(pos + window // 2, 0, t)
    total = jnp.take(s, hi, axis=1) - jnp.take(s, lo, axis=1)
    return total / (hi - lo).astype(v.dtype)[None, :, None]


def pool_mix(u, p_w, p_scale):
    bsz, t, _ = u.shape
    g = u.astype(jnp.float32).reshape(bsz, t, POOL_GROUPS, POOL_GW)
    pooled = jnp.stack([centered_mean(g[:, :, i], POOL_WINDOWS[i]) - g[:, :, i]
                        for i in range(POOL_GROUPS)], axis=2)
    y = jnp.einsum('btgc,gcd->btgd', pooled, p_w.astype(jnp.float32)).reshape(bsz, t, POOL_WIDTH)
    return (y * p_scale.astype(jnp.float32)).astype(u.dtype)


def short_conv(v, w):
    return lax.conv_general_dilated(v, w[:, None, :].astype(v.dtype), window_strides=(1,),
                                    padding=[(CONV_K // 2, CONV_K // 2)],
                                    dimension_numbers=('NWC', 'WIO', 'NWC'),
                                    feature_group_count=v.shape[-1])


def odd_mixer(h, w_in, w_out, p_w, p_scale, conv_w):
    p = h @ w_in
    o1 = POOL_WIDTH
    o2 = o1 + CONV_WIDTH
    o3 = o2 + CONV_WIDTH
    o4 = o3 + CONV_WIDTH
    y_c = pool_mix(p[..., :o1], p_w, p_scale)
    c_gate, x_path, b_gate, z = p[..., o1:o2], p[..., o2:o3], p[..., o3:o4], p[..., o4:]
    y_d = b_gate * short_conv(c_gate * x_path, conv_w)
    y = jnp.concatenate([y_c, y_d], axis=-1) * jax.nn.silu(z)
    return y @ w_out


def setup_inputs(seed: int = 0) -> dict:
    key = jax.random.key(seed)
    ks = jax.random.split(key, 26)
    f32 = jnp.float32
    D = D_MODEL

    def nrm(k, shape, s):
        return jax.random.normal(k, shape, f32) * s

    s5_state_shape = (N_EVEN, 2, S5_GROUPS, S5_P)
    return {
        'x': nrm(ks[0], (BATCH, SEQ, D), 1.0),
        'c': nrm(ks[1], (BATCH, D), 1.0),
        'ctx': nrm(ks[2], (BATCH, CTX_LEN, D), 1.0),
        'c_ctx': nrm(ks[3], (D,), 1.0),
        'norm_g': 1.0 + nrm(ks[4], (DEPTH, D), 0.02),
        'ada_w': nrm(ks[5], (DEPTH, D, 3 * D), 0.5 * D ** -0.5),
        'ada_b': nrm(ks[6], (DEPTH, 3 * D), 0.02),
        'even_w_in': nrm(ks[7], (N_EVEN, D, EVEN_IN), D ** -0.5),
        'even_w_out': nrm(ks[8], (N_EVEN, MIX_WIDTH, D), MIX_WIDTH ** -0.5),
        's5_lam_re': -0.5 + nrm(ks[9], s5_state_shape, 0.01),
        's5_lam_im': jnp.pi * jnp.arange(S5_P, dtype=f32) + nrm(ks[10], s5_state_shape, 0.01),
        's5_log_step': jax.random.uniform(ks[11], (N_EVEN, 2, S5_GROUPS), f32,
                                          math.log(STEP_MIN), math.log(STEP_MAX)),
        's5_b_re': nrm(ks[12], (N_EVEN, 2, S5_GROUPS, S5_P, S5_H), (2 * S5_H) ** -0.5),
        's5_b_im': nrm(ks[13], (N_EVEN, 2, S5_GROUPS, S5_P, S5_H), (2 * S5_H) ** -0.5),
        's5_c_re': nrm(ks[14], (N_EVEN, 2, S5_GROUPS, S5_H, S5_P), S5_P ** -0.5),
        's5_c_im': nrm(ks[15], (N_EVEN, 2, S5_GROUPS, S5_H, S5_P), S5_P ** -0.5),
        's5_d': 1.0 + nrm(ks[16], (N_EVEN, S5_GROUPS, S5_H), 0.1),
        's5_glu_w': nrm(ks[17], (N_EVEN, S5_WIDTH, S5_WIDTH), S5_WIDTH ** -0.5),
        's5_glu_b': nrm(ks[18], (N_EVEN, S5_WIDTH), 0.02),
        'fnet_w': nrm(ks[19], (N_EVEN, FNET_GROUPS, FNET_GW, FNET_GW), FNET_GW ** -0.5),
        'odd_w_in': nrm(ks[20], (N_ODD, D, ODD_IN), D ** -0.5),
        'odd_w_out': nrm(ks[21], (N_ODD, MIX_WIDTH, D), MIX_WIDTH ** -0.5),
        'pool_w': nrm(ks[22], (N_ODD, POOL_GROUPS, POOL_GW, POOL_GW), POOL_GW ** -0.5),
        'pool_scale': 1.0 + nrm(ks[23], (N_ODD, POOL_WIDTH), 0.1),
        'conv_w': nrm(ks[24], (N_ODD, CONV_K, CONV_WIDTH), CONV_K ** -0.5),
        'final_g': 1.0 + nrm(ks[25], (D,), 0.02),
    }


def reference(x, c, ctx, c_ctx, norm_g, ada_w, ada_b, even_w_in, even_w_out,
              s5_lam_re, s5_lam_im, s5_log_step, s5_b_re, s5_b_im, s5_c_re, s5_c_im,
              s5_d, s5_glu_w, s5_glu_b, fnet_w, odd_w_in, odd_w_out, pool_w,
              pool_scale, conv_w, final_g):
    n_tok = x.shape[1]
    xl = x + sincos_2d(n_tok, x.shape[-1]).astype(x.dtype)[None]
    xc = ctx
    need_ctx = [any(j % 2 == 0 for j in range(l + 1, DEPTH)) for l in range(DEPTH)]
    for l in range(DEPTH):
        i = l // 2
        shift, scale, gate = ada_params(c, ada_w[l], ada_b[l])
        hl = rmsnorm(xl, norm_g[l]) * (1.0 + scale[:, None]) + shift[:, None]
        use_ctx_in = (l % 2 == 0) or need_ctx[l]
        if use_ctx_in:
            shift_c, scale_c, gate_c = ada_params(c_ctx, ada_w[l], ada_b[l])
            hc = rmsnorm(xc, norm_g[l]) * (1.0 + scale_c) + shift_c
        if l % 2 == 0:
            out_l, out_c = even_mixer(hl, hc, need_ctx[l], even_w_in[i], even_w_out[i],
                                      s5_lam_re[i], s5_lam_im[i], s5_log_step[i],
                                      s5_b_re[i], s5_b_im[i], s5_c_re[i], s5_c_im[i],
                                      s5_d[i], s5_glu_w[i], s5_glu_b[i], fnet_w[i])
        else:
            out_l = odd_mixer(hl, odd_w_in[i], odd_w_out[i], pool_w[i], pool_scale[i], conv_w[i])
            out_c = (odd_mixer(hc, odd_w_in[i], odd_w_out[i], pool_w[i], pool_scale[i], conv_w[i])
                     if need_ctx[l] else None)
        xl = xl + gate[:, None] * out_l.astype(xl.dtype)
        if need_ctx[l]:
            xc = xc + gate_c * out_c.astype(xc.dtype)
    return rmsnorm(xl, final_g)
```

```python
import functools
import math

import numpy as np
import jax
import jax.numpy as jnp
from jax import lax
from jax.experimental import pallas as pl
from jax.experimental.pallas import tpu as pltpu

D = 1024
MIX = 1024
S5_W = 768
FN_W = 256
S5_H = 16
S5_G = 48
S5_P = 64
FN_G = 4
FN_GW = 64
POOL_W = 512
CONV_W = 512
POOL_WINDOWS = (2, 4, 8, 16)
POOL_GW = 128
GRID_W = 64
EPS = 1e-6
POS_BASE = 10000.0
CH = 16
HALO = 8
VMEM_LIMIT = 56 * 1024 * 1024

F32 = jnp.float32
BF16 = jnp.bfloat16
HI = lax.Precision.HIGHEST


def _cparams(sem):
    return pltpu.CompilerParams(dimension_semantics=sem, vmem_limit_bytes=VMEM_LIMIT)


def _silu(v):
    return v * (1.0 / (1.0 + jnp.exp(-v)))


def _sigmoid(v):
    return 1.0 / (1.0 + jnp.exp(-v))


def _gelu_tanh(v):
    c = math.sqrt(2.0 / math.pi)
    return 0.5 * v * (1.0 + jnp.tanh(c * (v + 0.044715 * (v * v * v))))


def _mod_norm(x, g, scale, shift):
    ms = jnp.mean(x * x, axis=-1, keepdims=True)
    y = x * lax.rsqrt(ms + EPS) * g
    return y * (1.0 + scale) + shift


def _ada_kernel(c_ref, w_ref, b_ref, o_ref):
    s = _silu(c_ref[...])
    o_ref[...] = jnp.dot(s, w_ref[...], precision=HI, preferred_element_type=F32) + b_ref[...]


def _ada_all(cond, ada_w, ada_b):
    depth = ada_w.shape[0]
    return pl.pallas_call(
        _ada_kernel,
        grid=(depth, 3),
        in_specs=[
            pl.BlockSpec((16, D), lambda l, j: (0, 0)),
            pl.BlockSpec((None, D, D), lambda l, j: (l, 0, j)),
            pl.BlockSpec((None, None, 1, D), lambda l, j: (l, j, 0, 0)),
        ],
        out_specs=pl.BlockSpec((None, None, 16, D), lambda l, j: (l, j, 0, 0)),
        out_shape=jax.ShapeDtypeStruct((depth, 3, 16, D), F32),
        compiler_params=_cparams(("arbitrary", "arbitrary")),
    )(cond, ada_w, ada_b.reshape(depth, 3, 1, D))


def _s5_tables_kernel(lr_ref, li_ref, lrc_ref, lic_ref, ls_ref, br_ref, bi_ref, cr_ref, ci_ref,
                      k_ref, cw_ref, e_ref, l16_ref):
    step = jnp.exp(ls_ref[...])
    a_r = lr_ref[...] * step
    b_r = li_ref[...] * step
    kk = lax.broadcasted_iota(jnp.int32, (CH + 1, S5_P), 0).astype(F32)
    mag = jnp.exp(kk * a_r)
    pwre = mag * jnp.cos(kk * b_r)
    pwim = mag * jnp.sin(kk * b_r)

    lrc = lrc_ref[...]
    lic = lic_ref[...]
    a_c = lrc * step
    b_c = lic * step
    kc = lax.broadcasted_iota(jnp.int32, (S5_P, CH), 1).astype(F32)
    magc = jnp.exp(kc * a_c)
    pwre_c = magc * jnp.cos(kc * b_c)
    pwim_c = magc * jnp.sin(kc * b_c)

    e1 = jnp.exp(a_c)
    n_re = e1 * jnp.cos(b_c) - 1.0
    n_im = e1 * jnp.sin(b_c)
    den = lrc * lrc + lic * lic
    co_re = (n_re * lrc + n_im * lic) / den
    co_im = (n_im * lrc - n_re * lic) / den
    br = br_ref[...]
    bi = bi_ref[...]
    bb_re = co_re * br - co_im * bi
    bb_im = co_re * bi + co_im * br
    bb = jnp.concatenate([bb_re, bb_im], axis=0)

    cr = cr_ref[...]
    ci = ci_ref[...]
    for k in range(CH + 1):
        pr = pwre[k:k + 1, :]
        pi = pwim[k:k + 1, :]
        w_re = cr * pr - ci * pi
        w_im = cr * pi + ci * pr
        cw = jnp.concatenate([w_re, -w_im], axis=1)
        cw_ref[k] = cw
        if k < CH:
            k_ref[k] = jnp.dot(cw, bb, precision=HI, preferred_element_type=F32)
            qr = pwre_c[:, k:k + 1]
            qi = pwim_c[:, k:k + 1]
            e_ref[k] = jnp.concatenate([qr * bb_re - qi * bb_im, qr * bb_im + qi * bb_re], axis=1)
    l16_ref[...] = jnp.concatenate([pwre[CH:CH + 1, :], pwim[CH:CH + 1, :]], axis=0)


def _s5_tables(lam_re, lam_im, log_step, b_re, b_im, c_re, c_im):
    n = lam_re.size // S5_P
    lr = lam_re.reshape(n, 1, S5_P)
    li = lam_im.reshape(n, 1, S5_P)
    lrc = lam_re.reshape(n, S5_P, 1)
    lic = lam_im.reshape(n, S5_P, 1)
    ls = log_step.reshape(n, 1, 1)
    br = b_re.reshape(n, S5_P, S5_H)
    bi = b_im.reshape(n, S5_P, S5_H)
    cr = c_re.reshape(n, S5_H, S5_P)
    ci = c_im.reshape(n, S5_H, S5_P)

    def spec(*shape):
        nd = len(shape)
        return pl.BlockSpec((None,) + shape, lambda g: (g,) + (0,) * nd)

    return pl.pallas_call(
        _s5_tables_kernel,
        grid=(n,),
        in_specs=[spec(1, S5_P), spec(1, S5_P), spec(S5_P, 1), spec(S5_P, 1), spec(1, 1),
                  spec(S5_P, S5_H), spec(S5_P, S5_H), spec(S5_H, S5_P), spec(S5_H, S5_P)],
        out_specs=[spec(CH, S5_H, S5_H), spec(CH + 1, S5_H, 2 * S5_P), spec(CH, S5_P, 2 * S5_H),
                   spec(2, S5_P)],
        out_shape=[jax.ShapeDtypeStruct((n, CH, S5_H, S5_H), F32),
                   jax.ShapeDtypeStruct((n, CH + 1, S5_H, 2 * S5_P), F32),
                   jax.ShapeDtypeStruct((n, CH, S5_P, 2 * S5_H), F32),
                   jax.ShapeDtypeStruct((n, 2, S5_P), F32)],
        compiler_params=_cparams(("arbitrary",)),
    )(lr, li, lrc, lic, ls, br, bi, cr, ci)


def _s5_assemble(kt, cw, et, l16, d_skip):
    idx = np.arange(CH)
    lag = idx[None, :] - idx[:, None]
    fsel = jnp.asarray((lag >= 0).astype(np.float32))[None, :, :, None, None]
    bsel = jnp.asarray((lag <= 0).astype(np.float32))[None, :, :, None, None]
    mf = kt[0][:, np.clip(lag, 0, CH - 1)] * fsel
    mb = kt[1][:, np.clip(-lag, 0, CH - 1)] * bsel
    eye_t = jnp.asarray(np.eye(CH, dtype=np.float32))[None, :, :, None, None]
    eye_h = jnp.asarray(np.eye(S5_H, dtype=np.float32))[None, None, None, :, :]
    skip = eye_t * eye_h * d_skip[:, None, None, :, None]
    m = mf + mb + skip
    mt = m.transpose(0, 2, 3, 1, 4).reshape(S5_G, CH * S5_H, CH * S5_H)

    ef = et[0][:, ::-1]
    eb = et[1]

    def to_rows(e):
        return e.transpose(0, 2, 1, 3).reshape(S5_G, S5_P, CH * S5_H)

    bend_t = jnp.concatenate([to_rows(ef[..., :S5_H]), to_rows(eb[..., :S5_H]),
                              to_rows(ef[..., S5_H:]), to_rows(eb[..., S5_H:])], axis=1)

    cf = cw[0][:, 1:CH + 1]
    cb = cw[1][:, CH - idx]
    zero = jnp.zeros((S5_G, CH * S5_H, S5_P), F32)

    def to_cols(c):
        return c.reshape(S5_G, CH * S5_H, S5_P)

    cpa = jnp.concatenate([to_cols(cf[..., :S5_P]), zero, to_cols(cf[..., S5_P:]), zero], axis=2)
    cpb = jnp.concatenate([zero, to_cols(cb[..., :S5_P]), zero, to_cols(cb[..., S5_P:])], axis=2)
    lam16 = jnp.concatenate([l16[0], l16[1]], axis=2)
    return mt.astype(BF16), bend_t.astype(BF16), cpa.astype(BF16), cpb.astype(BF16), lam16


def _even_in_kernel(*refs, has_pos, full):
    it = iter(refs)
    x_ref = next(it)
    pos_ref = next(it) if has_pos else None
    shift_ref, scale_ref, g_ref, wat_ref = next(it), next(it), next(it), next(it)
    wbz_ref = next(it) if full else None
    ut_ref = next(it)
    if full:
        ub_ref, z_ref = next(it), next(it)

    x = x_ref[...]
    if has_pos:
        x = x + pos_ref[...][None]
    h = _mod_norm(x, g_ref[...], scale_ref[...], shift_ref[...])
    nb, nc, _ = x.shape
    hb = h.reshape(nb * nc, D).astype(BF16)
    pt = lax.dot_general(wat_ref[...], hb, (((1,), (1,)), ((), ())), preferred_element_type=F32)
    ut_ref[...] = pt.astype(BF16).reshape(S5_G, S5_H, nb * nc)
    if full:
        p = jnp.dot(hb, wbz_ref[...], preferred_element_type=F32)
        ub_ref[...] = p[:, :FN_W].astype(BF16)
        z_ref[...] = p[:, FN_W:]


def _even_in(x, pos, shift, scale, g, w_in, *, full):
    bsz, t, _ = x.shape
    nc = t // CH
    rows = bsz * nc
    xv = x.reshape(bsz, nc, CH * D)
    wat = w_in[:, :S5_W].T.astype(BF16)
    args = [xv]
    specs = [pl.BlockSpec((bsz, nc, D), lambda i: (0, 0, i))]
    if pos is not None:
        args.append(pos.reshape(nc, CH * D))
        specs.append(pl.BlockSpec((nc, D), lambda i: (0, i)))
    args += [shift, scale, g.reshape(1, D), wat]
    specs += [pl.BlockSpec((bsz, 1, D), lambda i: (0, 0, 0)),
              pl.BlockSpec((bsz, 1, D), lambda i: (0, 0, 0)),
              pl.BlockSpec((1, D), lambda i: (0, 0)),
              pl.BlockSpec((S5_W, D), lambda i: (0, 0))]
    out_shape = [jax.ShapeDtypeStruct((S5_G, CH * S5_H, rows), BF16)]
    out_specs = [pl.BlockSpec((S5_G, S5_H, rows), lambda i: (0, i, 0))]
    if full:
        args.append(w_in[:, S5_W:].astype(BF16))
        specs.append(pl.BlockSpec((D, FN_W + MIX), lambda i: (0, 0)))
        out_shape += [jax.ShapeDtypeStruct((CH, rows, FN_W), BF16),
                      jax.ShapeDtypeStruct((CH, rows, MIX), F32)]
        out_specs += [pl.BlockSpec((None, rows, FN_W), lambda i: (i, 0, 0)),
                      pl.BlockSpec((None, rows, MIX), lambda i: (i, 0, 0))]
    return pl.pallas_call(
        functools.partial(_even_in_kernel, has_pos=pos is not None, full=full),
        grid=(CH,),
        in_specs=specs,
        out_specs=out_specs,
        out_shape=out_shape,
        compiler_params=_cparams(("arbitrary",)),
    )(*args)


def _s5_kernel(utl_ref, utc_ref, mt_ref, bt_ref, cpa_ref, cpb_ref, l16_ref, ytl_ref, ytc_ref,
               sre_ref, sim_ref, are_ref, aim_ref, bre_ref, bim_ref, *, bsz, ncl, ncc):
    nl = bsz * ncl
    nx = bsz * ncc
    nst = 2 * S5_P
    ul = utl_ref[...]
    uc = utc_ref[...]
    mt = mt_ref[...]
    bt = bt_ref[...]
    a1l = jnp.dot(mt, ul, preferred_element_type=F32)
    a1c = jnp.dot(mt, uc, preferred_element_type=F32)
    sl = jnp.dot(bt, ul, preferred_element_type=F32).T
    sc = jnp.dot(bt, uc, preferred_element_type=F32).T
    sre_ref[0:nl, :] = sl[:, :nst]
    sim_ref[0:nl, :] = sl[:, nst:]
    sre_ref[nl:nl + nx, :] = sc[:, :nst]
    sim_ref[nl:nl + nx, :] = sc[:, nst:]

    l16 = l16_ref[...]
    lre = l16[0:1, :]
    lim = l16[1:2, :]
    is_fwd = lax.broadcasted_iota(jnp.int32, (bsz, nst), 1) < S5_P

    def make_step(base, nchunk):
        def step(c, carry):
            sre, sim = carry
            rf = pl.ds(base + c, bsz, stride=nchunk)
            rb = pl.ds(base + (nchunk - 1 - c), bsz, stride=nchunk)
            are_ref[rf, :] = sre
            aim_ref[rf, :] = sim
            bre_ref[rb, :] = sre
            bim_ref[rb, :] = sim
            in_re = jnp.where(is_fwd, sre_ref[rf, :], sre_ref[rb, :])
            in_im = jnp.where(is_fwd, sim_ref[rf, :], sim_ref[rb, :])
            return (lre * sre - lim * sim + in_re, lre * sim + lim * sre + in_im)
        return step

    zero = jnp.zeros((bsz, nst), F32)
    carry = lax.fori_loop(0, ncc, make_step(nl, ncc), (zero, zero))
    lax.fori_loop(0, ncl, make_step(0, ncl), carry)

    nt = (((1,), (1,)), ((), ()))
    cpa = cpa_ref[...]
    cpb = cpb_ref[...]

    def carried(r0, r1):
        s0a = jnp.concatenate([are_ref[r0:r1, :], aim_ref[r0:r1, :]], axis=1).astype(BF16)
        s0b = jnp.concatenate([bre_ref[r0:r1, :], bim_ref[r0:r1, :]], axis=1).astype(BF16)
        return (lax.dot_general(cpa, s0a, nt, preferred_element_type=F32)
                + lax.dot_general(cpb, s0b, nt, preferred_element_type=F32))

    ytl_ref[...] = a1l + carried(0, nl)
    ytc_ref[...] = a1c + carried(nl, nl + nx)


def _s5_mix(utl, utc, mt, bend_t, cpa, cpb, lam16, bsz):
    nl = utl.shape[2]
    nx = utc.shape[2]
    kk = CH * S5_H
    nst = 4 * S5_P

    def gspec(r, c):
        return pl.BlockSpec((None, r, c), lambda g: (g, 0, 0))

    return pl.pallas_call(
        functools.partial(_s5_kernel, bsz=bsz, ncl=nl // bsz, ncc=nx // bsz),
        grid=(S5_G,),
        in_specs=[gspec(kk, nl), gspec(kk, nx), gspec(kk, kk), gspec(nst, kk), gspec(kk, nst),
                  gspec(kk, nst), gspec(2, 2 * S5_P)],
        out_specs=[gspec(kk, nl), gspec(kk, nx)],
        out_shape=[jax.ShapeDtypeStruct((S5_G, kk, nl), F32),
                   jax.ShapeDtypeStruct((S5_G, kk, nx), F32)],
        scratch_shapes=[pltpu.VMEM((nl + nx, 2 * S5_P), F32)] * 6,
        compiler_params=_cparams(("arbitrary",)),
    )(utl, utc, mt, bend_t, cpa, cpb, lam16)


def _fnet_kernel(cm_ref, x_ref, ccs_ref, fw_ref, o_ref):
    fw = fw_ref[...]
    ccs = ccs_ref[...]
    wc = jnp.dot(ccs[:, :FN_W], fw, precision=HI, preferred_element_type=F32)
    ws = jnp.dot(ccs[:, FN_W:], fw, precision=HI, preferred_element_type=F32)
    xs = x_ref[...]
    x = xs.reshape(xs.shape[0] * xs.shape[1], FN_W)
    xc = jnp.dot(x, wc.astype(BF16), preferred_element_type=F32).astype(BF16)
    xsn = jnp.dot(x, ws.astype(BF16), preferred_element_type=F32).astype(BF16)
    v = jnp.concatenate([xc, xsn], axis=0)
    y = jnp.dot(cm_ref[...], v, preferred_element_type=F32)
    o_ref[...] = y.reshape(o_ref.shape).astype(o_ref.dtype)


def _dft_time_matrix(nc):
    t = nc * CH
    order = (np.arange(nc)[None, :] * CH + np.arange(CH)[:, None]).reshape(-1)
    prod = (order[:, None].astype(np.int64) * order[None, :].astype(np.int64)) % t
    ang = prod.astype(np.float64) * (2.0 * np.pi / t)
    scale = 1.0 / math.sqrt(t * FN_GW)
    return jnp.asarray(np.concatenate([np.cos(ang), -np.sin(ang)], axis=1) * scale, dtype=F32).astype(BF16)


def _dft_channel_matrix():
    c = np.arange(FN_GW)
    ang = (c[:, None] * c[None, :] % FN_GW).astype(np.float64) * (2.0 * np.pi / FN_GW)
    eye = np.eye(FN_G)
    return jnp.asarray(np.concatenate([np.kron(eye, np.cos(ang)), np.kron(eye, np.sin(ang))], axis=1),
                       dtype=F32)


def _fnet(ub, fw_bd, bsz):
    rows = ub.shape[1]
    nc = rows // bsz
    t = nc * CH
    rt = min(t, 512)
    ipt = rt // nc
    cm = _dft_time_matrix(nc)
    xv = ub.reshape(CH, bsz, nc, FN_W)
    out = pl.pallas_call(
        _fnet_kernel,
        grid=(t // rt, bsz),
        in_specs=[pl.BlockSpec((rt, 2 * t), lambda r, b: (r, 0)),
                  pl.BlockSpec((CH, None, nc, FN_W), lambda r, b: (0, b, 0, 0)),
                  pl.BlockSpec((FN_W, 2 * FN_W), lambda r, b: (0, 0)),
                  pl.BlockSpec((FN_W, FN_W), lambda r, b: (0, 0))],
        out_specs=pl.BlockSpec((ipt, None, nc, FN_W), lambda r, b: (r, b, 0, 0)),
        out_shape=jax.ShapeDtypeStruct((CH, bsz, nc, FN_W), F32),
        compiler_params=_cparams(("arbitrary", "arbitrary")),
    )(cm, xv, _dft_channel_matrix(), fw_bd)
    return out.reshape(CH, rows, FN_W)


def _even_out_kernel(*refs, has_pos):
    it = iter(refs)
    yt_ref, yb_ref, z_ref, x_ref = next(it), next(it), next(it), next(it)
    pos_ref = next(it) if has_pos else None
    gate_ref, gw_ref, gb_ref, woa_ref, wob_ref, o_ref = (next(it), next(it), next(it), next(it),
                                                         next(it), next(it))
    yt = yt_ref[...]
    rows = yt.shape[2]
    ya = _gelu_tanh(yt.reshape(S5_W, rows).T)
    glu = jnp.dot(ya.astype(BF16), gw_ref[...], preferred_element_type=F32) + gb_ref[...]
    ya = ya * _sigmoid(glu)
    z = z_ref[...]
    sz = _silu(z)
    ma = (ya * sz[:, :S5_W]).astype(BF16)
    mb = (yb_ref[...] * sz[:, S5_W:]).astype(BF16)
    out = (jnp.dot(ma, woa_ref[...], preferred_element_type=F32)
           + jnp.dot(mb, wob_ref[...], preferred_element_type=F32))
    x = x_ref[...]
    if has_pos:
        x = x + pos_ref[...][None]
    o_ref[...] = x + gate_ref[...] * out.reshape(x.shape)


def _even_out(yt, yb, z, x, pos, gate, glu_w, glu_b, w_out):
    bsz, t, _ = x.shape
    nc = t // CH
    rows = bsz * nc
    xv = x.reshape(bsz, nc, CH * D)
    args = [yt, yb, z, xv]
    specs = [pl.BlockSpec((S5_G, S5_H, rows), lambda j: (0, j, 0)),
             pl.BlockSpec((None, rows, FN_W), lambda j: (j, 0, 0)),
             pl.BlockSpec((None, rows, MIX), lambda j: (j, 0, 0)),
             pl.BlockSpec((bsz, nc, D), lambda j: (0, 0, j))]
    if pos is not None:
        args.append(pos.reshape(nc, CH * D))
        specs.append(pl.BlockSpec((nc, D), lambda j: (0, j)))
    args += [gate, glu_w.astype(BF16), glu_b.reshape(1, S5_W), w_out[:S5_W].astype(BF16),
             w_out[S5_W:].astype(BF16)]
    specs += [pl.BlockSpec((bsz, 1, D), lambda j: (0, 0, 0)),
              pl.BlockSpec((S5_W, S5_W), lambda j: (0, 0)),
              pl.BlockSpec((1, S5_W), lambda j: (0, 0)),
              pl.BlockSpec((S5_W, D), lambda j: (0, 0)),
              pl.BlockSpec((FN_W, D), lambda j: (0, 0))]
    out = pl.pallas_call(
        functools.partial(_even_out_kernel, has_pos=pos is not None),
        grid=(CH,),
        in_specs=specs,
        out_specs=pl.BlockSpec((bsz, nc, D), lambda j: (0, 0, j)),
        out_shape=jax.ShapeDtypeStruct((bsz, nc, CH * D), F32),
        compiler_params=_cparams(("arbitrary",)),
    )(*args)
    return out.reshape(bsz, t, D)


def _odd_kernel(*refs, tm, t_total, final):
    it = iter(refs)
    xm_ref, xp_ref, xn_ref = next(it), next(it), next(it)
    shift_ref, scale_ref, gate_ref, g_ref = next(it), next(it), next(it), next(it)
    w1_ref, w2_ref, wo_ref, pw_ref, ps_ref, cw_ref = (next(it), next(it), next(it), next(it),
                                                      next(it), next(it))
    fg_ref = next(it) if final else None
    o_ref = next(it)
    pe_ref = next(it)

    ti = pl.program_id(1)
    xm = xm_ref[...]
    xe = jnp.concatenate([xp_ref[...], xm, xn_ref[...]], axis=0)
    he = _mod_norm(xe, g_ref[...], scale_ref[...], shift_ref[...]).astype(BF16)
    pe = jnp.dot(he, w1_ref[...], preferred_element_type=F32)
    te = ti * tm - HALO + lax.broadcasted_iota(jnp.int32, (tm + 2 * HALO, 1), 0)
    valid = jnp.logical_and(te >= 0, te < t_total)
    uc = jnp.where(valid, pe[:, :POOL_W], 0.0)
    v = jnp.where(valid, pe[:, POOL_W:POOL_W + CONV_W] * pe[:, POOL_W + CONV_W:], 0.0)
    pe_ref[:, 0:POOL_W] = uc
    pe_ref[:, POOL_W:POOL_W + CONV_W] = v

    p2 = jnp.dot(he[HALO:HALO + tm], w2_ref[...], preferred_element_type=F32)
    b_gate = p2[:, :CONV_W]
    sz = _silu(p2[:, CONV_W:])

    tpos = ti * tm + lax.broadcasted_iota(jnp.int32, (tm, 1), 0)
    ne = tm + 2 * HALO
    pooled = []
    for gi, w in enumerate(POOL_WINDOWS):
        c0 = gi * POOL_GW
        s = pe_ref[:, c0:c0 + POOL_GW]
        n = ne
        width = 1
        while width < w:
            s = s[0:n - width] + s[width:n]
            n -= width
            width *= 2
        start = HALO - w // 2
        total = s[start:start + tm]
        hi = jnp.minimum(tpos + w // 2, t_total)
        lo = jnp.maximum(tpos - w // 2, 0)
        cnt = (hi - lo).astype(F32)
        centre = pe_ref[HALO:HALO + tm, c0:c0 + POOL_GW]
        pg = total / cnt - centre
        pooled.append(jnp.dot(pg.astype(BF16), pw_ref[gi], preferred_element_type=F32))
    y_c = jnp.concatenate(pooled, axis=1) * ps_ref[...]

    cwt = cw_ref[...]
    vm = pe_ref[HALO - 1:HALO - 1 + tm, POOL_W:POOL_W + CONV_W]
    v0 = pe_ref[HALO:HALO + tm, POOL_W:POOL_W + CONV_W]
    vp = pe_ref[HALO + 1:HALO + 1 + tm, POOL_W:POOL_W + CONV_W]
    y_d = b_gate * (vm * cwt[0:1, :] + v0 * cwt[1:2, :] + vp * cwt[2:3, :])

    y = (jnp.concatenate([y_c, y_d], axis=1) * sz).astype(BF16)
    out = jnp.dot(y, wo_ref[...], preferred_element_type=F32)
    xo = xm + gate_ref[...] * out
    if final:
        ms = jnp.mean(xo * xo, axis=-1, keepdims=True)
        xo = xo * lax.rsqrt(ms + EPS) * fg_ref[...]
    o_ref[...] = xo


def _odd_layer(x, shift, scale, gate, g, w_in, w_out, pool_w, pool_scale, conv_w, final_g):
    bsz, t, _ = x.shape
    tm = min(t, 512)
    nt = t // tm
    hb = tm // HALO
    nhb = t // HALO
    n1 = POOL_W + 2 * CONV_W
    final = final_g is not None
    args = [x, x, x, shift, scale, gate, g.reshape(1, D),
            w_in[:, :n1].astype(BF16), w_in[:, n1:].astype(BF16), w_out.astype(BF16),
            pool_w.astype(BF16), pool_scale.reshape(1, POOL_W), conv_w]
    vec = pl.BlockSpec((None, 1, D), lambda b, i: (b, 0, 0))
    specs = [pl.BlockSpec((None, tm, D), lambda b, i: (b, i, 0)),
             pl.BlockSpec((None, HALO, D), lambda b, i: (b, jnp.maximum(i * hb - 1, 0), 0)),
             pl.BlockSpec((None, HALO, D), lambda b, i: (b, jnp.minimum((i + 1) * hb, nhb - 1), 0)),
             vec, vec, vec,
             pl.BlockSpec((1, D), lambda b, i: (0, 0)),
             pl.BlockSpec((D, n1), lambda b, i: (0, 0)),
             pl.BlockSpec((D, CONV_W + MIX), lambda b, i: (0, 0)),
             pl.BlockSpec((MIX, D), lambda b, i: (0, 0)),
             pl.BlockSpec((len(POOL_WINDOWS), POOL_GW, POOL_GW), lambda b, i: (0, 0, 0)),
             pl.BlockSpec((1, POOL_W), lambda b, i: (0, 0)),
             pl.BlockSpec((3, CONV_W), lambda b, i: (0, 0))]
    if final:
        args.append(final_g.reshape(1, D))
        specs.append(pl.BlockSpec((1, D), lambda b, i: (0, 0)))
    return pl.pallas_call(
        functools.partial(_odd_kernel, tm=tm, t_total=t, final=final),
        grid=(bsz, nt),
        in_specs=specs,
        out_specs=pl.BlockSpec((None, tm, D), lambda b, i: (b, i, 0)),
        out_shape=jax.ShapeDtypeStruct((bsz, t, D), F32),
        scratch_shapes=[pltpu.VMEM((tm + 2 * HALO, POOL_W + CONV_W), F32)],
        compiler_params=_cparams(("arbitrary", "arbitrary")),
    )(*args)


def _sincos_table(n_tok, dim):
    rows = n_tok // GRID_W
    rr, cc = np.meshgrid(np.arange(rows, dtype=np.float32), np.arange(GRID_W, dtype=np.float32),
                         indexing='ij')
    rr = jnp.asarray(rr.reshape(-1, 1))
    cc = jnp.asarray(cc.reshape(-1, 1))
    quarter = dim // 4
    omega = POS_BASE ** (-jnp.arange(quarter, dtype=F32) / quarter)
    return jnp.concatenate([jnp.sin(rr * omega), jnp.cos(rr * omega),
                            jnp.sin(cc * omega), jnp.cos(cc * omega)], axis=-1)


def _block_diag(w):
    g, c, _ = w.shape
    eye = jnp.eye(g, dtype=w.dtype)
    return (eye[:, None, :, None] * w[:, :, None, :]).reshape(g * c, g * c)


def kernel(x, c, ctx, c_ctx, norm_g, ada_w, ada_b, even_w_in, even_w_out, s5_lam_re, s5_lam_im, s5_log_step, s5_b_re, s5_b_im, s5_c_re, s5_c_im, s5_d, s5_glu_w, s5_glu_b, fnet_w, odd_w_in, odd_w_out, pool_w, pool_scale, conv_w, final_g):
    bsz, n_tok, _ = x.shape
    depth = norm_g.shape[0]
    n_even = even_w_in.shape[0]

    cond = jnp.concatenate([c, jnp.broadcast_to(c_ctx[None], (16 - bsz, D))], axis=0)
    ada = _ada_all(cond, ada_w, ada_b)

    def lat_mod(l, j):
        return ada[l, j, :bsz].reshape(bsz, 1, D)

    def ctx_mod(l, j):
        return jnp.broadcast_to(ada[l, j, bsz].reshape(1, 1, D), (bsz, 1, D))

    kt, cw, et, l16 = _s5_tables(s5_lam_re, s5_lam_im, s5_log_step, s5_b_re, s5_b_im, s5_c_re, s5_c_im)
    kt = kt.reshape(n_even, 2, S5_G, CH, S5_H, S5_H)
    cw = cw.reshape(n_even, 2, S5_G, CH + 1, S5_H, 2 * S5_P)
    et = et.reshape(n_even, 2, S5_G, CH, S5_P, 2 * S5_H)
    l16 = l16.reshape(n_even, 2, S5_G, 2, S5_P)

    pos = _sincos_table(n_tok, D)
    need_ctx = [any(j % 2 == 0 for j in range(l + 1, depth)) for l in range(depth)]

    xl = x
    xc = ctx
    for l in range(depth):
        i = l // 2
        lpos = pos if l == 0 else None
        last = l == depth - 1
        if l % 2 == 0:
            mt, bend_t, cpa, cpb, lam16 = _s5_assemble(kt[i], cw[i], et[i], l16[i], s5_d[i])
            utl, ubl, zl = _even_in(xl, lpos, lat_mod(l, 0), lat_mod(l, 1), norm_g[l], even_w_in[i],
                                    full=True)
            if need_ctx[l]:
                utc, ubc, zc = _even_in(xc, None, ctx_mod(l, 0), ctx_mod(l, 1), norm_g[l],
                                        even_w_in[i], full=True)
            else:
                (utc,) = _even_in(xc, None, ctx_mod(l, 0), ctx_mod(l, 1), norm_g[l], even_w_in[i],
                                  full=False)
            ytl, ytc = _s5_mix(utl, utc, mt, bend_t, cpa, cpb, lam16, bsz)
            fw_bd = _block_diag(fnet_w[i])
            ybl = _fnet(ubl, fw_bd, bsz)
            xl = _even_out(ytl, ybl, zl, xl, lpos, lat_mod(l, 2), s5_glu_w[i], s5_glu_b[i],
                           even_w_out[i])
            if need_ctx[l]:
                ybc = _fnet(ubc, fw_bd, bsz)
                xc = _even_out(ytc, ybc, zc, xc, None, ctx_mod(l, 2), s5_glu_w[i], s5_glu_b[i],
                               even_w_out[i])
        else:
            if need_ctx[l]:
                xc = _odd_layer(xc, ctx_mod(l, 0), ctx_mod(l, 1), ctx_mod(l, 2), norm_g[l],
                                odd_w_in[i], odd_w_out[i], pool_w[i], pool_scale[i], conv_w[i], None)
            xl = _odd_layer(xl, lat_mod(l, 0), lat_mod(l, 1), lat_mod(l, 2), norm_g[l], odd_w_in[i],
                            odd_w_out[i], pool_w[i], pool_scale[i], conv_w[i],
                            final_g if last else None)
    if depth % 2 == 1:
        raise NotImplementedError("final norm is fused into the last (odd) layer")
    return xl
```

```python
import functools
import math

import numpy as np
import jax
import jax.numpy as jnp
from jax import lax
from jax.experimental import pallas as pl
from jax.experimental.pallas import tpu as pltpu

D = 1024
MIX = 1024
S5_W = 768
FN_W = 256
S5_H = 16
S5_G = 48
S5_P = 64
FN_G = 4
FN_GW = 64
POOL_W = 512
CONV_W = 512
POOL_WINDOWS = (2, 4, 8, 16)
POOL_GW = 128
GRID_W = 64
EPS = 1e-6
POS_BASE = 10000.0
CH = 16
HALO = 8
VMEM_LIMIT = 56 * 1024 * 1024

F32 = jnp.float32
BF16 = jnp.bfloat16
HI = lax.Precision.HIGHEST


def _cparams(sem):
    return pltpu.CompilerParams(dimension_semantics=sem, vmem_limit_bytes=VMEM_LIMIT)


def _silu(v):
    return v * (1.0 / (1.0 + jnp.exp(-v)))


def _sigmoid(v):
    return 1.0 / (1.0 + jnp.exp(-v))


def _gelu_tanh(v):
    c = math.sqrt(2.0 / math.pi)
    return 0.5 * v * (1.0 + jnp.tanh(c * (v + 0.044715 * (v * v * v))))


def _mod_norm(x, g, scale, shift):
    ms = jnp.mean(x * x, axis=-1, keepdims=True)
    y = x * lax.rsqrt(ms + EPS) * g
    return y * (1.0 + scale) + shift


def _ada_kernel(c_ref, w_ref, b_ref, o_ref):
    s = _silu(c_ref[...])
    o_ref[...] = jnp.dot(s, w_ref[...], precision=HI, preferred_element_type=F32) + b_ref[...]


def _ada_all(cond, ada_w, ada_b):
    depth = ada_w.shape[0]
    return pl.pallas_call(
        _ada_kernel,
        grid=(depth, 3),
        in_specs=[
            pl.BlockSpec((16, D), lambda l, j: (0, 0)),
            pl.BlockSpec((None, D, D), lambda l, j: (l, 0, j)),
            pl.BlockSpec((None, None, 1, D), lambda l, j: (l, j, 0, 0)),
        ],
        out_specs=pl.BlockSpec((None, None, 16, D), lambda l, j: (l, j, 0, 0)),
        out_shape=jax.ShapeDtypeStruct((depth, 3, 16, D), F32),
        compiler_params=_cparams(("arbitrary", "arbitrary")),
    )(cond, ada_w, ada_b.reshape(depth, 3, 1, D))


def _split3(v):
    hi = v.astype(BF16)
    lo = (v - hi.astype(F32)).astype(BF16)
    return hi, lo


def _s5_tables_kernel(lr_ref, li_ref, ls_ref, btr_ref, bti_ref, cr_ref, ci_ref, d_ref,
                      mt_ref, bt_ref, cp_ref, l16_ref, ere_ref, eim_ref, *, gb):
    nst = 2 * S5_P
    kk = CH * S5_H
    step = jnp.exp(ls_ref[...])
    lr = lr_ref[...]
    li = li_ref[...]
    a = lr * step
    b = li * step

    def powers(expo):
        mag = jnp.exp(expo * a)
        return mag * jnp.cos(expo * b), mag * jnp.sin(expo * b)

    row = lax.broadcasted_iota(jnp.int32, (1, CH, nst), 1).astype(F32)
    fwd = lax.broadcasted_iota(jnp.int32, (1, CH, nst), 2) < S5_P
    one = jnp.ones((1, 1, nst), F32)

    l1re, l1im = powers(one)
    n_re = l1re - 1.0
    den = lr * lr + li * li
    co_re = (n_re * lr + l1im * li) / den
    co_im = (l1im * lr - n_re * li) / den
    btr = btr_ref[...]
    bti = bti_ref[...]
    bb_re = co_re * btr - co_im * bti
    bb_im = co_re * bti + co_im * btr

    pe_re, pe_im = powers(jnp.where(fwd, (CH - 1) - row, row))
    for l in range(CH):
        pr = pe_re[:, l:l + 1, :]
        pi = pe_im[:, l:l + 1, :]
        ere_ref[:, l * S5_H:(l + 1) * S5_H, :] = pr * bb_re - pi * bb_im
        eim_ref[:, l * S5_H:(l + 1) * S5_H, :] = pr * bb_im + pi * bb_re

    cr = cr_ref[...]
    ci = ci_ref[...]
    pc_re, pc_im = powers(jnp.where(fwd, row + 1.0, CH - row))
    for j in range(CH):
        pr = pc_re[:, j:j + 1, :]
        pi = pc_im[:, j:j + 1, :]
        w_re = cr * pr - ci * pi
        w_im = cr * pi + ci * pr
        cp_ref[:, j * S5_H:(j + 1) * S5_H, :] = jnp.concatenate([w_re, -w_im], axis=2).astype(BF16)

    l16re, l16im = powers(one * float(CH))
    l16_ref[...] = jnp.concatenate([l16re, l16im], axis=1)

    fwd2 = lax.broadcasted_iota(jnp.int32, (S5_H, nst), 1) < S5_P
    lane = lax.broadcasted_iota(jnp.int32, (S5_H, kk), 1)
    iblk = lane // S5_H
    hrow = lax.broadcasted_iota(jnp.int32, (S5_H, kk), 0)
    nt = (((1,), (1,)), ((), ()))
    for g in range(gb):
        ere = ere_ref[g]
        eim = eim_ref[g]
        bt_ref[g, 0:nst, :] = ere.T.astype(BF16)
        bt_ref[g, nst:2 * nst, :] = eim.T.astype(BF16)
        e_hi, e_lo = _split3(jnp.concatenate([ere, eim], axis=1))
        rhs = jnp.concatenate([e_hi, e_hi, e_lo], axis=1)
        crg = cr[g]
        cig = ci[g]

        def lag_kernels(sel):
            c2 = jnp.concatenate([jnp.where(sel, crg, 0.0), jnp.where(sel, -cig, 0.0)], axis=1)
            c_hi, c_lo = _split3(c2)
            lhs = jnp.concatenate([c_hi, c_lo, c_hi], axis=1)
            return lax.dot_general(lhs, rhs, nt, preferred_element_type=F32)

        kf = lag_kernels(fwd2)
        kb = lag_kernels(jnp.logical_not(fwd2))
        dg = d_ref[g]
        for j in range(CH):
            sf = (kk - (CH - 1 - j) * S5_H) % kk
            rf = pltpu.roll(kf, sf, 1) if sf else kf
            rb = pltpu.roll(kb, j * S5_H, 1) if j else kb
            blk = (jnp.where(iblk <= j, rf, 0.0) + jnp.where(iblk >= j, rb, 0.0)
                   + jnp.where(lane == j * S5_H + hrow, dg, 0.0))
            mt_ref[g, j * S5_H:(j + 1) * S5_H, :] = blk.astype(BF16)


def _s5_tables(lam_re, lam_im, log_step, b_re, b_im, c_re, c_im, d_skip):
    n = lam_re.shape[0] * S5_G
    gb = 8
    nst = 2 * S5_P
    kk = CH * S5_H

    def fb(v):
        return jnp.concatenate([v[:, 0], v[:, 1]], axis=-1).reshape(n, v.shape[3], nst)

    lr = fb(lam_re[:, :, :, None, :])
    li = fb(lam_im[:, :, :, None, :])
    ls = fb(jnp.broadcast_to(log_step[:, :, :, None, None], log_step.shape + (1, S5_P)))
    btr = fb(jnp.swapaxes(b_re, -1, -2))
    bti = fb(jnp.swapaxes(b_im, -1, -2))
    cr = fb(c_re)
    ci = fb(c_im)
    d = d_skip.reshape(n, S5_H, 1)

    def spec(r, c):
        return pl.BlockSpec((gb, r, c), lambda g: (g, 0, 0))

    return pl.pallas_call(
        functools.partial(_s5_tables_kernel, gb=gb),
        grid=(n // gb,),
        in_specs=[spec(1, nst), spec(1, nst), spec(1, nst), spec(S5_H, nst), spec(S5_H, nst),
                  spec(S5_H, nst), spec(S5_H, nst), spec(S5_H, 1)],
        out_specs=[spec(kk, kk), spec(2 * nst, kk), spec(kk, 2 * nst), spec(2, nst)],
        out_shape=[jax.ShapeDtypeStruct((n, kk, kk), BF16),
                   jax.ShapeDtypeStruct((n, 2 * nst, kk), BF16),
                   jax.ShapeDtypeStruct((n, kk, 2 * nst), BF16),
                   jax.ShapeDtypeStruct((n, 2, nst), F32)],
        scratch_shapes=[pltpu.VMEM((gb, kk, nst), F32)] * 2,
        compiler_params=_cparams(("arbitrary",)),
    )(lr, li, ls, btr, bti, cr, ci, d)


def _even_in_kernel(*refs, has_pos, full):
    it = iter(refs)
    x_ref = next(it)
    pos_ref = next(it) if has_pos else None
    shift_ref, scale_ref, g_ref, wat_ref = next(it), next(it), next(it), next(it)
    wbz_ref = next(it) if full else None
    ut_ref = next(it)
    if full:
        ub_ref, z_ref = next(it), next(it)

    x = x_ref[...]
    if has_pos:
        x = x + pos_ref[...][None]
    h = _mod_norm(x, g_ref[...], scale_ref[...], shift_ref[...])
    nb, nc, _ = x.shape
    hb = h.reshape(nb * nc, D).astype(BF16)
    pt = lax.dot_general(wat_ref[...], hb, (((1,), (1,)), ((), ())), preferred_element_type=F32)
    ut_ref[...] = pt.astype(BF16).reshape(S5_G, S5_H, nb * nc)
    if full:
        p = jnp.dot(hb, wbz_ref[...], preferred_element_type=F32)
        ub_ref[...] = p[:, :FN_W].astype(BF16)
        z_ref[...] = p[:, FN_W:]


def _even_in(x, pos, shift, scale, g, w_in, *, full):
    bsz, t, _ = x.shape
    nc = t // CH
    rows = bsz * nc
    xv = x.reshape(bsz, nc, CH * D)
    wat = w_in[:, :S5_W].T.astype(BF16)
    args = [xv]
    specs = [pl.BlockSpec((bsz, nc, D), lambda i: (0, 0, i))]
    if pos is not None:
        args.append(pos.reshape(nc, CH * D))
        specs.append(pl.BlockSpec((nc, D), lambda i: (0, i)))
    args += [shift, scale, g.reshape(1, D), wat]
    specs += [pl.BlockSpec((bsz, 1, D), lambda i: (0, 0, 0)),
              pl.BlockSpec((bsz, 1, D), lambda i: (0, 0, 0)),
              pl.BlockSpec((1, D), lambda i: (0, 0)),
              pl.BlockSpec((S5_W, D), lambda i: (0, 0))]
    out_shape = [jax.ShapeDtypeStruct((S5_G, CH * S5_H, rows), BF16)]
    out_specs = [pl.BlockSpec((S5_G, S5_H, rows), lambda i: (0, i, 0))]
    if full:
        args.append(w_in[:, S5_W:].astype(BF16))
        specs.append(pl.BlockSpec((D, FN_W + MIX), lambda i: (0, 0)))
        out_shape += [jax.ShapeDtypeStruct((CH, rows, FN_W), BF16),
                      jax.ShapeDtypeStruct((CH, rows, MIX), F32)]
        out_specs += [pl.BlockSpec((None, rows, FN_W), lambda i: (i, 0, 0)),
                      pl.BlockSpec((None, rows, MIX), lambda i: (i, 0, 0))]
    return pl.pallas_call(
        functools.partial(_even_in_kernel, has_pos=pos is not None, full=full),
        grid=(CH,),
        in_specs=specs,
        out_specs=out_specs,
        out_shape=out_shape,
        compiler_params=_cparams(("arbitrary",)),
    )(*args)


def _s5_kernel(utl_ref, utc_ref, mt_ref, bt_ref, cp_ref, l16_ref, ytl_ref, ytc_ref,
               sre_ref, sim_ref, are_ref, aim_ref, bre_ref, bim_ref, *, bsz, ncl, ncc, gs):
    nl = bsz * ncl
    nx = bsz * ncc
    nst = 2 * S5_P
    for g in range(gs):
        bt = bt_ref[g]
        sl = jnp.dot(bt, utl_ref[g], preferred_element_type=F32).T
        sc = jnp.dot(bt, utc_ref[g], preferred_element_type=F32).T
        sre_ref[g, 0:nl, :] = sl[:, :nst]
        sim_ref[g, 0:nl, :] = sl[:, nst:]
        sre_ref[g, nl:nl + nx, :] = sc[:, :nst]
        sim_ref[g, nl:nl + nx, :] = sc[:, nst:]

    lam = [l16_ref[g] for g in range(gs)]
    is_fwd = lax.broadcasted_iota(jnp.int32, (bsz, nst), 1) < S5_P

    def make_step(base, nchunk):
        def step(c, carry):
            rf = pl.ds(base + c, bsz, stride=nchunk)
            rb = pl.ds(base + (nchunk - 1 - c), bsz, stride=nchunk)
            out = []
            for g in range(gs):
                sre, sim = carry[2 * g], carry[2 * g + 1]
                lre = lam[g][0:1, :]
                lim = lam[g][1:2, :]
                are_ref[g, rf, :] = sre
                aim_ref[g, rf, :] = sim
                bre_ref[g, rb, :] = sre
                bim_ref[g, rb, :] = sim
                in_re = jnp.where(is_fwd, sre_ref[g, rf, :], sre_ref[g, rb, :])
                in_im = jnp.where(is_fwd, sim_ref[g, rf, :], sim_ref[g, rb, :])
                out.append(lre * sre - lim * sim + in_re)
                out.append(lre * sim + lim * sre + in_im)
            return tuple(out)
        return step

    zero = jnp.zeros((bsz, nst), F32)
    carry = lax.fori_loop(0, ncc, make_step(nl, ncc), (zero,) * (2 * gs))
    lax.fori_loop(0, ncl, make_step(0, ncl), carry)

    nt = (((1,), (1,)), ((), ()))
    fwd_rows = lax.broadcasted_iota(jnp.int32, (nl + nx, nst), 1) < S5_P
    for g in range(gs):
        mt = mt_ref[g]
        cp = cp_ref[g]
        s0 = jnp.concatenate([jnp.where(fwd_rows, are_ref[g], bre_ref[g]),
                              jnp.where(fwd_rows, aim_ref[g], bim_ref[g])], axis=1).astype(BF16)
        ytl_ref[g] = (jnp.dot(mt, utl_ref[g], preferred_element_type=F32)
                      + lax.dot_general(cp, s0[0:nl], nt, preferred_element_type=F32))
        ytc_ref[g] = (jnp.dot(mt, utc_ref[g], preferred_element_type=F32)
                      + lax.dot_general(cp, s0[nl:nl + nx], nt, preferred_element_type=F32))


def _s5_mix(utl, utc, mt, bend_t, cp, lam16, bsz, layer):
    nl = utl.shape[2]
    nx = utc.shape[2]
    kk = CH * S5_H
    nst = 2 * S5_P
    gs = 4
    off = layer * (S5_G // gs)

    def gspec(r, c):
        return pl.BlockSpec((gs, r, c), lambda g: (g, 0, 0))

    def tspec(r, c):
        return pl.BlockSpec((gs, r, c), lambda g: (g + off, 0, 0))

    return pl.pallas_call(
        functools.partial(_s5_kernel, bsz=bsz, ncl=nl // bsz, ncc=nx // bsz, gs=gs),
        grid=(S5_G // gs,),
        in_specs=[gspec(kk, nl), gspec(kk, nx), tspec(kk, kk), tspec(2 * nst, kk), tspec(kk, 2 * nst),
                  tspec(2, nst)],
        out_specs=[gspec(kk, nl), gspec(kk, nx)],
        out_shape=[jax.ShapeDtypeStruct((S5_G, kk, nl), F32),
                   jax.ShapeDtypeStruct((S5_G, kk, nx), F32)],
        scratch_shapes=[pltpu.VMEM((gs, nl + nx, nst), F32)] * 6,
        compiler_params=_cparams(("arbitrary",)),
    )(utl, utc, mt, bend_t, cp, lam16)


def _fnet_kernel(cm_ref, x_ref, ccs_ref, fw_ref, o_ref):
    fw = fw_ref[...]
    ccs = ccs_ref[...]
    wc = jnp.dot(ccs[:, :FN_W], fw, precision=HI, preferred_element_type=F32)
    ws = jnp.dot(ccs[:, FN_W:], fw, precision=HI, preferred_element_type=F32)
    xs = x_ref[...]
    x = xs.reshape(xs.shape[0] * xs.shape[1], FN_W)
    xc = jnp.dot(x, wc.astype(BF16), preferred_element_type=F32).astype(BF16)
    xsn = jnp.dot(x, ws.astype(BF16), preferred_element_type=F32).astype(BF16)
    v = jnp.concatenate([xc, xsn], axis=0)
    y = jnp.dot(cm_ref[...], v, preferred_element_type=F32)
    o_ref[...] = y.reshape(o_ref.shape).astype(o_ref.dtype)


def _dft_time_matrix(nc):
    t = nc * CH
    order = (np.arange(nc)[None, :] * CH + np.arange(CH)[:, None]).reshape(-1)
    prod = (order[:, None].astype(np.int64) * order[None, :].astype(np.int64)) % t
    ang = prod.astype(np.float64) * (2.0 * np.pi / t)
    scale = 1.0 / math.sqrt(t * FN_GW)
    return jnp.asarray(np.concatenate([np.cos(ang), -np.sin(ang)], axis=1) * scale, dtype=F32).astype(BF16)


def _dft_channel_matrix():
    c = np.arange(FN_GW)
    ang = (c[:, None] * c[None, :] % FN_GW).astype(np.float64) * (2.0 * np.pi / FN_GW)
    eye = np.eye(FN_G)
    return jnp.asarray(np.concatenate([np.kron(eye, np.cos(ang)), np.kron(eye, np.sin(ang))], axis=1),
                       dtype=F32)


def _fnet(ub, fw_bd, bsz):
    rows = ub.shape[1]
    nc = rows // bsz
    t = nc * CH
    rt = min(t, 512)
    ipt = rt // nc
    cm = _dft_time_matrix(nc)
    xv = ub.reshape(CH, bsz, nc, FN_W)
    out = pl.pallas_call(
        _fnet_kernel,
        grid=(t // rt, bsz),
        in_specs=[pl.BlockSpec((rt, 2 * t), lambda r, b: (r, 0)),
                  pl.BlockSpec((CH, None, nc, FN_W), lambda r, b: (0, b, 0, 0)),
                  pl.BlockSpec((FN_W, 2 * FN_W), lambda r, b: (0, 0)),
                  pl.BlockSpec((FN_W, FN_W), lambda r, b: (0, 0))],
        out_specs=pl.BlockSpec((ipt, None, nc, FN_W), lambda r, b: (r, b, 0, 0)),
        out_shape=jax.ShapeDtypeStruct((CH, bsz, nc, FN_W), F32),
        compiler_params=_cparams(("arbitrary", "arbitrary")),
    )(cm, xv, _dft_channel_matrix(), fw_bd)
    return out.reshape(CH, rows, FN_W)


def _even_out_kernel(*refs, has_pos):
    it = iter(refs)
    yt_ref, yb_ref, z_ref, x_ref = next(it), next(it), next(it), next(it)
    pos_ref = next(it) if has_pos else None
    gate_ref, gw_ref, gb_ref, woa_ref, wob_ref, o_ref = (next(it), next(it), next(it), next(it),
                                                         next(it), next(it))
    yt = yt_ref[...]
    rows = yt.shape[2]
    ya = _gelu_tanh(yt.reshape(S5_W, rows).T)
    glu = jnp.dot(ya.astype(BF16), gw_ref[...], preferred_element_type=F32) + gb_ref[...]
    ya = ya * _sigmoid(glu)
    z = z_ref[...]
    sz = _silu(z)
    ma = (ya * sz[:, :S5_W]).astype(BF16)
    mb = (yb_ref[...] * sz[:, S5_W:]).astype(BF16)
    out = (jnp.dot(ma, woa_ref[...], preferred_element_type=F32)
           + jnp.dot(mb, wob_ref[...], preferred_element_type=F32))
    x = x_ref[...]
    if has_pos:
        x = x + pos_ref[...][None]
    o_ref[...] = x + gate_ref[...] * out.reshape(x.shape)


def _even_out(yt, yb, z, x, pos, gate, glu_w, glu_b, w_out):
    bsz, t, _ = x.shape
    nc = t // CH
    rows = bsz * nc
    xv = x.reshape(bsz, nc, CH * D)
    args = [yt, yb, z, xv]
    specs = [pl.BlockSpec((S5_G, S5_H, rows), lambda j: (0, j, 0)),
             pl.BlockSpec((None, rows, FN_W), lambda j: (j, 0, 0)),
             pl.BlockSpec((None, rows, MIX), lambda j: (j, 0, 0)),
             pl.BlockSpec((bsz, nc, D), lambda j: (0, 0, j))]
    if pos is not None:
        args.append(pos.reshape(nc, CH * D))
        specs.append(pl.BlockSpec((nc, D), lambda j: (0, j)))
    args += [gate, glu_w.astype(BF16), glu_b.reshape(1, S5_W), w_out[:S5_W].astype(BF16),
             w_out[S5_W:].astype(BF16)]
    specs += [pl.BlockSpec((bsz, 1, D), lambda j: (0, 0, 0)),
              pl.BlockSpec((S5_W, S5_W), lambda j: (0, 0)),
              pl.BlockSpec((1, S5_W), lambda j: (0, 0)),
              pl.BlockSpec((S5_W, D), lambda j: (0, 0)),
              pl.BlockSpec((FN_W, D), lambda j: (0, 0))]
    out = pl.pallas_call(
        functools.partial(_even_out_kernel, has_pos=pos is not None),
        grid=(CH,),
        in_specs=specs,
        out_specs=pl.BlockSpec((bsz, nc, D), lambda j: (0, 0, j)),
        out_shape=jax.ShapeDtypeStruct((bsz, nc, CH * D), F32),
        compiler_params=_cparams(("arbitrary",)),
    )(*args)
    return out.reshape(bsz, t, D)


def _odd_kernel(*refs, tm, t_total, final):
    it = iter(refs)
    xm_ref, xp_ref, xn_ref = next(it), next(it), next(it)
    shift_ref, scale_ref, gate_ref, g_ref = next(it), next(it), next(it), next(it)
    w1_ref, w2_ref, wo_ref, pw_ref, ps_ref, cw_ref = (next(it), next(it), next(it), next(it),
                                                      next(it), next(it))
    fg_ref = next(it) if final else None
    o_ref = next(it)
    pe_ref = next(it)

    ti = pl.program_id(1)
    xm = xm_ref[...]
    xe = jnp.concatenate([xp_ref[...], xm, xn_ref[...]], axis=0)
    he = _mod_norm(xe, g_ref[...], scale_ref[...], shift_ref[...]).astype(BF16)
    pe = jnp.dot(he, w1_ref[...], preferred_element_type=F32)
    te = ti * tm - HALO + lax.broadcasted_iota(jnp.int32, (tm + 2 * HALO, 1), 0)
    valid = jnp.logical_and(te >= 0, te < t_total)
    uc = jnp.where(valid, pe[:, :POOL_W], 0.0)
    v = jnp.where(valid, pe[:, POOL_W:POOL_W + CONV_W] * pe[:, POOL_W + CONV_W:], 0.0)
    pe_ref[:, 0:POOL_W] = uc
    pe_ref[:, POOL_W:POOL_W + CONV_W] = v

    p2 = jnp.dot(he[HALO:HALO + tm], w2_ref[...], preferred_element_type=F32)
    b_gate = p2[:, :CONV_W]
    sz = _silu(p2[:, CONV_W:])

    tpos = ti * tm + lax.broadcasted_iota(jnp.int32, (tm, 1), 0)
    ne = tm + 2 * HALO
    pooled = []
    for gi, w in enumerate(POOL_WINDOWS):
        c0 = gi * POOL_GW
        s = pe_ref[:, c0:c0 + POOL_GW]
        n = ne
        width = 1
        while width < w:
            s = s[0:n - width] + s[width:n]
            n -= width
            width *= 2
        start = HALO - w // 2
        total = s[start:start + tm]
        hi = jnp.minimum(tpos + w // 2, t_total)
        lo = jnp.maximum(tpos - w // 2, 0)
        cnt = (hi - lo).astype(F32)
        centre = pe_ref[HALO:HALO + tm, c0:c0 + POOL_GW]
        pg = total / cnt - centre
        pooled.append(jnp.dot(pg.astype(BF16), pw_ref[gi], preferred_element_type=F32))
    y_c = jnp.concatenate(pooled, axis=1) * ps_ref[...]

    cwt = cw_ref[...]
    vm = pe_ref[HALO - 1:HALO - 1 + tm, POOL_W:POOL_W + CONV_W]
    v0 = pe_ref[HALO:HALO + tm, POOL_W:POOL_W + CONV_W]
    vp = pe_ref[HALO + 1:HALO + 1 + tm, POOL_W:POOL_W + CONV_W]
    y_d = b_gate * (vm * cwt[0:1, :] + v0 * cwt[1:2, :] + vp * cwt[2:3, :])

    y = (jnp.concatenate([y_c, y_d], axis=1) * sz).astype(BF16)
    out = jnp.dot(y, wo_ref[...], preferred_element_type=F32)
    xo = xm + gate_ref[...] * out
    if final:
        ms = jnp.mean(xo * xo, axis=-1, keepdims=True)
        xo = xo * lax.rsqrt(ms + EPS) * fg_ref[...]
    o_ref[...] = xo


def _odd_layer(x, shift, scale, gate, g, w_in, w_out, pool_w, pool_scale, conv_w, final_g):
    bsz, t, _ = x.shape
    tm = min(t, 512)
    nt = t // tm
    hb = tm // HALO
    nhb = t // HALO
    n1 = POOL_W + 2 * CONV_W
    final = final_g is not None
    args = [x, x, x, shift, scale, gate, g.reshape(1, D),
            w_in[:, :n1].astype(BF16), w_in[:, n1:].astype(BF16), w_out.astype(BF16),
            pool_w.astype(BF16), pool_scale.reshape(1, POOL_W), conv_w]
    vec = pl.BlockSpec((None, 1, D), lambda b, i: (b, 0, 0))
    specs = [pl.BlockSpec((None, tm, D), lambda b, i: (b, i, 0)),
             pl.BlockSpec((None, HALO, D), lambda b, i: (b, jnp.maximum(i * hb - 1, 0), 0)),
             pl.BlockSpec((None, HALO, D), lambda b, i: (b, jnp.minimum((i + 1) * hb, nhb - 1), 0)),
             vec, vec, vec,
             pl.BlockSpec((1, D), lambda b, i: (0, 0)),
             pl.BlockSpec((D, n1), lambda b, i: (0, 0)),
             pl.BlockSpec((D, CONV_W + MIX), lambda b, i: (0, 0)),
             pl.BlockSpec((MIX, D), lambda b, i: (0, 0)),
             pl.BlockSpec((len(POOL_WINDOWS), POOL_GW, POOL_GW), lambda b, i: (0, 0, 0)),
             pl.BlockSpec((1, POOL_W), lambda b, i: (0, 0)),
             pl.BlockSpec((3, CONV_W), lambda b, i: (0, 0))]
    if final:
        args.append(final_g.reshape(1, D))
        specs.append(pl.BlockSpec((1, D), lambda b, i: (0, 0)))
    return pl.pallas_call(
        functools.partial(_odd_kernel, tm=tm, t_total=t, final=final),
        grid=(bsz, nt),
        in_specs=specs,
        out_specs=pl.BlockSpec((None, tm, D), lambda b, i: (b, i, 0)),
        out_shape=jax.ShapeDtypeStruct((bsz, t, D), F32),
        scratch_shapes=[pltpu.VMEM((tm + 2 * HALO, POOL_W + CONV_W), F32)],
        compiler_params=_cparams(("arbitrary", "arbitrary")),
    )(*args)


def _sincos_table(n_tok, dim):
    rows = n_tok // GRID_W
    rr, cc = np.meshgrid(np.arange(rows, dtype=np.float32), np.arange(GRID_W, dtype=np.float32),
                         indexing='ij')
    rr = jnp.asarray(rr.reshape(-1, 1))
    cc = jnp.asarray(cc.reshape(-1, 1))
    quarter = dim // 4
    omega = POS_BASE ** (-jnp.arange(quarter, dtype=F32) / quarter)
    return jnp.concatenate([jnp.sin(rr * omega), jnp.cos(rr * omega),
                            jnp.sin(cc * omega), jnp.cos(cc * omega)], axis=-1)


def _block_diag(w):
    g, c, _ = w.shape
    eye = jnp.eye(g, dtype=w.dtype)
    return (eye[:, None, :, None] * w[:, :, None, :]).reshape(g * c, g * c)


def kernel(x, c, ctx, c_ctx, norm_g, ada_w, ada_b, even_w_in, even_w_out, s5_lam_re, s5_lam_im, s5_log_step, s5_b_re, s5_b_im, s5_c_re, s5_c_im, s5_d, s5_glu_w, s5_glu_b, fnet_w, odd_w_in, odd_w_out, pool_w, pool_scale, conv_w, final_g):
    bsz, n_tok, _ = x.shape
    depth = norm_g.shape[0]

    cond = jnp.concatenate([c, jnp.broadcast_to(c_ctx[None], (16 - bsz, D))], axis=0)
    ada = _ada_all(cond, ada_w, ada_b)

    def lat_mod(l, j):
        return ada[l, j, :bsz].reshape(bsz, 1, D)

    def ctx_mod(l, j):
        return jnp.broadcast_to(ada[l, j, bsz].reshape(1, 1, D), (bsz, 1, D))

    mt, bend_t, cp, lam16 = _s5_tables(s5_lam_re, s5_lam_im, s5_log_step, s5_b_re, s5_b_im, s5_c_re,
                                       s5_c_im, s5_d)

    pos = _sincos_table(n_tok, D)
    need_ctx = [any(j % 2 == 0 for j in range(l + 1, depth)) for l in range(depth)]

    xl = x
    xc = ctx
    for l in range(depth):
        i = l // 2
        lpos = pos if l == 0 else None
        last = l == depth - 1
        if l % 2 == 0:
            utl, ubl, zl = _even_in(xl, lpos, lat_mod(l, 0), lat_mod(l, 1), norm_g[l], even_w_in[i],
                                    full=True)
            if need_ctx[l]:
                utc, ubc, zc = _even_in(xc, None, ctx_mod(l, 0), ctx_mod(l, 1), norm_g[l],
                                        even_w_in[i], full=True)
            else:
                (utc,) = _even_in(xc, None, ctx_mod(l, 0), ctx_mod(l, 1), norm_g[l], even_w_in[i],
                                  full=False)
            ytl, ytc = _s5_mix(utl, utc, mt, bend_t, cp, lam16, bsz, i)
            fw_bd = _block_diag(fnet_w[i])
            ybl = _fnet(ubl, fw_bd, bsz)
            xl = _even_out(ytl, ybl, zl, xl, lpos, lat_mod(l, 2), s5_glu_w[i], s5_glu_b[i],
                           even_w_out[i])
            if need_ctx[l]:
                ybc = _fnet(ubc, fw_bd, bsz)
                xc = _even_out(ytc, ybc, zc, xc, None, ctx_mod(l, 2), s5_glu_w[i], s5_glu_b[i],
                               even_w_out[i])
        else:
            if need_ctx[l]:
                xc = _odd_layer(xc, ctx_mod(l, 0), ctx_mod(l, 1), ctx_mod(l, 2), norm_g[l],
                                odd_w_in[i], odd_w_out[i], pool_w[i], pool_scale[i], conv_w[i], None)
            xl = _odd_layer(xl, lat_mod(l, 0), lat_mod(l, 1), lat_mod(l, 2), norm_g[l], odd_w_in[i],
                            odd_w_out[i], pool_w[i], pool_scale[i], conv_w[i],
                            final_g if last else None)
    if depth % 2 == 1:
        raise NotImplementedError("final norm is fused into the last (odd) layer")
    return xl
```

```python
import functools
import math

import numpy as np
import jax
import jax.numpy as jnp
from jax import lax
from jax.experimental import pallas as pl
from jax.experimental.pallas import tpu as pltpu

D = 1024
MIX = 1024
S5_W = 768
FN_W = 256
S5_H = 16
S5_G = 48
S5_P = 64
FN_G = 4
FN_GW = 64
POOL_W = 512
CONV_W = 512
POOL_WINDOWS = (2, 4, 8, 16)
POOL_GW = 128
GRID_W = 64
EPS = 1e-6
POS_BASE = 10000.0
CH = 16
HALO = 8
VMEM_LIMIT = 56 * 1024 * 1024

F32 = jnp.float32
BF16 = jnp.bfloat16
HI = lax.Precision.HIGHEST


def _cparams(sem):
    return pltpu.CompilerParams(dimension_semantics=sem, vmem_limit_bytes=VMEM_LIMIT)


def _silu(v):
    return v * (1.0 / (1.0 + jnp.exp(-v)))


def _sigmoid(v):
    return 1.0 / (1.0 + jnp.exp(-v))


def _gelu_tanh(v):
    c = math.sqrt(2.0 / math.pi)
    return 0.5 * v * (1.0 + jnp.tanh(c * (v + 0.044715 * (v * v * v))))


def _mod_norm(x, g, scale, shift):
    ms = jnp.mean(x * x, axis=-1, keepdims=True)
    y = x * lax.rsqrt(ms + EPS) * g
    return y * (1.0 + scale) + shift


def _ada_kernel(c_ref, w_ref, b_ref, o_ref):
    s = _silu(c_ref[...])
    o_ref[...] = jnp.dot(s, w_ref[...], precision=HI, preferred_element_type=F32) + b_ref[...]


def _ada_all(cond, ada_w, ada_b):
    depth = ada_w.shape[0]
    return pl.pallas_call(
        _ada_kernel,
        grid=(depth, 3),
        in_specs=[
            pl.BlockSpec((16, D), lambda l, j: (0, 0)),
            pl.BlockSpec((None, D, D), lambda l, j: (l, 0, j)),
            pl.BlockSpec((None, None, 1, D), lambda l, j: (l, j, 0, 0)),
        ],
        out_specs=pl.BlockSpec((None, None, 16, D), lambda l, j: (l, j, 0, 0)),
        out_shape=jax.ShapeDtypeStruct((depth, 3, 16, D), F32),
        compiler_params=_cparams(("arbitrary", "arbitrary")),
    )(cond, ada_w, ada_b.reshape(depth, 3, 1, D))


def _split3(v):
    hi = v.astype(BF16)
    lo = (v - hi.astype(F32)).astype(BF16)
    return hi, lo


def _s5_tables_kernel(lr_ref, li_ref, ls_ref, btr_ref, bti_ref, cr_ref, ci_ref, d_ref,
                      mt_ref, bt_ref, cp_ref, l16_ref, ere_ref, eim_ref, *, gb):
    nst = 2 * S5_P
    kk = CH * S5_H
    step = jnp.exp(ls_ref[...])
    lr = lr_ref[...]
    li = li_ref[...]
    a = lr * step
    b = li * step

    def powers(expo):
        mag = jnp.exp(expo * a)
        return mag * jnp.cos(expo * b), mag * jnp.sin(expo * b)

    row = lax.broadcasted_iota(jnp.int32, (1, CH, nst), 1).astype(F32)
    fwd = lax.broadcasted_iota(jnp.int32, (1, CH, nst), 2) < S5_P
    one = jnp.ones((1, 1, nst), F32)

    l1re, l1im = powers(one)
    n_re = l1re - 1.0
    den = lr * lr + li * li
    co_re = (n_re * lr + l1im * li) / den
    co_im = (l1im * lr - n_re * li) / den
    btr = btr_ref[...]
    bti = bti_ref[...]
    bb_re = co_re * btr - co_im * bti
    bb_im = co_re * bti + co_im * btr

    pe_re, pe_im = powers(jnp.where(fwd, (CH - 1) - row, row))
    for l in range(CH):
        pr = pe_re[:, l:l + 1, :]
        pi = pe_im[:, l:l + 1, :]
        ere_ref[:, l * S5_H:(l + 1) * S5_H, :] = pr * bb_re - pi * bb_im
        eim_ref[:, l * S5_H:(l + 1) * S5_H, :] = pr * bb_im + pi * bb_re

    cr = cr_ref[...]
    ci = ci_ref[...]
    pc_re, pc_im = powers(jnp.where(fwd, row + 1.0, CH - row))
    for j in range(CH):
        pr = pc_re[:, j:j + 1, :]
        pi = pc_im[:, j:j + 1, :]
        w_re = cr * pr - ci * pi
        w_im = cr * pi + ci * pr
        cp_ref[:, j * S5_H:(j + 1) * S5_H, :] = jnp.concatenate([w_re, -w_im], axis=2).astype(BF16)

    l16re, l16im = powers(one * float(CH))
    l16_ref[...] = jnp.concatenate([l16re, l16im], axis=1)

    fwd2 = lax.broadcasted_iota(jnp.int32, (S5_H, nst), 1) < S5_P
    lane = lax.broadcasted_iota(jnp.int32, (S5_H, kk), 1)
    iblk = lane // S5_H
    hrow = lax.broadcasted_iota(jnp.int32, (S5_H, kk), 0)
    nt = (((1,), (1,)), ((), ()))
    for g in range(gb):
        ere = ere_ref[g]
        eim = eim_ref[g]
        bt_ref[g, 0:nst, :] = ere.T.astype(BF16)
        bt_ref[g, nst:2 * nst, :] = eim.T.astype(BF16)
        e_hi, e_lo = _split3(jnp.concatenate([ere, eim], axis=1))
        rhs = jnp.concatenate([e_hi, e_hi, e_lo], axis=1)
        crg = cr[g]
        cig = ci[g]

        def lag_kernels(sel):
            c2 = jnp.concatenate([jnp.where(sel, crg, 0.0), jnp.where(sel, -cig, 0.0)], axis=1)
            c_hi, c_lo = _split3(c2)
            lhs = jnp.concatenate([c_hi, c_lo, c_hi], axis=1)
            return lax.dot_general(lhs, rhs, nt, preferred_element_type=F32)

        kf = lag_kernels(fwd2)
        kb = lag_kernels(jnp.logical_not(fwd2))
        dg = d_ref[g]
        for j in range(CH):
            sf = (kk - (CH - 1 - j) * S5_H) % kk
            rf = pltpu.roll(kf, sf, 1) if sf else kf
            rb = pltpu.roll(kb, j * S5_H, 1) if j else kb
            blk = (jnp.where(iblk <= j, rf, 0.0) + jnp.where(iblk >= j, rb, 0.0)
                   + jnp.where(lane == j * S5_H + hrow, dg, 0.0))
            mt_ref[g, j * S5_H:(j + 1) * S5_H, :] = blk.astype(BF16)


def _s5_tables(lam_re, lam_im, log_step, b_re, b_im, c_re, c_im, d_skip):
    n = lam_re.shape[0] * S5_G
    gb = 8
    nst = 2 * S5_P
    kk = CH * S5_H

    def fb(v):
        return jnp.concatenate([v[:, 0], v[:, 1]], axis=-1).reshape(n, v.shape[3], nst)

    lr = fb(lam_re[:, :, :, None, :])
    li = fb(lam_im[:, :, :, None, :])
    ls = fb(jnp.broadcast_to(log_step[:, :, :, None, None], log_step.shape + (1, S5_P)))
    btr = fb(jnp.swapaxes(b_re, -1, -2))
    bti = fb(jnp.swapaxes(b_im, -1, -2))
    cr = fb(c_re)
    ci = fb(c_im)
    d = d_skip.reshape(n, S5_H, 1)

    def spec(r, c):
        return pl.BlockSpec((gb, r, c), lambda g: (g, 0, 0))

    return pl.pallas_call(
        functools.partial(_s5_tables_kernel, gb=gb),
        grid=(n // gb,),
        in_specs=[spec(1, nst), spec(1, nst), spec(1, nst), spec(S5_H, nst), spec(S5_H, nst),
                  spec(S5_H, nst), spec(S5_H, nst), spec(S5_H, 1)],
        out_specs=[spec(kk, kk), spec(2 * nst, kk), spec(kk, 2 * nst), spec(2, nst)],
        out_shape=[jax.ShapeDtypeStruct((n, kk, kk), BF16),
                   jax.ShapeDtypeStruct((n, 2 * nst, kk), BF16),
                   jax.ShapeDtypeStruct((n, kk, 2 * nst), BF16),
                   jax.ShapeDtypeStruct((n, 2, nst), F32)],
        scratch_shapes=[pltpu.VMEM((gb, kk, nst), F32)] * 2,
        compiler_params=_cparams(("arbitrary",)),
    )(lr, li, ls, btr, bti, cr, ci, d)


def _even_in_kernel(*refs, has_pos, full):
    it = iter(refs)
    x_ref = next(it)
    pos_ref = next(it) if has_pos else None
    shift_ref, scale_ref, g_ref, wat_ref = next(it), next(it), next(it), next(it)
    if full:
        wbz_ref, wcs_ref = next(it), next(it)
    ut_ref = next(it)
    if full:
        xw_ref, z_ref = next(it), next(it)
        scr = [next(it) for _ in range(4)]

    x = x_ref[...]
    if has_pos:
        x = x + pos_ref[...][:, None, :]
    h = _mod_norm(x, g_ref[...], scale_ref[...], shift_ref[...])
    nc, nb, _ = x.shape
    rows = nc * nb
    hb = h.reshape(rows, D).astype(BF16)
    pt = lax.dot_general(wat_ref[...], hb, (((1,), (1,)), ((), ())), preferred_element_type=F32)
    ut_ref[...] = pt.astype(BF16).reshape(S5_G, S5_H, rows)
    if full:
        p = jnp.dot(hb, wbz_ref[...], preferred_element_type=F32)
        z_ref[...] = p[:, FN_W:]
        xw = jnp.dot(p[:, :FN_W].astype(BF16), wcs_ref[...], preferred_element_type=F32)
        for q in range(4):
            scr[q][...] = xw[:, q * 128:(q + 1) * 128]
        for q in range(4):
            part, half = divmod(q, 2)
            for bi in range(nb):
                piece = scr[q][pl.ds(bi, nc, stride=nb), :]
                lo = bi * FN_W + half * 128
                xw_ref[part, :, lo:lo + 128] = piece.astype(BF16)


def _even_in(xs, pos, shift, scale, g, w_in, wcs, *, full):
    nc, bsz, _ = xs.shape
    rows = nc * bsz
    wat = w_in[:, :S5_W].T.astype(BF16)
    args = [xs]
    specs = [pl.BlockSpec((nc, bsz, D), lambda i: (0, 0, i))]
    if pos is not None:
        args.append(pos.reshape(nc, CH * D))
        specs.append(pl.BlockSpec((nc, D), lambda i: (0, i)))
    args += [shift, scale, g.reshape(1, D), wat]
    specs += [pl.BlockSpec((1, bsz, D), lambda i: (0, 0, 0)),
              pl.BlockSpec((1, bsz, D), lambda i: (0, 0, 0)),
              pl.BlockSpec((1, D), lambda i: (0, 0)),
              pl.BlockSpec((S5_W, D), lambda i: (0, 0))]
    out_shape = [jax.ShapeDtypeStruct((S5_G, CH * S5_H, rows), BF16)]
    out_specs = [pl.BlockSpec((S5_G, S5_H, rows), lambda i: (0, i, 0))]
    scratch = []
    if full:
        args += [w_in[:, S5_W:].astype(BF16), wcs]
        specs += [pl.BlockSpec((D, FN_W + MIX), lambda i: (0, 0)),
                  pl.BlockSpec((FN_W, 2 * FN_W), lambda i: (0, 0))]
        out_shape += [jax.ShapeDtypeStruct((2, CH, nc, bsz * FN_W), BF16),
                      jax.ShapeDtypeStruct((CH, rows, MIX), F32)]
        out_specs += [pl.BlockSpec((2, None, nc, bsz * FN_W), lambda i: (0, i, 0, 0)),
                      pl.BlockSpec((None, rows, MIX), lambda i: (i, 0, 0))]
        scratch = [pltpu.VMEM((rows, 128), F32)] * 4
    return pl.pallas_call(
        functools.partial(_even_in_kernel, has_pos=pos is not None, full=full),
        grid=(CH,),
        in_specs=specs,
        out_specs=out_specs,
        out_shape=out_shape,
        scratch_shapes=scratch,
        compiler_params=_cparams(("arbitrary",)),
    )(*args)


def _s5_kernel(utl_ref, utc_ref, mt_ref, bt_ref, cp_ref, l16_ref, ytl_ref, ytc_ref,
               sre_ref, sim_ref, are_ref, aim_ref, bre_ref, bim_ref, *, bsz, ncl, ncc, gs):
    nl = bsz * ncl
    nx = bsz * ncc
    nst = 2 * S5_P
    for g in range(gs):
        bt = bt_ref[g]
        sl = jnp.dot(bt, utl_ref[g], preferred_element_type=F32).T
        sc = jnp.dot(bt, utc_ref[g], preferred_element_type=F32).T
        sre_ref[g, 0:nl, :] = sl[:, :nst]
        sim_ref[g, 0:nl, :] = sl[:, nst:]
        sre_ref[g, nl:nl + nx, :] = sc[:, :nst]
        sim_ref[g, nl:nl + nx, :] = sc[:, nst:]

    lam = [l16_ref[g] for g in range(gs)]
    is_fwd = lax.broadcasted_iota(jnp.int32, (bsz, nst), 1) < S5_P

    def make_step(base, nchunk):
        def step(c, carry):
            rf = pl.ds(pl.multiple_of(base + c * bsz, bsz), bsz)
            rb = pl.ds(pl.multiple_of(base + (nchunk - 1 - c) * bsz, bsz), bsz)
            out = []
            for g in range(gs):
                sre, sim = carry[2 * g], carry[2 * g + 1]
                lre = lam[g][0:1, :]
                lim = lam[g][1:2, :]
                are_ref[g, rf, :] = sre
                aim_ref[g, rf, :] = sim
                bre_ref[g, rb, :] = sre
                bim_ref[g, rb, :] = sim
                in_re = jnp.where(is_fwd, sre_ref[g, rf, :], sre_ref[g, rb, :])
                in_im = jnp.where(is_fwd, sim_ref[g, rf, :], sim_ref[g, rb, :])
                out.append(lre * sre - lim * sim + in_re)
                out.append(lre * sim + lim * sre + in_im)
            return tuple(out)
        return step

    zero = jnp.zeros((bsz, nst), F32)
    carry = lax.fori_loop(0, ncc, make_step(nl, ncc), (zero,) * (2 * gs))
    lax.fori_loop(0, ncl, make_step(0, ncl), carry)

    nt = (((1,), (1,)), ((), ()))
    fwd_rows = lax.broadcasted_iota(jnp.int32, (nl + nx, nst), 1) < S5_P
    for g in range(gs):
        mt = mt_ref[g]
        cp = cp_ref[g]
        s0 = jnp.concatenate([jnp.where(fwd_rows, are_ref[g], bre_ref[g]),
                              jnp.where(fwd_rows, aim_ref[g], bim_ref[g])], axis=1).astype(BF16)
        ytl_ref[g] = (jnp.dot(mt, utl_ref[g], preferred_element_type=F32)
                      + lax.dot_general(cp, s0[0:nl], nt, preferred_element_type=F32))
        ytc_ref[g] = (jnp.dot(mt, utc_ref[g], preferred_element_type=F32)
                      + lax.dot_general(cp, s0[nl:nl + nx], nt, preferred_element_type=F32))


def _s5_mix(utl, utc, mt, bend_t, cp, lam16, bsz, layer):
    nl = utl.shape[2]
    nx = utc.shape[2]
    kk = CH * S5_H
    nst = 2 * S5_P
    gs = 4
    off = layer * (S5_G // gs)

    def gspec(r, c):
        return pl.BlockSpec((gs, r, c), lambda g: (g, 0, 0))

    def tspec(r, c):
        return pl.BlockSpec((gs, r, c), lambda g: (g + off, 0, 0))

    return pl.pallas_call(
        functools.partial(_s5_kernel, bsz=bsz, ncl=nl // bsz, ncc=nx // bsz, gs=gs),
        grid=(S5_G // gs,),
        in_specs=[gspec(kk, nl), gspec(kk, nx), tspec(kk, kk), tspec(2 * nst, kk), tspec(kk, 2 * nst),
                  tspec(2, nst)],
        out_specs=[gspec(kk, nl), gspec(kk, nx)],
        out_shape=[jax.ShapeDtypeStruct((S5_G, kk, nl), F32),
                   jax.ShapeDtypeStruct((S5_G, kk, nx), F32)],
        scratch_shapes=[pltpu.VMEM((gs, nl + nx, nst), F32)] * 6,
        compiler_params=_cparams(("arbitrary",)),
    )(utl, utc, mt, bend_t, cp, lam16)


def _fnet_weights_kernel(ccs_ref, fw_ref, o_ref):
    fw = fw_ref[...]
    ccs = ccs_ref[...]
    wc = jnp.dot(ccs[:, :FN_W], fw, precision=HI, preferred_element_type=F32)
    ws = jnp.dot(ccs[:, FN_W:], fw, precision=HI, preferred_element_type=F32)
    o_ref[...] = jnp.concatenate([wc, ws], axis=1).astype(BF16)


def _fnet_weights(fw_bd):
    return pl.pallas_call(
        _fnet_weights_kernel,
        out_shape=jax.ShapeDtypeStruct((FN_W, 2 * FN_W), BF16),
    )(_dft_channel_matrix(), fw_bd)


def _fnet_kernel(cm_ref, v_ref, o_ref):
    v = v_ref[...]
    v2 = v.reshape(v.shape[0] * v.shape[1] * v.shape[2], FN_W)
    y = jnp.dot(cm_ref[...], v2, preferred_element_type=F32)
    o_ref[...] = y.reshape(o_ref.shape)


def _dft_time_matrix(nc):
    t = nc * CH
    order = (np.arange(nc)[None, :] * CH + np.arange(CH)[:, None]).reshape(-1)
    prod = (order[:, None].astype(np.int64) * order[None, :].astype(np.int64)) % t
    ang = prod.astype(np.float64) * (2.0 * np.pi / t)
    scale = 1.0 / math.sqrt(t * FN_GW)
    return jnp.asarray(np.concatenate([np.cos(ang), -np.sin(ang)], axis=1) * scale, dtype=F32).astype(BF16)


def _dft_channel_matrix():
    c = np.arange(FN_GW)
    ang = (c[:, None] * c[None, :] % FN_GW).astype(np.float64) * (2.0 * np.pi / FN_GW)
    eye = np.eye(FN_G)
    return jnp.asarray(np.concatenate([np.kron(eye, np.cos(ang)), np.kron(eye, np.sin(ang))], axis=1),
                       dtype=F32)


def _fnet(xw, bsz):
    nc = xw.shape[2]
    t = nc * CH
    rt = min(t, 512)
    ipt = rt // nc
    cm = _dft_time_matrix(nc)
    return pl.pallas_call(
        _fnet_kernel,
        grid=(t // rt, bsz),
        in_specs=[pl.BlockSpec((rt, 2 * t), lambda r, b: (r, 0)),
                  pl.BlockSpec((2, CH, nc, FN_W), lambda r, b: (0, 0, 0, b))],
        out_specs=pl.BlockSpec((ipt, nc, FN_W), lambda r, b: (r, 0, b)),
        out_shape=jax.ShapeDtypeStruct((CH, nc, bsz * FN_W), F32),
        compiler_params=_cparams(("arbitrary", "arbitrary")),
    )(cm, xw)


def _even_out_kernel(*refs, has_pos):
    it = iter(refs)
    yt_ref, yb_ref, z_ref, x_ref = next(it), next(it), next(it), next(it)
    pos_ref = next(it) if has_pos else None
    gate_ref, gw_ref, gb_ref, woa_ref, wob_ref, o_ref = (next(it), next(it), next(it), next(it),
                                                         next(it), next(it))
    scr = [next(it), next(it)]
    yt = yt_ref[...]
    rows = yt.shape[2]
    ya = _gelu_tanh(yt.reshape(S5_W, rows).T)
    glu = jnp.dot(ya.astype(BF16), gw_ref[...], preferred_element_type=F32) + gb_ref[...]
    ya = ya * _sigmoid(glu)
    x = x_ref[...]
    nc, nb, _ = x.shape
    for half in range(2):
        for bi in range(nb):
            lo = bi * FN_W + half * 128
            scr[half][pl.ds(bi, nc, stride=nb), :] = yb_ref[:, lo:lo + 128]
    yb = jnp.concatenate([scr[0][...], scr[1][...]], axis=1)
    sz = _silu(z_ref[...])
    ma = (ya * sz[:, :S5_W]).astype(BF16)
    mb = (yb * sz[:, S5_W:]).astype(BF16)
    out = (jnp.dot(ma, woa_ref[...], preferred_element_type=F32)
           + jnp.dot(mb, wob_ref[...], preferred_element_type=F32))
    if has_pos:
        x = x + pos_ref[...][:, None, :]
    o_ref[...] = x + gate_ref[...] * out.reshape(x.shape)


def _even_out(yt, yb, z, xs, pos, gate, glu_w, glu_b, w_out):
    nc, bsz, _ = xs.shape
    rows = nc * bsz
    args = [yt, yb, z, xs]
    specs = [pl.BlockSpec((S5_G, S5_H, rows), lambda j: (0, j, 0)),
             pl.BlockSpec((None, nc, bsz * FN_W), lambda j: (j, 0, 0)),
             pl.BlockSpec((None, rows, MIX), lambda j: (j, 0, 0)),
             pl.BlockSpec((nc, bsz, D), lambda j: (0, 0, j))]
    if pos is not None:
        args.append(pos.reshape(nc, CH * D))
        specs.append(pl.BlockSpec((nc, D), lambda j: (0, j)))
    args += [gate, glu_w.astype(BF16), glu_b.reshape(1, S5_W), w_out[:S5_W].astype(BF16),
             w_out[S5_W:].astype(BF16)]
    specs += [pl.BlockSpec((1, bsz, D), lambda j: (0, 0, 0)),
              pl.BlockSpec((S5_W, S5_W), lambda j: (0, 0)),
              pl.BlockSpec((1, S5_W), lambda j: (0, 0)),
              pl.BlockSpec((S5_W, D), lambda j: (0, 0)),
              pl.BlockSpec((FN_W, D), lambda j: (0, 0))]
    return pl.pallas_call(
        functools.partial(_even_out_kernel, has_pos=pos is not None),
        grid=(CH,),
        in_specs=specs,
        out_specs=pl.BlockSpec((nc, bsz, D), lambda j: (0, 0, j)),
        out_shape=jax.ShapeDtypeStruct((nc, bsz, CH * D), F32),
        scratch_shapes=[pltpu.VMEM((rows, 128), F32)] * 2,
        compiler_params=_cparams(("arbitrary",)),
    )(*args)


def _odd_kernel(*refs, ct, t_total, final):
    it = iter(refs)
    xm_ref, xp_ref, xn_ref = next(it), next(it), next(it)
    shift_ref, scale_ref, gate_ref, g_ref = next(it), next(it), next(it), next(it)
    w1_ref, w2_ref, wo_ref, pw_ref, ps_ref, cw_ref = (next(it), next(it), next(it), next(it),
                                                      next(it), next(it))
    fg_ref = next(it) if final else None
    o_ref = next(it)
    h_ref = next(it)
    pe_ref = next(it)

    ti = pl.program_id(0)
    nb = xm_ref.shape[1]
    tm = ct * CH
    ne = tm + 2 * HALO

    def hn(xv):
        return _mod_norm(xv, g_ref[...], scale_ref[...], shift_ref[...])

    for i in range(CH):
        hi = hn(xm_ref[:, :, i * D:(i + 1) * D])
        if i < HALO:
            h_ref[0:ct, i + HALO] = hi
        else:
            h_ref[1:ct + 1, i - HALO] = hi
    for i in range(HALO):
        h_ref[0, i] = hn(xp_ref[:, :, (i + HALO) * D:(i + HALO + 1) * D])[0]
        h_ref[ct, i + HALO] = hn(xn_ref[:, :, i * D:(i + 1) * D])[0]

    he = h_ref[...].reshape(ne * nb, D).astype(BF16)
    pe = jnp.dot(he, w1_ref[...], preferred_element_type=F32)
    te = ti * tm - HALO + lax.broadcasted_iota(jnp.int32, (ne * nb, 1), 0) // nb
    valid = jnp.logical_and(te >= 0, te < t_total)
    pe_ref[:, 0:POOL_W] = jnp.where(valid, pe[:, :POOL_W], 0.0)
    pe_ref[:, POOL_W:POOL_W + CONV_W] = jnp.where(
        valid, pe[:, POOL_W:POOL_W + CONV_W] * pe[:, POOL_W + CONV_W:], 0.0)

    m0 = HALO * nb
    mr = tm * nb
    p2 = jnp.dot(he[m0:m0 + mr], w2_ref[...], preferred_element_type=F32)
    b_gate = p2[:, :CONV_W]
    sz = _silu(p2[:, CONV_W:])

    tpos = ti * tm + lax.broadcasted_iota(jnp.int32, (mr, 1), 0) // nb
    pooled = []
    for gi, w in enumerate(POOL_WINDOWS):
        c0 = gi * POOL_GW
        s = pe_ref[:, c0:c0 + POOL_GW]
        n = ne
        width = 1
        while width < w:
            s = s[0:(n - width) * nb] + s[width * nb:n * nb]
            n -= width
            width *= 2
        start = (HALO - w // 2) * nb
        total = s[start:start + mr]
        hi = jnp.minimum(tpos + w // 2, t_total)
        lo = jnp.maximum(tpos - w // 2, 0)
        cnt = (hi - lo).astype(F32)
        centre = pe_ref[m0:m0 + mr, c0:c0 + POOL_GW]
        pg = total / cnt - centre
        pooled.append(jnp.dot(pg.astype(BF16), pw_ref[gi], preferred_element_type=F32))
    y_c = jnp.concatenate(pooled, axis=1) * ps_ref[...]

    cwt = cw_ref[...]
    vm = pe_ref[m0 - nb:m0 - nb + mr, POOL_W:POOL_W + CONV_W]
    v0 = pe_ref[m0:m0 + mr, POOL_W:POOL_W + CONV_W]
    vp = pe_ref[m0 + nb:m0 + nb + mr, POOL_W:POOL_W + CONV_W]
    y_d = b_gate * (vm * cwt[0:1, :] + v0 * cwt[1:2, :] + vp * cwt[2:3, :])

    y = (jnp.concatenate([y_c, y_d], axis=1) * sz).astype(BF16)
    out = jnp.dot(y, wo_ref[...], preferred_element_type=F32)
    go = (gate_ref[...] * out.reshape(tm, nb, D)).reshape(ct, CH, nb, D)
    for i in range(CH):
        xo = xm_ref[:, :, i * D:(i + 1) * D] + go[:, i]
        if final:
            ms = jnp.mean(xo * xo, axis=-1, keepdims=True)
            xo = xo * lax.rsqrt(ms + EPS) * fg_ref[...]
        o_ref[:, :, i * D:(i + 1) * D] = xo


def _odd_layer(xs, shift, scale, gate, g, w_in, w_out, pool_w, pool_scale, conv_w, final_g):
    nc, bsz, _ = xs.shape
    ct = 8
    n1 = POOL_W + 2 * CONV_W
    final = final_g is not None
    args = [xs, xs, xs, shift, scale, gate, g.reshape(1, D),
            w_in[:, :n1].astype(BF16), w_in[:, n1:].astype(BF16), w_out.astype(BF16),
            pool_w.astype(BF16), pool_scale.reshape(1, POOL_W), conv_w]
    vec = pl.BlockSpec((1, bsz, D), lambda i: (0, 0, 0))

    def const(*shape):
        return pl.BlockSpec(shape, lambda i: (0,) * len(shape), pipeline_mode=pl.Buffered(1))

    specs = [pl.BlockSpec((ct, bsz, CH * D), lambda i: (i, 0, 0)),
             pl.BlockSpec((1, bsz, CH * D), lambda i: (jnp.maximum(i * ct - 1, 0), 0, 0)),
             pl.BlockSpec((1, bsz, CH * D), lambda i: (jnp.minimum((i + 1) * ct, nc - 1), 0, 0)),
             vec, vec, vec,
             const(1, D), const(D, n1), const(D, CONV_W + MIX), const(MIX, D),
             const(len(POOL_WINDOWS), POOL_GW, POOL_GW), const(1, POOL_W), const(3, CONV_W)]
    if final:
        args.append(final_g.reshape(1, D))
        specs.append(const(1, D))
    return pl.pallas_call(
        functools.partial(_odd_kernel, ct=ct, t_total=nc * CH, final=final),
        grid=(nc // ct,),
        in_specs=specs,
        out_specs=pl.BlockSpec((ct, bsz, CH * D), lambda i: (i, 0, 0)),
        out_shape=jax.ShapeDtypeStruct((nc, bsz, CH * D), F32),
        scratch_shapes=[pltpu.VMEM((ct + 1, CH, bsz, D), F32),
                        pltpu.VMEM(((ct + 1) * CH * bsz, POOL_W + CONV_W), F32)],
        compiler_params=_cparams(("arbitrary",)),
    )(*args)


def _sincos_table(n_tok, dim):
    rows = n_tok // GRID_W
    rr, cc = np.meshgrid(np.arange(rows, dtype=np.float32), np.arange(GRID_W, dtype=np.float32),
                         indexing='ij')
    rr = jnp.asarray(rr.reshape(-1, 1))
    cc = jnp.asarray(cc.reshape(-1, 1))
    quarter = dim // 4
    omega = POS_BASE ** (-jnp.arange(quarter, dtype=F32) / quarter)
    return jnp.concatenate([jnp.sin(rr * omega), jnp.cos(rr * omega),
                            jnp.sin(cc * omega), jnp.cos(cc * omega)], axis=-1)


def _block_diag(w):
    g, c, _ = w.shape
    eye = jnp.eye(g, dtype=w.dtype)
    return (eye[:, None, :, None] * w[:, :, None, :]).reshape(g * c, g * c)


def _to_stream(x):
    bsz, t, d = x.shape
    return x.reshape(bsz, t // CH, CH * d).transpose(1, 0, 2)


def _from_stream(xs):
    nc, bsz, _ = xs.shape
    return xs.transpose(1, 0, 2).reshape(bsz, nc * CH, D)


def kernel(x, c, ctx, c_ctx, norm_g, ada_w, ada_b, even_w_in, even_w_out, s5_lam_re, s5_lam_im, s5_log_step, s5_b_re, s5_b_im, s5_c_re, s5_c_im, s5_d, s5_glu_w, s5_glu_b, fnet_w, odd_w_in, odd_w_out, pool_w, pool_scale, conv_w, final_g):
    bsz, n_tok, _ = x.shape
    depth = norm_g.shape[0]

    cond = jnp.concatenate([c, jnp.broadcast_to(c_ctx[None], (16 - bsz, D))], axis=0)
    ada = _ada_all(cond, ada_w, ada_b)

    def lat_mod(l, j):
        return ada[l, j, :bsz].reshape(1, bsz, D)

    def ctx_mod(l, j):
        return jnp.broadcast_to(ada[l, j, bsz].reshape(1, 1, D), (1, bsz, D))

    mt, bend_t, cp, lam16 = _s5_tables(s5_lam_re, s5_lam_im, s5_log_step, s5_b_re, s5_b_im, s5_c_re,
                                       s5_c_im, s5_d)

    pos = _sincos_table(n_tok, D)
    need_ctx = [any(j % 2 == 0 for j in range(l + 1, depth)) for l in range(depth)]

    xl = _to_stream(x)
    xc = _to_stream(ctx)
    for l in range(depth):
        i = l // 2
        lpos = pos if l == 0 else None
        last = l == depth - 1
        if l % 2 == 0:
            wcs = _fnet_weights(_block_diag(fnet_w[i]))
            utl, xwl, zl = _even_in(xl, lpos, lat_mod(l, 0), lat_mod(l, 1), norm_g[l], even_w_in[i],
                                    wcs, full=True)
            if need_ctx[l]:
                utc, xwc, zc = _even_in(xc, None, ctx_mod(l, 0), ctx_mod(l, 1), norm_g[l],
                                        even_w_in[i], wcs, full=True)
            else:
                (utc,) = _even_in(xc, None, ctx_mod(l, 0), ctx_mod(l, 1), norm_g[l], even_w_in[i],
                                  None, full=False)
            ytl, ytc = _s5_mix(utl, utc, mt, bend_t, cp, lam16, bsz, i)
            ybl = _fnet(xwl, bsz)
            xl = _even_out(ytl, ybl, zl, xl, lpos, lat_mod(l, 2), s5_glu_w[i], s5_glu_b[i],
                           even_w_out[i])
            if need_ctx[l]:
                ybc = _fnet(xwc, bsz)
                xc = _even_out(ytc, ybc, zc, xc, None, ctx_mod(l, 2), s5_glu_w[i], s5_glu_b[i],
                               even_w_out[i])
        else:
            if need_ctx[l]:
                xc = _odd_layer(xc, ctx_mod(l, 0), ctx_mod(l, 1), ctx_mod(l, 2), norm_g[l],
                                odd_w_in[i], odd_w_out[i], pool_w[i], pool_scale[i], conv_w[i], None)
            xl = _odd_layer(xl, lat_mod(l, 0), lat_mod(l, 1), lat_mod(l, 2), norm_g[l], odd_w_in[i],
                            odd_w_out[i], pool_w[i], pool_scale[i], conv_w[i],
                            final_g if last else None)
    if depth % 2 == 1:
        raise NotImplementedError("final norm is fused into the last (odd) layer")
    return _from_stream(xl)
```

```python
import functools
import math

import numpy as np
import jax
import jax.numpy as jnp
from jax import lax
from jax.experimental import pallas as pl
from jax.experimental.pallas import tpu as pltpu

D = 1024
MIX = 1024
S5_W = 768
FN_W = 256
S5_H = 16
S5_G = 48
S5_P = 64
FN_G = 4
FN_GW = 64
POOL_W = 512
CONV_W = 512
POOL_WINDOWS = (2, 4, 8, 16)
POOL_GW = 128
GRID_W = 64
EPS = 1e-6
POS_BASE = 10000.0
CH = 16
HALO = 8
VMEM_LIMIT = 56 * 1024 * 1024

F32 = jnp.float32
BF16 = jnp.bfloat16
HI = lax.Precision.HIGHEST


def _cparams(sem):
    return pltpu.CompilerParams(dimension_semantics=sem, vmem_limit_bytes=VMEM_LIMIT)


def _sigmoid(v):
    return 0.5 * jnp.tanh(0.5 * v) + 0.5


def _silu(v):
    return v * _sigmoid(v)


def _gelu_tanh(v):
    c = math.sqrt(2.0 / math.pi)
    return 0.5 * v * (1.0 + jnp.tanh(c * (v + 0.044715 * (v * v * v))))


def _mod_norm(x, g, scale, shift):
    ms = jnp.mean(x * x, axis=-1, keepdims=True)
    y = x * lax.rsqrt(ms + EPS) * g
    return y * (1.0 + scale) + shift


def _ada_kernel(c_ref, w_ref, b_ref, o_ref):
    s = _silu(c_ref[...])
    o_ref[...] = jnp.dot(s, w_ref[...], precision=HI, preferred_element_type=F32) + b_ref[...]


def _ada_all(cond, ada_w, ada_b):
    depth = ada_w.shape[0]
    return pl.pallas_call(
        _ada_kernel,
        grid=(depth, 3),
        in_specs=[
            pl.BlockSpec((16, D), lambda l, j: (0, 0)),
            pl.BlockSpec((None, D, D), lambda l, j: (l, 0, j)),
            pl.BlockSpec((None, None, 1, D), lambda l, j: (l, j, 0, 0)),
        ],
        out_specs=pl.BlockSpec((None, None, 16, D), lambda l, j: (l, j, 0, 0)),
        out_shape=jax.ShapeDtypeStruct((depth, 3, 16, D), F32),
        compiler_params=_cparams(("arbitrary", "arbitrary")),
    )(cond, ada_w, ada_b.reshape(depth, 3, 1, D))


def _split3(v):
    hi = v.astype(BF16)
    lo = (v - hi.astype(F32)).astype(BF16)
    return hi, lo


def _s5_tables_kernel(lr_ref, li_ref, ls_ref, btr_ref, bti_ref, cr_ref, ci_ref, d_ref,
                      mt_ref, bt_ref, cp_ref, l16_ref, ere_ref, eim_ref, *, gb):
    nst = 2 * S5_P
    kk = CH * S5_H
    step = jnp.exp(ls_ref[...])
    lr = lr_ref[...]
    li = li_ref[...]
    a = lr * step
    b = li * step

    def powers(expo):
        mag = jnp.exp(expo * a)
        return mag * jnp.cos(expo * b), mag * jnp.sin(expo * b)

    row = lax.broadcasted_iota(jnp.int32, (1, CH, nst), 1).astype(F32)
    fwd = lax.broadcasted_iota(jnp.int32, (1, CH, nst), 2) < S5_P
    one = jnp.ones((1, 1, nst), F32)

    l1re, l1im = powers(one)
    n_re = l1re - 1.0
    den = lr * lr + li * li
    co_re = (n_re * lr + l1im * li) / den
    co_im = (l1im * lr - n_re * li) / den
    btr = btr_ref[...]
    bti = bti_ref[...]
    bb_re = co_re * btr - co_im * bti
    bb_im = co_re * bti + co_im * btr

    pe_re, pe_im = powers(jnp.where(fwd, (CH - 1) - row, row))
    for l in range(CH):
        pr = pe_re[:, l:l + 1, :]
        pi = pe_im[:, l:l + 1, :]
        ere_ref[:, l * S5_H:(l + 1) * S5_H, :] = pr * bb_re - pi * bb_im
        eim_ref[:, l * S5_H:(l + 1) * S5_H, :] = pr * bb_im + pi * bb_re

    cr = cr_ref[...]
    ci = ci_ref[...]
    pc_re, pc_im = powers(jnp.where(fwd, row + 1.0, CH - row))
    for j in range(CH):
        pr = pc_re[:, j:j + 1, :]
        pi = pc_im[:, j:j + 1, :]
        w_re = cr * pr - ci * pi
        w_im = cr * pi + ci * pr
        cp_ref[:, j * S5_H:(j + 1) * S5_H, :] = jnp.concatenate([w_re, -w_im], axis=2).astype(BF16)

    l16re, l16im = powers(one * float(CH))
    l16_ref[...] = jnp.concatenate([l16re, l16im], axis=1)

    fwd2 = lax.broadcasted_iota(jnp.int32, (S5_H, nst), 1) < S5_P
    lane = lax.broadcasted_iota(jnp.int32, (S5_H, kk), 1)
    iblk = lane // S5_H
    hrow = lax.broadcasted_iota(jnp.int32, (S5_H, kk), 0)
    nt = (((1,), (1,)), ((), ()))
    for g in range(gb):
        ere = ere_ref[g]
        eim = eim_ref[g]
        bt_ref[g, 0:nst, :] = ere.T.astype(BF16)
        bt_ref[g, nst:2 * nst, :] = eim.T.astype(BF16)
        e_hi, e_lo = _split3(jnp.concatenate([ere, eim], axis=1))
        rhs = jnp.concatenate([e_hi, e_hi, e_lo], axis=1)
        crg = cr[g]
        cig = ci[g]

        def lag_kernels(sel):
            c2 = jnp.concatenate([jnp.where(sel, crg, 0.0), jnp.where(sel, -cig, 0.0)], axis=1)
            c_hi, c_lo = _split3(c2)
            lhs = jnp.concatenate([c_hi, c_lo, c_hi], axis=1)
            return lax.dot_general(lhs, rhs, nt, preferred_element_type=F32)

        kf = lag_kernels(fwd2)
        kb = lag_kernels(jnp.logical_not(fwd2))
        dg = d_ref[g]
        for j in range(CH):
            sf = (kk - (CH - 1 - j) * S5_H) % kk
            rf = pltpu.roll(kf, sf, 1) if sf else kf
            rb = pltpu.roll(kb, j * S5_H, 1) if j else kb
            blk = (jnp.where(iblk <= j, rf, 0.0) + jnp.where(iblk >= j, rb, 0.0)
                   + jnp.where(lane == j * S5_H + hrow, dg, 0.0))
            mt_ref[g, j * S5_H:(j + 1) * S5_H, :] = blk.astype(BF16)


def _s5_tables(lam_re, lam_im, log_step, b_re, b_im, c_re, c_im, d_skip):
    n = lam_re.shape[0] * S5_G
    gb = 8
    nst = 2 * S5_P
    kk = CH * S5_H

    def fb(v):
        return jnp.concatenate([v[:, 0], v[:, 1]], axis=-1).reshape(n, v.shape[3], nst)

    lr = fb(lam_re[:, :, :, None, :])
    li = fb(lam_im[:, :, :, None, :])
    ls = fb(jnp.broadcast_to(log_step[:, :, :, None, None], log_step.shape + (1, S5_P)))
    btr = fb(jnp.swapaxes(b_re, -1, -2))
    bti = fb(jnp.swapaxes(b_im, -1, -2))
    cr = fb(c_re)
    ci = fb(c_im)
    d = d_skip.reshape(n, S5_H, 1)

    def spec(r, c):
        return pl.BlockSpec((gb, r, c), lambda g: (g, 0, 0))

    return pl.pallas_call(
        functools.partial(_s5_tables_kernel, gb=gb),
        grid=(n // gb,),
        in_specs=[spec(1, nst), spec(1, nst), spec(1, nst), spec(S5_H, nst), spec(S5_H, nst),
                  spec(S5_H, nst), spec(S5_H, nst), spec(S5_H, 1)],
        out_specs=[spec(kk, kk), spec(2 * nst, kk), spec(kk, 2 * nst), spec(2, nst)],
        out_shape=[jax.ShapeDtypeStruct((n, kk, kk), BF16),
                   jax.ShapeDtypeStruct((n, 2 * nst, kk), BF16),
                   jax.ShapeDtypeStruct((n, kk, 2 * nst), BF16),
                   jax.ShapeDtypeStruct((n, 2, nst), F32)],
        scratch_shapes=[pltpu.VMEM((gb, kk, nst), F32)] * 2,
        compiler_params=_cparams(("arbitrary",)),
    )(lr, li, ls, btr, bti, cr, ci, d)


def _even_in_kernel(*refs, full):
    it = iter(refs)
    x_ref = next(it)
    shift_ref, scale_ref, g_ref, wat_ref = next(it), next(it), next(it), next(it)
    if full:
        wbz_ref, wcs_ref = next(it), next(it)
    ut_ref = next(it)
    if full:
        xw_ref, z_ref = next(it), next(it)
        scr = [next(it) for _ in range(4)]

    x = x_ref[...]
    h = _mod_norm(x, g_ref[...], scale_ref[...], shift_ref[...])
    nc, nb, _ = x.shape
    rows = nc * nb
    hb = h.reshape(rows, D).astype(BF16)
    pt = lax.dot_general(wat_ref[...], hb, (((1,), (1,)), ((), ())), preferred_element_type=F32)
    ut_ref[...] = pt.astype(BF16).reshape(S5_G, S5_H, rows)
    if full:
        p = jnp.dot(hb, wbz_ref[...], preferred_element_type=F32)
        z_ref[...] = p[:, FN_W:]
        xw = jnp.dot(p[:, :FN_W].astype(BF16), wcs_ref[...], preferred_element_type=F32)
        for q in range(4):
            scr[q][...] = xw[:, q * 128:(q + 1) * 128]
        for q in range(4):
            part, half = divmod(q, 2)
            for bi in range(nb):
                piece = scr[q][pl.ds(bi, nc, stride=nb), :]
                lo = bi * FN_W + half * 128
                xw_ref[part, :, lo:lo + 128] = piece.astype(BF16)


def _even_in(xs, shift, scale, g, w_in, wcs, *, full):
    nc, bsz, _ = xs.shape
    rows = nc * bsz
    wat = w_in[:, :S5_W].T.astype(BF16)
    args = [xs]
    specs = [pl.BlockSpec((nc, bsz, D), lambda i: (0, 0, i))]
    args += [shift, scale, g.reshape(1, D), wat]
    specs += [pl.BlockSpec((1, bsz, D), lambda i: (0, 0, 0)),
              pl.BlockSpec((1, bsz, D), lambda i: (0, 0, 0)),
              pl.BlockSpec((1, D), lambda i: (0, 0)),
              pl.BlockSpec((S5_W, D), lambda i: (0, 0))]
    out_shape = [jax.ShapeDtypeStruct((S5_G, CH * S5_H, rows), BF16)]
    out_specs = [pl.BlockSpec((S5_G, S5_H, rows), lambda i: (0, i, 0))]
    scratch = []
    if full:
        args += [w_in[:, S5_W:].astype(BF16), wcs]
        specs += [pl.BlockSpec((D, FN_W + MIX), lambda i: (0, 0)),
                  pl.BlockSpec((FN_W, 2 * FN_W), lambda i: (0, 0))]
        out_shape += [jax.ShapeDtypeStruct((2, CH, nc, bsz * FN_W), BF16),
                      jax.ShapeDtypeStruct((CH, rows, MIX), F32)]
        out_specs += [pl.BlockSpec((2, None, nc, bsz * FN_W), lambda i: (0, i, 0, 0)),
                      pl.BlockSpec((None, rows, MIX), lambda i: (i, 0, 0))]
        scratch = [pltpu.VMEM((rows, 128), F32)] * 4
    return pl.pallas_call(
        functools.partial(_even_in_kernel, full=full),
        grid=(CH,),
        in_specs=specs,
        out_specs=out_specs,
        out_shape=out_shape,
        scratch_shapes=scratch,
        compiler_params=_cparams(("arbitrary",)),
    )(*args)


def _s5_kernel(utl_ref, utc_ref, mt_ref, bt_ref, cp_ref, l16_ref, ytl_ref, ytc_ref,
               sre_ref, sim_ref, are_ref, aim_ref, bre_ref, bim_ref, *, bsz, ncl, ncc, gs):
    nl = bsz * ncl
    nx = bsz * ncc
    nst = 2 * S5_P
    for g in range(gs):
        bt = bt_ref[g]
        sl = jnp.dot(bt, utl_ref[g], preferred_element_type=F32).T
        sc = jnp.dot(bt, utc_ref[g], preferred_element_type=F32).T
        sre_ref[g, 0:nl, :] = sl[:, :nst]
        sim_ref[g, 0:nl, :] = sl[:, nst:]
        sre_ref[g, nl:nl + nx, :] = sc[:, :nst]
        sim_ref[g, nl:nl + nx, :] = sc[:, nst:]

    lam = [l16_ref[g] for g in range(gs)]
    is_fwd = lax.broadcasted_iota(jnp.int32, (bsz, nst), 1) < S5_P

    def make_step(base, nchunk):
        def step(c, carry):
            rf = pl.ds(pl.multiple_of(base + c * bsz, bsz), bsz)
            rb = pl.ds(pl.multiple_of(base + (nchunk - 1 - c) * bsz, bsz), bsz)
            out = []
            for g in range(gs):
                sre, sim = carry[2 * g], carry[2 * g + 1]
                lre = lam[g][0:1, :]
                lim = lam[g][1:2, :]
                are_ref[g, rf, :] = sre
                aim_ref[g, rf, :] = sim
                bre_ref[g, rb, :] = sre
                bim_ref[g, rb, :] = sim
                in_re = jnp.where(is_fwd, sre_ref[g, rf, :], sre_ref[g, rb, :])
                in_im = jnp.where(is_fwd, sim_ref[g, rf, :], sim_ref[g, rb, :])
                out.append(lre * sre - lim * sim + in_re)
                out.append(lre * sim + lim * sre + in_im)
            return tuple(out)
        return step

    zero = jnp.zeros((bsz, nst), F32)
    carry = lax.fori_loop(0, ncc, make_step(nl, ncc), (zero,) * (2 * gs))
    lax.fori_loop(0, ncl, make_step(0, ncl), carry)

    nt = (((1,), (1,)), ((), ()))
    fwd_rows = lax.broadcasted_iota(jnp.int32, (nl + nx, nst), 1) < S5_P
    for g in range(gs):
        mt = mt_ref[g]
        cp = cp_ref[g]
        s0 = jnp.concatenate([jnp.where(fwd_rows, are_ref[g], bre_ref[g]),
                              jnp.where(fwd_rows, aim_ref[g], bim_ref[g])], axis=1).astype(BF16)
        ytl_ref[g] = (jnp.dot(mt, utl_ref[g], preferred_element_type=F32)
                      + lax.dot_general(cp, s0[0:nl], nt, preferred_element_type=F32))
        ytc_ref[g] = (jnp.dot(mt, utc_ref[g], preferred_element_type=F32)
                      + lax.dot_general(cp, s0[nl:nl + nx], nt, preferred_element_type=F32))


def _s5_mix(utl, utc, mt, bend_t, cp, lam16, bsz, layer):
    nl = utl.shape[2]
    nx = utc.shape[2]
    kk = CH * S5_H
    nst = 2 * S5_P
    gs = 4
    off = layer * (S5_G // gs)

    def gspec(r, c):
        return pl.BlockSpec((gs, r, c), lambda g: (g, 0, 0))

    def tspec(r, c):
        return pl.BlockSpec((gs, r, c), lambda g: (g + off, 0, 0))

    return pl.pallas_call(
        functools.partial(_s5_kernel, bsz=bsz, ncl=nl // bsz, ncc=nx // bsz, gs=gs),
        grid=(S5_G // gs,),
        in_specs=[gspec(kk, nl), gspec(kk, nx), tspec(kk, kk), tspec(2 * nst, kk), tspec(kk, 2 * nst),
                  tspec(2, nst)],
        out_specs=[gspec(kk, nl), gspec(kk, nx)],
        out_shape=[jax.ShapeDtypeStruct((S5_G, kk, nl), F32),
                   jax.ShapeDtypeStruct((S5_G, kk, nx), F32)],
        scratch_shapes=[pltpu.VMEM((gs, nl + nx, nst), F32)] * 6,
        compiler_params=_cparams(("arbitrary",)),
    )(utl, utc, mt, bend_t, cp, lam16)


def _fnet_weights_kernel(ccs_ref, fw_ref, o_ref):
    fw = fw_ref[...]
    ccs = ccs_ref[...]
    wc = jnp.dot(ccs[:, :FN_W], fw, precision=HI, preferred_element_type=F32)
    ws = jnp.dot(ccs[:, FN_W:], fw, precision=HI, preferred_element_type=F32)
    o_ref[...] = jnp.concatenate([wc, ws], axis=1).astype(BF16)


def _fnet_weights(fw_bd):
    return pl.pallas_call(
        _fnet_weights_kernel,
        out_shape=jax.ShapeDtypeStruct((FN_W, 2 * FN_W), BF16),
    )(_dft_channel_matrix(), fw_bd)


def _fnet_kernel(cm_ref, v_ref, o_ref, cmb_ref):
    @pl.when(pl.program_id(1) == 0)
    def _():
        cmb_ref[...] = cm_ref[...].astype(BF16)

    v = v_ref[...]
    v2 = v.reshape(v.shape[0] * v.shape[1] * v.shape[2], FN_W)
    y = jnp.dot(cmb_ref[...], v2, preferred_element_type=F32)
    o_ref[...] = y.reshape(o_ref.shape)


def _dft_time_matrix(nc):
    t = nc * CH
    order = (np.arange(nc)[None, :] * CH + np.arange(CH)[:, None]).reshape(-1)
    prod = (order[:, None].astype(np.int64) * order[None, :].astype(np.int64)) % t
    ang = prod.astype(np.float64) * (2.0 * np.pi / t)
    scale = 1.0 / math.sqrt(t * FN_GW)
    return jnp.asarray(np.concatenate([np.cos(ang), -np.sin(ang)], axis=1) * scale, dtype=F32)


def _dft_channel_matrix():
    c = np.arange(FN_GW)
    ang = (c[:, None] * c[None, :] % FN_GW).astype(np.float64) * (2.0 * np.pi / FN_GW)
    eye = np.eye(FN_G)
    return jnp.asarray(np.concatenate([np.kron(eye, np.cos(ang)), np.kron(eye, np.sin(ang))], axis=1),
                       dtype=F32)


def _fnet(xw, bsz):
    nc = xw.shape[2]
    t = nc * CH
    rt = min(t, 512)
    ipt = rt // nc
    cm = _dft_time_matrix(nc)
    return pl.pallas_call(
        _fnet_kernel,
        grid=(t // rt, bsz),
        in_specs=[pl.BlockSpec((rt, 2 * t), lambda r, b: (r, 0)),
                  pl.BlockSpec((2, CH, nc, FN_W), lambda r, b: (0, 0, 0, b))],
        out_specs=pl.BlockSpec((ipt, nc, FN_W), lambda r, b: (r, 0, b)),
        out_shape=jax.ShapeDtypeStruct((CH, nc, bsz * FN_W), F32),
        scratch_shapes=[pltpu.VMEM((rt, 2 * t), BF16)],
        compiler_params=_cparams(("arbitrary", "arbitrary")),
    )(cm, xw)


def _even_out_kernel(yt_ref, yb_ref, z_ref, x_ref, gate_ref, gw_ref, gb_ref, woa_ref, wob_ref, o_ref,
                     scr0_ref, scr1_ref):
    scr = [scr0_ref, scr1_ref]
    yt = yt_ref[...]
    rows = yt.shape[2]
    ya = _gelu_tanh(yt.reshape(S5_W, rows).T)
    glu = jnp.dot(ya.astype(BF16), gw_ref[...], preferred_element_type=F32) + gb_ref[...]
    ya = ya * _sigmoid(glu)
    x = x_ref[...]
    nc, nb, _ = x.shape
    for half in range(2):
        for bi in range(nb):
            lo = bi * FN_W + half * 128
            scr[half][pl.ds(bi, nc, stride=nb), :] = yb_ref[:, lo:lo + 128]
    yb = jnp.concatenate([scr[0][...], scr[1][...]], axis=1)
    sz = _silu(z_ref[...])
    ma = (ya * sz[:, :S5_W]).astype(BF16)
    mb = (yb * sz[:, S5_W:]).astype(BF16)
    out = (jnp.dot(ma, woa_ref[...], preferred_element_type=F32)
           + jnp.dot(mb, wob_ref[...], preferred_element_type=F32))
    o_ref[...] = x + gate_ref[...] * out.reshape(x.shape)


def _even_out(yt, yb, z, xs, gate, glu_w, glu_b, w_out):
    nc, bsz, _ = xs.shape
    rows = nc * bsz
    args = [yt, yb, z, xs]
    specs = [pl.BlockSpec((S5_G, S5_H, rows), lambda j: (0, j, 0)),
             pl.BlockSpec((None, nc, bsz * FN_W), lambda j: (j, 0, 0)),
             pl.BlockSpec((None, rows, MIX), lambda j: (j, 0, 0)),
             pl.BlockSpec((nc, bsz, D), lambda j: (0, 0, j))]
    args += [gate, glu_w.astype(BF16), glu_b.reshape(1, S5_W), w_out[:S5_W].astype(BF16),
             w_out[S5_W:].astype(BF16)]
    specs += [pl.BlockSpec((1, bsz, D), lambda j: (0, 0, 0)),
              pl.BlockSpec((S5_W, S5_W), lambda j: (0, 0)),
              pl.BlockSpec((1, S5_W), lambda j: (0, 0)),
              pl.BlockSpec((S5_W, D), lambda j: (0, 0)),
              pl.BlockSpec((FN_W, D), lambda j: (0, 0))]
    return pl.pallas_call(
        _even_out_kernel,
        grid=(CH,),
        in_specs=specs,
        out_specs=pl.BlockSpec((nc, bsz, D), lambda j: (0, 0, j)),
        out_shape=jax.ShapeDtypeStruct((nc, bsz, CH * D), F32),
        scratch_shapes=[pltpu.VMEM((rows, 128), F32)] * 2,
        compiler_params=_cparams(("arbitrary",)),
    )(*args)


def _odd_kernel(*refs, ct, t_total, final):
    it = iter(refs)
    xm_ref, xp_ref, xn_ref = next(it), next(it), next(it)
    shift_ref, scale_ref, gate_ref, g_ref = next(it), next(it), next(it), next(it)
    w1_ref, w2_ref, wo_ref, pw_ref, ps_ref, cw_ref = (next(it), next(it), next(it), next(it),
                                                      next(it), next(it))
    fg_ref = next(it) if final else None
    o_ref = next(it)
    h_ref = next(it)
    pe_ref = next(it)
    if final:
        pack_ref = next(it)
        nat_refs = [next(it) for _ in range(D // 128)]

    ti = pl.program_id(0)
    nb = xm_ref.shape[1]
    tm = ct * CH
    ne = tm + 2 * HALO

    def hn(xv):
        return _mod_norm(xv, g_ref[...], scale_ref[...], shift_ref[...])

    for i in range(CH):
        hi = hn(xm_ref[:, :, i * D:(i + 1) * D])
        if i < HALO:
            h_ref[0:ct, i + HALO] = hi
        else:
            h_ref[1:ct + 1, i - HALO] = hi
    for i in range(HALO):
        h_ref[0, i] = hn(xp_ref[:, :, (i + HALO) * D:(i + HALO + 1) * D])[0]
        h_ref[ct, i + HALO] = hn(xn_ref[:, :, i * D:(i + 1) * D])[0]

    he = h_ref[...].reshape(ne * nb, D).astype(BF16)
    pe = jnp.dot(he, w1_ref[...], preferred_element_type=F32)
    te = ti * tm - HALO + lax.broadcasted_iota(jnp.int32, (ne * nb, 1), 0) // nb
    valid = jnp.logical_and(te >= 0, te < t_total)
    pe_ref[:, 0:POOL_W] = jnp.where(valid, pe[:, :POOL_W], 0.0)
    pe_ref[:, POOL_W:POOL_W + CONV_W] = jnp.where(
        valid, pe[:, POOL_W:POOL_W + CONV_W] * pe[:, POOL_W + CONV_W:], 0.0)

    m0 = HALO * nb
    mr = tm * nb
    p2 = jnp.dot(he[m0:m0 + mr], w2_ref[...], preferred_element_type=F32)
    b_gate = p2[:, :CONV_W]
    sz = _silu(p2[:, CONV_W:])

    tpos = ti * tm + lax.broadcasted_iota(jnp.int32, (mr, 1), 0) // nb
    pooled = []
    for gi, w in enumerate(POOL_WINDOWS):
        c0 = gi * POOL_GW
        s = pe_ref[:, c0:c0 + POOL_GW]
        n = ne
        width = 1
        while width < w:
            s = s[0:(n - width) * nb] + s[width * nb:n * nb]
            n -= width
            width *= 2
        start = (HALO - w // 2) * nb
        total = s[start:start + mr]
        hi = jnp.minimum(tpos + w // 2, t_total)
        lo = jnp.maximum(tpos - w // 2, 0)
        cnt = (hi - lo).astype(F32)
        centre = pe_ref[m0:m0 + mr, c0:c0 + POOL_GW]
        pg = total / cnt - centre
        pooled.append(jnp.dot(pg.astype(BF16), pw_ref[gi], preferred_element_type=F32))
    y_c = jnp.concatenate(pooled, axis=1) * ps_ref[...]

    cwt = cw_ref[...]
    vm = pe_ref[m0 - nb:m0 - nb + mr, POOL_W:POOL_W + CONV_W]
    v0 = pe_ref[m0:m0 + mr, POOL_W:POOL_W + CONV_W]
    vp = pe_ref[m0 + nb:m0 + nb + mr, POOL_W:POOL_W + CONV_W]
    y_d = b_gate * (vm * cwt[0:1, :] + v0 * cwt[1:2, :] + vp * cwt[2:3, :])

    y = (jnp.concatenate([y_c, y_d], axis=1) * sz).astype(BF16)
    out = jnp.dot(y, wo_ref[...], preferred_element_type=F32)
    go = (gate_ref[...] * out.reshape(tm, nb, D)).reshape(ct, CH, nb, D)
    for i in range(CH):
        xo = xm_ref[:, :, i * D:(i + 1) * D] + go[:, i]
        if not final:
            o_ref[:, :, i * D:(i + 1) * D] = xo
            continue
        ms = jnp.mean(xo * xo, axis=-1, keepdims=True)
        xo = xo * lax.rsqrt(ms + EPS) * fg_ref[...]
        for col in range(D // 128):
            pack_ref[...] = xo[:, :, col * 128:(col + 1) * 128].reshape(ct * nb, 128)
            for bi in range(nb):
                nat_refs[col][pl.ds(bi * tm + i, ct, stride=CH), :] = pack_ref[pl.ds(bi, ct, stride=nb), :]
    if final:
        for col in range(D // 128):
            o_ref[:, :, col * 128:(col + 1) * 128] = nat_refs[col][...].reshape(nb, tm, 128)


def _odd_layer(xs, shift, scale, gate, g, w_in, w_out, pool_w, pool_scale, conv_w, final_g):
    nc, bsz, _ = xs.shape
    ct = 8
    n1 = POOL_W + 2 * CONV_W
    final = final_g is not None
    args = [xs, xs, xs, shift, scale, gate, g.reshape(1, D),
            w_in[:, :n1].astype(BF16), w_in[:, n1:].astype(BF16), w_out.astype(BF16),
            pool_w.astype(BF16), pool_scale.reshape(1, POOL_W), conv_w]
    vec = pl.BlockSpec((1, bsz, D), lambda i: (0, 0, 0))

    def const(*shape):
        return pl.BlockSpec(shape, lambda i: (0,) * len(shape), pipeline_mode=pl.Buffered(1))

    specs = [pl.BlockSpec((ct, bsz, CH * D), lambda i: (i, 0, 0)),
             pl.BlockSpec((1, bsz, CH * D), lambda i: (jnp.maximum(i * ct - 1, 0), 0, 0)),
             pl.BlockSpec((1, bsz, CH * D), lambda i: (jnp.minimum((i + 1) * ct, nc - 1), 0, 0)),
             vec, vec, vec,
             const(1, D), const(D, n1), const(D, CONV_W + MIX), const(MIX, D),
             const(len(POOL_WINDOWS), POOL_GW, POOL_GW), const(1, POOL_W), const(3, CONV_W)]
    scratch = [pltpu.VMEM((ct + 1, CH, bsz, D), F32),
               pltpu.VMEM(((ct + 1) * CH * bsz, POOL_W + CONV_W), F32)]
    if final:
        args.append(final_g.reshape(1, D))
        specs.append(const(1, D))
        scratch += [pltpu.VMEM((ct * bsz, 128), F32)] + [pltpu.VMEM((bsz * ct * CH, 128), F32)] * (D // 128)
        out_spec = pl.BlockSpec((bsz, ct * CH, D), lambda i: (0, i, 0))
        out_shape = jax.ShapeDtypeStruct((bsz, nc * CH, D), F32)
    else:
        out_spec = pl.BlockSpec((ct, bsz, CH * D), lambda i: (i, 0, 0))
        out_shape = jax.ShapeDtypeStruct((nc, bsz, CH * D), F32)
    return pl.pallas_call(
        functools.partial(_odd_kernel, ct=ct, t_total=nc * CH, final=final),
        grid=(nc // ct,),
        in_specs=specs,
        out_specs=out_spec,
        out_shape=out_shape,
        scratch_shapes=scratch,
        compiler_params=_cparams(("arbitrary",)),
    )(*args)


def _sincos_table(n_tok, dim):
    rows = n_tok // GRID_W
    rr, cc = np.meshgrid(np.arange(rows, dtype=np.float64), np.arange(GRID_W, dtype=np.float64),
                         indexing='ij')
    rr = rr.reshape(-1, 1)
    cc = cc.reshape(-1, 1)
    quarter = dim // 4
    omega = POS_BASE ** (-np.arange(quarter, dtype=np.float64) / quarter)
    tab = np.concatenate([np.sin(rr * omega), np.cos(rr * omega), np.sin(cc * omega), np.cos(cc * omega)],
                         axis=-1)
    return jnp.asarray(tab, dtype=F32)


def _block_diag(w):
    g, c, _ = w.shape
    eye = jnp.eye(g, dtype=w.dtype)
    return (eye[:, None, :, None] * w[:, :, None, :]).reshape(g * c, g * c)


def _to_stream_kernel(*refs, has_pos):
    it = iter(refs)
    x_ref = next(it)
    pos_ref = next(it) if has_pos else None
    o_ref = next(it)
    pack_ref = next(it)
    col_refs = [next(it) for _ in range(D // 128)]
    nb, tm, _ = x_ref.shape
    ct = tm // CH
    x = x_ref[...]
    if has_pos:
        x = x + pos_ref[...][None]
    for col in range(D // 128):
        col_refs[col][...] = x[:, :, col * 128:(col + 1) * 128].reshape(nb * tm, 128)
    for i in range(CH):
        for col in range(D // 128):
            for bi in range(nb):
                pack_ref[pl.ds(bi, ct, stride=nb), :] = col_refs[col][pl.ds(bi * tm + i, ct, stride=CH), :]
            lo = i * D + col * 128
            o_ref[:, :, lo:lo + 128] = pack_ref[...].reshape(ct, nb, 128)


def _to_stream(x, pos):
    bsz, t, _ = x.shape
    ct = 8
    tm = ct * CH
    args = [x]
    specs = [pl.BlockSpec((bsz, tm, D), lambda i: (0, i, 0))]
    if pos is not None:
        args.append(pos)
        specs.append(pl.BlockSpec((tm, D), lambda i: (i, 0)))
    return pl.pallas_call(
        functools.partial(_to_stream_kernel, has_pos=pos is not None),
        grid=(t // tm,),
        in_specs=specs,
        out_specs=pl.BlockSpec((ct, bsz, CH * D), lambda i: (i, 0, 0)),
        out_shape=jax.ShapeDtypeStruct((t // CH, bsz, CH * D), F32),
        scratch_shapes=[pltpu.VMEM((ct * bsz, 128), F32)] + [pltpu.VMEM((bsz * tm, 128), F32)] * (D // 128),
        compiler_params=_cparams(("arbitrary",)),
    )(*args)


def kernel(x, c, ctx, c_ctx, norm_g, ada_w, ada_b, even_w_in, even_w_out, s5_lam_re, s5_lam_im, s5_log_step, s5_b_re, s5_b_im, s5_c_re, s5_c_im, s5_d, s5_glu_w, s5_glu_b, fnet_w, odd_w_in, odd_w_out, pool_w, pool_scale, conv_w, final_g):
    bsz, n_tok, _ = x.shape
    depth = norm_g.shape[0]

    cond = jnp.concatenate([c, jnp.broadcast_to(c_ctx[None], (16 - bsz, D))], axis=0)
    ada = _ada_all(cond, ada_w, ada_b)

    def lat_mod(l, j):
        return ada[l, j, :bsz].reshape(1, bsz, D)

    def ctx_mod(l, j):
        return jnp.broadcast_to(ada[l, j, bsz].reshape(1, 1, D), (1, bsz, D))

    mt, bend_t, cp, lam16 = _s5_tables(s5_lam_re, s5_lam_im, s5_log_step, s5_b_re, s5_b_im, s5_c_re,
                                       s5_c_im, s5_d)

    need_ctx = [any(j % 2 == 0 for j in range(l + 1, depth)) for l in range(depth)]

    xl = _to_stream(x, _sincos_table(n_tok, D))
    xc = _to_stream(ctx, None)
    for l in range(depth):
        i = l // 2
        last = l == depth - 1
        if l % 2 == 0:
            wcs = _fnet_weights(_block_diag(fnet_w[i]))
            utl, xwl, zl = _even_in(xl, lat_mod(l, 0), lat_mod(l, 1), norm_g[l], even_w_in[i], wcs,
                                    full=True)
            if need_ctx[l]:
                utc, xwc, zc = _even_in(xc, ctx_mod(l, 0), ctx_mod(l, 1), norm_g[l], even_w_in[i], wcs,
                                        full=True)
            else:
                (utc,) = _even_in(xc, ctx_mod(l, 0), ctx_mod(l, 1), norm_g[l], even_w_in[i], None,
                                  full=False)
            ytl, ytc = _s5_mix(utl, utc, mt, bend_t, cp, lam16, bsz, i)
            ybl = _fnet(xwl, bsz)
            xl = _even_out(ytl, ybl, zl, xl, lat_mod(l, 2), s5_glu_w[i], s5_glu_b[i], even_w_out[i])
            if need_ctx[l]:
                ybc = _fnet(xwc, bsz)
                xc = _even_out(ytc, ybc, zc, xc, ctx_mod(l, 2), s5_glu_w[i], s5_glu_b[i], even_w_out[i])
        else:
            if need_ctx[l]:
                xc = _odd_layer(xc, ctx_mod(l, 0), ctx_mod(l, 1), ctx_mod(l, 2), norm_g[l],
                                odd_w_in[i], odd_w_out[i], pool_w[i], pool_scale[i], conv_w[i], None)
            xl = _odd_layer(xl, lat_mod(l, 0), lat_mod(l, 1), lat_mod(l, 2), norm_g[l], odd_w_in[i],
                            odd_w_out[i], pool_w[i], pool_scale[i], conv_w[i],
                            final_g if last else None)
    if depth % 2 == 1:
        raise NotImplementedError("final norm and (B, T, D) order are produced by the last (odd) layer")
    return xl
```

```python
import functools
import math

import numpy as np
import jax
import jax.numpy as jnp
from jax import lax
from jax.experimental import pallas as pl
from jax.experimental.pallas import tpu as pltpu

D = 1024
MIX = 1024
S5_W = 768
FN_W = 256
S5_H = 16
S5_G = 48
S5_P = 64
FN_G = 4
FN_GW = 64
POOL_W = 512
CONV_W = 512
POOL_WINDOWS = (2, 4, 8, 16)
POOL_GW = 128
GRID_W = 64
EPS = 1e-6
POS_BASE = 10000.0
CH = 16
HALO = 8
VMEM_LIMIT = 56 * 1024 * 1024

F32 = jnp.float32
BF16 = jnp.bfloat16
HI = lax.Precision.HIGHEST


def _cparams(sem):
    return pltpu.CompilerParams(dimension_semantics=sem, vmem_limit_bytes=VMEM_LIMIT)


def _sigmoid(v):
    return 0.5 * jnp.tanh(0.5 * v) + 0.5


def _silu(v):
    return v * _sigmoid(v)


def _gelu_tanh(v):
    c = math.sqrt(2.0 / math.pi)
    return 0.5 * v * (1.0 + jnp.tanh(c * (v + 0.044715 * (v * v * v))))


def _mod_norm(x, g, scale, shift):
    ms = jnp.mean(x * x, axis=-1, keepdims=True)
    y = x * lax.rsqrt(ms + EPS) * g
    return y * (1.0 + scale) + shift


def _ada_kernel(c_ref, w_ref, b_ref, o_ref):
    s = _silu(c_ref[...])
    o_ref[...] = jnp.dot(s, w_ref[...], precision=HI, preferred_element_type=F32) + b_ref[...]


def _ada_all(cond, ada_w, ada_b):
    depth = ada_w.shape[0]
    return pl.pallas_call(
        _ada_kernel,
        grid=(depth, 3),
        in_specs=[
            pl.BlockSpec((16, D), lambda l, j: (0, 0)),
            pl.BlockSpec((None, D, D), lambda l, j: (l, 0, j)),
            pl.BlockSpec((None, None, 1, D), lambda l, j: (l, j, 0, 0)),
        ],
        out_specs=pl.BlockSpec((None, None, 16, D), lambda l, j: (l, j, 0, 0)),
        out_shape=jax.ShapeDtypeStruct((depth, 3, 16, D), F32),
        compiler_params=_cparams(("arbitrary", "arbitrary")),
    )(cond, ada_w, ada_b.reshape(depth, 3, 1, D))


def _split3(v):
    hi = v.astype(BF16)
    lo = (v - hi.astype(F32)).astype(BF16)
    return hi, lo


def _s5_tables_kernel(lr_ref, li_ref, ls_ref, btr_ref, bti_ref, cr_ref, ci_ref, d_ref,
                      mt_ref, bt_ref, cp_ref, l16_ref, ere_ref, eim_ref, *, gb):
    nst = 2 * S5_P
    kk = CH * S5_H
    step = jnp.exp(ls_ref[...])
    lr = lr_ref[...]
    li = li_ref[...]
    a = lr * step
    b = li * step

    def powers(expo):
        mag = jnp.exp(expo * a)
        return mag * jnp.cos(expo * b), mag * jnp.sin(expo * b)

    row = lax.broadcasted_iota(jnp.int32, (1, CH, nst), 1).astype(F32)
    fwd = lax.broadcasted_iota(jnp.int32, (1, CH, nst), 2) < S5_P
    one = jnp.ones((1, 1, nst), F32)

    l1re, l1im = powers(one)
    n_re = l1re - 1.0
    den = lr * lr + li * li
    co_re = (n_re * lr + l1im * li) / den
    co_im = (l1im * lr - n_re * li) / den
    btr = btr_ref[...]
    bti = bti_ref[...]
    bb_re = co_re * btr - co_im * bti
    bb_im = co_re * bti + co_im * btr

    pe_re, pe_im = powers(jnp.where(fwd, (CH - 1) - row, row))
    for l in range(CH):
        pr = pe_re[:, l:l + 1, :]
        pi = pe_im[:, l:l + 1, :]
        ere_ref[:, l * S5_H:(l + 1) * S5_H, :] = pr * bb_re - pi * bb_im
        eim_ref[:, l * S5_H:(l + 1) * S5_H, :] = pr * bb_im + pi * bb_re

    cr = cr_ref[...]
    ci = ci_ref[...]
    pc_re, pc_im = powers(jnp.where(fwd, row + 1.0, CH - row))
    for j in range(CH):
        pr = pc_re[:, j:j + 1, :]
        pi = pc_im[:, j:j + 1, :]
        w_re = cr * pr - ci * pi
        w_im = cr * pi + ci * pr
        cp_ref[:, j * S5_H:(j + 1) * S5_H, :] = jnp.concatenate([w_re, -w_im], axis=2).astype(BF16)

    l16re, l16im = powers(one * float(CH))
    l16_ref[...] = jnp.concatenate([l16re, l16im], axis=1)

    fwd2 = lax.broadcasted_iota(jnp.int32, (S5_H, nst), 1) < S5_P
    lane = lax.broadcasted_iota(jnp.int32, (S5_H, kk), 1)
    iblk = lane // S5_H
    hrow = lax.broadcasted_iota(jnp.int32, (S5_H, kk), 0)
    nt = (((1,), (1,)), ((), ()))
    for g in range(gb):
        ere = ere_ref[g]
        eim = eim_ref[g]
        bt_ref[g, 0:nst, :] = ere.T.astype(BF16)
        bt_ref[g, nst:2 * nst, :] = eim.T.astype(BF16)
        e_hi, e_lo = _split3(jnp.concatenate([ere, eim], axis=1))
        rhs = jnp.concatenate([e_hi, e_hi, e_lo], axis=1)
        crg = cr[g]
        cig = ci[g]

        def lag_kernels(sel):
            c2 = jnp.concatenate([jnp.where(sel, crg, 0.0), jnp.where(sel, -cig, 0.0)], axis=1)
            c_hi, c_lo = _split3(c2)
            lhs = jnp.concatenate([c_hi, c_lo, c_hi], axis=1)
            return lax.dot_general(lhs, rhs, nt, preferred_element_type=F32)

        kf = lag_kernels(fwd2)
        kb = lag_kernels(jnp.logical_not(fwd2))
        dg = d_ref[g]
        for j in range(CH):
            sf = (kk - (CH - 1 - j) * S5_H) % kk
            rf = pltpu.roll(kf, sf, 1) if sf else kf
            rb = pltpu.roll(kb, j * S5_H, 1) if j else kb
            blk = (jnp.where(iblk <= j, rf, 0.0) + jnp.where(iblk >= j, rb, 0.0)
                   + jnp.where(lane == j * S5_H + hrow, dg, 0.0))
            mt_ref[g, j * S5_H:(j + 1) * S5_H, :] = blk.astype(BF16)


def _s5_tables(lam_re, lam_im, log_step, b_re, b_im, c_re, c_im, d_skip):
    n = lam_re.shape[0] * S5_G
    gb = 8
    nst = 2 * S5_P
    kk = CH * S5_H

    def fb(v):
        return jnp.concatenate([v[:, 0], v[:, 1]], axis=-1).reshape(n, v.shape[3], nst)

    lr = fb(lam_re[:, :, :, None, :])
    li = fb(lam_im[:, :, :, None, :])
    ls = fb(jnp.broadcast_to(log_step[:, :, :, None, None], log_step.shape + (1, S5_P)))
    btr = fb(jnp.swapaxes(b_re, -1, -2))
    bti = fb(jnp.swapaxes(b_im, -1, -2))
    cr = fb(c_re)
    ci = fb(c_im)
    d = d_skip.reshape(n, S5_H, 1)

    def spec(r, c):
        return pl.BlockSpec((gb, r, c), lambda g: (g, 0, 0))

    return pl.pallas_call(
        functools.partial(_s5_tables_kernel, gb=gb),
        grid=(n // gb,),
        in_specs=[spec(1, nst), spec(1, nst), spec(1, nst), spec(S5_H, nst), spec(S5_H, nst),
                  spec(S5_H, nst), spec(S5_H, nst), spec(S5_H, 1)],
        out_specs=[spec(kk, kk), spec(2 * nst, kk), spec(kk, 2 * nst), spec(2, nst)],
        out_shape=[jax.ShapeDtypeStruct((n, kk, kk), BF16),
                   jax.ShapeDtypeStruct((n, 2 * nst, kk), BF16),
                   jax.ShapeDtypeStruct((n, kk, 2 * nst), BF16),
                   jax.ShapeDtypeStruct((n, 2, nst), F32)],
        scratch_shapes=[pltpu.VMEM((gb, kk, nst), F32)] * 2,
        compiler_params=_cparams(("arbitrary",)),
    )(lr, li, ls, btr, bti, cr, ci, d)


def _even_in_kernel(*refs, full):
    it = iter(refs)
    x_ref = next(it)
    shift_ref, scale_ref, g_ref, wat_ref = next(it), next(it), next(it), next(it)
    if full:
        wbz_ref, wcs_ref = next(it), next(it)
    ut_ref = next(it)
    if full:
        xw_ref, z_ref = next(it), next(it)
        scr = [next(it) for _ in range(4)]

    x = x_ref[...]
    h = _mod_norm(x, g_ref[...], scale_ref[...], shift_ref[...])
    nc, nb, _ = x.shape
    rows = nc * nb
    hb = h.reshape(rows, D).astype(BF16)
    pt = lax.dot_general(wat_ref[...], hb, (((1,), (1,)), ((), ())), preferred_element_type=F32)
    ut_ref[...] = pt.astype(BF16).reshape(S5_G, S5_H, rows)
    if full:
        p = jnp.dot(hb, wbz_ref[...], preferred_element_type=F32)
        z_ref[...] = _silu(p[:, FN_W:]).astype(BF16)
        xw = jnp.dot(p[:, :FN_W].astype(BF16), wcs_ref[...], preferred_element_type=F32)
        for q in range(4):
            scr[q][...] = xw[:, q * 128:(q + 1) * 128]
        for q in range(4):
            part, half = divmod(q, 2)
            for bi in range(nb):
                piece = scr[q][pl.ds(bi, nc, stride=nb), :]
                lo = bi * FN_W + half * 128
                xw_ref[part, :, lo:lo + 128] = piece.astype(BF16)


def _even_in(xs, shift, scale, g, w_in, wcs, *, full):
    nc, bsz, _ = xs.shape
    rows = nc * bsz
    wat = w_in[:, :S5_W].T.astype(BF16)
    args = [xs]
    specs = [pl.BlockSpec((nc, bsz, D), lambda i: (0, 0, i))]
    args += [shift, scale, g.reshape(1, D), wat]
    specs += [pl.BlockSpec((1, bsz, D), lambda i: (0, 0, 0)),
              pl.BlockSpec((1, bsz, D), lambda i: (0, 0, 0)),
              pl.BlockSpec((1, D), lambda i: (0, 0)),
              pl.BlockSpec((S5_W, D), lambda i: (0, 0))]
    out_shape = [jax.ShapeDtypeStruct((S5_G, CH * S5_H, rows), BF16)]
    out_specs = [pl.BlockSpec((S5_G, S5_H, rows), lambda i: (0, i, 0))]
    scratch = []
    if full:
        args += [w_in[:, S5_W:].astype(BF16), wcs]
        specs += [pl.BlockSpec((D, FN_W + MIX), lambda i: (0, 0)),
                  pl.BlockSpec((FN_W, 2 * FN_W), lambda i: (0, 0))]
        out_shape += [jax.ShapeDtypeStruct((2, CH, nc, bsz * FN_W), BF16),
                      jax.ShapeDtypeStruct((CH, rows, MIX), BF16)]
        out_specs += [pl.BlockSpec((2, None, nc, bsz * FN_W), lambda i: (0, i, 0, 0)),
                      pl.BlockSpec((None, rows, MIX), lambda i: (i, 0, 0))]
        scratch = [pltpu.VMEM((rows, 128), F32)] * 4
    return pl.pallas_call(
        functools.partial(_even_in_kernel, full=full),
        grid=(CH,),
        in_specs=specs,
        out_specs=out_specs,
        out_shape=out_shape,
        scratch_shapes=scratch,
        compiler_params=_cparams(("arbitrary",)),
    )(*args)


def _s5_kernel(utl_ref, utc_ref, mt_ref, bt_ref, cp_ref, l16_ref, ytl_ref, ytc_ref,
               sre_ref, sim_ref, are_ref, aim_ref, bre_ref, bim_ref, *, bsz, ncl, ncc, gs):
    nl = bsz * ncl
    nx = bsz * ncc
    nst = 2 * S5_P
    for g in range(gs):
        bt = bt_ref[g]
        sl = jnp.dot(bt, utl_ref[g], preferred_element_type=F32).T
        sc = jnp.dot(bt, utc_ref[g], preferred_element_type=F32).T
        sre_ref[g, 0:nl, :] = sl[:, :nst]
        sim_ref[g, 0:nl, :] = sl[:, nst:]
        sre_ref[g, nl:nl + nx, :] = sc[:, :nst]
        sim_ref[g, nl:nl + nx, :] = sc[:, nst:]

    lam = [l16_ref[g] for g in range(gs)]
    is_fwd = lax.broadcasted_iota(jnp.int32, (bsz, nst), 1) < S5_P

    def make_step(base, nchunk):
        def step(c, carry):
            rf = pl.ds(pl.multiple_of(base + c * bsz, bsz), bsz)
            rb = pl.ds(pl.multiple_of(base + (nchunk - 1 - c) * bsz, bsz), bsz)
            out = []
            for g in range(gs):
                sre, sim = carry[2 * g], carry[2 * g + 1]
                lre = lam[g][0:1, :]
                lim = lam[g][1:2, :]
                are_ref[g, rf, :] = sre
                aim_ref[g, rf, :] = sim
                bre_ref[g, rb, :] = sre
                bim_ref[g, rb, :] = sim
                in_re = jnp.where(is_fwd, sre_ref[g, rf, :], sre_ref[g, rb, :])
                in_im = jnp.where(is_fwd, sim_ref[g, rf, :], sim_ref[g, rb, :])
                out.append(lre * sre - lim * sim + in_re)
                out.append(lre * sim + lim * sre + in_im)
            return tuple(out)
        return step

    zero = jnp.zeros((bsz, nst), F32)
    carry = lax.fori_loop(0, ncc, make_step(nl, ncc), (zero,) * (2 * gs))
    lax.fori_loop(0, ncl, make_step(0, ncl), carry)

    nt = (((1,), (1,)), ((), ()))
    fwd_rows = lax.broadcasted_iota(jnp.int32, (nl + nx, nst), 1) < S5_P
    for g in range(gs):
        mt = mt_ref[g]
        cp = cp_ref[g]
        s0 = jnp.concatenate([jnp.where(fwd_rows, are_ref[g], bre_ref[g]),
                              jnp.where(fwd_rows, aim_ref[g], bim_ref[g])], axis=1).astype(BF16)
        ytl_ref[g] = _gelu_tanh(jnp.dot(mt, utl_ref[g], preferred_element_type=F32)
                                + lax.dot_general(cp, s0[0:nl], nt, preferred_element_type=F32))
        ytc_ref[g] = _gelu_tanh(jnp.dot(mt, utc_ref[g], preferred_element_type=F32)
                                + lax.dot_general(cp, s0[nl:nl + nx], nt, preferred_element_type=F32))


def _s5_mix(utl, utc, mt, bend_t, cp, lam16, bsz, layer):
    nl = utl.shape[2]
    nx = utc.shape[2]
    kk = CH * S5_H
    nst = 2 * S5_P
    gs = 4
    off = layer * (S5_G // gs)

    def gspec(r, c):
        return pl.BlockSpec((gs, r, c), lambda g: (g, 0, 0))

    def tspec(r, c):
        return pl.BlockSpec((gs, r, c), lambda g: (g + off, 0, 0))

    return pl.pallas_call(
        functools.partial(_s5_kernel, bsz=bsz, ncl=nl // bsz, ncc=nx // bsz, gs=gs),
        grid=(S5_G // gs,),
        in_specs=[gspec(kk, nl), gspec(kk, nx), tspec(kk, kk), tspec(2 * nst, kk), tspec(kk, 2 * nst),
                  tspec(2, nst)],
        out_specs=[gspec(kk, nl), gspec(kk, nx)],
        out_shape=[jax.ShapeDtypeStruct((S5_G, kk, nl), F32),
                   jax.ShapeDtypeStruct((S5_G, kk, nx), F32)],
        scratch_shapes=[pltpu.VMEM((gs, nl + nx, nst), F32)] * 6,
        compiler_params=_cparams(("arbitrary",)),
    )(utl, utc, mt, bend_t, cp, lam16)


def _fnet_weights_kernel(ccs_ref, fw_ref, o_ref):
    fw = fw_ref[...]
    ccs = ccs_ref[...]
    wc = jnp.dot(ccs[:, :FN_W], fw, precision=HI, preferred_element_type=F32)
    ws = jnp.dot(ccs[:, FN_W:], fw, precision=HI, preferred_element_type=F32)
    o_ref[...] = jnp.concatenate([wc, ws], axis=1).astype(BF16)


def _fnet_weights(fw_bd):
    return pl.pallas_call(
        _fnet_weights_kernel,
        out_shape=jax.ShapeDtypeStruct((FN_W, 2 * FN_W), BF16),
    )(_dft_channel_matrix(), fw_bd)


def _fnet_kernel(cm_ref, v_ref, o_ref, cmb_ref):
    @pl.when(pl.program_id(1) == 0)
    def _():
        cmb_ref[...] = cm_ref[...].astype(BF16)

    v = v_ref[...]
    v2 = v.reshape(v.shape[0] * v.shape[1] * v.shape[2], FN_W)
    y = jnp.dot(cmb_ref[...], v2, preferred_element_type=F32)
    o_ref[...] = y.reshape(o_ref.shape)


def _dft_time_matrix(nc):
    t = nc * CH
    order = (np.arange(nc)[None, :] * CH + np.arange(CH)[:, None]).reshape(-1)
    prod = (order[:, None].astype(np.int64) * order[None, :].astype(np.int64)) % t
    ang = prod.astype(np.float64) * (2.0 * np.pi / t)
    scale = 1.0 / math.sqrt(t * FN_GW)
    return jnp.asarray(np.concatenate([np.cos(ang), -np.sin(ang)], axis=1) * scale, dtype=F32)


def _dft_channel_matrix():
    c = np.arange(FN_GW)
    ang = (c[:, None] * c[None, :] % FN_GW).astype(np.float64) * (2.0 * np.pi / FN_GW)
    eye = np.eye(FN_G)
    return jnp.asarray(np.concatenate([np.kron(eye, np.cos(ang)), np.kron(eye, np.sin(ang))], axis=1),
                       dtype=F32)


def _fnet(xw, bsz):
    nc = xw.shape[2]
    t = nc * CH
    rt = min(t, 512)
    ipt = rt // nc
    cm = _dft_time_matrix(nc)
    return pl.pallas_call(
        _fnet_kernel,
        grid=(t // rt, bsz),
        in_specs=[pl.BlockSpec((rt, 2 * t), lambda r, b: (r, 0)),
                  pl.BlockSpec((2, CH, nc, FN_W), lambda r, b: (0, 0, 0, b))],
        out_specs=pl.BlockSpec((ipt, nc, FN_W), lambda r, b: (r, 0, b)),
        out_shape=jax.ShapeDtypeStruct((CH, nc, bsz * FN_W), F32),
        scratch_shapes=[pltpu.VMEM((rt, 2 * t), BF16)],
        compiler_params=_cparams(("arbitrary", "arbitrary")),
    )(cm, xw)


def _even_out_kernel(yt_ref, yb_ref, z_ref, x_ref, gate_ref, gw_ref, gb_ref, woa_ref, wob_ref, o_ref,
                     scr0_ref, scr1_ref):
    scr = [scr0_ref, scr1_ref]
    yt = yt_ref[...]
    rows = yt.shape[2]
    ya = yt.reshape(S5_W, rows).T
    glu = jnp.dot(ya.astype(BF16), gw_ref[...], preferred_element_type=F32) + gb_ref[...]
    ya = ya * _sigmoid(glu)
    x = x_ref[...]
    nc, nb, _ = x.shape
    for half in range(2):
        for bi in range(nb):
            lo = bi * FN_W + half * 128
            scr[half][pl.ds(bi, nc, stride=nb), :] = yb_ref[:, lo:lo + 128]
    yb = jnp.concatenate([scr[0][...], scr[1][...]], axis=1)
    sz = z_ref[...]
    ma = (ya * sz[:, :S5_W]).astype(BF16)
    mb = (yb * sz[:, S5_W:]).astype(BF16)
    out = (jnp.dot(ma, woa_ref[...], preferred_element_type=F32)
           + jnp.dot(mb, wob_ref[...], preferred_element_type=F32))
    o_ref[...] = x + gate_ref[...] * out.reshape(x.shape)


def _even_out(yt, yb, z, xs, gate, glu_w, glu_b, w_out):
    nc, bsz, _ = xs.shape
    rows = nc * bsz
    args = [yt, yb, z, xs]
    specs = [pl.BlockSpec((S5_G, S5_H, rows), lambda j: (0, j, 0)),
             pl.BlockSpec((None, nc, bsz * FN_W), lambda j: (j, 0, 0)),
             pl.BlockSpec((None, rows, MIX), lambda j: (j, 0, 0)),
             pl.BlockSpec((nc, bsz, D), lambda j: (0, 0, j))]
    args += [gate, glu_w.astype(BF16), glu_b.reshape(1, S5_W), w_out[:S5_W].astype(BF16),
             w_out[S5_W:].astype(BF16)]
    specs += [pl.BlockSpec((1, bsz, D), lambda j: (0, 0, 0)),
              pl.BlockSpec((S5_W, S5_W), lambda j: (0, 0)),
              pl.BlockSpec((1, S5_W), lambda j: (0, 0)),
              pl.BlockSpec((S5_W, D), lambda j: (0, 0)),
              pl.BlockSpec((FN_W, D), lambda j: (0, 0))]
    return pl.pallas_call(
        _even_out_kernel,
        grid=(CH,),
        in_specs=specs,
        out_specs=pl.BlockSpec((nc, bsz, D), lambda j: (0, 0, j)),
        out_shape=jax.ShapeDtypeStruct((nc, bsz, CH * D), F32),
        scratch_shapes=[pltpu.VMEM((rows, 128), F32)] * 2,
        compiler_params=_cparams(("arbitrary",)),
    )(*args)


def _odd_kernel(*refs, ct, t_total, final):
    it = iter(refs)
    xm_ref, xp_ref, xn_ref = next(it), next(it), next(it)
    shift_ref, scale_ref, gate_ref, g_ref = next(it), next(it), next(it), next(it)
    w1_ref, w2_ref, wo_ref, pw_ref, ps_ref, cw_ref = (next(it), next(it), next(it), next(it),
                                                      next(it), next(it))
    fg_ref = next(it) if final else None
    o_ref = next(it)
    h_ref = next(it)
    pe_ref = next(it)
    fin_ref = next(it) if final else None

    ti = pl.program_id(0)
    nb = xm_ref.shape[1]
    tm = ct * CH
    ne = tm + 2 * HALO

    def hn(xv):
        return _mod_norm(xv, g_ref[...], scale_ref[...], shift_ref[...])

    for i in range(CH):
        hi = hn(xm_ref[:, :, i * D:(i + 1) * D])
        if i < HALO:
            h_ref[0:ct, i + HALO] = hi
        else:
            h_ref[1:ct + 1, i - HALO] = hi
    for i in range(HALO):
        h_ref[0, i] = hn(xp_ref[:, :, (i + HALO) * D:(i + HALO + 1) * D])[0]
        h_ref[ct, i + HALO] = hn(xn_ref[:, :, i * D:(i + 1) * D])[0]

    he = h_ref[...].reshape(ne * nb, D).astype(BF16)
    pe = jnp.dot(he, w1_ref[...], preferred_element_type=F32)
    te = ti * tm - HALO + lax.broadcasted_iota(jnp.int32, (ne * nb, 1), 0) // nb
    valid = jnp.logical_and(te >= 0, te < t_total)
    pe_ref[:, 0:POOL_W] = jnp.where(valid, pe[:, :POOL_W], 0.0)
    pe_ref[:, POOL_W:POOL_W + CONV_W] = jnp.where(
        valid, pe[:, POOL_W:POOL_W + CONV_W] * pe[:, POOL_W + CONV_W:], 0.0)

    m0 = HALO * nb
    mr = tm * nb
    p2 = jnp.dot(he[m0:m0 + mr], w2_ref[...], preferred_element_type=F32)
    b_gate = p2[:, :CONV_W]
    sz = _silu(p2[:, CONV_W:])

    tpos = ti * tm + lax.broadcasted_iota(jnp.int32, (mr, 1), 0) // nb
    pooled = []
    for gi, w in enumerate(POOL_WINDOWS):
        c0 = gi * POOL_GW
        s = pe_ref[:, c0:c0 + POOL_GW]
        n = ne
        width = 1
        while width < w:
            s = s[0:(n - width) * nb] + s[width * nb:n * nb]
            n -= width
            width *= 2
        start = (HALO - w // 2) * nb
        total = s[start:start + mr]
        hi = jnp.minimum(tpos + w // 2, t_total)
        lo = jnp.maximum(tpos - w // 2, 0)
        cnt = (hi - lo).astype(F32)
        centre = pe_ref[m0:m0 + mr, c0:c0 + POOL_GW]
        pg = total / cnt - centre
        pooled.append(jnp.dot(pg.astype(BF16), pw_ref[gi], preferred_element_type=F32))
    y_c = jnp.concatenate(pooled, axis=1) * ps_ref[...]

    cwt = cw_ref[...]
    vm = pe_ref[m0 - nb:m0 - nb + mr, POOL_W:POOL_W + CONV_W]
    v0 = pe_ref[m0:m0 + mr, POOL_W:POOL_W + CONV_W]
    vp = pe_ref[m0 + nb:m0 + nb + mr, POOL_W:POOL_W + CONV_W]
    y_d = b_gate * (vm * cwt[0:1, :] + v0 * cwt[1:2, :] + vp * cwt[2:3, :])

    y = (jnp.concatenate([y_c, y_d], axis=1) * sz).astype(BF16)
    out = jnp.dot(y, wo_ref[...], preferred_element_type=F32)
    go = (gate_ref[...] * out.reshape(tm, nb, D)).reshape(ct, CH, nb, D)
    for i in range(CH):
        xo = xm_ref[:, :, i * D:(i + 1) * D] + go[:, i]
        if not final:
            o_ref[:, :, i * D:(i + 1) * D] = xo
            continue
        ms = jnp.mean(xo * xo, axis=-1, keepdims=True)
        fin_ref[:, :, i * D:(i + 1) * D] = xo * lax.rsqrt(ms + EPS) * fg_ref[...]
    if final:
        o_ref[...] = pltpu.einshape("cb(id)->b(ci)d", fin_ref[...], i=CH)


def _odd_layer(xs, shift, scale, gate, g, w_in, w_out, pool_w, pool_scale, conv_w, final_g):
    nc, bsz, _ = xs.shape
    ct = 8
    n1 = POOL_W + 2 * CONV_W
    final = final_g is not None
    args = [xs, xs, xs, shift, scale, gate, g.reshape(1, D),
            w_in[:, :n1].astype(BF16), w_in[:, n1:].astype(BF16), w_out.astype(BF16),
            pool_w.astype(BF16), pool_scale.reshape(1, POOL_W), conv_w]
    vec = pl.BlockSpec((1, bsz, D), lambda i: (0, 0, 0))

    def const(*shape):
        return pl.BlockSpec(shape, lambda i: (0,) * len(shape), pipeline_mode=pl.Buffered(1))

    specs = [pl.BlockSpec((ct, bsz, CH * D), lambda i: (i, 0, 0)),
             pl.BlockSpec((1, bsz, CH * D), lambda i: (jnp.maximum(i * ct - 1, 0), 0, 0)),
             pl.BlockSpec((1, bsz, CH * D), lambda i: (jnp.minimum((i + 1) * ct, nc - 1), 0, 0)),
             vec, vec, vec,
             const(1, D), const(D, n1), const(D, CONV_W + MIX), const(MIX, D),
             const(len(POOL_WINDOWS), POOL_GW, POOL_GW), const(1, POOL_W), const(3, CONV_W)]
    scratch = [pltpu.VMEM((ct + 1, CH, bsz, D), F32),
               pltpu.VMEM(((ct + 1) * CH * bsz, POOL_W + CONV_W), F32)]
    if final:
        args.append(final_g.reshape(1, D))
        specs.append(const(1, D))
        scratch.append(pltpu.VMEM((ct, bsz, CH * D), F32))
        out_spec = pl.BlockSpec((bsz, ct * CH, D), lambda i: (0, i, 0))
        out_shape = jax.ShapeDtypeStruct((bsz, nc * CH, D), F32)
    else:
        out_spec = pl.BlockSpec((ct, bsz, CH * D), lambda i: (i, 0, 0))
        out_shape = jax.ShapeDtypeStruct((nc, bsz, CH * D), F32)
    return pl.pallas_call(
        functools.partial(_odd_kernel, ct=ct, t_total=nc * CH, final=final),
        grid=(nc // ct,),
        in_specs=specs,
        out_specs=out_spec,
        out_shape=out_shape,
        scratch_shapes=scratch,
        compiler_params=_cparams(("arbitrary",)),
    )(*args)


def _sincos_table(n_tok, dim):
    rows = n_tok // GRID_W
    rr, cc = np.meshgrid(np.arange(rows, dtype=np.float64), np.arange(GRID_W, dtype=np.float64),
                         indexing='ij')
    rr = rr.reshape(-1, 1)
    cc = cc.reshape(-1, 1)
    quarter = dim // 4
    omega = POS_BASE ** (-np.arange(quarter, dtype=np.float64) / quarter)
    tab = np.concatenate([np.sin(rr * omega), np.cos(rr * omega), np.sin(cc * omega), np.cos(cc * omega)],
                         axis=-1)
    return jnp.asarray(tab, dtype=F32)


def _block_diag(w):
    g, c, _ = w.shape
    eye = jnp.eye(g, dtype=w.dtype)
    return (eye[:, None, :, None] * w[:, :, None, :]).reshape(g * c, g * c)


def _to_stream_kernel(*refs, has_pos):
    it = iter(refs)
    x_ref = next(it)
    pos_ref = next(it) if has_pos else None
    o_ref = next(it)
    x = x_ref[...]
    if has_pos:
        x = x + pos_ref[...][None]
    o_ref[...] = pltpu.einshape("b(ci)d->cb(id)", x, i=CH)


def _to_stream(x, pos):
    bsz, t, _ = x.shape
    ct = 8
    tm = ct * CH
    args = [x]
    specs = [pl.BlockSpec((bsz, tm, D), lambda i: (0, i, 0))]
    if pos is not None:
        args.append(pos)
        specs.append(pl.BlockSpec((tm, D), lambda i: (i, 0)))
    return pl.pallas_call(
        functools.partial(_to_stream_kernel, has_pos=pos is not None),
        grid=(t // tm,),
        in_specs=specs,
        out_specs=pl.BlockSpec((ct, bsz, CH * D), lambda i: (i, 0, 0)),
        out_shape=jax.ShapeDtypeStruct((t // CH, bsz, CH * D), F32),
        compiler_params=_cparams(("arbitrary",)),
    )(*args)


def kernel(x, c, ctx, c_ctx, norm_g, ada_w, ada_b, even_w_in, even_w_out, s5_lam_re, s5_lam_im, s5_log_step, s5_b_re, s5_b_im, s5_c_re, s5_c_im, s5_d, s5_glu_w, s5_glu_b, fnet_w, odd_w_in, odd_w_out, pool_w, pool_scale, conv_w, final_g):
    bsz, n_tok, _ = x.shape
    depth = norm_g.shape[0]

    cond = jnp.concatenate([c, jnp.broadcast_to(c_ctx[None], (16 - bsz, D))], axis=0)
    ada = _ada_all(cond, ada_w, ada_b)

    def lat_mod(l, j):
        return ada[l, j, :bsz].reshape(1, bsz, D)

    def ctx_mod(l, j):
        return jnp.broadcast_to(ada[l, j, bsz].reshape(1, 1, D), (1, bsz, D))

    mt, bend_t, cp, lam16 = _s5_tables(s5_lam_re, s5_lam_im, s5_log_step, s5_b_re, s5_b_im, s5_c_re,
                                       s5_c_im, s5_d)

    need_ctx = [any(j % 2 == 0 for j in range(l + 1, depth)) for l in range(depth)]

    xl = _to_stream(x, _sincos_table(n_tok, D))
    xc = _to_stream(ctx, None)
    for l in range(depth):
        i = l // 2
        last = l == depth - 1
        if l % 2 == 0:
            wcs = _fnet_weights(_block_diag(fnet_w[i]))
            utl, xwl, zl = _even_in(xl, lat_mod(l, 0), lat_mod(l, 1), norm_g[l], even_w_in[i], wcs,
                                    full=True)
            if need_ctx[l]:
                utc, xwc, zc = _even_in(xc, ctx_mod(l, 0), ctx_mod(l, 1), norm_g[l], even_w_in[i], wcs,
                                        full=True)
            else:
                (utc,) = _even_in(xc, ctx_mod(l, 0), ctx_mod(l, 1), norm_g[l], even_w_in[i], None,
                                  full=False)
            ytl, ytc = _s5_mix(utl, utc, mt, bend_t, cp, lam16, bsz, i)
            ybl = _fnet(xwl, bsz)
            xl = _even_out(ytl, ybl, zl, xl, lat_mod(l, 2), s5_glu_w[i], s5_glu_b[i], even_w_out[i])
            if need_ctx[l]:
                ybc = _fnet(xwc, bsz)
                xc = _even_out(ytc, ybc, zc, xc, ctx_mod(l, 2), s5_glu_w[i], s5_glu_b[i], even_w_out[i])
        else:
            if need_ctx[l]:
                xc = _odd_layer(xc, ctx_mod(l, 0), ctx_mod(l, 1), ctx_mod(l, 2), norm_g[l],
                                odd_w_in[i], odd_w_out[i], pool_w[i], pool_scale[i], conv_w[i], None)
            xl = _odd_layer(xl, lat_mod(l, 0), lat_mod(l, 1), lat_mod(l, 2), norm_g[l], odd_w_in[i],
                            odd_w_out[i], pool_w[i], pool_scale[i], conv_w[i],
                            final_g if last else None)
    if depth % 2 == 1:
        raise NotImplementedError("final norm and (B, T, D) order are produced by the last (odd) layer")
    return xl
```

```python
import functools
import math

import numpy as np
import jax
import jax.numpy as jnp
from jax import lax
from jax.experimental import pallas as pl
from jax.experimental.pallas import tpu as pltpu

D = 1024
MIX = 1024
S5_W = 768
FN_W = 256
S5_H = 16
S5_G = 48
S5_P = 64
FN_G = 4
FN_GW = 64
POOL_W = 512
CONV_W = 512
POOL_WINDOWS = (2, 4, 8, 16)
POOL_GW = 128
GRID_W = 64
EPS = 1e-6
POS_BASE = 10000.0
CH = 16
HALO = 8
VMEM_LIMIT = 56 * 1024 * 1024

F32 = jnp.float32
BF16 = jnp.bfloat16
HI = lax.Precision.HIGHEST


def _cparams(sem):
    return pltpu.CompilerParams(dimension_semantics=sem, vmem_limit_bytes=VMEM_LIMIT)


def _sigmoid(v):
    return 0.5 * jnp.tanh(0.5 * v) + 0.5


def _silu(v):
    return v * _sigmoid(v)


def _gelu_tanh(v):
    c = math.sqrt(2.0 / math.pi)
    return 0.5 * v * (1.0 + jnp.tanh(c * (v + 0.044715 * (v * v * v))))


def _mod_norm(x, g, scale, shift):
    ms = jnp.mean(x * x, axis=-1, keepdims=True)
    y = x * lax.rsqrt(ms + EPS) * g
    return y * (1.0 + scale) + shift


def _ada_kernel(c_ref, w_ref, b_ref, o_ref):
    s = _silu(c_ref[...])
    o_ref[...] = jnp.dot(s, w_ref[...], precision=HI, preferred_element_type=F32) + b_ref[...]


def _ada_all(cond, ada_w, ada_b):
    depth = ada_w.shape[0]
    return pl.pallas_call(
        _ada_kernel,
        grid=(depth, 3),
        in_specs=[
            pl.BlockSpec((16, D), lambda l, j: (0, 0)),
            pl.BlockSpec((None, D, D), lambda l, j: (l, 0, j)),
            pl.BlockSpec((None, None, 1, D), lambda l, j: (l, j, 0, 0)),
        ],
        out_specs=pl.BlockSpec((None, None, 16, D), lambda l, j: (l, j, 0, 0)),
        out_shape=jax.ShapeDtypeStruct((depth, 3, 16, D), F32),
        compiler_params=_cparams(("arbitrary", "arbitrary")),
    )(cond, ada_w, ada_b.reshape(depth, 3, 1, D))


def _split3(v):
    hi = v.astype(BF16)
    lo = (v - hi.astype(F32)).astype(BF16)
    return hi, lo


def _s5_tables_kernel(lr_ref, li_ref, ls_ref, btr_ref, bti_ref, cr_ref, ci_ref, d_ref,
                      mt_ref, bt_ref, cp_ref, l16_ref, ere_ref, eim_ref, *, gb):
    nst = 2 * S5_P
    kk = CH * S5_H
    step = jnp.exp(ls_ref[...])
    lr = lr_ref[...]
    li = li_ref[...]
    a = lr * step
    b = li * step

    def powers(expo):
        mag = jnp.exp(expo * a)
        return mag * jnp.cos(expo * b), mag * jnp.sin(expo * b)

    row = lax.broadcasted_iota(jnp.int32, (1, CH, nst), 1).astype(F32)
    fwd = lax.broadcasted_iota(jnp.int32, (1, CH, nst), 2) < S5_P
    one = jnp.ones((1, 1, nst), F32)

    l1re, l1im = powers(one)
    n_re = l1re - 1.0
    den = lr * lr + li * li
    co_re = (n_re * lr + l1im * li) / den
    co_im = (l1im * lr - n_re * li) / den
    btr = btr_ref[...]
    bti = bti_ref[...]
    bb_re = co_re * btr - co_im * bti
    bb_im = co_re * bti + co_im * btr

    pe_re, pe_im = powers(jnp.where(fwd, (CH - 1) - row, row))
    for l in range(CH):
        pr = pe_re[:, l:l + 1, :]
        pi = pe_im[:, l:l + 1, :]
        ere_ref[:, l * S5_H:(l + 1) * S5_H, :] = pr * bb_re - pi * bb_im
        eim_ref[:, l * S5_H:(l + 1) * S5_H, :] = pr * bb_im + pi * bb_re

    cr = cr_ref[...]
    ci = ci_ref[...]
    pc_re, pc_im = powers(jnp.where(fwd, row + 1.0, CH - row))
    for j in range(CH):
        pr = pc_re[:, j:j + 1, :]
        pi = pc_im[:, j:j + 1, :]
        w_re = cr * pr - ci * pi
        w_im = cr * pi + ci * pr
        cp_ref[:, j * S5_H:(j + 1) * S5_H, :] = jnp.concatenate([w_re, -w_im], axis=2).astype(BF16)

    l16re, l16im = powers(one * float(CH))
    l16_ref[...] = jnp.concatenate([l16re, l16im], axis=1)

    fwd2 = lax.broadcasted_iota(jnp.int32, (S5_H, nst), 1) < S5_P
    lane = lax.broadcasted_iota(jnp.int32, (S5_H, kk), 1)
    iblk = lane // S5_H
    hrow = lax.broadcasted_iota(jnp.int32, (S5_H, kk), 0)
    nt = (((1,), (1,)), ((), ()))
    for g in range(gb):
        ere = ere_ref[g]
        eim = eim_ref[g]
        bt_ref[g, 0:nst, :] = ere.T.astype(BF16)
        bt_ref[g, nst:2 * nst, :] = eim.T.astype(BF16)
        e_hi, e_lo = _split3(jnp.concatenate([ere, eim], axis=1))
        rhs = jnp.concatenate([e_hi, e_hi, e_lo], axis=1)
        crg = cr[g]
        cig = ci[g]

        def lag_kernels(sel):
            c2 = jnp.concatenate([jnp.where(sel, crg, 0.0), jnp.where(sel, -cig, 0.0)], axis=1)
            c_hi, c_lo = _split3(c2)
            lhs = jnp.concatenate([c_hi, c_lo, c_hi], axis=1)
            return lax.dot_general(lhs, rhs, nt, preferred_element_type=F32)

        kf = lag_kernels(fwd2)
        kb = lag_kernels(jnp.logical_not(fwd2))
        dg = d_ref[g]
        for j in range(CH):
            sf = (kk - (CH - 1 - j) * S5_H) % kk
            rf = pltpu.roll(kf, sf, 1) if sf else kf
            rb = pltpu.roll(kb, j * S5_H, 1) if j else kb
            blk = (jnp.where(iblk <= j, rf, 0.0) + jnp.where(iblk >= j, rb, 0.0)
                   + jnp.where(lane == j * S5_H + hrow, dg, 0.0))
            mt_ref[g, j * S5_H:(j + 1) * S5_H, :] = blk.astype(BF16)


def _s5_tables(lam_re, lam_im, log_step, b_re, b_im, c_re, c_im, d_skip):
    n = lam_re.shape[0] * S5_G
    gb = 8
    nst = 2 * S5_P
    kk = CH * S5_H

    def fb(v):
        return jnp.concatenate([v[:, 0], v[:, 1]], axis=-1).reshape(n, v.shape[3], nst)

    lr = fb(lam_re[:, :, :, None, :])
    li = fb(lam_im[:, :, :, None, :])
    ls = fb(jnp.broadcast_to(log_step[:, :, :, None, None], log_step.shape + (1, S5_P)))
    btr = fb(jnp.swapaxes(b_re, -1, -2))
    bti = fb(jnp.swapaxes(b_im, -1, -2))
    cr = fb(c_re)
    ci = fb(c_im)
    d = d_skip.reshape(n, S5_H, 1)

    def spec(r, c):
        return pl.BlockSpec((gb, r, c), lambda g: (g, 0, 0))

    return pl.pallas_call(
        functools.partial(_s5_tables_kernel, gb=gb),
        grid=(n // gb,),
        in_specs=[spec(1, nst), spec(1, nst), spec(1, nst), spec(S5_H, nst), spec(S5_H, nst),
                  spec(S5_H, nst), spec(S5_H, nst), spec(S5_H, 1)],
        out_specs=[spec(kk, kk), spec(2 * nst, kk), spec(kk, 2 * nst), spec(2, nst)],
        out_shape=[jax.ShapeDtypeStruct((n, kk, kk), BF16),
                   jax.ShapeDtypeStruct((n, 2 * nst, kk), BF16),
                   jax.ShapeDtypeStruct((n, kk, 2 * nst), BF16),
                   jax.ShapeDtypeStruct((n, 2, nst), F32)],
        scratch_shapes=[pltpu.VMEM((gb, kk, nst), F32)] * 2,
        compiler_params=_cparams(("arbitrary",)),
    )(lr, li, ls, btr, bti, cr, ci, d)


def _even_in_kernel(*refs, full):
    it = iter(refs)
    x_ref = next(it)
    shift_ref, scale_ref, g_ref, wat_ref = next(it), next(it), next(it), next(it)
    if full:
        wbz_ref, wcs_ref = next(it), next(it)
    ut_ref = next(it)
    if full:
        xw_ref, z_ref = next(it), next(it)
        scr = [next(it) for _ in range(4)]

    x = x_ref[...]
    h = _mod_norm(x, g_ref[...], scale_ref[...], shift_ref[...])
    nc, nb, _ = x.shape
    rows = nc * nb
    hb = h.reshape(rows, D).astype(BF16)
    pt = lax.dot_general(wat_ref[...], hb, (((1,), (1,)), ((), ())), preferred_element_type=F32)
    ut_ref[...] = pt.astype(BF16).reshape(S5_G, S5_H, rows)
    if full:
        p = jnp.dot(hb, wbz_ref[...], preferred_element_type=F32)
        z_ref[...] = _silu(p[:, FN_W:]).astype(BF16)
        xw = jnp.dot(p[:, :FN_W].astype(BF16), wcs_ref[...], preferred_element_type=F32)
        for q in range(4):
            scr[q][...] = xw[:, q * 128:(q + 1) * 128]
        for q in range(4):
            part, half = divmod(q, 2)
            for bi in range(nb):
                piece = scr[q][pl.ds(bi, nc, stride=nb), :]
                lo = bi * FN_W + half * 128
                xw_ref[part, :, lo:lo + 128] = piece.astype(BF16)


def _even_in(xs, shift, scale, g, w_in, wcs, *, full):
    nc, bsz, _ = xs.shape
    rows = nc * bsz
    wat = w_in[:, :S5_W].T.astype(BF16)
    args = [xs]
    specs = [pl.BlockSpec((nc, bsz, D), lambda i: (0, 0, i))]
    args += [shift, scale, g.reshape(1, D), wat]
    specs += [pl.BlockSpec((1, bsz, D), lambda i: (0, 0, 0)),
              pl.BlockSpec((1, bsz, D), lambda i: (0, 0, 0)),
              pl.BlockSpec((1, D), lambda i: (0, 0)),
              pl.BlockSpec((S5_W, D), lambda i: (0, 0))]
    out_shape = [jax.ShapeDtypeStruct((S5_G, CH * S5_H, rows), BF16)]
    out_specs = [pl.BlockSpec((S5_G, S5_H, rows), lambda i: (0, i, 0))]
    scratch = []
    if full:
        args += [w_in[:, S5_W:].astype(BF16), wcs]
        specs += [pl.BlockSpec((D, FN_W + MIX), lambda i: (0, 0)),
                  pl.BlockSpec((FN_W, 2 * FN_W), lambda i: (0, 0))]
        out_shape += [jax.ShapeDtypeStruct((2, CH, nc, bsz * FN_W), BF16),
                      jax.ShapeDtypeStruct((CH, rows, MIX), BF16)]
        out_specs += [pl.BlockSpec((2, None, nc, bsz * FN_W), lambda i: (0, i, 0, 0)),
                      pl.BlockSpec((None, rows, MIX), lambda i: (i, 0, 0))]
        scratch = [pltpu.VMEM((rows, 128), F32)] * 4
    return pl.pallas_call(
        functools.partial(_even_in_kernel, full=full),
        grid=(CH,),
        in_specs=specs,
        out_specs=out_specs,
        out_shape=out_shape,
        scratch_shapes=scratch,
        compiler_params=_cparams(("arbitrary",)),
    )(*args)


def _s5_kernel(utl_ref, utc_ref, mt_ref, bt_ref, cp_ref, l16_ref, ytl_ref, ytc_ref,
               sre_ref, sim_ref, are_ref, aim_ref, bre_ref, bim_ref, *, bsz, ncl, ncc, gs):
    nl = bsz * ncl
    nx = bsz * ncc
    nst = 2 * S5_P
    for g in range(gs):
        bt = bt_ref[g]
        sl = jnp.dot(bt, utl_ref[g], preferred_element_type=F32).T
        sc = jnp.dot(bt, utc_ref[g], preferred_element_type=F32).T
        sre_ref[g, 0:nl, :] = sl[:, :nst]
        sim_ref[g, 0:nl, :] = sl[:, nst:]
        sre_ref[g, nl:nl + nx, :] = sc[:, :nst]
        sim_ref[g, nl:nl + nx, :] = sc[:, nst:]

    lam = [l16_ref[g] for g in range(gs)]
    is_fwd = lax.broadcasted_iota(jnp.int32, (bsz, nst), 1) < S5_P

    def make_step(base, nchunk):
        def step(c, carry):
            rf = pl.ds(pl.multiple_of(base + c * bsz, bsz), bsz)
            rb = pl.ds(pl.multiple_of(base + (nchunk - 1 - c) * bsz, bsz), bsz)
            out = []
            for g in range(gs):
                sre, sim = carry[2 * g], carry[2 * g + 1]
                lre = lam[g][0:1, :]
                lim = lam[g][1:2, :]
                are_ref[g, rf, :] = sre
                aim_ref[g, rf, :] = sim
                bre_ref[g, rb, :] = sre
                bim_ref[g, rb, :] = sim
                in_re = jnp.where(is_fwd, sre_ref[g, rf, :], sre_ref[g, rb, :])
                in_im = jnp.where(is_fwd, sim_ref[g, rf, :], sim_ref[g, rb, :])
                out.append(lre * sre - lim * sim + in_re)
                out.append(lre * sim + lim * sre + in_im)
            return tuple(out)
        return step

    zero = jnp.zeros((bsz, nst), F32)
    carry = lax.fori_loop(0, ncc, make_step(nl, ncc), (zero,) * (2 * gs))
    lax.fori_loop(0, ncl, make_step(0, ncl), carry)

    nt = (((1,), (1,)), ((), ()))
    fwd_rows = lax.broadcasted_iota(jnp.int32, (nl + nx, nst), 1) < S5_P
    for g in range(gs):
        mt = mt_ref[g]
        cp = cp_ref[g]
        s0 = jnp.concatenate([jnp.where(fwd_rows, are_ref[g], bre_ref[g]),
                              jnp.where(fwd_rows, aim_ref[g], bim_ref[g])], axis=1).astype(BF16)
        ytl_ref[g] = (jnp.dot(mt, utl_ref[g], preferred_element_type=F32)
                      + lax.dot_general(cp, s0[0:nl], nt, preferred_element_type=F32)).astype(BF16)
        ytc_ref[g] = (jnp.dot(mt, utc_ref[g], preferred_element_type=F32)
                      + lax.dot_general(cp, s0[nl:nl + nx], nt, preferred_element_type=F32)).astype(BF16)


def _s5_mix(utl, utc, mt, bend_t, cp, lam16, bsz, layer):
    nl = utl.shape[2]
    nx = utc.shape[2]
    kk = CH * S5_H
    nst = 2 * S5_P
    gs = 4
    off = layer * (S5_G // gs)

    def gspec(r, c):
        return pl.BlockSpec((gs, r, c), lambda g: (g, 0, 0))

    def tspec(r, c):
        return pl.BlockSpec((gs, r, c), lambda g: (g + off, 0, 0))

    return pl.pallas_call(
        functools.partial(_s5_kernel, bsz=bsz, ncl=nl // bsz, ncc=nx // bsz, gs=gs),
        grid=(S5_G // gs,),
        in_specs=[gspec(kk, nl), gspec(kk, nx), tspec(kk, kk), tspec(2 * nst, kk), tspec(kk, 2 * nst),
                  tspec(2, nst)],
        out_specs=[gspec(kk, nl), gspec(kk, nx)],
        out_shape=[jax.ShapeDtypeStruct((S5_G, kk, nl), BF16),
                   jax.ShapeDtypeStruct((S5_G, kk, nx), BF16)],
        scratch_shapes=[pltpu.VMEM((gs, nl + nx, nst), F32)] * 6,
        compiler_params=_cparams(("arbitrary",)),
    )(utl, utc, mt, bend_t, cp, lam16)


def _fnet_weights_kernel(ccs_ref, fw_ref, o_ref):
    fw = fw_ref[...]
    ccs = ccs_ref[...]
    wc = jnp.dot(ccs[:, :FN_W], fw, precision=HI, preferred_element_type=F32)
    ws = jnp.dot(ccs[:, FN_W:], fw, precision=HI, preferred_element_type=F32)
    o_ref[...] = jnp.concatenate([wc, ws], axis=1).astype(BF16)


def _fnet_weights(fw_bd):
    return pl.pallas_call(
        _fnet_weights_kernel,
        out_shape=jax.ShapeDtypeStruct((FN_W, 2 * FN_W), BF16),
    )(_dft_channel_matrix(), fw_bd)


def _fnet_kernel(cm_ref, v_ref, o_ref, cmb_ref):
    @pl.when(pl.program_id(1) == 0)
    def _():
        cmb_ref[...] = cm_ref[...].astype(BF16)

    v = v_ref[...]
    v2 = v.reshape(v.shape[0] * v.shape[1] * v.shape[2], v.shape[3])
    y = jnp.dot(cmb_ref[...], v2, preferred_element_type=F32)
    o_ref[...] = y.reshape(o_ref.shape).astype(BF16)


def _dft_time_matrix(nc):
    t = nc * CH
    order = (np.arange(nc)[None, :] * CH + np.arange(CH)[:, None]).reshape(-1)
    prod = (order[:, None].astype(np.int64) * order[None, :].astype(np.int64)) % t
    ang = prod.astype(np.float64) * (2.0 * np.pi / t)
    scale = 1.0 / math.sqrt(t * FN_GW)
    return jnp.asarray(np.concatenate([np.cos(ang), -np.sin(ang)], axis=1) * scale, dtype=F32)


def _dft_channel_matrix():
    c = np.arange(FN_GW)
    ang = (c[:, None] * c[None, :] % FN_GW).astype(np.float64) * (2.0 * np.pi / FN_GW)
    eye = np.eye(FN_G)
    return jnp.asarray(np.concatenate([np.kron(eye, np.cos(ang)), np.kron(eye, np.sin(ang))], axis=1),
                       dtype=F32)


def _fnet(xw, bsz):
    nc = xw.shape[2]
    t = nc * CH
    rt = min(t, 512)
    ipt = rt // nc
    bpb = 4
    cm = _dft_time_matrix(nc)
    return pl.pallas_call(
        _fnet_kernel,
        grid=(t // rt, bsz // bpb),
        in_specs=[pl.BlockSpec((rt, 2 * t), lambda r, b: (r, 0)),
                  pl.BlockSpec((2, CH, nc, bpb * FN_W), lambda r, b: (0, 0, 0, b))],
        out_specs=pl.BlockSpec((ipt, nc, bpb * FN_W), lambda r, b: (r, 0, b)),
        out_shape=jax.ShapeDtypeStruct((CH, nc, bsz * FN_W), BF16),
        scratch_shapes=[pltpu.VMEM((rt, 2 * t), BF16)],
        compiler_params=_cparams(("arbitrary", "arbitrary")),
    )(cm, xw)


def _even_out_kernel(yt_ref, yb_ref, z_ref, x_ref, gate_ref, gw_ref, gb_ref, woa_ref, wob_ref, o_ref,
                     scr0_ref, scr1_ref):
    scr = [scr0_ref, scr1_ref]
    yt = yt_ref[...]
    rows = yt.shape[2]
    ya = _gelu_tanh(yt.astype(F32).reshape(S5_W, rows).T)
    glu = jnp.dot(ya.astype(BF16), gw_ref[...], preferred_element_type=F32) + gb_ref[...]
    ya = ya * _sigmoid(glu)
    x = x_ref[...]
    nc, nb, _ = x.shape
    for half in range(2):
        for bi in range(nb):
            lo = bi * FN_W + half * 128
            scr[half][pl.ds(bi, nc, stride=nb), :] = yb_ref[:, lo:lo + 128].astype(F32)
    yb = jnp.concatenate([scr[0][...], scr[1][...]], axis=1)
    sz = z_ref[...]
    ma = (ya * sz[:, :S5_W]).astype(BF16)
    mb = (yb * sz[:, S5_W:]).astype(BF16)
    out = (jnp.dot(ma, woa_ref[...], preferred_element_type=F32)
           + jnp.dot(mb, wob_ref[...], preferred_element_type=F32))
    o_ref[...] = x + gate_ref[...] * out.reshape(x.shape)


def _even_out(yt, yb, z, xs, gate, glu_w, glu_b, w_out):
    nc, bsz, _ = xs.shape
    rows = nc * bsz
    args = [yt, yb, z, xs]
    specs = [pl.BlockSpec((S5_G, S5_H, rows), lambda j: (0, j, 0)),
             pl.BlockSpec((None, nc, bsz * FN_W), lambda j: (j, 0, 0)),
             pl.BlockSpec((None, rows, MIX), lambda j: (j, 0, 0)),
             pl.BlockSpec((nc, bsz, D), lambda j: (0, 0, j))]
    args += [gate, glu_w.astype(BF16), glu_b.reshape(1, S5_W), w_out[:S5_W].astype(BF16),
             w_out[S5_W:].astype(BF16)]
    specs += [pl.BlockSpec((1, bsz, D), lambda j: (0, 0, 0)),
              pl.BlockSpec((S5_W, S5_W), lambda j: (0, 0)),
              pl.BlockSpec((1, S5_W), lambda j: (0, 0)),
              pl.BlockSpec((S5_W, D), lambda j: (0, 0)),
              pl.BlockSpec((FN_W, D), lambda j: (0, 0))]
    return pl.pallas_call(
        _even_out_kernel,
        grid=(CH,),
        in_specs=specs,
        out_specs=pl.BlockSpec((nc, bsz, D), lambda j: (0, 0, j)),
        out_shape=jax.ShapeDtypeStruct((nc, bsz, CH * D), F32),
        scratch_shapes=[pltpu.VMEM((rows, 128), F32)] * 2,
        compiler_params=_cparams(("arbitrary",)),
    )(*args)


def _odd_kernel(*refs, ct, t_total, final):
    it = iter(refs)
    xm_ref, xp_ref, xn_ref = next(it), next(it), next(it)
    shift_ref, scale_ref, gate_ref, g_ref = next(it), next(it), next(it), next(it)
    w1_ref, w2_ref, wo_ref, pw_ref, ps_ref, cw_ref = (next(it), next(it), next(it), next(it),
                                                      next(it), next(it))
    fg_ref = next(it) if final else None
    o_ref = next(it)
    h_ref = next(it)
    pe_ref = next(it)
    fin_ref = next(it) if final else None

    ti = pl.program_id(0)
    nb = xm_ref.shape[1]
    tm = ct * CH
    ne = tm + 2 * HALO

    def hn(xv):
        return _mod_norm(xv, g_ref[...], scale_ref[...], shift_ref[...])

    for i in range(CH):
        hi = hn(xm_ref[:, :, i * D:(i + 1) * D])
        if i < HALO:
            h_ref[0:ct, i + HALO] = hi
        else:
            h_ref[1:ct + 1, i - HALO] = hi
    for i in range(HALO):
        h_ref[0, i] = hn(xp_ref[:, :, (i + HALO) * D:(i + HALO + 1) * D])[0]
        h_ref[ct, i + HALO] = hn(xn_ref[:, :, i * D:(i + 1) * D])[0]

    he = h_ref[...].reshape(ne * nb, D).astype(BF16)
    pe = jnp.dot(he, w1_ref[...], preferred_element_type=F32)
    te = ti * tm - HALO + lax.broadcasted_iota(jnp.int32, (ne * nb, 1), 0) // nb
    valid = jnp.logical_and(te >= 0, te < t_total)
    pe_ref[:, 0:POOL_W] = jnp.where(valid, pe[:, :POOL_W], 0.0)
    pe_ref[:, POOL_W:POOL_W + CONV_W] = jnp.where(
        valid, pe[:, POOL_W:POOL_W + CONV_W] * pe[:, POOL_W + CONV_W:], 0.0)

    m0 = HALO * nb
    mr = tm * nb
    p2 = jnp.dot(he[m0:m0 + mr], w2_ref[...], preferred_element_type=F32)
    b_gate = p2[:, :CONV_W]
    sz = _silu(p2[:, CONV_W:])

    tpos = ti * tm + lax.broadcasted_iota(jnp.int32, (mr, 1), 0) // nb
    pooled = []
    for gi, w in enumerate(POOL_WINDOWS):
        c0 = gi * POOL_GW
        s = pe_ref[:, c0:c0 + POOL_GW]
        n = ne
        width = 1
        while width < w:
            s = s[0:(n - width) * nb] + s[width * nb:n * nb]
            n -= width
            width *= 2
        start = (HALO - w // 2) * nb
        total = s[start:start + mr]
        hi = jnp.minimum(tpos + w // 2, t_total)
        lo = jnp.maximum(tpos - w // 2, 0)
        cnt = (hi - lo).astype(F32)
        centre = pe_ref[m0:m0 + mr, c0:c0 + POOL_GW]
        pg = total / cnt - centre
        pooled.append(jnp.dot(pg.astype(BF16), pw_ref[gi], preferred_element_type=F32))
    y_c = jnp.concatenate(pooled, axis=1) * ps_ref[...]

    cwt = cw_ref[...]
    vm = pe_ref[m0 - nb:m0 - nb + mr, POOL_W:POOL_W + CONV_W]
    v0 = pe_ref[m0:m0 + mr, POOL_W:POOL_W + CONV_W]
    vp = pe_ref[m0 + nb:m0 + nb + mr, POOL_W:POOL_W + CONV_W]
    y_d = b_gate * (vm * cwt[0:1, :] + v0 * cwt[1:2, :] + vp * cwt[2:3, :])

    y = (jnp.concatenate([y_c, y_d], axis=1) * sz).astype(BF16)
    out = jnp.dot(y, wo_ref[...], preferred_element_type=F32)
    go = (gate_ref[...] * out.reshape(tm, nb, D)).reshape(ct, CH, nb, D)
    for i in range(CH):
        xo = xm_ref[:, :, i * D:(i + 1) * D] + go[:, i]
        if not final:
            o_ref[:, :, i * D:(i + 1) * D] = xo
            continue
        ms = jnp.mean(xo * xo, axis=-1, keepdims=True)
        fin_ref[:, :, i * D:(i + 1) * D] = xo * lax.rsqrt(ms + EPS) * fg_ref[...]
    if final:
        o_ref[...] = pltpu.einshape("cb(id)->b(ci)d", fin_ref[...], i=CH)


def _odd_layer(xs, shift, scale, gate, g, w_in, w_out, pool_w, pool_scale, conv_w, final_g):
    nc, bsz, _ = xs.shape
    ct = 8
    n1 = POOL_W + 2 * CONV_W
    final = final_g is not None
    args = [xs, xs, xs, shift, scale, gate, g.reshape(1, D),
            w_in[:, :n1].astype(BF16), w_in[:, n1:].astype(BF16), w_out.astype(BF16),
            pool_w.astype(BF16), pool_scale.reshape(1, POOL_W), conv_w]
    vec = pl.BlockSpec((1, bsz, D), lambda i: (0, 0, 0))

    def const(*shape):
        return pl.BlockSpec(shape, lambda i: (0,) * len(shape), pipeline_mode=pl.Buffered(1))

    specs = [pl.BlockSpec((ct, bsz, CH * D), lambda i: (i, 0, 0)),
             pl.BlockSpec((1, bsz, CH * D), lambda i: (jnp.maximum(i * ct - 1, 0), 0, 0)),
             pl.BlockSpec((1, bsz, CH * D), lambda i: (jnp.minimum((i + 1) * ct, nc - 1), 0, 0)),
             vec, vec, vec,
             const(1, D), const(D, n1), const(D, CONV_W + MIX), const(MIX, D),
             const(len(POOL_WINDOWS), POOL_GW, POOL_GW), const(1, POOL_W), const(3, CONV_W)]
    scratch = [pltpu.VMEM((ct + 1, CH, bsz, D), F32),
               pltpu.VMEM(((ct + 1) * CH * bsz, POOL_W + CONV_W), F32)]
    if final:
        args.append(final_g.reshape(1, D))
        specs.append(const(1, D))
        scratch.append(pltpu.VMEM((ct, bsz, CH * D), F32))
        out_spec = pl.BlockSpec((bsz, ct * CH, D), lambda i: (0, i, 0))
        out_shape = jax.ShapeDtypeStruct((bsz, nc * CH, D), F32)
    else:
        out_spec = pl.BlockSpec((ct, bsz, CH * D), lambda i: (i, 0, 0))
        out_shape = jax.ShapeDtypeStruct((nc, bsz, CH * D), F32)
    return pl.pallas_call(
        functools.partial(_odd_kernel, ct=ct, t_total=nc * CH, final=final),
        grid=(nc // ct,),
        in_specs=specs,
        out_specs=out_spec,
        out_shape=out_shape,
        scratch_shapes=scratch,
        compiler_params=_cparams(("arbitrary",)),
    )(*args)


def _sincos_table(n_tok, dim):
    rows = n_tok // GRID_W
    rr, cc = np.meshgrid(np.arange(rows, dtype=np.float64), np.arange(GRID_W, dtype=np.float64),
                         indexing='ij')
    rr = rr.reshape(-1, 1)
    cc = cc.reshape(-1, 1)
    quarter = dim // 4
    omega = POS_BASE ** (-np.arange(quarter, dtype=np.float64) / quarter)
    tab = np.concatenate([np.sin(rr * omega), np.cos(rr * omega), np.sin(cc * omega), np.cos(cc * omega)],
                         axis=-1)
    return jnp.asarray(tab, dtype=F32)


def _block_diag(w):
    g, c, _ = w.shape
    eye = jnp.eye(g, dtype=w.dtype)
    return (eye[:, None, :, None] * w[:, :, None, :]).reshape(g * c, g * c)


def _to_stream_kernel(*refs, has_pos):
    it = iter(refs)
    x_ref = next(it)
    pos_ref = next(it) if has_pos else None
    o_ref = next(it)
    x = x_ref[...]
    if has_pos:
        x = x + pos_ref[...][None]
    o_ref[...] = pltpu.einshape("b(ci)d->cb(id)", x, i=CH)


def _to_stream(x, pos):
    bsz, t, _ = x.shape
    ct = 8
    tm = ct * CH
    args = [x]
    specs = [pl.BlockSpec((bsz, tm, D), lambda i: (0, i, 0))]
    if pos is not None:
        args.append(pos)
        specs.append(pl.BlockSpec((tm, D), lambda i: (i, 0)))
    return pl.pallas_call(
        functools.partial(_to_stream_kernel, has_pos=pos is not None),
        grid=(t // tm,),
        in_specs=specs,
        out_specs=pl.BlockSpec((ct, bsz, CH * D), lambda i: (i, 0, 0)),
        out_shape=jax.ShapeDtypeStruct((t // CH, bsz, CH * D), F32),
        compiler_params=_cparams(("arbitrary",)),
    )(*args)


def kernel(x, c, ctx, c_ctx, norm_g, ada_w, ada_b, even_w_in, even_w_out, s5_lam_re, s5_lam_im, s5_log_step, s5_b_re, s5_b_im, s5_c_re, s5_c_im, s5_d, s5_glu_w, s5_glu_b, fnet_w, odd_w_in, odd_w_out, pool_w, pool_scale, conv_w, final_g):
    bsz, n_tok, _ = x.shape
    depth = norm_g.shape[0]

    cond = jnp.concatenate([c, jnp.broadcast_to(c_ctx[None], (16 - bsz, D))], axis=0)
    ada = _ada_all(cond, ada_w, ada_b)

    def lat_mod(l, j):
        return ada[l, j, :bsz].reshape(1, bsz, D)

    def ctx_mod(l, j):
        return jnp.broadcast_to(ada[l, j, bsz].reshape(1, 1, D), (1, bsz, D))

    mt, bend_t, cp, lam16 = _s5_tables(s5_lam_re, s5_lam_im, s5_log_step, s5_b_re, s5_b_im, s5_c_re,
                                       s5_c_im, s5_d)

    need_ctx = [any(j % 2 == 0 for j in range(l + 1, depth)) for l in range(depth)]

    xl = _to_stream(x, _sincos_table(n_tok, D))
    xc = _to_stream(ctx, None)
    for l in range(depth):
        i = l // 2
        last = l == depth - 1
        if l % 2 == 0:
            wcs = _fnet_weights(_block_diag(fnet_w[i]))
            utl, xwl, zl = _even_in(xl, lat_mod(l, 0), lat_mod(l, 1), norm_g[l], even_w_in[i], wcs,
                                    full=True)
            if need_ctx[l]:
                utc, xwc, zc = _even_in(xc, ctx_mod(l, 0), ctx_mod(l, 1), norm_g[l], even_w_in[i], wcs,
                                        full=True)
            else:
                (utc,) = _even_in(xc, ctx_mod(l, 0), ctx_mod(l, 1), norm_g[l], even_w_in[i], None,
                                  full=False)
            ytl, ytc = _s5_mix(utl, utc, mt, bend_t, cp, lam16, bsz, i)
            ybl = _fnet(xwl, bsz)
            xl = _even_out(ytl, ybl, zl, xl, lat_mod(l, 2), s5_glu_w[i], s5_glu_b[i], even_w_out[i])
            if need_ctx[l]:
                ybc = _fnet(xwc, bsz)
                xc = _even_out(ytc, ybc, zc, xc, ctx_mod(l, 2), s5_glu_w[i], s5_glu_b[i], even_w_out[i])
        else:
            if need_ctx[l]:
                xc = _odd_layer(xc, ctx_mod(l, 0), ctx_mod(l, 1), ctx_mod(l, 2), norm_g[l],
                                odd_w_in[i], odd_w_out[i], pool_w[i], pool_scale[i], conv_w[i], None)
            xl = _odd_layer(xl, lat_mod(l, 0), lat_mod(l, 1), lat_mod(l, 2), norm_g[l], odd_w_in[i],
                            odd_w_out[i], pool_w[i], pool_scale[i], conv_w[i],
                            final_g if last else None)
    if depth % 2 == 1:
        raise NotImplementedError("final norm and (B, T, D) order are produced by the last (odd) layer")
    return xl
```

```python
import functools
import math

import numpy as np
import jax
import jax.numpy as jnp
from jax import lax
from jax.experimental import pallas as pl
from jax.experimental.pallas import tpu as pltpu

D = 1024
MIX = 1024
S5_W = 768
FN_W = 256
S5_H = 16
S5_G = 48
S5_P = 64
FN_G = 4
FN_GW = 64
POOL_W = 512
CONV_W = 512
POOL_WINDOWS = (2, 4, 8, 16)
POOL_GW = 128
GRID_W = 64
EPS = 1e-6
POS_BASE = 10000.0
CH = 16
HALO = 8
VMEM_LIMIT = 56 * 1024 * 1024

F32 = jnp.float32
BF16 = jnp.bfloat16
HI = lax.Precision.HIGHEST


def _cparams(sem):
    return pltpu.CompilerParams(dimension_semantics=sem, vmem_limit_bytes=VMEM_LIMIT)


def _sigmoid(v):
    return 0.5 * jnp.tanh(0.5 * v) + 0.5


def _silu(v):
    return v * _sigmoid(v)


def _gelu_tanh(v):
    c = math.sqrt(2.0 / math.pi)
    return 0.5 * v * (1.0 + jnp.tanh(c * (v + 0.044715 * (v * v * v))))


def _mod_norm(x, g, scale, shift):
    ms = jnp.mean(x * x, axis=-1, keepdims=True)
    y = x * lax.rsqrt(ms + EPS) * g
    return y * (1.0 + scale) + shift


def _ada_kernel(c_ref, w_ref, b_ref, o_ref):
    s = _silu(c_ref[...])
    o_ref[...] = jnp.dot(s, w_ref[...], precision=HI, preferred_element_type=F32) + b_ref[...]


def _ada_all(cond, ada_w, ada_b):
    depth = ada_w.shape[0]
    return pl.pallas_call(
        _ada_kernel,
        grid=(depth, 3),
        in_specs=[
            pl.BlockSpec((16, D), lambda l, j: (0, 0)),
            pl.BlockSpec((None, D, D), lambda l, j: (l, 0, j)),
            pl.BlockSpec((None, None, 1, D), lambda l, j: (l, j, 0, 0)),
        ],
        out_specs=pl.BlockSpec((None, None, 16, D), lambda l, j: (l, j, 0, 0)),
        out_shape=jax.ShapeDtypeStruct((depth, 3, 16, D), F32),
        compiler_params=_cparams(("arbitrary", "arbitrary")),
    )(cond, ada_w, ada_b.reshape(depth, 3, 1, D))


def _split3(v):
    hi = v.astype(BF16)
    lo = (v - hi.astype(F32)).astype(BF16)
    return hi, lo


def _s5_tables_kernel(lr_ref, li_ref, ls_ref, btr_ref, bti_ref, cr_ref, ci_ref, d_ref,
                      mt_ref, bt_ref, cp_ref, l16_ref, ere_ref, eim_ref, *, gb):
    nst = 2 * S5_P
    kk = CH * S5_H
    step = jnp.exp(ls_ref[...])
    lr = lr_ref[...]
    li = li_ref[...]
    a = lr * step
    b = li * step

    def powers(expo):
        mag = jnp.exp(expo * a)
        return mag * jnp.cos(expo * b), mag * jnp.sin(expo * b)

    row = lax.broadcasted_iota(jnp.int32, (1, CH, nst), 1).astype(F32)
    fwd = lax.broadcasted_iota(jnp.int32, (1, CH, nst), 2) < S5_P
    one = jnp.ones((1, 1, nst), F32)

    l1re, l1im = powers(one)
    n_re = l1re - 1.0
    den = lr * lr + li * li
    co_re = (n_re * lr + l1im * li) / den
    co_im = (l1im * lr - n_re * li) / den
    btr = btr_ref[...]
    bti = bti_ref[...]
    bb_re = co_re * btr - co_im * bti
    bb_im = co_re * bti + co_im * btr

    pe_re, pe_im = powers(jnp.where(fwd, (CH - 1) - row, row))
    for l in range(CH):
        pr = pe_re[:, l:l + 1, :]
        pi = pe_im[:, l:l + 1, :]
        ere_ref[:, l * S5_H:(l + 1) * S5_H, :] = pr * bb_re - pi * bb_im
        eim_ref[:, l * S5_H:(l + 1) * S5_H, :] = pr * bb_im + pi * bb_re

    cr = cr_ref[...]
    ci = ci_ref[...]
    pc_re, pc_im = powers(jnp.where(fwd, row + 1.0, CH - row))
    for j in range(CH):
        pr = pc_re[:, j:j + 1, :]
        pi = pc_im[:, j:j + 1, :]
        w_re = cr * pr - ci * pi
        w_im = cr * pi + ci * pr
        cp_ref[:, j * S5_H:(j + 1) * S5_H, :] = jnp.concatenate([w_re, -w_im], axis=2).astype(BF16)

    l16re, l16im = powers(one * float(CH))
    l16_ref[...] = jnp.concatenate([l16re, l16im], axis=1)

    fwd2 = lax.broadcasted_iota(jnp.int32, (S5_H, nst), 1) < S5_P
    lane = lax.broadcasted_iota(jnp.int32, (S5_H, kk), 1)
    iblk = lane // S5_H
    hrow = lax.broadcasted_iota(jnp.int32, (S5_H, kk), 0)
    nt = (((1,), (1,)), ((), ()))
    for g in range(gb):
        ere = ere_ref[g]
        eim = eim_ref[g]
        bt_ref[g, 0:nst, :] = ere.T.astype(BF16)
        bt_ref[g, nst:2 * nst, :] = eim.T.astype(BF16)
        e_hi, e_lo = _split3(jnp.concatenate([ere, eim], axis=1))
        rhs = jnp.concatenate([e_hi, e_hi, e_lo], axis=1)
        crg = cr[g]
        cig = ci[g]

        def lag_kernels(sel):
            c2 = jnp.concatenate([jnp.where(sel, crg, 0.0), jnp.where(sel, -cig, 0.0)], axis=1)
            c_hi, c_lo = _split3(c2)
            lhs = jnp.concatenate([c_hi, c_lo, c_hi], axis=1)
            return lax.dot_general(lhs, rhs, nt, preferred_element_type=F32)

        kf = lag_kernels(fwd2)
        kb = lag_kernels(jnp.logical_not(fwd2))
        dg = d_ref[g]
        for j in range(CH):
            sf = (kk - (CH - 1 - j) * S5_H) % kk
            rf = pltpu.roll(kf, sf, 1) if sf else kf
            rb = pltpu.roll(kb, j * S5_H, 1) if j else kb
            blk = (jnp.where(iblk <= j, rf, 0.0) + jnp.where(iblk >= j, rb, 0.0)
                   + jnp.where(lane == j * S5_H + hrow, dg, 0.0))
            mt_ref[g, j * S5_H:(j + 1) * S5_H, :] = blk.astype(BF16)


def _s5_tables(lam_re, lam_im, log_step, b_re, b_im, c_re, c_im, d_skip):
    n = lam_re.shape[0] * S5_G
    gb = 8
    nst = 2 * S5_P
    kk = CH * S5_H

    def fb(v):
        return jnp.concatenate([v[:, 0], v[:, 1]], axis=-1).reshape(n, v.shape[3], nst)

    lr = fb(lam_re[:, :, :, None, :])
    li = fb(lam_im[:, :, :, None, :])
    ls = fb(jnp.broadcast_to(log_step[:, :, :, None, None], log_step.shape + (1, S5_P)))
    btr = fb(jnp.swapaxes(b_re, -1, -2))
    bti = fb(jnp.swapaxes(b_im, -1, -2))
    cr = fb(c_re)
    ci = fb(c_im)
    d = d_skip.reshape(n, S5_H, 1)

    def spec(r, c):
        return pl.BlockSpec((gb, r, c), lambda g: (g, 0, 0))

    return pl.pallas_call(
        functools.partial(_s5_tables_kernel, gb=gb),
        grid=(n // gb,),
        in_specs=[spec(1, nst), spec(1, nst), spec(1, nst), spec(S5_H, nst), spec(S5_H, nst),
                  spec(S5_H, nst), spec(S5_H, nst), spec(S5_H, 1)],
        out_specs=[spec(kk, kk), spec(2 * nst, kk), spec(kk, 2 * nst), spec(2, nst)],
        out_shape=[jax.ShapeDtypeStruct((n, kk, kk), BF16),
                   jax.ShapeDtypeStruct((n, 2 * nst, kk), BF16),
                   jax.ShapeDtypeStruct((n, kk, 2 * nst), BF16),
                   jax.ShapeDtypeStruct((n, 2, nst), F32)],
        scratch_shapes=[pltpu.VMEM((gb, kk, nst), F32)] * 2,
        compiler_params=_cparams(("arbitrary",)),
    )(lr, li, ls, btr, bti, cr, ci, d)


def _even_in_kernel(*refs, full):
    it = iter(refs)
    x_ref = next(it)
    shift_ref, scale_ref, g_ref, w_ref = next(it), next(it), next(it), next(it)
    wcs_ref = next(it) if full else None
    ut_ref = next(it)
    if full:
        xw_ref, z_ref = next(it), next(it)
    wat_ref = next(it)
    if full:
        wbz_ref = next(it)
        scr = [next(it) for _ in range(4)]

    @pl.when(pl.program_id(0) == 0)
    def _():
        wat_ref[...] = w_ref[:, :S5_W].T.astype(BF16)
        if full:
            wbz_ref[...] = w_ref[:, S5_W:].astype(BF16)

    x = x_ref[...]
    h = _mod_norm(x, g_ref[...], scale_ref[...], shift_ref[...])
    nc, nb, _ = x.shape
    rows = nc * nb
    hb = h.reshape(rows, D).astype(BF16)
    pt = lax.dot_general(wat_ref[...], hb, (((1,), (1,)), ((), ())), preferred_element_type=F32)
    ut_ref[...] = pt.astype(BF16).reshape(S5_G, S5_H, rows)
    if full:
        p = jnp.dot(hb, wbz_ref[...], preferred_element_type=F32)
        z_ref[...] = _silu(p[:, FN_W:]).astype(BF16)
        xw = jnp.dot(p[:, :FN_W].astype(BF16), wcs_ref[...], preferred_element_type=F32)
        for q in range(4):
            scr[q][...] = xw[:, q * 128:(q + 1) * 128]
        for q in range(4):
            part, half = divmod(q, 2)
            for bi in range(nb):
                piece = scr[q][pl.ds(bi, nc, stride=nb), :]
                lo = bi * FN_W + half * 128
                xw_ref[part, :, lo:lo + 128] = piece.astype(BF16)


def _even_in(xs, shift, scale, g, w_in, wcs, *, full):
    nc, bsz, _ = xs.shape
    rows = nc * bsz
    ncol = w_in.shape[1] if full else S5_W
    args = [xs]
    specs = [pl.BlockSpec((nc, bsz, D), lambda i: (0, 0, i))]
    args += [shift, scale, g.reshape(1, D), w_in]
    specs += [pl.BlockSpec((1, bsz, D), lambda i: (0, 0, 0)),
              pl.BlockSpec((1, bsz, D), lambda i: (0, 0, 0)),
              pl.BlockSpec((1, D), lambda i: (0, 0)),
              pl.BlockSpec((D, ncol), lambda i: (0, 0), pipeline_mode=pl.Buffered(1))]
    out_shape = [jax.ShapeDtypeStruct((S5_G, CH * S5_H, rows), BF16)]
    out_specs = [pl.BlockSpec((S5_G, S5_H, rows), lambda i: (0, i, 0))]
    scratch = [pltpu.VMEM((S5_W, D), BF16)]
    if full:
        args += [wcs]
        specs += [pl.BlockSpec((FN_W, 2 * FN_W), lambda i: (0, 0))]
        out_shape += [jax.ShapeDtypeStruct((2, CH, nc, bsz * FN_W), BF16),
                      jax.ShapeDtypeStruct((CH, rows, MIX), BF16)]
        out_specs += [pl.BlockSpec((2, None, nc, bsz * FN_W), lambda i: (0, i, 0, 0)),
                      pl.BlockSpec((None, rows, MIX), lambda i: (i, 0, 0))]
        scratch += [pltpu.VMEM((D, FN_W + MIX), BF16)] + [pltpu.VMEM((rows, 128), F32)] * 4
    return pl.pallas_call(
        functools.partial(_even_in_kernel, full=full),
        grid=(CH,),
        in_specs=specs,
        out_specs=out_specs,
        out_shape=out_shape,
        scratch_shapes=scratch,
        compiler_params=_cparams(("arbitrary",)),
    )(*args)


def _s5_kernel(utl_ref, utc_ref, mt_ref, bt_ref, cp_ref, l16_ref, ytl_ref, ytc_ref,
               sre_ref, sim_ref, are_ref, aim_ref, bre_ref, bim_ref, *, bsz, ncl, ncc, gs):
    nl = bsz * ncl
    nx = bsz * ncc
    nst = 2 * S5_P
    for g in range(gs):
        bt = bt_ref[g]
        sl = jnp.dot(bt, utl_ref[g], preferred_element_type=F32).T
        sc = jnp.dot(bt, utc_ref[g], preferred_element_type=F32).T
        sre_ref[g, 0:nl, :] = sl[:, :nst]
        sim_ref[g, 0:nl, :] = sl[:, nst:]
        sre_ref[g, nl:nl + nx, :] = sc[:, :nst]
        sim_ref[g, nl:nl + nx, :] = sc[:, nst:]

    lam = [l16_ref[g] for g in range(gs)]
    is_fwd = lax.broadcasted_iota(jnp.int32, (bsz, nst), 1) < S5_P

    def make_step(base, nchunk):
        def step(c, carry):
            rf = pl.ds(pl.multiple_of(base + c * bsz, bsz), bsz)
            rb = pl.ds(pl.multiple_of(base + (nchunk - 1 - c) * bsz, bsz), bsz)
            out = []
            for g in range(gs):
                sre, sim = carry[2 * g], carry[2 * g + 1]
                lre = lam[g][0:1, :]
                lim = lam[g][1:2, :]
                are_ref[g, rf, :] = sre
                aim_ref[g, rf, :] = sim
                bre_ref[g, rb, :] = sre
                bim_ref[g, rb, :] = sim
                in_re = jnp.where(is_fwd, sre_ref[g, rf, :], sre_ref[g, rb, :])
                in_im = jnp.where(is_fwd, sim_ref[g, rf, :], sim_ref[g, rb, :])
                out.append(lre * sre - lim * sim + in_re)
                out.append(lre * sim + lim * sre + in_im)
            return tuple(out)
        return step

    zero = jnp.zeros((bsz, nst), F32)
    carry = lax.fori_loop(0, ncc, make_step(nl, ncc), (zero,) * (2 * gs))
    lax.fori_loop(0, ncl, make_step(0, ncl), carry)

    nt = (((1,), (1,)), ((), ()))
    fwd_rows = lax.broadcasted_iota(jnp.int32, (nl + nx, nst), 1) < S5_P
    for g in range(gs):
        mt = mt_ref[g]
        cp = cp_ref[g]
        s0 = jnp.concatenate([jnp.where(fwd_rows, are_ref[g], bre_ref[g]),
                              jnp.where(fwd_rows, aim_ref[g], bim_ref[g])], axis=1).astype(BF16)
        ytl_ref[g] = (jnp.dot(mt, utl_ref[g], preferred_element_type=F32)
                      + lax.dot_general(cp, s0[0:nl], nt, preferred_element_type=F32)).astype(BF16)
        ytc_ref[g] = (jnp.dot(mt, utc_ref[g], preferred_element_type=F32)
                      + lax.dot_general(cp, s0[nl:nl + nx], nt, preferred_element_type=F32)).astype(BF16)


def _s5_mix(utl, utc, mt, bend_t, cp, lam16, bsz, layer):
    nl = utl.shape[2]
    nx = utc.shape[2]
    kk = CH * S5_H
    nst = 2 * S5_P
    gs = 4
    off = layer * (S5_G // gs)

    def gspec(r, c):
        return pl.BlockSpec((gs, r, c), lambda g: (g, 0, 0))

    def tspec(r, c):
        return pl.BlockSpec((gs, r, c), lambda g: (g + off, 0, 0))

    return pl.pallas_call(
        functools.partial(_s5_kernel, bsz=bsz, ncl=nl // bsz, ncc=nx // bsz, gs=gs),
        grid=(S5_G // gs,),
        in_specs=[gspec(kk, nl), gspec(kk, nx), tspec(kk, kk), tspec(2 * nst, kk), tspec(kk, 2 * nst),
                  tspec(2, nst)],
        out_specs=[gspec(kk, nl), gspec(kk, nx)],
        out_shape=[jax.ShapeDtypeStruct((S5_G, kk, nl), BF16),
                   jax.ShapeDtypeStruct((S5_G, kk, nx), BF16)],
        scratch_shapes=[pltpu.VMEM((gs, nl + nx, nst), F32)] * 6,
        compiler_params=_cparams(("arbitrary",)),
    )(utl, utc, mt, bend_t, cp, lam16)


def _fnet_weights_kernel(ccs_ref, fw_ref, o_ref):
    fw = fw_ref[...]
    ccs = ccs_ref[...]
    wc = jnp.dot(ccs[:, :FN_W], fw, precision=HI, preferred_element_type=F32)
    ws = jnp.dot(ccs[:, FN_W:], fw, precision=HI, preferred_element_type=F32)
    o_ref[...] = jnp.concatenate([wc, ws], axis=1).astype(BF16)


def _fnet_weights(fw_bd):
    return pl.pallas_call(
        _fnet_weights_kernel,
        out_shape=jax.ShapeDtypeStruct((FN_W, 2 * FN_W), BF16),
    )(_dft_channel_matrix(), fw_bd)


def _fnet_kernel(cm_ref, v_ref, o_ref, cmb_ref):
    @pl.when(pl.program_id(1) == 0)
    def _():
        cmb_ref[...] = cm_ref[...].astype(BF16)

    v = v_ref[...]
    v2 = v.reshape(v.shape[0] * v.shape[1] * v.shape[2], v.shape[3])
    y = jnp.dot(cmb_ref[...], v2, preferred_element_type=F32)
    o_ref[...] = y.reshape(o_ref.shape).astype(BF16)


def _dft_time_matrix(nc):
    t = nc * CH
    order = (np.arange(nc)[None, :] * CH + np.arange(CH)[:, None]).reshape(-1)
    prod = (order[:, None].astype(np.int64) * order[None, :].astype(np.int64)) % t
    ang = prod.astype(np.float64) * (2.0 * np.pi / t)
    scale = 1.0 / math.sqrt(t * FN_GW)
    return jnp.asarray(np.concatenate([np.cos(ang), -np.sin(ang)], axis=1) * scale, dtype=F32)


def _dft_channel_matrix():
    c = np.arange(FN_GW)
    ang = (c[:, None] * c[None, :] % FN_GW).astype(np.float64) * (2.0 * np.pi / FN_GW)
    eye = np.eye(FN_G)
    return jnp.asarray(np.concatenate([np.kron(eye, np.cos(ang)), np.kron(eye, np.sin(ang))], axis=1),
                       dtype=F32)


def _fnet(xw, bsz):
    nc = xw.shape[2]
    t = nc * CH
    rt = min(t, 512)
    ipt = rt // nc
    bpb = 4
    cm = _dft_time_matrix(nc)
    return pl.pallas_call(
        _fnet_kernel,
        grid=(t // rt, bsz // bpb),
        in_specs=[pl.BlockSpec((rt, 2 * t), lambda r, b: (r, 0)),
                  pl.BlockSpec((2, CH, nc, bpb * FN_W), lambda r, b: (0, 0, 0, b))],
        out_specs=pl.BlockSpec((ipt, nc, bpb * FN_W), lambda r, b: (r, 0, b)),
        out_shape=jax.ShapeDtypeStruct((CH, nc, bsz * FN_W), BF16),
        scratch_shapes=[pltpu.VMEM((rt, 2 * t), BF16)],
        compiler_params=_cparams(("arbitrary", "arbitrary")),
    )(cm, xw)


def _even_out_kernel(yt_ref, yb_ref, z_ref, x_ref, gate_ref, gwf_ref, gb_ref, wof_ref, o_ref,
                     scr0_ref, scr1_ref, gw_ref, wo_ref):
    scr = [scr0_ref, scr1_ref]

    @pl.when(pl.program_id(0) == 0)
    def _():
        gw_ref[...] = gwf_ref[...].astype(BF16)
        wo_ref[...] = wof_ref[...].astype(BF16)

    yt = yt_ref[...]
    rows = yt.shape[2]
    ya = _gelu_tanh(yt.astype(F32).reshape(S5_W, rows).T)
    glu = jnp.dot(ya.astype(BF16), gw_ref[...], preferred_element_type=F32) + gb_ref[...]
    ya = ya * _sigmoid(glu)
    x = x_ref[...]
    nc, nb, _ = x.shape
    for half in range(2):
        for bi in range(nb):
            lo = bi * FN_W + half * 128
            scr[half][pl.ds(bi, nc, stride=nb), :] = yb_ref[:, lo:lo + 128].astype(F32)
    yb = jnp.concatenate([scr[0][...], scr[1][...]], axis=1)
    sz = z_ref[...]
    ma = (ya * sz[:, :S5_W]).astype(BF16)
    mb = (yb * sz[:, S5_W:]).astype(BF16)
    out = (jnp.dot(ma, wo_ref[0:S5_W, :], preferred_element_type=F32)
           + jnp.dot(mb, wo_ref[S5_W:MIX, :], preferred_element_type=F32))
    o_ref[...] = x + gate_ref[...] * out.reshape(x.shape)


def _even_out(yt, yb, z, xs, gate, glu_w, glu_b, w_out):
    nc, bsz, _ = xs.shape
    rows = nc * bsz
    args = [yt, yb, z, xs]
    specs = [pl.BlockSpec((S5_G, S5_H, rows), lambda j: (0, j, 0)),
             pl.BlockSpec((None, nc, bsz * FN_W), lambda j: (j, 0, 0)),
             pl.BlockSpec((None, rows, MIX), lambda j: (j, 0, 0)),
             pl.BlockSpec((nc, bsz, D), lambda j: (0, 0, j))]
    args += [gate, glu_w, glu_b.reshape(1, S5_W), w_out]
    specs += [pl.BlockSpec((1, bsz, D), lambda j: (0, 0, 0)),
              pl.BlockSpec((S5_W, S5_W), lambda j: (0, 0), pipeline_mode=pl.Buffered(1)),
              pl.BlockSpec((1, S5_W), lambda j: (0, 0)),
              pl.BlockSpec((MIX, D), lambda j: (0, 0), pipeline_mode=pl.Buffered(1))]
    return pl.pallas_call(
        _even_out_kernel,
        grid=(CH,),
        in_specs=specs,
        out_specs=pl.BlockSpec((nc, bsz, D), lambda j: (0, 0, j)),
        out_shape=jax.ShapeDtypeStruct((nc, bsz, CH * D), F32),
        scratch_shapes=[pltpu.VMEM((rows, 128), F32)] * 2 + [pltpu.VMEM((S5_W, S5_W), BF16),
                                                             pltpu.VMEM((MIX, D), BF16)],
        compiler_params=_cparams(("arbitrary",)),
    )(*args)


def _odd_kernel(*refs, ct, t_total, final):
    it = iter(refs)
    xm_ref, xp_ref, xn_ref = next(it), next(it), next(it)
    shift_ref, scale_ref, gate_ref, g_ref = next(it), next(it), next(it), next(it)
    wif_ref, wof_ref, pwf_ref, ps_ref, cw_ref = next(it), next(it), next(it), next(it), next(it)
    fg_ref = next(it) if final else None
    o_ref = next(it)
    h_ref = next(it)
    pe_ref = next(it)
    wi_ref, wo_ref, pw_ref = next(it), next(it), next(it)
    fin_ref = next(it) if final else None

    ti = pl.program_id(0)

    @pl.when(ti == 0)
    def _():
        wi_ref[...] = wif_ref[...].astype(BF16)
        wo_ref[...] = wof_ref[...].astype(BF16)
        pw_ref[...] = pwf_ref[...].astype(BF16)

    n1 = POOL_W + 2 * CONV_W
    nb = xm_ref.shape[1]
    tm = ct * CH
    ne = tm + 2 * HALO

    def hn(xv):
        return _mod_norm(xv, g_ref[...], scale_ref[...], shift_ref[...])

    for i in range(CH):
        hi = hn(xm_ref[:, :, i * D:(i + 1) * D])
        if i < HALO:
            h_ref[0:ct, i + HALO] = hi
        else:
            h_ref[1:ct + 1, i - HALO] = hi
    for i in range(HALO):
        h_ref[0, i] = hn(xp_ref[:, :, (i + HALO) * D:(i + HALO + 1) * D])[0]
        h_ref[ct, i + HALO] = hn(xn_ref[:, :, i * D:(i + 1) * D])[0]

    he = h_ref[...].reshape(ne * nb, D).astype(BF16)
    pe = jnp.dot(he, wi_ref[:, 0:n1], preferred_element_type=F32)
    te = ti * tm - HALO + lax.broadcasted_iota(jnp.int32, (ne * nb, 1), 0) // nb
    valid = jnp.logical_and(te >= 0, te < t_total)
    pe_ref[:, 0:POOL_W] = jnp.where(valid, pe[:, :POOL_W], 0.0)
    pe_ref[:, POOL_W:POOL_W + CONV_W] = jnp.where(
        valid, pe[:, POOL_W:POOL_W + CONV_W] * pe[:, POOL_W + CONV_W:], 0.0)

    m0 = HALO * nb
    mr = tm * nb
    p2 = jnp.dot(he[m0:m0 + mr], wi_ref[:, n1:], preferred_element_type=F32)
    b_gate = p2[:, :CONV_W]
    sz = _silu(p2[:, CONV_W:])

    tpos = ti * tm + lax.broadcasted_iota(jnp.int32, (mr, 1), 0) // nb
    pooled = []
    for gi, w in enumerate(POOL_WINDOWS):
        c0 = gi * POOL_GW
        s = pe_ref[:, c0:c0 + POOL_GW]
        n = ne
        width = 1
        while width < w:
            s = s[0:(n - width) * nb] + s[width * nb:n * nb]
            n -= width
            width *= 2
        start = (HALO - w // 2) * nb
        total = s[start:start + mr]
        hi = jnp.minimum(tpos + w // 2, t_total)
        lo = jnp.maximum(tpos - w // 2, 0)
        cnt = (hi - lo).astype(F32)
        centre = pe_ref[m0:m0 + mr, c0:c0 + POOL_GW]
        pg = total / cnt - centre
        pooled.append(jnp.dot(pg.astype(BF16), pw_ref[gi], preferred_element_type=F32))
    y_c = jnp.concatenate(pooled, axis=1) * ps_ref[...]

    cwt = cw_ref[...]
    vm = pe_ref[m0 - nb:m0 - nb + mr, POOL_W:POOL_W + CONV_W]
    v0 = pe_ref[m0:m0 + mr, POOL_W:POOL_W + CONV_W]
    vp = pe_ref[m0 + nb:m0 + nb + mr, POOL_W:POOL_W + CONV_W]
    y_d = b_gate * (vm * cwt[0:1, :] + v0 * cwt[1:2, :] + vp * cwt[2:3, :])

    y = (jnp.concatenate([y_c, y_d], axis=1) * sz).astype(BF16)
    out = jnp.dot(y, wo_ref[...], preferred_element_type=F32)
    go = (gate_ref[...] * out.reshape(tm, nb, D)).reshape(ct, CH, nb, D)
    for i in range(CH):
        xo = xm_ref[:, :, i * D:(i + 1) * D] + go[:, i]
        if not final:
            o_ref[:, :, i * D:(i + 1) * D] = xo
            continue
        ms = jnp.mean(xo * xo, axis=-1, keepdims=True)
        fin_ref[:, :, i * D:(i + 1) * D] = xo * lax.rsqrt(ms + EPS) * fg_ref[...]
    if final:
        o_ref[...] = pltpu.einshape("cb(id)->b(ci)d", fin_ref[...], i=CH)


def _odd_layer(xs, shift, scale, gate, g, w_in, w_out, pool_w, pool_scale, conv_w, final_g):
    nc, bsz, _ = xs.shape
    ct = 8
    final = final_g is not None
    args = [xs, xs, xs, shift, scale, gate, g.reshape(1, D), w_in, w_out, pool_w,
            pool_scale.reshape(1, POOL_W), conv_w]
    vec = pl.BlockSpec((1, bsz, D), lambda i: (0, 0, 0))

    def const(*shape):
        return pl.BlockSpec(shape, lambda i: (0,) * len(shape), pipeline_mode=pl.Buffered(1))

    specs = [pl.BlockSpec((ct, bsz, CH * D), lambda i: (i, 0, 0)),
             pl.BlockSpec((1, bsz, CH * D), lambda i: (jnp.maximum(i * ct - 1, 0), 0, 0)),
             pl.BlockSpec((1, bsz, CH * D), lambda i: (jnp.minimum((i + 1) * ct, nc - 1), 0, 0)),
             vec, vec, vec,
             const(1, D), const(*w_in.shape), const(MIX, D),
             const(len(POOL_WINDOWS), POOL_GW, POOL_GW), const(1, POOL_W), const(3, CONV_W)]
    scratch = [pltpu.VMEM((ct + 1, CH, bsz, D), F32),
               pltpu.VMEM(((ct + 1) * CH * bsz, POOL_W + CONV_W), F32),
               pltpu.VMEM(w_in.shape, BF16), pltpu.VMEM((MIX, D), BF16),
               pltpu.VMEM((len(POOL_WINDOWS), POOL_GW, POOL_GW), BF16)]
    if final:
        args.append(final_g.reshape(1, D))
        specs.append(const(1, D))
        scratch.append(pltpu.VMEM((ct, bsz, CH * D), F32))
        out_spec = pl.BlockSpec((bsz, ct * CH, D), lambda i: (0, i, 0))
        out_shape = jax.ShapeDtypeStruct((bsz, nc * CH, D), F32)
    else:
        out_spec = pl.BlockSpec((ct, bsz, CH * D), lambda i: (i, 0, 0))
        out_shape = jax.ShapeDtypeStruct((nc, bsz, CH * D), F32)
    return pl.pallas_call(
        functools.partial(_odd_kernel, ct=ct, t_total=nc * CH, final=final),
        grid=(nc // ct,),
        in_specs=specs,
        out_specs=out_spec,
        out_shape=out_shape,
        scratch_shapes=scratch,
        compiler_params=_cparams(("arbitrary",)),
    )(*args)


def _sincos_table(n_tok, dim):
    rows = n_tok // GRID_W
    rr, cc = np.meshgrid(np.arange(rows, dtype=np.float64), np.arange(GRID_W, dtype=np.float64),
                         indexing='ij')
    rr = rr.reshape(-1, 1)
    cc = cc.reshape(-1, 1)
    quarter = dim // 4
    omega = POS_BASE ** (-np.arange(quarter, dtype=np.float64) / quarter)
    tab = np.concatenate([np.sin(rr * omega), np.cos(rr * omega), np.sin(cc * omega), np.cos(cc * omega)],
                         axis=-1)
    return jnp.asarray(tab, dtype=F32)


def _block_diag(w):
    g, c, _ = w.shape
    eye = jnp.eye(g, dtype=w.dtype)
    return (eye[:, None, :, None] * w[:, :, None, :]).reshape(g * c, g * c)


def _to_stream_kernel(*refs, has_pos):
    it = iter(refs)
    x_ref = next(it)
    pos_ref = next(it) if has_pos else None
    o_ref = next(it)
    x = x_ref[...]
    if has_pos:
        x = x + pos_ref[...][None]
    o_ref[...] = pltpu.einshape("b(ci)d->cb(id)", x, i=CH)


def _to_stream(x, pos):
    bsz, t, _ = x.shape
    ct = 8
    tm = ct * CH
    args = [x]
    specs = [pl.BlockSpec((bsz, tm, D), lambda i: (0, i, 0))]
    if pos is not None:
        args.append(pos)
        specs.append(pl.BlockSpec((tm, D), lambda i: (i, 0)))
    return pl.pallas_call(
        functools.partial(_to_stream_kernel, has_pos=pos is not None),
        grid=(t // tm,),
        in_specs=specs,
        out_specs=pl.BlockSpec((ct, bsz, CH * D), lambda i: (i, 0, 0)),
        out_shape=jax.ShapeDtypeStruct((t // CH, bsz, CH * D), F32),
        compiler_params=_cparams(("arbitrary",)),
    )(*args)


def kernel(x, c, ctx, c_ctx, norm_g, ada_w, ada_b, even_w_in, even_w_out, s5_lam_re, s5_lam_im, s5_log_step, s5_b_re, s5_b_im, s5_c_re, s5_c_im, s5_d, s5_glu_w, s5_glu_b, fnet_w, odd_w_in, odd_w_out, pool_w, pool_scale, conv_w, final_g):
    bsz, n_tok, _ = x.shape
    depth = norm_g.shape[0]

    cond = jnp.concatenate([c, jnp.broadcast_to(c_ctx[None], (16 - bsz, D))], axis=0)
    ada = _ada_all(cond, ada_w, ada_b)

    def lat_mod(l, j):
        return ada[l, j, :bsz].reshape(1, bsz, D)

    def ctx_mod(l, j):
        return jnp.broadcast_to(ada[l, j, bsz].reshape(1, 1, D), (1, bsz, D))

    mt, bend_t, cp, lam16 = _s5_tables(s5_lam_re, s5_lam_im, s5_log_step, s5_b_re, s5_b_im, s5_c_re,
                                       s5_c_im, s5_d)

    need_ctx = [any(j % 2 == 0 for j in range(l + 1, depth)) for l in range(depth)]

    xl = _to_stream(x, _sincos_table(n_tok, D))
    xc = _to_stream(ctx, None)
    for l in range(depth):
        i = l // 2
        last = l == depth - 1
        if l % 2 == 0:
            wcs = _fnet_weights(_block_diag(fnet_w[i]))
            utl, xwl, zl = _even_in(xl, lat_mod(l, 0), lat_mod(l, 1), norm_g[l], even_w_in[i], wcs,
                                    full=True)
            if need_ctx[l]:
                utc, xwc, zc = _even_in(xc, ctx_mod(l, 0), ctx_mod(l, 1), norm_g[l], even_w_in[i], wcs,
                                        full=True)
            else:
                (utc,) = _even_in(xc, ctx_mod(l, 0), ctx_mod(l, 1), norm_g[l], even_w_in[i], None,
                                  full=False)
            ytl, ytc = _s5_mix(utl, utc, mt, bend_t, cp, lam16, bsz, i)
            ybl = _fnet(xwl, bsz)
            xl = _even_out(ytl, ybl, zl, xl, lat_mod(l, 2), s5_glu_w[i], s5_glu_b[i], even_w_out[i])
            if need_ctx[l]:
                ybc = _fnet(xwc, bsz)
                xc = _even_out(ytc, ybc, zc, xc, ctx_mod(l, 2), s5_glu_w[i], s5_glu_b[i], even_w_out[i])
        else:
            if need_ctx[l]:
                xc = _odd_layer(xc, ctx_mod(l, 0), ctx_mod(l, 1), ctx_mod(l, 2), norm_g[l],
                                odd_w_in[i], odd_w_out[i], pool_w[i], pool_scale[i], conv_w[i], None)
            xl = _odd_layer(xl, lat_mod(l, 0), lat_mod(l, 1), lat_mod(l, 2), norm_g[l], odd_w_in[i],
                            odd_w_out[i], pool_w[i], pool_scale[i], conv_w[i],
                            final_g if last else None)
    if depth % 2 == 1:
        raise NotImplementedError("final norm and (B, T, D) order are produced by the last (odd) layer")
    return xl
```

```python
import functools
import math

import numpy as np
import jax
import jax.numpy as jnp
from jax import lax
from jax.experimental import pallas as pl
from jax.experimental.pallas import tpu as pltpu

D = 1024
MIX = 1024
S5_W = 768
FN_W = 256
S5_H = 16
S5_G = 48
S5_P = 64
FN_G = 4
FN_GW = 64
POOL_W = 512
CONV_W = 512
POOL_WINDOWS = (2, 4, 8, 16)
POOL_GW = 128
GRID_W = 64
EPS = 1e-6
POS_BASE = 10000.0
CH = 16
HALO = 8
VMEM_LIMIT = 56 * 1024 * 1024

F32 = jnp.float32
BF16 = jnp.bfloat16
HI = lax.Precision.HIGHEST


def _cparams(sem):
    return pltpu.CompilerParams(dimension_semantics=sem, vmem_limit_bytes=VMEM_LIMIT)


def _sigmoid(v):
    return 0.5 * jnp.tanh(0.5 * v) + 0.5


def _silu(v):
    return v * _sigmoid(v)


def _gelu_tanh(v):
    c = math.sqrt(2.0 / math.pi)
    return 0.5 * v * (1.0 + jnp.tanh(c * (v + 0.044715 * (v * v * v))))


def _mod_norm(x, g, scale, shift):
    ms = jnp.mean(x * x, axis=-1, keepdims=True)
    y = x * lax.rsqrt(ms + EPS) * g
    return y * (1.0 + scale) + shift


def _ada_kernel(c_ref, w_ref, b_ref, o_ref):
    s = _silu(c_ref[...])
    o_ref[...] = jnp.dot(s, w_ref[...], precision=HI, preferred_element_type=F32) + b_ref[...]


def _ada_all(cond, ada_w, ada_b):
    depth = ada_w.shape[0]
    return pl.pallas_call(
        _ada_kernel,
        grid=(depth, 3),
        in_specs=[
            pl.BlockSpec((16, D), lambda l, j: (0, 0)),
            pl.BlockSpec((None, D, D), lambda l, j: (l, 0, j)),
            pl.BlockSpec((None, None, 1, D), lambda l, j: (l, j, 0, 0)),
        ],
        out_specs=pl.BlockSpec((None, None, 16, D), lambda l, j: (l, j, 0, 0)),
        out_shape=jax.ShapeDtypeStruct((depth, 3, 16, D), F32),
        compiler_params=_cparams(("arbitrary", "arbitrary")),
    )(cond, ada_w, ada_b.reshape(depth, 3, 1, D))


def _split3(v):
    hi = v.astype(BF16)
    lo = (v - hi.astype(F32)).astype(BF16)
    return hi, lo


def _s5_tables_kernel(lr_ref, li_ref, ls_ref, btr_ref, bti_ref, cr_ref, ci_ref, d_ref,
                      mt_ref, bt_ref, cp_ref, l16_ref, ere_ref, eim_ref, *, gb):
    nst = 2 * S5_P
    kk = CH * S5_H
    step = jnp.exp(ls_ref[...])
    lr = lr_ref[...]
    li = li_ref[...]
    a = lr * step
    b = li * step

    def powers(expo):
        mag = jnp.exp(expo * a)
        return mag * jnp.cos(expo * b), mag * jnp.sin(expo * b)

    row = lax.broadcasted_iota(jnp.int32, (1, CH, nst), 1).astype(F32)
    fwd = lax.broadcasted_iota(jnp.int32, (1, CH, nst), 2) < S5_P
    one = jnp.ones((1, 1, nst), F32)

    l1re, l1im = powers(one)
    n_re = l1re - 1.0
    den = lr * lr + li * li
    co_re = (n_re * lr + l1im * li) / den
    co_im = (l1im * lr - n_re * li) / den
    btr = btr_ref[...]
    bti = bti_ref[...]
    bb_re = co_re * btr - co_im * bti
    bb_im = co_re * bti + co_im * btr

    pe_re, pe_im = powers(jnp.where(fwd, (CH - 1) - row, row))
    for l in range(CH):
        pr = pe_re[:, l:l + 1, :]
        pi = pe_im[:, l:l + 1, :]
        ere_ref[:, l * S5_H:(l + 1) * S5_H, :] = pr * bb_re - pi * bb_im
        eim_ref[:, l * S5_H:(l + 1) * S5_H, :] = pr * bb_im + pi * bb_re

    cr = cr_ref[...]
    ci = ci_ref[...]
    pc_re, pc_im = powers(jnp.where(fwd, row + 1.0, CH - row))
    for j in range(CH):
        pr = pc_re[:, j:j + 1, :]
        pi = pc_im[:, j:j + 1, :]
        w_re = cr * pr - ci * pi
        w_im = cr * pi + ci * pr
        cp_ref[:, j * S5_H:(j + 1) * S5_H, :] = jnp.concatenate([w_re, -w_im], axis=2).astype(BF16)

    l16re, l16im = powers(one * float(CH))
    l16_ref[...] = jnp.concatenate([l16re, l16im], axis=1)

    fwd2 = lax.broadcasted_iota(jnp.int32, (S5_H, nst), 1) < S5_P
    lane = lax.broadcasted_iota(jnp.int32, (S5_H, kk), 1)
    iblk = lane // S5_H
    hrow = lax.broadcasted_iota(jnp.int32, (S5_H, kk), 0)
    nt = (((1,), (1,)), ((), ()))
    for g in range(gb):
        ere = ere_ref[g]
        eim = eim_ref[g]
        bt_ref[g, 0:nst, :] = ere.T.astype(BF16)
        bt_ref[g, nst:2 * nst, :] = eim.T.astype(BF16)
        e_hi, e_lo = _split3(jnp.concatenate([ere, eim], axis=1))
        rhs = jnp.concatenate([e_hi, e_hi, e_lo], axis=1)
        crg = cr[g]
        cig = ci[g]

        def lag_kernels(sel):
            c2 = jnp.concatenate([jnp.where(sel, crg, 0.0), jnp.where(sel, -cig, 0.0)], axis=1)
            c_hi, c_lo = _split3(c2)
            lhs = jnp.concatenate([c_hi, c_lo, c_hi], axis=1)
            return lax.dot_general(lhs, rhs, nt, preferred_element_type=F32)

        kf = lag_kernels(fwd2)
        kb = lag_kernels(jnp.logical_not(fwd2))
        dg = d_ref[g]
        for j in range(CH):
            sf = (kk - (CH - 1 - j) * S5_H) % kk
            rf = pltpu.roll(kf, sf, 1) if sf else kf
            rb = pltpu.roll(kb, j * S5_H, 1) if j else kb
            blk = (jnp.where(iblk <= j, rf, 0.0) + jnp.where(iblk >= j, rb, 0.0)
                   + jnp.where(lane == j * S5_H + hrow, dg, 0.0))
            mt_ref[g, j * S5_H:(j + 1) * S5_H, :] = blk.astype(BF16)


def _s5_tables(lam_re, lam_im, log_step, b_re, b_im, c_re, c_im, d_skip):
    n = lam_re.shape[0] * S5_G
    gb = 8
    nst = 2 * S5_P
    kk = CH * S5_H

    def fb(v):
        return jnp.concatenate([v[:, 0], v[:, 1]], axis=-1).reshape(n, v.shape[3], nst)

    lr = fb(lam_re[:, :, :, None, :])
    li = fb(lam_im[:, :, :, None, :])
    ls = fb(jnp.broadcast_to(log_step[:, :, :, None, None], log_step.shape + (1, S5_P)))
    btr = fb(jnp.swapaxes(b_re, -1, -2))
    bti = fb(jnp.swapaxes(b_im, -1, -2))
    cr = fb(c_re)
    ci = fb(c_im)
    d = d_skip.reshape(n, S5_H, 1)

    def spec(r, c):
        return pl.BlockSpec((gb, r, c), lambda g: (g, 0, 0))

    return pl.pallas_call(
        functools.partial(_s5_tables_kernel, gb=gb),
        grid=(n // gb,),
        in_specs=[spec(1, nst), spec(1, nst), spec(1, nst), spec(S5_H, nst), spec(S5_H, nst),
                  spec(S5_H, nst), spec(S5_H, nst), spec(S5_H, 1)],
        out_specs=[spec(kk, kk), spec(2 * nst, kk), spec(kk, 2 * nst), spec(2, nst)],
        out_shape=[jax.ShapeDtypeStruct((n, kk, kk), BF16),
                   jax.ShapeDtypeStruct((n, 2 * nst, kk), BF16),
                   jax.ShapeDtypeStruct((n, kk, 2 * nst), BF16),
                   jax.ShapeDtypeStruct((n, 2, nst), F32)],
        scratch_shapes=[pltpu.VMEM((gb, kk, nst), F32)] * 2,
        compiler_params=_cparams(("arbitrary",)),
    )(lr, li, ls, btr, bti, cr, ci, d)


def _even_in_kernel(*refs, full):
    it = iter(refs)
    x_ref = next(it)
    shift_ref, scale_ref, g_ref, w_ref = next(it), next(it), next(it), next(it)
    wcs_ref = next(it) if full else None
    ut_ref = next(it)
    if full:
        xw_ref, z_ref = next(it), next(it)
    wat_ref = next(it)
    if full:
        wbz_ref = next(it)
        scr = [next(it) for _ in range(4)]

    @pl.when(pl.program_id(0) == 0)
    def _():
        wat_ref[...] = w_ref[:, :S5_W].T.astype(BF16)
        if full:
            wbz_ref[...] = w_ref[:, S5_W:].astype(BF16)

    x = x_ref[...]
    h = _mod_norm(x, g_ref[...], scale_ref[...], shift_ref[...])
    nc, nb, _ = x.shape
    rows = nc * nb
    hb = h.reshape(rows, D).astype(BF16)
    pt = lax.dot_general(wat_ref[...], hb, (((1,), (1,)), ((), ())), preferred_element_type=F32)
    ut_ref[...] = pt.astype(BF16).reshape(S5_G, S5_H, rows)
    if full:
        p = jnp.dot(hb, wbz_ref[...], preferred_element_type=F32)
        z_ref[...] = _silu(p[:, FN_W:]).astype(BF16)
        xw = jnp.dot(p[:, :FN_W].astype(BF16), wcs_ref[...], preferred_element_type=F32)
        for q in range(4):
            scr[q][...] = xw[:, q * 128:(q + 1) * 128]
        for q in range(4):
            part, half = divmod(q, 2)
            for bi in range(nb):
                piece = scr[q][pl.ds(bi, nc, stride=nb), :]
                lo = bi * FN_W + half * 128
                xw_ref[part, :, lo:lo + 128] = piece.astype(BF16)


def _even_in(xs, shift, scale, g, w_in, layer, wcs, *, full):
    nc, bsz, _ = xs.shape
    rows = nc * bsz
    ncol = w_in.shape[2] if full else S5_W
    args = [xs]
    specs = [pl.BlockSpec((nc, bsz, D), lambda i: (0, 0, i))]
    args += [shift, scale, g.reshape(1, D), w_in]
    specs += [pl.BlockSpec((1, bsz, D), lambda i: (0, 0, 0)),
              pl.BlockSpec((1, bsz, D), lambda i: (0, 0, 0)),
              pl.BlockSpec((1, D), lambda i: (0, 0)),
              pl.BlockSpec((None, D, ncol), lambda i: (layer, 0, 0), pipeline_mode=pl.Buffered(1))]
    out_shape = [jax.ShapeDtypeStruct((S5_G, CH * S5_H, rows), BF16)]
    out_specs = [pl.BlockSpec((S5_G, S5_H, rows), lambda i: (0, i, 0))]
    scratch = [pltpu.VMEM((S5_W, D), BF16)]
    if full:
        args += [wcs]
        specs += [pl.BlockSpec((FN_W, 2 * FN_W), lambda i: (0, 0))]
        out_shape += [jax.ShapeDtypeStruct((2, CH, nc, bsz * FN_W), BF16),
                      jax.ShapeDtypeStruct((CH, rows, MIX), BF16)]
        out_specs += [pl.BlockSpec((2, None, nc, bsz * FN_W), lambda i: (0, i, 0, 0)),
                      pl.BlockSpec((None, rows, MIX), lambda i: (i, 0, 0))]
        scratch += [pltpu.VMEM((D, FN_W + MIX), BF16)] + [pltpu.VMEM((rows, 128), F32)] * 4
    return pl.pallas_call(
        functools.partial(_even_in_kernel, full=full),
        grid=(CH,),
        in_specs=specs,
        out_specs=out_specs,
        out_shape=out_shape,
        scratch_shapes=scratch,
        compiler_params=_cparams(("arbitrary",)),
    )(*args)


def _s5_kernel(utl_ref, utc_ref, mt_ref, bt_ref, cp_ref, l16_ref, ytl_ref, ytc_ref,
               sre_ref, sim_ref, are_ref, aim_ref, bre_ref, bim_ref, *, bsz, ncl, ncc, gs):
    nl = bsz * ncl
    nx = bsz * ncc
    nst = 2 * S5_P
    for g in range(gs):
        bt = bt_ref[g]
        sl = jnp.dot(bt, utl_ref[g], preferred_element_type=F32).T
        sc = jnp.dot(bt, utc_ref[g], preferred_element_type=F32).T
        sre_ref[g, 0:nl, :] = sl[:, :nst]
        sim_ref[g, 0:nl, :] = sl[:, nst:]
        sre_ref[g, nl:nl + nx, :] = sc[:, :nst]
        sim_ref[g, nl:nl + nx, :] = sc[:, nst:]

    lam = [l16_ref[g] for g in range(gs)]
    is_fwd = lax.broadcasted_iota(jnp.int32, (bsz, nst), 1) < S5_P

    def make_step(base, nchunk):
        def step(c, carry):
            rf = pl.ds(pl.multiple_of(base + c * bsz, bsz), bsz)
            rb = pl.ds(pl.multiple_of(base + (nchunk - 1 - c) * bsz, bsz), bsz)
            out = []
            for g in range(gs):
                sre, sim = carry[2 * g], carry[2 * g + 1]
                lre = lam[g][0:1, :]
                lim = lam[g][1:2, :]
                are_ref[g, rf, :] = sre
                aim_ref[g, rf, :] = sim
                bre_ref[g, rb, :] = sre
                bim_ref[g, rb, :] = sim
                in_re = jnp.where(is_fwd, sre_ref[g, rf, :], sre_ref[g, rb, :])
                in_im = jnp.where(is_fwd, sim_ref[g, rf, :], sim_ref[g, rb, :])
                out.append(lre * sre - lim * sim + in_re)
                out.append(lre * sim + lim * sre + in_im)
            return tuple(out)
        return step

    zero = jnp.zeros((bsz, nst), F32)
    carry = lax.fori_loop(0, ncc, make_step(nl, ncc), (zero,) * (2 * gs))
    lax.fori_loop(0, ncl, make_step(0, ncl), carry)

    nt = (((1,), (1,)), ((), ()))
    fwd_rows = lax.broadcasted_iota(jnp.int32, (nl + nx, nst), 1) < S5_P
    for g in range(gs):
        mt = mt_ref[g]
        cp = cp_ref[g]
        s0 = jnp.concatenate([jnp.where(fwd_rows, are_ref[g], bre_ref[g]),
                              jnp.where(fwd_rows, aim_ref[g], bim_ref[g])], axis=1).astype(BF16)
        ytl_ref[g] = (jnp.dot(mt, utl_ref[g], preferred_element_type=F32)
                      + lax.dot_general(cp, s0[0:nl], nt, preferred_element_type=F32)).astype(BF16)
        ytc_ref[g] = (jnp.dot(mt, utc_ref[g], preferred_element_type=F32)
                      + lax.dot_general(cp, s0[nl:nl + nx], nt, preferred_element_type=F32)).astype(BF16)


def _s5_mix(utl, utc, mt, bend_t, cp, lam16, bsz, layer):
    nl = utl.shape[2]
    nx = utc.shape[2]
    kk = CH * S5_H
    nst = 2 * S5_P
    gs = 4
    off = layer * (S5_G // gs)

    def gspec(r, c):
        return pl.BlockSpec((gs, r, c), lambda g: (g, 0, 0))

    def tspec(r, c):
        return pl.BlockSpec((gs, r, c), lambda g: (g + off, 0, 0))

    return pl.pallas_call(
        functools.partial(_s5_kernel, bsz=bsz, ncl=nl // bsz, ncc=nx // bsz, gs=gs),
        grid=(S5_G // gs,),
        in_specs=[gspec(kk, nl), gspec(kk, nx), tspec(kk, kk), tspec(2 * nst, kk), tspec(kk, 2 * nst),
                  tspec(2, nst)],
        out_specs=[gspec(kk, nl), gspec(kk, nx)],
        out_shape=[jax.ShapeDtypeStruct((S5_G, kk, nl), BF16),
                   jax.ShapeDtypeStruct((S5_G, kk, nx), BF16)],
        scratch_shapes=[pltpu.VMEM((gs, nl + nx, nst), F32)] * 6,
        compiler_params=_cparams(("arbitrary",)),
    )(utl, utc, mt, bend_t, cp, lam16)


def _fnet_weights_kernel(ccs_ref, fw_ref, o_ref):
    fw = fw_ref[...]
    ccs = ccs_ref[...]
    wc = jnp.dot(ccs[:, :FN_W], fw, precision=HI, preferred_element_type=F32)
    ws = jnp.dot(ccs[:, FN_W:], fw, precision=HI, preferred_element_type=F32)
    o_ref[...] = jnp.concatenate([wc, ws], axis=1).astype(BF16)


def _fnet_weights(fw_bd):
    return pl.pallas_call(
        _fnet_weights_kernel,
        out_shape=jax.ShapeDtypeStruct((FN_W, 2 * FN_W), BF16),
    )(_dft_channel_matrix(), fw_bd)


def _fnet_kernel(cm_ref, v_ref, o_ref, cmb_ref):
    @pl.when(pl.program_id(1) == 0)
    def _():
        cmb_ref[...] = cm_ref[...].astype(BF16)

    v = v_ref[...]
    v2 = v.reshape(v.shape[0] * v.shape[1] * v.shape[2], v.shape[3])
    y = jnp.dot(cmb_ref[...], v2, preferred_element_type=F32)
    o_ref[...] = y.reshape(o_ref.shape).astype(BF16)


def _dft_time_matrix(nc):
    t = nc * CH
    order = (np.arange(nc)[None, :] * CH + np.arange(CH)[:, None]).reshape(-1)
    prod = (order[:, None].astype(np.int64) * order[None, :].astype(np.int64)) % t
    ang = prod.astype(np.float64) * (2.0 * np.pi / t)
    scale = 1.0 / math.sqrt(t * FN_GW)
    return jnp.asarray(np.concatenate([np.cos(ang), -np.sin(ang)], axis=1) * scale, dtype=F32)


def _dft_channel_matrix():
    c = np.arange(FN_GW)
    ang = (c[:, None] * c[None, :] % FN_GW).astype(np.float64) * (2.0 * np.pi / FN_GW)
    eye = np.eye(FN_G)
    return jnp.asarray(np.concatenate([np.kron(eye, np.cos(ang)), np.kron(eye, np.sin(ang))], axis=1),
                       dtype=F32)


def _fnet(xw, bsz):
    nc = xw.shape[2]
    t = nc * CH
    rt = min(t, 512)
    ipt = rt // nc
    bpb = 4
    cm = _dft_time_matrix(nc)
    return pl.pallas_call(
        _fnet_kernel,
        grid=(t // rt, bsz // bpb),
        in_specs=[pl.BlockSpec((rt, 2 * t), lambda r, b: (r, 0)),
                  pl.BlockSpec((2, CH, nc, bpb * FN_W), lambda r, b: (0, 0, 0, b))],
        out_specs=pl.BlockSpec((ipt, nc, bpb * FN_W), lambda r, b: (r, 0, b)),
        out_shape=jax.ShapeDtypeStruct((CH, nc, bsz * FN_W), BF16),
        scratch_shapes=[pltpu.VMEM((rt, 2 * t), BF16)],
        compiler_params=_cparams(("arbitrary", "arbitrary")),
    )(cm, xw)


def _even_out_kernel(yt_ref, yb_ref, z_ref, x_ref, gate_ref, gwf_ref, gb_ref, wof_ref, o_ref,
                     scr0_ref, scr1_ref, gw_ref, wo_ref):
    scr = [scr0_ref, scr1_ref]

    @pl.when(pl.program_id(0) == 0)
    def _():
        gw_ref[...] = gwf_ref[...].astype(BF16)
        wo_ref[...] = wof_ref[...].astype(BF16)

    yt = yt_ref[...]
    rows = yt.shape[2]
    ya = _gelu_tanh(yt.astype(F32).reshape(S5_W, rows).T)
    glu = jnp.dot(ya.astype(BF16), gw_ref[...], preferred_element_type=F32) + gb_ref[...]
    ya = ya * _sigmoid(glu)
    x = x_ref[...]
    nc, nb, _ = x.shape
    for half in range(2):
        for bi in range(nb):
            lo = bi * FN_W + half * 128
            scr[half][pl.ds(bi, nc, stride=nb), :] = yb_ref[:, lo:lo + 128].astype(F32)
    yb = jnp.concatenate([scr[0][...], scr[1][...]], axis=1)
    sz = z_ref[...]
    ma = (ya * sz[:, :S5_W]).astype(BF16)
    mb = (yb * sz[:, S5_W:]).astype(BF16)
    out = (jnp.dot(ma, wo_ref[0:S5_W, :], preferred_element_type=F32)
           + jnp.dot(mb, wo_ref[S5_W:MIX, :], preferred_element_type=F32))
    o_ref[...] = x + gate_ref[...] * out.reshape(x.shape)


def _even_out(yt, yb, z, xs, gate, glu_w, glu_b, w_out, layer):
    nc, bsz, _ = xs.shape
    rows = nc * bsz
    args = [yt, yb, z, xs]
    specs = [pl.BlockSpec((S5_G, S5_H, rows), lambda j: (0, j, 0)),
             pl.BlockSpec((None, nc, bsz * FN_W), lambda j: (j, 0, 0)),
             pl.BlockSpec((None, rows, MIX), lambda j: (j, 0, 0)),
             pl.BlockSpec((nc, bsz, D), lambda j: (0, 0, j))]
    args += [gate, glu_w, glu_b.reshape(1, S5_W), w_out]
    specs += [pl.BlockSpec((1, bsz, D), lambda j: (0, 0, 0)),
              pl.BlockSpec((None, S5_W, S5_W), lambda j: (layer, 0, 0), pipeline_mode=pl.Buffered(1)),
              pl.BlockSpec((1, S5_W), lambda j: (0, 0)),
              pl.BlockSpec((None, MIX, D), lambda j: (layer, 0, 0), pipeline_mode=pl.Buffered(1))]
    return pl.pallas_call(
        _even_out_kernel,
        grid=(CH,),
        in_specs=specs,
        out_specs=pl.BlockSpec((nc, bsz, D), lambda j: (0, 0, j)),
        out_shape=jax.ShapeDtypeStruct((nc, bsz, CH * D), F32),
        scratch_shapes=[pltpu.VMEM((rows, 128), F32)] * 2 + [pltpu.VMEM((S5_W, S5_W), BF16),
                                                             pltpu.VMEM((MIX, D), BF16)],
        compiler_params=_cparams(("arbitrary",)),
    )(*args)


def _odd_kernel(*refs, ct, t_total, final):
    it = iter(refs)
    xm_ref, xp_ref, xn_ref = next(it), next(it), next(it)
    shift_ref, scale_ref, gate_ref, g_ref = next(it), next(it), next(it), next(it)
    wif_ref, wof_ref, pwf_ref, ps_ref, cw_ref = next(it), next(it), next(it), next(it), next(it)
    fg_ref = next(it) if final else None
    o_ref = next(it)
    h_ref = next(it)
    pe_ref = next(it)
    wi_ref, wo_ref, pw_ref = next(it), next(it), next(it)
    fin_ref = next(it) if final else None

    ti = pl.program_id(0)

    @pl.when(ti == 0)
    def _():
        wi_ref[...] = wif_ref[...].astype(BF16)
        wo_ref[...] = wof_ref[...].astype(BF16)
        pw_ref[...] = pwf_ref[...].astype(BF16)

    n1 = POOL_W + 2 * CONV_W
    nb = xm_ref.shape[1]
    tm = ct * CH
    ne = tm + 2 * HALO

    def hn(xv):
        return _mod_norm(xv, g_ref[...], scale_ref[...], shift_ref[...])

    for i in range(CH):
        hi = hn(xm_ref[:, :, i * D:(i + 1) * D])
        if i < HALO:
            h_ref[0:ct, i + HALO] = hi
        else:
            h_ref[1:ct + 1, i - HALO] = hi
    for i in range(HALO):
        h_ref[0, i] = hn(xp_ref[:, :, (i + HALO) * D:(i + HALO + 1) * D])[0]
        h_ref[ct, i + HALO] = hn(xn_ref[:, :, i * D:(i + 1) * D])[0]

    he = h_ref[...].reshape(ne * nb, D).astype(BF16)
    pe = jnp.dot(he, wi_ref[:, 0:n1], preferred_element_type=F32)
    te = ti * tm - HALO + lax.broadcasted_iota(jnp.int32, (ne * nb, 1), 0) // nb
    valid = jnp.logical_and(te >= 0, te < t_total)
    pe_ref[:, 0:POOL_W] = jnp.where(valid, pe[:, :POOL_W], 0.0)
    pe_ref[:, POOL_W:POOL_W + CONV_W] = jnp.where(
        valid, pe[:, POOL_W:POOL_W + CONV_W] * pe[:, POOL_W + CONV_W:], 0.0)

    m0 = HALO * nb
    mr = tm * nb
    p2 = jnp.dot(he[m0:m0 + mr], wi_ref[:, n1:], preferred_element_type=F32)
    b_gate = p2[:, :CONV_W]
    sz = _silu(p2[:, CONV_W:])

    tpos = ti * tm + lax.broadcasted_iota(jnp.int32, (mr, 1), 0) // nb
    pooled = []
    for gi, w in enumerate(POOL_WINDOWS):
        c0 = gi * POOL_GW
        s = pe_ref[:, c0:c0 + POOL_GW]
        n = ne
        width = 1
        while width < w:
            s = s[0:(n - width) * nb] + s[width * nb:n * nb]
            n -= width
            width *= 2
        start = (HALO - w // 2) * nb
        total = s[start:start + mr]
        hi = jnp.minimum(tpos + w // 2, t_total)
        lo = jnp.maximum(tpos - w // 2, 0)
        cnt = (hi - lo).astype(F32)
        centre = pe_ref[m0:m0 + mr, c0:c0 + POOL_GW]
        pg = total / cnt - centre
        pooled.append(jnp.dot(pg.astype(BF16), pw_ref[gi], preferred_element_type=F32))
    y_c = jnp.concatenate(pooled, axis=1) * ps_ref[...]

    cwt = cw_ref[...]
    vm = pe_ref[m0 - nb:m0 - nb + mr, POOL_W:POOL_W + CONV_W]
    v0 = pe_ref[m0:m0 + mr, POOL_W:POOL_W + CONV_W]
    vp = pe_ref[m0 + nb:m0 + nb + mr, POOL_W:POOL_W + CONV_W]
    y_d = b_gate * (vm * cwt[0:1, :] + v0 * cwt[1:2, :] + vp * cwt[2:3, :])

    y = (jnp.concatenate([y_c, y_d], axis=1) * sz).astype(BF16)
    out = jnp.dot(y, wo_ref[...], preferred_element_type=F32)
    go = (gate_ref[...] * out.reshape(tm, nb, D)).reshape(ct, CH, nb, D)
    for i in range(CH):
        xo = xm_ref[:, :, i * D:(i + 1) * D] + go[:, i]
        if not final:
            o_ref[:, :, i * D:(i + 1) * D] = xo
            continue
        ms = jnp.mean(xo * xo, axis=-1, keepdims=True)
        fin_ref[:, :, i * D:(i + 1) * D] = xo * lax.rsqrt(ms + EPS) * fg_ref[...]
    if final:
        o_ref[...] = pltpu.einshape("cb(id)->b(ci)d", fin_ref[...], i=CH)


def _odd_layer(xs, shift, scale, gate, g, w_in, w_out, pool_w, layer, pool_scale, conv_w, final_g):
    nc, bsz, _ = xs.shape
    ct = 8
    final = final_g is not None
    args = [xs, xs, xs, shift, scale, gate, g.reshape(1, D), w_in, w_out, pool_w,
            pool_scale.reshape(1, POOL_W), conv_w]
    vec = pl.BlockSpec((1, bsz, D), lambda i: (0, 0, 0))

    def const(*shape):
        return pl.BlockSpec(shape, lambda i: (0,) * len(shape), pipeline_mode=pl.Buffered(1))

    def stacked(*shape):
        return pl.BlockSpec((None,) + shape, lambda i: (layer,) + (0,) * len(shape),
                            pipeline_mode=pl.Buffered(1))

    specs = [pl.BlockSpec((ct, bsz, CH * D), lambda i: (i, 0, 0)),
             pl.BlockSpec((1, bsz, CH * D), lambda i: (jnp.maximum(i * ct - 1, 0), 0, 0)),
             pl.BlockSpec((1, bsz, CH * D), lambda i: (jnp.minimum((i + 1) * ct, nc - 1), 0, 0)),
             vec, vec, vec,
             const(1, D), stacked(*w_in.shape[1:]), stacked(MIX, D),
             stacked(len(POOL_WINDOWS), POOL_GW, POOL_GW), const(1, POOL_W), const(3, CONV_W)]
    scratch = [pltpu.VMEM((ct + 1, CH, bsz, D), F32),
               pltpu.VMEM(((ct + 1) * CH * bsz, POOL_W + CONV_W), F32),
               pltpu.VMEM(w_in.shape[1:], BF16), pltpu.VMEM((MIX, D), BF16),
               pltpu.VMEM((len(POOL_WINDOWS), POOL_GW, POOL_GW), BF16)]
    if final:
        args.append(final_g.reshape(1, D))
        specs.append(const(1, D))
        scratch.append(pltpu.VMEM((ct, bsz, CH * D), F32))
        out_spec = pl.BlockSpec((bsz, ct * CH, D), lambda i: (0, i, 0))
        out_shape = jax.ShapeDtypeStruct((bsz, nc * CH, D), F32)
    else:
        out_spec = pl.BlockSpec((ct, bsz, CH * D), lambda i: (i, 0, 0))
        out_shape = jax.ShapeDtypeStruct((nc, bsz, CH * D), F32)
    return pl.pallas_call(
        functools.partial(_odd_kernel, ct=ct, t_total=nc * CH, final=final),
        grid=(nc // ct,),
        in_specs=specs,
        out_specs=out_spec,
        out_shape=out_shape,
        scratch_shapes=scratch,
        compiler_params=_cparams(("arbitrary",)),
    )(*args)


def _sincos_table(n_tok, dim):
    rows = n_tok // GRID_W
    rr, cc = np.meshgrid(np.arange(rows, dtype=np.float64), np.arange(GRID_W, dtype=np.float64),
                         indexing='ij')
    rr = rr.reshape(-1, 1)
    cc = cc.reshape(-1, 1)
    quarter = dim // 4
    omega = POS_BASE ** (-np.arange(quarter, dtype=np.float64) / quarter)
    tab = np.concatenate([np.sin(rr * omega), np.cos(rr * omega), np.sin(cc * omega), np.cos(cc * omega)],
                         axis=-1)
    return jnp.asarray(tab, dtype=F32)


def _block_diag(w):
    g, c, _ = w.shape
    eye = jnp.eye(g, dtype=w.dtype)
    return (eye[:, None, :, None] * w[:, :, None, :]).reshape(g * c, g * c)


def _to_stream_kernel(*refs, has_pos):
    it = iter(refs)
    x_ref = next(it)
    pos_ref = next(it) if has_pos else None
    o_ref = next(it)
    x = x_ref[...]
    if has_pos:
        x = x + pos_ref[...][None]
    o_ref[...] = pltpu.einshape("b(ci)d->cb(id)", x, i=CH)


def _to_stream(x, pos):
    bsz, t, _ = x.shape
    ct = 8
    tm = ct * CH
    args = [x]
    specs = [pl.BlockSpec((bsz, tm, D), lambda i: (0, i, 0))]
    if pos is not None:
        args.append(pos)
        specs.append(pl.BlockSpec((tm, D), lambda i: (i, 0)))
    return pl.pallas_call(
        functools.partial(_to_stream_kernel, has_pos=pos is not None),
        grid=(t // tm,),
        in_specs=specs,
        out_specs=pl.BlockSpec((ct, bsz, CH * D), lambda i: (i, 0, 0)),
        out_shape=jax.ShapeDtypeStruct((t // CH, bsz, CH * D), F32),
        compiler_params=_cparams(("arbitrary",)),
    )(*args)


def kernel(x, c, ctx, c_ctx, norm_g, ada_w, ada_b, even_w_in, even_w_out, s5_lam_re, s5_lam_im, s5_log_step, s5_b_re, s5_b_im, s5_c_re, s5_c_im, s5_d, s5_glu_w, s5_glu_b, fnet_w, odd_w_in, odd_w_out, pool_w, pool_scale, conv_w, final_g):
    bsz, n_tok, _ = x.shape
    depth = norm_g.shape[0]

    cond = jnp.concatenate([c, jnp.broadcast_to(c_ctx[None], (16 - bsz, D))], axis=0)
    ada = _ada_all(cond, ada_w, ada_b)

    def lat_mod(l, j):
        return ada[l, j, :bsz].reshape(1, bsz, D)

    def ctx_mod(l, j):
        return jnp.broadcast_to(ada[l, j, bsz].reshape(1, 1, D), (1, bsz, D))

    mt, bend_t, cp, lam16 = _s5_tables(s5_lam_re, s5_lam_im, s5_log_step, s5_b_re, s5_b_im, s5_c_re,
                                       s5_c_im, s5_d)

    need_ctx = [any(j % 2 == 0 for j in range(l + 1, depth)) for l in range(depth)]

    xl = _to_stream(x, _sincos_table(n_tok, D))
    xc = _to_stream(ctx, None)
    for l in range(depth):
        i = l // 2
        last = l == depth - 1
        if l % 2 == 0:
            wcs = _fnet_weights(_block_diag(fnet_w[i]))
            utl, xwl, zl = _even_in(xl, lat_mod(l, 0), lat_mod(l, 1), norm_g[l], even_w_in, i, wcs,
                                    full=True)
            if need_ctx[l]:
                utc, xwc, zc = _even_in(xc, ctx_mod(l, 0), ctx_mod(l, 1), norm_g[l], even_w_in, i, wcs,
                                        full=True)
            else:
                (utc,) = _even_in(xc, ctx_mod(l, 0), ctx_mod(l, 1), norm_g[l], even_w_in, i, None,
                                  full=False)
            ytl, ytc = _s5_mix(utl, utc, mt, bend_t, cp, lam16, bsz, i)
            ybl = _fnet(xwl, bsz)
            xl = _even_out(ytl, ybl, zl, xl, lat_mod(l, 2), s5_glu_w, s5_glu_b[i], even_w_out, i)
            if need_ctx[l]:
                ybc = _fnet(xwc, bsz)
                xc = _even_out(ytc, ybc, zc, xc, ctx_mod(l, 2), s5_glu_w, s5_glu_b[i], even_w_out, i)
        else:
            if need_ctx[l]:
                xc = _odd_layer(xc, ctx_mod(l, 0), ctx_mod(l, 1), ctx_mod(l, 2), norm_g[l],
                                odd_w_in, odd_w_out, pool_w, i, pool_scale[i], conv_w[i], None)
            xl = _odd_layer(xl, lat_mod(l, 0), lat_mod(l, 1), lat_mod(l, 2), norm_g[l], odd_w_in,
                            odd_w_out, pool_w, i, pool_scale[i], conv_w[i],
                            final_g if last else None)
    if depth % 2 == 1:
        raise NotImplementedError("final norm and (B, T, D) order are produced by the last (odd) layer")
    return xl
```

```python
import functools
import math

import numpy as np
import jax
import jax.numpy as jnp
from jax import lax
from jax.experimental import pallas as pl
from jax.experimental.pallas import tpu as pltpu

D = 1024
MIX = 1024
S5_W = 768
FN_W = 256
S5_H = 16
S5_G = 48
S5_P = 64
FN_G = 4
FN_GW = 64
POOL_W = 512
CONV_W = 512
POOL_WINDOWS = (2, 4, 8, 16)
POOL_GW = 128
GRID_W = 64
EPS = 1e-6
POS_BASE = 10000.0
CH = 16
HALO = 8
VMEM_LIMIT = 56 * 1024 * 1024

F32 = jnp.float32
BF16 = jnp.bfloat16
HI = lax.Precision.HIGHEST


def _cparams(sem):
    return pltpu.CompilerParams(dimension_semantics=sem, vmem_limit_bytes=VMEM_LIMIT)


def _sigmoid(v):
    return 0.5 * jnp.tanh(0.5 * v) + 0.5


def _silu(v):
    return v * _sigmoid(v)


def _gelu_tanh(v):
    c = math.sqrt(2.0 / math.pi)
    return 0.5 * v * (1.0 + jnp.tanh(c * (v + 0.044715 * (v * v * v))))


def _mod_norm(x, g, scale, shift):
    ms = jnp.mean(x * x, axis=-1, keepdims=True)
    y = x * lax.rsqrt(ms + EPS) * g
    return y * (1.0 + scale) + shift


def _ada_kernel(c_ref, w_ref, b_ref, o_ref):
    s = _silu(c_ref[...])
    o_ref[...] = jnp.dot(s, w_ref[...], precision=HI, preferred_element_type=F32) + b_ref[...]


def _ada_all(cond, ada_w, ada_b):
    depth = ada_w.shape[0]
    return pl.pallas_call(
        _ada_kernel,
        grid=(depth, 3),
        in_specs=[
            pl.BlockSpec((16, D), lambda l, j: (0, 0)),
            pl.BlockSpec((None, D, D), lambda l, j: (l, 0, j)),
            pl.BlockSpec((None, None, 1, D), lambda l, j: (l, j, 0, 0)),
        ],
        out_specs=pl.BlockSpec((None, None, 16, D), lambda l, j: (l, j, 0, 0)),
        out_shape=jax.ShapeDtypeStruct((depth, 3, 16, D), F32),
        compiler_params=_cparams(("arbitrary", "arbitrary")),
    )(cond, ada_w, ada_b.reshape(depth, 3, 1, D))


def _split3(v):
    hi = v.astype(BF16)
    lo = (v - hi.astype(F32)).astype(BF16)
    return hi, lo


def _s5_tables_kernel(lr_ref, li_ref, ls_ref, btr_ref, bti_ref, cr_ref, ci_ref, d_ref,
                      mt_ref, bt_ref, cp_ref, l16_ref, ere_ref, eim_ref, *, gb):
    nst = 2 * S5_P
    kk = CH * S5_H
    step = jnp.exp(ls_ref[...])
    lr = lr_ref[...]
    li = li_ref[...]
    a = lr * step
    b = li * step

    def powers(expo):
        mag = jnp.exp(expo * a)
        return mag * jnp.cos(expo * b), mag * jnp.sin(expo * b)

    row = lax.broadcasted_iota(jnp.int32, (1, CH, nst), 1).astype(F32)
    fwd = lax.broadcasted_iota(jnp.int32, (1, CH, nst), 2) < S5_P
    one = jnp.ones((1, 1, nst), F32)

    l1re, l1im = powers(one)
    n_re = l1re - 1.0
    den = lr * lr + li * li
    co_re = (n_re * lr + l1im * li) / den
    co_im = (l1im * lr - n_re * li) / den
    btr = btr_ref[...]
    bti = bti_ref[...]
    bb_re = co_re * btr - co_im * bti
    bb_im = co_re * bti + co_im * btr

    pe_re, pe_im = powers(jnp.where(fwd, (CH - 1) - row, row))
    for l in range(CH):
        pr = pe_re[:, l:l + 1, :]
        pi = pe_im[:, l:l + 1, :]
        ere_ref[:, l * S5_H:(l + 1) * S5_H, :] = pr * bb_re - pi * bb_im
        eim_ref[:, l * S5_H:(l + 1) * S5_H, :] = pr * bb_im + pi * bb_re

    cr = cr_ref[...]
    ci = ci_ref[...]
    pc_re, pc_im = powers(jnp.where(fwd, row + 1.0, CH - row))
    for j in range(CH):
        pr = pc_re[:, j:j + 1, :]
        pi = pc_im[:, j:j + 1, :]
        w_re = cr * pr - ci * pi
        w_im = cr * pi + ci * pr
        cp_ref[:, j * S5_H:(j + 1) * S5_H, :] = jnp.concatenate([w_re, -w_im], axis=2).astype(BF16)

    l16re, l16im = powers(one * float(CH))
    l16_ref[...] = jnp.concatenate([l16re, l16im], axis=1)

    fwd2 = lax.broadcasted_iota(jnp.int32, (S5_H, nst), 1) < S5_P
    lane = lax.broadcasted_iota(jnp.int32, (S5_H, kk), 1)
    iblk = lane // S5_H
    hrow = lax.broadcasted_iota(jnp.int32, (S5_H, kk), 0)
    nt = (((1,), (1,)), ((), ()))
    for g in range(gb):
        ere = ere_ref[g]
        eim = eim_ref[g]
        bt_ref[g, 0:nst, :] = ere.T.astype(BF16)
        bt_ref[g, nst:2 * nst, :] = eim.T.astype(BF16)
        e_hi, e_lo = _split3(jnp.concatenate([ere, eim], axis=1))
        rhs = jnp.concatenate([e_hi, e_hi, e_lo], axis=1)
        crg = cr[g]
        cig = ci[g]

        def lag_kernels(sel):
            c2 = jnp.concatenate([jnp.where(sel, crg, 0.0), jnp.where(sel, -cig, 0.0)], axis=1)
            c_hi, c_lo = _split3(c2)
            lhs = jnp.concatenate([c_hi, c_lo, c_hi], axis=1)
            return lax.dot_general(lhs, rhs, nt, preferred_element_type=F32)

        kf = lag_kernels(fwd2)
        kb = lag_kernels(jnp.logical_not(fwd2))
        dg = d_ref[g]
        for j in range(CH):
            sf = (kk - (CH - 1 - j) * S5_H) % kk
            rf = pltpu.roll(kf, sf, 1) if sf else kf
            rb = pltpu.roll(kb, j * S5_H, 1) if j else kb
            blk = (jnp.where(iblk <= j, rf, 0.0) + jnp.where(iblk >= j, rb, 0.0)
                   + jnp.where(lane == j * S5_H + hrow, dg, 0.0))
            mt_ref[g, j * S5_H:(j + 1) * S5_H, :] = blk.astype(BF16)


def _s5_tables(lam_re, lam_im, log_step, b_re, b_im, c_re, c_im, d_skip):
    n = lam_re.shape[0] * S5_G
    gb = 8
    nst = 2 * S5_P
    kk = CH * S5_H

    def fb(v):
        return jnp.concatenate([v[:, 0], v[:, 1]], axis=-1).reshape(n, v.shape[3], nst)

    lr = fb(lam_re[:, :, :, None, :])
    li = fb(lam_im[:, :, :, None, :])
    ls = fb(jnp.broadcast_to(log_step[:, :, :, None, None], log_step.shape + (1, S5_P)))
    btr = fb(jnp.swapaxes(b_re, -1, -2))
    bti = fb(jnp.swapaxes(b_im, -1, -2))
    cr = fb(c_re)
    ci = fb(c_im)
    d = d_skip.reshape(n, S5_H, 1)

    def spec(r, c):
        return pl.BlockSpec((gb, r, c), lambda g: (g, 0, 0))

    return pl.pallas_call(
        functools.partial(_s5_tables_kernel, gb=gb),
        grid=(n // gb,),
        in_specs=[spec(1, nst), spec(1, nst), spec(1, nst), spec(S5_H, nst), spec(S5_H, nst),
                  spec(S5_H, nst), spec(S5_H, nst), spec(S5_H, 1)],
        out_specs=[spec(kk, kk), spec(2 * nst, kk), spec(kk, 2 * nst), spec(2, nst)],
        out_shape=[jax.ShapeDtypeStruct((n, kk, kk), BF16),
                   jax.ShapeDtypeStruct((n, 2 * nst, kk), BF16),
                   jax.ShapeDtypeStruct((n, kk, 2 * nst), BF16),
                   jax.ShapeDtypeStruct((n, 2, nst), F32)],
        scratch_shapes=[pltpu.VMEM((gb, kk, nst), F32)] * 2,
        compiler_params=_cparams(("arbitrary",)),
    )(lr, li, ls, btr, bti, cr, ci, d)


def _even_in_kernel(x_ref, shift_ref, scale_ref, g_ref, w_ref, wcs_ref, ut_ref, xw_ref, z_ref,
                    wat_ref, wbz_ref, *scr, ncl):
    @pl.when(pl.program_id(0) == 0)
    def _():
        wat_ref[...] = w_ref[:, :S5_W].T.astype(BF16)
        wbz_ref[...] = w_ref[:, S5_W:].astype(BF16)

    x = x_ref[...]
    nc, nb, _ = x.shape
    rows = nc * nb
    g = g_ref[...]
    h = jnp.concatenate([_mod_norm(x[0:ncl], g, scale_ref[0:1], shift_ref[0:1]),
                         _mod_norm(x[ncl:nc], g, scale_ref[1:2], shift_ref[1:2])], axis=0)
    hb = h.reshape(rows, D).astype(BF16)
    pt = lax.dot_general(wat_ref[...], hb, (((1,), (1,)), ((), ())), preferred_element_type=F32)
    ut_ref[...] = pt.astype(BF16).reshape(S5_G, S5_H, rows)
    p = jnp.dot(hb, wbz_ref[...], preferred_element_type=F32)
    z_ref[...] = _silu(p[:, FN_W:]).astype(BF16)
    xw = jnp.dot(p[:, :FN_W].astype(BF16), wcs_ref[...], preferred_element_type=F32)
    for q in range(4):
        scr[q][...] = xw[:, q * 128:(q + 1) * 128]
    for q in range(4):
        part, half = divmod(q, 2)
        for bi in range(nb):
            piece = scr[q][pl.ds(bi, nc, stride=nb), :]
            lo = bi * FN_W + half * 128
            xw_ref[part, :, lo:lo + 128] = piece.astype(BF16)


def _even_in(xs, ncl, shift, scale, g, w_in, layer, wcs):
    nc, bsz, _ = xs.shape
    rows = nc * bsz
    mod = pl.BlockSpec((2, bsz, D), lambda i: (0, 0, 0))
    return pl.pallas_call(
        functools.partial(_even_in_kernel, ncl=ncl),
        grid=(CH,),
        in_specs=[pl.BlockSpec((nc, bsz, D), lambda i: (0, 0, i)), mod, mod,
                  pl.BlockSpec((1, D), lambda i: (0, 0)),
                  pl.BlockSpec((None, D, w_in.shape[2]), lambda i: (layer, 0, 0),
                               pipeline_mode=pl.Buffered(1)),
                  pl.BlockSpec((FN_W, 2 * FN_W), lambda i: (0, 0))],
        out_specs=[pl.BlockSpec((S5_G, S5_H, rows), lambda i: (0, i, 0)),
                   pl.BlockSpec((2, None, nc, bsz * FN_W), lambda i: (0, i, 0, 0)),
                   pl.BlockSpec((None, rows, MIX), lambda i: (i, 0, 0))],
        out_shape=[jax.ShapeDtypeStruct((S5_G, CH * S5_H, rows), BF16),
                   jax.ShapeDtypeStruct((2, CH, nc, bsz * FN_W), BF16),
                   jax.ShapeDtypeStruct((CH, rows, MIX), BF16)],
        scratch_shapes=[pltpu.VMEM((S5_W, D), BF16), pltpu.VMEM((D, FN_W + MIX), BF16)]
        + [pltpu.VMEM((rows, 128), F32)] * 4,
        compiler_params=_cparams(("arbitrary",)),
    )(xs, shift, scale, g.reshape(1, D), w_in, wcs)


def _s5_kernel(ut_ref, mt_ref, bt_ref, cp_ref, l16_ref, yt_ref,
               sre_ref, sim_ref, are_ref, aim_ref, bre_ref, bim_ref, *, bsz, ncl, ncc, gs):
    nl = bsz * ncl
    nst = 2 * S5_P
    for g in range(gs):
        st = jnp.dot(bt_ref[g], ut_ref[g], preferred_element_type=F32).T
        sre_ref[g] = st[:, :nst]
        sim_ref[g] = st[:, nst:]

    lam = [l16_ref[g] for g in range(gs)]
    is_fwd = lax.broadcasted_iota(jnp.int32, (bsz, nst), 1) < S5_P

    def make_step(base, nchunk):
        def step(c, carry):
            rf = pl.ds(pl.multiple_of(base + c * bsz, bsz), bsz)
            rb = pl.ds(pl.multiple_of(base + (nchunk - 1 - c) * bsz, bsz), bsz)
            out = []
            for g in range(gs):
                sre, sim = carry[2 * g], carry[2 * g + 1]
                lre = lam[g][0:1, :]
                lim = lam[g][1:2, :]
                are_ref[g, rf, :] = sre
                aim_ref[g, rf, :] = sim
                bre_ref[g, rb, :] = sre
                bim_ref[g, rb, :] = sim
                in_re = jnp.where(is_fwd, sre_ref[g, rf, :], sre_ref[g, rb, :])
                in_im = jnp.where(is_fwd, sim_ref[g, rf, :], sim_ref[g, rb, :])
                out.append(lre * sre - lim * sim + in_re)
                out.append(lre * sim + lim * sre + in_im)
            return tuple(out)
        return step

    zero = jnp.zeros((bsz, nst), F32)
    carry = lax.fori_loop(0, ncc, make_step(nl, ncc), (zero,) * (2 * gs))
    lax.fori_loop(0, ncl, make_step(0, ncl), carry)

    nt = (((1,), (1,)), ((), ()))
    fwd_rows = lax.broadcasted_iota(jnp.int32, (bsz * (ncl + ncc), nst), 1) < S5_P
    for g in range(gs):
        s0 = jnp.concatenate([jnp.where(fwd_rows, are_ref[g], bre_ref[g]),
                              jnp.where(fwd_rows, aim_ref[g], bim_ref[g])], axis=1).astype(BF16)
        yt_ref[g] = (jnp.dot(mt_ref[g], ut_ref[g], preferred_element_type=F32)
                     + lax.dot_general(cp_ref[g], s0, nt, preferred_element_type=F32)).astype(BF16)


def _s5_mix(ut, ncl, mt, bend_t, cp, lam16, bsz, layer):
    rows = ut.shape[2]
    kk = CH * S5_H
    nst = 2 * S5_P
    gs = 4
    off = layer * (S5_G // gs)

    def gspec(r, c):
        return pl.BlockSpec((gs, r, c), lambda g: (g, 0, 0))

    def tspec(r, c):
        return pl.BlockSpec((gs, r, c), lambda g: (g + off, 0, 0))

    return pl.pallas_call(
        functools.partial(_s5_kernel, bsz=bsz, ncl=ncl, ncc=rows // bsz - ncl, gs=gs),
        grid=(S5_G // gs,),
        in_specs=[gspec(kk, rows), tspec(kk, kk), tspec(2 * nst, kk), tspec(kk, 2 * nst), tspec(2, nst)],
        out_specs=gspec(kk, rows),
        out_shape=jax.ShapeDtypeStruct((S5_G, kk, rows), BF16),
        scratch_shapes=[pltpu.VMEM((gs, rows, nst), F32)] * 6,
        compiler_params=_cparams(("arbitrary",)),
    )(ut, mt, bend_t, cp, lam16)


def _fnet_weights_kernel(ccs_ref, fw_ref, o_ref):
    fw = fw_ref[...]
    ccs = ccs_ref[...]
    wc = jnp.dot(ccs[:, :FN_W], fw, precision=HI, preferred_element_type=F32)
    ws = jnp.dot(ccs[:, FN_W:], fw, precision=HI, preferred_element_type=F32)
    o_ref[...] = jnp.concatenate([wc, ws], axis=1).astype(BF16)


def _fnet_weights(fw_bd):
    return pl.pallas_call(
        _fnet_weights_kernel,
        out_shape=jax.ShapeDtypeStruct((FN_W, 2 * FN_W), BF16),
    )(_dft_channel_matrix(), fw_bd)


def _fnet_kernel(cm_ref, v_ref, o_ref, cmb_ref):
    @pl.when(pl.program_id(1) == 0)
    def _():
        cmb_ref[...] = cm_ref[...].astype(BF16)

    v = v_ref[...]
    v2 = v.reshape(v.shape[0] * v.shape[1] * v.shape[2], v.shape[3])
    y = jnp.dot(cmb_ref[...], v2, preferred_element_type=F32)
    o_ref[...] = y.reshape(o_ref.shape).astype(BF16)


def _dft_time_matrix(nc):
    t = nc * CH
    order = (np.arange(nc)[None, :] * CH + np.arange(CH)[:, None]).reshape(-1)
    prod = (order[:, None].astype(np.int64) * order[None, :].astype(np.int64)) % t
    ang = prod.astype(np.float64) * (2.0 * np.pi / t)
    scale = 1.0 / math.sqrt(t * FN_GW)
    return jnp.asarray(np.concatenate([np.cos(ang), -np.sin(ang)], axis=1) * scale, dtype=F32)


def _dft_channel_matrix():
    c = np.arange(FN_GW)
    ang = (c[:, None] * c[None, :] % FN_GW).astype(np.float64) * (2.0 * np.pi / FN_GW)
    eye = np.eye(FN_G)
    return jnp.asarray(np.concatenate([np.kron(eye, np.cos(ang)), np.kron(eye, np.sin(ang))], axis=1),
                       dtype=F32)


def _fnet(xw, c0, nc, bsz):
    t = nc * CH
    rt = min(t, 512)
    ipt = rt // nc
    bpb = 4
    cm = _dft_time_matrix(nc)
    return pl.pallas_call(
        _fnet_kernel,
        grid=(t // rt, bsz // bpb),
        in_specs=[pl.BlockSpec((rt, 2 * t), lambda r, b: (r, 0)),
                  pl.BlockSpec((2, CH, nc, bpb * FN_W), lambda r, b: (0, 0, c0 // nc, b))],
        out_specs=pl.BlockSpec((ipt, nc, bpb * FN_W), lambda r, b: (r, 0, b)),
        out_shape=jax.ShapeDtypeStruct((CH, nc, bsz * FN_W), BF16),
        scratch_shapes=[pltpu.VMEM((rt, 2 * t), BF16)],
        compiler_params=_cparams(("arbitrary", "arbitrary")),
    )(cm, xw)


def _even_out_kernel(*refs, ncl, with_ctx):
    it = iter(refs)
    yt_ref, ybl_ref = next(it), next(it)
    ybc_ref = next(it) if with_ctx else None
    z_ref, x_ref, gate_ref, gwf_ref, gb_ref, wof_ref, o_ref = (next(it), next(it), next(it), next(it),
                                                               next(it), next(it), next(it))
    scr = [next(it), next(it)]
    gw_ref, wo_ref = next(it), next(it)

    @pl.when(pl.program_id(0) == 0)
    def _():
        gw_ref[...] = gwf_ref[...].astype(BF16)
        wo_ref[...] = wof_ref[...].astype(BF16)

    yt = yt_ref[...]
    rows = yt.shape[2]
    ya = _gelu_tanh(yt.astype(F32).reshape(S5_W, rows).T)
    glu = jnp.dot(ya.astype(BF16), gw_ref[...], preferred_element_type=F32) + gb_ref[...]
    ya = ya * _sigmoid(glu)
    x = x_ref[...]
    nc, nb, _ = x.shape
    for half in range(2):
        for bi in range(nb):
            lo = bi * FN_W + half * 128
            scr[half][pl.ds(bi, ncl, stride=nb), :] = ybl_ref[:, lo:lo + 128].astype(F32)
            if with_ctx:
                scr[half][pl.ds(ncl * nb + bi, nc - ncl, stride=nb), :] = (
                    ybc_ref[:, lo:lo + 128].astype(F32))
    yb = jnp.concatenate([scr[0][...], scr[1][...]], axis=1)
    sz = z_ref[...]
    ma = (ya * sz[:, :S5_W]).astype(BF16)
    mb = (yb * sz[:, S5_W:]).astype(BF16)
    out = (jnp.dot(ma, wo_ref[0:S5_W, :], preferred_element_type=F32)
           + jnp.dot(mb, wo_ref[S5_W:MIX, :], preferred_element_type=F32)).reshape(x.shape)
    o_ref[0:ncl] = x[0:ncl] + gate_ref[0:1] * out[0:ncl]
    if with_ctx:
        o_ref[ncl:nc] = x[ncl:nc] + gate_ref[1:2] * out[ncl:nc]


def _even_out(yt, ybl, ybc, z, xs, ncl, gate, glu_w, glu_b, w_out, layer):
    bsz = xs.shape[1]
    with_ctx = ybc is not None
    nc = xs.shape[0] if with_ctx else ncl
    rows = nc * bsz
    args = [yt, ybl]
    specs = [pl.BlockSpec((S5_G, S5_H, rows), lambda j: (0, j, 0)),
             pl.BlockSpec((None, ncl, bsz * FN_W), lambda j: (j, 0, 0))]
    if with_ctx:
        args.append(ybc)
        specs.append(pl.BlockSpec((None, nc - ncl, bsz * FN_W), lambda j: (j, 0, 0)))
    args += [z, xs, gate, glu_w, glu_b.reshape(1, S5_W), w_out]
    specs += [pl.BlockSpec((None, rows, MIX), lambda j: (j, 0, 0)),
              pl.BlockSpec((nc, bsz, D), lambda j: (0, 0, j)),
              pl.BlockSpec((2, bsz, D), lambda j: (0, 0, 0)),
              pl.BlockSpec((None, S5_W, S5_W), lambda j: (layer, 0, 0), pipeline_mode=pl.Buffered(1)),
              pl.BlockSpec((1, S5_W), lambda j: (0, 0)),
              pl.BlockSpec((None, MIX, D), lambda j: (layer, 0, 0), pipeline_mode=pl.Buffered(1))]
    return pl.pallas_call(
        functools.partial(_even_out_kernel, ncl=ncl, with_ctx=with_ctx),
        grid=(CH,),
        in_specs=specs,
        out_specs=pl.BlockSpec((nc, bsz, D), lambda j: (0, 0, j)),
        out_shape=jax.ShapeDtypeStruct((nc, bsz, CH * D), F32),
        scratch_shapes=[pltpu.VMEM((rows, 128), F32)] * 2 + [pltpu.VMEM((S5_W, S5_W), BF16),
                                                             pltpu.VMEM((MIX, D), BF16)],
        compiler_params=_cparams(("arbitrary",)),
    )(*args)


def _odd_kernel(*refs, ct, ntl, t_lat, t_ctx, final):
    it = iter(refs)
    xm_ref, xp_ref, xn_ref = next(it), next(it), next(it)
    shift_ref, scale_ref, gate_ref, g_ref = next(it), next(it), next(it), next(it)
    wif_ref, wof_ref, pwf_ref, ps_ref, cw_ref = next(it), next(it), next(it), next(it), next(it)
    fg_ref = next(it) if final else None
    o_ref = next(it)
    h_ref = next(it)
    pe_ref = next(it)
    wi_ref, wo_ref, pw_ref = next(it), next(it), next(it)
    fin_ref = next(it) if final else None

    ti = pl.program_id(0)
    is_ctx = ti >= ntl
    t0 = jnp.where(is_ctx, ti - ntl, ti) * (ct * CH)
    t_total = jnp.where(is_ctx, t_ctx, t_lat)

    @pl.when(ti == 0)
    def _():
        wi_ref[...] = wif_ref[...].astype(BF16)
        wo_ref[...] = wof_ref[...].astype(BF16)
        pw_ref[...] = pwf_ref[...].astype(BF16)

    n1 = POOL_W + 2 * CONV_W
    nb = xm_ref.shape[1]
    tm = ct * CH
    ne = tm + 2 * HALO

    def hn(xv):
        return _mod_norm(xv, g_ref[...], scale_ref[...], shift_ref[...])

    for i in range(CH):
        hi = hn(xm_ref[:, :, i * D:(i + 1) * D])
        if i < HALO:
            h_ref[0:ct, i + HALO] = hi
        else:
            h_ref[1:ct + 1, i - HALO] = hi
    for i in range(HALO):
        h_ref[0, i] = hn(xp_ref[:, :, (i + HALO) * D:(i + HALO + 1) * D])[0]
        h_ref[ct, i + HALO] = hn(xn_ref[:, :, i * D:(i + 1) * D])[0]

    he = h_ref[...].reshape(ne * nb, D).astype(BF16)
    pe = jnp.dot(he, wi_ref[:, 0:n1], preferred_element_type=F32)
    te = t0 - HALO + lax.broadcasted_iota(jnp.int32, (ne * nb, 1), 0) // nb
    valid = jnp.logical_and(te >= 0, te < t_total)
    pe_ref[:, 0:POOL_W] = jnp.where(valid, pe[:, :POOL_W], 0.0)
    pe_ref[:, POOL_W:POOL_W + CONV_W] = jnp.where(
        valid, pe[:, POOL_W:POOL_W + CONV_W] * pe[:, POOL_W + CONV_W:], 0.0)

    m0 = HALO * nb
    mr = tm * nb
    p2 = jnp.dot(he[m0:m0 + mr], wi_ref[:, n1:], preferred_element_type=F32)
    b_gate = p2[:, :CONV_W]
    sz = _silu(p2[:, CONV_W:])

    tpos = t0 + lax.broadcasted_iota(jnp.int32, (mr, 1), 0) // nb
    pooled = []
    for gi, w in enumerate(POOL_WINDOWS):
        c0 = gi * POOL_GW
        s = pe_ref[:, c0:c0 + POOL_GW]
        n = ne
        width = 1
        while width < w:
            s = s[0:(n - width) * nb] + s[width * nb:n * nb]
            n -= width
            width *= 2
        start = (HALO - w // 2) * nb
        total = s[start:start + mr]
        hi = jnp.minimum(tpos + w // 2, t_total)
        lo = jnp.maximum(tpos - w // 2, 0)
        cnt = (hi - lo).astype(F32)
        centre = pe_ref[m0:m0 + mr, c0:c0 + POOL_GW]
        pg = total / cnt - centre
        pooled.append(jnp.dot(pg.astype(BF16), pw_ref[gi], preferred_element_type=F32))
    y_c = jnp.concatenate(pooled, axis=1) * ps_ref[...]

    cwt = cw_ref[...]
    vm = pe_ref[m0 - nb:m0 - nb + mr, POOL_W:POOL_W + CONV_W]
    v0 = pe_ref[m0:m0 + mr, POOL_W:POOL_W + CONV_W]
    vp = pe_ref[m0 + nb:m0 + nb + mr, POOL_W:POOL_W + CONV_W]
    y_d = b_gate * (vm * cwt[0:1, :] + v0 * cwt[1:2, :] + vp * cwt[2:3, :])

    y = (jnp.concatenate([y_c, y_d], axis=1) * sz).astype(BF16)
    out = jnp.dot(y, wo_ref[...], preferred_element_type=F32)
    go = (gate_ref[...] * out.reshape(tm, nb, D)).reshape(ct, CH, nb, D)
    for i in range(CH):
        xo = xm_ref[:, :, i * D:(i + 1) * D] + go[:, i]
        if not final:
            o_ref[:, :, i * D:(i + 1) * D] = xo
            continue
        ms = jnp.mean(xo * xo, axis=-1, keepdims=True)
        fin_ref[:, :, i * D:(i + 1) * D] = xo * lax.rsqrt(ms + EPS) * fg_ref[...]
    if final:
        o_ref[...] = pltpu.einshape("cb(id)->b(ci)d", fin_ref[...], i=CH)


def _odd_layer(xs, ncl, with_ctx, shift, scale, gate, g, w_in, w_out, pool_w, layer, pool_scale, conv_w,
               final_g):
    nc_all, bsz, _ = xs.shape
    ct = 8
    ntl = ncl // ct
    nc = nc_all if with_ctx else ncl
    final = final_g is not None
    args = [xs, xs, xs, shift, scale, gate, g.reshape(1, D), w_in, w_out, pool_w,
            pool_scale.reshape(1, POOL_W), conv_w]
    vec = pl.BlockSpec((1, bsz, D), lambda i: (jnp.minimum(i // ntl, 1), 0, 0))

    def const(*shape):
        return pl.BlockSpec(shape, lambda i: (0,) * len(shape), pipeline_mode=pl.Buffered(1))

    def stacked(*shape):
        return pl.BlockSpec((None,) + shape, lambda i: (layer,) + (0,) * len(shape),
                            pipeline_mode=pl.Buffered(1))

    specs = [pl.BlockSpec((ct, bsz, CH * D), lambda i: (i, 0, 0)),
             pl.BlockSpec((1, bsz, CH * D), lambda i: (jnp.maximum(i * ct - 1, 0), 0, 0)),
             pl.BlockSpec((1, bsz, CH * D), lambda i: (jnp.minimum((i + 1) * ct, nc_all - 1), 0, 0)),
             vec, vec, vec,
             const(1, D), stacked(*w_in.shape[1:]), stacked(MIX, D),
             stacked(len(POOL_WINDOWS), POOL_GW, POOL_GW), const(1, POOL_W), const(3, CONV_W)]
    scratch = [pltpu.VMEM((ct + 1, CH, bsz, D), F32),
               pltpu.VMEM(((ct + 1) * CH * bsz, POOL_W + CONV_W), F32),
               pltpu.VMEM(w_in.shape[1:], BF16), pltpu.VMEM((MIX, D), BF16),
               pltpu.VMEM((len(POOL_WINDOWS), POOL_GW, POOL_GW), BF16)]
    if final:
        args.append(final_g.reshape(1, D))
        specs.append(const(1, D))
        scratch.append(pltpu.VMEM((ct, bsz, CH * D), F32))
        out_spec = pl.BlockSpec((bsz, ct * CH, D), lambda i: (0, i, 0))
        out_shape = jax.ShapeDtypeStruct((bsz, nc * CH, D), F32)
    else:
        out_spec = pl.BlockSpec((ct, bsz, CH * D), lambda i: (i, 0, 0))
        out_shape = jax.ShapeDtypeStruct((nc, bsz, CH * D), F32)
    return pl.pallas_call(
        functools.partial(_odd_kernel, ct=ct, ntl=ntl, t_lat=ncl * CH, t_ctx=(nc_all - ncl) * CH,
                          final=final),
        grid=(nc // ct,),
        in_specs=specs,
        out_specs=out_spec,
        out_shape=out_shape,
        scratch_shapes=scratch,
        compiler_params=_cparams(("arbitrary",)),
    )(*args)


def _sincos_table(n_tok, dim):
    rows = n_tok // GRID_W
    rr, cc = np.meshgrid(np.arange(rows, dtype=np.float64), np.arange(GRID_W, dtype=np.float64),
                         indexing='ij')
    rr = rr.reshape(-1, 1)
    cc = cc.reshape(-1, 1)
    quarter = dim // 4
    omega = POS_BASE ** (-np.arange(quarter, dtype=np.float64) / quarter)
    tab = np.concatenate([np.sin(rr * omega), np.cos(rr * omega), np.sin(cc * omega), np.cos(cc * omega)],
                         axis=-1)
    return jnp.asarray(tab, dtype=F32)


def _block_diag(w):
    g, c, _ = w.shape
    eye = jnp.eye(g, dtype=w.dtype)
    return (eye[:, None, :, None] * w[:, :, None, :]).reshape(g * c, g * c)


def _to_stream_kernel(x_ref, pos_ref, c_ref, o_ref, *, ntl):
    step = pl.program_id(0)

    @pl.when(step < ntl)
    def _():
        o_ref[...] = pltpu.einshape("b(ci)d->cb(id)", x_ref[...] + pos_ref[...][None], i=CH)

    @pl.when(step >= ntl)
    def _():
        o_ref[...] = pltpu.einshape("b(ci)d->cb(id)", c_ref[...], i=CH)


def _to_stream(x, pos, ctx):
    bsz, t, _ = x.shape
    tc = ctx.shape[1]
    ct = 8
    tm = ct * CH
    ntl = t // tm
    return pl.pallas_call(
        functools.partial(_to_stream_kernel, ntl=ntl),
        grid=((t + tc) // tm,),
        in_specs=[pl.BlockSpec((bsz, tm, D), lambda i: (0, jnp.minimum(i, ntl - 1), 0)),
                  pl.BlockSpec((tm, D), lambda i: (jnp.minimum(i, ntl - 1), 0)),
                  pl.BlockSpec((bsz, tm, D), lambda i: (0, jnp.maximum(i - ntl, 0), 0))],
        out_specs=pl.BlockSpec((ct, bsz, CH * D), lambda i: (i, 0, 0)),
        out_shape=jax.ShapeDtypeStruct(((t + tc) // CH, bsz, CH * D), F32),
        compiler_params=_cparams(("arbitrary",)),
    )(x, pos, ctx)


def kernel(x, c, ctx, c_ctx, norm_g, ada_w, ada_b, even_w_in, even_w_out, s5_lam_re, s5_lam_im, s5_log_step, s5_b_re, s5_b_im, s5_c_re, s5_c_im, s5_d, s5_glu_w, s5_glu_b, fnet_w, odd_w_in, odd_w_out, pool_w, pool_scale, conv_w, final_g):
    bsz, n_tok, _ = x.shape
    depth = norm_g.shape[0]
    ncl = n_tok // CH
    ncc = ctx.shape[1] // CH

    cond = jnp.concatenate([c, jnp.broadcast_to(c_ctx[None], (16 - bsz, D))], axis=0)
    ada = _ada_all(cond, ada_w, ada_b)

    def mod(l, j):
        return jnp.stack([ada[l, j, :bsz], jnp.broadcast_to(ada[l, j, bsz], (bsz, D))])

    mt, bend_t, cp, lam16 = _s5_tables(s5_lam_re, s5_lam_im, s5_log_step, s5_b_re, s5_b_im, s5_c_re,
                                       s5_c_im, s5_d)
    need_ctx = [any(j % 2 == 0 for j in range(l + 1, depth)) for l in range(depth)]

    xs = _to_stream(x, _sincos_table(n_tok, D), ctx)
    for l in range(depth):
        i = l // 2
        last = l == depth - 1
        if l % 2 == 0:
            wcs = _fnet_weights(_block_diag(fnet_w[i]))
            ut, xw, z = _even_in(xs, ncl, mod(l, 0), mod(l, 1), norm_g[l], even_w_in, i, wcs)
            yt = _s5_mix(ut, ncl, mt, bend_t, cp, lam16, bsz, i)
            ybl = _fnet(xw, 0, ncl, bsz)
            ybc = _fnet(xw, ncl, ncc, bsz) if need_ctx[l] else None
            xs = _even_out(yt, ybl, ybc, z, xs, ncl, mod(l, 2), s5_glu_w, s5_glu_b[i], even_w_out, i)
        else:
            xs = _odd_layer(xs, ncl, need_ctx[l], mod(l, 0), mod(l, 1), mod(l, 2), norm_g[l], odd_w_in,
                            odd_w_out, pool_w, i, pool_scale[i], conv_w[i], final_g if last else None)
    if depth % 2 == 1:
        raise NotImplementedError("final norm and (B, T, D) order are produced by the last (odd) layer")
    return xs
```

```python
import functools
import math

import numpy as np
import jax
import jax.numpy as jnp
from jax import lax
from jax.experimental import pallas as pl
from jax.experimental.pallas import tpu as pltpu

D = 1024
MIX = 1024
S5_W = 768
FN_W = 256
S5_H = 16
S5_G = 48
S5_P = 64
FN_G = 4
FN_GW = 64
POOL_W = 512
CONV_W = 512
POOL_WINDOWS = (2, 4, 8, 16)
POOL_GW = 128
GRID_W = 64
EPS = 1e-6
POS_BASE = 10000.0
CH = 16
HALO = 8
VMEM_LIMIT = 56 * 1024 * 1024

F32 = jnp.float32
BF16 = jnp.bfloat16
HI = lax.Precision.HIGHEST


def _cparams(sem):
    return pltpu.CompilerParams(dimension_semantics=sem, vmem_limit_bytes=VMEM_LIMIT)


def _sigmoid(v):
    return 0.5 * jnp.tanh(0.5 * v) + 0.5


def _silu(v):
    return v * _sigmoid(v)


def _gelu_tanh(v):
    c = math.sqrt(2.0 / math.pi)
    return 0.5 * v * (1.0 + jnp.tanh(c * (v + 0.044715 * (v * v * v))))


def _mod_norm(x, g, scale, shift):
    ms = jnp.mean(x * x, axis=-1, keepdims=True)
    y = x * lax.rsqrt(ms + EPS) * g
    return y * (1.0 + scale) + shift


def _ada_kernel(c_ref, w_ref, b_ref, o_ref):
    s = _silu(c_ref[...])
    o_ref[...] = jnp.dot(s, w_ref[...], precision=HI, preferred_element_type=F32) + b_ref[...]


def _ada_all(cond, ada_w, ada_b):
    depth = ada_w.shape[0]
    return pl.pallas_call(
        _ada_kernel,
        grid=(depth, 3),
        in_specs=[
            pl.BlockSpec((16, D), lambda l, j: (0, 0)),
            pl.BlockSpec((None, D, D), lambda l, j: (l, 0, j)),
            pl.BlockSpec((None, None, 1, D), lambda l, j: (l, j, 0, 0)),
        ],
        out_specs=pl.BlockSpec((None, None, 16, D), lambda l, j: (l, j, 0, 0)),
        out_shape=jax.ShapeDtypeStruct((depth, 3, 16, D), F32),
        compiler_params=_cparams(("arbitrary", "arbitrary")),
    )(cond, ada_w, ada_b.reshape(depth, 3, 1, D))


def _split3(v):
    hi = v.astype(BF16)
    lo = (v - hi.astype(F32)).astype(BF16)
    return hi, lo


def _s5_tables_kernel(lr_ref, li_ref, ls_ref, btr_ref, bti_ref, cr_ref, ci_ref, d_ref,
                      mt_ref, bt_ref, cp_ref, l16_ref, ere_ref, eim_ref, *, gb):
    nst = 2 * S5_P
    kk = CH * S5_H
    step = jnp.exp(ls_ref[...])
    lr = lr_ref[...]
    li = li_ref[...]
    a = lr * step
    b = li * step

    def powers(expo):
        mag = jnp.exp(expo * a)
        return mag * jnp.cos(expo * b), mag * jnp.sin(expo * b)

    row = lax.broadcasted_iota(jnp.int32, (1, CH, nst), 1).astype(F32)
    fwd = lax.broadcasted_iota(jnp.int32, (1, CH, nst), 2) < S5_P
    one = jnp.ones((1, 1, nst), F32)

    l1re, l1im = powers(one)
    n_re = l1re - 1.0
    den = lr * lr + li * li
    co_re = (n_re * lr + l1im * li) / den
    co_im = (l1im * lr - n_re * li) / den
    btr = btr_ref[...]
    bti = bti_ref[...]
    bb_re = co_re * btr - co_im * bti
    bb_im = co_re * bti + co_im * btr

    pe_re, pe_im = powers(jnp.where(fwd, (CH - 1) - row, row))
    for l in range(CH):
        pr = pe_re[:, l:l + 1, :]
        pi = pe_im[:, l:l + 1, :]
        ere_ref[:, l * S5_H:(l + 1) * S5_H, :] = pr * bb_re - pi * bb_im
        eim_ref[:, l * S5_H:(l + 1) * S5_H, :] = pr * bb_im + pi * bb_re

    cr = cr_ref[...]
    ci = ci_ref[...]
    pc_re, pc_im = powers(jnp.where(fwd, row + 1.0, CH - row))
    for j in range(CH):
        pr = pc_re[:, j:j + 1, :]
        pi = pc_im[:, j:j + 1, :]
        w_re = cr * pr - ci * pi
        w_im = cr * pi + ci * pr
        cp_ref[:, j * S5_H:(j + 1) * S5_H, :] = jnp.concatenate([w_re, -w_im], axis=2).astype(BF16)

    l16re, l16im = powers(one * float(CH))
    l16_ref[...] = jnp.concatenate([l16re, l16im], axis=1)

    fwd2 = lax.broadcasted_iota(jnp.int32, (S5_H, nst), 1) < S5_P
    lane = lax.broadcasted_iota(jnp.int32, (S5_H, kk), 1)
    iblk = lane // S5_H
    hrow = lax.broadcasted_iota(jnp.int32, (S5_H, kk), 0)
    nt = (((1,), (1,)), ((), ()))
    for g in range(gb):
        ere = ere_ref[g]
        eim = eim_ref[g]
        bt_ref[g, 0:nst, :] = ere.T.astype(BF16)
        bt_ref[g, nst:2 * nst, :] = eim.T.astype(BF16)
        e_hi, e_lo = _split3(jnp.concatenate([ere, eim], axis=1))
        rhs = jnp.concatenate([e_hi, e_hi, e_lo], axis=1)
        crg = cr[g]
        cig = ci[g]

        def lag_kernels(sel):
            c2 = jnp.concatenate([jnp.where(sel, crg, 0.0), jnp.where(sel, -cig, 0.0)], axis=1)
            c_hi, c_lo = _split3(c2)
            lhs = jnp.concatenate([c_hi, c_lo, c_hi], axis=1)
            return lax.dot_general(lhs, rhs, nt, preferred_element_type=F32)

        kf = lag_kernels(fwd2)
        kb = lag_kernels(jnp.logical_not(fwd2))
        dg = d_ref[g]
        for j in range(CH):
            sf = (kk - (CH - 1 - j) * S5_H) % kk
            rf = pltpu.roll(kf, sf, 1) if sf else kf
            rb = pltpu.roll(kb, j * S5_H, 1) if j else kb
            blk = (jnp.where(iblk <= j, rf, 0.0) + jnp.where(iblk >= j, rb, 0.0)
                   + jnp.where(lane == j * S5_H + hrow, dg, 0.0))
            mt_ref[g, j * S5_H:(j + 1) * S5_H, :] = blk.astype(BF16)


def _s5_tables(lam_re, lam_im, log_step, b_re, b_im, c_re, c_im, d_skip):
    n = lam_re.shape[0] * S5_G
    gb = 8
    nst = 2 * S5_P
    kk = CH * S5_H

    def fb(v):
        return jnp.concatenate([v[:, 0], v[:, 1]], axis=-1).reshape(n, v.shape[3], nst)

    lr = fb(lam_re[:, :, :, None, :])
    li = fb(lam_im[:, :, :, None, :])
    ls = fb(jnp.broadcast_to(log_step[:, :, :, None, None], log_step.shape + (1, S5_P)))
    btr = fb(jnp.swapaxes(b_re, -1, -2))
    bti = fb(jnp.swapaxes(b_im, -1, -2))
    cr = fb(c_re)
    ci = fb(c_im)
    d = d_skip.reshape(n, S5_H, 1)

    def spec(r, c):
        return pl.BlockSpec((gb, r, c), lambda g: (g, 0, 0))

    return pl.pallas_call(
        functools.partial(_s5_tables_kernel, gb=gb),
        grid=(n // gb,),
        in_specs=[spec(1, nst), spec(1, nst), spec(1, nst), spec(S5_H, nst), spec(S5_H, nst),
                  spec(S5_H, nst), spec(S5_H, nst), spec(S5_H, 1)],
        out_specs=[spec(kk, kk), spec(2 * nst, kk), spec(kk, 2 * nst), spec(2, nst)],
        out_shape=[jax.ShapeDtypeStruct((n, kk, kk), BF16),
                   jax.ShapeDtypeStruct((n, 2 * nst, kk), BF16),
                   jax.ShapeDtypeStruct((n, kk, 2 * nst), BF16),
                   jax.ShapeDtypeStruct((n, 2, nst), F32)],
        scratch_shapes=[pltpu.VMEM((gb, kk, nst), F32)] * 2,
        compiler_params=_cparams(("arbitrary",)),
    )(lr, li, ls, btr, bti, cr, ci, d)


def _even_in_kernel(x_ref, shift_ref, scale_ref, g_ref, w_ref, wcs_ref, ut_ref, xw_ref, z_ref,
                    wat_ref, wbz_ref, *scr, ncl):
    @pl.when(pl.program_id(0) == 0)
    def _():
        wat_ref[...] = w_ref[:, :S5_W].T.astype(BF16)
        wbz_ref[...] = w_ref[:, S5_W:].astype(BF16)

    x = x_ref[...]
    nc, nb, _ = x.shape
    rows = nc * nb
    g = g_ref[...]
    h = jnp.concatenate([_mod_norm(x[0:ncl], g, scale_ref[0:1], shift_ref[0:1]),
                         _mod_norm(x[ncl:nc], g, scale_ref[1:2], shift_ref[1:2])], axis=0)
    hb = h.reshape(rows, D).astype(BF16)
    pt = lax.dot_general(wat_ref[...], hb, (((1,), (1,)), ((), ())), preferred_element_type=F32)
    ut_ref[...] = pt.astype(BF16).reshape(S5_G, S5_H, rows)
    p = jnp.dot(hb, wbz_ref[...], preferred_element_type=F32)
    z_ref[...] = _silu(p[:, FN_W:]).astype(BF16)
    xw = jnp.dot(p[:, :FN_W].astype(BF16), wcs_ref[...], preferred_element_type=F32)
    for q in range(4):
        scr[q][...] = xw[:, q * 128:(q + 1) * 128]
    for q in range(4):
        part, half = divmod(q, 2)
        for bi in range(nb):
            piece = scr[q][pl.ds(bi, nc, stride=nb), :]
            lo = bi * FN_W + half * 128
            xw_ref[part, :, lo:lo + 128] = piece.astype(BF16)


def _even_in(xs, ncl, shift, scale, g, w_in, layer, wcs):
    nc, bsz, _ = xs.shape
    rows = nc * bsz
    mod = pl.BlockSpec((2, bsz, D), lambda i: (0, 0, 0))
    return pl.pallas_call(
        functools.partial(_even_in_kernel, ncl=ncl),
        grid=(CH,),
        in_specs=[pl.BlockSpec((nc, bsz, D), lambda i: (0, 0, i)), mod, mod,
                  pl.BlockSpec((1, D), lambda i: (0, 0)),
                  pl.BlockSpec((None, D, w_in.shape[2]), lambda i: (layer, 0, 0),
                               pipeline_mode=pl.Buffered(1)),
                  pl.BlockSpec((FN_W, 2 * FN_W), lambda i: (0, 0))],
        out_specs=[pl.BlockSpec((S5_G, S5_H, rows), lambda i: (0, i, 0)),
                   pl.BlockSpec((2, None, nc, bsz * FN_W), lambda i: (0, i, 0, 0)),
                   pl.BlockSpec((None, rows, MIX), lambda i: (i, 0, 0))],
        out_shape=[jax.ShapeDtypeStruct((S5_G, CH * S5_H, rows), BF16),
                   jax.ShapeDtypeStruct((2, CH, nc, bsz * FN_W), BF16),
                   jax.ShapeDtypeStruct((CH, rows, MIX), BF16)],
        scratch_shapes=[pltpu.VMEM((S5_W, D), BF16), pltpu.VMEM((D, FN_W + MIX), BF16)]
        + [pltpu.VMEM((rows, 128), F32)] * 4,
        compiler_params=_cparams(("arbitrary",)),
    )(xs, shift, scale, g.reshape(1, D), w_in, wcs)


def _s5_kernel(ut_ref, mt_ref, bt_ref, cp_ref, l16_ref, yt_ref,
               sre_ref, sim_ref, are_ref, aim_ref, bre_ref, bim_ref, *, bsz, ncl, ncc, gs):
    nl = bsz * ncl
    nst = 2 * S5_P
    for g in range(gs):
        st = jnp.dot(bt_ref[g], ut_ref[g], preferred_element_type=F32).T
        sre_ref[g] = st[:, :nst]
        sim_ref[g] = st[:, nst:]

    lam = [l16_ref[g] for g in range(gs)]
    is_fwd = lax.broadcasted_iota(jnp.int32, (bsz, nst), 1) < S5_P

    def make_step(base, nchunk):
        def step(c, carry):
            rf = pl.ds(pl.multiple_of(base + c * bsz, bsz), bsz)
            rb = pl.ds(pl.multiple_of(base + (nchunk - 1 - c) * bsz, bsz), bsz)
            out = []
            for g in range(gs):
                sre, sim = carry[2 * g], carry[2 * g + 1]
                lre = lam[g][0:1, :]
                lim = lam[g][1:2, :]
                are_ref[g, rf, :] = sre
                aim_ref[g, rf, :] = sim
                bre_ref[g, rb, :] = sre
                bim_ref[g, rb, :] = sim
                in_re = jnp.where(is_fwd, sre_ref[g, rf, :], sre_ref[g, rb, :])
                in_im = jnp.where(is_fwd, sim_ref[g, rf, :], sim_ref[g, rb, :])
                out.append(lre * sre - lim * sim + in_re)
                out.append(lre * sim + lim * sre + in_im)
            return tuple(out)
        return step

    zero = jnp.zeros((bsz, nst), F32)
    carry = lax.fori_loop(0, ncc, make_step(nl, ncc), (zero,) * (2 * gs))
    lax.fori_loop(0, ncl, make_step(0, ncl), carry)

    nt = (((1,), (1,)), ((), ()))
    fwd_rows = lax.broadcasted_iota(jnp.int32, (bsz * (ncl + ncc), nst), 1) < S5_P
    for g in range(gs):
        s0 = jnp.concatenate([jnp.where(fwd_rows, are_ref[g], bre_ref[g]),
                              jnp.where(fwd_rows, aim_ref[g], bim_ref[g])], axis=1).astype(BF16)
        yt_ref[g] = (jnp.dot(mt_ref[g], ut_ref[g], preferred_element_type=F32)
                     + lax.dot_general(cp_ref[g], s0, nt, preferred_element_type=F32)).astype(BF16)


def _s5_mix(ut, ncl, mt, bend_t, cp, lam16, bsz, layer):
    rows = ut.shape[2]
    kk = CH * S5_H
    nst = 2 * S5_P
    gs = 4
    off = layer * (S5_G // gs)

    def gspec(r, c):
        return pl.BlockSpec((gs, r, c), lambda g: (g, 0, 0))

    def tspec(r, c):
        return pl.BlockSpec((gs, r, c), lambda g: (g + off, 0, 0))

    return pl.pallas_call(
        functools.partial(_s5_kernel, bsz=bsz, ncl=ncl, ncc=rows // bsz - ncl, gs=gs),
        grid=(S5_G // gs,),
        in_specs=[gspec(kk, rows), tspec(kk, kk), tspec(2 * nst, kk), tspec(kk, 2 * nst), tspec(2, nst)],
        out_specs=gspec(kk, rows),
        out_shape=jax.ShapeDtypeStruct((S5_G, kk, rows), BF16),
        scratch_shapes=[pltpu.VMEM((gs, rows, nst), F32)] * 6,
        compiler_params=_cparams(("arbitrary",)),
    )(ut, mt, bend_t, cp, lam16)


def _fnet_weights_kernel(ccs_ref, fw_ref, o_ref):
    fw = fw_ref[...]
    ccs = ccs_ref[...]
    wc = jnp.dot(ccs[:, :FN_W], fw, precision=HI, preferred_element_type=F32)
    ws = jnp.dot(ccs[:, FN_W:], fw, precision=HI, preferred_element_type=F32)
    o_ref[...] = jnp.concatenate([wc, ws], axis=1).astype(BF16)


def _fnet_weights(fw_bd):
    return pl.pallas_call(
        _fnet_weights_kernel,
        out_shape=jax.ShapeDtypeStruct((FN_W, 2 * FN_W), BF16),
    )(_dft_channel_matrix(), fw_bd)


def _fnet_kernel(tab_ref, v_ref, jr_ref, o_ref, *, ipt):
    r = pl.program_id(1)
    tab = tab_ref[...].astype(BF16)
    v = v_ref[...]
    nc, n = v.shape[2], v.shape[3]
    t = CH * nc
    a = jnp.dot(tab[:, :t], v[0].reshape(t, n), preferred_element_type=F32)
    b = jnp.dot(tab[:, t:], v[1].reshape(t, n), preferred_element_type=F32)
    o_ref[pl.ds(r * ipt, ipt)] = (a - b).reshape(ipt, nc, n).astype(BF16)
    m = (a + b).astype(BF16)
    for k in range(ipt):
        i = r * ipt + k

        @pl.when(jnp.logical_and(i >= 1, i <= CH // 2 - 1))
        def _():
            o_ref[CH - i] = jnp.dot(jr_ref[...], m[k * nc:(k + 1) * nc],
                                    preferred_element_type=F32).astype(BF16)


def _dft_half_table(nc):
    t = nc * CH
    cols = (np.arange(nc)[None, :] * CH + np.arange(CH)[:, None]).reshape(-1)
    rows = (np.arange(nc)[None, :] * CH + np.arange(CH // 2 + 1)[:, None]).reshape(-1)
    prod = (rows[:, None].astype(np.int64) * cols[None, :].astype(np.int64)) % t
    ang = prod.astype(np.float64) * (2.0 * np.pi / t)
    scale = 1.0 / math.sqrt(t * FN_GW)
    return jnp.asarray(np.concatenate([np.cos(ang), np.sin(ang)], axis=1) * scale, dtype=F32)


def _dft_channel_matrix():
    c = np.arange(FN_GW)
    ang = (c[:, None] * c[None, :] % FN_GW).astype(np.float64) * (2.0 * np.pi / FN_GW)
    eye = np.eye(FN_G)
    return jnp.asarray(np.concatenate([np.kron(eye, np.cos(ang)), np.kron(eye, np.sin(ang))], axis=1),
                       dtype=F32)


def _fnet(xw, c0, nc, bsz):
    t = nc * CH
    nhb = CH // 2 + 1
    ipt = 3
    bpb = 4
    jr = jnp.asarray(np.eye(nc)[::-1], dtype=BF16)
    return pl.pallas_call(
        functools.partial(_fnet_kernel, ipt=ipt),
        grid=(bsz // bpb, nhb // ipt),
        in_specs=[pl.BlockSpec((ipt * nc, 2 * t), lambda b, r: (r, 0)),
                  pl.BlockSpec((2, CH, nc, bpb * FN_W), lambda b, r: (0, 0, c0 // nc, b)),
                  pl.BlockSpec((nc, nc), lambda b, r: (0, 0))],
        out_specs=pl.BlockSpec((CH, nc, bpb * FN_W), lambda b, r: (0, 0, b)),
        out_shape=jax.ShapeDtypeStruct((CH, nc, bsz * FN_W), BF16),
        compiler_params=_cparams(("arbitrary", "arbitrary")),
    )(_dft_half_table(nc), xw, jr)


def _even_out_kernel(*refs, ncl, with_ctx):
    it = iter(refs)
    yt_ref, ybl_ref = next(it), next(it)
    ybc_ref = next(it) if with_ctx else None
    z_ref, x_ref, gate_ref, gwf_ref, gb_ref, wof_ref, o_ref = (next(it), next(it), next(it), next(it),
                                                               next(it), next(it), next(it))
    scr = [next(it), next(it)]
    gw_ref, wo_ref = next(it), next(it)

    @pl.when(pl.program_id(0) == 0)
    def _():
        gw_ref[...] = gwf_ref[...].astype(BF16)
        wo_ref[...] = wof_ref[...].astype(BF16)

    yt = yt_ref[...]
    rows = yt.shape[2]
    ya = _gelu_tanh(yt.astype(F32).reshape(S5_W, rows).T)
    glu = jnp.dot(ya.astype(BF16), gw_ref[...], preferred_element_type=F32) + gb_ref[...]
    ya = ya * _sigmoid(glu)
    x = x_ref[...]
    nc, nb, _ = x.shape
    for half in range(2):
        for bi in range(nb):
            lo = bi * FN_W + half * 128
            scr[half][pl.ds(bi, ncl, stride=nb), :] = ybl_ref[:, lo:lo + 128].astype(F32)
            if with_ctx:
                scr[half][pl.ds(ncl * nb + bi, nc - ncl, stride=nb), :] = (
                    ybc_ref[:, lo:lo + 128].astype(F32))
    yb = jnp.concatenate([scr[0][...], scr[1][...]], axis=1)
    sz = z_ref[...]
    ma = (ya * sz[:, :S5_W]).astype(BF16)
    mb = (yb * sz[:, S5_W:]).astype(BF16)
    out = (jnp.dot(ma, wo_ref[0:S5_W, :], preferred_element_type=F32)
           + jnp.dot(mb, wo_ref[S5_W:MIX, :], preferred_element_type=F32)).reshape(x.shape)
    o_ref[0:ncl] = x[0:ncl] + gate_ref[0:1] * out[0:ncl]
    if with_ctx:
        o_ref[ncl:nc] = x[ncl:nc] + gate_ref[1:2] * out[ncl:nc]


def _even_out(yt, ybl, ybc, z, xs, ncl, gate, glu_w, glu_b, w_out, layer):
    bsz = xs.shape[1]
    with_ctx = ybc is not None
    nc = xs.shape[0] if with_ctx else ncl
    rows = nc * bsz
    args = [yt, ybl]
    specs = [pl.BlockSpec((S5_G, S5_H, rows), lambda j: (0, j, 0)),
             pl.BlockSpec((None, ncl, bsz * FN_W), lambda j: (j, 0, 0))]
    if with_ctx:
        args.append(ybc)
        specs.append(pl.BlockSpec((None, nc - ncl, bsz * FN_W), lambda j: (j, 0, 0)))
    args += [z, xs, gate, glu_w, glu_b.reshape(1, S5_W), w_out]
    specs += [pl.BlockSpec((None, rows, MIX), lambda j: (j, 0, 0)),
              pl.BlockSpec((nc, bsz, D), lambda j: (0, 0, j)),
              pl.BlockSpec((2, bsz, D), lambda j: (0, 0, 0)),
              pl.BlockSpec((None, S5_W, S5_W), lambda j: (layer, 0, 0), pipeline_mode=pl.Buffered(1)),
              pl.BlockSpec((1, S5_W), lambda j: (0, 0)),
              pl.BlockSpec((None, MIX, D), lambda j: (layer, 0, 0), pipeline_mode=pl.Buffered(1))]
    return pl.pallas_call(
        functools.partial(_even_out_kernel, ncl=ncl, with_ctx=with_ctx),
        grid=(CH,),
        in_specs=specs,
        out_specs=pl.BlockSpec((nc, bsz, D), lambda j: (0, 0, j)),
        out_shape=jax.ShapeDtypeStruct((nc, bsz, CH * D), F32),
        scratch_shapes=[pltpu.VMEM((rows, 128), F32)] * 2 + [pltpu.VMEM((S5_W, S5_W), BF16),
                                                             pltpu.VMEM((MIX, D), BF16)],
        compiler_params=_cparams(("arbitrary",)),
    )(*args)


def _odd_kernel(*refs, ct, ntl, t_lat, t_ctx, final):
    it = iter(refs)
    xm_ref, xp_ref, xn_ref = next(it), next(it), next(it)
    shift_ref, scale_ref, gate_ref, g_ref = next(it), next(it), next(it), next(it)
    wif_ref, wof_ref, pwf_ref, ps_ref, cw_ref = next(it), next(it), next(it), next(it), next(it)
    fg_ref = next(it) if final else None
    o_ref = next(it)
    h_ref = next(it)
    pe_ref = next(it)
    wi_ref, wo_ref, pw_ref = next(it), next(it), next(it)
    fin_ref = next(it) if final else None

    ti = pl.program_id(0)
    is_ctx = ti >= ntl
    t0 = jnp.where(is_ctx, ti - ntl, ti) * (ct * CH)
    t_total = jnp.where(is_ctx, t_ctx, t_lat)

    @pl.when(ti == 0)
    def _():
        wi_ref[...] = wif_ref[...].astype(BF16)
        wo_ref[...] = wof_ref[...].astype(BF16)
        pw_ref[...] = pwf_ref[...].astype(BF16)

    n1 = POOL_W + 2 * CONV_W
    nb = xm_ref.shape[1]
    tm = ct * CH
    ne = tm + 2 * HALO

    def hn(xv):
        return _mod_norm(xv, g_ref[...], scale_ref[...], shift_ref[...])

    for i in range(CH):
        hi = hn(xm_ref[:, :, i * D:(i + 1) * D])
        if i < HALO:
            h_ref[0:ct, i + HALO] = hi
        else:
            h_ref[1:ct + 1, i - HALO] = hi
    for i in range(HALO):
        h_ref[0, i] = hn(xp_ref[:, :, (i + HALO) * D:(i + HALO + 1) * D])[0]
        h_ref[ct, i + HALO] = hn(xn_ref[:, :, i * D:(i + 1) * D])[0]

    he = h_ref[...].reshape(ne * nb, D).astype(BF16)
    pe = jnp.dot(he, wi_ref[:, 0:n1], preferred_element_type=F32)
    te = t0 - HALO + lax.broadcasted_iota(jnp.int32, (ne * nb, 1), 0) // nb
    valid = jnp.logical_and(te >= 0, te < t_total)
    pe_ref[:, 0:POOL_W] = jnp.where(valid, pe[:, :POOL_W], 0.0)
    pe_ref[:, POOL_W:POOL_W + CONV_W] = jnp.where(
        valid, pe[:, POOL_W:POOL_W + CONV_W] * pe[:, POOL_W + CONV_W:], 0.0)

    m0 = HALO * nb
    mr = tm * nb
    p2 = jnp.dot(he[m0:m0 + mr], wi_ref[:, n1:], preferred_element_type=F32)
    b_gate = p2[:, :CONV_W]
    sz = _silu(p2[:, CONV_W:])

    tpos = t0 + lax.broadcasted_iota(jnp.int32, (mr, 1), 0) // nb
    pooled = []
    for gi, w in enumerate(POOL_WINDOWS):
        c0 = gi * POOL_GW
        s = pe_ref[:, c0:c0 + POOL_GW]
        n = ne
        width = 1
        while width < w:
            s = s[0:(n - width) * nb] + s[width * nb:n * nb]
            n -= width
            width *= 2
        start = (HALO - w // 2) * nb
        total = s[start:start + mr]
        hi = jnp.minimum(tpos + w // 2, t_total)
        lo = jnp.maximum(tpos - w // 2, 0)
        cnt = (hi - lo).astype(F32)
        centre = pe_ref[m0:m0 + mr, c0:c0 + POOL_GW]
        pg = total / cnt - centre
        pooled.append(jnp.dot(pg.astype(BF16), pw_ref[gi], preferred_element_type=F32))
    y_c = jnp.concatenate(pooled, axis=1) * ps_ref[...]

    cwt = cw_ref[...]
    vm = pe_ref[m0 - nb:m0 - nb + mr, POOL_W:POOL_W + CONV_W]
    v0 = pe_ref[m0:m0 + mr, POOL_W:POOL_W + CONV_W]
    vp = pe_ref[m0 + nb:m0 + nb + mr, POOL_W:POOL_W + CONV_W]
    y_d = b_gate * (vm * cwt[0:1, :] + v0 * cwt[1:2, :] + vp * cwt[2:3, :])

    y = (jnp.concatenate([y_c, y_d], axis=1) * sz).astype(BF16)
    out = jnp.dot(y, wo_ref[...], preferred_element_type=F32)
    go = (gate_ref[...] * out.reshape(tm, nb, D)).reshape(ct, CH, nb, D)
    for i in range(CH):
        xo = xm_ref[:, :, i * D:(i + 1) * D] + go[:, i]
        if not final:
            o_ref[:, :, i * D:(i + 1) * D] = xo
            continue
        ms = jnp.mean(xo * xo, axis=-1, keepdims=True)
        fin_ref[:, :, i * D:(i + 1) * D] = xo * lax.rsqrt(ms + EPS) * fg_ref[...]
    if final:
        o_ref[...] = pltpu.einshape("cb(id)->b(ci)d", fin_ref[...], i=CH)


def _odd_layer(xs, ncl, with_ctx, shift, scale, gate, g, w_in, w_out, pool_w, layer, pool_scale, conv_w,
               final_g):
    nc_all, bsz, _ = xs.shape
    ct = 8
    ntl = ncl // ct
    nc = nc_all if with_ctx else ncl
    final = final_g is not None
    args = [xs, xs, xs, shift, scale, gate, g.reshape(1, D), w_in, w_out, pool_w,
            pool_scale.reshape(1, POOL_W), conv_w]
    vec = pl.BlockSpec((1, bsz, D), lambda i: (jnp.minimum(i // ntl, 1), 0, 0))

    def const(*shape):
        return pl.BlockSpec(shape, lambda i: (0,) * len(shape), pipeline_mode=pl.Buffered(1))

    def stacked(*shape):
        return pl.BlockSpec((None,) + shape, lambda i: (layer,) + (0,) * len(shape),
                            pipeline_mode=pl.Buffered(1))

    specs = [pl.BlockSpec((ct, bsz, CH * D), lambda i: (i, 0, 0)),
             pl.BlockSpec((1, bsz, CH * D), lambda i: (jnp.maximum(i * ct - 1, 0), 0, 0)),
             pl.BlockSpec((1, bsz, CH * D), lambda i: (jnp.minimum((i + 1) * ct, nc_all - 1), 0, 0)),
             vec, vec, vec,
             const(1, D), stacked(*w_in.shape[1:]), stacked(MIX, D),
             stacked(len(POOL_WINDOWS), POOL_GW, POOL_GW), const(1, POOL_W), const(3, CONV_W)]
    scratch = [pltpu.VMEM((ct + 1, CH, bsz, D), F32),
               pltpu.VMEM(((ct + 1) * CH * bsz, POOL_W + CONV_W), F32),
               pltpu.VMEM(w_in.shape[1:], BF16), pltpu.VMEM((MIX, D), BF16),
               pltpu.VMEM((len(POOL_WINDOWS), POOL_GW, POOL_GW), BF16)]
    if final:
        args.append(final_g.reshape(1, D))
        specs.append(const(1, D))
        scratch.append(pltpu.VMEM((ct, bsz, CH * D), F32))
        out_spec = pl.BlockSpec((bsz, ct * CH, D), lambda i: (0, i, 0))
        out_shape = jax.ShapeDtypeStruct((bsz, nc * CH, D), F32)
    else:
        out_spec = pl.BlockSpec((ct, bsz, CH * D), lambda i: (i, 0, 0))
        out_shape = jax.ShapeDtypeStruct((nc, bsz, CH * D), F32)
    return pl.pallas_call(
        functools.partial(_odd_kernel, ct=ct, ntl=ntl, t_lat=ncl * CH, t_ctx=(nc_all - ncl) * CH,
                          final=final),
        grid=(nc // ct,),
        in_specs=specs,
        out_specs=out_spec,
        out_shape=out_shape,
        scratch_shapes=scratch,
        compiler_params=_cparams(("arbitrary",)),
    )(*args)


def _sincos_table(n_tok, dim):
    rows = n_tok // GRID_W
    rr, cc = np.meshgrid(np.arange(rows, dtype=np.float64), np.arange(GRID_W, dtype=np.float64),
                         indexing='ij')
    rr = rr.reshape(-1, 1)
    cc = cc.reshape(-1, 1)
    quarter = dim // 4
    omega = POS_BASE ** (-np.arange(quarter, dtype=np.float64) / quarter)
    tab = np.concatenate([np.sin(rr * omega), np.cos(rr * omega), np.sin(cc * omega), np.cos(cc * omega)],
                         axis=-1)
    return jnp.asarray(tab, dtype=F32)


def _block_diag(w):
    g, c, _ = w.shape
    eye = jnp.eye(g, dtype=w.dtype)
    return (eye[:, None, :, None] * w[:, :, None, :]).reshape(g * c, g * c)


def _to_stream_kernel(x_ref, pos_ref, c_ref, o_ref, *, ntl):
    step = pl.program_id(0)

    @pl.when(step < ntl)
    def _():
        o_ref[...] = pltpu.einshape("b(ci)d->cb(id)", x_ref[...] + pos_ref[...][None], i=CH)

    @pl.when(step >= ntl)
    def _():
        o_ref[...] = pltpu.einshape("b(ci)d->cb(id)", c_ref[...], i=CH)


def _to_stream(x, pos, ctx):
    bsz, t, _ = x.shape
    tc = ctx.shape[1]
    ct = 8
    tm = ct * CH
    ntl = t // tm
    return pl.pallas_call(
        functools.partial(_to_stream_kernel, ntl=ntl),
        grid=((t + tc) // tm,),
        in_specs=[pl.BlockSpec((bsz, tm, D), lambda i: (0, jnp.minimum(i, ntl - 1), 0)),
                  pl.BlockSpec((tm, D), lambda i: (jnp.minimum(i, ntl - 1), 0)),
                  pl.BlockSpec((bsz, tm, D), lambda i: (0, jnp.maximum(i - ntl, 0), 0))],
        out_specs=pl.BlockSpec((ct, bsz, CH * D), lambda i: (i, 0, 0)),
        out_shape=jax.ShapeDtypeStruct(((t + tc) // CH, bsz, CH * D), F32),
        compiler_params=_cparams(("arbitrary",)),
    )(x, pos, ctx)


def kernel(x, c, ctx, c_ctx, norm_g, ada_w, ada_b, even_w_in, even_w_out, s5_lam_re, s5_lam_im, s5_log_step, s5_b_re, s5_b_im, s5_c_re, s5_c_im, s5_d, s5_glu_w, s5_glu_b, fnet_w, odd_w_in, odd_w_out, pool_w, pool_scale, conv_w, final_g):
    bsz, n_tok, _ = x.shape
    depth = norm_g.shape[0]
    ncl = n_tok // CH
    ncc = ctx.shape[1] // CH

    cond = jnp.concatenate([c, jnp.broadcast_to(c_ctx[None], (16 - bsz, D))], axis=0)
    ada = _ada_all(cond, ada_w, ada_b)

    def mod(l, j):
        return jnp.stack([ada[l, j, :bsz], jnp.broadcast_to(ada[l, j, bsz], (bsz, D))])

    mt, bend_t, cp, lam16 = _s5_tables(s5_lam_re, s5_lam_im, s5_log_step, s5_b_re, s5_b_im, s5_c_re,
                                       s5_c_im, s5_d)
    need_ctx = [any(j % 2 == 0 for j in range(l + 1, depth)) for l in range(depth)]

    xs = _to_stream(x, _sincos_table(n_tok, D), ctx)
    for l in range(depth):
        i = l // 2
        last = l == depth - 1
        if l % 2 == 0:
            wcs = _fnet_weights(_block_diag(fnet_w[i]))
            ut, xw, z = _even_in(xs, ncl, mod(l, 0), mod(l, 1), norm_g[l], even_w_in, i, wcs)
            yt = _s5_mix(ut, ncl, mt, bend_t, cp, lam16, bsz, i)
            ybl = _fnet(xw, 0, ncl, bsz)
            ybc = _fnet(xw, ncl, ncc, bsz) if need_ctx[l] else None
            xs = _even_out(yt, ybl, ybc, z, xs, ncl, mod(l, 2), s5_glu_w, s5_glu_b[i], even_w_out, i)
        else:
            xs = _odd_layer(xs, ncl, need_ctx[l], mod(l, 0), mod(l, 1), mod(l, 2), norm_g[l], odd_w_in,
                            odd_w_out, pool_w, i, pool_scale[i], conv_w[i], final_g if last else None)
    if depth % 2 == 1:
        raise NotImplementedError("final norm and (B, T, D) order are produced by the last (odd) layer")
    return xs
```

```python
import functools
import math

import numpy as np
import jax
import jax.numpy as jnp
from jax import lax
from jax.experimental import pallas as pl
from jax.experimental.pallas import tpu as pltpu

D = 1024
MIX = 1024
S5_W = 768
FN_W = 256
S5_H = 16
S5_G = 48
S5_P = 64
FN_G = 4
FN_GW = 64
POOL_W = 512
CONV_W = 512
POOL_WINDOWS = (2, 4, 8, 16)
POOL_GW = 128
GRID_W = 64
EPS = 1e-6
POS_BASE = 10000.0
CH = 16
HALO = 8
VMEM_LIMIT = 56 * 1024 * 1024

F32 = jnp.float32
BF16 = jnp.bfloat16
HI = lax.Precision.HIGHEST


def _cparams(sem):
    return pltpu.CompilerParams(dimension_semantics=sem, vmem_limit_bytes=VMEM_LIMIT)


def _sigmoid(v):
    return 0.5 * jnp.tanh(0.5 * v) + 0.5


def _silu(v):
    return v * _sigmoid(v)


def _gelu_tanh(v):
    c = math.sqrt(2.0 / math.pi)
    return 0.5 * v * (1.0 + jnp.tanh(c * (v + 0.044715 * (v * v * v))))


def _mod_norm(x, g, scale, shift):
    ms = jnp.mean(x * x, axis=-1, keepdims=True)
    y = x * lax.rsqrt(ms + EPS) * g
    return y * (1.0 + scale) + shift


def _split3(v):
    hi = v.astype(BF16)
    lo = (v - hi.astype(F32)).astype(BF16)
    return hi, lo


def _prep_kernel(c_ref, w_ref, b_ref, lr_ref, li_ref, ls_ref, btr_ref, bti_ref, cr_ref, ci_ref, d_ref,
                 ada_ref, mt_ref, be_ref, cp_ref, l16_ref, ere_ref, eim_ref, *, gb):
    s_hi, s_lo = _split3(_silu(c_ref[...]))
    w_hi, w_lo = _split3(w_ref[...])
    ada_ref[...] = (jnp.dot(s_hi, w_hi, preferred_element_type=F32)
                    + jnp.dot(s_hi, w_lo, preferred_element_type=F32)
                    + jnp.dot(s_lo, w_hi, preferred_element_type=F32) + b_ref[...])

    nst = 2 * S5_P
    kk = CH * S5_H
    step = jnp.exp(ls_ref[...])
    lr = lr_ref[...]
    li = li_ref[...]
    a = lr * step
    b = li * step

    def powers(expo):
        mag = jnp.exp(expo * a)
        return mag * jnp.cos(expo * b), mag * jnp.sin(expo * b)

    row = lax.broadcasted_iota(jnp.int32, (1, CH, nst), 1).astype(F32)
    fwd = lax.broadcasted_iota(jnp.int32, (1, CH, nst), 2) < S5_P
    one = jnp.ones((1, 1, nst), F32)

    l1re, l1im = powers(one)
    n_re = l1re - 1.0
    den = lr * lr + li * li
    co_re = (n_re * lr + l1im * li) / den
    co_im = (l1im * lr - n_re * li) / den
    btr = btr_ref[...]
    bti = bti_ref[...]
    bb_re = co_re * btr - co_im * bti
    bb_im = co_re * bti + co_im * btr

    pe_re, pe_im = powers(jnp.where(fwd, (CH - 1) - row, row))
    for l in range(CH):
        pr = pe_re[:, l:l + 1, :]
        pi = pe_im[:, l:l + 1, :]
        ere_ref[:, l * S5_H:(l + 1) * S5_H, :] = pr * bb_re - pi * bb_im
        eim_ref[:, l * S5_H:(l + 1) * S5_H, :] = pr * bb_im + pi * bb_re

    cr = cr_ref[...]
    ci = ci_ref[...]
    pc_re, pc_im = powers(jnp.where(fwd, row + 1.0, CH - row))
    for j in range(CH):
        pr = pc_re[:, j:j + 1, :]
        pi = pc_im[:, j:j + 1, :]
        w_re = cr * pr - ci * pi
        w_im = cr * pi + ci * pr
        cp_ref[:, j * S5_H:(j + 1) * S5_H, :] = jnp.concatenate([w_re, -w_im], axis=2).astype(BF16)

    l16re, l16im = powers(one * float(CH))
    l16_ref[...] = jnp.concatenate([l16re, l16im], axis=1)

    fwd2 = lax.broadcasted_iota(jnp.int32, (S5_H, nst), 1) < S5_P
    lane = lax.broadcasted_iota(jnp.int32, (S5_H, kk), 1)
    iblk = lane // S5_H
    hrow = lax.broadcasted_iota(jnp.int32, (S5_H, kk), 0)
    nt = (((1,), (1,)), ((), ()))
    for g in range(gb):
        ere = ere_ref[g]
        eim = eim_ref[g]
        e2 = jnp.concatenate([ere, eim], axis=1)
        be_ref[g] = e2.astype(BF16)
        e_hi, e_lo = _split3(e2)
        rhs = jnp.concatenate([e_hi, e_hi, e_lo], axis=1)
        crg = cr[g]
        cig = ci[g]

        def lag_kernels(sel):
            c2 = jnp.concatenate([jnp.where(sel, crg, 0.0), jnp.where(sel, -cig, 0.0)], axis=1)
            c_hi, c_lo = _split3(c2)
            lhs = jnp.concatenate([c_hi, c_lo, c_hi], axis=1)
            return lax.dot_general(lhs, rhs, nt, preferred_element_type=F32)

        kf = lag_kernels(fwd2)
        kb = lag_kernels(jnp.logical_not(fwd2))
        dg = d_ref[g]
        for j in range(CH):
            sf = (kk - (CH - 1 - j) * S5_H) % kk
            rf = pltpu.roll(kf, sf, 1) if sf else kf
            rb = pltpu.roll(kb, j * S5_H, 1) if j else kb
            blk = (jnp.where(iblk <= j, rf, 0.0) + jnp.where(iblk >= j, rb, 0.0)
                   + jnp.where(lane == j * S5_H + hrow, dg, 0.0))
            mt_ref[g, j * S5_H:(j + 1) * S5_H, :] = blk.astype(BF16)


def _prepare(cond, ada_w, ada_b, lam_re, lam_im, log_step, b_re, b_im, c_re, c_im, d_skip):
    n = lam_re.shape[0] * S5_G
    gb = 8
    nst = 2 * S5_P
    kk = CH * S5_H

    def fb(v):
        return jnp.concatenate([v[:, 0], v[:, 1]], axis=-1).reshape(n, v.shape[3], nst)

    lr = fb(lam_re[:, :, :, None, :])
    li = fb(lam_im[:, :, :, None, :])
    ls = fb(jnp.broadcast_to(log_step[:, :, :, None, None], log_step.shape + (1, S5_P)))
    btr = fb(jnp.swapaxes(b_re, -1, -2))
    bti = fb(jnp.swapaxes(b_im, -1, -2))
    cr = fb(c_re)
    ci = fb(c_im)
    d = d_skip.reshape(n, S5_H, 1)

    depth = ada_w.shape[0]
    n_ada = depth * 3
    n_tab = n // gb

    def spec(r, c):
        return pl.BlockSpec((gb, r, c), lambda s: (jnp.minimum(s, n_tab - 1), 0, 0))

    def ada_idx(s):
        s = jnp.minimum(s, n_ada - 1)
        return s // 3, s % 3

    return pl.pallas_call(
        functools.partial(_prep_kernel, gb=gb),
        grid=(max(n_ada, n_tab),),
        in_specs=[pl.BlockSpec((16, D), lambda s: (0, 0)),
                  pl.BlockSpec((None, D, D), lambda s: (ada_idx(s)[0], 0, ada_idx(s)[1])),
                  pl.BlockSpec((None, None, 1, D), lambda s: ada_idx(s) + (0, 0)),
                  spec(1, nst), spec(1, nst), spec(1, nst), spec(S5_H, nst), spec(S5_H, nst),
                  spec(S5_H, nst), spec(S5_H, nst), spec(S5_H, 1)],
        out_specs=[pl.BlockSpec((None, None, 16, D), lambda s: ada_idx(s) + (0, 0)),
                   spec(kk, kk), spec(kk, 2 * nst), spec(kk, 2 * nst), spec(2, nst)],
        out_shape=[jax.ShapeDtypeStruct((depth, 3, 16, D), F32),
                   jax.ShapeDtypeStruct((n, kk, kk), BF16),
                   jax.ShapeDtypeStruct((n, kk, 2 * nst), BF16),
                   jax.ShapeDtypeStruct((n, kk, 2 * nst), BF16),
                   jax.ShapeDtypeStruct((n, 2, nst), F32)],
        scratch_shapes=[pltpu.VMEM((gb, kk, nst), F32)] * 2,
        compiler_params=_cparams(("arbitrary",)),
    )(cond, ada_w, ada_b.reshape(depth, 3, 1, D), lr, li, ls, btr, bti, cr, ci, d)


def _even_in_kernel(x_ref, shift_ref, scale_ref, g_ref, w_ref, wcs_ref, ut_ref, xw_ref, z_ref,
                    wat_ref, wbz_ref, *scr, ncl):
    @pl.when(pl.program_id(0) == 0)
    def _():
        wat_ref[...] = w_ref[:, :S5_W].T.astype(BF16)
        wbz_ref[...] = w_ref[:, S5_W:].astype(BF16)

    x = x_ref[...]
    nc, nb, _ = x.shape
    rows = nc * nb
    g = g_ref[...]
    h = jnp.concatenate([_mod_norm(x[0:ncl], g, scale_ref[0:1], shift_ref[0:1]),
                         _mod_norm(x[ncl:nc], g, scale_ref[1:2], shift_ref[1:2])], axis=0)
    hb = h.reshape(rows, D).astype(BF16)
    pt = lax.dot_general(wat_ref[...], hb, (((1,), (1,)), ((), ())), preferred_element_type=F32)
    ut_ref[...] = pt.astype(BF16).reshape(S5_G, S5_H, rows)
    p = jnp.dot(hb, wbz_ref[...], preferred_element_type=F32)
    z_ref[...] = _silu(p[:, FN_W:]).astype(BF16)
    xw = jnp.dot(p[:, :FN_W].astype(BF16), wcs_ref[...], preferred_element_type=F32)
    for q in range(4):
        scr[q][...] = xw[:, q * 128:(q + 1) * 128]
    for q in range(4):
        part, half = divmod(q, 2)
        for bi in range(nb):
            piece = scr[q][pl.ds(bi, nc, stride=nb), :]
            lo = bi * FN_W + half * 128
            xw_ref[part, :, lo:lo + 128] = piece.astype(BF16)


def _even_in(xs, ncl, shift, scale, g, w_in, layer, wcs):
    nc, bsz, _ = xs.shape
    rows = nc * bsz
    mod = pl.BlockSpec((2, bsz, D), lambda i: (0, 0, 0))
    return pl.pallas_call(
        functools.partial(_even_in_kernel, ncl=ncl),
        grid=(CH,),
        in_specs=[pl.BlockSpec((nc, bsz, D), lambda i: (0, 0, i)), mod, mod,
                  pl.BlockSpec((1, D), lambda i: (0, 0)),
                  pl.BlockSpec((None, D, w_in.shape[2]), lambda i: (layer, 0, 0),
                               pipeline_mode=pl.Buffered(1)),
                  pl.BlockSpec((FN_W, 2 * FN_W), lambda i: (0, 0))],
        out_specs=[pl.BlockSpec((S5_G, S5_H, rows), lambda i: (0, i, 0)),
                   pl.BlockSpec((2, None, nc, bsz * FN_W), lambda i: (0, i, 0, 0)),
                   pl.BlockSpec((None, rows, MIX), lambda i: (i, 0, 0))],
        out_shape=[jax.ShapeDtypeStruct((S5_G, CH * S5_H, rows), BF16),
                   jax.ShapeDtypeStruct((2, CH, nc, bsz * FN_W), BF16),
                   jax.ShapeDtypeStruct((CH, rows, MIX), BF16)],
        scratch_shapes=[pltpu.VMEM((S5_W, D), BF16), pltpu.VMEM((D, FN_W + MIX), BF16)]
        + [pltpu.VMEM((rows, 128), F32)] * 4,
        compiler_params=_cparams(("arbitrary",)),
    )(xs, shift, scale, g.reshape(1, D), w_in, wcs)


def _s5_kernel(ut_ref, mt_ref, be_ref, cp_ref, l16_ref, yt_ref,
               sre_ref, sim_ref, are_ref, aim_ref, bre_ref, bim_ref, *, bsz, ncl, ncc, gs):
    nl = bsz * ncl
    nst = 2 * S5_P
    for g in range(gs):
        st = lax.dot_general(ut_ref[g], be_ref[g], (((0,), (0,)), ((), ())), preferred_element_type=F32)
        sre_ref[g] = st[:, :nst]
        sim_ref[g] = st[:, nst:]

    lam = [l16_ref[g] for g in range(gs)]
    is_fwd = lax.broadcasted_iota(jnp.int32, (bsz, nst), 1) < S5_P

    def make_step(base, nchunk):
        def step(c, carry):
            rf = pl.ds(pl.multiple_of(base + c * bsz, bsz), bsz)
            rb = pl.ds(pl.multiple_of(base + (nchunk - 1 - c) * bsz, bsz), bsz)
            out = []
            for g in range(gs):
                sre, sim = carry[2 * g], carry[2 * g + 1]
                lre = lam[g][0:1, :]
                lim = lam[g][1:2, :]
                are_ref[g, rf, :] = sre
                aim_ref[g, rf, :] = sim
                bre_ref[g, rb, :] = sre
                bim_ref[g, rb, :] = sim
                in_re = jnp.where(is_fwd, sre_ref[g, rf, :], sre_ref[g, rb, :])
                in_im = jnp.where(is_fwd, sim_ref[g, rf, :], sim_ref[g, rb, :])
                out.append(lre * sre - lim * sim + in_re)
                out.append(lre * sim + lim * sre + in_im)
            return tuple(out)
        return step

    zero = jnp.zeros((bsz, nst), F32)
    carry = lax.fori_loop(0, ncc, make_step(nl, ncc), (zero,) * (2 * gs))
    lax.fori_loop(0, ncl, make_step(0, ncl), carry)

    nt = (((1,), (1,)), ((), ()))
    fwd_rows = lax.broadcasted_iota(jnp.int32, (bsz * (ncl + ncc), nst), 1) < S5_P
    for g in range(gs):
        s0 = jnp.concatenate([jnp.where(fwd_rows, are_ref[g], bre_ref[g]),
                              jnp.where(fwd_rows, aim_ref[g], bim_ref[g])], axis=1).astype(BF16)
        yt_ref[g] = (jnp.dot(mt_ref[g], ut_ref[g], preferred_element_type=F32)
                     + lax.dot_general(cp_ref[g], s0, nt, preferred_element_type=F32)).astype(BF16)


def _s5_mix(ut, ncl, mt, bend, cp, lam16, bsz, layer):
    rows = ut.shape[2]
    kk = CH * S5_H
    nst = 2 * S5_P
    gs = 4
    off = layer * (S5_G // gs)

    def gspec(r, c):
        return pl.BlockSpec((gs, r, c), lambda g: (g, 0, 0))

    def tspec(r, c):
        return pl.BlockSpec((gs, r, c), lambda g: (g + off, 0, 0))

    return pl.pallas_call(
        functools.partial(_s5_kernel, bsz=bsz, ncl=ncl, ncc=rows // bsz - ncl, gs=gs),
        grid=(S5_G // gs,),
        in_specs=[gspec(kk, rows), tspec(kk, kk), tspec(kk, 2 * nst), tspec(kk, 2 * nst), tspec(2, nst)],
        out_specs=gspec(kk, rows),
        out_shape=jax.ShapeDtypeStruct((S5_G, kk, rows), BF16),
        scratch_shapes=[pltpu.VMEM((gs, rows, nst), F32)] * 6,
        compiler_params=_cparams(("arbitrary",)),
    )(ut, mt, bend, cp, lam16)


def _fnet_weights_kernel(ccs_ref, fw_ref, o_ref):
    fw = fw_ref[...]
    ccs = ccs_ref[...]
    wc = jnp.dot(ccs[:, :FN_W], fw, precision=HI, preferred_element_type=F32)
    ws = jnp.dot(ccs[:, FN_W:], fw, precision=HI, preferred_element_type=F32)
    o_ref[...] = jnp.concatenate([wc, ws], axis=1).astype(BF16)


def _fnet_weights(fw_bd):
    return pl.pallas_call(
        _fnet_weights_kernel,
        out_shape=jax.ShapeDtypeStruct((FN_W, 2 * FN_W), BF16),
    )(_dft_channel_matrix(), fw_bd)


def _fnet_kernel(tab_ref, v_ref, jr_ref, o_ref, *, ipt):
    r = pl.program_id(1)
    tab = tab_ref[...].astype(BF16)
    v = v_ref[...]
    nc, n = v.shape[2], v.shape[3]
    t = CH * nc
    a = jnp.dot(tab[:, :t], v[0].reshape(t, n), preferred_element_type=F32)
    b = jnp.dot(tab[:, t:], v[1].reshape(t, n), preferred_element_type=F32)
    o_ref[pl.ds(r * ipt, ipt)] = (a - b).reshape(ipt, nc, n).astype(BF16)
    m = (a + b).astype(BF16)
    for k in range(ipt):
        i = r * ipt + k

        @pl.when(jnp.logical_and(i >= 1, i <= CH // 2 - 1))
        def _():
            o_ref[CH - i] = jnp.dot(jr_ref[...], m[k * nc:(k + 1) * nc],
                                    preferred_element_type=F32).astype(BF16)


def _dft_half_table(nc):
    t = nc * CH
    cols = (np.arange(nc)[None, :] * CH + np.arange(CH)[:, None]).reshape(-1)
    rows = (np.arange(nc)[None, :] * CH + np.arange(CH // 2 + 1)[:, None]).reshape(-1)
    prod = (rows[:, None].astype(np.int64) * cols[None, :].astype(np.int64)) % t
    ang = prod.astype(np.float64) * (2.0 * np.pi / t)
    scale = 1.0 / math.sqrt(t * FN_GW)
    return jnp.asarray(np.concatenate([np.cos(ang), np.sin(ang)], axis=1) * scale, dtype=F32)


def _dft_channel_matrix():
    c = np.arange(FN_GW)
    ang = (c[:, None] * c[None, :] % FN_GW).astype(np.float64) * (2.0 * np.pi / FN_GW)
    eye = np.eye(FN_G)
    return jnp.asarray(np.concatenate([np.kron(eye, np.cos(ang)), np.kron(eye, np.sin(ang))], axis=1),
                       dtype=F32)


def _fnet(xw, c0, nc, bsz):
    t = nc * CH
    nhb = CH // 2 + 1
    ipt = 3
    bpb = 4
    jr = jnp.asarray(np.eye(nc)[::-1], dtype=BF16)
    return pl.pallas_call(
        functools.partial(_fnet_kernel, ipt=ipt),
        grid=(bsz // bpb, nhb // ipt),
        in_specs=[pl.BlockSpec((ipt * nc, 2 * t), lambda b, r: (r, 0)),
                  pl.BlockSpec((2, CH, nc, bpb * FN_W), lambda b, r: (0, 0, c0 // nc, b)),
                  pl.BlockSpec((nc, nc), lambda b, r: (0, 0))],
        out_specs=pl.BlockSpec((CH, nc, bpb * FN_W), lambda b, r: (0, 0, b)),
        out_shape=jax.ShapeDtypeStruct((CH, nc, bsz * FN_W), BF16),
        compiler_params=_cparams(("arbitrary", "arbitrary")),
    )(_dft_half_table(nc), xw, jr)


def _even_out_kernel(*refs, ncl, with_ctx):
    it = iter(refs)
    yt_ref, ybl_ref = next(it), next(it)
    ybc_ref = next(it) if with_ctx else None
    z_ref, x_ref, gate_ref, gwf_ref, gb_ref, wof_ref, o_ref = (next(it), next(it), next(it), next(it),
                                                               next(it), next(it), next(it))
    scr = [next(it), next(it)]
    gw_ref, wo_ref = next(it), next(it)

    @pl.when(pl.program_id(0) == 0)
    def _():
        gw_ref[...] = gwf_ref[...].astype(BF16)
        wo_ref[...] = wof_ref[...].astype(BF16)

    yt = yt_ref[...]
    rows = yt.shape[2]
    ya = _gelu_tanh(yt.astype(F32).reshape(S5_W, rows).T)
    glu = jnp.dot(ya.astype(BF16), gw_ref[...], preferred_element_type=F32) + gb_ref[...]
    ya = ya * _sigmoid(glu)
    x = x_ref[...]
    nc, nb, _ = x.shape
    for half in range(2):
        for bi in range(nb):
            lo = bi * FN_W + half * 128
            scr[half][pl.ds(bi, ncl, stride=nb), :] = ybl_ref[:, lo:lo + 128].astype(F32)
            if with_ctx:
                scr[half][pl.ds(ncl * nb + bi, nc - ncl, stride=nb), :] = (
                    ybc_ref[:, lo:lo + 128].astype(F32))
    yb = jnp.concatenate([scr[0][...], scr[1][...]], axis=1)
    sz = z_ref[...]
    ma = (ya * sz[:, :S5_W]).astype(BF16)
    mb = (yb * sz[:, S5_W:]).astype(BF16)
    out = (jnp.dot(ma, wo_ref[0:S5_W, :], preferred_element_type=F32)
           + jnp.dot(mb, wo_ref[S5_W:MIX, :], preferred_element_type=F32)).reshape(x.shape)
    o_ref[0:ncl] = x[0:ncl] + gate_ref[0:1] * out[0:ncl]
    if with_ctx:
        o_ref[ncl:nc] = x[ncl:nc] + gate_ref[1:2] * out[ncl:nc]


def _even_out(yt, ybl, ybc, z, xs, ncl, gate, glu_w, glu_b, w_out, layer):
    bsz = xs.shape[1]
    with_ctx = ybc is not None
    nc = xs.shape[0] if with_ctx else ncl
    rows = nc * bsz
    args = [yt, ybl]
    specs = [pl.BlockSpec((S5_G, S5_H, rows), lambda j: (0, j, 0)),
             pl.BlockSpec((None, ncl, bsz * FN_W), lambda j: (j, 0, 0))]
    if with_ctx:
        args.append(ybc)
        specs.append(pl.BlockSpec((None, nc - ncl, bsz * FN_W), lambda j: (j, 0, 0)))
    args += [z, xs, gate, glu_w, glu_b.reshape(1, S5_W), w_out]
    specs += [pl.BlockSpec((None, rows, MIX), lambda j: (j, 0, 0)),
              pl.BlockSpec((nc, bsz, D), lambda j: (0, 0, j)),
              pl.BlockSpec((2, bsz, D), lambda j: (0, 0, 0)),
              pl.BlockSpec((None, S5_W, S5_W), lambda j: (layer, 0, 0), pipeline_mode=pl.Buffered(1)),
              pl.BlockSpec((1, S5_W), lambda j: (0, 0)),
              pl.BlockSpec((None, MIX, D), lambda j: (layer, 0, 0), pipeline_mode=pl.Buffered(1))]
    return pl.pallas_call(
        functools.partial(_even_out_kernel, ncl=ncl, with_ctx=with_ctx),
        grid=(CH,),
        in_specs=specs,
        out_specs=pl.BlockSpec((nc, bsz, D), lambda j: (0, 0, j)),
        out_shape=jax.ShapeDtypeStruct((nc, bsz, CH * D), F32),
        scratch_shapes=[pltpu.VMEM((rows, 128), F32)] * 2 + [pltpu.VMEM((S5_W, S5_W), BF16),
                                                             pltpu.VMEM((MIX, D), BF16)],
        compiler_params=_cparams(("arbitrary",)),
    )(*args)


def _odd_kernel(*refs, ct, ntl, t_lat, t_ctx, final):
    it = iter(refs)
    xm_ref, xp_ref, xn_ref = next(it), next(it), next(it)
    shift_ref, scale_ref, gate_ref, g_ref = next(it), next(it), next(it), next(it)
    wif_ref, wof_ref, pwf_ref, ps_ref, cw_ref = next(it), next(it), next(it), next(it), next(it)
    fg_ref = next(it) if final else None
    o_ref = next(it)
    h_ref = next(it)
    pe_ref = next(it)
    wi_ref, wo_ref, pw_ref = next(it), next(it), next(it)
    fin_ref = next(it) if final else None

    ti = pl.program_id(0)
    is_ctx = ti >= ntl
    t0 = jnp.where(is_ctx, ti - ntl, ti) * (ct * CH)
    t_total = jnp.where(is_ctx, t_ctx, t_lat)

    @pl.when(ti == 0)
    def _():
        wi_ref[...] = wif_ref[...].astype(BF16)
        wo_ref[...] = wof_ref[...].astype(BF16)
        pw_ref[...] = pwf_ref[...].astype(BF16)

    n1 = POOL_W + 2 * CONV_W
    nb = xm_ref.shape[1]
    tm = ct * CH
    ne = tm + 2 * HALO

    def hn(xv):
        return _mod_norm(xv, g_ref[...], scale_ref[...], shift_ref[...])

    for i in range(CH):
        hi = hn(xm_ref[:, :, i * D:(i + 1) * D])
        if i < HALO:
            h_ref[0:ct, i + HALO] = hi
        else:
            h_ref[1:ct + 1, i - HALO] = hi
    for i in range(HALO):
        h_ref[0, i] = hn(xp_ref[:, :, (i + HALO) * D:(i + HALO + 1) * D])[0]
        h_ref[ct, i + HALO] = hn(xn_ref[:, :, i * D:(i + 1) * D])[0]

    he = h_ref[...].reshape(ne * nb, D).astype(BF16)
    pe = jnp.dot(he, wi_ref[:, 0:n1], preferred_element_type=F32)
    te = t0 - HALO + lax.broadcasted_iota(jnp.int32, (ne * nb, 1), 0) // nb
    valid = jnp.logical_and(te >= 0, te < t_total)
    pe_ref[:, 0:POOL_W] = jnp.where(valid, pe[:, :POOL_W], 0.0)
    pe_ref[:, POOL_W:POOL_W + CONV_W] = jnp.where(
        valid, pe[:, POOL_W:POOL_W + CONV_W] * pe[:, POOL_W + CONV_W:], 0.0)

    m0 = HALO * nb
    mr = tm * nb
    p2 = jnp.dot(he[m0:m0 + mr], wi_ref[:, n1:], preferred_element_type=F32)
    b_gate = p2[:, :CONV_W]
    sz = _silu(p2[:, CONV_W:])

    tpos = t0 + lax.broadcasted_iota(jnp.int32, (mr, 1), 0) // nb
    pooled = []
    for gi, w in enumerate(POOL_WINDOWS):
        c0 = gi * POOL_GW
        s = pe_ref[:, c0:c0 + POOL_GW]
        n = ne
        width = 1
        while width < w:
            s = s[0:(n - width) * nb] + s[width * nb:n * nb]
            n -= width
            width *= 2
        start = (HALO - w // 2) * nb
        total = s[start:start + mr]
        hi = jnp.minimum(tpos + w // 2, t_total)
        lo = jnp.maximum(tpos - w // 2, 0)
        cnt = (hi - lo).astype(F32)
        centre = pe_ref[m0:m0 + mr, c0:c0 + POOL_GW]
        pg = total / cnt - centre
        pooled.append(jnp.dot(pg.astype(BF16), pw_ref[gi], preferred_element_type=F32))
    y_c = jnp.concatenate(pooled, axis=1) * ps_ref[...]

    cwt = cw_ref[...]
    vm = pe_ref[m0 - nb:m0 - nb + mr, POOL_W:POOL_W + CONV_W]
    v0 = pe_ref[m0:m0 + mr, POOL_W:POOL_W + CONV_W]
    vp = pe_ref[m0 + nb:m0 + nb + mr, POOL_W:POOL_W + CONV_W]
    y_d = b_gate * (vm * cwt[0:1, :] + v0 * cwt[1:2, :] + vp * cwt[2:3, :])

    y = (jnp.concatenate([y_c, y_d], axis=1) * sz).astype(BF16)
    out = jnp.dot(y, wo_ref[...], preferred_element_type=F32)
    go = (gate_ref[...] * out.reshape(tm, nb, D)).reshape(ct, CH, nb, D)
    for i in range(CH):
        xo = xm_ref[:, :, i * D:(i + 1) * D] + go[:, i]
        if not final:
            o_ref[:, :, i * D:(i + 1) * D] = xo
            continue
        ms = jnp.mean(xo * xo, axis=-1, keepdims=True)
        fin_ref[:, :, i * D:(i + 1) * D] = xo * lax.rsqrt(ms + EPS) * fg_ref[...]
    if final:
        o_ref[...] = pltpu.einshape("cb(id)->b(ci)d", fin_ref[...], i=CH)


def _odd_layer(xs, ncl, with_ctx, shift, scale, gate, g, w_in, w_out, pool_w, layer, pool_scale, conv_w,
               final_g):
    nc_all, bsz, _ = xs.shape
    ct = 8
    ntl = ncl // ct
    nc = nc_all if with_ctx else ncl
    final = final_g is not None
    args = [xs, xs, xs, shift, scale, gate, g.reshape(1, D), w_in, w_out, pool_w,
            pool_scale.reshape(1, POOL_W), conv_w]
    vec = pl.BlockSpec((1, bsz, D), lambda i: (jnp.minimum(i // ntl, 1), 0, 0))

    def const(*shape):
        return pl.BlockSpec(shape, lambda i: (0,) * len(shape), pipeline_mode=pl.Buffered(1))

    def stacked(*shape):
        return pl.BlockSpec((None,) + shape, lambda i: (layer,) + (0,) * len(shape),
                            pipeline_mode=pl.Buffered(1))

    specs = [pl.BlockSpec((ct, bsz, CH * D), lambda i: (i, 0, 0)),
             pl.BlockSpec((1, bsz, CH * D), lambda i: (jnp.maximum(i * ct - 1, 0), 0, 0)),
             pl.BlockSpec((1, bsz, CH * D), lambda i: (jnp.minimum((i + 1) * ct, nc_all - 1), 0, 0)),
             vec, vec, vec,
             const(1, D), stacked(*w_in.shape[1:]), stacked(MIX, D),
             stacked(len(POOL_WINDOWS), POOL_GW, POOL_GW), const(1, POOL_W), const(3, CONV_W)]
    scratch = [pltpu.VMEM((ct + 1, CH, bsz, D), F32),
               pltpu.VMEM(((ct + 1) * CH * bsz, POOL_W + CONV_W), F32),
               pltpu.VMEM(w_in.shape[1:], BF16), pltpu.VMEM((MIX, D), BF16),
               pltpu.VMEM((len(POOL_WINDOWS), POOL_GW, POOL_GW), BF16)]
    if final:
        args.append(final_g.reshape(1, D))
        specs.append(const(1, D))
        scratch.append(pltpu.VMEM((ct, bsz, CH * D), F32))
        out_spec = pl.BlockSpec((bsz, ct * CH, D), lambda i: (0, i, 0))
        out_shape = jax.ShapeDtypeStruct((bsz, nc * CH, D), F32)
    else:
        out_spec = pl.BlockSpec((ct, bsz, CH * D), lambda i: (i, 0, 0))
        out_shape = jax.ShapeDtypeStruct((nc, bsz, CH * D), F32)
    return pl.pallas_call(
        functools.partial(_odd_kernel, ct=ct, ntl=ntl, t_lat=ncl * CH, t_ctx=(nc_all - ncl) * CH,
                          final=final),
        grid=(nc // ct,),
        in_specs=specs,
        out_specs=out_spec,
        out_shape=out_shape,
        scratch_shapes=scratch,
        compiler_params=_cparams(("arbitrary",)),
    )(*args)


def _sincos_table(n_tok, dim):
    rows = n_tok // GRID_W
    rr, cc = np.meshgrid(np.arange(rows, dtype=np.float64), np.arange(GRID_W, dtype=np.float64),
                         indexing='ij')
    rr = rr.reshape(-1, 1)
    cc = cc.reshape(-1, 1)
    quarter = dim // 4
    omega = POS_BASE ** (-np.arange(quarter, dtype=np.float64) / quarter)
    tab = np.concatenate([np.sin(rr * omega), np.cos(rr * omega), np.sin(cc * omega), np.cos(cc * omega)],
                         axis=-1)
    return jnp.asarray(tab, dtype=F32)


def _block_diag(w):
    g, c, _ = w.shape
    eye = jnp.eye(g, dtype=w.dtype)
    return (eye[:, None, :, None] * w[:, :, None, :]).reshape(g * c, g * c)


def _to_stream_kernel(x_ref, pos_ref, c_ref, o_ref, *, ntl):
    step = pl.program_id(0)

    @pl.when(step < ntl)
    def _():
        o_ref[...] = pltpu.einshape("b(ci)d->cb(id)", x_ref[...] + pos_ref[...][None], i=CH)

    @pl.when(step >= ntl)
    def _():
        o_ref[...] = pltpu.einshape("b(ci)d->cb(id)", c_ref[...], i=CH)


def _to_stream(x, pos, ctx):
    bsz, t, _ = x.shape
    tc = ctx.shape[1]
    ct = 8
    tm = ct * CH
    ntl = t // tm
    return pl.pallas_call(
        functools.partial(_to_stream_kernel, ntl=ntl),
        grid=((t + tc) // tm,),
        in_specs=[pl.BlockSpec((bsz, tm, D), lambda i: (0, jnp.minimum(i, ntl - 1), 0)),
                  pl.BlockSpec((tm, D), lambda i: (jnp.minimum(i, ntl - 1), 0)),
                  pl.BlockSpec((bsz, tm, D), lambda i: (0, jnp.maximum(i - ntl, 0), 0))],
        out_specs=pl.BlockSpec((ct, bsz, CH * D), lambda i: (i, 0, 0)),
        out_shape=jax.ShapeDtypeStruct(((t + tc) // CH, bsz, CH * D), F32),
        compiler_params=_cparams(("arbitrary",)),
    )(x, pos, ctx)


def kernel(x, c, ctx, c_ctx, norm_g, ada_w, ada_b, even_w_in, even_w_out, s5_lam_re, s5_lam_im, s5_log_step, s5_b_re, s5_b_im, s5_c_re, s5_c_im, s5_d, s5_glu_w, s5_glu_b, fnet_w, odd_w_in, odd_w_out, pool_w, pool_scale, conv_w, final_g):
    bsz, n_tok, _ = x.shape
    depth = norm_g.shape[0]
    ncl = n_tok // CH
    ncc = ctx.shape[1] // CH

    cond = jnp.concatenate([c, jnp.broadcast_to(c_ctx[None], (16 - bsz, D))], axis=0)
    ada, mt, bend, cp, lam16 = _prepare(cond, ada_w, ada_b, s5_lam_re, s5_lam_im, s5_log_step, s5_b_re,
                                        s5_b_im, s5_c_re, s5_c_im, s5_d)

    def mod(l, j):
        return jnp.stack([ada[l, j, :bsz], jnp.broadcast_to(ada[l, j, bsz], (bsz, D))])

    need_ctx = [any(j % 2 == 0 for j in range(l + 1, depth)) for l in range(depth)]

    xs = _to_stream(x, _sincos_table(n_tok, D), ctx)
    for l in range(depth):
        i = l // 2
        last = l == depth - 1
        if l % 2 == 0:
            wcs = _fnet_weights(_block_diag(fnet_w[i]))
            ut, xw, z = _even_in(xs, ncl, mod(l, 0), mod(l, 1), norm_g[l], even_w_in, i, wcs)
            yt = _s5_mix(ut, ncl, mt, bend, cp, lam16, bsz, i)
            ybl = _fnet(xw, 0, ncl, bsz)
            ybc = _fnet(xw, ncl, ncc, bsz) if need_ctx[l] else None
            xs = _even_out(yt, ybl, ybc, z, xs, ncl, mod(l, 2), s5_glu_w, s5_glu_b[i], even_w_out, i)
        else:
            xs = _odd_layer(xs, ncl, need_ctx[l], mod(l, 0), mod(l, 1), mod(l, 2), norm_g[l], odd_w_in,
                            odd_w_out, pool_w, i, pool_scale[i], conv_w[i], final_g if last else None)
    if depth % 2 == 1:
        raise NotImplementedError("final norm and (B, T, D) order are produced by the last (odd) layer")
    return xs
```

```python
import functools
import math

import numpy as np
import jax
import jax.numpy as jnp
from jax import lax
from jax.experimental import pallas as pl
from jax.experimental.pallas import tpu as pltpu

D = 1024
MIX = 1024
S5_W = 768
FN_W = 256
S5_H = 16
S5_G = 48
S5_P = 64
FN_G = 4
FN_GW = 64
POOL_W = 512
CONV_W = 512
POOL_WINDOWS = (2, 4, 8, 16)
POOL_GW = 128
GRID_W = 64
EPS = 1e-6
POS_BASE = 10000.0
CH = 16
HALO = 8
LANE = 128
VMEM_LIMIT = 56 * 1024 * 1024

F32 = jnp.float32
BF16 = jnp.bfloat16
HI = lax.Precision.HIGHEST


def _cparams(sem):
    return pltpu.CompilerParams(dimension_semantics=sem, vmem_limit_bytes=VMEM_LIMIT)


def _sigmoid(v):
    return 0.5 * jnp.tanh(0.5 * v) + 0.5


def _silu(v):
    return v * _sigmoid(v)


def _gelu_tanh(v):
    c = math.sqrt(2.0 / math.pi)
    return 0.5 * v * (1.0 + jnp.tanh(c * (v + 0.044715 * (v * v * v))))


def _mod_norm(x, g, scale, shift):
    ms = jnp.mean(x * x, axis=-1, keepdims=True)
    y = x * lax.rsqrt(ms + EPS) * g
    return y * (1.0 + scale) + shift


def _split3(v):
    hi = v.astype(BF16)
    lo = (v - hi.astype(F32)).astype(BF16)
    return hi, lo


def _prep_kernel(c_ref, w_ref, b_ref, lr_ref, li_ref, ls_ref, btr_ref, bti_ref, cr_ref, ci_ref, d_ref,
                 ada_ref, mt_ref, be_ref, cp_ref, l16_ref, ere_ref, eim_ref, *, gb):
    s_hi, s_lo = _split3(_silu(c_ref[...]))
    w_hi, w_lo = _split3(w_ref[...])
    ada_ref[...] = (jnp.dot(s_hi, w_hi, preferred_element_type=F32)
                    + jnp.dot(s_hi, w_lo, preferred_element_type=F32)
                    + jnp.dot(s_lo, w_hi, preferred_element_type=F32) + b_ref[...])

    nst = 2 * S5_P
    kk = CH * S5_H
    step = jnp.exp(ls_ref[...])
    lr = lr_ref[...]
    li = li_ref[...]
    a = lr * step
    b = li * step

    def powers(expo):
        mag = jnp.exp(expo * a)
        return mag * jnp.cos(expo * b), mag * jnp.sin(expo * b)

    row = lax.broadcasted_iota(jnp.int32, (1, CH, nst), 1).astype(F32)
    fwd = lax.broadcasted_iota(jnp.int32, (1, CH, nst), 2) < S5_P
    one = jnp.ones((1, 1, nst), F32)

    l1re, l1im = powers(one)
    n_re = l1re - 1.0
    den = lr * lr + li * li
    co_re = (n_re * lr + l1im * li) / den
    co_im = (l1im * lr - n_re * li) / den
    btr = btr_ref[...]
    bti = bti_ref[...]
    bb_re = co_re * btr - co_im * bti
    bb_im = co_re * bti + co_im * btr

    pe_re, pe_im = powers(jnp.where(fwd, (CH - 1) - row, row))
    for l in range(CH):
        pr = pe_re[:, l:l + 1, :]
        pi = pe_im[:, l:l + 1, :]
        ere_ref[:, l * S5_H:(l + 1) * S5_H, :] = pr * bb_re - pi * bb_im
        eim_ref[:, l * S5_H:(l + 1) * S5_H, :] = pr * bb_im + pi * bb_re

    cr = cr_ref[...]
    ci = ci_ref[...]
    pc_re, pc_im = powers(jnp.where(fwd, row + 1.0, CH - row))
    for j in range(CH):
        pr = pc_re[:, j:j + 1, :]
        pi = pc_im[:, j:j + 1, :]
        w_re = cr * pr - ci * pi
        w_im = cr * pi + ci * pr
        cp_ref[:, j * S5_H:(j + 1) * S5_H, :] = jnp.concatenate([w_re, -w_im], axis=2).astype(BF16)

    l16re, l16im = powers(one * float(CH))
    l16_ref[...] = jnp.concatenate([l16re, l16im], axis=1)

    fwd2 = lax.broadcasted_iota(jnp.int32, (S5_H, nst), 1) < S5_P
    lane = lax.broadcasted_iota(jnp.int32, (S5_H, kk), 1)
    iblk = lane // S5_H
    hrow = lax.broadcasted_iota(jnp.int32, (S5_H, kk), 0)
    nt = (((1,), (1,)), ((), ()))
    for g in range(gb):
        ere = ere_ref[g]
        eim = eim_ref[g]
        e2 = jnp.concatenate([ere, eim], axis=1)
        be_ref[g] = e2.astype(BF16)
        e_hi, e_lo = _split3(e2)
        rhs = jnp.concatenate([e_hi, e_hi, e_lo], axis=1)
        crg = cr[g]
        cig = ci[g]

        c2 = jnp.concatenate([
            jnp.concatenate([jnp.where(fwd2, crg, 0.0), jnp.where(fwd2, -cig, 0.0)], axis=1),
            jnp.concatenate([jnp.where(fwd2, 0.0, crg), jnp.where(fwd2, 0.0, -cig)], axis=1)], axis=0)
        c_hi, c_lo = _split3(c2)
        lhs = jnp.concatenate([c_hi, c_lo, c_hi], axis=1)
        kfb = lax.dot_general(lhs, rhs, nt, preferred_element_type=F32)
        kf = kfb[0:S5_H]
        kb = kfb[S5_H:2 * S5_H]
        dg = d_ref[g]
        for j in range(CH):
            sf = (kk - (CH - 1 - j) * S5_H) % kk
            rf = pltpu.roll(kf, sf, 1) if sf else kf
            rb = pltpu.roll(kb, j * S5_H, 1) if j else kb
            blk = (jnp.where(iblk <= j, rf, 0.0) + jnp.where(iblk >= j, rb, 0.0)
                   + jnp.where(lane == j * S5_H + hrow, dg, 0.0))
            mt_ref[g, j * S5_H:(j + 1) * S5_H, :] = blk.astype(BF16)


def _prepare(cond, ada_w, ada_b, lam_re, lam_im, log_step, b_re, b_im, c_re, c_im, d_skip):
    n = lam_re.shape[0] * S5_G
    gb = 8
    nst = 2 * S5_P
    kk = CH * S5_H

    def fb(v):
        return jnp.concatenate([v[:, 0], v[:, 1]], axis=-1).reshape(n, v.shape[3], nst)

    lr = fb(lam_re[:, :, :, None, :])
    li = fb(lam_im[:, :, :, None, :])
    ls = fb(jnp.broadcast_to(log_step[:, :, :, None, None], log_step.shape + (1, S5_P)))
    btr = fb(jnp.swapaxes(b_re, -1, -2))
    bti = fb(jnp.swapaxes(b_im, -1, -2))
    cr = fb(c_re)
    ci = fb(c_im)
    d = d_skip.reshape(n, S5_H, 1)

    depth = ada_w.shape[0]
    n_ada = depth * 3
    n_tab = n // gb

    def spec(r, c):
        return pl.BlockSpec((gb, r, c), lambda s: (jnp.minimum(s, n_tab - 1), 0, 0))

    def ada_idx(s):
        s = jnp.minimum(s, n_ada - 1)
        return s // 3, s % 3

    return pl.pallas_call(
        functools.partial(_prep_kernel, gb=gb),
        grid=(max(n_ada, n_tab),),
        in_specs=[pl.BlockSpec((16, D), lambda s: (0, 0)),
                  pl.BlockSpec((None, D, D), lambda s: (ada_idx(s)[0], 0, ada_idx(s)[1])),
                  pl.BlockSpec((None, None, 1, D), lambda s: ada_idx(s) + (0, 0)),
                  spec(1, nst), spec(1, nst), spec(1, nst), spec(S5_H, nst), spec(S5_H, nst),
                  spec(S5_H, nst), spec(S5_H, nst), spec(S5_H, 1)],
        out_specs=[pl.BlockSpec((None, None, 16, D), lambda s: ada_idx(s) + (0, 0)),
                   spec(kk, kk), spec(kk, 2 * nst), spec(kk, 2 * nst), spec(2, nst)],
        out_shape=[jax.ShapeDtypeStruct((depth, 3, 16, D), F32),
                   jax.ShapeDtypeStruct((n, kk, kk), BF16),
                   jax.ShapeDtypeStruct((n, kk, 2 * nst), BF16),
                   jax.ShapeDtypeStruct((n, kk, 2 * nst), BF16),
                   jax.ShapeDtypeStruct((n, 2, nst), F32)],
        scratch_shapes=[pltpu.VMEM((gb, kk, nst), F32)] * 2,
        compiler_params=_cparams(("arbitrary",)),
    )(cond, ada_w, ada_b.reshape(depth, 3, 1, D), lr, li, ls, btr, bti, cr, ci, d)


def _even_in_kernel(x_ref, shift_ref, scale_ref, g_ref, w_ref, wcs_ref, ut_ref, xw_ref, z_ref,
                    wat_ref, wbz_ref, *scr, ncl):
    @pl.when(pl.program_id(0) == 0)
    def _():
        wat_ref[...] = w_ref[:, :S5_W].T.astype(BF16)
        wbz_ref[...] = w_ref[:, S5_W:].astype(BF16)

    x = x_ref[...]
    nc, nb, _ = x.shape
    rows = nc * nb
    g = g_ref[...]
    h = jnp.concatenate([_mod_norm(x[0:ncl], g, scale_ref[0:nb][None], shift_ref[0:nb][None]),
                         _mod_norm(x[ncl:nc], g, scale_ref[nb:nb + 1][None], shift_ref[nb:nb + 1][None])],
                        axis=0)
    hb = h.reshape(rows, D).astype(BF16)
    pt = lax.dot_general(wat_ref[...], hb, (((1,), (1,)), ((), ())), preferred_element_type=F32)
    ut_ref[...] = pt.astype(BF16).reshape(S5_G, S5_H, rows)
    p = jnp.dot(hb, wbz_ref[...], preferred_element_type=F32)
    z_ref[...] = _silu(p[:, FN_W:]).astype(BF16)
    xw = jnp.dot(p[:, :FN_W].astype(BF16), wcs_ref[...], preferred_element_type=F32)
    for q in range(4):
        scr[q][...] = xw[:, q * LANE:(q + 1) * LANE]
    for q in range(4):
        part, half = divmod(q, 2)
        for bi in range(nb):
            piece = scr[q][pl.ds(bi, nc, stride=nb), :]
            lo = bi * FN_W + half * LANE
            xw_ref[part, :, lo:lo + LANE] = piece.astype(BF16)


def _even_in(xs, ncl, ada, l, g, w_in, layer, wcs):
    nc, bsz, _ = xs.shape
    rows = nc * bsz
    return pl.pallas_call(
        functools.partial(_even_in_kernel, ncl=ncl),
        grid=(CH,),
        in_specs=[pl.BlockSpec((nc, bsz, D), lambda i: (0, 0, i)),
                  pl.BlockSpec((None, None) + ada.shape[2:], lambda i: (l, 0, 0, 0)),
                  pl.BlockSpec((None, None) + ada.shape[2:], lambda i: (l, 1, 0, 0)),
                  pl.BlockSpec((1, D), lambda i: (0, 0)),
                  pl.BlockSpec((None, D, w_in.shape[2]), lambda i: (layer, 0, 0),
                               pipeline_mode=pl.Buffered(1)),
                  pl.BlockSpec((FN_W, 2 * FN_W), lambda i: (0, 0))],
        out_specs=[pl.BlockSpec((S5_G, S5_H, rows), lambda i: (0, i, 0)),
                   pl.BlockSpec((2, None, nc, bsz * FN_W), lambda i: (0, i, 0, 0)),
                   pl.BlockSpec((None, rows, MIX), lambda i: (i, 0, 0))],
        out_shape=[jax.ShapeDtypeStruct((S5_G, CH * S5_H, rows), BF16),
                   jax.ShapeDtypeStruct((2, CH, nc, bsz * FN_W), BF16),
                   jax.ShapeDtypeStruct((CH, rows, MIX), BF16)],
        scratch_shapes=[pltpu.VMEM((S5_W, D), BF16), pltpu.VMEM((D, FN_W + MIX), BF16)]
        + [pltpu.VMEM((rows, LANE), F32)] * 4,
        compiler_params=_cparams(("arbitrary",)),
    )(xs, ada, ada, g.reshape(1, D), w_in, wcs)


def _s5_kernel(ut_ref, mt_ref, be_ref, cp_ref, l16_ref, yt_ref,
               sre_ref, sim_ref, are_ref, aim_ref, bre_ref, bim_ref, *, bsz, ncl, ncc, gs):
    nl = bsz * ncl
    nst = 2 * S5_P
    for g in range(gs):
        st = lax.dot_general(ut_ref[g], be_ref[g], (((0,), (0,)), ((), ())), preferred_element_type=F32)
        sre_ref[g] = st[:, :nst]
        sim_ref[g] = st[:, nst:]

    lam = [l16_ref[g] for g in range(gs)]
    is_fwd = lax.broadcasted_iota(jnp.int32, (bsz, nst), 1) < S5_P

    def make_step(base, nchunk):
        def step(c, carry):
            rf = pl.ds(pl.multiple_of(base + c * bsz, bsz), bsz)
            rb = pl.ds(pl.multiple_of(base + (nchunk - 1 - c) * bsz, bsz), bsz)
            out = []
            for g in range(gs):
                sre, sim = carry[2 * g], carry[2 * g + 1]
                lre = lam[g][0:1, :]
                lim = lam[g][1:2, :]
                are_ref[g, rf, :] = sre
                aim_ref[g, rf, :] = sim
                bre_ref[g, rb, :] = sre
                bim_ref[g, rb, :] = sim
                in_re = jnp.where(is_fwd, sre_ref[g, rf, :], sre_ref[g, rb, :])
                in_im = jnp.where(is_fwd, sim_ref[g, rf, :], sim_ref[g, rb, :])
                out.append(lre * sre - lim * sim + in_re)
                out.append(lre * sim + lim * sre + in_im)
            return tuple(out)
        return step

    zero = jnp.zeros((bsz, nst), F32)
    carry = lax.fori_loop(0, ncc, make_step(nl, ncc), (zero,) * (2 * gs))
    lax.fori_loop(0, ncl, make_step(0, ncl), carry)

    nt = (((1,), (1,)), ((), ()))
    fwd_rows = lax.broadcasted_iota(jnp.int32, (bsz * (ncl + ncc), nst), 1) < S5_P
    for g in range(gs):
        s0 = jnp.concatenate([jnp.where(fwd_rows, are_ref[g], bre_ref[g]),
                              jnp.where(fwd_rows, aim_ref[g], bim_ref[g])], axis=1).astype(BF16)
        yt_ref[g] = (jnp.dot(mt_ref[g], ut_ref[g], preferred_element_type=F32)
                     + lax.dot_general(cp_ref[g], s0, nt, preferred_element_type=F32)).astype(BF16)


def _s5_mix(ut, ncl, mt, bend, cp, lam16, bsz, layer):
    rows = ut.shape[2]
    kk = CH * S5_H
    nst = 2 * S5_P
    gs = 4
    off = layer * (S5_G // gs)

    def gspec(r, c):
        return pl.BlockSpec((gs, r, c), lambda g: (g, 0, 0))

    def tspec(r, c):
        return pl.BlockSpec((gs, r, c), lambda g: (g + off, 0, 0))

    return pl.pallas_call(
        functools.partial(_s5_kernel, bsz=bsz, ncl=ncl, ncc=rows // bsz - ncl, gs=gs),
        grid=(S5_G // gs,),
        in_specs=[gspec(kk, rows), tspec(kk, kk), tspec(kk, 2 * nst), tspec(kk, 2 * nst), tspec(2, nst)],
        out_specs=gspec(kk, rows),
        out_shape=jax.ShapeDtypeStruct((S5_G, kk, rows), BF16),
        scratch_shapes=[pltpu.VMEM((gs, rows, nst), F32)] * 6,
        compiler_params=_cparams(("arbitrary",)),
    )(ut, mt, bend, cp, lam16)


def _fnet_weights_kernel(ccs_ref, fw_ref, o_ref):
    fw = fw_ref[...]
    ccs = ccs_ref[...]
    wc = jnp.dot(ccs[:, :FN_W], fw, precision=HI, preferred_element_type=F32)
    ws = jnp.dot(ccs[:, FN_W:], fw, precision=HI, preferred_element_type=F32)
    o_ref[...] = jnp.concatenate([wc, ws], axis=1).astype(BF16)


def _fnet_weights(fw_bd):
    return pl.pallas_call(
        _fnet_weights_kernel,
        out_shape=jax.ShapeDtypeStruct((FN_W, 2 * FN_W), BF16),
    )(_dft_channel_matrix(), fw_bd)


def _fnet_kernel(tab_ref, v_ref, jr_ref, o_ref, *, ipt):
    r = pl.program_id(1)
    tab = tab_ref[...].astype(BF16)
    v = v_ref[...]
    nc, n = v.shape[2], v.shape[3]
    t = CH * nc
    a = jnp.dot(tab[:, :t], v[0].reshape(t, n), preferred_element_type=F32)
    b = jnp.dot(tab[:, t:], v[1].reshape(t, n), preferred_element_type=F32)
    o_ref[pl.ds(r * ipt, ipt)] = (a - b).reshape(ipt, nc, n).astype(BF16)
    m = (a + b).astype(BF16)
    for k in range(ipt):
        i = r * ipt + k

        @pl.when(jnp.logical_and(i >= 1, i <= CH // 2 - 1))
        def _():
            o_ref[CH - i] = jnp.dot(jr_ref[...], m[k * nc:(k + 1) * nc],
                                    preferred_element_type=F32).astype(BF16)


def _dft_half_table(nc):
    t = nc * CH
    cols = (np.arange(nc)[None, :] * CH + np.arange(CH)[:, None]).reshape(-1)
    rows = (np.arange(nc)[None, :] * CH + np.arange(CH // 2 + 1)[:, None]).reshape(-1)
    prod = (rows[:, None].astype(np.int64) * cols[None, :].astype(np.int64)) % t
    ang = prod.astype(np.float64) * (2.0 * np.pi / t)
    scale = 1.0 / math.sqrt(t * FN_GW)
    return jnp.asarray(np.concatenate([np.cos(ang), np.sin(ang)], axis=1) * scale, dtype=F32)


def _dft_channel_matrix():
    c = np.arange(FN_GW)
    ang = (c[:, None] * c[None, :] % FN_GW).astype(np.float64) * (2.0 * np.pi / FN_GW)
    eye = np.eye(FN_G)
    return jnp.asarray(np.concatenate([np.kron(eye, np.cos(ang)), np.kron(eye, np.sin(ang))], axis=1),
                       dtype=F32)


def _fnet(xw, c0, nc, bsz):
    t = nc * CH
    nhb = CH // 2 + 1
    ipt = 3
    bpb = 4
    jr = jnp.asarray(np.eye(nc)[::-1], dtype=BF16)
    return pl.pallas_call(
        functools.partial(_fnet_kernel, ipt=ipt),
        grid=(bsz // bpb, nhb // ipt),
        in_specs=[pl.BlockSpec((ipt * nc, 2 * t), lambda b, r: (r, 0)),
                  pl.BlockSpec((2, CH, nc, bpb * FN_W), lambda b, r: (0, 0, c0 // nc, b)),
                  pl.BlockSpec((nc, nc), lambda b, r: (0, 0))],
        out_specs=pl.BlockSpec((CH, nc, bpb * FN_W), lambda b, r: (0, 0, b)),
        out_shape=jax.ShapeDtypeStruct((CH, nc, bsz * FN_W), BF16),
        compiler_params=_cparams(("arbitrary", "arbitrary")),
    )(_dft_half_table(nc), xw, jr)


def _even_out_kernel(*refs, ncl, with_ctx):
    it = iter(refs)
    yt_ref, ybl_ref = next(it), next(it)
    ybc_ref = next(it) if with_ctx else None
    z_ref, x_ref, gate_ref, gwf_ref, gb_ref, wof_ref, o_ref = (next(it), next(it), next(it), next(it),
                                                               next(it), next(it), next(it))
    scr = [next(it), next(it)]
    gw_ref, wo_ref = next(it), next(it)

    @pl.when(pl.program_id(0) == 0)
    def _():
        gw_ref[...] = gwf_ref[...].astype(BF16)
        wo_ref[...] = wof_ref[...].astype(BF16)

    yt = yt_ref[...]
    rows = yt.shape[2]
    ya = _gelu_tanh(yt.astype(F32).reshape(S5_W, rows).T)
    glu = jnp.dot(ya.astype(BF16), gw_ref[...], preferred_element_type=F32) + gb_ref[...]
    ya = ya * _sigmoid(glu)
    x = x_ref[...]
    nc, nb, _ = x.shape
    for half in range(2):
        for bi in range(nb):
            lo = bi * FN_W + half * LANE
            scr[half][pl.ds(bi, ncl, stride=nb), :] = ybl_ref[:, lo:lo + LANE].astype(F32)
            if with_ctx:
                scr[half][pl.ds(ncl * nb + bi, nc - ncl, stride=nb), :] = (
                    ybc_ref[:, lo:lo + LANE].astype(F32))
    yb = jnp.concatenate([scr[0][...], scr[1][...]], axis=1)
    sz = z_ref[...]
    ma = (ya * sz[:, :S5_W]).astype(BF16)
    mb = (yb * sz[:, S5_W:]).astype(BF16)
    out = (jnp.dot(ma, wo_ref[0:S5_W, :], preferred_element_type=F32)
           + jnp.dot(mb, wo_ref[S5_W:MIX, :], preferred_element_type=F32)).reshape(x.shape)
    o_ref[0:ncl] = x[0:ncl] + gate_ref[0:nb][None] * out[0:ncl]
    if with_ctx:
        o_ref[ncl:nc] = x[ncl:nc] + gate_ref[nb:nb + 1][None] * out[ncl:nc]


def _even_out(yt, ybl, ybc, z, xs, ncl, ada, l, glu_w, glu_b, w_out, layer):
    bsz = xs.shape[1]
    with_ctx = ybc is not None
    nc = xs.shape[0] if with_ctx else ncl
    rows = nc * bsz
    args = [yt, ybl]
    specs = [pl.BlockSpec((S5_G, S5_H, rows), lambda j: (0, j, 0)),
             pl.BlockSpec((None, ncl, bsz * FN_W), lambda j: (j, 0, 0))]
    if with_ctx:
        args.append(ybc)
        specs.append(pl.BlockSpec((None, nc - ncl, bsz * FN_W), lambda j: (j, 0, 0)))
    args += [z, xs, ada, glu_w, glu_b.reshape(1, S5_W), w_out]
    specs += [pl.BlockSpec((None, rows, MIX), lambda j: (j, 0, 0)),
              pl.BlockSpec((nc, bsz, D), lambda j: (0, 0, j)),
              pl.BlockSpec((None, None) + ada.shape[2:], lambda j: (l, 2, 0, 0)),
              pl.BlockSpec((None, S5_W, S5_W), lambda j: (layer, 0, 0), pipeline_mode=pl.Buffered(1)),
              pl.BlockSpec((1, S5_W), lambda j: (0, 0)),
              pl.BlockSpec((None, MIX, D), lambda j: (layer, 0, 0), pipeline_mode=pl.Buffered(1))]
    return pl.pallas_call(
        functools.partial(_even_out_kernel, ncl=ncl, with_ctx=with_ctx),
        grid=(CH,),
        in_specs=specs,
        out_specs=pl.BlockSpec((nc, bsz, D), lambda j: (0, 0, j)),
        out_shape=jax.ShapeDtypeStruct((nc, bsz, CH * D), F32),
        scratch_shapes=[pltpu.VMEM((rows, LANE), F32)] * 2 + [pltpu.VMEM((S5_W, S5_W), BF16),
                                                             pltpu.VMEM((MIX, D), BF16)],
        compiler_params=_cparams(("arbitrary",)),
    )(*args)


def _odd_kernel(*refs, ct, ntl, t_lat, t_ctx, final):
    it = iter(refs)
    xm_ref, xp_ref, xn_ref = next(it), next(it), next(it)
    shift_ref, scale_ref, gate_ref, g_ref = next(it), next(it), next(it), next(it)
    wif_ref, wof_ref, pwf_ref, ps_ref, cw_ref = next(it), next(it), next(it), next(it), next(it)
    fg_ref = next(it) if final else None
    o_ref = next(it)
    h_ref = next(it)
    pe_ref = next(it)
    wi_ref, wo_ref, pw_ref = next(it), next(it), next(it)
    fin_ref = next(it) if final else None

    ti = pl.program_id(0)
    is_ctx = ti >= ntl
    t0 = jnp.where(is_ctx, ti - ntl, ti) * (ct * CH)
    t_total = jnp.where(is_ctx, t_ctx, t_lat)

    @pl.when(ti == 0)
    def _():
        wi_ref[...] = wif_ref[...].astype(BF16)
        wo_ref[...] = wof_ref[...].astype(BF16)
        pw_ref[...] = pwf_ref[...].astype(BF16)

    n1 = POOL_W + 2 * CONV_W
    nb = xm_ref.shape[1]
    tm = ct * CH
    ne = tm + 2 * HALO

    def rows_of(ref):
        return jnp.where(is_ctx, ref[nb:nb + 1], ref[0:nb])

    scale = rows_of(scale_ref)
    shift = rows_of(shift_ref)

    def hn(xv):
        return _mod_norm(xv, g_ref[...], scale, shift)

    for i in range(CH):
        hi = hn(xm_ref[:, :, i * D:(i + 1) * D])
        if i < HALO:
            h_ref[0:ct, i + HALO] = hi
        else:
            h_ref[1:ct + 1, i - HALO] = hi
    for i in range(HALO):
        h_ref[0, i] = hn(xp_ref[:, :, (i + HALO) * D:(i + HALO + 1) * D])[0]
        h_ref[ct, i + HALO] = hn(xn_ref[:, :, i * D:(i + 1) * D])[0]

    he = h_ref[...].reshape(ne * nb, D).astype(BF16)
    pe = jnp.dot(he, wi_ref[:, 0:n1], preferred_element_type=F32)
    te = t0 - HALO + lax.broadcasted_iota(jnp.int32, (ne * nb, 1), 0) // nb
    valid = jnp.logical_and(te >= 0, te < t_total)
    pe_ref[:, 0:POOL_W] = jnp.where(valid, pe[:, :POOL_W], 0.0)
    pe_ref[:, POOL_W:POOL_W + CONV_W] = jnp.where(
        valid, pe[:, POOL_W:POOL_W + CONV_W] * pe[:, POOL_W + CONV_W:], 0.0)

    m0 = HALO * nb
    mr = tm * nb
    p2 = jnp.dot(he[m0:m0 + mr], wi_ref[:, n1:], preferred_element_type=F32)
    b_gate = p2[:, :CONV_W]
    sz = _silu(p2[:, CONV_W:])

    tpos = t0 + lax.broadcasted_iota(jnp.int32, (mr, 1), 0) // nb
    pooled = []
    for gi, w in enumerate(POOL_WINDOWS):
        c0 = gi * POOL_GW
        s = pe_ref[:, c0:c0 + POOL_GW]
        n = ne
        width = 1
        while width < w:
            s = s[0:(n - width) * nb] + s[width * nb:n * nb]
            n -= width
            width *= 2
        start = (HALO - w // 2) * nb
        total = s[start:start + mr]
        hi = jnp.minimum(tpos + w // 2, t_total)
        lo = jnp.maximum(tpos - w // 2, 0)
        cnt = (hi - lo).astype(F32)
        centre = pe_ref[m0:m0 + mr, c0:c0 + POOL_GW]
        pg = total / cnt - centre
        pooled.append(jnp.dot(pg.astype(BF16), pw_ref[gi], preferred_element_type=F32))
    y_c = jnp.concatenate(pooled, axis=1) * ps_ref[...]

    cwt = cw_ref[...]
    vm = pe_ref[m0 - nb:m0 - nb + mr, POOL_W:POOL_W + CONV_W]
    v0 = pe_ref[m0:m0 + mr, POOL_W:POOL_W + CONV_W]
    vp = pe_ref[m0 + nb:m0 + nb + mr, POOL_W:POOL_W + CONV_W]
    y_d = b_gate * (vm * cwt[0:1, :] + v0 * cwt[1:2, :] + vp * cwt[2:3, :])

    y = (jnp.concatenate([y_c, y_d], axis=1) * sz).astype(BF16)
    out = jnp.dot(y, wo_ref[...], preferred_element_type=F32)
    go = (rows_of(gate_ref) * out.reshape(tm, nb, D)).reshape(ct, CH, nb, D)
    for i in range(CH):
        xo = xm_ref[:, :, i * D:(i + 1) * D] + go[:, i]
        if not final:
            o_ref[:, :, i * D:(i + 1) * D] = xo
            continue
        ms = jnp.mean(xo * xo, axis=-1, keepdims=True)
        fin_ref[:, :, i * D:(i + 1) * D] = xo * lax.rsqrt(ms + EPS) * fg_ref[...]
    if final:
        o_ref[...] = pltpu.einshape("cb(id)->b(ci)d", fin_ref[...], i=CH)


def _odd_layer(xs, ncl, with_ctx, ada, l, g, w_in, w_out, pool_w, layer, pool_scale, conv_w, final_g):
    nc_all, bsz, _ = xs.shape
    ct = 8
    ntl = ncl // ct
    nc = nc_all if with_ctx else ncl
    final = final_g is not None
    args = [xs, xs, xs, ada, ada, ada, g.reshape(1, D), w_in, w_out, pool_w,
            pool_scale.reshape(1, POOL_W), conv_w]

    def vec(j):
        return pl.BlockSpec((None, None) + ada.shape[2:], lambda i: (l, j, 0, 0))

    def const(*shape):
        return pl.BlockSpec(shape, lambda i: (0,) * len(shape), pipeline_mode=pl.Buffered(1))

    def stacked(*shape):
        return pl.BlockSpec((None,) + shape, lambda i: (layer,) + (0,) * len(shape),
                            pipeline_mode=pl.Buffered(1))

    specs = [pl.BlockSpec((ct, bsz, CH * D), lambda i: (i, 0, 0)),
             pl.BlockSpec((1, bsz, CH * D), lambda i: (jnp.maximum(i * ct - 1, 0), 0, 0)),
             pl.BlockSpec((1, bsz, CH * D), lambda i: (jnp.minimum((i + 1) * ct, nc_all - 1), 0, 0)),
             vec(0), vec(1), vec(2),
             const(1, D), stacked(*w_in.shape[1:]), stacked(MIX, D),
             stacked(len(POOL_WINDOWS), POOL_GW, POOL_GW), const(1, POOL_W), const(3, CONV_W)]
    scratch = [pltpu.VMEM((ct + 1, CH, bsz, D), F32),
               pltpu.VMEM(((ct + 1) * CH * bsz, POOL_W + CONV_W), F32),
               pltpu.VMEM(w_in.shape[1:], BF16), pltpu.VMEM((MIX, D), BF16),
               pltpu.VMEM((len(POOL_WINDOWS), POOL_GW, POOL_GW), BF16)]
    if final:
        args.append(final_g.reshape(1, D))
        specs.append(const(1, D))
        scratch.append(pltpu.VMEM((ct, bsz, CH * D), F32))
        out_spec = pl.BlockSpec((bsz, ct * CH, D), lambda i: (0, i, 0))
        out_shape = jax.ShapeDtypeStruct((bsz, nc * CH, D), F32)
    else:
        out_spec = pl.BlockSpec((ct, bsz, CH * D), lambda i: (i, 0, 0))
        out_shape = jax.ShapeDtypeStruct((nc, bsz, CH * D), F32)
    return pl.pallas_call(
        functools.partial(_odd_kernel, ct=ct, ntl=ntl, t_lat=ncl * CH, t_ctx=(nc_all - ncl) * CH,
                          final=final),
        grid=(nc // ct,),
        in_specs=specs,
        out_specs=out_spec,
        out_shape=out_shape,
        scratch_shapes=scratch,
        compiler_params=_cparams(("arbitrary",)),
    )(*args)


def _sincos_table(n_tok, dim):
    rows = n_tok // GRID_W
    rr, cc = np.meshgrid(np.arange(rows, dtype=np.float64), np.arange(GRID_W, dtype=np.float64),
                         indexing='ij')
    rr = rr.reshape(-1, 1)
    cc = cc.reshape(-1, 1)
    quarter = dim // 4
    omega = POS_BASE ** (-np.arange(quarter, dtype=np.float64) / quarter)
    tab = np.concatenate([np.sin(rr * omega), np.cos(rr * omega), np.sin(cc * omega), np.cos(cc * omega)],
                         axis=-1)
    return jnp.asarray(tab, dtype=F32)


def _block_diag(w):
    g, c, _ = w.shape
    eye = jnp.eye(g, dtype=w.dtype)
    return (eye[:, None, :, None] * w[:, :, None, :]).reshape(g * c, g * c)


def _to_stream_kernel(x_ref, pos_ref, c_ref, o_ref, *, ntl):
    step = pl.program_id(0)

    @pl.when(step < ntl)
    def _():
        o_ref[...] = pltpu.einshape("b(ci)d->cb(id)", x_ref[...] + pos_ref[...][None], i=CH)

    @pl.when(step >= ntl)
    def _():
        o_ref[...] = pltpu.einshape("b(ci)d->cb(id)", c_ref[...], i=CH)


def _to_stream(x, pos, ctx):
    bsz, t, _ = x.shape
    tc = ctx.shape[1]
    ct = 8
    tm = ct * CH
    ntl = t // tm
    return pl.pallas_call(
        functools.partial(_to_stream_kernel, ntl=ntl),
        grid=((t + tc) // tm,),
        in_specs=[pl.BlockSpec((bsz, tm, D), lambda i: (0, jnp.minimum(i, ntl - 1), 0)),
                  pl.BlockSpec((tm, D), lambda i: (jnp.minimum(i, ntl - 1), 0)),
                  pl.BlockSpec((bsz, tm, D), lambda i: (0, jnp.maximum(i - ntl, 0), 0))],
        out_specs=pl.BlockSpec((ct, bsz, CH * D), lambda i: (i, 0, 0)),
        out_shape=jax.ShapeDtypeStruct(((t + tc) // CH, bsz, CH * D), F32),
        compiler_params=_cparams(("arbitrary",)),
    )(x, pos, ctx)


def kernel(x, c, ctx, c_ctx, norm_g, ada_w, ada_b, even_w_in, even_w_out, s5_lam_re, s5_lam_im, s5_log_step, s5_b_re, s5_b_im, s5_c_re, s5_c_im, s5_d, s5_glu_w, s5_glu_b, fnet_w, odd_w_in, odd_w_out, pool_w, pool_scale, conv_w, final_g):
    bsz, n_tok, _ = x.shape
    depth = norm_g.shape[0]
    ncl = n_tok // CH
    ncc = ctx.shape[1] // CH

    cond = jnp.concatenate([c, jnp.broadcast_to(c_ctx[None], (16 - bsz, D))], axis=0)
    ada, mt, bend, cp, lam16 = _prepare(cond, ada_w, ada_b, s5_lam_re, s5_lam_im, s5_log_step, s5_b_re,
                                        s5_b_im, s5_c_re, s5_c_im, s5_d)

    need_ctx = [any(j % 2 == 0 for j in range(l + 1, depth)) for l in range(depth)]

    xs = _to_stream(x, _sincos_table(n_tok, D), ctx)
    for l in range(depth):
        i = l // 2
        last = l == depth - 1
        if l % 2 == 0:
            wcs = _fnet_weights(_block_diag(fnet_w[i]))
            ut, xw, z = _even_in(xs, ncl, ada, l, norm_g[l], even_w_in, i, wcs)
            yt = _s5_mix(ut, ncl, mt, bend, cp, lam16, bsz, i)
            ybl = _fnet(xw, 0, ncl, bsz)
            ybc = _fnet(xw, ncl, ncc, bsz) if need_ctx[l] else None
            xs = _even_out(yt, ybl, ybc, z, xs, ncl, ada, l, s5_glu_w, s5_glu_b[i], even_w_out, i)
        else:
            xs = _odd_layer(xs, ncl, need_ctx[l], ada, l, norm_g[l], odd_w_in, odd_w_out, pool_w, i,
                            pool_scale[i], conv_w[i], final_g if last else None)
    if depth % 2 == 1:
        raise NotImplementedError("final norm and (B, T, D) order are produced by the last (odd) layer")
    return xs
```

```python
import functools
import math

import numpy as np
import jax
import jax.numpy as jnp
from jax import lax
from jax.experimental import pallas as pl
from jax.experimental.pallas import tpu as pltpu

D = 1024
MIX = 1024
S5_W = 768
FN_W = 256
S5_H = 16
S5_G = 48
S5_P = 64
FN_G = 4
FN_GW = 64
POOL_W = 512
CONV_W = 512
POOL_WINDOWS = (2, 4, 8, 16)
POOL_GW = 128
GRID_W = 64
EPS = 1e-6
POS_BASE = 10000.0
CH = 16
HALO = 8
LANE = 128
VMEM_LIMIT = 56 * 1024 * 1024

F32 = jnp.float32
BF16 = jnp.bfloat16
HI = lax.Precision.HIGHEST


def _cparams(sem):
    return pltpu.CompilerParams(dimension_semantics=sem, vmem_limit_bytes=VMEM_LIMIT)


def _sigmoid(v):
    return 0.5 * jnp.tanh(0.5 * v) + 0.5


def _silu(v):
    return v * _sigmoid(v)


def _gelu_tanh(v):
    c = math.sqrt(2.0 / math.pi)
    return 0.5 * v * (1.0 + jnp.tanh(c * (v + 0.044715 * (v * v * v))))


def _mod_norm(x, g, scale, shift):
    ms = jnp.mean(x * x, axis=-1, keepdims=True)
    y = x * lax.rsqrt(ms + EPS) * g
    return y * (1.0 + scale) + shift


def _split3(v):
    hi = v.astype(BF16)
    lo = (v - hi.astype(F32)).astype(BF16)
    return hi, lo


def _ada_item(c_ref, w_ref, b_ref, ada_ref):
    s_hi, s_lo = _split3(_silu(c_ref[...]))
    w_hi, w_lo = _split3(w_ref[...])
    ada_ref[...] = (jnp.dot(s_hi, w_hi, preferred_element_type=F32)
                    + jnp.dot(s_hi, w_lo, preferred_element_type=F32)
                    + jnp.dot(s_lo, w_hi, preferred_element_type=F32) + b_ref[...])


def _stream_item(x_ref, pos_ref, c_ref, o_ref, step, ntl):
    @pl.when(step < ntl)
    def _():
        o_ref[...] = pltpu.einshape("b(ci)d->cb(id)", x_ref[...] + pos_ref[...][None], i=CH)

    @pl.when(step >= ntl)
    def _():
        o_ref[...] = pltpu.einshape("b(ci)d->cb(id)", c_ref[...], i=CH)


def _prologue_kernel(c_ref, w_ref, b_ref, lr_ref, li_ref, ls_ref, btr_ref, bti_ref, cr_ref, ci_ref, d_ref,
                     x_ref, pos_ref, ctx_ref, ada_ref, mt_ref, be_ref, cp_ref, l16_ref, xs_ref,
                     ere_ref, eim_ref, *, gb, n_ada, n_tab, ntl):
    step = pl.program_id(0)

    @pl.when(step < n_ada)
    def _():
        _ada_item(c_ref, w_ref, b_ref, ada_ref)

    @pl.when(step < n_tab)
    def _():
        _tables_item(lr_ref, li_ref, ls_ref, btr_ref, bti_ref, cr_ref, ci_ref, d_ref,
                     mt_ref, be_ref, cp_ref, l16_ref, ere_ref, eim_ref, gb)

    _stream_item(x_ref, pos_ref, ctx_ref, xs_ref, step, ntl)


def _tables_item(lr_ref, li_ref, ls_ref, btr_ref, bti_ref, cr_ref, ci_ref, d_ref,
                 mt_ref, be_ref, cp_ref, l16_ref, ere_ref, eim_ref, gb):
    nst = 2 * S5_P
    kk = CH * S5_H
    step = jnp.exp(ls_ref[...])
    lr = lr_ref[...]
    li = li_ref[...]
    a = lr * step
    b = li * step

    def powers(expo):
        mag = jnp.exp(expo * a)
        return mag * jnp.cos(expo * b), mag * jnp.sin(expo * b)

    row = lax.broadcasted_iota(jnp.int32, (1, CH, nst), 1).astype(F32)
    fwd = lax.broadcasted_iota(jnp.int32, (1, CH, nst), 2) < S5_P
    one = jnp.ones((1, 1, nst), F32)

    l1re, l1im = powers(one)
    n_re = l1re - 1.0
    den = lr * lr + li * li
    co_re = (n_re * lr + l1im * li) / den
    co_im = (l1im * lr - n_re * li) / den
    btr = btr_ref[...]
    bti = bti_ref[...]
    bb_re = co_re * btr - co_im * bti
    bb_im = co_re * bti + co_im * btr

    pe_re, pe_im = powers(jnp.where(fwd, (CH - 1) - row, row))
    for l in range(CH):
        pr = pe_re[:, l:l + 1, :]
        pi = pe_im[:, l:l + 1, :]
        ere_ref[:, l * S5_H:(l + 1) * S5_H, :] = pr * bb_re - pi * bb_im
        eim_ref[:, l * S5_H:(l + 1) * S5_H, :] = pr * bb_im + pi * bb_re

    cr = cr_ref[...]
    ci = ci_ref[...]
    pc_re, pc_im = powers(jnp.where(fwd, row + 1.0, CH - row))
    for j in range(CH):
        pr = pc_re[:, j:j + 1, :]
        pi = pc_im[:, j:j + 1, :]
        w_re = cr * pr - ci * pi
        w_im = cr * pi + ci * pr
        cp_ref[:, j * S5_H:(j + 1) * S5_H, :] = jnp.concatenate([w_re, -w_im], axis=2).astype(BF16)

    l16re, l16im = powers(one * float(CH))
    l16_ref[...] = jnp.concatenate([l16re, l16im], axis=1)

    fwd2 = lax.broadcasted_iota(jnp.int32, (S5_H, nst), 1) < S5_P
    lane = lax.broadcasted_iota(jnp.int32, (S5_H, kk), 1)
    iblk = lane // S5_H
    hrow = lax.broadcasted_iota(jnp.int32, (S5_H, kk), 0)
    nt = (((1,), (1,)), ((), ()))
    for g in range(gb):
        ere = ere_ref[g]
        eim = eim_ref[g]
        e2 = jnp.concatenate([ere, eim], axis=1)
        be_ref[g] = e2.astype(BF16)
        e_hi, e_lo = _split3(e2)
        rhs = jnp.concatenate([e_hi, e_hi, e_lo], axis=1)
        crg = cr[g]
        cig = ci[g]

        c2 = jnp.concatenate([
            jnp.concatenate([jnp.where(fwd2, crg, 0.0), jnp.where(fwd2, -cig, 0.0)], axis=1),
            jnp.concatenate([jnp.where(fwd2, 0.0, crg), jnp.where(fwd2, 0.0, -cig)], axis=1)], axis=0)
        c_hi, c_lo = _split3(c2)
        lhs = jnp.concatenate([c_hi, c_lo, c_hi], axis=1)
        kfb = lax.dot_general(lhs, rhs, nt, preferred_element_type=F32)
        kf = kfb[0:S5_H]
        kb = kfb[S5_H:2 * S5_H]
        dg = d_ref[g]
        for j in range(CH):
            sf = (kk - (CH - 1 - j) * S5_H) % kk
            rf = pltpu.roll(kf, sf, 1) if sf else kf
            rb = pltpu.roll(kb, j * S5_H, 1) if j else kb
            blk = (jnp.where(iblk <= j, rf, 0.0) + jnp.where(iblk >= j, rb, 0.0)
                   + jnp.where(lane == j * S5_H + hrow, dg, 0.0))
            mt_ref[g, j * S5_H:(j + 1) * S5_H, :] = blk.astype(BF16)


def _prologue(cond, ada_w, ada_b, lam_re, lam_im, log_step, b_re, b_im, c_re, c_im, d_skip, x, pos, ctx):
    n = lam_re.shape[0] * S5_G
    gb = 8
    nst = 2 * S5_P
    kk = CH * S5_H

    def fb(v):
        return jnp.concatenate([v[:, 0], v[:, 1]], axis=-1).reshape(n, v.shape[3], nst)

    lr = fb(lam_re[:, :, :, None, :])
    li = fb(lam_im[:, :, :, None, :])
    ls = fb(jnp.broadcast_to(log_step[:, :, :, None, None], log_step.shape + (1, S5_P)))
    btr = fb(jnp.swapaxes(b_re, -1, -2))
    bti = fb(jnp.swapaxes(b_im, -1, -2))
    cr = fb(c_re)
    ci = fb(c_im)
    d = d_skip.reshape(n, S5_H, 1)

    depth = ada_w.shape[0]
    n_ada = depth * 3
    n_tab = n // gb
    bsz, t, _ = x.shape
    tc = ctx.shape[1]
    ct = 8
    tm = ct * CH
    ntl = t // tm
    n_str = (t + tc) // tm

    def spec(r, c):
        return pl.BlockSpec((gb, r, c), lambda s: (jnp.minimum(s, n_tab - 1), 0, 0))

    def ada_idx(s):
        s = jnp.minimum(s, n_ada - 1)
        return s // 3, s % 3

    def lat_tile(s):
        return jnp.minimum(s, ntl - 1)

    return pl.pallas_call(
        functools.partial(_prologue_kernel, gb=gb, n_ada=n_ada, n_tab=n_tab, ntl=ntl),
        grid=(max(n_ada, n_tab, n_str),),
        in_specs=[pl.BlockSpec((16, D), lambda s: (0, 0)),
                  pl.BlockSpec((None, D, D), lambda s: (ada_idx(s)[0], 0, ada_idx(s)[1])),
                  pl.BlockSpec((None, None, 1, D), lambda s: ada_idx(s) + (0, 0)),
                  spec(1, nst), spec(1, nst), spec(1, nst), spec(S5_H, nst), spec(S5_H, nst),
                  spec(S5_H, nst), spec(S5_H, nst), spec(S5_H, 1),
                  pl.BlockSpec((bsz, tm, D), lambda s: (0, lat_tile(s), 0)),
                  pl.BlockSpec((tm, D), lambda s: (lat_tile(s), 0)),
                  pl.BlockSpec((bsz, tm, D), lambda s: (0, jnp.clip(s - ntl, 0, tc // tm - 1), 0),
                               pipeline_mode=pl.Buffered(1))],
        out_specs=[pl.BlockSpec((None, None, 16, D), lambda s: ada_idx(s) + (0, 0)),
                   spec(kk, kk), spec(kk, 2 * nst), spec(kk, 2 * nst), spec(2, nst),
                   pl.BlockSpec((ct, bsz, CH * D), lambda s: (jnp.minimum(s, n_str - 1), 0, 0))],
        out_shape=[jax.ShapeDtypeStruct((depth, 3, 16, D), F32),
                   jax.ShapeDtypeStruct((n, kk, kk), BF16),
                   jax.ShapeDtypeStruct((n, kk, 2 * nst), BF16),
                   jax.ShapeDtypeStruct((n, kk, 2 * nst), BF16),
                   jax.ShapeDtypeStruct((n, 2, nst), F32),
                   jax.ShapeDtypeStruct(((t + tc) // CH, bsz, CH * D), F32)],
        scratch_shapes=[pltpu.VMEM((gb, kk, nst), F32)] * 2,
        compiler_params=_cparams(("arbitrary",)),
    )(cond, ada_w, ada_b.reshape(depth, 3, 1, D), lr, li, ls, btr, bti, cr, ci, d, x, pos, ctx)


def _even_in_kernel(x_ref, shift_ref, scale_ref, g_ref, w_ref, wcs_ref, ut_ref, xw_ref, z_ref,
                    wat_ref, wbz_ref, *scr, ncl):
    @pl.when(pl.program_id(0) == 0)
    def _():
        wat_ref[...] = w_ref[:, :S5_W].T.astype(BF16)
        wbz_ref[...] = w_ref[:, S5_W:].astype(BF16)

    x = x_ref[...]
    nc, nb, _ = x.shape
    rows = nc * nb
    g = g_ref[...]
    h = jnp.concatenate([_mod_norm(x[0:ncl], g, scale_ref[0:nb][None], shift_ref[0:nb][None]),
                         _mod_norm(x[ncl:nc], g, scale_ref[nb:nb + 1][None], shift_ref[nb:nb + 1][None])],
                        axis=0)
    hb = h.reshape(rows, D).astype(BF16)
    pt = lax.dot_general(wat_ref[...], hb, (((1,), (1,)), ((), ())), preferred_element_type=F32)
    ut_ref[...] = pt.astype(BF16).reshape(S5_G, S5_H, rows)
    p = jnp.dot(hb, wbz_ref[...], preferred_element_type=F32)
    z_ref[...] = _silu(p[:, FN_W:]).astype(BF16)
    xw = jnp.dot(p[:, :FN_W].astype(BF16), wcs_ref[...], preferred_element_type=F32)
    for q in range(4):
        scr[q][...] = xw[:, q * LANE:(q + 1) * LANE]
    for q in range(4):
        part, half = divmod(q, 2)
        for bi in range(nb):
            piece = scr[q][pl.ds(bi, nc, stride=nb), :]
            lo = bi * FN_W + half * LANE
            xw_ref[part, :, lo:lo + LANE] = piece.astype(BF16)


def _even_in(xs, ncl, ada, l, g, w_in, layer, wcs):
    nc, bsz, _ = xs.shape
    rows = nc * bsz
    return pl.pallas_call(
        functools.partial(_even_in_kernel, ncl=ncl),
        grid=(CH,),
        in_specs=[pl.BlockSpec((nc, bsz, D), lambda i: (0, 0, i)),
                  pl.BlockSpec((None, None) + ada.shape[2:], lambda i: (l, 0, 0, 0)),
                  pl.BlockSpec((None, None) + ada.shape[2:], lambda i: (l, 1, 0, 0)),
                  pl.BlockSpec((1, D), lambda i: (0, 0)),
                  pl.BlockSpec((None, D, w_in.shape[2]), lambda i: (layer, 0, 0),
                               pipeline_mode=pl.Buffered(1)),
                  pl.BlockSpec((FN_W, 2 * FN_W), lambda i: (0, 0))],
        out_specs=[pl.BlockSpec((S5_G, S5_H, rows), lambda i: (0, i, 0)),
                   pl.BlockSpec((2, None, nc, bsz * FN_W), lambda i: (0, i, 0, 0)),
                   pl.BlockSpec((None, rows, MIX), lambda i: (i, 0, 0))],
        out_shape=[jax.ShapeDtypeStruct((S5_G, CH * S5_H, rows), BF16),
                   jax.ShapeDtypeStruct((2, CH, nc, bsz * FN_W), BF16),
                   jax.ShapeDtypeStruct((CH, rows, MIX), BF16)],
        scratch_shapes=[pltpu.VMEM((S5_W, D), BF16), pltpu.VMEM((D, FN_W + MIX), BF16)]
        + [pltpu.VMEM((rows, LANE), F32)] * 4,
        compiler_params=_cparams(("arbitrary",)),
    )(xs, ada, ada, g.reshape(1, D), w_in, wcs)


def _s5_kernel(ut_ref, mt_ref, be_ref, cp_ref, l16_ref, yt_ref,
               sre_ref, sim_ref, are_ref, aim_ref, bre_ref, bim_ref, *, bsz, ncl, ncc, gs):
    nl = bsz * ncl
    nst = 2 * S5_P
    for g in range(gs):
        st = lax.dot_general(ut_ref[g], be_ref[g], (((0,), (0,)), ((), ())), preferred_element_type=F32)
        sre_ref[g] = st[:, :nst]
        sim_ref[g] = st[:, nst:]

    lam = [l16_ref[g] for g in range(gs)]
    is_fwd = lax.broadcasted_iota(jnp.int32, (bsz, nst), 1) < S5_P

    def make_step(base, nchunk):
        def step(c, carry):
            rf = pl.ds(pl.multiple_of(base + c * bsz, bsz), bsz)
            rb = pl.ds(pl.multiple_of(base + (nchunk - 1 - c) * bsz, bsz), bsz)
            out = []
            for g in range(gs):
                sre, sim = carry[2 * g], carry[2 * g + 1]
                lre = lam[g][0:1, :]
                lim = lam[g][1:2, :]
                are_ref[g, rf, :] = sre
                aim_ref[g, rf, :] = sim
                bre_ref[g, rb, :] = sre
                bim_ref[g, rb, :] = sim
                in_re = jnp.where(is_fwd, sre_ref[g, rf, :], sre_ref[g, rb, :])
                in_im = jnp.where(is_fwd, sim_ref[g, rf, :], sim_ref[g, rb, :])
                out.append(lre * sre - lim * sim + in_re)
                out.append(lre * sim + lim * sre + in_im)
            return tuple(out)
        return step

    zero = jnp.zeros((bsz, nst), F32)
    carry = lax.fori_loop(0, ncc, make_step(nl, ncc), (zero,) * (2 * gs))
    lax.fori_loop(0, ncl, make_step(0, ncl), carry)

    nt = (((1,), (1,)), ((), ()))
    fwd_rows = lax.broadcasted_iota(jnp.int32, (bsz * (ncl + ncc), nst), 1) < S5_P
    for g in range(gs):
        s0 = jnp.concatenate([jnp.where(fwd_rows, are_ref[g], bre_ref[g]),
                              jnp.where(fwd_rows, aim_ref[g], bim_ref[g])], axis=1).astype(BF16)
        yt_ref[g] = (jnp.dot(mt_ref[g], ut_ref[g], preferred_element_type=F32)
                     + lax.dot_general(cp_ref[g], s0, nt, preferred_element_type=F32)).astype(BF16)


def _s5_mix(ut, ncl, mt, bend, cp, lam16, bsz, layer):
    rows = ut.shape[2]
    kk = CH * S5_H
    nst = 2 * S5_P
    gs = 4
    off = layer * (S5_G // gs)

    def gspec(r, c):
        return pl.BlockSpec((gs, r, c), lambda g: (g, 0, 0))

    def tspec(r, c):
        return pl.BlockSpec((gs, r, c), lambda g: (g + off, 0, 0))

    return pl.pallas_call(
        functools.partial(_s5_kernel, bsz=bsz, ncl=ncl, ncc=rows // bsz - ncl, gs=gs),
        grid=(S5_G // gs,),
        in_specs=[gspec(kk, rows), tspec(kk, kk), tspec(kk, 2 * nst), tspec(kk, 2 * nst), tspec(2, nst)],
        out_specs=gspec(kk, rows),
        out_shape=jax.ShapeDtypeStruct((S5_G, kk, rows), BF16),
        scratch_shapes=[pltpu.VMEM((gs, rows, nst), F32)] * 6,
        compiler_params=_cparams(("arbitrary",)),
    )(ut, mt, bend, cp, lam16)


def _fnet_weights_kernel(ccs_ref, fw_ref, o_ref):
    fw = fw_ref[...]
    ccs = ccs_ref[...]
    wc = jnp.dot(ccs[:, :FN_W], fw, precision=HI, preferred_element_type=F32)
    ws = jnp.dot(ccs[:, FN_W:], fw, precision=HI, preferred_element_type=F32)
    o_ref[...] = jnp.concatenate([wc, ws], axis=1).astype(BF16)


def _fnet_weights(fw_bd):
    return pl.pallas_call(
        _fnet_weights_kernel,
        out_shape=jax.ShapeDtypeStruct((FN_W, 2 * FN_W), BF16),
    )(_dft_channel_matrix(), fw_bd)


def _fnet_kernel(tab_ref, v_ref, jr_ref, o_ref, *, ipt):
    r = pl.program_id(1)
    tab = tab_ref[...].astype(BF16)
    v = v_ref[...]
    nc, n = v.shape[2], v.shape[3]
    t = CH * nc
    a = jnp.dot(tab[:, :t], v[0].reshape(t, n), preferred_element_type=F32)
    b = jnp.dot(tab[:, t:], v[1].reshape(t, n), preferred_element_type=F32)
    o_ref[pl.ds(r * ipt, ipt)] = (a - b).reshape(ipt, nc, n).astype(BF16)
    m = (a + b).astype(BF16)
    for k in range(ipt):
        i = r * ipt + k

        @pl.when(jnp.logical_and(i >= 1, i <= CH // 2 - 1))
        def _():
            o_ref[CH - i] = jnp.dot(jr_ref[...], m[k * nc:(k + 1) * nc],
                                    preferred_element_type=F32).astype(BF16)


def _dft_half_table(nc):
    t = nc * CH
    cols = (np.arange(nc)[None, :] * CH + np.arange(CH)[:, None]).reshape(-1)
    rows = (np.arange(nc)[None, :] * CH + np.arange(CH // 2 + 1)[:, None]).reshape(-1)
    prod = (rows[:, None].astype(np.int64) * cols[None, :].astype(np.int64)) % t
    ang = prod.astype(np.float64) * (2.0 * np.pi / t)
    scale = 1.0 / math.sqrt(t * FN_GW)
    return jnp.asarray(np.concatenate([np.cos(ang), np.sin(ang)], axis=1) * scale, dtype=F32)


def _dft_channel_matrix():
    c = np.arange(FN_GW)
    ang = (c[:, None] * c[None, :] % FN_GW).astype(np.float64) * (2.0 * np.pi / FN_GW)
    eye = np.eye(FN_G)
    return jnp.asarray(np.concatenate([np.kron(eye, np.cos(ang)), np.kron(eye, np.sin(ang))], axis=1),
                       dtype=F32)


def _fnet(xw, c0, nc, bsz):
    t = nc * CH
    nhb = CH // 2 + 1
    ipt = 3
    bpb = 4
    jr = jnp.asarray(np.eye(nc)[::-1], dtype=BF16)
    return pl.pallas_call(
        functools.partial(_fnet_kernel, ipt=ipt),
        grid=(bsz // bpb, nhb // ipt),
        in_specs=[pl.BlockSpec((ipt * nc, 2 * t), lambda b, r: (r, 0)),
                  pl.BlockSpec((2, CH, nc, bpb * FN_W), lambda b, r: (0, 0, c0 // nc, b)),
                  pl.BlockSpec((nc, nc), lambda b, r: (0, 0))],
        out_specs=pl.BlockSpec((CH, nc, bpb * FN_W), lambda b, r: (0, 0, b)),
        out_shape=jax.ShapeDtypeStruct((CH, nc, bsz * FN_W), BF16),
        compiler_params=_cparams(("arbitrary", "arbitrary")),
    )(_dft_half_table(nc), xw, jr)


def _even_out_kernel(*refs, ncl, with_ctx):
    it = iter(refs)
    yt_ref, ybl_ref = next(it), next(it)
    ybc_ref = next(it) if with_ctx else None
    z_ref, x_ref, gate_ref, gwf_ref, gb_ref, wof_ref, o_ref = (next(it), next(it), next(it), next(it),
                                                               next(it), next(it), next(it))
    scr = [next(it), next(it)]
    gw_ref, wo_ref = next(it), next(it)

    @pl.when(pl.program_id(0) == 0)
    def _():
        gw_ref[...] = gwf_ref[...].astype(BF16)
        wo_ref[...] = wof_ref[...].astype(BF16)

    yt = yt_ref[...]
    rows = yt.shape[2]
    ya = _gelu_tanh(yt.astype(F32).reshape(S5_W, rows).T)
    glu = jnp.dot(ya.astype(BF16), gw_ref[...], preferred_element_type=F32) + gb_ref[...]
    ya = ya * _sigmoid(glu)
    x = x_ref[...]
    nc, nb, _ = x.shape
    for half in range(2):
        for bi in range(nb):
            lo = bi * FN_W + half * LANE
            scr[half][pl.ds(bi, ncl, stride=nb), :] = ybl_ref[:, lo:lo + LANE].astype(F32)
            if with_ctx:
                scr[half][pl.ds(ncl * nb + bi, nc - ncl, stride=nb), :] = (
                    ybc_ref[:, lo:lo + LANE].astype(F32))
    yb = jnp.concatenate([scr[0][...], scr[1][...]], axis=1)
    sz = z_ref[...]
    ma = (ya * sz[:, :S5_W]).astype(BF16)
    mb = (yb * sz[:, S5_W:]).astype(BF16)
    out = (jnp.dot(ma, wo_ref[0:S5_W, :], preferred_element_type=F32)
           + jnp.dot(mb, wo_ref[S5_W:MIX, :], preferred_element_type=F32)).reshape(x.shape)
    o_ref[0:ncl] = x[0:ncl] + gate_ref[0:nb][None] * out[0:ncl]
    if with_ctx:
        o_ref[ncl:nc] = x[ncl:nc] + gate_ref[nb:nb + 1][None] * out[ncl:nc]


def _even_out(yt, ybl, ybc, z, xs, ncl, ada, l, glu_w, glu_b, w_out, layer):
    bsz = xs.shape[1]
    with_ctx = ybc is not None
    nc = xs.shape[0] if with_ctx else ncl
    rows = nc * bsz
    args = [yt, ybl]
    specs = [pl.BlockSpec((S5_G, S5_H, rows), lambda j: (0, j, 0)),
             pl.BlockSpec((None, ncl, bsz * FN_W), lambda j: (j, 0, 0))]
    if with_ctx:
        args.append(ybc)
        specs.append(pl.BlockSpec((None, nc - ncl, bsz * FN_W), lambda j: (j, 0, 0)))
    args += [z, xs, ada, glu_w, glu_b.reshape(1, S5_W), w_out]
    specs += [pl.BlockSpec((None, rows, MIX), lambda j: (j, 0, 0)),
              pl.BlockSpec((nc, bsz, D), lambda j: (0, 0, j)),
              pl.BlockSpec((None, None) + ada.shape[2:], lambda j: (l, 2, 0, 0)),
              pl.BlockSpec((None, S5_W, S5_W), lambda j: (layer, 0, 0), pipeline_mode=pl.Buffered(1)),
              pl.BlockSpec((1, S5_W), lambda j: (0, 0)),
              pl.BlockSpec((None, MIX, D), lambda j: (layer, 0, 0), pipeline_mode=pl.Buffered(1))]
    return pl.pallas_call(
        functools.partial(_even_out_kernel, ncl=ncl, with_ctx=with_ctx),
        grid=(CH,),
        in_specs=specs,
        out_specs=pl.BlockSpec((nc, bsz, D), lambda j: (0, 0, j)),
        out_shape=jax.ShapeDtypeStruct((nc, bsz, CH * D), F32),
        scratch_shapes=[pltpu.VMEM((rows, LANE), F32)] * 2 + [pltpu.VMEM((S5_W, S5_W), BF16),
                                                             pltpu.VMEM((MIX, D), BF16)],
        compiler_params=_cparams(("arbitrary",)),
    )(*args)


def _odd_kernel(*refs, ct, ntl, t_lat, t_ctx, final):
    it = iter(refs)
    xm_ref, xp_ref, xn_ref = next(it), next(it), next(it)
    shift_ref, scale_ref, gate_ref, g_ref = next(it), next(it), next(it), next(it)
    wif_ref, wof_ref, pwf_ref, ps_ref, cw_ref = next(it), next(it), next(it), next(it), next(it)
    fg_ref = next(it) if final else None
    o_ref = next(it)
    h_ref = next(it)
    pe_ref = next(it)
    wi_ref, wo_ref, pw_ref = next(it), next(it), next(it)
    fin_ref = next(it) if final else None

    ti = pl.program_id(0)
    is_ctx = ti >= ntl
    t0 = jnp.where(is_ctx, ti - ntl, ti) * (ct * CH)
    t_total = jnp.where(is_ctx, t_ctx, t_lat)

    @pl.when(ti == 0)
    def _():
        wi_ref[...] = wif_ref[...].astype(BF16)
        wo_ref[...] = wof_ref[...].astype(BF16)
        pw_ref[...] = pwf_ref[...].astype(BF16)

    n1 = POOL_W + 2 * CONV_W
    nb = xm_ref.shape[1]
    tm = ct * CH
    ne = tm + 2 * HALO

    def rows_of(ref):
        return jnp.where(is_ctx, ref[nb:nb + 1], ref[0:nb])

    scale = rows_of(scale_ref)
    shift = rows_of(shift_ref)

    def hn(xv):
        return _mod_norm(xv, g_ref[...], scale, shift)

    for i in range(CH):
        hi = hn(xm_ref[:, :, i * D:(i + 1) * D])
        if i < HALO:
            h_ref[0:ct, i + HALO] = hi
        else:
            h_ref[1:ct + 1, i - HALO] = hi
    for i in range(HALO):
        h_ref[0, i] = hn(xp_ref[:, :, (i + HALO) * D:(i + HALO + 1) * D])[0]
        h_ref[ct, i + HALO] = hn(xn_ref[:, :, i * D:(i + 1) * D])[0]

    he = h_ref[...].reshape(ne * nb, D).astype(BF16)
    pe = jnp.dot(he, wi_ref[:, 0:n1], preferred_element_type=F32)
    te = t0 - HALO + lax.broadcasted_iota(jnp.int32, (ne * nb, 1), 0) // nb
    valid = jnp.logical_and(te >= 0, te < t_total)
    pe_ref[:, 0:POOL_W] = jnp.where(valid, pe[:, :POOL_W], 0.0)
    pe_ref[:, POOL_W:POOL_W + CONV_W] = jnp.where(
        valid, pe[:, POOL_W:POOL_W + CONV_W] * pe[:, POOL_W + CONV_W:], 0.0)

    m0 = HALO * nb
    mr = tm * nb
    p2 = jnp.dot(he[m0:m0 + mr], wi_ref[:, n1:], preferred_element_type=F32)
    b_gate = p2[:, :CONV_W]
    sz = _silu(p2[:, CONV_W:])

    tpos = t0 + lax.broadcasted_iota(jnp.int32, (mr, 1), 0) // nb
    pooled = []
    for gi, w in enumerate(POOL_WINDOWS):
        c0 = gi * POOL_GW
        s = pe_ref[:, c0:c0 + POOL_GW]
        n = ne
        width = 1
        while width < w:
            s = s[0:(n - width) * nb] + s[width * nb:n * nb]
            n -= width
            width *= 2
        start = (HALO - w // 2) * nb
        total = s[start:start + mr]
        hi = jnp.minimum(tpos + w // 2, t_total)
        lo = jnp.maximum(tpos - w // 2, 0)
        cnt = (hi - lo).astype(F32)
        centre = pe_ref[m0:m0 + mr, c0:c0 + POOL_GW]
        pg = total / cnt - centre
        pooled.append(jnp.dot(pg.astype(BF16), pw_ref[gi], preferred_element_type=F32))
    y_c = jnp.concatenate(pooled, axis=1) * ps_ref[...]

    cwt = cw_ref[...]
    vm = pe_ref[m0 - nb:m0 - nb + mr, POOL_W:POOL_W + CONV_W]
    v0 = pe_ref[m0:m0 + mr, POOL_W:POOL_W + CONV_W]
    vp = pe_ref[m0 + nb:m0 + nb + mr, POOL_W:POOL_W + CONV_W]
    y_d = b_gate * (vm * cwt[0:1, :] + v0 * cwt[1:2, :] + vp * cwt[2:3, :])

    y = (jnp.concatenate([y_c, y_d], axis=1) * sz).astype(BF16)
    out = jnp.dot(y, wo_ref[...], preferred_element_type=F32)
    go = (rows_of(gate_ref) * out.reshape(tm, nb, D)).reshape(ct, CH, nb, D)
    for i in range(CH):
        xo = xm_ref[:, :, i * D:(i + 1) * D] + go[:, i]
        if not final:
            o_ref[:, :, i * D:(i + 1) * D] = xo
            continue
        ms = jnp.mean(xo * xo, axis=-1, keepdims=True)
        fin_ref[:, :, i * D:(i + 1) * D] = xo * lax.rsqrt(ms + EPS) * fg_ref[...]
    if final:
        o_ref[...] = pltpu.einshape("cb(id)->b(ci)d", fin_ref[...], i=CH)


def _odd_layer(xs, ncl, with_ctx, ada, l, g, w_in, w_out, pool_w, layer, pool_scale, conv_w, final_g):
    nc_all, bsz, _ = xs.shape
    ct = 8
    ntl = ncl // ct
    nc = nc_all if with_ctx else ncl
    final = final_g is not None
    args = [xs, xs, xs, ada, ada, ada, g.reshape(1, D), w_in, w_out, pool_w,
            pool_scale.reshape(1, POOL_W), conv_w]

    def vec(j):
        return pl.BlockSpec((None, None) + ada.shape[2:], lambda i: (l, j, 0, 0))

    def const(*shape):
        return pl.BlockSpec(shape, lambda i: (0,) * len(shape), pipeline_mode=pl.Buffered(1))

    def stacked(*shape):
        return pl.BlockSpec((None,) + shape, lambda i: (layer,) + (0,) * len(shape),
                            pipeline_mode=pl.Buffered(1))

    specs = [pl.BlockSpec((ct, bsz, CH * D), lambda i: (i, 0, 0)),
             pl.BlockSpec((1, bsz, CH * D), lambda i: (jnp.maximum(i * ct - 1, 0), 0, 0)),
             pl.BlockSpec((1, bsz, CH * D), lambda i: (jnp.minimum((i + 1) * ct, nc_all - 1), 0, 0)),
             vec(0), vec(1), vec(2),
             const(1, D), stacked(*w_in.shape[1:]), stacked(MIX, D),
             stacked(len(POOL_WINDOWS), POOL_GW, POOL_GW), const(1, POOL_W), const(3, CONV_W)]
    scratch = [pltpu.VMEM((ct + 1, CH, bsz, D), F32),
               pltpu.VMEM(((ct + 1) * CH * bsz, POOL_W + CONV_W), F32),
               pltpu.VMEM(w_in.shape[1:], BF16), pltpu.VMEM((MIX, D), BF16),
               pltpu.VMEM((len(POOL_WINDOWS), POOL_GW, POOL_GW), BF16)]
    if final:
        args.append(final_g.reshape(1, D))
        specs.append(const(1, D))
        scratch.append(pltpu.VMEM((ct, bsz, CH * D), F32))
        out_spec = pl.BlockSpec((bsz, ct * CH, D), lambda i: (0, i, 0))
        out_shape = jax.ShapeDtypeStruct((bsz, nc * CH, D), F32)
    else:
        out_spec = pl.BlockSpec((ct, bsz, CH * D), lambda i: (i, 0, 0))
        out_shape = jax.ShapeDtypeStruct((nc, bsz, CH * D), F32)
    return pl.pallas_call(
        functools.partial(_odd_kernel, ct=ct, ntl=ntl, t_lat=ncl * CH, t_ctx=(nc_all - ncl) * CH,
                          final=final),
        grid=(nc // ct,),
        in_specs=specs,
        out_specs=out_spec,
        out_shape=out_shape,
        scratch_shapes=scratch,
        compiler_params=_cparams(("arbitrary",)),
    )(*args)


def _sincos_table(n_tok, dim):
    rows = n_tok // GRID_W
    rr, cc = np.meshgrid(np.arange(rows, dtype=np.float64), np.arange(GRID_W, dtype=np.float64),
                         indexing='ij')
    rr = rr.reshape(-1, 1)
    cc = cc.reshape(-1, 1)
    quarter = dim // 4
    omega = POS_BASE ** (-np.arange(quarter, dtype=np.float64) / quarter)
    tab = np.concatenate([np.sin(rr * omega), np.cos(rr * omega), np.sin(cc * omega), np.cos(cc * omega)],
                         axis=-1)
    return jnp.asarray(tab, dtype=F32)


def _block_diag(w):
    g, c, _ = w.shape
    eye = jnp.eye(g, dtype=w.dtype)
    return (eye[:, None, :, None] * w[:, :, None, :]).reshape(g * c, g * c)


def kernel(x, c, ctx, c_ctx, norm_g, ada_w, ada_b, even_w_in, even_w_out, s5_lam_re, s5_lam_im, s5_log_step, s5_b_re, s5_b_im, s5_c_re, s5_c_im, s5_d, s5_glu_w, s5_glu_b, fnet_w, odd_w_in, odd_w_out, pool_w, pool_scale, conv_w, final_g):
    bsz, n_tok, _ = x.shape
    depth = norm_g.shape[0]
    ncl = n_tok // CH
    ncc = ctx.shape[1] // CH

    cond = jnp.concatenate([c, jnp.broadcast_to(c_ctx[None], (16 - bsz, D))], axis=0)
    ada, mt, bend, cp, lam16, xs = _prologue(cond, ada_w, ada_b, s5_lam_re, s5_lam_im, s5_log_step, s5_b_re,
                                             s5_b_im, s5_c_re, s5_c_im, s5_d, x, _sincos_table(n_tok, D), ctx)

    need_ctx = [any(j % 2 == 0 for j in range(l + 1, depth)) for l in range(depth)]

    for l in range(depth):
        i = l // 2
        last = l == depth - 1
        if l % 2 == 0:
            wcs = _fnet_weights(_block_diag(fnet_w[i]))
            ut, xw, z = _even_in(xs, ncl, ada, l, norm_g[l], even_w_in, i, wcs)
            yt = _s5_mix(ut, ncl, mt, bend, cp, lam16, bsz, i)
            ybl = _fnet(xw, 0, ncl, bsz)
            ybc = _fnet(xw, ncl, ncc, bsz) if need_ctx[l] else None
            xs = _even_out(yt, ybl, ybc, z, xs, ncl, ada, l, s5_glu_w, s5_glu_b[i], even_w_out, i)
        else:
            xs = _odd_layer(xs, ncl, need_ctx[l], ada, l, norm_g[l], odd_w_in, odd_w_out, pool_w, i,
                            pool_scale[i], conv_w[i], final_g if last else None)
    if depth % 2 == 1:
        raise NotImplementedError("final norm and (B, T, D) order are produced by the last (odd) layer")
    return xs
```

```python
import functools
import math

import numpy as np
import jax
import jax.numpy as jnp
from jax import lax
from jax.experimental import pallas as pl
from jax.experimental.pallas import tpu as pltpu

D = 1024
MIX = 1024
S5_W = 768
FN_W = 256
S5_H = 16
S5_G = 48
S5_P = 64
FN_G = 4
FN_GW = 64
POOL_W = 512
CONV_W = 512
POOL_WINDOWS = (2, 4, 8, 16)
POOL_GW = 128
GRID_W = 64
EPS = 1e-6
POS_BASE = 10000.0
CH = 16
HALO = 8
LANE = 128
VMEM_LIMIT = 56 * 1024 * 1024

F32 = jnp.float32
BF16 = jnp.bfloat16
HI = lax.Precision.HIGHEST


def _cparams(sem):
    return pltpu.CompilerParams(dimension_semantics=sem, vmem_limit_bytes=VMEM_LIMIT)


def _silu(v):
    h = 0.5 * v
    return h + h * jnp.tanh(h)


def _gelu_tanh(v):
    c = math.sqrt(2.0 / math.pi)
    h = 0.5 * v
    return h + h * jnp.tanh(v * (c + (c * 0.044715) * (v * v)))


def _mod_norm(x, g, scale, shift):
    ms = jnp.mean(x * x, axis=-1, keepdims=True)
    y = x * lax.rsqrt(ms + EPS) * g
    return y * (1.0 + scale) + shift


def _split3(v):
    hi = v.astype(BF16)
    lo = (v - hi.astype(F32)).astype(BF16)
    return hi, lo


def _ada_item(c_ref, w_ref, b_ref, ada_ref):
    s_hi, s_lo = _split3(_silu(c_ref[...]))
    w_hi, w_lo = _split3(w_ref[...])
    ada_ref[...] = (jnp.dot(s_hi, w_hi, preferred_element_type=F32)
                    + jnp.dot(s_hi, w_lo, preferred_element_type=F32)
                    + jnp.dot(s_lo, w_hi, preferred_element_type=F32) + b_ref[...])


def _stream_item(x_ref, pos_ref, c_ref, o_ref, step, ntl):
    @pl.when(step < ntl)
    def _():
        o_ref[...] = pltpu.einshape("b(ci)d->cb(id)", x_ref[...] + pos_ref[...][None], i=CH)

    @pl.when(step >= ntl)
    def _():
        o_ref[...] = pltpu.einshape("b(ci)d->cb(id)", c_ref[...], i=CH)


def _prologue_kernel(c_ref, w_ref, b_ref, lr_ref, li_ref, ls_ref, btr_ref, bti_ref, cr_ref, ci_ref, d_ref,
                     x_ref, pos_ref, ctx_ref, ada_ref, mt_ref, be_ref, cp_ref, l16_ref, xs_ref,
                     ere_ref, eim_ref, *, gb, n_ada, n_tab, ntl):
    step = pl.program_id(0)

    @pl.when(step < n_ada)
    def _():
        _ada_item(c_ref, w_ref, b_ref, ada_ref)

    @pl.when(step < n_tab)
    def _():
        _tables_item(lr_ref, li_ref, ls_ref, btr_ref, bti_ref, cr_ref, ci_ref, d_ref,
                     mt_ref, be_ref, cp_ref, l16_ref, ere_ref, eim_ref, gb)

    _stream_item(x_ref, pos_ref, ctx_ref, xs_ref, step, ntl)


def _tables_item(lr_ref, li_ref, ls_ref, btr_ref, bti_ref, cr_ref, ci_ref, d_ref,
                 mt_ref, be_ref, cp_ref, l16_ref, ere_ref, eim_ref, gb):
    nst = 2 * S5_P
    kk = CH * S5_H
    step = jnp.exp(ls_ref[...])
    lr = lr_ref[...]
    li = li_ref[...]
    a = lr * step
    b = li * step

    def powers(expo):
        mag = jnp.exp(expo * a)
        return mag * jnp.cos(expo * b), mag * jnp.sin(expo * b)

    row = lax.broadcasted_iota(jnp.int32, (1, CH, nst), 1).astype(F32)
    fwd = lax.broadcasted_iota(jnp.int32, (1, CH, nst), 2) < S5_P
    one = jnp.ones((1, 1, nst), F32)

    l1re, l1im = powers(one)
    n_re = l1re - 1.0
    den = lr * lr + li * li
    co_re = (n_re * lr + l1im * li) / den
    co_im = (l1im * lr - n_re * li) / den
    btr = btr_ref[...]
    bti = bti_ref[...]
    bb_re = co_re * btr - co_im * bti
    bb_im = co_re * bti + co_im * btr

    pe_re, pe_im = powers(jnp.where(fwd, (CH - 1) - row, row))
    for l in range(CH):
        pr = pe_re[:, l:l + 1, :]
        pi = pe_im[:, l:l + 1, :]
        ere_ref[:, l * S5_H:(l + 1) * S5_H, :] = pr * bb_re - pi * bb_im
        eim_ref[:, l * S5_H:(l + 1) * S5_H, :] = pr * bb_im + pi * bb_re

    cr = cr_ref[...]
    ci = ci_ref[...]
    pc_re, pc_im = powers(jnp.where(fwd, row + 1.0, CH - row))
    for j in range(CH):
        pr = pc_re[:, j:j + 1, :]
        pi = pc_im[:, j:j + 1, :]
        w_re = cr * pr - ci * pi
        w_im = cr * pi + ci * pr
        cp_ref[:, j * S5_H:(j + 1) * S5_H, :] = jnp.concatenate([w_re, -w_im], axis=2).astype(BF16)

    l16re, l16im = powers(one * float(CH))
    l16_ref[...] = jnp.concatenate([l16re, l16im], axis=1)

    fwd2 = lax.broadcasted_iota(jnp.int32, (S5_H, nst), 1) < S5_P
    lane = lax.broadcasted_iota(jnp.int32, (S5_H, kk), 1)
    iblk = lane // S5_H
    hrow = lax.broadcasted_iota(jnp.int32, (S5_H, kk), 0)
    nt = (((1,), (1,)), ((), ()))
    for g in range(gb):
        ere = ere_ref[g]
        eim = eim_ref[g]
        e2 = jnp.concatenate([ere, eim], axis=1)
        be_ref[g] = e2.astype(BF16)
        e_hi, e_lo = _split3(e2)
        rhs = jnp.concatenate([e_hi, e_hi, e_lo], axis=1)
        crg = cr[g]
        cig = ci[g]

        c2 = jnp.concatenate([
            jnp.concatenate([jnp.where(fwd2, crg, 0.0), jnp.where(fwd2, -cig, 0.0)], axis=1),
            jnp.concatenate([jnp.where(fwd2, 0.0, crg), jnp.where(fwd2, 0.0, -cig)], axis=1)], axis=0)
        c_hi, c_lo = _split3(c2)
        lhs = jnp.concatenate([c_hi, c_lo, c_hi], axis=1)
        kfb = lax.dot_general(lhs, rhs, nt, preferred_element_type=F32)
        kf = kfb[0:S5_H]
        kb = kfb[S5_H:2 * S5_H]
        dg = d_ref[g]
        for j in range(CH):
            sf = (kk - (CH - 1 - j) * S5_H) % kk
            rf = pltpu.roll(kf, sf, 1) if sf else kf
            rb = pltpu.roll(kb, j * S5_H, 1) if j else kb
            blk = (jnp.where(iblk <= j, rf, 0.0) + jnp.where(iblk >= j, rb, 0.0)
                   + jnp.where(lane == j * S5_H + hrow, dg, 0.0))
            mt_ref[g, j * S5_H:(j + 1) * S5_H, :] = blk.astype(BF16)


def _prologue(cond, ada_w, ada_b, lam_re, lam_im, log_step, b_re, b_im, c_re, c_im, d_skip, x, pos, ctx):
    n = lam_re.shape[0] * S5_G
    gb = 8
    nst = 2 * S5_P
    kk = CH * S5_H

    def fb(v):
        return jnp.concatenate([v[:, 0], v[:, 1]], axis=-1).reshape(n, v.shape[3], nst)

    lr = fb(lam_re[:, :, :, None, :])
    li = fb(lam_im[:, :, :, None, :])
    ls = fb(jnp.broadcast_to(log_step[:, :, :, None, None], log_step.shape + (1, S5_P)))
    btr = fb(jnp.swapaxes(b_re, -1, -2))
    bti = fb(jnp.swapaxes(b_im, -1, -2))
    cr = fb(c_re)
    ci = fb(c_im)
    d = d_skip.reshape(n, S5_H, 1)

    depth = ada_w.shape[0]
    n_ada = depth * 3
    n_tab = n // gb
    bsz, t, _ = x.shape
    tc = ctx.shape[1]
    ct = 8
    tm = ct * CH
    ntl = t // tm
    n_str = (t + tc) // tm

    def spec(r, c):
        return pl.BlockSpec((gb, r, c), lambda s: (jnp.minimum(s, n_tab - 1), 0, 0))

    def ada_idx(s):
        s = jnp.minimum(s, n_ada - 1)
        return s // 3, s % 3

    def lat_tile(s):
        return jnp.minimum(s, ntl - 1)

    return pl.pallas_call(
        functools.partial(_prologue_kernel, gb=gb, n_ada=n_ada, n_tab=n_tab, ntl=ntl),
        grid=(max(n_ada, n_tab, n_str),),
        in_specs=[pl.BlockSpec((16, D), lambda s: (0, 0)),
                  pl.BlockSpec((None, D, D), lambda s: (ada_idx(s)[0], 0, ada_idx(s)[1])),
                  pl.BlockSpec((None, None, 1, D), lambda s: ada_idx(s) + (0, 0)),
                  spec(1, nst), spec(1, nst), spec(1, nst), spec(S5_H, nst), spec(S5_H, nst),
                  spec(S5_H, nst), spec(S5_H, nst), spec(S5_H, 1),
                  pl.BlockSpec((bsz, tm, D), lambda s: (0, lat_tile(s), 0)),
                  pl.BlockSpec((tm, D), lambda s: (lat_tile(s), 0)),
                  pl.BlockSpec((bsz, tm, D), lambda s: (0, jnp.clip(s - ntl, 0, tc // tm - 1), 0),
                               pipeline_mode=pl.Buffered(1))],
        out_specs=[pl.BlockSpec((None, None, 16, D), lambda s: ada_idx(s) + (0, 0)),
                   spec(kk, kk), spec(kk, 2 * nst), spec(kk, 2 * nst), spec(2, nst),
                   pl.BlockSpec((ct, bsz, CH * D), lambda s: (jnp.minimum(s, n_str - 1), 0, 0))],
        out_shape=[jax.ShapeDtypeStruct((depth, 3, 16, D), F32),
                   jax.ShapeDtypeStruct((n, kk, kk), BF16),
                   jax.ShapeDtypeStruct((n, kk, 2 * nst), BF16),
                   jax.ShapeDtypeStruct((n, kk, 2 * nst), BF16),
                   jax.ShapeDtypeStruct((n, 2, nst), F32),
                   jax.ShapeDtypeStruct(((t + tc) // CH, bsz, CH * D), F32)],
        scratch_shapes=[pltpu.VMEM((gb, kk, nst), F32)] * 2,
        compiler_params=_cparams(("arbitrary",)),
    )(cond, ada_w, ada_b.reshape(depth, 3, 1, D), lr, li, ls, btr, bti, cr, ci, d, x, pos, ctx)


def _even_in_kernel(x_ref, shift_ref, scale_ref, g_ref, w_ref, wcs_ref, ut_ref, xw_ref, z_ref,
                    wat_ref, wbz_ref, *scr, ncl):
    @pl.when(pl.program_id(0) == 0)
    def _():
        wat_ref[...] = w_ref[:, :S5_W].T.astype(BF16)
        wbz_ref[...] = w_ref[:, S5_W:].astype(BF16)

    x = x_ref[...]
    nc, nb, _ = x.shape
    rows = nc * nb
    g = g_ref[...]
    h = jnp.concatenate([_mod_norm(x[0:ncl], g, scale_ref[0:nb][None], shift_ref[0:nb][None]),
                         _mod_norm(x[ncl:nc], g, scale_ref[nb:nb + 1][None], shift_ref[nb:nb + 1][None])],
                        axis=0)
    hb = h.reshape(rows, D).astype(BF16)
    pt = lax.dot_general(wat_ref[...], hb, (((1,), (1,)), ((), ())), preferred_element_type=F32)
    ut_ref[...] = pt.astype(BF16).reshape(S5_G, S5_H, rows)
    p = jnp.dot(hb, wbz_ref[...], preferred_element_type=F32)
    z_ref[...] = _silu(p[:, FN_W:]).astype(BF16)
    xw = jnp.dot(p[:, :FN_W].astype(BF16), wcs_ref[...], preferred_element_type=F32)
    for q in range(4):
        scr[q][...] = xw[:, q * LANE:(q + 1) * LANE]
    for q in range(4):
        part, half = divmod(q, 2)
        for bi in range(nb):
            piece = scr[q][pl.ds(bi, nc, stride=nb), :]
            lo = bi * FN_W + half * LANE
            xw_ref[part, :, lo:lo + LANE] = piece.astype(BF16)


def _even_in(xs, ncl, ada, l, g, w_in, layer, wcs):
    nc, bsz, _ = xs.shape
    rows = nc * bsz
    return pl.pallas_call(
        functools.partial(_even_in_kernel, ncl=ncl),
        grid=(CH,),
        in_specs=[pl.BlockSpec((nc, bsz, D), lambda i: (0, 0, i)),
                  pl.BlockSpec((None, None) + ada.shape[2:], lambda i: (l, 0, 0, 0)),
                  pl.BlockSpec((None, None) + ada.shape[2:], lambda i: (l, 1, 0, 0)),
                  pl.BlockSpec((1, D), lambda i: (0, 0)),
                  pl.BlockSpec((None, D, w_in.shape[2]), lambda i: (layer, 0, 0),
                               pipeline_mode=pl.Buffered(1)),
                  pl.BlockSpec((FN_W, 2 * FN_W), lambda i: (0, 0))],
        out_specs=[pl.BlockSpec((S5_G, S5_H, rows), lambda i: (0, i, 0)),
                   pl.BlockSpec((2, None, nc, bsz * FN_W), lambda i: (0, i, 0, 0)),
                   pl.BlockSpec((None, rows, MIX), lambda i: (i, 0, 0))],
        out_shape=[jax.ShapeDtypeStruct((S5_G, CH * S5_H, rows), BF16),
                   jax.ShapeDtypeStruct((2, CH, nc, bsz * FN_W), BF16),
                   jax.ShapeDtypeStruct((CH, rows, MIX), BF16)],
        scratch_shapes=[pltpu.VMEM((S5_W, D), BF16), pltpu.VMEM((D, FN_W + MIX), BF16)]
        + [pltpu.VMEM((rows, LANE), F32)] * 4,
        compiler_params=_cparams(("arbitrary",)),
    )(xs, ada, ada, g.reshape(1, D), w_in, wcs)


def _s5_kernel(ut_ref, mt_ref, be_ref, cp_ref, l16_ref, yt_ref,
               sre_ref, sim_ref, are_ref, aim_ref, bre_ref, bim_ref, *, bsz, ncl, ncc, gs):
    nl = bsz * ncl
    nst = 2 * S5_P
    for g in range(gs):
        st = lax.dot_general(ut_ref[g], be_ref[g], (((0,), (0,)), ((), ())), preferred_element_type=F32)
        sre_ref[g] = st[:, :nst]
        sim_ref[g] = st[:, nst:]

    lam = [l16_ref[g] for g in range(gs)]
    is_fwd = lax.broadcasted_iota(jnp.int32, (bsz, nst), 1) < S5_P

    def make_step(base, nchunk):
        def step(c, carry):
            rf = pl.ds(pl.multiple_of(base + c * bsz, bsz), bsz)
            rb = pl.ds(pl.multiple_of(base + (nchunk - 1 - c) * bsz, bsz), bsz)
            out = []
            for g in range(gs):
                sre, sim = carry[2 * g], carry[2 * g + 1]
                lre = lam[g][0:1, :]
                lim = lam[g][1:2, :]
                are_ref[g, rf, :] = sre
                aim_ref[g, rf, :] = sim
                bre_ref[g, rb, :] = sre
                bim_ref[g, rb, :] = sim
                in_re = jnp.where(is_fwd, sre_ref[g, rf, :], sre_ref[g, rb, :])
                in_im = jnp.where(is_fwd, sim_ref[g, rf, :], sim_ref[g, rb, :])
                out.append(lre * sre - lim * sim + in_re)
                out.append(lre * sim + lim * sre + in_im)
            return tuple(out)
        return step

    zero = jnp.zeros((bsz, nst), F32)
    carry = lax.fori_loop(0, ncc, make_step(nl, ncc), (zero,) * (2 * gs))
    lax.fori_loop(0, ncl, make_step(0, ncl), carry)

    nt = (((1,), (1,)), ((), ()))
    fwd_rows = lax.broadcasted_iota(jnp.int32, (bsz * (ncl + ncc), nst), 1) < S5_P
    for g in range(gs):
        s0 = jnp.concatenate([jnp.where(fwd_rows, are_ref[g], bre_ref[g]),
                              jnp.where(fwd_rows, aim_ref[g], bim_ref[g])], axis=1).astype(BF16)
        yt_ref[g] = (jnp.dot(mt_ref[g], ut_ref[g], preferred_element_type=F32)
                     + lax.dot_general(cp_ref[g], s0, nt, preferred_element_type=F32)).astype(BF16)


def _s5_mix(ut, ncl, mt, bend, cp, lam16, bsz, layer):
    rows = ut.shape[2]
    kk = CH * S5_H
    nst = 2 * S5_P
    gs = 6
    off = layer * (S5_G // gs)

    def gspec(r, c):
        return pl.BlockSpec((gs, r, c), lambda g: (g, 0, 0))

    def tspec(r, c):
        return pl.BlockSpec((gs, r, c), lambda g: (g + off, 0, 0))

    return pl.pallas_call(
        functools.partial(_s5_kernel, bsz=bsz, ncl=ncl, ncc=rows // bsz - ncl, gs=gs),
        grid=(S5_G // gs,),
        in_specs=[gspec(kk, rows), tspec(kk, kk), tspec(kk, 2 * nst), tspec(kk, 2 * nst), tspec(2, nst)],
        out_specs=gspec(kk, rows),
        out_shape=jax.ShapeDtypeStruct((S5_G, kk, rows), BF16),
        scratch_shapes=[pltpu.VMEM((gs, rows, nst), F32)] * 6,
        compiler_params=_cparams(("arbitrary",)),
    )(ut, mt, bend, cp, lam16)


def _fnet_weights_kernel(ccs_ref, fw_ref, o_ref):
    fw = fw_ref[...]
    ccs = ccs_ref[...]
    wc = jnp.dot(ccs[:, :FN_W], fw, precision=HI, preferred_element_type=F32)
    ws = jnp.dot(ccs[:, FN_W:], fw, precision=HI, preferred_element_type=F32)
    o_ref[...] = jnp.concatenate([wc, ws], axis=1).astype(BF16)


def _fnet_weights(fw_bd):
    return pl.pallas_call(
        _fnet_weights_kernel,
        out_shape=jax.ShapeDtypeStruct((FN_W, 2 * FN_W), BF16),
    )(_dft_channel_matrix(), fw_bd)


def _fnet_kernel(tab_ref, v_ref, jr_ref, o_ref, *, ipt):
    r = pl.program_id(1)
    tab = tab_ref[...].astype(BF16)
    v = v_ref[...]
    nc, n = v.shape[2], v.shape[3]
    t = CH * nc
    a = jnp.dot(tab[:, :t], v[0].reshape(t, n), preferred_element_type=F32)
    b = jnp.dot(tab[:, t:], v[1].reshape(t, n), preferred_element_type=F32)
    o_ref[pl.ds(r * ipt, ipt)] = (a - b).reshape(ipt, nc, n).astype(BF16)
    m = (a + b).astype(BF16)
    for k in range(ipt):
        i = r * ipt + k

        @pl.when(jnp.logical_and(i >= 1, i <= CH // 2 - 1))
        def _():
            o_ref[CH - i] = jnp.dot(jr_ref[...], m[k * nc:(k + 1) * nc],
                                    preferred_element_type=F32).astype(BF16)


def _dft_half_table(nc):
    t = nc * CH
    cols = (np.arange(nc)[None, :] * CH + np.arange(CH)[:, None]).reshape(-1)
    rows = (np.arange(nc)[None, :] * CH + np.arange(CH // 2 + 1)[:, None]).reshape(-1)
    prod = (rows[:, None].astype(np.int64) * cols[None, :].astype(np.int64)) % t
    ang = prod.astype(np.float64) * (2.0 * np.pi / t)
    scale = 1.0 / math.sqrt(t * FN_GW)
    return jnp.asarray(np.concatenate([np.cos(ang), np.sin(ang)], axis=1) * scale, dtype=F32)


def _dft_channel_matrix():
    c = np.arange(FN_GW)
    ang = (c[:, None] * c[None, :] % FN_GW).astype(np.float64) * (2.0 * np.pi / FN_GW)
    eye = np.eye(FN_G)
    return jnp.asarray(np.concatenate([np.kron(eye, np.cos(ang)), np.kron(eye, np.sin(ang))], axis=1),
                       dtype=F32)


def _fnet(xw, c0, nc, bsz):
    t = nc * CH
    nhb = CH // 2 + 1
    ipt = 3
    bpb = 4
    jr = jnp.asarray(np.eye(nc)[::-1], dtype=BF16)
    return pl.pallas_call(
        functools.partial(_fnet_kernel, ipt=ipt),
        grid=(bsz // bpb, nhb // ipt),
        in_specs=[pl.BlockSpec((ipt * nc, 2 * t), lambda b, r: (r, 0)),
                  pl.BlockSpec((2, CH, nc, bpb * FN_W), lambda b, r: (0, 0, c0 // nc, b)),
                  pl.BlockSpec((nc, nc), lambda b, r: (0, 0))],
        out_specs=pl.BlockSpec((CH, nc, bpb * FN_W), lambda b, r: (0, 0, b)),
        out_shape=jax.ShapeDtypeStruct((CH, nc, bsz * FN_W), BF16),
        compiler_params=_cparams(("arbitrary", "arbitrary")),
    )(_dft_half_table(nc), xw, jr)


def _even_out_kernel(*refs, ncl, with_ctx):
    it = iter(refs)
    yt_ref, ybl_ref = next(it), next(it)
    ybc_ref = next(it) if with_ctx else None
    z_ref, x_ref, gate_ref, gwf_ref, gb_ref, wof_ref, o_ref = (next(it), next(it), next(it), next(it),
                                                               next(it), next(it), next(it))
    scr = [next(it), next(it)]
    gw_ref, wo_ref = next(it), next(it)

    @pl.when(pl.program_id(0) == 0)
    def _():
        gw_ref[...] = (0.5 * gwf_ref[...]).astype(BF16)
        wo_ref[...] = wof_ref[...].astype(BF16)

    yt = yt_ref[...]
    rows = yt.shape[2]
    ya = _gelu_tanh(yt.astype(F32).reshape(S5_W, rows).T)
    half_g = jnp.dot(ya.astype(BF16), gw_ref[...], preferred_element_type=F32) + 0.5 * gb_ref[...]
    hy = 0.5 * ya
    ya = hy + hy * jnp.tanh(half_g)
    x = x_ref[...]
    nc, nb, _ = x.shape
    for half in range(2):
        for bi in range(nb):
            lo = bi * FN_W + half * LANE
            scr[half][pl.ds(bi, ncl, stride=nb), :] = ybl_ref[:, lo:lo + LANE].astype(F32)
            if with_ctx:
                scr[half][pl.ds(ncl * nb + bi, nc - ncl, stride=nb), :] = (
                    ybc_ref[:, lo:lo + LANE].astype(F32))
    yb = jnp.concatenate([scr[0][...], scr[1][...]], axis=1)
    sz = z_ref[...]
    ma = (ya * sz[:, :S5_W]).astype(BF16)
    mb = (yb * sz[:, S5_W:]).astype(BF16)
    out = (jnp.dot(ma, wo_ref[0:S5_W, :], preferred_element_type=F32)
           + jnp.dot(mb, wo_ref[S5_W:MIX, :], preferred_element_type=F32)).reshape(x.shape)
    o_ref[0:ncl] = x[0:ncl] + gate_ref[0:nb][None] * out[0:ncl]
    if with_ctx:
        o_ref[ncl:nc] = x[ncl:nc] + gate_ref[nb:nb + 1][None] * out[ncl:nc]


def _even_out(yt, ybl, ybc, z, xs, ncl, ada, l, glu_w, glu_b, w_out, layer):
    bsz = xs.shape[1]
    with_ctx = ybc is not None
    nc = xs.shape[0] if with_ctx else ncl
    rows = nc * bsz
    args = [yt, ybl]
    specs = [pl.BlockSpec((S5_G, S5_H, rows), lambda j: (0, j, 0)),
             pl.BlockSpec((None, ncl, bsz * FN_W), lambda j: (j, 0, 0))]
    if with_ctx:
        args.append(ybc)
        specs.append(pl.BlockSpec((None, nc - ncl, bsz * FN_W), lambda j: (j, 0, 0)))
    args += [z, xs, ada, glu_w, glu_b.reshape(1, S5_W), w_out]
    specs += [pl.BlockSpec((None, rows, MIX), lambda j: (j, 0, 0)),
              pl.BlockSpec((nc, bsz, D), lambda j: (0, 0, j)),
              pl.BlockSpec((None, None) + ada.shape[2:], lambda j: (l, 2, 0, 0)),
              pl.BlockSpec((None, S5_W, S5_W), lambda j: (layer, 0, 0), pipeline_mode=pl.Buffered(1)),
              pl.BlockSpec((1, S5_W), lambda j: (0, 0)),
              pl.BlockSpec((None, MIX, D), lambda j: (layer, 0, 0), pipeline_mode=pl.Buffered(1))]
    return pl.pallas_call(
        functools.partial(_even_out_kernel, ncl=ncl, with_ctx=with_ctx),
        grid=(CH,),
        in_specs=specs,
        out_specs=pl.BlockSpec((nc, bsz, D), lambda j: (0, 0, j)),
        out_shape=jax.ShapeDtypeStruct((nc, bsz, CH * D), F32),
        scratch_shapes=[pltpu.VMEM((rows, LANE), F32)] * 2 + [pltpu.VMEM((S5_W, S5_W), BF16),
                                                             pltpu.VMEM((MIX, D), BF16)],
        compiler_params=_cparams(("arbitrary",)),
    )(*args)


def _odd_kernel(*refs, ct, ntl, t_lat, t_ctx, final):
    it = iter(refs)
    xm_ref, xp_ref, xn_ref = next(it), next(it), next(it)
    shift_ref, scale_ref, gate_ref, g_ref = next(it), next(it), next(it), next(it)
    wif_ref, wof_ref, pwf_ref, ps_ref, cw_ref = next(it), next(it), next(it), next(it), next(it)
    fg_ref = next(it) if final else None
    o_ref = next(it)
    h_ref = next(it)
    pe_ref = next(it)
    wi_ref, wo_ref, pw_ref = next(it), next(it), next(it)
    fin_ref = next(it) if final else None

    ti = pl.program_id(0)
    is_ctx = ti >= ntl
    t0 = jnp.where(is_ctx, ti - ntl, ti) * (ct * CH)
    t_total = jnp.where(is_ctx, t_ctx, t_lat)

    @pl.when(ti == 0)
    def _():
        wi_ref[...] = wif_ref[...].astype(BF16)
        wo_ref[...] = wof_ref[...].astype(BF16)
        pw_ref[...] = pwf_ref[...].astype(BF16)

    n1 = POOL_W + 2 * CONV_W
    nb = xm_ref.shape[1]
    tm = ct * CH
    ne = tm + 2 * HALO

    def rows_of(ref):
        return jnp.where(is_ctx, ref[nb:nb + 1], ref[0:nb])

    scale = rows_of(scale_ref)
    shift = rows_of(shift_ref)

    def hn(xv):
        return _mod_norm(xv, g_ref[...], scale, shift)

    for i in range(CH):
        hi = hn(xm_ref[:, :, i * D:(i + 1) * D])
        if i < HALO:
            h_ref[0:ct, i + HALO] = hi
        else:
            h_ref[1:ct + 1, i - HALO] = hi
    for i in range(HALO):
        h_ref[0, i] = hn(xp_ref[:, :, (i + HALO) * D:(i + HALO + 1) * D])[0]
        h_ref[ct, i + HALO] = hn(xn_ref[:, :, i * D:(i + 1) * D])[0]

    he = h_ref[...].reshape(ne * nb, D).astype(BF16)
    pe = jnp.dot(he, wi_ref[:, 0:n1], preferred_element_type=F32)
    te = t0 - HALO + lax.broadcasted_iota(jnp.int32, (ne * nb, 1), 0) // nb
    valid = jnp.logical_and(te >= 0, te < t_total)
    pe_ref[:, 0:POOL_W] = jnp.where(valid, pe[:, :POOL_W], 0.0)
    pe_ref[:, POOL_W:POOL_W + CONV_W] = jnp.where(
        valid, pe[:, POOL_W:POOL_W + CONV_W] * pe[:, POOL_W + CONV_W:], 0.0)

    m0 = HALO * nb
    mr = tm * nb
    p2 = jnp.dot(he[m0:m0 + mr], wi_ref[:, n1:], preferred_element_type=F32)
    b_gate = p2[:, :CONV_W]
    sz = _silu(p2[:, CONV_W:])

    tpos = t0 + lax.broadcasted_iota(jnp.int32, (mr, 1), 0) // nb
    pooled = []
    for gi, w in enumerate(POOL_WINDOWS):
        c0 = gi * POOL_GW
        s = pe_ref[:, c0:c0 + POOL_GW]
        n = ne
        width = 1
        while width < w:
            s = s[0:(n - width) * nb] + s[width * nb:n * nb]
            n -= width
            width *= 2
        start = (HALO - w // 2) * nb
        total = s[start:start + mr]
        hi = jnp.minimum(tpos + w // 2, t_total)
        lo = jnp.maximum(tpos - w // 2, 0)
        cnt = (hi - lo).astype(F32)
        centre = pe_ref[m0:m0 + mr, c0:c0 + POOL_GW]
        pg = total / cnt - centre
        pooled.append(jnp.dot(pg.astype(BF16), pw_ref[gi], preferred_element_type=F32))
    y_c = jnp.concatenate(pooled, axis=1) * ps_ref[...]

    cwt = cw_ref[...]
    vm = pe_ref[m0 - nb:m0 - nb + mr, POOL_W:POOL_W + CONV_W]
    v0 = pe_ref[m0:m0 + mr, POOL_W:POOL_W + CONV_W]
    vp = pe_ref[m0 + nb:m0 + nb + mr, POOL_W:POOL_W + CONV_W]
    y_d = b_gate * (vm * cwt[0:1, :] + v0 * cwt[1:2, :] + vp * cwt[2:3, :])

    y = (jnp.concatenate([y_c, y_d], axis=1) * sz).astype(BF16)
    out = jnp.dot(y, wo_ref[...], preferred_element_type=F32)
    go = (rows_of(gate_ref) * out.reshape(tm, nb, D)).reshape(ct, CH, nb, D)
    for i in range(CH):
        xo = xm_ref[:, :, i * D:(i + 1) * D] + go[:, i]
        if not final:
            o_ref[:, :, i * D:(i + 1) * D] = xo
            continue
        ms = jnp.mean(xo * xo, axis=-1, keepdims=True)
        fin_ref[:, :, i * D:(i + 1) * D] = xo * lax.rsqrt(ms + EPS) * fg_ref[...]
    if final:
        o_ref[...] = pltpu.einshape("cb(id)->b(ci)d", fin_ref[...], i=CH)


def _odd_layer(xs, ncl, with_ctx, ada, l, g, w_in, w_out, pool_w, layer, pool_scale, conv_w, final_g):
    nc_all, bsz, _ = xs.shape
    ct = 8
    ntl = ncl // ct
    nc = nc_all if with_ctx else ncl
    final = final_g is not None
    args = [xs, xs, xs, ada, ada, ada, g.reshape(1, D), w_in, w_out, pool_w,
            pool_scale.reshape(1, POOL_W), conv_w]

    def vec(j):
        return pl.BlockSpec((None, None) + ada.shape[2:], lambda i: (l, j, 0, 0))

    def const(*shape):
        return pl.BlockSpec(shape, lambda i: (0,) * len(shape), pipeline_mode=pl.Buffered(1))

    def stacked(*shape):
        return pl.BlockSpec((None,) + shape, lambda i: (layer,) + (0,) * len(shape),
                            pipeline_mode=pl.Buffered(1))

    specs = [pl.BlockSpec((ct, bsz, CH * D), lambda i: (i, 0, 0)),
             pl.BlockSpec((1, bsz, CH * D), lambda i: (jnp.maximum(i * ct - 1, 0), 0, 0)),
             pl.BlockSpec((1, bsz, CH * D), lambda i: (jnp.minimum((i + 1) * ct, nc_all - 1), 0, 0)),
             vec(0), vec(1), vec(2),
             const(1, D), stacked(*w_in.shape[1:]), stacked(MIX, D),
             stacked(len(POOL_WINDOWS), POOL_GW, POOL_GW), const(1, POOL_W), const(3, CONV_W)]
    scratch = [pltpu.VMEM((ct + 1, CH, bsz, D), F32),
               pltpu.VMEM(((ct + 1) * CH * bsz, POOL_W + CONV_W), F32),
               pltpu.VMEM(w_in.shape[1:], BF16), pltpu.VMEM((MIX, D), BF16),
               pltpu.VMEM((len(POOL_WINDOWS), POOL_GW, POOL_GW), BF16)]
    if final:
        args.append(final_g.reshape(1, D))
        specs.append(const(1, D))
        scratch.append(pltpu.VMEM((ct, bsz, CH * D), F32))
        out_spec = pl.BlockSpec((bsz, ct * CH, D), lambda i: (0, i, 0))
        out_shape = jax.ShapeDtypeStruct((bsz, nc * CH, D), F32)
    else:
        out_spec = pl.BlockSpec((ct, bsz, CH * D), lambda i: (i, 0, 0))
        out_shape = jax.ShapeDtypeStruct((nc, bsz, CH * D), F32)
    return pl.pallas_call(
        functools.partial(_odd_kernel, ct=ct, ntl=ntl, t_lat=ncl * CH, t_ctx=(nc_all - ncl) * CH,
                          final=final),
        grid=(nc // ct,),
        in_specs=specs,
        out_specs=out_spec,
        out_shape=out_shape,
        scratch_shapes=scratch,
        compiler_params=_cparams(("arbitrary",)),
    )(*args)


def _sincos_table(n_tok, dim):
    rows = n_tok // GRID_W
    rr, cc = np.meshgrid(np.arange(rows, dtype=np.float64), np.arange(GRID_W, dtype=np.float64),
                         indexing='ij')
    rr = rr.reshape(-1, 1)
    cc = cc.reshape(-1, 1)
    quarter = dim // 4
    omega = POS_BASE ** (-np.arange(quarter, dtype=np.float64) / quarter)
    tab = np.concatenate([np.sin(rr * omega), np.cos(rr * omega), np.sin(cc * omega), np.cos(cc * omega)],
                         axis=-1)
    return jnp.asarray(tab, dtype=F32)


def _block_diag(w):
    g, c, _ = w.shape
    eye = jnp.eye(g, dtype=w.dtype)
    return (eye[:, None, :, None] * w[:, :, None, :]).reshape(g * c, g * c)


def kernel(x, c, ctx, c_ctx, norm_g, ada_w, ada_b, even_w_in, even_w_out, s5_lam_re, s5_lam_im, s5_log_step, s5_b_re, s5_b_im, s5_c_re, s5_c_im, s5_d, s5_glu_w, s5_glu_b, fnet_w, odd_w_in, odd_w_out, pool_w, pool_scale, conv_w, final_g):
    bsz, n_tok, _ = x.shape
    depth = norm_g.shape[0]
    ncl = n_tok // CH
    ncc = ctx.shape[1] // CH

    cond = jnp.concatenate([c, jnp.broadcast_to(c_ctx[None], (16 - bsz, D))], axis=0)
    ada, mt, bend, cp, lam16, xs = _prologue(cond, ada_w, ada_b, s5_lam_re, s5_lam_im, s5_log_step, s5_b_re,
                                             s5_b_im, s5_c_re, s5_c_im, s5_d, x, _sincos_table(n_tok, D), ctx)

    need_ctx = [any(j % 2 == 0 for j in range(l + 1, depth)) for l in range(depth)]

    for l in range(depth):
        i = l // 2
        last = l == depth - 1
        if l % 2 == 0:
            wcs = _fnet_weights(_block_diag(fnet_w[i]))
            ut, xw, z = _even_in(xs, ncl, ada, l, norm_g[l], even_w_in, i, wcs)
            yt = _s5_mix(ut, ncl, mt, bend, cp, lam16, bsz, i)
            ybl = _fnet(xw, 0, ncl, bsz)
            ybc = _fnet(xw, ncl, ncc, bsz) if need_ctx[l] else None
            xs = _even_out(yt, ybl, ybc, z, xs, ncl, ada, l, s5_glu_w, s5_glu_b[i], even_w_out, i)
        else:
            xs = _odd_layer(xs, ncl, need_ctx[l], ada, l, norm_g[l], odd_w_in, odd_w_out, pool_w, i,
                            pool_scale[i], conv_w[i], final_g if last else None)
    if depth % 2 == 1:
        raise NotImplementedError("final norm and (B, T, D) order are produced by the last (odd) layer")
    return xs
```

```python
import functools
import math

import numpy as np
import jax
import jax.numpy as jnp
from jax import lax
from jax.experimental import pallas as pl
from jax.experimental.pallas import tpu as pltpu

D = 1024
MIX = 1024
S5_W = 768
FN_W = 256
S5_H = 16
S5_G = 48
S5_P = 64
FN_G = 4
FN_GW = 64
POOL_W = 512
CONV_W = 512
POOL_WINDOWS = (2, 4, 8, 16)
POOL_GW = 128
GRID_W = 64
EPS = 1e-6
POS_BASE = 10000.0
CH = 16
HALO = 8
LANE = 128
VMEM_LIMIT = 56 * 1024 * 1024

F32 = jnp.float32
BF16 = jnp.bfloat16
HI = lax.Precision.HIGHEST


def _cparams(sem):
    return pltpu.CompilerParams(dimension_semantics=sem, vmem_limit_bytes=VMEM_LIMIT)


def _silu(v):
    h = 0.5 * v
    return h + h * jnp.tanh(h)


def _gelu_tanh(v):
    c = math.sqrt(2.0 / math.pi)
    h = 0.5 * v
    return h + h * jnp.tanh(v * (c + (c * 0.044715) * (v * v)))


def _mod_norm(x, gain, shift):
    ms = jnp.mean(x * x, axis=-1, keepdims=True)
    return x * lax.rsqrt(ms + EPS) * gain + shift


def _split3(v):
    hi = v.astype(BF16)
    lo = (v - hi.astype(F32)).astype(BF16)
    return hi, lo


def _ada_item(c_ref, w_ref, b_ref, ada_ref):
    s_hi, s_lo = _split3(_silu(c_ref[...]))
    w_hi, w_lo = _split3(w_ref[...])
    ada_ref[...] = (jnp.dot(s_hi, w_hi, preferred_element_type=F32)
                    + jnp.dot(s_hi, w_lo, preferred_element_type=F32)
                    + jnp.dot(s_lo, w_hi, preferred_element_type=F32) + b_ref[...])


def _stream_item(x_ref, pos_ref, c_ref, o_ref, step, ntl):
    @pl.when(step < ntl)
    def _():
        o_ref[...] = pltpu.einshape("b(ci)d->cb(id)", x_ref[...] + pos_ref[...][None], i=CH)

    @pl.when(step >= ntl)
    def _():
        o_ref[...] = pltpu.einshape("b(ci)d->cb(id)", c_ref[...], i=CH)


def _prologue_kernel(c_ref, w_ref, b_ref, lr_ref, li_ref, ls_ref, btr_ref, bti_ref, cr_ref, ci_ref, d_ref,
                     x_ref, pos_ref, ctx_ref, ada_ref, mt_ref, be_ref, cp_ref, l16_ref, xs_ref,
                     ere_ref, eim_ref, *, gb, n_ada, n_tab, ntl):
    step = pl.program_id(0)

    @pl.when(step < n_ada)
    def _():
        _ada_item(c_ref, w_ref, b_ref, ada_ref)

    @pl.when(step < n_tab)
    def _():
        _tables_item(lr_ref, li_ref, ls_ref, btr_ref, bti_ref, cr_ref, ci_ref, d_ref,
                     mt_ref, be_ref, cp_ref, l16_ref, ere_ref, eim_ref, gb)

    _stream_item(x_ref, pos_ref, ctx_ref, xs_ref, step, ntl)


def _tables_item(lr_ref, li_ref, ls_ref, btr_ref, bti_ref, cr_ref, ci_ref, d_ref,
                 mt_ref, be_ref, cp_ref, l16_ref, ere_ref, eim_ref, gb):
    nst = 2 * S5_P
    kk = CH * S5_H
    step = jnp.exp(ls_ref[...])
    lr = lr_ref[...]
    li = li_ref[...]
    a = lr * step
    b = li * step

    def powers(expo):
        mag = jnp.exp(expo * a)
        return mag * jnp.cos(expo * b), mag * jnp.sin(expo * b)

    row = lax.broadcasted_iota(jnp.int32, (1, CH, nst), 1).astype(F32)
    fwd = lax.broadcasted_iota(jnp.int32, (1, CH, nst), 2) < S5_P
    one = jnp.ones((1, 1, nst), F32)

    l1re, l1im = powers(one)
    n_re = l1re - 1.0
    den = lr * lr + li * li
    co_re = (n_re * lr + l1im * li) / den
    co_im = (l1im * lr - n_re * li) / den
    btr = btr_ref[...]
    bti = bti_ref[...]
    bb_re = co_re * btr - co_im * bti
    bb_im = co_re * bti + co_im * btr

    pe_re, pe_im = powers(jnp.where(fwd, (CH - 1) - row, row))
    for l in range(CH):
        pr = pe_re[:, l:l + 1, :]
        pi = pe_im[:, l:l + 1, :]
        ere_ref[:, l * S5_H:(l + 1) * S5_H, :] = pr * bb_re - pi * bb_im
        eim_ref[:, l * S5_H:(l + 1) * S5_H, :] = pr * bb_im + pi * bb_re

    cr = cr_ref[...]
    ci = ci_ref[...]
    pc_re, pc_im = powers(jnp.where(fwd, row + 1.0, CH - row))
    for j in range(CH):
        pr = pc_re[:, j:j + 1, :]
        pi = pc_im[:, j:j + 1, :]
        w_re = cr * pr - ci * pi
        w_im = cr * pi + ci * pr
        cp_ref[:, j * S5_H:(j + 1) * S5_H, :] = jnp.concatenate([w_re, -w_im], axis=2).astype(BF16)

    l16re, l16im = powers(one * float(CH))
    l16_ref[...] = jnp.concatenate([l16re, l16im], axis=1)

    fwd2 = lax.broadcasted_iota(jnp.int32, (S5_H, nst), 1) < S5_P
    lane = lax.broadcasted_iota(jnp.int32, (S5_H, kk), 1)
    iblk = lane // S5_H
    hrow = lax.broadcasted_iota(jnp.int32, (S5_H, kk), 0)
    nt = (((1,), (1,)), ((), ()))
    for g in range(gb):
        ere = ere_ref[g]
        eim = eim_ref[g]
        e2 = jnp.concatenate([ere, eim], axis=1)
        be_ref[g] = e2.astype(BF16)
        e_hi, e_lo = _split3(e2)
        rhs = jnp.concatenate([e_hi, e_hi, e_lo], axis=1)
        crg = cr[g]
        cig = ci[g]

        c2 = jnp.concatenate([
            jnp.concatenate([jnp.where(fwd2, crg, 0.0), jnp.where(fwd2, -cig, 0.0)], axis=1),
            jnp.concatenate([jnp.where(fwd2, 0.0, crg), jnp.where(fwd2, 0.0, -cig)], axis=1)], axis=0)
        c_hi, c_lo = _split3(c2)
        lhs = jnp.concatenate([c_hi, c_lo, c_hi], axis=1)
        kfb = lax.dot_general(lhs, rhs, nt, preferred_element_type=F32)
        kf = kfb[0:S5_H]
        kb = kfb[S5_H:2 * S5_H]
        dg = d_ref[g]
        for j in range(CH):
            sf = (kk - (CH - 1 - j) * S5_H) % kk
            rf = pltpu.roll(kf, sf, 1) if sf else kf
            rb = pltpu.roll(kb, j * S5_H, 1) if j else kb
            blk = (jnp.where(iblk <= j, rf, 0.0) + jnp.where(iblk >= j, rb, 0.0)
                   + jnp.where(lane == j * S5_H + hrow, dg, 0.0))
            mt_ref[g, j * S5_H:(j + 1) * S5_H, :] = blk.astype(BF16)


def _prologue(cond, ada_w, ada_b, lam_re, lam_im, log_step, b_re, b_im, c_re, c_im, d_skip, x, pos, ctx):
    n = lam_re.shape[0] * S5_G
    gb = 8
    nst = 2 * S5_P
    kk = CH * S5_H

    def fb(v):
        return jnp.concatenate([v[:, 0], v[:, 1]], axis=-1).reshape(n, v.shape[3], nst)

    lr = fb(lam_re[:, :, :, None, :])
    li = fb(lam_im[:, :, :, None, :])
    ls = fb(jnp.broadcast_to(log_step[:, :, :, None, None], log_step.shape + (1, S5_P)))
    btr = fb(jnp.swapaxes(b_re, -1, -2))
    bti = fb(jnp.swapaxes(b_im, -1, -2))
    cr = fb(c_re)
    ci = fb(c_im)
    d = d_skip.reshape(n, S5_H, 1)

    depth = ada_w.shape[0]
    n_ada = depth * 3
    n_tab = n // gb
    bsz, t, _ = x.shape
    tc = ctx.shape[1]
    ct = 8
    tm = ct * CH
    ntl = t // tm
    n_str = (t + tc) // tm

    def spec(r, c):
        return pl.BlockSpec((gb, r, c), lambda s: (jnp.minimum(s, n_tab - 1), 0, 0))

    def ada_idx(s):
        s = jnp.minimum(s, n_ada - 1)
        return s // 3, s % 3

    def lat_tile(s):
        return jnp.minimum(s, ntl - 1)

    return pl.pallas_call(
        functools.partial(_prologue_kernel, gb=gb, n_ada=n_ada, n_tab=n_tab, ntl=ntl),
        grid=(max(n_ada, n_tab, n_str),),
        in_specs=[pl.BlockSpec((16, D), lambda s: (0, 0)),
                  pl.BlockSpec((None, D, D), lambda s: (ada_idx(s)[0], 0, ada_idx(s)[1])),
                  pl.BlockSpec((None, None, 1, D), lambda s: ada_idx(s) + (0, 0)),
                  spec(1, nst), spec(1, nst), spec(1, nst), spec(S5_H, nst), spec(S5_H, nst),
                  spec(S5_H, nst), spec(S5_H, nst), spec(S5_H, 1),
                  pl.BlockSpec((bsz, tm, D), lambda s: (0, lat_tile(s), 0)),
                  pl.BlockSpec((tm, D), lambda s: (lat_tile(s), 0)),
                  pl.BlockSpec((bsz, tm, D), lambda s: (0, jnp.clip(s - ntl, 0, tc // tm - 1), 0),
                               pipeline_mode=pl.Buffered(1))],
        out_specs=[pl.BlockSpec((None, None, 16, D), lambda s: ada_idx(s) + (0, 0)),
                   spec(kk, kk), spec(kk, 2 * nst), spec(kk, 2 * nst), spec(2, nst),
                   pl.BlockSpec((ct, bsz, CH * D), lambda s: (jnp.minimum(s, n_str - 1), 0, 0))],
        out_shape=[jax.ShapeDtypeStruct((depth, 3, 16, D), F32),
                   jax.ShapeDtypeStruct((n, kk, kk), BF16),
                   jax.ShapeDtypeStruct((n, kk, 2 * nst), BF16),
                   jax.ShapeDtypeStruct((n, kk, 2 * nst), BF16),
                   jax.ShapeDtypeStruct((n, 2, nst), F32),
                   jax.ShapeDtypeStruct(((t + tc) // CH, bsz, CH * D), F32)],
        scratch_shapes=[pltpu.VMEM((gb, kk, nst), F32)] * 2,
        compiler_params=_cparams(("arbitrary",)),
    )(cond, ada_w, ada_b.reshape(depth, 3, 1, D), lr, li, ls, btr, bti, cr, ci, d, x, pos, ctx)


def _even_in_kernel(x_ref, shift_ref, scale_ref, g_ref, w_ref, wcs_ref, ut_ref, xw_ref, z_ref,
                    wat_ref, wbz_ref, *scr, ncl):
    @pl.when(pl.program_id(0) == 0)
    def _():
        wat_ref[...] = w_ref[:, :S5_W].T.astype(BF16)
        wbz_ref[...] = w_ref[:, S5_W:].astype(BF16)

    x = x_ref[...]
    nc, nb, _ = x.shape
    rows = nc * nb
    gain = g_ref[...] * (1.0 + scale_ref[...])
    h = jnp.concatenate([_mod_norm(x[0:ncl], gain[0:nb][None], shift_ref[0:nb][None]),
                         _mod_norm(x[ncl:nc], gain[nb:nb + 1][None], shift_ref[nb:nb + 1][None])], axis=0)
    hb = h.reshape(rows, D).astype(BF16)
    pt = lax.dot_general(wat_ref[...], hb, (((1,), (1,)), ((), ())), preferred_element_type=F32)
    ut_ref[...] = pt.astype(BF16).reshape(S5_G, S5_H, rows)
    p = jnp.dot(hb, wbz_ref[...], preferred_element_type=F32)
    z_ref[...] = _silu(p[:, FN_W:]).astype(BF16)
    xw = jnp.dot(p[:, :FN_W].astype(BF16), wcs_ref[...], preferred_element_type=F32)
    for q in range(4):
        scr[q][...] = xw[:, q * LANE:(q + 1) * LANE]
    for q in range(4):
        part, half = divmod(q, 2)
        for bi in range(nb):
            piece = scr[q][pl.ds(bi, nc, stride=nb), :]
            lo = bi * FN_W + half * LANE
            xw_ref[part, :, lo:lo + LANE] = piece.astype(BF16)


def _even_in(xs, ncl, ada, l, g, w_in, layer, wcs):
    nc, bsz, _ = xs.shape
    rows = nc * bsz
    return pl.pallas_call(
        functools.partial(_even_in_kernel, ncl=ncl),
        grid=(CH,),
        in_specs=[pl.BlockSpec((nc, bsz, D), lambda i: (0, 0, i)),
                  pl.BlockSpec((None, None) + ada.shape[2:], lambda i: (l, 0, 0, 0)),
                  pl.BlockSpec((None, None) + ada.shape[2:], lambda i: (l, 1, 0, 0)),
                  pl.BlockSpec((1, D), lambda i: (0, 0)),
                  pl.BlockSpec((None, D, w_in.shape[2]), lambda i: (layer, 0, 0),
                               pipeline_mode=pl.Buffered(1)),
                  pl.BlockSpec((FN_W, 2 * FN_W), lambda i: (0, 0))],
        out_specs=[pl.BlockSpec((S5_G, S5_H, rows), lambda i: (0, i, 0)),
                   pl.BlockSpec((2, None, nc, bsz * FN_W), lambda i: (0, i, 0, 0)),
                   pl.BlockSpec((None, rows, MIX), lambda i: (i, 0, 0))],
        out_shape=[jax.ShapeDtypeStruct((S5_G, CH * S5_H, rows), BF16),
                   jax.ShapeDtypeStruct((2, CH, nc, bsz * FN_W), BF16),
                   jax.ShapeDtypeStruct((CH, rows, MIX), BF16)],
        scratch_shapes=[pltpu.VMEM((S5_W, D), BF16), pltpu.VMEM((D, FN_W + MIX), BF16)]
        + [pltpu.VMEM((rows, LANE), F32)] * 4,
        compiler_params=_cparams(("arbitrary",)),
    )(xs, ada, ada, g.reshape(1, D), w_in, wcs)


def _s5_kernel(ut_ref, mt_ref, be_ref, cp_ref, l16_ref, yt_ref,
               sre_ref, sim_ref, are_ref, aim_ref, bre_ref, bim_ref, *, bsz, ncl, ncc, gs):
    nl = bsz * ncl
    nst = 2 * S5_P
    for g in range(gs):
        st = lax.dot_general(ut_ref[g], be_ref[g], (((0,), (0,)), ((), ())), preferred_element_type=F32)
        sre_ref[g] = st[:, :nst]
        sim_ref[g] = st[:, nst:]

    lam = [l16_ref[g] for g in range(gs)]
    is_fwd = lax.broadcasted_iota(jnp.int32, (bsz, nst), 1) < S5_P

    def make_step(base, nchunk):
        def step(c, carry):
            rf = pl.ds(pl.multiple_of(base + c * bsz, bsz), bsz)
            rb = pl.ds(pl.multiple_of(base + (nchunk - 1 - c) * bsz, bsz), bsz)
            out = []
            for g in range(gs):
                sre, sim = carry[2 * g], carry[2 * g + 1]
                lre = lam[g][0:1, :]
                lim = lam[g][1:2, :]
                are_ref[g, rf, :] = sre
                aim_ref[g, rf, :] = sim
                bre_ref[g, rb, :] = sre
                bim_ref[g, rb, :] = sim
                in_re = jnp.where(is_fwd, sre_ref[g, rf, :], sre_ref[g, rb, :])
                in_im = jnp.where(is_fwd, sim_ref[g, rf, :], sim_ref[g, rb, :])
                out.append(lre * sre - lim * sim + in_re)
                out.append(lre * sim + lim * sre + in_im)
            return tuple(out)
        return step

    zero = jnp.zeros((bsz, nst), F32)
    carry = lax.fori_loop(0, ncc, make_step(nl, ncc), (zero,) * (2 * gs))
    lax.fori_loop(0, ncl, make_step(0, ncl), carry)

    nt = (((1,), (1,)), ((), ()))
    fwd_rows = lax.broadcasted_iota(jnp.int32, (bsz * (ncl + ncc), nst), 1) < S5_P
    for g in range(gs):
        s0 = jnp.concatenate([jnp.where(fwd_rows, are_ref[g], bre_ref[g]),
                              jnp.where(fwd_rows, aim_ref[g], bim_ref[g])], axis=1).astype(BF16)
        yt_ref[g] = (jnp.dot(mt_ref[g], ut_ref[g], preferred_element_type=F32)
                     + lax.dot_general(cp_ref[g], s0, nt, preferred_element_type=F32)).astype(BF16)


def _s5_mix(ut, ncl, mt, bend, cp, lam16, bsz, layer):
    rows = ut.shape[2]
    kk = CH * S5_H
    nst = 2 * S5_P
    gs = 6
    off = layer * (S5_G // gs)

    def gspec(r, c):
        return pl.BlockSpec((gs, r, c), lambda g: (g, 0, 0))

    def tspec(r, c):
        return pl.BlockSpec((gs, r, c), lambda g: (g + off, 0, 0))

    return pl.pallas_call(
        functools.partial(_s5_kernel, bsz=bsz, ncl=ncl, ncc=rows // bsz - ncl, gs=gs),
        grid=(S5_G // gs,),
        in_specs=[gspec(kk, rows), tspec(kk, kk), tspec(kk, 2 * nst), tspec(kk, 2 * nst), tspec(2, nst)],
        out_specs=gspec(kk, rows),
        out_shape=jax.ShapeDtypeStruct((S5_G, kk, rows), BF16),
        scratch_shapes=[pltpu.VMEM((gs, rows, nst), F32)] * 6,
        compiler_params=_cparams(("arbitrary",)),
    )(ut, mt, bend, cp, lam16)


def _fnet_weights_kernel(ccs_ref, fw_ref, o_ref):
    fw = fw_ref[...]
    ccs = ccs_ref[...]
    wc = jnp.dot(ccs[:, :FN_W], fw, precision=HI, preferred_element_type=F32)
    ws = jnp.dot(ccs[:, FN_W:], fw, precision=HI, preferred_element_type=F32)
    o_ref[...] = jnp.concatenate([wc, ws], axis=1).astype(BF16)


def _fnet_weights(fw_bd):
    return pl.pallas_call(
        _fnet_weights_kernel,
        out_shape=jax.ShapeDtypeStruct((FN_W, 2 * FN_W), BF16),
    )(_dft_channel_matrix(), fw_bd)


def _fnet_kernel(tab_ref, v_ref, jr_ref, o_ref, *, ipt):
    r = pl.program_id(1)
    tab = tab_ref[...].astype(BF16)
    v = v_ref[...]
    nc, n = v.shape[2], v.shape[3]
    t = CH * nc
    a = jnp.dot(tab[:, :t], v[0].reshape(t, n), preferred_element_type=F32)
    b = jnp.dot(tab[:, t:], v[1].reshape(t, n), preferred_element_type=F32)
    o_ref[pl.ds(r * ipt, ipt)] = (a - b).reshape(ipt, nc, n).astype(BF16)
    m = (a + b).astype(BF16)
    for k in range(ipt):
        i = r * ipt + k

        @pl.when(jnp.logical_and(i >= 1, i <= CH // 2 - 1))
        def _():
            o_ref[CH - i] = jnp.dot(jr_ref[...], m[k * nc:(k + 1) * nc],
                                    preferred_element_type=F32).astype(BF16)


def _dft_half_table(nc):
    t = nc * CH
    cols = (np.arange(nc)[None, :] * CH + np.arange(CH)[:, None]).reshape(-1)
    rows = (np.arange(nc)[None, :] * CH + np.arange(CH // 2 + 1)[:, None]).reshape(-1)
    prod = (rows[:, None].astype(np.int64) * cols[None, :].astype(np.int64)) % t
    ang = prod.astype(np.float64) * (2.0 * np.pi / t)
    scale = 1.0 / math.sqrt(t * FN_GW)
    return jnp.asarray(np.concatenate([np.cos(ang), np.sin(ang)], axis=1) * scale, dtype=F32)


def _dft_channel_matrix():
    c = np.arange(FN_GW)
    ang = (c[:, None] * c[None, :] % FN_GW).astype(np.float64) * (2.0 * np.pi / FN_GW)
    eye = np.eye(FN_G)
    return jnp.asarray(np.concatenate([np.kron(eye, np.cos(ang)), np.kron(eye, np.sin(ang))], axis=1),
                       dtype=F32)


def _fnet(xw, c0, nc, bsz):
    t = nc * CH
    nhb = CH // 2 + 1
    ipt = 3
    bpb = 4
    jr = jnp.asarray(np.eye(nc)[::-1], dtype=BF16)
    return pl.pallas_call(
        functools.partial(_fnet_kernel, ipt=ipt),
        grid=(bsz // bpb, nhb // ipt),
        in_specs=[pl.BlockSpec((ipt * nc, 2 * t), lambda b, r: (r, 0)),
                  pl.BlockSpec((2, CH, nc, bpb * FN_W), lambda b, r: (0, 0, c0 // nc, b)),
                  pl.BlockSpec((nc, nc), lambda b, r: (0, 0))],
        out_specs=pl.BlockSpec((CH, nc, bpb * FN_W), lambda b, r: (0, 0, b)),
        out_shape=jax.ShapeDtypeStruct((CH, nc, bsz * FN_W), BF16),
        compiler_params=_cparams(("arbitrary", "arbitrary")),
    )(_dft_half_table(nc), xw, jr)


def _even_out_kernel(*refs, ncl, with_ctx):
    it = iter(refs)
    yt_ref, ybl_ref = next(it), next(it)
    ybc_ref = next(it) if with_ctx else None
    z_ref, x_ref, gate_ref, gwf_ref, gb_ref, wof_ref, o_ref = (next(it), next(it), next(it), next(it),
                                                               next(it), next(it), next(it))
    scr = [next(it), next(it)]
    gw_ref, wo_ref = next(it), next(it)

    @pl.when(pl.program_id(0) == 0)
    def _():
        gw_ref[...] = (0.5 * gwf_ref[...]).astype(BF16)
        wo_ref[...] = wof_ref[...].astype(BF16)

    yt = yt_ref[...]
    rows = yt.shape[2]
    ya = _gelu_tanh(yt.astype(F32).reshape(S5_W, rows).T)
    half_g = jnp.dot(ya.astype(BF16), gw_ref[...], preferred_element_type=F32) + 0.5 * gb_ref[...]
    hy = 0.5 * ya
    ya = hy + hy * jnp.tanh(half_g)
    x = x_ref[...]
    nc, nb, _ = x.shape
    for half in range(2):
        for bi in range(nb):
            lo = bi * FN_W + half * LANE
            scr[half][pl.ds(bi, ncl, stride=nb), :] = ybl_ref[:, lo:lo + LANE].astype(F32)
            if with_ctx:
                scr[half][pl.ds(ncl * nb + bi, nc - ncl, stride=nb), :] = (
                    ybc_ref[:, lo:lo + LANE].astype(F32))
    yb = jnp.concatenate([scr[0][...], scr[1][...]], axis=1)
    sz = z_ref[...]
    ma = (ya * sz[:, :S5_W]).astype(BF16)
    mb = (yb * sz[:, S5_W:]).astype(BF16)
    out = (jnp.dot(ma, wo_ref[0:S5_W, :], preferred_element_type=F32)
           + jnp.dot(mb, wo_ref[S5_W:MIX, :], preferred_element_type=F32)).reshape(x.shape)
    o_ref[0:ncl] = x[0:ncl] + gate_ref[0:nb][None] * out[0:ncl]
    if with_ctx:
        o_ref[ncl:nc] = x[ncl:nc] + gate_ref[nb:nb + 1][None] * out[ncl:nc]


def _even_out(yt, ybl, ybc, z, xs, ncl, ada, l, glu_w, glu_b, w_out, layer):
    bsz = xs.shape[1]
    with_ctx = ybc is not None
    nc = xs.shape[0] if with_ctx else ncl
    rows = nc * bsz
    args = [yt, ybl]
    specs = [pl.BlockSpec((S5_G, S5_H, rows), lambda j: (0, j, 0)),
             pl.BlockSpec((None, ncl, bsz * FN_W), lambda j: (j, 0, 0))]
    if with_ctx:
        args.append(ybc)
        specs.append(pl.BlockSpec((None, nc - ncl, bsz * FN_W), lambda j: (j, 0, 0)))
    args += [z, xs, ada, glu_w, glu_b.reshape(1, S5_W), w_out]
    specs += [pl.BlockSpec((None, rows, MIX), lambda j: (j, 0, 0)),
              pl.BlockSpec((nc, bsz, D), lambda j: (0, 0, j)),
              pl.BlockSpec((None, None) + ada.shape[2:], lambda j: (l, 2, 0, 0)),
              pl.BlockSpec((None, S5_W, S5_W), lambda j: (layer, 0, 0), pipeline_mode=pl.Buffered(1)),
              pl.BlockSpec((1, S5_W), lambda j: (0, 0)),
              pl.BlockSpec((None, MIX, D), lambda j: (layer, 0, 0), pipeline_mode=pl.Buffered(1))]
    return pl.pallas_call(
        functools.partial(_even_out_kernel, ncl=ncl, with_ctx=with_ctx),
        grid=(CH,),
        in_specs=specs,
        out_specs=pl.BlockSpec((nc, bsz, D), lambda j: (0, 0, j)),
        out_shape=jax.ShapeDtypeStruct((nc, bsz, CH * D), F32),
        scratch_shapes=[pltpu.VMEM((rows, LANE), F32)] * 2 + [pltpu.VMEM((S5_W, S5_W), BF16),
                                                             pltpu.VMEM((MIX, D), BF16)],
        compiler_params=_cparams(("arbitrary",)),
    )(*args)


def _odd_kernel(*refs, ct, ntl, t_lat, t_ctx, final):
    it = iter(refs)
    xm_ref, xp_ref, xn_ref = next(it), next(it), next(it)
    shift_ref, scale_ref, gate_ref, g_ref = next(it), next(it), next(it), next(it)
    wif_ref, wof_ref, pwf_ref, ps_ref, cw_ref = next(it), next(it), next(it), next(it), next(it)
    fg_ref = next(it) if final else None
    o_ref = next(it)
    h_ref = next(it)
    pe_ref = next(it)
    wi_ref, wo_ref, pw_ref = next(it), next(it), next(it)
    fin_ref = next(it) if final else None

    ti = pl.program_id(0)
    is_ctx = ti >= ntl
    t0 = jnp.where(is_ctx, ti - ntl, ti) * (ct * CH)
    t_total = jnp.where(is_ctx, t_ctx, t_lat)

    @pl.when(ti == 0)
    def _():
        wi_ref[...] = wif_ref[...].astype(BF16)
        wo_ref[...] = wof_ref[...].astype(BF16)
        pw_ref[...] = pwf_ref[...].astype(BF16)

    n1 = POOL_W + 2 * CONV_W
    nb = xm_ref.shape[1]
    tm = ct * CH
    ne = tm + 2 * HALO

    def rows_of(ref):
        return jnp.where(is_ctx, ref[nb:nb + 1], ref[0:nb])

    gain = g_ref[...] * (1.0 + rows_of(scale_ref))
    shift = rows_of(shift_ref)

    def hn(xv):
        return _mod_norm(xv, gain, shift)

    for i in range(CH):
        hi = hn(xm_ref[:, :, i * D:(i + 1) * D])
        if i < HALO:
            h_ref[0:ct, i + HALO] = hi
        else:
            h_ref[1:ct + 1, i - HALO] = hi
    for i in range(HALO):
        h_ref[0, i] = hn(xp_ref[:, :, (i + HALO) * D:(i + HALO + 1) * D])[0]
        h_ref[ct, i + HALO] = hn(xn_ref[:, :, i * D:(i + 1) * D])[0]

    he = h_ref[...].reshape(ne * nb, D).astype(BF16)
    pe = jnp.dot(he, wi_ref[:, 0:n1], preferred_element_type=F32)
    te = t0 - HALO + lax.broadcasted_iota(jnp.int32, (ne * nb, 1), 0) // nb
    valid = jnp.logical_and(te >= 0, te < t_total)
    pe_ref[:, 0:POOL_W] = jnp.where(valid, pe[:, :POOL_W], 0.0)
    pe_ref[:, POOL_W:POOL_W + CONV_W] = jnp.where(
        valid, pe[:, POOL_W:POOL_W + CONV_W] * pe[:, POOL_W + CONV_W:], 0.0)

    m0 = HALO * nb
    mr = tm * nb
    p2 = jnp.dot(he[m0:m0 + mr], wi_ref[:, n1:], preferred_element_type=F32)
    b_gate = p2[:, :CONV_W]
    sz = _silu(p2[:, CONV_W:])

    tpos = t0 + lax.broadcasted_iota(jnp.int32, (mr, 1), 0) // nb
    pooled = []
    for gi, w in enumerate(POOL_WINDOWS):
        c0 = gi * POOL_GW
        s = pe_ref[:, c0:c0 + POOL_GW]
        n = ne
        width = 1
        while width < w:
            s = s[0:(n - width) * nb] + s[width * nb:n * nb]
            n -= width
            width *= 2
        start = (HALO - w // 2) * nb
        total = s[start:start + mr]
        hi = jnp.minimum(tpos + w // 2, t_total)
        lo = jnp.maximum(tpos - w // 2, 0)
        inv_cnt = 1.0 / (hi - lo).astype(F32)
        centre = pe_ref[m0:m0 + mr, c0:c0 + POOL_GW]
        pg = total * inv_cnt - centre
        pooled.append(jnp.dot(pg.astype(BF16), pw_ref[gi], preferred_element_type=F32))
    y_c = jnp.concatenate(pooled, axis=1) * ps_ref[...]

    cwt = cw_ref[...]
    vm = pe_ref[m0 - nb:m0 - nb + mr, POOL_W:POOL_W + CONV_W]
    v0 = pe_ref[m0:m0 + mr, POOL_W:POOL_W + CONV_W]
    vp = pe_ref[m0 + nb:m0 + nb + mr, POOL_W:POOL_W + CONV_W]
    y_d = b_gate * (vm * cwt[0:1, :] + v0 * cwt[1:2, :] + vp * cwt[2:3, :])

    y = (jnp.concatenate([y_c, y_d], axis=1) * sz).astype(BF16)
    out = jnp.dot(y, wo_ref[...], preferred_element_type=F32)
    go = (rows_of(gate_ref) * out.reshape(tm, nb, D)).reshape(ct, CH, nb, D)
    for i in range(CH):
        xo = xm_ref[:, :, i * D:(i + 1) * D] + go[:, i]
        if not final:
            o_ref[:, :, i * D:(i + 1) * D] = xo
            continue
        ms = jnp.mean(xo * xo, axis=-1, keepdims=True)
        fin_ref[:, :, i * D:(i + 1) * D] = xo * lax.rsqrt(ms + EPS) * fg_ref[...]
    if final:
        o_ref[...] = pltpu.einshape("cb(id)->b(ci)d", fin_ref[...], i=CH)


def _odd_layer(xs, ncl, with_ctx, ada, l, g, w_in, w_out, pool_w, layer, pool_scale, conv_w, final_g):
    nc_all, bsz, _ = xs.shape
    ct = 8
    ntl = ncl // ct
    nc = nc_all if with_ctx else ncl
    final = final_g is not None
    args = [xs, xs, xs, ada, ada, ada, g.reshape(1, D), w_in, w_out, pool_w,
            pool_scale.reshape(1, POOL_W), conv_w]

    def vec(j):
        return pl.BlockSpec((None, None) + ada.shape[2:], lambda i: (l, j, 0, 0))

    def const(*shape):
        return pl.BlockSpec(shape, lambda i: (0,) * len(shape), pipeline_mode=pl.Buffered(1))

    def stacked(*shape):
        return pl.BlockSpec((None,) + shape, lambda i: (layer,) + (0,) * len(shape),
                            pipeline_mode=pl.Buffered(1))

    specs = [pl.BlockSpec((ct, bsz, CH * D), lambda i: (i, 0, 0)),
             pl.BlockSpec((1, bsz, CH * D), lambda i: (jnp.maximum(i * ct - 1, 0), 0, 0)),
             pl.BlockSpec((1, bsz, CH * D), lambda i: (jnp.minimum((i + 1) * ct, nc_all - 1), 0, 0)),
             vec(0), vec(1), vec(2),
             const(1, D), stacked(*w_in.shape[1:]), stacked(MIX, D),
             stacked(len(POOL_WINDOWS), POOL_GW, POOL_GW), const(1, POOL_W), const(3, CONV_W)]
    scratch = [pltpu.VMEM((ct + 1, CH, bsz, D), F32),
               pltpu.VMEM(((ct + 1) * CH * bsz, POOL_W + CONV_W), F32),
               pltpu.VMEM(w_in.shape[1:], BF16), pltpu.VMEM((MIX, D), BF16),
               pltpu.VMEM((len(POOL_WINDOWS), POOL_GW, POOL_GW), BF16)]
    if final:
        args.append(final_g.reshape(1, D))
        specs.append(const(1, D))
        scratch.append(pltpu.VMEM((ct, bsz, CH * D), F32))
        out_spec = pl.BlockSpec((bsz, ct * CH, D), lambda i: (0, i, 0))
        out_shape = jax.ShapeDtypeStruct((bsz, nc * CH, D), F32)
    else:
        out_spec = pl.BlockSpec((ct, bsz, CH * D), lambda i: (i, 0, 0))
        out_shape = jax.ShapeDtypeStruct((nc, bsz, CH * D), F32)
    return pl.pallas_call(
        functools.partial(_odd_kernel, ct=ct, ntl=ntl, t_lat=ncl * CH, t_ctx=(nc_all - ncl) * CH,
                          final=final),
        grid=(nc // ct,),
        in_specs=specs,
        out_specs=out_spec,
        out_shape=out_shape,
        scratch_shapes=scratch,
        compiler_params=_cparams(("arbitrary",)),
    )(*args)


def _sincos_table(n_tok, dim):
    rows = n_tok // GRID_W
    rr, cc = np.meshgrid(np.arange(rows, dtype=np.float64), np.arange(GRID_W, dtype=np.float64),
                         indexing='ij')
    rr = rr.reshape(-1, 1)
    cc = cc.reshape(-1, 1)
    quarter = dim // 4
    omega = POS_BASE ** (-np.arange(quarter, dtype=np.float64) / quarter)
    tab = np.concatenate([np.sin(rr * omega), np.cos(rr * omega), np.sin(cc * omega), np.cos(cc * omega)],
                         axis=-1)
    return jnp.asarray(tab, dtype=F32)


def _block_diag(w):
    g, c, _ = w.shape
    eye = jnp.eye(g, dtype=w.dtype)
    return (eye[:, None, :, None] * w[:, :, None, :]).reshape(g * c, g * c)


def kernel(x, c, ctx, c_ctx, norm_g, ada_w, ada_b, even_w_in, even_w_out, s5_lam_re, s5_lam_im, s5_log_step, s5_b_re, s5_b_im, s5_c_re, s5_c_im, s5_d, s5_glu_w, s5_glu_b, fnet_w, odd_w_in, odd_w_out, pool_w, pool_scale, conv_w, final_g):
    bsz, n_tok, _ = x.shape
    depth = norm_g.shape[0]
    ncl = n_tok // CH
    ncc = ctx.shape[1] // CH

    cond = jnp.concatenate([c, jnp.broadcast_to(c_ctx[None], (16 - bsz, D))], axis=0)
    ada, mt, bend, cp, lam16, xs = _prologue(cond, ada_w, ada_b, s5_lam_re, s5_lam_im, s5_log_step, s5_b_re,
                                             s5_b_im, s5_c_re, s5_c_im, s5_d, x, _sincos_table(n_tok, D), ctx)

    need_ctx = [any(j % 2 == 0 for j in range(l + 1, depth)) for l in range(depth)]

    for l in range(depth):
        i = l // 2
        last = l == depth - 1
        if l % 2 == 0:
            wcs = _fnet_weights(_block_diag(fnet_w[i]))
            ut, xw, z = _even_in(xs, ncl, ada, l, norm_g[l], even_w_in, i, wcs)
            yt = _s5_mix(ut, ncl, mt, bend, cp, lam16, bsz, i)
            ybl = _fnet(xw, 0, ncl, bsz)
            ybc = _fnet(xw, ncl, ncc, bsz) if need_ctx[l] else None
            xs = _even_out(yt, ybl, ybc, z, xs, ncl, ada, l, s5_glu_w, s5_glu_b[i], even_w_out, i)
        else:
            xs = _odd_layer(xs, ncl, need_ctx[l], ada, l, norm_g[l], odd_w_in, odd_w_out, pool_w, i,
                            pool_scale[i], conv_w[i], final_g if last else None)
    if depth % 2 == 1:
        raise NotImplementedError("final norm and (B, T, D) order are produced by the last (odd) layer")
    return xs
```

```python
import functools
import math

import numpy as np
import jax
import jax.numpy as jnp
from jax import lax
from jax.experimental import pallas as pl
from jax.experimental.pallas import tpu as pltpu

D = 1024
MIX = 1024
S5_W = 768
FN_W = 256
S5_H = 16
S5_G = 48
S5_P = 64
FN_G = 4
FN_GW = 64
POOL_W = 512
CONV_W = 512
POOL_WINDOWS = (2, 4, 8, 16)
POOL_GW = 128
GRID_W = 64
EPS = 1e-6
POS_BASE = 10000.0
CH = 16
HALO = 8
LANE = 128
VMEM_LIMIT = 56 * 1024 * 1024

F32 = jnp.float32
BF16 = jnp.bfloat16
HI = lax.Precision.HIGHEST


def _cparams(sem):
    return pltpu.CompilerParams(dimension_semantics=sem, vmem_limit_bytes=VMEM_LIMIT)


def _silu(v):
    h = 0.5 * v
    return h + h * jnp.tanh(h)


def _gelu_tanh(v):
    c = math.sqrt(2.0 / math.pi)
    h = 0.5 * v
    return h + h * jnp.tanh(v * (c + (c * 0.044715) * (v * v)))


def _mod_norm(x, gain, shift):
    ms = jnp.mean(x * x, axis=-1, keepdims=True)
    return x * lax.rsqrt(ms + EPS) * gain + shift


def _split3(v):
    hi = v.astype(BF16)
    lo = (v - hi.astype(F32)).astype(BF16)
    return hi, lo


def _ada_item(c_ref, w_ref, b_ref, ada_ref):
    s_hi, s_lo = _split3(_silu(c_ref[...]))
    w_hi, w_lo = _split3(w_ref[...])
    ada_ref[...] = (jnp.dot(s_hi, w_hi, preferred_element_type=F32)
                    + jnp.dot(s_hi, w_lo, preferred_element_type=F32)
                    + jnp.dot(s_lo, w_hi, preferred_element_type=F32) + b_ref[...])


def _stream_item(x_ref, pos_ref, c_ref, o_ref, step, ntl):
    @pl.when(step < ntl)
    def _():
        o_ref[...] = pltpu.einshape("b(ci)d->cb(id)", x_ref[...] + pos_ref[...][None], i=CH)

    @pl.when(step >= ntl)
    def _():
        o_ref[...] = pltpu.einshape("b(ci)d->cb(id)", c_ref[...], i=CH)


def _prologue_kernel(c_ref, w_ref, b_ref, lr_ref, li_ref, ls_ref, btr_ref, bti_ref, cr_ref, ci_ref, d_ref,
                     x_ref, pos_ref, ctx_ref, ada_ref, mt_ref, be_ref, cp_ref, l16_ref, xs_ref,
                     ere_ref, eim_ref, *, gb, n_ada, n_tab, ntl):
    step = pl.program_id(0)

    @pl.when(step < n_ada)
    def _():
        _ada_item(c_ref, w_ref, b_ref, ada_ref)

    @pl.when(step < n_tab)
    def _():
        _tables_item(lr_ref, li_ref, ls_ref, btr_ref, bti_ref, cr_ref, ci_ref, d_ref,
                     mt_ref, be_ref, cp_ref, l16_ref, ere_ref, eim_ref, gb)

    _stream_item(x_ref, pos_ref, ctx_ref, xs_ref, step, ntl)


def _tables_item(lr_ref, li_ref, ls_ref, btr_ref, bti_ref, cr_ref, ci_ref, d_ref,
                 mt_ref, be_ref, cp_ref, l16_ref, ere_ref, eim_ref, gb):
    nst = 2 * S5_P
    kk = CH * S5_H
    step = jnp.exp(ls_ref[...])
    lr = lr_ref[...]
    li = li_ref[...]
    a = lr * step
    b = li * step

    mag = jnp.exp(a)
    l1re = mag * jnp.cos(b)
    l1im = mag * jnp.sin(b)
    squares = [(l1re, l1im)]
    for _ in range(4):
        sr, si = squares[-1]
        squares.append((sr * sr - si * si, 2.0 * (sr * si)))

    def powers(expo):
        pr = jnp.ones((gb,) + expo.shape[1:], F32)
        pi = jnp.zeros((gb,) + expo.shape[1:], F32)
        for bit, (sr, si) in enumerate(squares):
            on = ((expo >> bit) & 1) == 1
            pr, pi = jnp.where(on, pr * sr - pi * si, pr), jnp.where(on, pr * si + pi * sr, pi)
        return pr, pi

    row = lax.broadcasted_iota(jnp.int32, (1, CH, nst), 1)
    fwd = lax.broadcasted_iota(jnp.int32, (1, CH, nst), 2) < S5_P

    n_re = l1re - 1.0
    den = lr * lr + li * li
    co_re = (n_re * lr + l1im * li) / den
    co_im = (l1im * lr - n_re * li) / den
    btr = btr_ref[...]
    bti = bti_ref[...]
    bb_re = co_re * btr - co_im * bti
    bb_im = co_re * bti + co_im * btr

    pe_re, pe_im = powers(jnp.where(fwd, (CH - 1) - row, row))
    for l in range(CH):
        pr = pe_re[:, l:l + 1, :]
        pi = pe_im[:, l:l + 1, :]
        ere_ref[:, l * S5_H:(l + 1) * S5_H, :] = pr * bb_re - pi * bb_im
        eim_ref[:, l * S5_H:(l + 1) * S5_H, :] = pr * bb_im + pi * bb_re

    cr = cr_ref[...]
    ci = ci_ref[...]
    pc_re, pc_im = powers(jnp.where(fwd, row + 1, CH - row))
    for j in range(CH):
        pr = pc_re[:, j:j + 1, :]
        pi = pc_im[:, j:j + 1, :]
        w_re = cr * pr - ci * pi
        w_im = cr * pi + ci * pr
        cp_ref[:, j * S5_H:(j + 1) * S5_H, :] = jnp.concatenate([w_re, -w_im], axis=2).astype(BF16)

    l16_ref[...] = jnp.concatenate(squares[4], axis=1)

    fwd2 = lax.broadcasted_iota(jnp.int32, (S5_H, nst), 1) < S5_P
    lane = lax.broadcasted_iota(jnp.int32, (S5_H, kk), 1)
    iblk = lane // S5_H
    hrow = lax.broadcasted_iota(jnp.int32, (S5_H, kk), 0)
    nt = (((1,), (1,)), ((), ()))
    for g in range(gb):
        ere = ere_ref[g]
        eim = eim_ref[g]
        e2 = jnp.concatenate([ere, eim], axis=1)
        be_ref[g] = e2.astype(BF16)
        e_hi, e_lo = _split3(e2)
        rhs = jnp.concatenate([e_hi, e_hi, e_lo], axis=1)
        crg = cr[g]
        cig = ci[g]

        c2 = jnp.concatenate([
            jnp.concatenate([jnp.where(fwd2, crg, 0.0), jnp.where(fwd2, -cig, 0.0)], axis=1),
            jnp.concatenate([jnp.where(fwd2, 0.0, crg), jnp.where(fwd2, 0.0, -cig)], axis=1)], axis=0)
        c_hi, c_lo = _split3(c2)
        lhs = jnp.concatenate([c_hi, c_lo, c_hi], axis=1)
        kfb = lax.dot_general(lhs, rhs, nt, preferred_element_type=F32)
        kf = kfb[0:S5_H]
        kb = kfb[S5_H:2 * S5_H]
        dg = d_ref[g]
        for j in range(CH):
            sf = (kk - (CH - 1 - j) * S5_H) % kk
            rf = pltpu.roll(kf, sf, 1) if sf else kf
            rb = pltpu.roll(kb, j * S5_H, 1) if j else kb
            blk = (jnp.where(iblk <= j, rf, 0.0) + jnp.where(iblk >= j, rb, 0.0)
                   + jnp.where(lane == j * S5_H + hrow, dg, 0.0))
            mt_ref[g, j * S5_H:(j + 1) * S5_H, :] = blk.astype(BF16)


def _prologue(cond, ada_w, ada_b, lam_re, lam_im, log_step, b_re, b_im, c_re, c_im, d_skip, x, pos, ctx):
    n = lam_re.shape[0] * S5_G
    gb = 8
    nst = 2 * S5_P
    kk = CH * S5_H

    def fb(v):
        return jnp.concatenate([v[:, 0], v[:, 1]], axis=-1).reshape(n, v.shape[3], nst)

    lr = fb(lam_re[:, :, :, None, :])
    li = fb(lam_im[:, :, :, None, :])
    ls = fb(jnp.broadcast_to(log_step[:, :, :, None, None], log_step.shape + (1, S5_P)))
    btr = fb(jnp.swapaxes(b_re, -1, -2))
    bti = fb(jnp.swapaxes(b_im, -1, -2))
    cr = fb(c_re)
    ci = fb(c_im)
    d = d_skip.reshape(n, S5_H, 1)

    depth = ada_w.shape[0]
    n_ada = depth * 3
    n_tab = n // gb
    bsz, t, _ = x.shape
    tc = ctx.shape[1]
    ct = 8
    tm = ct * CH
    ntl = t // tm
    n_str = (t + tc) // tm

    def spec(r, c):
        return pl.BlockSpec((gb, r, c), lambda s: (jnp.minimum(s, n_tab - 1), 0, 0))

    def ada_idx(s):
        s = jnp.minimum(s, n_ada - 1)
        return s // 3, s % 3

    def lat_tile(s):
        return jnp.minimum(s, ntl - 1)

    return pl.pallas_call(
        functools.partial(_prologue_kernel, gb=gb, n_ada=n_ada, n_tab=n_tab, ntl=ntl),
        grid=(max(n_ada, n_tab, n_str),),
        in_specs=[pl.BlockSpec((16, D), lambda s: (0, 0)),
                  pl.BlockSpec((None, D, D), lambda s: (ada_idx(s)[0], 0, ada_idx(s)[1])),
                  pl.BlockSpec((None, None, 1, D), lambda s: ada_idx(s) + (0, 0)),
                  spec(1, nst), spec(1, nst), spec(1, nst), spec(S5_H, nst), spec(S5_H, nst),
                  spec(S5_H, nst), spec(S5_H, nst), spec(S5_H, 1),
                  pl.BlockSpec((bsz, tm, D), lambda s: (0, lat_tile(s), 0)),
                  pl.BlockSpec((tm, D), lambda s: (lat_tile(s), 0)),
                  pl.BlockSpec((bsz, tm, D), lambda s: (0, jnp.clip(s - ntl, 0, tc // tm - 1), 0),
                               pipeline_mode=pl.Buffered(1))],
        out_specs=[pl.BlockSpec((None, None, 16, D), lambda s: ada_idx(s) + (0, 0)),
                   spec(kk, kk), spec(kk, 2 * nst), spec(kk, 2 * nst), spec(2, nst),
                   pl.BlockSpec((ct, bsz, CH * D), lambda s: (jnp.minimum(s, n_str - 1), 0, 0))],
        out_shape=[jax.ShapeDtypeStruct((depth, 3, 16, D), F32),
                   jax.ShapeDtypeStruct((n, kk, kk), BF16),
                   jax.ShapeDtypeStruct((n, kk, 2 * nst), BF16),
                   jax.ShapeDtypeStruct((n, kk, 2 * nst), BF16),
                   jax.ShapeDtypeStruct((n, 2, nst), F32),
                   jax.ShapeDtypeStruct(((t + tc) // CH, bsz, CH * D), F32)],
        scratch_shapes=[pltpu.VMEM((gb, kk, nst), F32)] * 2,
        compiler_params=_cparams(("arbitrary",)),
    )(cond, ada_w, ada_b.reshape(depth, 3, 1, D), lr, li, ls, btr, bti, cr, ci, d, x, pos, ctx)


def _even_in_kernel(x_ref, shift_ref, scale_ref, g_ref, w_ref, wcs_ref, ut_ref, xw_ref, z_ref,
                    wat_ref, wbz_ref, *scr, ncl):
    @pl.when(pl.program_id(0) == 0)
    def _():
        wat_ref[...] = w_ref[:, :S5_W].T.astype(BF16)
        wbz_ref[...] = w_ref[:, S5_W:].astype(BF16)

    x = x_ref[...]
    nc, nb, _ = x.shape
    rows = nc * nb
    gain = g_ref[...] * (1.0 + scale_ref[...])
    h = jnp.concatenate([_mod_norm(x[0:ncl], gain[0:nb][None], shift_ref[0:nb][None]),
                         _mod_norm(x[ncl:nc], gain[nb:nb + 1][None], shift_ref[nb:nb + 1][None])], axis=0)
    hb = h.reshape(rows, D).astype(BF16)
    pt = lax.dot_general(wat_ref[...], hb, (((1,), (1,)), ((), ())), preferred_element_type=F32)
    ut_ref[...] = pt.astype(BF16).reshape(S5_G, S5_H, rows)
    p = jnp.dot(hb, wbz_ref[...], preferred_element_type=F32)
    z_ref[...] = _silu(p[:, FN_W:]).astype(BF16)
    xw = jnp.dot(p[:, :FN_W].astype(BF16), wcs_ref[...], preferred_element_type=F32)
    for q in range(4):
        scr[q][...] = xw[:, q * LANE:(q + 1) * LANE]
    for q in range(4):
        part, half = divmod(q, 2)
        for bi in range(nb):
            piece = scr[q][pl.ds(bi, nc, stride=nb), :]
            lo = bi * FN_W + half * LANE
            xw_ref[part, :, lo:lo + LANE] = piece.astype(BF16)


def _even_in(xs, ncl, ada, l, g, w_in, layer, wcs):
    nc, bsz, _ = xs.shape
    rows = nc * bsz
    return pl.pallas_call(
        functools.partial(_even_in_kernel, ncl=ncl),
        grid=(CH,),
        in_specs=[pl.BlockSpec((nc, bsz, D), lambda i: (0, 0, i)),
                  pl.BlockSpec((None, None) + ada.shape[2:], lambda i: (l, 0, 0, 0)),
                  pl.BlockSpec((None, None) + ada.shape[2:], lambda i: (l, 1, 0, 0)),
                  pl.BlockSpec((1, D), lambda i: (0, 0)),
                  pl.BlockSpec((None, D, w_in.shape[2]), lambda i: (layer, 0, 0),
                               pipeline_mode=pl.Buffered(1)),
                  pl.BlockSpec((FN_W, 2 * FN_W), lambda i: (0, 0))],
        out_specs=[pl.BlockSpec((S5_G, S5_H, rows), lambda i: (0, i, 0)),
                   pl.BlockSpec((2, None, nc, bsz * FN_W), lambda i: (0, i, 0, 0)),
                   pl.BlockSpec((None, rows, MIX), lambda i: (i, 0, 0))],
        out_shape=[jax.ShapeDtypeStruct((S5_G, CH * S5_H, rows), BF16),
                   jax.ShapeDtypeStruct((2, CH, nc, bsz * FN_W), BF16),
                   jax.ShapeDtypeStruct((CH, rows, MIX), BF16)],
        scratch_shapes=[pltpu.VMEM((S5_W, D), BF16), pltpu.VMEM((D, FN_W + MIX), BF16)]
        + [pltpu.VMEM((rows, LANE), F32)] * 4,
        compiler_params=_cparams(("arbitrary",)),
    )(xs, ada, ada, g.reshape(1, D), w_in, wcs)


def _s5_kernel(ut_ref, mt_ref, be_ref, cp_ref, l16_ref, yt_ref,
               sre_ref, sim_ref, are_ref, aim_ref, bre_ref, bim_ref, *, bsz, ncl, ncc, gs):
    nl = bsz * ncl
    nst = 2 * S5_P
    for g in range(gs):
        st = lax.dot_general(ut_ref[g], be_ref[g], (((0,), (0,)), ((), ())), preferred_element_type=F32)
        sre_ref[g] = st[:, :nst]
        sim_ref[g] = st[:, nst:]

    lam = [l16_ref[g] for g in range(gs)]
    is_fwd = lax.broadcasted_iota(jnp.int32, (bsz, nst), 1) < S5_P

    def make_step(base, nchunk):
        def step(c, carry):
            rf = pl.ds(pl.multiple_of(base + c * bsz, bsz), bsz)
            rb = pl.ds(pl.multiple_of(base + (nchunk - 1 - c) * bsz, bsz), bsz)
            out = []
            for g in range(gs):
                sre, sim = carry[2 * g], carry[2 * g + 1]
                lre = lam[g][0:1, :]
                lim = lam[g][1:2, :]
                are_ref[g, rf, :] = sre
                aim_ref[g, rf, :] = sim
                bre_ref[g, rb, :] = sre
                bim_ref[g, rb, :] = sim
                in_re = jnp.where(is_fwd, sre_ref[g, rf, :], sre_ref[g, rb, :])
                in_im = jnp.where(is_fwd, sim_ref[g, rf, :], sim_ref[g, rb, :])
                out.append(lre * sre - lim * sim + in_re)
                out.append(lre * sim + lim * sre + in_im)
            return tuple(out)
        return step

    zero = jnp.zeros((bsz, nst), F32)
    carry = lax.fori_loop(0, ncc, make_step(nl, ncc), (zero,) * (2 * gs))
    lax.fori_loop(0, ncl, make_step(0, ncl), carry)

    nt = (((1,), (1,)), ((), ()))
    fwd_rows = lax.broadcasted_iota(jnp.int32, (bsz * (ncl + ncc), nst), 1) < S5_P
    for g in range(gs):
        s0 = jnp.concatenate([jnp.where(fwd_rows, are_ref[g], bre_ref[g]),
                              jnp.where(fwd_rows, aim_ref[g], bim_ref[g])], axis=1).astype(BF16)
        yt_ref[g] = (jnp.dot(mt_ref[g], ut_ref[g], preferred_element_type=F32)
                     + lax.dot_general(cp_ref[g], s0, nt, preferred_element_type=F32)).astype(BF16)


def _s5_mix(ut, ncl, mt, bend, cp, lam16, bsz, layer):
    rows = ut.shape[2]
    kk = CH * S5_H
    nst = 2 * S5_P
    gs = 6
    off = layer * (S5_G // gs)

    def gspec(r, c):
        return pl.BlockSpec((gs, r, c), lambda g: (g, 0, 0))

    def tspec(r, c):
        return pl.BlockSpec((gs, r, c), lambda g: (g + off, 0, 0))

    return pl.pallas_call(
        functools.partial(_s5_kernel, bsz=bsz, ncl=ncl, ncc=rows // bsz - ncl, gs=gs),
        grid=(S5_G // gs,),
        in_specs=[gspec(kk, rows), tspec(kk, kk), tspec(kk, 2 * nst), tspec(kk, 2 * nst), tspec(2, nst)],
        out_specs=gspec(kk, rows),
        out_shape=jax.ShapeDtypeStruct((S5_G, kk, rows), BF16),
        scratch_shapes=[pltpu.VMEM((gs, rows, nst), F32)] * 6,
        compiler_params=_cparams(("arbitrary",)),
    )(ut, mt, bend, cp, lam16)


def _fnet_weights_kernel(ccs_ref, fw_ref, o_ref):
    fw = fw_ref[...]
    ccs = ccs_ref[...]
    wc = jnp.dot(ccs[:, :FN_W], fw, precision=HI, preferred_element_type=F32)
    ws = jnp.dot(ccs[:, FN_W:], fw, precision=HI, preferred_element_type=F32)
    o_ref[...] = jnp.concatenate([wc, ws], axis=1).astype(BF16)


def _fnet_weights(fw_bd):
    return pl.pallas_call(
        _fnet_weights_kernel,
        out_shape=jax.ShapeDtypeStruct((FN_W, 2 * FN_W), BF16),
    )(_dft_channel_matrix(), fw_bd)


def _fnet_kernel(tab_ref, v_ref, jr_ref, o_ref, *, ipt):
    r = pl.program_id(1)
    tab = tab_ref[...].astype(BF16)
    v = v_ref[...]
    nc, n = v.shape[2], v.shape[3]
    t = CH * nc
    a = jnp.dot(tab[:, :t], v[0].reshape(t, n), preferred_element_type=F32)
    b = jnp.dot(tab[:, t:], v[1].reshape(t, n), preferred_element_type=F32)
    o_ref[pl.ds(r * ipt, ipt)] = (a - b).reshape(ipt, nc, n).astype(BF16)
    m = (a + b).astype(BF16)
    for k in range(ipt):
        i = r * ipt + k

        @pl.when(jnp.logical_and(i >= 1, i <= CH // 2 - 1))
        def _():
            o_ref[CH - i] = jnp.dot(jr_ref[...], m[k * nc:(k + 1) * nc],
                                    preferred_element_type=F32).astype(BF16)


def _dft_half_table(nc):
    t = nc * CH
    cols = (np.arange(nc)[None, :] * CH + np.arange(CH)[:, None]).reshape(-1)
    rows = (np.arange(nc)[None, :] * CH + np.arange(CH // 2 + 1)[:, None]).reshape(-1)
    prod = (rows[:, None].astype(np.int64) * cols[None, :].astype(np.int64)) % t
    ang = prod.astype(np.float64) * (2.0 * np.pi / t)
    scale = 1.0 / math.sqrt(t * FN_GW)
    return jnp.asarray(np.concatenate([np.cos(ang), np.sin(ang)], axis=1) * scale, dtype=F32)


def _dft_channel_matrix():
    c = np.arange(FN_GW)
    ang = (c[:, None] * c[None, :] % FN_GW).astype(np.float64) * (2.0 * np.pi / FN_GW)
    eye = np.eye(FN_G)
    return jnp.asarray(np.concatenate([np.kron(eye, np.cos(ang)), np.kron(eye, np.sin(ang))], axis=1),
                       dtype=F32)


def _fnet(xw, c0, nc, bsz):
    t = nc * CH
    nhb = CH // 2 + 1
    ipt = 3
    bpb = 4
    jr = jnp.asarray(np.eye(nc)[::-1], dtype=BF16)
    return pl.pallas_call(
        functools.partial(_fnet_kernel, ipt=ipt),
        grid=(bsz // bpb, nhb // ipt),
        in_specs=[pl.BlockSpec((ipt * nc, 2 * t), lambda b, r: (r, 0)),
                  pl.BlockSpec((2, CH, nc, bpb * FN_W), lambda b, r: (0, 0, c0 // nc, b)),
                  pl.BlockSpec((nc, nc), lambda b, r: (0, 0))],
        out_specs=pl.BlockSpec((CH, nc, bpb * FN_W), lambda b, r: (0, 0, b)),
        out_shape=jax.ShapeDtypeStruct((CH, nc, bsz * FN_W), BF16),
        compiler_params=_cparams(("arbitrary", "arbitrary")),
    )(_dft_half_table(nc), xw, jr)


def _even_out_kernel(*refs, ncl, with_ctx):
    it = iter(refs)
    yt_ref, ybl_ref = next(it), next(it)
    ybc_ref = next(it) if with_ctx else None
    z_ref, x_ref, gate_ref, gwf_ref, gb_ref, wof_ref, o_ref = (next(it), next(it), next(it), next(it),
                                                               next(it), next(it), next(it))
    scr = [next(it), next(it)]
    gw_ref, wo_ref = next(it), next(it)

    @pl.when(pl.program_id(0) == 0)
    def _():
        gw_ref[...] = (0.5 * gwf_ref[...]).astype(BF16)
        wo_ref[...] = wof_ref[...].astype(BF16)

    yt = yt_ref[...]
    rows = yt.shape[2]
    ya = _gelu_tanh(yt.astype(F32).reshape(S5_W, rows).T)
    half_g = jnp.dot(ya.astype(BF16), gw_ref[...], preferred_element_type=F32) + 0.5 * gb_ref[...]
    hy = 0.5 * ya
    ya = hy + hy * jnp.tanh(half_g)
    x = x_ref[...]
    nc, nb, _ = x.shape
    for half in range(2):
        for bi in range(nb):
            lo = bi * FN_W + half * LANE
            scr[half][pl.ds(bi, ncl, stride=nb), :] = ybl_ref[:, lo:lo + LANE].astype(F32)
            if with_ctx:
                scr[half][pl.ds(ncl * nb + bi, nc - ncl, stride=nb), :] = (
                    ybc_ref[:, lo:lo + LANE].astype(F32))
    yb = jnp.concatenate([scr[0][...], scr[1][...]], axis=1)
    sz = z_ref[...]
    ma = (ya * sz[:, :S5_W]).astype(BF16)
    mb = (yb * sz[:, S5_W:]).astype(BF16)
    out = (jnp.dot(ma, wo_ref[0:S5_W, :], preferred_element_type=F32)
           + jnp.dot(mb, wo_ref[S5_W:MIX, :], preferred_element_type=F32)).reshape(x.shape)
    o_ref[0:ncl] = x[0:ncl] + gate_ref[0:nb][None] * out[0:ncl]
    if with_ctx:
        o_ref[ncl:nc] = x[ncl:nc] + gate_ref[nb:nb + 1][None] * out[ncl:nc]


def _even_out(yt, ybl, ybc, z, xs, ncl, ada, l, glu_w, glu_b, w_out, layer):
    bsz = xs.shape[1]
    with_ctx = ybc is not None
    nc = xs.shape[0] if with_ctx else ncl
    rows = nc * bsz
    args = [yt, ybl]
    specs = [pl.BlockSpec((S5_G, S5_H, rows), lambda j: (0, j, 0)),
             pl.BlockSpec((None, ncl, bsz * FN_W), lambda j: (j, 0, 0))]
    if with_ctx:
        args.append(ybc)
        specs.append(pl.BlockSpec((None, nc - ncl, bsz * FN_W), lambda j: (j, 0, 0)))
    args += [z, xs, ada, glu_w, glu_b.reshape(1, S5_W), w_out]
    specs += [pl.BlockSpec((None, rows, MIX), lambda j: (j, 0, 0)),
              pl.BlockSpec((nc, bsz, D), lambda j: (0, 0, j)),
              pl.BlockSpec((None, None) + ada.shape[2:], lambda j: (l, 2, 0, 0)),
              pl.BlockSpec((None, S5_W, S5_W), lambda j: (layer, 0, 0), pipeline_mode=pl.Buffered(1)),
              pl.BlockSpec((1, S5_W), lambda j: (0, 0)),
              pl.BlockSpec((None, MIX, D), lambda j: (layer, 0, 0), pipeline_mode=pl.Buffered(1))]
    return pl.pallas_call(
        functools.partial(_even_out_kernel, ncl=ncl, with_ctx=with_ctx),
        grid=(CH,),
        in_specs=specs,
        out_specs=pl.BlockSpec((nc, bsz, D), lambda j: (0, 0, j)),
        out_shape=jax.ShapeDtypeStruct((nc, bsz, CH * D), F32),
        scratch_shapes=[pltpu.VMEM((rows, LANE), F32)] * 2 + [pltpu.VMEM((S5_W, S5_W), BF16),
                                                             pltpu.VMEM((MIX, D), BF16)],
        compiler_params=_cparams(("arbitrary",)),
    )(*args)


def _odd_kernel(*refs, ct, ntl, t_lat, t_ctx, final):
    it = iter(refs)
    xm_ref, xp_ref, xn_ref = next(it), next(it), next(it)
    shift_ref, scale_ref, gate_ref, g_ref = next(it), next(it), next(it), next(it)
    wif_ref, wof_ref, pwf_ref, ps_ref, cw_ref = next(it), next(it), next(it), next(it), next(it)
    fg_ref = next(it) if final else None
    o_ref = next(it)
    h_ref = next(it)
    pe_ref = next(it)
    wi_ref, wo_ref, pw_ref = next(it), next(it), next(it)
    fin_ref = next(it) if final else None

    ti = pl.program_id(0)
    is_ctx = ti >= ntl
    t0 = jnp.where(is_ctx, ti - ntl, ti) * (ct * CH)
    t_total = jnp.where(is_ctx, t_ctx, t_lat)

    @pl.when(ti == 0)
    def _():
        wi_ref[...] = wif_ref[...].astype(BF16)
        wo_ref[...] = wof_ref[...].astype(BF16)
        pw_ref[...] = pwf_ref[...].astype(BF16)

    n1 = POOL_W + 2 * CONV_W
    nb = xm_ref.shape[1]
    tm = ct * CH
    ne = tm + 2 * HALO

    def rows_of(ref):
        return jnp.where(is_ctx, ref[nb:nb + 1], ref[0:nb])

    gain = g_ref[...] * (1.0 + rows_of(scale_ref))
    shift = rows_of(shift_ref)

    def hn(xv):
        return _mod_norm(xv, gain, shift)

    for i in range(CH):
        hi = hn(xm_ref[:, :, i * D:(i + 1) * D])
        if i < HALO:
            h_ref[0:ct, i + HALO] = hi
        else:
            h_ref[1:ct + 1, i - HALO] = hi
    for i in range(HALO):
        h_ref[0, i] = hn(xp_ref[:, :, (i + HALO) * D:(i + HALO + 1) * D])[0]
        h_ref[ct, i + HALO] = hn(xn_ref[:, :, i * D:(i + 1) * D])[0]

    he = h_ref[...].reshape(ne * nb, D).astype(BF16)
    pe = jnp.dot(he, wi_ref[:, 0:n1], preferred_element_type=F32)
    te = t0 - HALO + lax.broadcasted_iota(jnp.int32, (ne * nb, 1), 0) // nb
    valid = jnp.logical_and(te >= 0, te < t_total)
    pe_ref[:, 0:POOL_W] = jnp.where(valid, pe[:, :POOL_W], 0.0)
    pe_ref[:, POOL_W:POOL_W + CONV_W] = jnp.where(
        valid, pe[:, POOL_W:POOL_W + CONV_W] * pe[:, POOL_W + CONV_W:], 0.0)

    m0 = HALO * nb
    mr = tm * nb
    p2 = jnp.dot(he[m0:m0 + mr], wi_ref[:, n1:], preferred_element_type=F32)
    b_gate = p2[:, :CONV_W]
    sz = _silu(p2[:, CONV_W:])

    tpos = t0 + lax.broadcasted_iota(jnp.int32, (mr, 1), 0) // nb
    pooled = []
    for gi, w in enumerate(POOL_WINDOWS):
        c0 = gi * POOL_GW
        s = pe_ref[:, c0:c0 + POOL_GW]
        n = ne
        width = 1
        while width < w:
            s = s[0:(n - width) * nb] + s[width * nb:n * nb]
            n -= width
            width *= 2
        start = (HALO - w // 2) * nb
        total = s[start:start + mr]
        hi = jnp.minimum(tpos + w // 2, t_total)
        lo = jnp.maximum(tpos - w // 2, 0)
        inv_cnt = 1.0 / (hi - lo).astype(F32)
        centre = pe_ref[m0:m0 + mr, c0:c0 + POOL_GW]
        pg = total * inv_cnt - centre
        pooled.append(jnp.dot(pg.astype(BF16), pw_ref[gi], preferred_element_type=F32))
    y_c = jnp.concatenate(pooled, axis=1) * ps_ref[...]

    cwt = cw_ref[...]
    vm = pe_ref[m0 - nb:m0 - nb + mr, POOL_W:POOL_W + CONV_W]
    v0 = pe_ref[m0:m0 + mr, POOL_W:POOL_W + CONV_W]
    vp = pe_ref[m0 + nb:m0 + nb + mr, POOL_W:POOL_W + CONV_W]
    y_d = b_gate * (vm * cwt[0:1, :] + v0 * cwt[1:2, :] + vp * cwt[2:3, :])

    y = (jnp.concatenate([y_c, y_d], axis=1) * sz).astype(BF16)
    out = jnp.dot(y, wo_ref[...], preferred_element_type=F32)
    go = (rows_of(gate_ref) * out.reshape(tm, nb, D)).reshape(ct, CH, nb, D)
    for i in range(CH):
        xo = xm_ref[:, :, i * D:(i + 1) * D] + go[:, i]
        if not final:
            o_ref[:, :, i * D:(i + 1) * D] = xo
            continue
        ms = jnp.mean(xo * xo, axis=-1, keepdims=True)
        fin_ref[:, :, i * D:(i + 1) * D] = xo * lax.rsqrt(ms + EPS) * fg_ref[...]
    if final:
        o_ref[...] = pltpu.einshape("cb(id)->b(ci)d", fin_ref[...], i=CH)


def _odd_layer(xs, ncl, with_ctx, ada, l, g, w_in, w_out, pool_w, layer, pool_scale, conv_w, final_g):
    nc_all, bsz, _ = xs.shape
    ct = 8
    ntl = ncl // ct
    nc = nc_all if with_ctx else ncl
    final = final_g is not None
    args = [xs, xs, xs, ada, ada, ada, g.reshape(1, D), w_in, w_out, pool_w,
            pool_scale.reshape(1, POOL_W), conv_w]

    def vec(j):
        return pl.BlockSpec((None, None) + ada.shape[2:], lambda i: (l, j, 0, 0))

    def const(*shape):
        return pl.BlockSpec(shape, lambda i: (0,) * len(shape), pipeline_mode=pl.Buffered(1))

    def stacked(*shape):
        return pl.BlockSpec((None,) + shape, lambda i: (layer,) + (0,) * len(shape),
                            pipeline_mode=pl.Buffered(1))

    specs = [pl.BlockSpec((ct, bsz, CH * D), lambda i: (i, 0, 0)),
             pl.BlockSpec((1, bsz, CH * D), lambda i: (jnp.maximum(i * ct - 1, 0), 0, 0)),
             pl.BlockSpec((1, bsz, CH * D), lambda i: (jnp.minimum((i + 1) * ct, nc_all - 1), 0, 0)),
             vec(0), vec(1), vec(2),
             const(1, D), stacked(*w_in.shape[1:]), stacked(MIX, D),
             stacked(len(POOL_WINDOWS), POOL_GW, POOL_GW), const(1, POOL_W), const(3, CONV_W)]
    scratch = [pltpu.VMEM((ct + 1, CH, bsz, D), F32),
               pltpu.VMEM(((ct + 1) * CH * bsz, POOL_W + CONV_W), F32),
               pltpu.VMEM(w_in.shape[1:], BF16), pltpu.VMEM((MIX, D), BF16),
               pltpu.VMEM((len(POOL_WINDOWS), POOL_GW, POOL_GW), BF16)]
    if final:
        args.append(final_g.reshape(1, D))
        specs.append(const(1, D))
        scratch.append(pltpu.VMEM((ct, bsz, CH * D), F32))
        out_spec = pl.BlockSpec((bsz, ct * CH, D), lambda i: (0, i, 0))
        out_shape = jax.ShapeDtypeStruct((bsz, nc * CH, D), F32)
    else:
        out_spec = pl.BlockSpec((ct, bsz, CH * D), lambda i: (i, 0, 0))
        out_shape = jax.ShapeDtypeStruct((nc, bsz, CH * D), F32)
    return pl.pallas_call(
        functools.partial(_odd_kernel, ct=ct, ntl=ntl, t_lat=ncl * CH, t_ctx=(nc_all - ncl) * CH,
                          final=final),
        grid=(nc // ct,),
        in_specs=specs,
        out_specs=out_spec,
        out_shape=out_shape,
        scratch_shapes=scratch,
        compiler_params=_cparams(("arbitrary",)),
    )(*args)


def _sincos_table(n_tok, dim):
    rows = n_tok // GRID_W
    rr, cc = np.meshgrid(np.arange(rows, dtype=np.float64), np.arange(GRID_W, dtype=np.float64),
                         indexing='ij')
    rr = rr.reshape(-1, 1)
    cc = cc.reshape(-1, 1)
    quarter = dim // 4
    omega = POS_BASE ** (-np.arange(quarter, dtype=np.float64) / quarter)
    tab = np.concatenate([np.sin(rr * omega), np.cos(rr * omega), np.sin(cc * omega), np.cos(cc * omega)],
                         axis=-1)
    return jnp.asarray(tab, dtype=F32)


def _block_diag(w):
    g, c, _ = w.shape
    eye = jnp.eye(g, dtype=w.dtype)
    return (eye[:, None, :, None] * w[:, :, None, :]).reshape(g * c, g * c)


def kernel(x, c, ctx, c_ctx, norm_g, ada_w, ada_b, even_w_in, even_w_out, s5_lam_re, s5_lam_im, s5_log_step, s5_b_re, s5_b_im, s5_c_re, s5_c_im, s5_d, s5_glu_w, s5_glu_b, fnet_w, odd_w_in, odd_w_out, pool_w, pool_scale, conv_w, final_g):
    bsz, n_tok, _ = x.shape
    depth = norm_g.shape[0]
    ncl = n_tok // CH
    ncc = ctx.shape[1] // CH

    cond = jnp.concatenate([c, jnp.broadcast_to(c_ctx[None], (16 - bsz, D))], axis=0)
    ada, mt, bend, cp, lam16, xs = _prologue(cond, ada_w, ada_b, s5_lam_re, s5_lam_im, s5_log_step, s5_b_re,
                                             s5_b_im, s5_c_re, s5_c_im, s5_d, x, _sincos_table(n_tok, D), ctx)

    need_ctx = [any(j % 2 == 0 for j in range(l + 1, depth)) for l in range(depth)]

    for l in range(depth):
        i = l // 2
        last = l == depth - 1
        if l % 2 == 0:
            wcs = _fnet_weights(_block_diag(fnet_w[i]))
            ut, xw, z = _even_in(xs, ncl, ada, l, norm_g[l], even_w_in, i, wcs)
            yt = _s5_mix(ut, ncl, mt, bend, cp, lam16, bsz, i)
            ybl = _fnet(xw, 0, ncl, bsz)
            ybc = _fnet(xw, ncl, ncc, bsz) if need_ctx[l] else None
            xs = _even_out(yt, ybl, ybc, z, xs, ncl, ada, l, s5_glu_w, s5_glu_b[i], even_w_out, i)
        else:
            xs = _odd_layer(xs, ncl, need_ctx[l], ada, l, norm_g[l], odd_w_in, odd_w_out, pool_w, i,
                            pool_scale[i], conv_w[i], final_g if last else None)
    if depth % 2 == 1:
        raise NotImplementedError("final norm and (B, T, D) order are produced by the last (odd) layer")
    return xs
```

```python
import functools
import math

import numpy as np
import jax
import jax.numpy as jnp
from jax import lax
from jax.experimental import pallas as pl
from jax.experimental.pallas import tpu as pltpu

D = 1024
MIX = 1024
S5_W = 768
FN_W = 256
S5_H = 16
S5_G = 48
S5_P = 64
FN_G = 4
FN_GW = 64
POOL_W = 512
CONV_W = 512
POOL_WINDOWS = (2, 4, 8, 16)
POOL_GW = 128
GRID_W = 64
EPS = 1e-6
POS_BASE = 10000.0
CH = 16
HALO = 8
LANE = 128
VMEM_LIMIT = 56 * 1024 * 1024

F32 = jnp.float32
BF16 = jnp.bfloat16
HI = lax.Precision.HIGHEST


def _cparams(sem):
    return pltpu.CompilerParams(dimension_semantics=sem, vmem_limit_bytes=VMEM_LIMIT)


def _silu(v):
    h = 0.5 * v
    return h + h * jnp.tanh(h)


def _gelu_tanh(v):
    c = math.sqrt(2.0 / math.pi)
    h = 0.5 * v
    return h + h * jnp.tanh(v * (c + (c * 0.044715) * (v * v)))


def _mod_norm(x, gain, shift):
    ms = jnp.mean(x * x, axis=-1, keepdims=True)
    return x * lax.rsqrt(ms + EPS) * gain + shift


def _split3(v):
    hi = v.astype(BF16)
    lo = (v - hi.astype(F32)).astype(BF16)
    return hi, lo


def _ada_item(c_ref, w_ref, b_ref, ada_ref):
    s_hi, s_lo = _split3(_silu(c_ref[...]))
    w_hi, w_lo = _split3(w_ref[...])
    ada_ref[...] = (jnp.dot(s_hi, w_hi, preferred_element_type=F32)
                    + jnp.dot(s_hi, w_lo, preferred_element_type=F32)
                    + jnp.dot(s_lo, w_hi, preferred_element_type=F32) + b_ref[...])


def _stream_item(x_ref, pos_ref, c_ref, o_ref, step, ntl):
    @pl.when(step < ntl)
    def _():
        o_ref[...] = pltpu.einshape("b(ci)d->cb(id)", x_ref[...] + pos_ref[...][None], i=CH)

    @pl.when(step >= ntl)
    def _():
        o_ref[...] = pltpu.einshape("b(ci)d->cb(id)", c_ref[...], i=CH)


def _prologue_kernel(c_ref, w_ref, b_ref, lr_ref, li_ref, ls_ref, btr_ref, bti_ref, cr_ref, ci_ref, d_ref,
                     x_ref, pos_ref, ctx_ref, ada_ref, mt_ref, be_ref, cp_ref, l16_ref, xs_ref,
                     ere_ref, eim_ref, *, gb, n_ada, n_tab, ntl):
    step = pl.program_id(0)

    @pl.when(step < n_ada)
    def _():
        _ada_item(c_ref, w_ref, b_ref, ada_ref)

    @pl.when(step < n_tab)
    def _():
        _tables_item(lr_ref, li_ref, ls_ref, btr_ref, bti_ref, cr_ref, ci_ref, d_ref,
                     mt_ref, be_ref, cp_ref, l16_ref, ere_ref, eim_ref, gb)

    _stream_item(x_ref, pos_ref, ctx_ref, xs_ref, step, ntl)


def _tables_item(lr_ref, li_ref, ls_ref, btr_ref, bti_ref, cr_ref, ci_ref, d_ref,
                 mt_ref, be_ref, cp_ref, l16_ref, ere_ref, eim_ref, gb):
    nst = 2 * S5_P
    kk = CH * S5_H
    step = jnp.exp(ls_ref[...])
    lr = lr_ref[...]
    li = li_ref[...]
    a = lr * step
    b = li * step

    mag = jnp.exp(a)
    l1re = mag * jnp.cos(b)
    l1im = mag * jnp.sin(b)
    squares = [(l1re, l1im)]
    for _ in range(4):
        sr, si = squares[-1]
        squares.append((sr * sr - si * si, 2.0 * (sr * si)))

    def powers(expo):
        pr = jnp.ones((gb,) + expo.shape[1:], F32)
        pi = jnp.zeros((gb,) + expo.shape[1:], F32)
        for bit, (sr, si) in enumerate(squares):
            on = ((expo >> bit) & 1) == 1
            pr, pi = jnp.where(on, pr * sr - pi * si, pr), jnp.where(on, pr * si + pi * sr, pi)
        return pr, pi

    row = lax.broadcasted_iota(jnp.int32, (1, CH, nst), 1)
    fwd = lax.broadcasted_iota(jnp.int32, (1, CH, nst), 2) < S5_P

    n_re = l1re - 1.0
    den = lr * lr + li * li
    co_re = (n_re * lr + l1im * li) / den
    co_im = (l1im * lr - n_re * li) / den
    btr = btr_ref[...]
    bti = bti_ref[...]
    bb_re = co_re * btr - co_im * bti
    bb_im = co_re * bti + co_im * btr

    pe_re, pe_im = powers(jnp.where(fwd, (CH - 1) - row, row))
    for l in range(CH):
        pr = pe_re[:, l:l + 1, :]
        pi = pe_im[:, l:l + 1, :]
        ere_ref[:, l * S5_H:(l + 1) * S5_H, :] = pr * bb_re - pi * bb_im
        eim_ref[:, l * S5_H:(l + 1) * S5_H, :] = pr * bb_im + pi * bb_re

    cr = cr_ref[...]
    ci = ci_ref[...]
    pc_re, pc_im = powers(jnp.where(fwd, row + 1, CH - row))
    for j in range(CH):
        pr = pc_re[:, j:j + 1, :]
        pi = pc_im[:, j:j + 1, :]
        w_re = cr * pr - ci * pi
        w_im = cr * pi + ci * pr
        cp_ref[:, j * S5_H:(j + 1) * S5_H, :] = jnp.concatenate([w_re, -w_im], axis=2).astype(BF16)

    l16_ref[...] = jnp.concatenate(squares[4], axis=1)

    fwd2 = lax.broadcasted_iota(jnp.int32, (S5_H, nst), 1) < S5_P
    lane = lax.broadcasted_iota(jnp.int32, (S5_H, kk), 1)
    iblk = lane // S5_H
    hrow = lax.broadcasted_iota(jnp.int32, (S5_H, kk), 0)
    nt = (((1,), (1,)), ((), ()))
    for g in range(gb):
        ere = ere_ref[g]
        eim = eim_ref[g]
        e2 = jnp.concatenate([ere, eim], axis=1)
        be_ref[g] = e2.astype(BF16)
        e_hi, e_lo = _split3(e2)
        rhs = jnp.concatenate([e_hi, e_hi, e_lo], axis=1)
        crg = cr[g]
        cig = ci[g]

        c2 = jnp.concatenate([
            jnp.concatenate([jnp.where(fwd2, crg, 0.0), jnp.where(fwd2, -cig, 0.0)], axis=1),
            jnp.concatenate([jnp.where(fwd2, 0.0, crg), jnp.where(fwd2, 0.0, -cig)], axis=1)], axis=0)
        c_hi, c_lo = _split3(c2)
        lhs = jnp.concatenate([c_hi, c_lo, c_hi], axis=1)
        kfb = lax.dot_general(lhs, rhs, nt, preferred_element_type=F32)
        kf = kfb[0:S5_H]
        kb = kfb[S5_H:2 * S5_H]
        dg = d_ref[g]
        for j in range(CH):
            sf = (kk - (CH - 1 - j) * S5_H) % kk
            rf = pltpu.roll(kf, sf, 1) if sf else kf
            rb = pltpu.roll(kb, j * S5_H, 1) if j else kb
            blk = (jnp.where(iblk <= j, rf, 0.0) + jnp.where(iblk >= j, rb, 0.0)
                   + jnp.where(lane == j * S5_H + hrow, dg, 0.0))
            mt_ref[g, j * S5_H:(j + 1) * S5_H, :] = blk.astype(BF16)


def _prologue(cond, ada_w, ada_b, lam_re, lam_im, log_step, b_re, b_im, c_re, c_im, d_skip, x, pos, ctx):
    n = lam_re.shape[0] * S5_G
    gb = 8
    nst = 2 * S5_P
    kk = CH * S5_H

    def fb(v):
        return jnp.concatenate([v[:, 0], v[:, 1]], axis=-1).reshape(n, v.shape[3], nst)

    lr = fb(lam_re[:, :, :, None, :])
    li = fb(lam_im[:, :, :, None, :])
    ls = fb(jnp.broadcast_to(log_step[:, :, :, None, None], log_step.shape + (1, S5_P)))
    btr = fb(jnp.swapaxes(b_re, -1, -2))
    bti = fb(jnp.swapaxes(b_im, -1, -2))
    cr = fb(c_re)
    ci = fb(c_im)
    d = d_skip.reshape(n, S5_H, 1)

    depth = ada_w.shape[0]
    n_ada = depth * 3
    n_tab = n // gb
    bsz, t, _ = x.shape
    tc = ctx.shape[1]
    ct = 8
    tm = ct * CH
    ntl = t // tm
    n_str = (t + tc) // tm

    def spec(r, c):
        return pl.BlockSpec((gb, r, c), lambda s: (jnp.minimum(s, n_tab - 1), 0, 0))

    def ada_idx(s):
        s = jnp.minimum(s, n_ada - 1)
        return s // 3, s % 3

    def lat_tile(s):
        return jnp.minimum(s, ntl - 1)

    return pl.pallas_call(
        functools.partial(_prologue_kernel, gb=gb, n_ada=n_ada, n_tab=n_tab, ntl=ntl),
        grid=(max(n_ada, n_tab, n_str),),
        in_specs=[pl.BlockSpec((16, D), lambda s: (0, 0)),
                  pl.BlockSpec((None, D, D), lambda s: (ada_idx(s)[0], 0, ada_idx(s)[1])),
                  pl.BlockSpec((None, None, 1, D), lambda s: ada_idx(s) + (0, 0)),
                  spec(1, nst), spec(1, nst), spec(1, nst), spec(S5_H, nst), spec(S5_H, nst),
                  spec(S5_H, nst), spec(S5_H, nst), spec(S5_H, 1),
                  pl.BlockSpec((bsz, tm, D), lambda s: (0, lat_tile(s), 0)),
                  pl.BlockSpec((tm, D), lambda s: (lat_tile(s), 0)),
                  pl.BlockSpec((bsz, tm, D), lambda s: (0, jnp.clip(s - ntl, 0, tc // tm - 1), 0),
                               pipeline_mode=pl.Buffered(1))],
        out_specs=[pl.BlockSpec((None, None, 16, D), lambda s: ada_idx(s) + (0, 0)),
                   spec(kk, kk), spec(kk, 2 * nst), spec(kk, 2 * nst), spec(2, nst),
                   pl.BlockSpec((ct, bsz, CH * D), lambda s: (jnp.minimum(s, n_str - 1), 0, 0))],
        out_shape=[jax.ShapeDtypeStruct((depth, 3, 16, D), F32),
                   jax.ShapeDtypeStruct((n, kk, kk), BF16),
                   jax.ShapeDtypeStruct((n, kk, 2 * nst), BF16),
                   jax.ShapeDtypeStruct((n, kk, 2 * nst), BF16),
                   jax.ShapeDtypeStruct((n, 2, nst), F32),
                   jax.ShapeDtypeStruct(((t + tc) // CH, bsz, CH * D), F32)],
        scratch_shapes=[pltpu.VMEM((gb, kk, nst), F32)] * 2,
        compiler_params=_cparams(("arbitrary",)),
    )(cond, ada_w, ada_b.reshape(depth, 3, 1, D), lr, li, ls, btr, bti, cr, ci, d, x, pos, ctx)


def _even_in_kernel(x_ref, shift_ref, scale_ref, g_ref, w_ref, wcs_ref, ut_ref, xw_ref, z_ref,
                    wat_ref, wbz_ref, *scr, ncl):
    @pl.when(pl.program_id(0) == 0)
    def _():
        wat_ref[...] = w_ref[:, :S5_W].T.astype(BF16)
        wbz_ref[...] = w_ref[:, S5_W:].astype(BF16)

    x = x_ref[...]
    nc, nb, _ = x.shape
    rows = nc * nb
    gain = g_ref[...] * (1.0 + scale_ref[...])
    h = jnp.concatenate([_mod_norm(x[0:ncl], gain[0:nb][None], shift_ref[0:nb][None]),
                         _mod_norm(x[ncl:nc], gain[nb:nb + 1][None], shift_ref[nb:nb + 1][None])], axis=0)
    hb = h.reshape(rows, D).astype(BF16)
    pt = lax.dot_general(wat_ref[...], hb, (((1,), (1,)), ((), ())), preferred_element_type=F32)
    ut_ref[...] = pt.astype(BF16).reshape(S5_G, S5_H, rows)
    p = jnp.dot(hb, wbz_ref[...], preferred_element_type=F32)
    z_ref[...] = _silu(p[:, FN_W:]).astype(BF16)
    xw = jnp.dot(p[:, :FN_W].astype(BF16), wcs_ref[...], preferred_element_type=F32)
    for q in range(4):
        scr[q][...] = xw[:, q * LANE:(q + 1) * LANE]
    for q in range(4):
        part, half = divmod(q, 2)
        for bi in range(nb):
            piece = scr[q][pl.ds(bi, nc, stride=nb), :]
            lo = bi * FN_W + half * LANE
            xw_ref[part, :, lo:lo + LANE] = piece.astype(BF16)


def _even_in(xs, ncl, ada, l, g, w_in, layer, wcs):
    nc, bsz, _ = xs.shape
    rows = nc * bsz
    return pl.pallas_call(
        functools.partial(_even_in_kernel, ncl=ncl),
        grid=(CH,),
        in_specs=[pl.BlockSpec((nc, bsz, D), lambda i: (0, 0, i)),
                  pl.BlockSpec((None, None) + ada.shape[2:], lambda i: (l, 0, 0, 0)),
                  pl.BlockSpec((None, None) + ada.shape[2:], lambda i: (l, 1, 0, 0)),
                  pl.BlockSpec((1, D), lambda i: (0, 0)),
                  pl.BlockSpec((None, D, w_in.shape[2]), lambda i: (layer, 0, 0),
                               pipeline_mode=pl.Buffered(1)),
                  pl.BlockSpec((FN_W, 2 * FN_W), lambda i: (0, 0))],
        out_specs=[pl.BlockSpec((S5_G, S5_H, rows), lambda i: (0, i, 0)),
                   pl.BlockSpec((2, None, nc, bsz * FN_W), lambda i: (0, i, 0, 0)),
                   pl.BlockSpec((None, rows, MIX), lambda i: (i, 0, 0))],
        out_shape=[jax.ShapeDtypeStruct((S5_G, CH * S5_H, rows), BF16),
                   jax.ShapeDtypeStruct((2, CH, nc, bsz * FN_W), BF16),
                   jax.ShapeDtypeStruct((CH, rows, MIX), BF16)],
        scratch_shapes=[pltpu.VMEM((S5_W, D), BF16), pltpu.VMEM((D, FN_W + MIX), BF16)]
        + [pltpu.VMEM((rows, LANE), F32)] * 4,
        compiler_params=_cparams(("arbitrary",)),
    )(xs, ada, ada, g.reshape(1, D), w_in, wcs)


def _s5_kernel(ut_ref, mt_ref, be_ref, cp_ref, l16_ref, yt_ref,
               sre_ref, sim_ref, are_ref, aim_ref, bre_ref, bim_ref, *, bsz, ncl, ncc, gs):
    nl = bsz * ncl
    nst = 2 * S5_P
    for g in range(gs):
        st = lax.dot_general(ut_ref[g], be_ref[g], (((0,), (0,)), ((), ())), preferred_element_type=F32)
        sre_ref[g] = st[:, :nst]
        sim_ref[g] = st[:, nst:]

    lam = [l16_ref[g] for g in range(gs)]
    is_fwd = lax.broadcasted_iota(jnp.int32, (bsz, nst), 1) < S5_P

    def make_step(base, nchunk):
        def step(c, carry):
            rf = pl.ds(pl.multiple_of(base + c * bsz, bsz), bsz)
            rb = pl.ds(pl.multiple_of(base + (nchunk - 1 - c) * bsz, bsz), bsz)
            out = []
            for g in range(gs):
                sre, sim = carry[2 * g], carry[2 * g + 1]
                lre = lam[g][0:1, :]
                lim = lam[g][1:2, :]
                are_ref[g, rf, :] = sre
                aim_ref[g, rf, :] = sim
                bre_ref[g, rb, :] = sre
                bim_ref[g, rb, :] = sim
                in_re = jnp.where(is_fwd, sre_ref[g, rf, :], sre_ref[g, rb, :])
                in_im = jnp.where(is_fwd, sim_ref[g, rf, :], sim_ref[g, rb, :])
                out.append(lre * sre - lim * sim + in_re)
                out.append(lre * sim + lim * sre + in_im)
            return tuple(out)
        return step

    zero = jnp.zeros((bsz, nst), F32)
    carry = lax.fori_loop(0, ncc, make_step(nl, ncc), (zero,) * (2 * gs))
    lax.fori_loop(0, ncl, make_step(0, ncl), carry)

    nt = (((1,), (1,)), ((), ()))
    fwd_rows = lax.broadcasted_iota(jnp.int32, (bsz * (ncl + ncc), nst), 1) < S5_P
    for g in range(gs):
        s0 = jnp.concatenate([jnp.where(fwd_rows, are_ref[g], bre_ref[g]),
                              jnp.where(fwd_rows, aim_ref[g], bim_ref[g])], axis=1).astype(BF16)
        yt_ref[g] = (jnp.dot(mt_ref[g], ut_ref[g], preferred_element_type=F32)
                     + lax.dot_general(cp_ref[g], s0, nt, preferred_element_type=F32)).astype(BF16)


def _s5_mix(ut, ncl, mt, bend, cp, lam16, bsz, layer):
    rows = ut.shape[2]
    kk = CH * S5_H
    nst = 2 * S5_P
    gs = 6
    off = layer * (S5_G // gs)

    def gspec(r, c):
        return pl.BlockSpec((gs, r, c), lambda g: (g, 0, 0))

    def tspec(r, c):
        return pl.BlockSpec((gs, r, c), lambda g: (g + off, 0, 0))

    return pl.pallas_call(
        functools.partial(_s5_kernel, bsz=bsz, ncl=ncl, ncc=rows // bsz - ncl, gs=gs),
        grid=(S5_G // gs,),
        in_specs=[gspec(kk, rows), tspec(kk, kk), tspec(kk, 2 * nst), tspec(kk, 2 * nst), tspec(2, nst)],
        out_specs=gspec(kk, rows),
        out_shape=jax.ShapeDtypeStruct((S5_G, kk, rows), BF16),
        scratch_shapes=[pltpu.VMEM((gs, rows, nst), F32)] * 6,
        compiler_params=_cparams(("arbitrary",)),
    )(ut, mt, bend, cp, lam16)


def _fnet_weights_kernel(ccs_ref, fw_ref, o_ref):
    fw = fw_ref[...]
    ccs = ccs_ref[...]
    wc = jnp.dot(ccs[:, :FN_W], fw, precision=HI, preferred_element_type=F32)
    ws = jnp.dot(ccs[:, FN_W:], fw, precision=HI, preferred_element_type=F32)
    o_ref[...] = jnp.concatenate([wc, ws], axis=1).astype(BF16)


def _fnet_weights(fw_bd):
    return pl.pallas_call(
        _fnet_weights_kernel,
        out_shape=jax.ShapeDtypeStruct((FN_W, 2 * FN_W), BF16),
    )(_dft_channel_matrix(), fw_bd)


def _fnet_kernel(tab_ref, v_ref, jr_ref, o_ref, *, ipt):
    r = pl.program_id(1)
    tab = tab_ref[...].astype(BF16)
    v = v_ref[...]
    nc, n = v.shape[2], v.shape[3]
    t = CH * nc
    a = jnp.dot(tab[:, :t], v[0].reshape(t, n), preferred_element_type=F32)
    b = jnp.dot(tab[:, t:], v[1].reshape(t, n), preferred_element_type=F32)
    o_ref[pl.ds(r * ipt, ipt)] = (a - b).reshape(ipt, nc, n).astype(BF16)
    m = (a + b).astype(BF16)
    for k in range(ipt):
        i = r * ipt + k

        @pl.when(jnp.logical_and(i >= 1, i <= CH // 2 - 1))
        def _():
            o_ref[CH - i] = jnp.dot(jr_ref[...], m[k * nc:(k + 1) * nc],
                                    preferred_element_type=F32).astype(BF16)


def _dft_half_table(nc):
    t = nc * CH
    cols = (np.arange(nc)[None, :] * CH + np.arange(CH)[:, None]).reshape(-1)
    rows = (np.arange(nc)[None, :] * CH + np.arange(CH // 2 + 1)[:, None]).reshape(-1)
    prod = (rows[:, None].astype(np.int64) * cols[None, :].astype(np.int64)) % t
    ang = prod.astype(np.float64) * (2.0 * np.pi / t)
    scale = 1.0 / math.sqrt(t * FN_GW)
    return jnp.asarray(np.concatenate([np.cos(ang), np.sin(ang)], axis=1) * scale, dtype=F32)


def _dft_channel_matrix():
    c = np.arange(FN_GW)
    ang = (c[:, None] * c[None, :] % FN_GW).astype(np.float64) * (2.0 * np.pi / FN_GW)
    eye = np.eye(FN_G)
    return jnp.asarray(np.concatenate([np.kron(eye, np.cos(ang)), np.kron(eye, np.sin(ang))], axis=1),
                       dtype=F32)


def _fnet(xw, c0, nc, bsz):
    t = nc * CH
    nhb = CH // 2 + 1
    ipt = 3
    bpb = 4
    jr = jnp.asarray(np.eye(nc)[::-1], dtype=BF16)
    return pl.pallas_call(
        functools.partial(_fnet_kernel, ipt=ipt),
        grid=(bsz // bpb, nhb // ipt),
        in_specs=[pl.BlockSpec((ipt * nc, 2 * t), lambda b, r: (r, 0)),
                  pl.BlockSpec((2, CH, nc, bpb * FN_W), lambda b, r: (0, 0, c0 // nc, b)),
                  pl.BlockSpec((nc, nc), lambda b, r: (0, 0))],
        out_specs=pl.BlockSpec((CH, nc, bpb * FN_W), lambda b, r: (0, 0, b)),
        out_shape=jax.ShapeDtypeStruct((CH, nc, bsz * FN_W), BF16),
        compiler_params=_cparams(("arbitrary", "arbitrary")),
    )(_dft_half_table(nc), xw, jr)


def _even_out_kernel(*refs, ncl, with_ctx):
    it = iter(refs)
    yt_ref, ybl_ref = next(it), next(it)
    ybc_ref = next(it) if with_ctx else None
    z_ref, x_ref, gate_ref, gwf_ref, gb_ref, wof_ref, o_ref = (next(it), next(it), next(it), next(it),
                                                               next(it), next(it), next(it))
    scr = [next(it), next(it)]
    gw_ref, wo_ref = next(it), next(it)

    @pl.when(pl.program_id(0) == 0)
    def _():
        gw_ref[...] = (0.5 * gwf_ref[...]).astype(BF16)
        wo_ref[...] = wof_ref[...].astype(BF16)

    yt = yt_ref[...]
    rows = yt.shape[2]
    ya = _gelu_tanh(yt.astype(F32).reshape(S5_W, rows).T)
    half_g = jnp.dot(ya.astype(BF16), gw_ref[...], preferred_element_type=F32) + 0.5 * gb_ref[...]
    hy = 0.5 * ya
    ya = hy + hy * jnp.tanh(half_g)
    x = x_ref[...]
    nc, nb, _ = x.shape
    for half in range(2):
        for bi in range(nb):
            lo = bi * FN_W + half * LANE
            scr[half][pl.ds(bi, ncl, stride=nb), :] = ybl_ref[:, lo:lo + LANE].astype(F32)
            if with_ctx:
                scr[half][pl.ds(ncl * nb + bi, nc - ncl, stride=nb), :] = (
                    ybc_ref[:, lo:lo + LANE].astype(F32))
    yb = jnp.concatenate([scr[0][...], scr[1][...]], axis=1)
    sz = z_ref[...]
    ma = (ya * sz[:, :S5_W]).astype(BF16)
    mb = (yb * sz[:, S5_W:]).astype(BF16)
    out = (jnp.dot(ma, wo_ref[0:S5_W, :], preferred_element_type=F32)
           + jnp.dot(mb, wo_ref[S5_W:MIX, :], preferred_element_type=F32)).reshape(x.shape)
    o_ref[0:ncl] = x[0:ncl] + gate_ref[0:nb][None] * out[0:ncl]
    if with_ctx:
        o_ref[ncl:nc] = x[ncl:nc] + gate_ref[nb:nb + 1][None] * out[ncl:nc]


def _even_out(yt, ybl, ybc, z, xs, ncl, ada, l, glu_w, glu_b, w_out, layer):
    bsz = xs.shape[1]
    with_ctx = ybc is not None
    nc = xs.shape[0] if with_ctx else ncl
    rows = nc * bsz
    args = [yt, ybl]
    specs = [pl.BlockSpec((S5_G, S5_H, rows), lambda j: (0, j, 0)),
             pl.BlockSpec((None, ncl, bsz * FN_W), lambda j: (j, 0, 0))]
    if with_ctx:
        args.append(ybc)
        specs.append(pl.BlockSpec((None, nc - ncl, bsz * FN_W), lambda j: (j, 0, 0)))
    args += [z, xs, ada, glu_w, glu_b.reshape(1, S5_W), w_out]
    specs += [pl.BlockSpec((None, rows, MIX), lambda j: (j, 0, 0)),
              pl.BlockSpec((nc, bsz, D), lambda j: (0, 0, j)),
              pl.BlockSpec((None, None) + ada.shape[2:], lambda j: (l, 2, 0, 0)),
              pl.BlockSpec((None, S5_W, S5_W), lambda j: (layer, 0, 0), pipeline_mode=pl.Buffered(1)),
              pl.BlockSpec((1, S5_W), lambda j: (0, 0)),
              pl.BlockSpec((None, MIX, D), lambda j: (layer, 0, 0), pipeline_mode=pl.Buffered(1))]
    return pl.pallas_call(
        functools.partial(_even_out_kernel, ncl=ncl, with_ctx=with_ctx),
        grid=(CH,),
        in_specs=specs,
        out_specs=pl.BlockSpec((nc, bsz, D), lambda j: (0, 0, j)),
        out_shape=jax.ShapeDtypeStruct((nc, bsz, CH * D), F32),
        scratch_shapes=[pltpu.VMEM((rows, LANE), F32)] * 2 + [pltpu.VMEM((S5_W, S5_W), BF16),
                                                             pltpu.VMEM((MIX, D), BF16)],
        compiler_params=_cparams(("arbitrary",)),
    )(*args)


def _odd_kernel(*refs, ct, ntl, t_lat, t_ctx, final):
    it = iter(refs)
    xm_ref, xn_ref = next(it), next(it)
    shift_ref, scale_ref, gate_ref, g_ref = next(it), next(it), next(it), next(it)
    wif_ref, wof_ref, pwf_ref, ps_ref, cw_ref = next(it), next(it), next(it), next(it), next(it)
    fg_ref = next(it) if final else None
    o_ref = next(it)
    h_ref = next(it)
    pe_ref = next(it)
    tail_ref = next(it)
    wi_ref, wo_ref, pw_ref = next(it), next(it), next(it)
    fin_ref = next(it) if final else None

    ti = pl.program_id(0)
    is_ctx = ti >= ntl
    t0 = jnp.where(is_ctx, ti - ntl, ti) * (ct * CH)
    t_total = jnp.where(is_ctx, t_ctx, t_lat)

    @pl.when(ti == 0)
    def _():
        wi_ref[...] = wif_ref[...].astype(BF16)
        wo_ref[...] = wof_ref[...].astype(BF16)
        pw_ref[...] = pwf_ref[...].astype(BF16)
        tail_ref[...] = jnp.zeros(tail_ref.shape, F32)

    n1 = POOL_W + 2 * CONV_W
    nb = xm_ref.shape[1]
    tm = ct * CH
    ne = tm + 2 * HALO

    def rows_of(ref):
        return jnp.where(is_ctx, ref[nb:nb + 1], ref[0:nb])

    gain = g_ref[...] * (1.0 + rows_of(scale_ref))
    shift = rows_of(shift_ref)

    def hn(xv):
        return _mod_norm(xv, gain, shift)

    for i in range(CH):
        h_ref[0:ct, i] = hn(xm_ref[:, :, i * D:(i + 1) * D])
    for i in range(HALO):
        h_ref[ct, i] = hn(xn_ref[:, :, i * D:(i + 1) * D])[0]

    m0 = HALO * nb
    mr = tm * nb
    he = h_ref[...].reshape((ct + 1) * CH * nb, D)[0:mr + m0].astype(BF16)
    pe = jnp.dot(he, wi_ref[:, 0:n1], preferred_element_type=F32)
    te = t0 + lax.broadcasted_iota(jnp.int32, (mr + m0, 1), 0) // nb
    valid = te < t_total
    pe_ref[m0:, 0:POOL_W] = jnp.where(valid, pe[:, :POOL_W], 0.0)
    pe_ref[m0:, POOL_W:POOL_W + CONV_W] = jnp.where(
        valid, pe[:, POOL_W:POOL_W + CONV_W] * pe[:, POOL_W + CONV_W:], 0.0)
    pe_ref[0:m0, :] = jnp.where(t0 > 0, tail_ref[...], 0.0)

    p2 = jnp.dot(he[0:mr], wi_ref[:, n1:], preferred_element_type=F32)
    b_gate = p2[:, :CONV_W]
    sz = _silu(p2[:, CONV_W:])

    tpos = t0 + lax.broadcasted_iota(jnp.int32, (mr, 1), 0) // nb
    pooled = []
    for gi, w in enumerate(POOL_WINDOWS):
        c0 = gi * POOL_GW
        s = pe_ref[:, c0:c0 + POOL_GW]
        n = ne
        width = 1
        while width < w:
            s = s[0:(n - width) * nb] + s[width * nb:n * nb]
            n -= width
            width *= 2
        start = (HALO - w // 2) * nb
        total = s[start:start + mr]
        hi = jnp.minimum(tpos + w // 2, t_total)
        lo = jnp.maximum(tpos - w // 2, 0)
        inv_cnt = 1.0 / (hi - lo).astype(F32)
        centre = pe_ref[m0:m0 + mr, c0:c0 + POOL_GW]
        pg = total * inv_cnt - centre
        pooled.append(jnp.dot(pg.astype(BF16), pw_ref[gi], preferred_element_type=F32))
    y_c = jnp.concatenate(pooled, axis=1) * ps_ref[...]

    cwt = cw_ref[...]
    vm = pe_ref[m0 - nb:m0 - nb + mr, POOL_W:POOL_W + CONV_W]
    v0 = pe_ref[m0:m0 + mr, POOL_W:POOL_W + CONV_W]
    vp = pe_ref[m0 + nb:m0 + nb + mr, POOL_W:POOL_W + CONV_W]
    y_d = b_gate * (vm * cwt[0:1, :] + v0 * cwt[1:2, :] + vp * cwt[2:3, :])

    y = (jnp.concatenate([y_c, y_d], axis=1) * sz).astype(BF16)
    out = jnp.dot(y, wo_ref[...], preferred_element_type=F32)
    go = (rows_of(gate_ref) * out.reshape(tm, nb, D)).reshape(ct, CH, nb, D)
    for i in range(CH):
        xo = xm_ref[:, :, i * D:(i + 1) * D] + go[:, i]
        if not final:
            o_ref[:, :, i * D:(i + 1) * D] = xo
            continue
        ms = jnp.mean(xo * xo, axis=-1, keepdims=True)
        fin_ref[:, :, i * D:(i + 1) * D] = xo * lax.rsqrt(ms + EPS) * fg_ref[...]
    tail_ref[...] = pe_ref[mr:mr + m0, :]
    if final:
        o_ref[...] = pltpu.einshape("cb(id)->b(ci)d", fin_ref[...], i=CH)


def _odd_layer(xs, ncl, with_ctx, ada, l, g, w_in, w_out, pool_w, layer, pool_scale, conv_w, final_g):
    nc_all, bsz, _ = xs.shape
    ct = 8
    ntl = ncl // ct
    nc = nc_all if with_ctx else ncl
    final = final_g is not None
    args = [xs, xs, ada, ada, ada, g.reshape(1, D), w_in, w_out, pool_w,
            pool_scale.reshape(1, POOL_W), conv_w]

    def vec(j):
        return pl.BlockSpec((None, None) + ada.shape[2:], lambda i: (l, j, 0, 0))

    def const(*shape):
        return pl.BlockSpec(shape, lambda i: (0,) * len(shape), pipeline_mode=pl.Buffered(1))

    def stacked(*shape):
        return pl.BlockSpec((None,) + shape, lambda i: (layer,) + (0,) * len(shape),
                            pipeline_mode=pl.Buffered(1))

    specs = [pl.BlockSpec((ct, bsz, CH * D), lambda i: (i, 0, 0)),
             pl.BlockSpec((1, bsz, CH * D), lambda i: (jnp.minimum((i + 1) * ct, nc_all - 1), 0, 0)),
             vec(0), vec(1), vec(2),
             const(1, D), stacked(*w_in.shape[1:]), stacked(MIX, D),
             stacked(len(POOL_WINDOWS), POOL_GW, POOL_GW), const(1, POOL_W), const(3, CONV_W)]
    scratch = [pltpu.VMEM((ct + 1, CH, bsz, D), F32),
               pltpu.VMEM(((ct + 1) * CH * bsz, POOL_W + CONV_W), F32),
               pltpu.VMEM((HALO * bsz, POOL_W + CONV_W), F32),
               pltpu.VMEM(w_in.shape[1:], BF16), pltpu.VMEM((MIX, D), BF16),
               pltpu.VMEM((len(POOL_WINDOWS), POOL_GW, POOL_GW), BF16)]
    if final:
        args.append(final_g.reshape(1, D))
        specs.append(const(1, D))
        scratch.append(pltpu.VMEM((ct, bsz, CH * D), F32))
        out_spec = pl.BlockSpec((bsz, ct * CH, D), lambda i: (0, i, 0))
        out_shape = jax.ShapeDtypeStruct((bsz, nc * CH, D), F32)
    else:
        out_spec = pl.BlockSpec((ct, bsz, CH * D), lambda i: (i, 0, 0))
        out_shape = jax.ShapeDtypeStruct((nc, bsz, CH * D), F32)
    return pl.pallas_call(
        functools.partial(_odd_kernel, ct=ct, ntl=ntl, t_lat=ncl * CH, t_ctx=(nc_all - ncl) * CH,
                          final=final),
        grid=(nc // ct,),
        in_specs=specs,
        out_specs=out_spec,
        out_shape=out_shape,
        scratch_shapes=scratch,
        compiler_params=_cparams(("arbitrary",)),
    )(*args)


def _sincos_table(n_tok, dim):
    rows = n_tok // GRID_W
    rr, cc = np.meshgrid(np.arange(rows, dtype=np.float64), np.arange(GRID_W, dtype=np.float64),
                         indexing='ij')
    rr = rr.reshape(-1, 1)
    cc = cc.reshape(-1, 1)
    quarter = dim // 4
    omega = POS_BASE ** (-np.arange(quarter, dtype=np.float64) / quarter)
    tab = np.concatenate([np.sin(rr * omega), np.cos(rr * omega), np.sin(cc * omega), np.cos(cc * omega)],
                         axis=-1)
    return jnp.asarray(tab, dtype=F32)


def _block_diag(w):
    g, c, _ = w.shape
    eye = jnp.eye(g, dtype=w.dtype)
    return (eye[:, None, :, None] * w[:, :, None, :]).reshape(g * c, g * c)


def kernel(x, c, ctx, c_ctx, norm_g, ada_w, ada_b, even_w_in, even_w_out, s5_lam_re, s5_lam_im, s5_log_step, s5_b_re, s5_b_im, s5_c_re, s5_c_im, s5_d, s5_glu_w, s5_glu_b, fnet_w, odd_w_in, odd_w_out, pool_w, pool_scale, conv_w, final_g):
    bsz, n_tok, _ = x.shape
    depth = norm_g.shape[0]
    ncl = n_tok // CH
    ncc = ctx.shape[1] // CH

    cond = jnp.concatenate([c, jnp.broadcast_to(c_ctx[None], (16 - bsz, D))], axis=0)
    ada, mt, bend, cp, lam16, xs = _prologue(cond, ada_w, ada_b, s5_lam_re, s5_lam_im, s5_log_step, s5_b_re,
                                             s5_b_im, s5_c_re, s5_c_im, s5_d, x, _sincos_table(n_tok, D), ctx)

    need_ctx = [any(j % 2 == 0 for j in range(l + 1, depth)) for l in range(depth)]

    for l in range(depth):
        i = l // 2
        last = l == depth - 1
        if l % 2 == 0:
            wcs = _fnet_weights(_block_diag(fnet_w[i]))
            ut, xw, z = _even_in(xs, ncl, ada, l, norm_g[l], even_w_in, i, wcs)
            yt = _s5_mix(ut, ncl, mt, bend, cp, lam16, bsz, i)
            ybl = _fnet(xw, 0, ncl, bsz)
            ybc = _fnet(xw, ncl, ncc, bsz) if need_ctx[l] else None
            xs = _even_out(yt, ybl, ybc, z, xs, ncl, ada, l, s5_glu_w, s5_glu_b[i], even_w_out, i)
        else:
            xs = _odd_layer(xs, ncl, need_ctx[l], ada, l, norm_g[l], odd_w_in, odd_w_out, pool_w, i,
                            pool_scale[i], conv_w[i], final_g if last else None)
    if depth % 2 == 1:
        raise NotImplementedError("final norm and (B, T, D) order are produced by the last (odd) layer")
    return xs
```

```python
import functools
import math

import numpy as np
import jax
import jax.numpy as jnp
from jax import lax
from jax.experimental import pallas as pl
from jax.experimental.pallas import tpu as pltpu

D = 1024
MIX = 1024
S5_W = 768
FN_W = 256
S5_H = 16
S5_G = 48
S5_P = 64
FN_G = 4
FN_GW = 64
POOL_W = 512
CONV_W = 512
POOL_WINDOWS = (2, 4, 8, 16)
POOL_GW = 128
GRID_W = 64
EPS = 1e-6
POS_BASE = 10000.0
CH = 16
HALO = 8
LANE = 128
VMEM_LIMIT = 56 * 1024 * 1024

F32 = jnp.float32
BF16 = jnp.bfloat16
HI = lax.Precision.HIGHEST


def _cparams(sem):
    return pltpu.CompilerParams(dimension_semantics=sem, vmem_limit_bytes=VMEM_LIMIT)


def _silu(v):
    h = 0.5 * v
    return h + h * jnp.tanh(h)


def _gelu_tanh(v):
    c = math.sqrt(2.0 / math.pi)
    h = 0.5 * v
    return h + h * jnp.tanh(v * (c + (c * 0.044715) * (v * v)))


def _mod_norm(x, gain, shift):
    ms = jnp.mean(x * x, axis=-1, keepdims=True)
    return x * lax.rsqrt(ms + EPS) * gain + shift


def _split3(v):
    hi = v.astype(BF16)
    lo = (v - hi.astype(F32)).astype(BF16)
    return hi, lo


def _ada_item(c_ref, w_ref, b_ref, ada_ref):
    s_hi, s_lo = _split3(_silu(c_ref[...]))
    w_hi, w_lo = _split3(w_ref[...])
    ada_ref[...] = (jnp.dot(s_hi, w_hi, preferred_element_type=F32)
                    + jnp.dot(s_hi, w_lo, preferred_element_type=F32)
                    + jnp.dot(s_lo, w_hi, preferred_element_type=F32) + b_ref[...])


def _stream_item(x_ref, pos_ref, c_ref, o_ref, step, ntl):
    @pl.when(step < ntl)
    def _():
        o_ref[...] = pltpu.einshape("b(ci)d->cb(id)", x_ref[...] + pos_ref[...][None], i=CH)

    @pl.when(step >= ntl)
    def _():
        o_ref[...] = pltpu.einshape("b(ci)d->cb(id)", c_ref[...], i=CH)


def _prologue_kernel(c_ref, w_ref, b_ref, lr_ref, li_ref, ls_ref, btr_ref, bti_ref, cr_ref, ci_ref, d_ref,
                     x_ref, pos_ref, ctx_ref, ada_ref, mt_ref, be_ref, cp_ref, l16_ref, xs_ref,
                     ere_ref, eim_ref, *, gb, n_ada, n_tab, ntl):
    step = pl.program_id(0)

    @pl.when(step < n_ada)
    def _():
        _ada_item(c_ref, w_ref, b_ref, ada_ref)

    @pl.when(step < n_tab)
    def _():
        _tables_item(lr_ref, li_ref, ls_ref, btr_ref, bti_ref, cr_ref, ci_ref, d_ref,
                     mt_ref, be_ref, cp_ref, l16_ref, ere_ref, eim_ref, gb)

    _stream_item(x_ref, pos_ref, ctx_ref, xs_ref, step, ntl)


def _tables_item(lr_ref, li_ref, ls_ref, btr_ref, bti_ref, cr_ref, ci_ref, d_ref,
                 mt_ref, be_ref, cp_ref, l16_ref, ere_ref, eim_ref, gb):
    nst = 2 * S5_P
    kk = CH * S5_H
    step = jnp.exp(ls_ref[...])
    lr = lr_ref[...]
    li = li_ref[...]
    a = lr * step
    b = li * step

    mag = jnp.exp(a)
    l1re = mag * jnp.cos(b)
    l1im = mag * jnp.sin(b)
    squares = [(l1re, l1im)]
    for _ in range(4):
        sr, si = squares[-1]
        squares.append((sr * sr - si * si, 2.0 * (sr * si)))

    def powers(expo):
        pr = jnp.ones((gb,) + expo.shape[1:], F32)
        pi = jnp.zeros((gb,) + expo.shape[1:], F32)
        for bit, (sr, si) in enumerate(squares):
            on = ((expo >> bit) & 1) == 1
            pr, pi = jnp.where(on, pr * sr - pi * si, pr), jnp.where(on, pr * si + pi * sr, pi)
        return pr, pi

    row = lax.broadcasted_iota(jnp.int32, (1, CH, nst), 1)
    fwd = lax.broadcasted_iota(jnp.int32, (1, CH, nst), 2) < S5_P

    n_re = l1re - 1.0
    den = lr * lr + li * li
    co_re = (n_re * lr + l1im * li) / den
    co_im = (l1im * lr - n_re * li) / den
    btr = btr_ref[...]
    bti = bti_ref[...]
    bb_re = co_re * btr - co_im * bti
    bb_im = co_re * bti + co_im * btr

    pe_re, pe_im = powers(jnp.where(fwd, (CH - 1) - row, row))
    for l in range(CH):
        pr = pe_re[:, l:l + 1, :]
        pi = pe_im[:, l:l + 1, :]
        ere_ref[:, l * S5_H:(l + 1) * S5_H, :] = pr * bb_re - pi * bb_im
        eim_ref[:, l * S5_H:(l + 1) * S5_H, :] = pr * bb_im + pi * bb_re

    cr = cr_ref[...]
    ci = ci_ref[...]
    pc_re, pc_im = powers(jnp.where(fwd, row + 1, CH - row))
    for j in range(CH):
        pr = pc_re[:, j:j + 1, :]
        pi = pc_im[:, j:j + 1, :]
        w_re = cr * pr - ci * pi
        w_im = cr * pi + ci * pr
        cp_ref[:, j * S5_H:(j + 1) * S5_H, :] = jnp.concatenate([w_re, -w_im], axis=2).astype(BF16)

    l16_ref[...] = jnp.concatenate(squares[4], axis=1)

    fwd2 = lax.broadcasted_iota(jnp.int32, (S5_H, nst), 1) < S5_P
    lane = lax.broadcasted_iota(jnp.int32, (S5_H, kk), 1)
    iblk = lane // S5_H
    hrow = lax.broadcasted_iota(jnp.int32, (S5_H, kk), 0)
    nt = (((1,), (1,)), ((), ()))
    for g in range(gb):
        ere = ere_ref[g]
        eim = eim_ref[g]
        e2 = jnp.concatenate([ere, eim], axis=1)
        be_ref[g] = e2.astype(BF16)
        e_hi, e_lo = _split3(e2)
        rhs = jnp.concatenate([e_hi, e_hi, e_lo], axis=1)
        crg = cr[g]
        cig = ci[g]

        c2 = jnp.concatenate([
            jnp.concatenate([jnp.where(fwd2, crg, 0.0), jnp.where(fwd2, -cig, 0.0)], axis=1),
            jnp.concatenate([jnp.where(fwd2, 0.0, crg), jnp.where(fwd2, 0.0, -cig)], axis=1)], axis=0)
        c_hi, c_lo = _split3(c2)
        lhs = jnp.concatenate([c_hi, c_lo, c_hi], axis=1)
        kfb = lax.dot_general(lhs, rhs, nt, preferred_element_type=F32)
        kf = kfb[0:S5_H]
        kb = kfb[S5_H:2 * S5_H]
        dg = d_ref[g]
        for j in range(CH):
            sf = (kk - (CH - 1 - j) * S5_H) % kk
            rf = pltpu.roll(kf, sf, 1) if sf else kf
            rb = pltpu.roll(kb, j * S5_H, 1) if j else kb
            blk = (jnp.where(iblk <= j, rf, 0.0) + jnp.where(iblk >= j, rb, 0.0)
                   + jnp.where(lane == j * S5_H + hrow, dg, 0.0))
            mt_ref[g, j * S5_H:(j + 1) * S5_H, :] = blk.astype(BF16)


def _prologue(cond, ada_w, ada_b, lam_re, lam_im, log_step, b_re, b_im, c_re, c_im, d_skip, x, pos, ctx):
    n = lam_re.shape[0] * S5_G
    gb = 8
    nst = 2 * S5_P
    kk = CH * S5_H

    def fb(v):
        return jnp.concatenate([v[:, 0], v[:, 1]], axis=-1).reshape(n, v.shape[3], nst)

    lr = fb(lam_re[:, :, :, None, :])
    li = fb(lam_im[:, :, :, None, :])
    ls = fb(jnp.broadcast_to(log_step[:, :, :, None, None], log_step.shape + (1, S5_P)))
    btr = fb(jnp.swapaxes(b_re, -1, -2))
    bti = fb(jnp.swapaxes(b_im, -1, -2))
    cr = fb(c_re)
    ci = fb(c_im)
    d = d_skip.reshape(n, S5_H, 1)

    depth = ada_w.shape[0]
    n_ada = depth * 3
    n_tab = n // gb
    bsz, t, _ = x.shape
    tc = ctx.shape[1]
    ct = 8
    tm = ct * CH
    ntl = t // tm
    n_str = (t + tc) // tm

    def spec(r, c):
        return pl.BlockSpec((gb, r, c), lambda s: (jnp.minimum(s, n_tab - 1), 0, 0))

    def ada_idx(s):
        s = jnp.minimum(s, n_ada - 1)
        return s // 3, s % 3

    def lat_tile(s):
        return jnp.minimum(s, ntl - 1)

    return pl.pallas_call(
        functools.partial(_prologue_kernel, gb=gb, n_ada=n_ada, n_tab=n_tab, ntl=ntl),
        grid=(max(n_ada, n_tab, n_str),),
        in_specs=[pl.BlockSpec((16, D), lambda s: (0, 0)),
                  pl.BlockSpec((None, D, D), lambda s: (ada_idx(s)[0], 0, ada_idx(s)[1])),
                  pl.BlockSpec((None, None, 1, D), lambda s: ada_idx(s) + (0, 0)),
                  spec(1, nst), spec(1, nst), spec(1, nst), spec(S5_H, nst), spec(S5_H, nst),
                  spec(S5_H, nst), spec(S5_H, nst), spec(S5_H, 1),
                  pl.BlockSpec((bsz, tm, D), lambda s: (0, lat_tile(s), 0)),
                  pl.BlockSpec((tm, D), lambda s: (lat_tile(s), 0)),
                  pl.BlockSpec((bsz, tm, D), lambda s: (0, jnp.clip(s - ntl, 0, tc // tm - 1), 0),
                               pipeline_mode=pl.Buffered(1))],
        out_specs=[pl.BlockSpec((None, None, 16, D), lambda s: ada_idx(s) + (0, 0)),
                   spec(kk, kk), spec(kk, 2 * nst), spec(kk, 2 * nst), spec(2, nst),
                   pl.BlockSpec((ct, bsz, CH * D), lambda s: (jnp.minimum(s, n_str - 1), 0, 0))],
        out_shape=[jax.ShapeDtypeStruct((depth, 3, 16, D), F32),
                   jax.ShapeDtypeStruct((n, kk, kk), BF16),
                   jax.ShapeDtypeStruct((n, kk, 2 * nst), BF16),
                   jax.ShapeDtypeStruct((n, kk, 2 * nst), BF16),
                   jax.ShapeDtypeStruct((n, 2, nst), F32),
                   jax.ShapeDtypeStruct(((t + tc) // CH, bsz, CH * D), F32)],
        scratch_shapes=[pltpu.VMEM((gb, kk, nst), F32)] * 2,
        compiler_params=_cparams(("arbitrary",)),
    )(cond, ada_w, ada_b.reshape(depth, 3, 1, D), lr, li, ls, btr, bti, cr, ci, d, x, pos, ctx)


def _even_in_kernel(x_ref, shift_ref, scale_ref, g_ref, w_ref, wcs_ref, ut_ref, xw_ref, z_ref,
                    wat_ref, wbz_ref, *scr, ncl):
    @pl.when(pl.program_id(0) == 0)
    def _():
        wat_ref[...] = w_ref[:, :S5_W].T.astype(BF16)
        wbz_ref[...] = w_ref[:, S5_W:].astype(BF16)

    x = x_ref[...]
    nc, nb, _ = x.shape
    rows = nc * nb
    gain = g_ref[...] * (1.0 + scale_ref[...])
    h = jnp.concatenate([_mod_norm(x[0:ncl], gain[0:nb][None], shift_ref[0:nb][None]),
                         _mod_norm(x[ncl:nc], gain[nb:nb + 1][None], shift_ref[nb:nb + 1][None])], axis=0)
    hb = h.reshape(rows, D).astype(BF16)
    pt = lax.dot_general(wat_ref[...], hb, (((1,), (1,)), ((), ())), preferred_element_type=F32)
    ut_ref[...] = pt.astype(BF16).reshape(S5_G, S5_H, rows)
    p = jnp.dot(hb, wbz_ref[...], preferred_element_type=F32)
    z_ref[...] = _silu(p[:, FN_W:]).astype(BF16)
    xw = jnp.dot(p[:, :FN_W].astype(BF16), wcs_ref[...], preferred_element_type=F32)
    for q in range(4):
        scr[q][...] = xw[:, q * LANE:(q + 1) * LANE]
    for q in range(4):
        part, half = divmod(q, 2)
        for bi in range(nb):
            piece = scr[q][pl.ds(bi, nc, stride=nb), :]
            lo = bi * FN_W + half * LANE
            xw_ref[part, :, lo:lo + LANE] = piece.astype(BF16)


def _even_in(xs, ncl, ada, l, g, w_in, layer, wcs):
    nc, bsz, _ = xs.shape
    rows = nc * bsz
    return pl.pallas_call(
        functools.partial(_even_in_kernel, ncl=ncl),
        grid=(CH,),
        in_specs=[pl.BlockSpec((nc, bsz, D), lambda i: (0, 0, i)),
                  pl.BlockSpec((None, None) + ada.shape[2:], lambda i: (l, 0, 0, 0)),
                  pl.BlockSpec((None, None) + ada.shape[2:], lambda i: (l, 1, 0, 0)),
                  pl.BlockSpec((None, 1, D), lambda i: (l, 0, 0)),
                  pl.BlockSpec((None, D, w_in.shape[2]), lambda i: (layer, 0, 0),
                               pipeline_mode=pl.Buffered(1)),
                  pl.BlockSpec((FN_W, 2 * FN_W), lambda i: (0, 0))],
        out_specs=[pl.BlockSpec((S5_G, S5_H, rows), lambda i: (0, i, 0)),
                   pl.BlockSpec((2, None, nc, bsz * FN_W), lambda i: (0, i, 0, 0)),
                   pl.BlockSpec((None, rows, MIX), lambda i: (i, 0, 0))],
        out_shape=[jax.ShapeDtypeStruct((S5_G, CH * S5_H, rows), BF16),
                   jax.ShapeDtypeStruct((2, CH, nc, bsz * FN_W), BF16),
                   jax.ShapeDtypeStruct((CH, rows, MIX), BF16)],
        scratch_shapes=[pltpu.VMEM((S5_W, D), BF16), pltpu.VMEM((D, FN_W + MIX), BF16)]
        + [pltpu.VMEM((rows, LANE), F32)] * 4,
        compiler_params=_cparams(("arbitrary",)),
    )(xs, ada, ada, g, w_in, wcs)


def _s5_kernel(ut_ref, mt_ref, be_ref, cp_ref, l16_ref, yt_ref,
               sre_ref, sim_ref, are_ref, aim_ref, bre_ref, bim_ref, *, bsz, ncl, ncc, gs):
    nl = bsz * ncl
    nst = 2 * S5_P
    for g in range(gs):
        st = lax.dot_general(ut_ref[g], be_ref[g], (((0,), (0,)), ((), ())), preferred_element_type=F32)
        sre_ref[g] = st[:, :nst]
        sim_ref[g] = st[:, nst:]

    lam = [l16_ref[g] for g in range(gs)]
    is_fwd = lax.broadcasted_iota(jnp.int32, (bsz, nst), 1) < S5_P

    def make_step(base, nchunk):
        def step(c, carry):
            rf = pl.ds(pl.multiple_of(base + c * bsz, bsz), bsz)
            rb = pl.ds(pl.multiple_of(base + (nchunk - 1 - c) * bsz, bsz), bsz)
            out = []
            for g in range(gs):
                sre, sim = carry[2 * g], carry[2 * g + 1]
                lre = lam[g][0:1, :]
                lim = lam[g][1:2, :]
                are_ref[g, rf, :] = sre
                aim_ref[g, rf, :] = sim
                bre_ref[g, rb, :] = sre
                bim_ref[g, rb, :] = sim
                in_re = jnp.where(is_fwd, sre_ref[g, rf, :], sre_ref[g, rb, :])
                in_im = jnp.where(is_fwd, sim_ref[g, rf, :], sim_ref[g, rb, :])
                out.append(lre * sre - lim * sim + in_re)
                out.append(lre * sim + lim * sre + in_im)
            return tuple(out)
        return step

    zero = jnp.zeros((bsz, nst), F32)
    carry = lax.fori_loop(0, ncc, make_step(nl, ncc), (zero,) * (2 * gs))
    lax.fori_loop(0, ncl, make_step(0, ncl), carry)

    nt = (((1,), (1,)), ((), ()))
    fwd_rows = lax.broadcasted_iota(jnp.int32, (bsz * (ncl + ncc), nst), 1) < S5_P
    for g in range(gs):
        s0 = jnp.concatenate([jnp.where(fwd_rows, are_ref[g], bre_ref[g]),
                              jnp.where(fwd_rows, aim_ref[g], bim_ref[g])], axis=1).astype(BF16)
        yt_ref[g] = (jnp.dot(mt_ref[g], ut_ref[g], preferred_element_type=F32)
                     + lax.dot_general(cp_ref[g], s0, nt, preferred_element_type=F32)).astype(BF16)


def _s5_mix(ut, ncl, mt, bend, cp, lam16, bsz, layer):
    rows = ut.shape[2]
    kk = CH * S5_H
    nst = 2 * S5_P
    gs = 6
    off = layer * (S5_G // gs)

    def gspec(r, c):
        return pl.BlockSpec((gs, r, c), lambda g: (g, 0, 0))

    def tspec(r, c):
        return pl.BlockSpec((gs, r, c), lambda g: (g + off, 0, 0))

    return pl.pallas_call(
        functools.partial(_s5_kernel, bsz=bsz, ncl=ncl, ncc=rows // bsz - ncl, gs=gs),
        grid=(S5_G // gs,),
        in_specs=[gspec(kk, rows), tspec(kk, kk), tspec(kk, 2 * nst), tspec(kk, 2 * nst), tspec(2, nst)],
        out_specs=gspec(kk, rows),
        out_shape=jax.ShapeDtypeStruct((S5_G, kk, rows), BF16),
        scratch_shapes=[pltpu.VMEM((gs, rows, nst), F32)] * 6,
        compiler_params=_cparams(("arbitrary",)),
    )(ut, mt, bend, cp, lam16)


def _fnet_weights_kernel(ccs_ref, fw_ref, o_ref):
    fw = fw_ref[...]
    ccs = ccs_ref[...]
    wc = jnp.dot(ccs[:, :FN_W], fw, precision=HI, preferred_element_type=F32)
    ws = jnp.dot(ccs[:, FN_W:], fw, precision=HI, preferred_element_type=F32)
    o_ref[...] = jnp.concatenate([wc, ws], axis=1).astype(BF16)


def _fnet_weights(fw_bd):
    return pl.pallas_call(
        _fnet_weights_kernel,
        out_shape=jax.ShapeDtypeStruct((FN_W, 2 * FN_W), BF16),
    )(_dft_channel_matrix(), fw_bd)


def _fnet_kernel(tab_ref, v_ref, jr_ref, o_ref, *, ipt):
    r = pl.program_id(1)
    tab = tab_ref[...].astype(BF16)
    v = v_ref[...]
    nc, n = v.shape[2], v.shape[3]
    t = CH * nc
    a = jnp.dot(tab[:, :t], v[0].reshape(t, n), preferred_element_type=F32)
    b = jnp.dot(tab[:, t:], v[1].reshape(t, n), preferred_element_type=F32)
    o_ref[pl.ds(r * ipt, ipt)] = (a - b).reshape(ipt, nc, n).astype(BF16)
    m = (a + b).astype(BF16)
    for k in range(ipt):
        i = r * ipt + k

        @pl.when(jnp.logical_and(i >= 1, i <= CH // 2 - 1))
        def _():
            o_ref[CH - i] = jnp.dot(jr_ref[...], m[k * nc:(k + 1) * nc],
                                    preferred_element_type=F32).astype(BF16)


def _dft_half_table(nc):
    t = nc * CH
    cols = (np.arange(nc)[None, :] * CH + np.arange(CH)[:, None]).reshape(-1)
    rows = (np.arange(nc)[None, :] * CH + np.arange(CH // 2 + 1)[:, None]).reshape(-1)
    prod = (rows[:, None].astype(np.int64) * cols[None, :].astype(np.int64)) % t
    ang = prod.astype(np.float64) * (2.0 * np.pi / t)
    scale = 1.0 / math.sqrt(t * FN_GW)
    return jnp.asarray(np.concatenate([np.cos(ang), np.sin(ang)], axis=1) * scale, dtype=F32)


def _dft_channel_matrix():
    c = np.arange(FN_GW)
    ang = (c[:, None] * c[None, :] % FN_GW).astype(np.float64) * (2.0 * np.pi / FN_GW)
    eye = np.eye(FN_G)
    return jnp.asarray(np.concatenate([np.kron(eye, np.cos(ang)), np.kron(eye, np.sin(ang))], axis=1),
                       dtype=F32)


def _fnet(xw, c0, nc, bsz):
    t = nc * CH
    nhb = CH // 2 + 1
    ipt = 3
    bpb = 4
    jr = jnp.asarray(np.eye(nc)[::-1], dtype=BF16)
    return pl.pallas_call(
        functools.partial(_fnet_kernel, ipt=ipt),
        grid=(bsz // bpb, nhb // ipt),
        in_specs=[pl.BlockSpec((ipt * nc, 2 * t), lambda b, r: (r, 0)),
                  pl.BlockSpec((2, CH, nc, bpb * FN_W), lambda b, r: (0, 0, c0 // nc, b)),
                  pl.BlockSpec((nc, nc), lambda b, r: (0, 0))],
        out_specs=pl.BlockSpec((CH, nc, bpb * FN_W), lambda b, r: (0, 0, b)),
        out_shape=jax.ShapeDtypeStruct((CH, nc, bsz * FN_W), BF16),
        compiler_params=_cparams(("arbitrary", "arbitrary")),
    )(_dft_half_table(nc), xw, jr)


def _even_out_kernel(*refs, ncl, with_ctx):
    it = iter(refs)
    yt_ref, ybl_ref = next(it), next(it)
    ybc_ref = next(it) if with_ctx else None
    z_ref, x_ref, gate_ref, gwf_ref, gb_ref, wof_ref, o_ref = (next(it), next(it), next(it), next(it),
                                                               next(it), next(it), next(it))
    scr = [next(it), next(it)]
    gw_ref, wo_ref = next(it), next(it)

    @pl.when(pl.program_id(0) == 0)
    def _():
        gw_ref[...] = (0.5 * gwf_ref[...]).astype(BF16)
        wo_ref[...] = wof_ref[...].astype(BF16)

    yt = yt_ref[...]
    rows = yt.shape[2]
    ya = _gelu_tanh(yt.astype(F32).reshape(S5_W, rows).T)
    half_g = jnp.dot(ya.astype(BF16), gw_ref[...], preferred_element_type=F32) + 0.5 * gb_ref[...]
    hy = 0.5 * ya
    ya = hy + hy * jnp.tanh(half_g)
    x = x_ref[...]
    nc, nb, _ = x.shape
    for half in range(2):
        for bi in range(nb):
            lo = bi * FN_W + half * LANE
            scr[half][pl.ds(bi, ncl, stride=nb), :] = ybl_ref[:, lo:lo + LANE].astype(F32)
            if with_ctx:
                scr[half][pl.ds(ncl * nb + bi, nc - ncl, stride=nb), :] = (
                    ybc_ref[:, lo:lo + LANE].astype(F32))
    yb = jnp.concatenate([scr[0][...], scr[1][...]], axis=1)
    sz = z_ref[...]
    ma = (ya * sz[:, :S5_W]).astype(BF16)
    mb = (yb * sz[:, S5_W:]).astype(BF16)
    out = (jnp.dot(ma, wo_ref[0:S5_W, :], preferred_element_type=F32)
           + jnp.dot(mb, wo_ref[S5_W:MIX, :], preferred_element_type=F32)).reshape(x.shape)
    o_ref[0:ncl] = x[0:ncl] + gate_ref[0:nb][None] * out[0:ncl]
    if with_ctx:
        o_ref[ncl:nc] = x[ncl:nc] + gate_ref[nb:nb + 1][None] * out[ncl:nc]


def _even_out(yt, ybl, ybc, z, xs, ncl, ada, l, glu_w, glu_b, w_out, layer):
    bsz = xs.shape[1]
    with_ctx = ybc is not None
    nc = xs.shape[0] if with_ctx else ncl
    rows = nc * bsz
    args = [yt, ybl]
    specs = [pl.BlockSpec((S5_G, S5_H, rows), lambda j: (0, j, 0)),
             pl.BlockSpec((None, ncl, bsz * FN_W), lambda j: (j, 0, 0))]
    if with_ctx:
        args.append(ybc)
        specs.append(pl.BlockSpec((None, nc - ncl, bsz * FN_W), lambda j: (j, 0, 0)))
    args += [z, xs, ada, glu_w, glu_b, w_out]
    specs += [pl.BlockSpec((None, rows, MIX), lambda j: (j, 0, 0)),
              pl.BlockSpec((nc, bsz, D), lambda j: (0, 0, j)),
              pl.BlockSpec((None, None) + ada.shape[2:], lambda j: (l, 2, 0, 0)),
              pl.BlockSpec((None, S5_W, S5_W), lambda j: (layer, 0, 0), pipeline_mode=pl.Buffered(1)),
              pl.BlockSpec((None, 1, S5_W), lambda j: (layer, 0, 0)),
              pl.BlockSpec((None, MIX, D), lambda j: (layer, 0, 0), pipeline_mode=pl.Buffered(1))]
    return pl.pallas_call(
        functools.partial(_even_out_kernel, ncl=ncl, with_ctx=with_ctx),
        grid=(CH,),
        in_specs=specs,
        out_specs=pl.BlockSpec((nc, bsz, D), lambda j: (0, 0, j)),
        out_shape=jax.ShapeDtypeStruct((nc, bsz, CH * D), F32),
        scratch_shapes=[pltpu.VMEM((rows, LANE), F32)] * 2 + [pltpu.VMEM((S5_W, S5_W), BF16),
                                                             pltpu.VMEM((MIX, D), BF16)],
        compiler_params=_cparams(("arbitrary",)),
    )(*args)


def _odd_kernel(*refs, ct, ntl, t_lat, t_ctx, final):
    it = iter(refs)
    xm_ref, xn_ref = next(it), next(it)
    shift_ref, scale_ref, gate_ref, g_ref = next(it), next(it), next(it), next(it)
    wif_ref, wof_ref, pwf_ref, ps_ref, cw_ref = next(it), next(it), next(it), next(it), next(it)
    fg_ref = next(it) if final else None
    o_ref = next(it)
    h_ref = next(it)
    pe_ref = next(it)
    tail_ref = next(it)
    wi_ref, wo_ref, pw_ref = next(it), next(it), next(it)
    fin_ref = next(it) if final else None

    ti = pl.program_id(0)
    is_ctx = ti >= ntl
    t0 = jnp.where(is_ctx, ti - ntl, ti) * (ct * CH)
    t_total = jnp.where(is_ctx, t_ctx, t_lat)

    @pl.when(ti == 0)
    def _():
        wi_ref[...] = wif_ref[...].astype(BF16)
        wo_ref[...] = wof_ref[...].astype(BF16)
        pw_ref[...] = pwf_ref[...].astype(BF16)
        tail_ref[...] = jnp.zeros(tail_ref.shape, F32)

    n1 = POOL_W + 2 * CONV_W
    nb = xm_ref.shape[1]
    tm = ct * CH
    ne = tm + 2 * HALO

    def rows_of(ref):
        return jnp.where(is_ctx, ref[nb:nb + 1], ref[0:nb])

    gain = g_ref[...] * (1.0 + rows_of(scale_ref))
    shift = rows_of(shift_ref)

    def hn(xv):
        return _mod_norm(xv, gain, shift)

    for i in range(CH):
        h_ref[0:ct, i] = hn(xm_ref[:, :, i * D:(i + 1) * D])
    for i in range(HALO):
        h_ref[ct, i] = hn(xn_ref[:, :, i * D:(i + 1) * D])[0]

    m0 = HALO * nb
    mr = tm * nb
    he = h_ref[...].reshape((ct + 1) * CH * nb, D)[0:mr + m0].astype(BF16)
    pe = jnp.dot(he, wi_ref[:, 0:n1], preferred_element_type=F32)
    te = t0 + lax.broadcasted_iota(jnp.int32, (mr + m0, 1), 0) // nb
    valid = te < t_total
    pe_ref[m0:, 0:POOL_W] = jnp.where(valid, pe[:, :POOL_W], 0.0)
    pe_ref[m0:, POOL_W:POOL_W + CONV_W] = jnp.where(
        valid, pe[:, POOL_W:POOL_W + CONV_W] * pe[:, POOL_W + CONV_W:], 0.0)
    pe_ref[0:m0, :] = jnp.where(t0 > 0, tail_ref[...], 0.0)

    p2 = jnp.dot(he[0:mr], wi_ref[:, n1:], preferred_element_type=F32)
    b_gate = p2[:, :CONV_W]
    sz = _silu(p2[:, CONV_W:])

    tpos = t0 + lax.broadcasted_iota(jnp.int32, (mr, 1), 0) // nb
    pooled = []
    for gi, w in enumerate(POOL_WINDOWS):
        c0 = gi * POOL_GW
        s = pe_ref[:, c0:c0 + POOL_GW]
        n = ne
        width = 1
        while width < w:
            s = s[0:(n - width) * nb] + s[width * nb:n * nb]
            n -= width
            width *= 2
        start = (HALO - w // 2) * nb
        total = s[start:start + mr]
        hi = jnp.minimum(tpos + w // 2, t_total)
        lo = jnp.maximum(tpos - w // 2, 0)
        inv_cnt = 1.0 / (hi - lo).astype(F32)
        centre = pe_ref[m0:m0 + mr, c0:c0 + POOL_GW]
        pg = total * inv_cnt - centre
        pooled.append(jnp.dot(pg.astype(BF16), pw_ref[gi], preferred_element_type=F32))
    y_c = jnp.concatenate(pooled, axis=1) * ps_ref[...]

    cwt = cw_ref[...]
    vm = pe_ref[m0 - nb:m0 - nb + mr, POOL_W:POOL_W + CONV_W]
    v0 = pe_ref[m0:m0 + mr, POOL_W:POOL_W + CONV_W]
    vp = pe_ref[m0 + nb:m0 + nb + mr, POOL_W:POOL_W + CONV_W]
    y_d = b_gate * (vm * cwt[0:1, :] + v0 * cwt[1:2, :] + vp * cwt[2:3, :])

    y = (jnp.concatenate([y_c, y_d], axis=1) * sz).astype(BF16)
    out = jnp.dot(y, wo_ref[...], preferred_element_type=F32)
    go = (rows_of(gate_ref) * out.reshape(tm, nb, D)).reshape(ct, CH, nb, D)
    for i in range(CH):
        xo = xm_ref[:, :, i * D:(i + 1) * D] + go[:, i]
        if not final:
            o_ref[:, :, i * D:(i + 1) * D] = xo
            continue
        ms = jnp.mean(xo * xo, axis=-1, keepdims=True)
        fin_ref[:, :, i * D:(i + 1) * D] = xo * lax.rsqrt(ms + EPS) * fg_ref[...]
    tail_ref[...] = pe_ref[mr:mr + m0, :]
    if final:
        o_ref[...] = pltpu.einshape("cb(id)->b(ci)d", fin_ref[...], i=CH)


def _odd_layer(xs, ncl, with_ctx, ada, l, g, w_in, w_out, pool_w, layer, pool_scale, conv_w, final_g):
    nc_all, bsz, _ = xs.shape
    ct = 8
    ntl = ncl // ct
    nc = nc_all if with_ctx else ncl
    final = final_g is not None
    args = [xs, xs, ada, ada, ada, g, w_in, w_out, pool_w, pool_scale, conv_w]

    def vec(j):
        return pl.BlockSpec((None, None) + ada.shape[2:], lambda i: (l, j, 0, 0))

    def const(*shape):
        return pl.BlockSpec(shape, lambda i: (0,) * len(shape), pipeline_mode=pl.Buffered(1))

    def stacked(*shape):
        return pl.BlockSpec((None,) + shape, lambda i: (layer,) + (0,) * len(shape),
                            pipeline_mode=pl.Buffered(1))

    specs = [pl.BlockSpec((ct, bsz, CH * D), lambda i: (i, 0, 0)),
             pl.BlockSpec((1, bsz, CH * D), lambda i: (jnp.minimum((i + 1) * ct, nc_all - 1), 0, 0)),
             vec(0), vec(1), vec(2),
             pl.BlockSpec((None, 1, D), lambda i: (l, 0, 0)), stacked(*w_in.shape[1:]), stacked(MIX, D),
             stacked(len(POOL_WINDOWS), POOL_GW, POOL_GW), stacked(1, POOL_W), stacked(3, CONV_W)]
    scratch = [pltpu.VMEM((ct + 1, CH, bsz, D), F32),
               pltpu.VMEM(((ct + 1) * CH * bsz, POOL_W + CONV_W), F32),
               pltpu.VMEM((HALO * bsz, POOL_W + CONV_W), F32),
               pltpu.VMEM(w_in.shape[1:], BF16), pltpu.VMEM((MIX, D), BF16),
               pltpu.VMEM((len(POOL_WINDOWS), POOL_GW, POOL_GW), BF16)]
    if final:
        args.append(final_g.reshape(1, D))
        specs.append(const(1, D))
        scratch.append(pltpu.VMEM((ct, bsz, CH * D), F32))
        out_spec = pl.BlockSpec((bsz, ct * CH, D), lambda i: (0, i, 0))
        out_shape = jax.ShapeDtypeStruct((bsz, nc * CH, D), F32)
    else:
        out_spec = pl.BlockSpec((ct, bsz, CH * D), lambda i: (i, 0, 0))
        out_shape = jax.ShapeDtypeStruct((nc, bsz, CH * D), F32)
    return pl.pallas_call(
        functools.partial(_odd_kernel, ct=ct, ntl=ntl, t_lat=ncl * CH, t_ctx=(nc_all - ncl) * CH,
                          final=final),
        grid=(nc // ct,),
        in_specs=specs,
        out_specs=out_spec,
        out_shape=out_shape,
        scratch_shapes=scratch,
        compiler_params=_cparams(("arbitrary",)),
    )(*args)


def _sincos_table(n_tok, dim):
    rows = n_tok // GRID_W
    rr, cc = np.meshgrid(np.arange(rows, dtype=np.float64), np.arange(GRID_W, dtype=np.float64),
                         indexing='ij')
    rr = rr.reshape(-1, 1)
    cc = cc.reshape(-1, 1)
    quarter = dim // 4
    omega = POS_BASE ** (-np.arange(quarter, dtype=np.float64) / quarter)
    tab = np.concatenate([np.sin(rr * omega), np.cos(rr * omega), np.sin(cc * omega), np.cos(cc * omega)],
                         axis=-1)
    return jnp.asarray(tab, dtype=F32)


def _block_diag(w):
    g, c, _ = w.shape
    eye = jnp.eye(g, dtype=w.dtype)
    return (eye[:, None, :, None] * w[:, :, None, :]).reshape(g * c, g * c)


def kernel(x, c, ctx, c_ctx, norm_g, ada_w, ada_b, even_w_in, even_w_out, s5_lam_re, s5_lam_im, s5_log_step, s5_b_re, s5_b_im, s5_c_re, s5_c_im, s5_d, s5_glu_w, s5_glu_b, fnet_w, odd_w_in, odd_w_out, pool_w, pool_scale, conv_w, final_g):
    bsz, n_tok, _ = x.shape
    depth = norm_g.shape[0]
    ncl = n_tok // CH
    ncc = ctx.shape[1] // CH

    cond = jnp.concatenate([c, jnp.broadcast_to(c_ctx[None], (16 - bsz, D))], axis=0)
    ada, mt, bend, cp, lam16, xs = _prologue(cond, ada_w, ada_b, s5_lam_re, s5_lam_im, s5_log_step, s5_b_re,
                                             s5_b_im, s5_c_re, s5_c_im, s5_d, x, _sincos_table(n_tok, D), ctx)

    need_ctx = [any(j % 2 == 0 for j in range(l + 1, depth)) for l in range(depth)]
    norm_g3 = norm_g.reshape(depth, 1, D)
    glu_b3 = s5_glu_b.reshape(-1, 1, S5_W)
    pool_scale3 = pool_scale.reshape(-1, 1, POOL_W)

    for l in range(depth):
        i = l // 2
        last = l == depth - 1
        if l % 2 == 0:
            wcs = _fnet_weights(_block_diag(fnet_w[i]))
            ut, xw, z = _even_in(xs, ncl, ada, l, norm_g3, even_w_in, i, wcs)
            yt = _s5_mix(ut, ncl, mt, bend, cp, lam16, bsz, i)
            ybl = _fnet(xw, 0, ncl, bsz)
            ybc = _fnet(xw, ncl, ncc, bsz) if need_ctx[l] else None
            xs = _even_out(yt, ybl, ybc, z, xs, ncl, ada, l, s5_glu_w, glu_b3, even_w_out, i)
        else:
            xs = _odd_layer(xs, ncl, need_ctx[l], ada, l, norm_g3, odd_w_in, odd_w_out, pool_w, i,
                            pool_scale3, conv_w, final_g if last else None)
    if depth % 2 == 1:
        raise NotImplementedError("final norm and (B, T, D) order are produced by the last (odd) layer")
    return xs
```

```python
import functools
import math

import numpy as np
import jax
import jax.numpy as jnp
from jax import lax
from jax.experimental import pallas as pl
from jax.experimental.pallas import tpu as pltpu

D = 1024
MIX = 1024
S5_W = 768
FN_W = 256
S5_H = 16
S5_G = 48
S5_P = 64
FN_G = 4
FN_GW = 64
POOL_W = 512
CONV_W = 512
POOL_WINDOWS = (2, 4, 8, 16)
POOL_GW = 128
GRID_W = 64
EPS = 1e-6
POS_BASE = 10000.0
CH = 16
HALO = 8
LANE = 128
ROW_BLOCK = 256
VMEM_LIMIT = 56 * 1024 * 1024

F32 = jnp.float32
BF16 = jnp.bfloat16
HI = lax.Precision.HIGHEST


def _cparams(sem):
    return pltpu.CompilerParams(dimension_semantics=sem, vmem_limit_bytes=VMEM_LIMIT)


def _silu(v):
    h = 0.5 * v
    return h + h * jnp.tanh(h)


def _gelu_tanh(v):
    c = math.sqrt(2.0 / math.pi)
    h = 0.5 * v
    return h + h * jnp.tanh(v * (c + (c * 0.044715) * (v * v)))


def _mod_norm(x, gain, shift):
    ms = jnp.mean(x * x, axis=-1, keepdims=True)
    return x * lax.rsqrt(ms + EPS) * gain + shift


def _split3(v):
    hi = v.astype(BF16)
    lo = (v - hi.astype(F32)).astype(BF16)
    return hi, lo


def _ada_item(c_ref, w_ref, b_ref, ada_ref):
    s_hi, s_lo = _split3(_silu(c_ref[...]))
    w_hi, w_lo = _split3(w_ref[...])
    ada_ref[...] = (jnp.dot(s_hi, w_hi, preferred_element_type=F32)
                    + jnp.dot(s_hi, w_lo, preferred_element_type=F32)
                    + jnp.dot(s_lo, w_hi, preferred_element_type=F32) + b_ref[...])


def _stream_item(x_ref, pos_ref, c_ref, o_ref, step, ntl):
    @pl.when(step < ntl)
    def _():
        o_ref[...] = pltpu.einshape("b(ci)d->cb(id)", x_ref[...] + pos_ref[...][None], i=CH)

    @pl.when(step >= ntl)
    def _():
        o_ref[...] = pltpu.einshape("b(ci)d->cb(id)", c_ref[...], i=CH)


def _prologue_kernel(c_ref, w_ref, b_ref, lr_ref, li_ref, ls_ref, btr_ref, bti_ref, cr_ref, ci_ref, d_ref,
                     x_ref, pos_ref, ctx_ref, ada_ref, mt_ref, be_ref, cp_ref, l16_ref, xs_ref,
                     ere_ref, eim_ref, *, gb, n_ada, n_tab, ntl):
    step = pl.program_id(0)

    @pl.when(step < n_ada)
    def _():
        _ada_item(c_ref, w_ref, b_ref, ada_ref)

    @pl.when(step < n_tab)
    def _():
        _tables_item(lr_ref, li_ref, ls_ref, btr_ref, bti_ref, cr_ref, ci_ref, d_ref,
                     mt_ref, be_ref, cp_ref, l16_ref, ere_ref, eim_ref, gb)

    _stream_item(x_ref, pos_ref, ctx_ref, xs_ref, step, ntl)


def _tables_item(lr_ref, li_ref, ls_ref, btr_ref, bti_ref, cr_ref, ci_ref, d_ref,
                 mt_ref, be_ref, cp_ref, l16_ref, ere_ref, eim_ref, gb):
    nst = 2 * S5_P
    kk = CH * S5_H
    step = jnp.exp(ls_ref[...])
    lr = lr_ref[...]
    li = li_ref[...]
    a = lr * step
    b = li * step

    mag = jnp.exp(a)
    l1re = mag * jnp.cos(b)
    l1im = mag * jnp.sin(b)
    squares = [(l1re, l1im)]
    for _ in range(4):
        sr, si = squares[-1]
        squares.append((sr * sr - si * si, 2.0 * (sr * si)))

    def powers(expo):
        pr = jnp.ones((gb,) + expo.shape[1:], F32)
        pi = jnp.zeros((gb,) + expo.shape[1:], F32)
        for bit, (sr, si) in enumerate(squares):
            on = ((expo >> bit) & 1) == 1
            pr, pi = jnp.where(on, pr * sr - pi * si, pr), jnp.where(on, pr * si + pi * sr, pi)
        return pr, pi

    row = lax.broadcasted_iota(jnp.int32, (1, CH, nst), 1)
    fwd = lax.broadcasted_iota(jnp.int32, (1, CH, nst), 2) < S5_P

    n_re = l1re - 1.0
    den = lr * lr + li * li
    co_re = (n_re * lr + l1im * li) / den
    co_im = (l1im * lr - n_re * li) / den
    btr = btr_ref[...]
    bti = bti_ref[...]
    bb_re = co_re * btr - co_im * bti
    bb_im = co_re * bti + co_im * btr

    pe_re, pe_im = powers(jnp.where(fwd, (CH - 1) - row, row))
    for l in range(CH):
        pr = pe_re[:, l:l + 1, :]
        pi = pe_im[:, l:l + 1, :]
        ere_ref[:, l * S5_H:(l + 1) * S5_H, :] = pr * bb_re - pi * bb_im
        eim_ref[:, l * S5_H:(l + 1) * S5_H, :] = pr * bb_im + pi * bb_re

    cr = cr_ref[...]
    ci = ci_ref[...]
    pc_re, pc_im = powers(jnp.where(fwd, row + 1, CH - row))
    for j in range(CH):
        pr = pc_re[:, j:j + 1, :]
        pi = pc_im[:, j:j + 1, :]
        w_re = cr * pr - ci * pi
        w_im = cr * pi + ci * pr
        cp_ref[:, j * S5_H:(j + 1) * S5_H, :] = jnp.concatenate([w_re, -w_im], axis=2).astype(BF16)

    l16_ref[...] = jnp.concatenate(squares[4], axis=1)

    fwd2 = lax.broadcasted_iota(jnp.int32, (S5_H, nst), 1) < S5_P
    lane = lax.broadcasted_iota(jnp.int32, (S5_H, kk), 1)
    iblk = lane // S5_H
    hrow = lax.broadcasted_iota(jnp.int32, (S5_H, kk), 0)
    nt = (((1,), (1,)), ((), ()))
    for g in range(gb):
        ere = ere_ref[g]
        eim = eim_ref[g]
        e2 = jnp.concatenate([ere, eim], axis=1)
        be_ref[g] = e2.astype(BF16)
        e_hi, e_lo = _split3(e2)
        rhs = jnp.concatenate([e_hi, e_hi, e_lo], axis=1)
        crg = cr[g]
        cig = ci[g]

        c2 = jnp.concatenate([
            jnp.concatenate([jnp.where(fwd2, crg, 0.0), jnp.where(fwd2, -cig, 0.0)], axis=1),
            jnp.concatenate([jnp.where(fwd2, 0.0, crg), jnp.where(fwd2, 0.0, -cig)], axis=1)], axis=0)
        c_hi, c_lo = _split3(c2)
        lhs = jnp.concatenate([c_hi, c_lo, c_hi], axis=1)
        kfb = lax.dot_general(lhs, rhs, nt, preferred_element_type=F32)
        kf = kfb[0:S5_H]
        kb = kfb[S5_H:2 * S5_H]
        dg = d_ref[g]
        for j in range(CH):
            sf = (kk - (CH - 1 - j) * S5_H) % kk
            rf = pltpu.roll(kf, sf, 1) if sf else kf
            rb = pltpu.roll(kb, j * S5_H, 1) if j else kb
            blk = (jnp.where(iblk <= j, rf, 0.0) + jnp.where(iblk >= j, rb, 0.0)
                   + jnp.where(lane == j * S5_H + hrow, dg, 0.0))
            mt_ref[g, j * S5_H:(j + 1) * S5_H, :] = blk.astype(BF16)


def _prologue(cond, ada_w, ada_b, lam_re, lam_im, log_step, b_re, b_im, c_re, c_im, d_skip, x, pos, ctx):
    n = lam_re.shape[0] * S5_G
    gb = 8
    nst = 2 * S5_P
    kk = CH * S5_H

    def fb(v):
        return jnp.concatenate([v[:, 0], v[:, 1]], axis=-1).reshape(n, v.shape[3], nst)

    lr = fb(lam_re[:, :, :, None, :])
    li = fb(lam_im[:, :, :, None, :])
    ls = fb(jnp.broadcast_to(log_step[:, :, :, None, None], log_step.shape + (1, S5_P)))
    btr = fb(jnp.swapaxes(b_re, -1, -2))
    bti = fb(jnp.swapaxes(b_im, -1, -2))
    cr = fb(c_re)
    ci = fb(c_im)
    d = d_skip.reshape(n, S5_H, 1)

    depth = ada_w.shape[0]
    n_ada = depth * 3
    n_tab = n // gb
    bsz, t, _ = x.shape
    tc = ctx.shape[1]
    ct = 8
    tm = ct * CH
    ntl = t // tm
    n_str = (t + tc) // tm

    def spec(r, c):
        return pl.BlockSpec((gb, r, c), lambda s: (jnp.minimum(s, n_tab - 1), 0, 0))

    def ada_idx(s):
        s = jnp.minimum(s, n_ada - 1)
        return s // 3, s % 3

    def lat_tile(s):
        return jnp.minimum(s, ntl - 1)

    return pl.pallas_call(
        functools.partial(_prologue_kernel, gb=gb, n_ada=n_ada, n_tab=n_tab, ntl=ntl),
        grid=(max(n_ada, n_tab, n_str),),
        in_specs=[pl.BlockSpec((16, D), lambda s: (0, 0)),
                  pl.BlockSpec((None, D, D), lambda s: (ada_idx(s)[0], 0, ada_idx(s)[1])),
                  pl.BlockSpec((None, None, 1, D), lambda s: ada_idx(s) + (0, 0)),
                  spec(1, nst), spec(1, nst), spec(1, nst), spec(S5_H, nst), spec(S5_H, nst),
                  spec(S5_H, nst), spec(S5_H, nst), spec(S5_H, 1),
                  pl.BlockSpec((bsz, tm, D), lambda s: (0, lat_tile(s), 0)),
                  pl.BlockSpec((tm, D), lambda s: (lat_tile(s), 0)),
                  pl.BlockSpec((bsz, tm, D), lambda s: (0, jnp.clip(s - ntl, 0, tc // tm - 1), 0),
                               pipeline_mode=pl.Buffered(1))],
        out_specs=[pl.BlockSpec((None, None, 16, D), lambda s: ada_idx(s) + (0, 0)),
                   spec(kk, kk), spec(kk, 2 * nst), spec(kk, 2 * nst), spec(2, nst),
                   pl.BlockSpec((ct, bsz, CH * D), lambda s: (jnp.minimum(s, n_str - 1), 0, 0))],
        out_shape=[jax.ShapeDtypeStruct((depth, 3, 16, D), F32),
                   jax.ShapeDtypeStruct((n, kk, kk), BF16),
                   jax.ShapeDtypeStruct((n, kk, 2 * nst), BF16),
                   jax.ShapeDtypeStruct((n, kk, 2 * nst), BF16),
                   jax.ShapeDtypeStruct((n, 2, nst), F32),
                   jax.ShapeDtypeStruct(((t + tc) // CH, bsz, CH * D), F32)],
        scratch_shapes=[pltpu.VMEM((gb, kk, nst), F32)] * 2,
        compiler_params=_cparams(("arbitrary",)),
    )(cond, ada_w, ada_b.reshape(depth, 3, 1, D), lr, li, ls, btr, bti, cr, ci, d, x, pos, ctx)


def _even_in_kernel(x_ref, shift_ref, scale_ref, g_ref, w_ref, wcs_ref, ut_ref, xw_ref, z_ref,
                    wat_ref, wbz_ref, *scr, ncl):
    @pl.when(pl.program_id(0) == 0)
    def _():
        wat_ref[...] = w_ref[:, :S5_W].T.astype(BF16)
        wbz_ref[...] = w_ref[:, S5_W:].astype(BF16)

    x = x_ref[...]
    nc, nb, _ = x.shape
    rows = nc * nb
    gain = g_ref[...] * (1.0 + scale_ref[...])
    h = jnp.concatenate([_mod_norm(x[0:ncl], gain[0:nb][None], shift_ref[0:nb][None]),
                         _mod_norm(x[ncl:nc], gain[nb:nb + 1][None], shift_ref[nb:nb + 1][None])], axis=0)
    hb = h.reshape(rows, D).astype(BF16)
    pt = lax.dot_general(wat_ref[...], hb, (((1,), (1,)), ((), ())), preferred_element_type=F32)
    ut_ref[...] = pt.astype(BF16).reshape(S5_G, S5_H, rows)
    p = jnp.dot(hb, wbz_ref[...], preferred_element_type=F32)
    z_ref[...] = _silu(p[:, FN_W:]).astype(BF16)
    xw = jnp.dot(p[:, :FN_W].astype(BF16), wcs_ref[...], preferred_element_type=F32)
    for q in range(4):
        scr[q][...] = xw[:, q * LANE:(q + 1) * LANE]
    for q in range(4):
        part, half = divmod(q, 2)
        for bi in range(nb):
            piece = scr[q][pl.ds(bi, nc, stride=nb), :]
            lo = bi * FN_W + half * LANE
            xw_ref[part, :, lo:lo + LANE] = piece.astype(BF16)


def _even_in(xs, ncl, ada, l, g, w_in, layer, wcs):
    nc, bsz, _ = xs.shape
    rows = nc * bsz
    return pl.pallas_call(
        functools.partial(_even_in_kernel, ncl=ncl),
        grid=(CH,),
        in_specs=[pl.BlockSpec((nc, bsz, D), lambda i: (0, 0, i)),
                  pl.BlockSpec((None, None) + ada.shape[2:], lambda i: (l, 0, 0, 0)),
                  pl.BlockSpec((None, None) + ada.shape[2:], lambda i: (l, 1, 0, 0)),
                  pl.BlockSpec((None, 1, D), lambda i: (l, 0, 0)),
                  pl.BlockSpec((None, D, w_in.shape[2]), lambda i: (layer, 0, 0),
                               pipeline_mode=pl.Buffered(1)),
                  pl.BlockSpec((FN_W, 2 * FN_W), lambda i: (0, 0))],
        out_specs=[pl.BlockSpec((S5_G, S5_H, rows), lambda i: (0, i, 0)),
                   pl.BlockSpec((2, None, nc, bsz * FN_W), lambda i: (0, i, 0, 0)),
                   pl.BlockSpec((None, rows, MIX), lambda i: (i, 0, 0))],
        out_shape=[jax.ShapeDtypeStruct((S5_G, CH * S5_H, rows), BF16),
                   jax.ShapeDtypeStruct((2, CH, nc, bsz * FN_W), BF16),
                   jax.ShapeDtypeStruct((CH, rows, MIX), BF16)],
        scratch_shapes=[pltpu.VMEM((S5_W, D), BF16), pltpu.VMEM((D, FN_W + MIX), BF16)]
        + [pltpu.VMEM((rows, LANE), F32)] * 4,
        compiler_params=_cparams(("arbitrary",)),
    )(xs, ada, ada, g, w_in, wcs)


def _s5_kernel(ut_ref, mt_ref, be_ref, cp_ref, l16_ref, yt_ref,
               sre_ref, sim_ref, are_ref, aim_ref, bre_ref, bim_ref, *, bsz, ncl, ncc, gs):
    nl = bsz * ncl
    nst = 2 * S5_P
    for g in range(gs):
        st = lax.dot_general(ut_ref[g], be_ref[g], (((0,), (0,)), ((), ())), preferred_element_type=F32)
        sre_ref[g] = st[:, :nst]
        sim_ref[g] = st[:, nst:]

    lam = [l16_ref[g] for g in range(gs)]
    is_fwd = lax.broadcasted_iota(jnp.int32, (bsz, nst), 1) < S5_P

    def make_step(base, nchunk):
        def step(c, carry):
            rf = pl.ds(pl.multiple_of(base + c * bsz, bsz), bsz)
            rb = pl.ds(pl.multiple_of(base + (nchunk - 1 - c) * bsz, bsz), bsz)
            out = []
            for g in range(gs):
                sre, sim = carry[2 * g], carry[2 * g + 1]
                lre = lam[g][0:1, :]
                lim = lam[g][1:2, :]
                are_ref[g, rf, :] = sre
                aim_ref[g, rf, :] = sim
                bre_ref[g, rb, :] = sre
                bim_ref[g, rb, :] = sim
                in_re = jnp.where(is_fwd, sre_ref[g, rf, :], sre_ref[g, rb, :])
                in_im = jnp.where(is_fwd, sim_ref[g, rf, :], sim_ref[g, rb, :])
                out.append(lre * sre - lim * sim + in_re)
                out.append(lre * sim + lim * sre + in_im)
            return tuple(out)
        return step

    zero = jnp.zeros((bsz, nst), F32)
    carry = lax.fori_loop(0, ncc, make_step(nl, ncc), (zero,) * (2 * gs))
    lax.fori_loop(0, ncl, make_step(0, ncl), carry)

    nt = (((1,), (1,)), ((), ()))
    fwd_rows = lax.broadcasted_iota(jnp.int32, (bsz * (ncl + ncc), nst), 1) < S5_P
    for g in range(gs):
        s0 = jnp.concatenate([jnp.where(fwd_rows, are_ref[g], bre_ref[g]),
                              jnp.where(fwd_rows, aim_ref[g], bim_ref[g])], axis=1).astype(BF16)
        yt_ref[g] = (jnp.dot(mt_ref[g], ut_ref[g], preferred_element_type=F32)
                     + lax.dot_general(cp_ref[g], s0, nt, preferred_element_type=F32)).astype(BF16)


def _s5_mix(ut, ncl, mt, bend, cp, lam16, bsz, layer):
    rows = ut.shape[2]
    kk = CH * S5_H
    nst = 2 * S5_P
    gs = 6
    off = layer * (S5_G // gs)

    def gspec(r, c):
        return pl.BlockSpec((gs, r, c), lambda g: (g, 0, 0))

    def tspec(r, c):
        return pl.BlockSpec((gs, r, c), lambda g: (g + off, 0, 0))

    return pl.pallas_call(
        functools.partial(_s5_kernel, bsz=bsz, ncl=ncl, ncc=rows // bsz - ncl, gs=gs),
        grid=(S5_G // gs,),
        in_specs=[gspec(kk, rows), tspec(kk, kk), tspec(kk, 2 * nst), tspec(kk, 2 * nst), tspec(2, nst)],
        out_specs=gspec(kk, rows),
        out_shape=jax.ShapeDtypeStruct((S5_G, kk, rows), BF16),
        scratch_shapes=[pltpu.VMEM((gs, rows, nst), F32)] * 6,
        compiler_params=_cparams(("arbitrary",)),
    )(ut, mt, bend, cp, lam16)


def _fnet_weights_kernel(ccs_ref, fw_ref, o_ref):
    fw = fw_ref[...]
    ccs = ccs_ref[...]
    wc = jnp.dot(ccs[:, :FN_W], fw, precision=HI, preferred_element_type=F32)
    ws = jnp.dot(ccs[:, FN_W:], fw, precision=HI, preferred_element_type=F32)
    o_ref[...] = jnp.concatenate([wc, ws], axis=1).astype(BF16)


def _fnet_weights(fw_bd):
    return pl.pallas_call(
        _fnet_weights_kernel,
        out_shape=jax.ShapeDtypeStruct((FN_W, 2 * FN_W), BF16),
    )(_dft_channel_matrix(), fw_bd)


def _fnet_kernel(tab_ref, v_ref, jr_ref, o_ref, *, ipt):
    r = pl.program_id(1)
    tab = tab_ref[...].astype(BF16)
    v = v_ref[...]
    nc, n = v.shape[2], v.shape[3]
    t = CH * nc
    a = jnp.dot(tab[:, :t], v[0].reshape(t, n), preferred_element_type=F32)
    b = jnp.dot(tab[:, t:], v[1].reshape(t, n), preferred_element_type=F32)
    o_ref[pl.ds(r * ipt, ipt)] = (a - b).reshape(ipt, nc, n).astype(BF16)
    m = (a + b).astype(BF16)
    for k in range(ipt):
        i = r * ipt + k

        @pl.when(jnp.logical_and(i >= 1, i <= CH // 2 - 1))
        def _():
            o_ref[CH - i] = jnp.dot(jr_ref[...], m[k * nc:(k + 1) * nc],
                                    preferred_element_type=F32).astype(BF16)


def _dft_half_table(nc):
    t = nc * CH
    cols = (np.arange(nc)[None, :] * CH + np.arange(CH)[:, None]).reshape(-1)
    rows = (np.arange(nc)[None, :] * CH + np.arange(CH // 2 + 1)[:, None]).reshape(-1)
    prod = (rows[:, None].astype(np.int64) * cols[None, :].astype(np.int64)) % t
    ang = prod.astype(np.float64) * (2.0 * np.pi / t)
    scale = 1.0 / math.sqrt(t * FN_GW)
    return jnp.asarray(np.concatenate([np.cos(ang), np.sin(ang)], axis=1) * scale, dtype=F32)


def _dft_channel_matrix():
    c = np.arange(FN_GW)
    ang = (c[:, None] * c[None, :] % FN_GW).astype(np.float64) * (2.0 * np.pi / FN_GW)
    eye = np.eye(FN_G)
    return jnp.asarray(np.concatenate([np.kron(eye, np.cos(ang)), np.kron(eye, np.sin(ang))], axis=1),
                       dtype=F32)


def _fnet(xw, c0, nc, bsz):
    t = nc * CH
    nhb = CH // 2 + 1
    ipt = 3
    bpb = 4
    jr = jnp.asarray(np.eye(nc)[::-1], dtype=BF16)
    return pl.pallas_call(
        functools.partial(_fnet_kernel, ipt=ipt),
        grid=(bsz // bpb, nhb // ipt),
        in_specs=[pl.BlockSpec((ipt * nc, 2 * t), lambda b, r: (r, 0)),
                  pl.BlockSpec((2, CH, nc, bpb * FN_W), lambda b, r: (0, 0, c0 // nc, b)),
                  pl.BlockSpec((nc, nc), lambda b, r: (0, 0))],
        out_specs=pl.BlockSpec((CH, nc, bpb * FN_W), lambda b, r: (0, 0, b)),
        out_shape=jax.ShapeDtypeStruct((CH, nc, bsz * FN_W), BF16),
        compiler_params=_cparams(("arbitrary", "arbitrary")),
    )(_dft_half_table(nc), xw, jr)


def _even_out_kernel(*refs, ncl, with_ctx):
    it = iter(refs)
    yt_ref, ybl_ref = next(it), next(it)
    ybc_ref = next(it) if with_ctx else None
    z_ref, x_ref, gate_ref, gwf_ref, gb_ref, wof_ref, o_ref = (next(it), next(it), next(it), next(it),
                                                               next(it), next(it), next(it))
    scr = [next(it), next(it)]
    gw_ref, wo_ref = next(it), next(it)

    @pl.when(pl.program_id(0) == 0)
    def _():
        gw_ref[...] = (0.5 * gwf_ref[...]).astype(BF16)
        wo_ref[...] = wof_ref[...].astype(BF16)

    rows = yt_ref.shape[2]
    nc, nb, _ = x_ref.shape
    for half in range(2):
        for bi in range(nb):
            lo = bi * FN_W + half * LANE
            scr[half][pl.ds(bi, ncl, stride=nb), :] = ybl_ref[:, lo:lo + LANE].astype(F32)
            if with_ctx:
                scr[half][pl.ds(ncl * nb + bi, nc - ncl, stride=nb), :] = (
                    ybc_ref[:, lo:lo + LANE].astype(F32))

    for r0 in range(0, rows, ROW_BLOCK):
        rb = min(ROW_BLOCK, rows - r0)
        yt = yt_ref[:, :, r0:r0 + rb]
        ya = _gelu_tanh(yt.astype(F32).reshape(S5_W, rb).T)
        half_g = jnp.dot(ya.astype(BF16), gw_ref[...], preferred_element_type=F32) + 0.5 * gb_ref[...]
        hy = 0.5 * ya
        ya = hy + hy * jnp.tanh(half_g)
        yb = jnp.concatenate([scr[0][r0:r0 + rb, :], scr[1][r0:r0 + rb, :]], axis=1)
        sz = z_ref[r0:r0 + rb, :]
        ma = (ya * sz[:, :S5_W]).astype(BF16)
        mb = (yb * sz[:, S5_W:]).astype(BF16)
        out = (jnp.dot(ma, wo_ref[0:S5_W, :], preferred_element_type=F32)
               + jnp.dot(mb, wo_ref[S5_W:MIX, :], preferred_element_type=F32))
        ca, cb = r0 // nb, (r0 + rb) // nb
        out = out.reshape(cb - ca, nb, D)
        cm = min(cb, ncl)
        if ca < cm:
            o_ref[ca:cm] = x_ref[ca:cm] + gate_ref[0:nb][None] * out[0:cm - ca]
        if cm < cb:
            lo = max(ca, ncl)
            o_ref[lo:cb] = x_ref[lo:cb] + gate_ref[nb:nb + 1][None] * out[lo - ca:cb - ca]


def _even_out(yt, ybl, ybc, z, xs, ncl, ada, l, glu_w, glu_b, w_out, layer):
    bsz = xs.shape[1]
    with_ctx = ybc is not None
    nc = xs.shape[0] if with_ctx else ncl
    rows = nc * bsz
    args = [yt, ybl]
    specs = [pl.BlockSpec((S5_G, S5_H, rows), lambda j: (0, j, 0)),
             pl.BlockSpec((None, ncl, bsz * FN_W), lambda j: (j, 0, 0))]
    if with_ctx:
        args.append(ybc)
        specs.append(pl.BlockSpec((None, nc - ncl, bsz * FN_W), lambda j: (j, 0, 0)))
    args += [z, xs, ada, glu_w, glu_b, w_out]
    specs += [pl.BlockSpec((None, rows, MIX), lambda j: (j, 0, 0)),
              pl.BlockSpec((nc, bsz, D), lambda j: (0, 0, j)),
              pl.BlockSpec((None, None) + ada.shape[2:], lambda j: (l, 2, 0, 0)),
              pl.BlockSpec((None, S5_W, S5_W), lambda j: (layer, 0, 0), pipeline_mode=pl.Buffered(1)),
              pl.BlockSpec((None, 1, S5_W), lambda j: (layer, 0, 0)),
              pl.BlockSpec((None, MIX, D), lambda j: (layer, 0, 0), pipeline_mode=pl.Buffered(1))]
    return pl.pallas_call(
        functools.partial(_even_out_kernel, ncl=ncl, with_ctx=with_ctx),
        grid=(CH,),
        in_specs=specs,
        out_specs=pl.BlockSpec((nc, bsz, D), lambda j: (0, 0, j)),
        out_shape=jax.ShapeDtypeStruct((nc, bsz, CH * D), F32),
        scratch_shapes=[pltpu.VMEM((rows, LANE), F32)] * 2 + [pltpu.VMEM((S5_W, S5_W), BF16),
                                                             pltpu.VMEM((MIX, D), BF16)],
        compiler_params=_cparams(("arbitrary",)),
    )(*args)


def _odd_kernel(*refs, ct, ntl, t_lat, t_ctx, final):
    it = iter(refs)
    xm_ref, xn_ref = next(it), next(it)
    shift_ref, scale_ref, gate_ref, g_ref = next(it), next(it), next(it), next(it)
    wif_ref, wof_ref, pwf_ref, ps_ref, cw_ref = next(it), next(it), next(it), next(it), next(it)
    fg_ref = next(it) if final else None
    o_ref = next(it)
    h_ref = next(it)
    pe_ref = next(it)
    tail_ref = next(it)
    wi_ref, wo_ref, pw_ref = next(it), next(it), next(it)
    fin_ref = next(it) if final else None

    ti = pl.program_id(0)
    is_ctx = ti >= ntl
    t0 = jnp.where(is_ctx, ti - ntl, ti) * (ct * CH)
    t_total = jnp.where(is_ctx, t_ctx, t_lat)

    @pl.when(ti == 0)
    def _():
        wi_ref[...] = wif_ref[...].astype(BF16)
        wo_ref[...] = wof_ref[...].astype(BF16)
        pw_ref[...] = pwf_ref[...].astype(BF16)
        tail_ref[...] = jnp.zeros(tail_ref.shape, F32)

    n1 = POOL_W + 2 * CONV_W
    nb = xm_ref.shape[1]
    tm = ct * CH
    ne = tm + 2 * HALO

    def rows_of(ref):
        return jnp.where(is_ctx, ref[nb:nb + 1], ref[0:nb])

    gain = g_ref[...] * (1.0 + rows_of(scale_ref))
    shift = rows_of(shift_ref)

    def hn(xv):
        return _mod_norm(xv, gain, shift)

    for i in range(CH):
        h_ref[0:ct, i] = hn(xm_ref[:, :, i * D:(i + 1) * D])
    for i in range(HALO):
        h_ref[ct, i] = hn(xn_ref[:, :, i * D:(i + 1) * D])[0]

    m0 = HALO * nb
    mr = tm * nb
    he = h_ref[...].reshape((ct + 1) * CH * nb, D)[0:mr + m0].astype(BF16)
    pe = jnp.dot(he, wi_ref[:, 0:n1], preferred_element_type=F32)
    te = t0 + lax.broadcasted_iota(jnp.int32, (mr + m0, 1), 0) // nb
    valid = te < t_total
    pe_ref[m0:, 0:POOL_W] = jnp.where(valid, pe[:, :POOL_W], 0.0)
    pe_ref[m0:, POOL_W:POOL_W + CONV_W] = jnp.where(
        valid, pe[:, POOL_W:POOL_W + CONV_W] * pe[:, POOL_W + CONV_W:], 0.0)
    pe_ref[0:m0, :] = jnp.where(t0 > 0, tail_ref[...], 0.0)

    p2 = jnp.dot(he[0:mr], wi_ref[:, n1:], preferred_element_type=F32)
    b_gate = p2[:, :CONV_W]
    sz = _silu(p2[:, CONV_W:])

    tpos = t0 + lax.broadcasted_iota(jnp.int32, (mr, 1), 0) // nb
    pooled = []
    for gi, w in enumerate(POOL_WINDOWS):
        c0 = gi * POOL_GW
        s = pe_ref[:, c0:c0 + POOL_GW]
        n = ne
        width = 1
        while width < w:
            s = s[0:(n - width) * nb] + s[width * nb:n * nb]
            n -= width
            width *= 2
        start = (HALO - w // 2) * nb
        total = s[start:start + mr]
        hi = jnp.minimum(tpos + w // 2, t_total)
        lo = jnp.maximum(tpos - w // 2, 0)
        inv_cnt = 1.0 / (hi - lo).astype(F32)
        centre = pe_ref[m0:m0 + mr, c0:c0 + POOL_GW]
        pg = total * inv_cnt - centre
        pooled.append(jnp.dot(pg.astype(BF16), pw_ref[gi], preferred_element_type=F32))
    y_c = jnp.concatenate(pooled, axis=1) * ps_ref[...]

    cwt = cw_ref[...]
    vm = pe_ref[m0 - nb:m0 - nb + mr, POOL_W:POOL_W + CONV_W]
    v0 = pe_ref[m0:m0 + mr, POOL_W:POOL_W + CONV_W]
    vp = pe_ref[m0 + nb:m0 + nb + mr, POOL_W:POOL_W + CONV_W]
    y_d = b_gate * (vm * cwt[0:1, :] + v0 * cwt[1:2, :] + vp * cwt[2:3, :])

    y = (jnp.concatenate([y_c, y_d], axis=1) * sz).astype(BF16)
    out = jnp.dot(y, wo_ref[...], preferred_element_type=F32)
    go = (rows_of(gate_ref) * out.reshape(tm, nb, D)).reshape(ct, CH, nb, D)
    for i in range(CH):
        xo = xm_ref[:, :, i * D:(i + 1) * D] + go[:, i]
        if not final:
            o_ref[:, :, i * D:(i + 1) * D] = xo
            continue
        ms = jnp.mean(xo * xo, axis=-1, keepdims=True)
        fin_ref[:, :, i * D:(i + 1) * D] = xo * lax.rsqrt(ms + EPS) * fg_ref[...]
    tail_ref[...] = pe_ref[mr:mr + m0, :]
    if final:
        o_ref[...] = pltpu.einshape("cb(id)->b(ci)d", fin_ref[...], i=CH)


def _odd_layer(xs, ncl, with_ctx, ada, l, g, w_in, w_out, pool_w, layer, pool_scale, conv_w, final_g):
    nc_all, bsz, _ = xs.shape
    ct = 8
    ntl = ncl // ct
    nc = nc_all if with_ctx else ncl
    final = final_g is not None
    args = [xs, xs, ada, ada, ada, g, w_in, w_out, pool_w, pool_scale, conv_w]

    def vec(j):
        return pl.BlockSpec((None, None) + ada.shape[2:], lambda i: (l, j, 0, 0))

    def const(*shape):
        return pl.BlockSpec(shape, lambda i: (0,) * len(shape), pipeline_mode=pl.Buffered(1))

    def stacked(*shape):
        return pl.BlockSpec((None,) + shape, lambda i: (layer,) + (0,) * len(shape),
                            pipeline_mode=pl.Buffered(1))

    specs = [pl.BlockSpec((ct, bsz, CH * D), lambda i: (i, 0, 0)),
             pl.BlockSpec((1, bsz, CH * D), lambda i: (jnp.minimum((i + 1) * ct, nc_all - 1), 0, 0)),
             vec(0), vec(1), vec(2),
             pl.BlockSpec((None, 1, D), lambda i: (l, 0, 0)), stacked(*w_in.shape[1:]), stacked(MIX, D),
             stacked(len(POOL_WINDOWS), POOL_GW, POOL_GW), stacked(1, POOL_W), stacked(3, CONV_W)]
    scratch = [pltpu.VMEM((ct + 1, CH, bsz, D), F32),
               pltpu.VMEM(((ct + 1) * CH * bsz, POOL_W + CONV_W), F32),
               pltpu.VMEM((HALO * bsz, POOL_W + CONV_W), F32),
               pltpu.VMEM(w_in.shape[1:], BF16), pltpu.VMEM((MIX, D), BF16),
               pltpu.VMEM((len(POOL_WINDOWS), POOL_GW, POOL_GW), BF16)]
    if final:
        args.append(final_g.reshape(1, D))
        specs.append(const(1, D))
        scratch.append(pltpu.VMEM((ct, bsz, CH * D), F32))
        out_spec = pl.BlockSpec((bsz, ct * CH, D), lambda i: (0, i, 0))
        out_shape = jax.ShapeDtypeStruct((bsz, nc * CH, D), F32)
    else:
        out_spec = pl.BlockSpec((ct, bsz, CH * D), lambda i: (i, 0, 0))
        out_shape = jax.ShapeDtypeStruct((nc, bsz, CH * D), F32)
    return pl.pallas_call(
        functools.partial(_odd_kernel, ct=ct, ntl=ntl, t_lat=ncl * CH, t_ctx=(nc_all - ncl) * CH,
                          final=final),
        grid=(nc // ct,),
        in_specs=specs,
        out_specs=out_spec,
        out_shape=out_shape,
        scratch_shapes=scratch,
        compiler_params=_cparams(("arbitrary",)),
    )(*args)


def _sincos_table(n_tok, dim):
    rows = n_tok // GRID_W
    rr, cc = np.meshgrid(np.arange(rows, dtype=np.float64), np.arange(GRID_W, dtype=np.float64),
                         indexing='ij')
    rr = rr.reshape(-1, 1)
    cc = cc.reshape(-1, 1)
    quarter = dim // 4
    omega = POS_BASE ** (-np.arange(quarter, dtype=np.float64) / quarter)
    tab = np.concatenate([np.sin(rr * omega), np.cos(rr * omega), np.sin(cc * omega), np.cos(cc * omega)],
                         axis=-1)
    return jnp.asarray(tab, dtype=F32)


def _block_diag(w):
    g, c, _ = w.shape
    eye = jnp.eye(g, dtype=w.dtype)
    return (eye[:, None, :, None] * w[:, :, None, :]).reshape(g * c, g * c)


def kernel(x, c, ctx, c_ctx, norm_g, ada_w, ada_b, even_w_in, even_w_out, s5_lam_re, s5_lam_im, s5_log_step, s5_b_re, s5_b_im, s5_c_re, s5_c_im, s5_d, s5_glu_w, s5_glu_b, fnet_w, odd_w_in, odd_w_out, pool_w, pool_scale, conv_w, final_g):
    bsz, n_tok, _ = x.shape
    depth = norm_g.shape[0]
    ncl = n_tok // CH
    ncc = ctx.shape[1] // CH

    cond = jnp.concatenate([c, jnp.broadcast_to(c_ctx[None], (16 - bsz, D))], axis=0)
    ada, mt, bend, cp, lam16, xs = _prologue(cond, ada_w, ada_b, s5_lam_re, s5_lam_im, s5_log_step, s5_b_re,
                                             s5_b_im, s5_c_re, s5_c_im, s5_d, x, _sincos_table(n_tok, D), ctx)

    need_ctx = [any(j % 2 == 0 for j in range(l + 1, depth)) for l in range(depth)]
    norm_g3 = norm_g.reshape(depth, 1, D)
    glu_b3 = s5_glu_b.reshape(-1, 1, S5_W)
    pool_scale3 = pool_scale.reshape(-1, 1, POOL_W)

    for l in range(depth):
        i = l // 2
        last = l == depth - 1
        if l % 2 == 0:
            wcs = _fnet_weights(_block_diag(fnet_w[i]))
            ut, xw, z = _even_in(xs, ncl, ada, l, norm_g3, even_w_in, i, wcs)
            yt = _s5_mix(ut, ncl, mt, bend, cp, lam16, bsz, i)
            ybl = _fnet(xw, 0, ncl, bsz)
            ybc = _fnet(xw, ncl, ncc, bsz) if need_ctx[l] else None
            xs = _even_out(yt, ybl, ybc, z, xs, ncl, ada, l, s5_glu_w, glu_b3, even_w_out, i)
        else:
            xs = _odd_layer(xs, ncl, need_ctx[l], ada, l, norm_g3, odd_w_in, odd_w_out, pool_w, i,
                            pool_scale3, conv_w, final_g if last else None)
    if depth % 2 == 1:
        raise NotImplementedError("final norm and (B, T, D) order are produced by the last (odd) layer")
    return xs
```

```python
import functools
import math

import numpy as np
import jax
import jax.numpy as jnp
from jax import lax
from jax.experimental import pallas as pl
from jax.experimental.pallas import tpu as pltpu

D = 1024
MIX = 1024
S5_W = 768
FN_W = 256
S5_H = 16
S5_G = 48
S5_P = 64
FN_G = 4
FN_GW = 64
POOL_W = 512
CONV_W = 512
POOL_WINDOWS = (2, 4, 8, 16)
POOL_GW = 128
GRID_W = 64
EPS = 1e-6
POS_BASE = 10000.0
CH = 16
HALO = 8
LANE = 128
ROW_BLOCK = 256
VMEM_LIMIT = 56 * 1024 * 1024

F32 = jnp.float32
BF16 = jnp.bfloat16
HI = lax.Precision.HIGHEST


def _cparams(sem):
    return pltpu.CompilerParams(dimension_semantics=sem, vmem_limit_bytes=VMEM_LIMIT)


def _silu(v):
    h = 0.5 * v
    return h + h * jnp.tanh(h)


def _gelu_tanh(v):
    c = math.sqrt(2.0 / math.pi)
    h = 0.5 * v
    return h + h * jnp.tanh(v * (c + (c * 0.044715) * (v * v)))


def _mod_norm(x, gain, shift):
    ms = jnp.mean(x * x, axis=-1, keepdims=True)
    return x * lax.rsqrt(ms + EPS) * gain + shift


def _split3(v):
    hi = v.astype(BF16)
    lo = (v - hi.astype(F32)).astype(BF16)
    return hi, lo


def _ada_item(c_ref, w_ref, b_ref, ada_ref):
    s_hi, s_lo = _split3(_silu(c_ref[...]))
    w_hi, w_lo = _split3(w_ref[...])
    ada_ref[...] = (jnp.dot(s_hi, w_hi, preferred_element_type=F32)
                    + jnp.dot(s_hi, w_lo, preferred_element_type=F32)
                    + jnp.dot(s_lo, w_hi, preferred_element_type=F32) + b_ref[...])


def _stream_item(x_ref, pos_ref, c_ref, o_ref, step, ntl):
    @pl.when(step < ntl)
    def _():
        o_ref[...] = pltpu.einshape("b(ci)d->cb(id)", x_ref[...] + pos_ref[...][None], i=CH)

    @pl.when(step >= ntl)
    def _():
        o_ref[...] = pltpu.einshape("b(ci)d->cb(id)", c_ref[...], i=CH)


def _prologue_kernel(c_ref, w_ref, b_ref, lr_ref, li_ref, ls_ref, btr_ref, bti_ref, cr_ref, ci_ref, d_ref,
                     x_ref, pos_ref, ctx_ref, ada_ref, mt_ref, be_ref, cp_ref, l16_ref, xs_ref,
                     ere_ref, eim_ref, *, gb, n_ada, n_tab, ntl):
    step = pl.program_id(0)

    @pl.when(step < n_ada)
    def _():
        _ada_item(c_ref, w_ref, b_ref, ada_ref)

    @pl.when(step < n_tab)
    def _():
        _tables_item(lr_ref, li_ref, ls_ref, btr_ref, bti_ref, cr_ref, ci_ref, d_ref,
                     mt_ref, be_ref, cp_ref, l16_ref, ere_ref, eim_ref, gb)

    _stream_item(x_ref, pos_ref, ctx_ref, xs_ref, step, ntl)


def _tables_item(lr_ref, li_ref, ls_ref, btr_ref, bti_ref, cr_ref, ci_ref, d_ref,
                 mt_ref, be_ref, cp_ref, l16_ref, ere_ref, eim_ref, gb):
    nst = 2 * S5_P
    kk = CH * S5_H
    step = jnp.exp(ls_ref[...])
    lr = lr_ref[...]
    li = li_ref[...]
    a = lr * step
    b = li * step

    mag = jnp.exp(a)
    l1re = mag * jnp.cos(b)
    l1im = mag * jnp.sin(b)
    squares = [(l1re, l1im)]
    for _ in range(4):
        sr, si = squares[-1]
        squares.append((sr * sr - si * si, 2.0 * (sr * si)))

    def powers(expo):
        pr = jnp.ones((gb,) + expo.shape[1:], F32)
        pi = jnp.zeros((gb,) + expo.shape[1:], F32)
        for bit, (sr, si) in enumerate(squares):
            on = ((expo >> bit) & 1) == 1
            pr, pi = jnp.where(on, pr * sr - pi * si, pr), jnp.where(on, pr * si + pi * sr, pi)
        return pr, pi

    row = lax.broadcasted_iota(jnp.int32, (1, CH, nst), 1)
    fwd = lax.broadcasted_iota(jnp.int32, (1, CH, nst), 2) < S5_P

    n_re = l1re - 1.0
    den = lr * lr + li * li
    co_re = (n_re * lr + l1im * li) / den
    co_im = (l1im * lr - n_re * li) / den
    btr = btr_ref[...]
    bti = bti_ref[...]
    bb_re = co_re * btr - co_im * bti
    bb_im = co_re * bti + co_im * btr

    pe_re, pe_im = powers(jnp.where(fwd, (CH - 1) - row, row))
    for l in range(CH):
        pr = pe_re[:, l:l + 1, :]
        pi = pe_im[:, l:l + 1, :]
        ere_ref[:, l * S5_H:(l + 1) * S5_H, :] = pr * bb_re - pi * bb_im
        eim_ref[:, l * S5_H:(l + 1) * S5_H, :] = pr * bb_im + pi * bb_re

    cr = cr_ref[...]
    ci = ci_ref[...]
    pc_re, pc_im = powers(jnp.where(fwd, row + 1, CH - row))
    for j in range(CH):
        pr = pc_re[:, j:j + 1, :]
        pi = pc_im[:, j:j + 1, :]
        w_re = cr * pr - ci * pi
        w_im = cr * pi + ci * pr
        cp_ref[:, j * S5_H:(j + 1) * S5_H, :] = jnp.concatenate([w_re, -w_im], axis=2).astype(BF16)

    l16_ref[...] = jnp.concatenate(squares[4], axis=1)

    fwd2 = lax.broadcasted_iota(jnp.int32, (S5_H, nst), 1) < S5_P
    lane = lax.broadcasted_iota(jnp.int32, (S5_H, kk), 1)
    iblk = lane // S5_H
    hrow = lax.broadcasted_iota(jnp.int32, (S5_H, kk), 0)
    nt = (((1,), (1,)), ((), ()))
    for g in range(gb):
        ere = ere_ref[g]
        eim = eim_ref[g]
        e2 = jnp.concatenate([ere, eim], axis=1)
        be_ref[g] = e2.astype(BF16)
        e_hi, e_lo = _split3(e2)
        rhs = jnp.concatenate([e_hi, e_hi, e_lo], axis=1)
        crg = cr[g]
        cig = ci[g]

        c2 = jnp.concatenate([
            jnp.concatenate([jnp.where(fwd2, crg, 0.0), jnp.where(fwd2, -cig, 0.0)], axis=1),
            jnp.concatenate([jnp.where(fwd2, 0.0, crg), jnp.where(fwd2, 0.0, -cig)], axis=1)], axis=0)
        c_hi, c_lo = _split3(c2)
        lhs = jnp.concatenate([c_hi, c_lo, c_hi], axis=1)
        kfb = lax.dot_general(lhs, rhs, nt, preferred_element_type=F32)
        kf = kfb[0:S5_H]
        kb = kfb[S5_H:2 * S5_H]
        dg = d_ref[g]
        for j in range(CH):
            sf = (kk - (CH - 1 - j) * S5_H) % kk
            rf = pltpu.roll(kf, sf, 1) if sf else kf
            rb = pltpu.roll(kb, j * S5_H, 1) if j else kb
            blk = (jnp.where(iblk <= j, rf, 0.0) + jnp.where(iblk >= j, rb, 0.0)
                   + jnp.where(lane == j * S5_H + hrow, dg, 0.0))
            mt_ref[g, j * S5_H:(j + 1) * S5_H, :] = blk.astype(BF16)


def _prologue(cond, ada_w, ada_b, lam_re, lam_im, log_step, b_re, b_im, c_re, c_im, d_skip, x, pos, ctx):
    n = lam_re.shape[0] * S5_G
    gb = 8
    nst = 2 * S5_P
    kk = CH * S5_H

    def fb(v):
        return jnp.concatenate([v[:, 0], v[:, 1]], axis=-1).reshape(n, v.shape[3], nst)

    lr = fb(lam_re[:, :, :, None, :])
    li = fb(lam_im[:, :, :, None, :])
    ls = fb(jnp.broadcast_to(log_step[:, :, :, None, None], log_step.shape + (1, S5_P)))
    btr = fb(jnp.swapaxes(b_re, -1, -2))
    bti = fb(jnp.swapaxes(b_im, -1, -2))
    cr = fb(c_re)
    ci = fb(c_im)
    d = d_skip.reshape(n, S5_H, 1)

    depth = ada_w.shape[0]
    n_ada = depth * 3
    n_tab = n // gb
    bsz, t, _ = x.shape
    tc = ctx.shape[1]
    ct = 8
    tm = ct * CH
    ntl = t // tm
    n_str = (t + tc) // tm

    def spec(r, c):
        return pl.BlockSpec((gb, r, c), lambda s: (jnp.minimum(s, n_tab - 1), 0, 0))

    def ada_idx(s):
        s = jnp.minimum(s, n_ada - 1)
        return s // 3, s % 3

    def lat_tile(s):
        return jnp.minimum(s, ntl - 1)

    return pl.pallas_call(
        functools.partial(_prologue_kernel, gb=gb, n_ada=n_ada, n_tab=n_tab, ntl=ntl),
        grid=(max(n_ada, n_tab, n_str),),
        in_specs=[pl.BlockSpec((16, D), lambda s: (0, 0)),
                  pl.BlockSpec((None, D, D), lambda s: (ada_idx(s)[0], 0, ada_idx(s)[1])),
                  pl.BlockSpec((None, None, 1, D), lambda s: ada_idx(s) + (0, 0)),
                  spec(1, nst), spec(1, nst), spec(1, nst), spec(S5_H, nst), spec(S5_H, nst),
                  spec(S5_H, nst), spec(S5_H, nst), spec(S5_H, 1),
                  pl.BlockSpec((bsz, tm, D), lambda s: (0, lat_tile(s), 0)),
                  pl.BlockSpec((tm, D), lambda s: (lat_tile(s), 0)),
                  pl.BlockSpec((bsz, tm, D), lambda s: (0, jnp.clip(s - ntl, 0, tc // tm - 1), 0),
                               pipeline_mode=pl.Buffered(1))],
        out_specs=[pl.BlockSpec((None, None, 16, D), lambda s: ada_idx(s) + (0, 0)),
                   spec(kk, kk), spec(kk, 2 * nst), spec(kk, 2 * nst), spec(2, nst),
                   pl.BlockSpec((ct, bsz, CH * D), lambda s: (jnp.minimum(s, n_str - 1), 0, 0))],
        out_shape=[jax.ShapeDtypeStruct((depth, 3, 16, D), F32),
                   jax.ShapeDtypeStruct((n, kk, kk), BF16),
                   jax.ShapeDtypeStruct((n, kk, 2 * nst), BF16),
                   jax.ShapeDtypeStruct((n, kk, 2 * nst), BF16),
                   jax.ShapeDtypeStruct((n, 2, nst), F32),
                   jax.ShapeDtypeStruct(((t + tc) // CH, bsz, CH * D), F32)],
        scratch_shapes=[pltpu.VMEM((gb, kk, nst), F32)] * 2,
        compiler_params=_cparams(("arbitrary",)),
    )(cond, ada_w, ada_b.reshape(depth, 3, 1, D), lr, li, ls, btr, bti, cr, ci, d, x, pos, ctx)


def _even_in_kernel(x_ref, shift_ref, scale_ref, g_ref, w_ref, wcs_ref, ut_ref, xw_ref, z_ref,
                    wat_ref, wbz_ref, *scr, ncl):
    @pl.when(pl.program_id(0) == 0)
    def _():
        wat_ref[...] = w_ref[:, :S5_W].T.astype(BF16)
        wbz_ref[...] = w_ref[:, S5_W:].astype(BF16)

    x = x_ref[...]
    nc, nb, _ = x.shape
    rows = nc * nb
    gain = g_ref[...] * (1.0 + scale_ref[...])
    h = jnp.concatenate([_mod_norm(x[0:ncl], gain[0:nb][None], shift_ref[0:nb][None]),
                         _mod_norm(x[ncl:nc], gain[nb:nb + 1][None], shift_ref[nb:nb + 1][None])], axis=0)
    hb = h.reshape(rows, D).astype(BF16)
    pt = lax.dot_general(wat_ref[...], hb, (((1,), (1,)), ((), ())), preferred_element_type=F32)
    ut_ref[...] = pt.astype(BF16).reshape(S5_G, S5_H, rows)
    p = jnp.dot(hb, wbz_ref[...], preferred_element_type=F32)
    z_ref[...] = _silu(p[:, FN_W:]).astype(BF16)
    xw = jnp.dot(p[:, :FN_W].astype(BF16), wcs_ref[...], preferred_element_type=F32)
    for q in range(4):
        scr[q][...] = xw[:, q * LANE:(q + 1) * LANE]
    for q in range(4):
        part, half = divmod(q, 2)
        for bi in range(nb):
            piece = scr[q][pl.ds(bi, nc, stride=nb), :]
            lo = bi * FN_W + half * LANE
            xw_ref[part, :, lo:lo + LANE] = piece.astype(BF16)


def _even_in(xs, ncl, ada, l, g, w_in, layer, wcs):
    nc, bsz, _ = xs.shape
    rows = nc * bsz
    return pl.pallas_call(
        functools.partial(_even_in_kernel, ncl=ncl),
        grid=(CH,),
        in_specs=[pl.BlockSpec((nc, bsz, D), lambda i: (0, 0, i)),
                  pl.BlockSpec((None, None) + ada.shape[2:], lambda i: (l, 0, 0, 0)),
                  pl.BlockSpec((None, None) + ada.shape[2:], lambda i: (l, 1, 0, 0)),
                  pl.BlockSpec((None, 1, D), lambda i: (l, 0, 0)),
                  pl.BlockSpec((None, D, w_in.shape[2]), lambda i: (layer, 0, 0),
                               pipeline_mode=pl.Buffered(1)),
                  pl.BlockSpec((FN_W, 2 * FN_W), lambda i: (0, 0))],
        out_specs=[pl.BlockSpec((S5_G, S5_H, rows), lambda i: (0, i, 0)),
                   pl.BlockSpec((2, None, nc, bsz * FN_W), lambda i: (0, i, 0, 0)),
                   pl.BlockSpec((None, rows, MIX), lambda i: (i, 0, 0))],
        out_shape=[jax.ShapeDtypeStruct((S5_G, CH * S5_H, rows), BF16),
                   jax.ShapeDtypeStruct((2, CH, nc, bsz * FN_W), BF16),
                   jax.ShapeDtypeStruct((CH, rows, MIX), BF16)],
        scratch_shapes=[pltpu.VMEM((S5_W, D), BF16), pltpu.VMEM((D, FN_W + MIX), BF16)]
        + [pltpu.VMEM((rows, LANE), F32)] * 4,
        compiler_params=_cparams(("arbitrary",)),
    )(xs, ada, ada, g, w_in, wcs)


def _s5_kernel(ut_ref, mt_ref, be_ref, cp_ref, l16_ref, yt_ref,
               sre_ref, sim_ref, are_ref, aim_ref, bre_ref, bim_ref, *, bsz, ncl, ncc, gs):
    nl = bsz * ncl
    nst = 2 * S5_P
    for g in range(gs):
        st = lax.dot_general(ut_ref[g], be_ref[g], (((0,), (0,)), ((), ())), preferred_element_type=F32)
        sre_ref[g] = st[:, :nst]
        sim_ref[g] = st[:, nst:]

    lam = [l16_ref[g] for g in range(gs)]
    is_fwd = lax.broadcasted_iota(jnp.int32, (bsz, nst), 1) < S5_P

    def make_step(base, nchunk):
        def step(c, carry):
            rf = pl.ds(pl.multiple_of(base + c * bsz, bsz), bsz)
            rb = pl.ds(pl.multiple_of(base + (nchunk - 1 - c) * bsz, bsz), bsz)
            out = []
            for g in range(gs):
                sre, sim = carry[2 * g], carry[2 * g + 1]
                lre = lam[g][0:1, :]
                lim = lam[g][1:2, :]
                are_ref[g, rf, :] = sre
                aim_ref[g, rf, :] = sim
                bre_ref[g, rb, :] = sre
                bim_ref[g, rb, :] = sim
                in_re = jnp.where(is_fwd, sre_ref[g, rf, :], sre_ref[g, rb, :])
                in_im = jnp.where(is_fwd, sim_ref[g, rf, :], sim_ref[g, rb, :])
                out.append(lre * sre - lim * sim + in_re)
                out.append(lre * sim + lim * sre + in_im)
            return tuple(out)
        return step

    zero = jnp.zeros((bsz, nst), F32)
    carry = lax.fori_loop(0, ncc, make_step(nl, ncc), (zero,) * (2 * gs))
    lax.fori_loop(0, ncl, make_step(0, ncl), carry)

    nt = (((1,), (1,)), ((), ()))
    fwd_rows = lax.broadcasted_iota(jnp.int32, (bsz * (ncl + ncc), nst), 1) < S5_P
    for g in range(gs):
        s0 = jnp.concatenate([jnp.where(fwd_rows, are_ref[g], bre_ref[g]),
                              jnp.where(fwd_rows, aim_ref[g], bim_ref[g])], axis=1).astype(BF16)
        yt_ref[g] = (jnp.dot(mt_ref[g], ut_ref[g], preferred_element_type=F32)
                     + lax.dot_general(cp_ref[g], s0, nt, preferred_element_type=F32)).astype(BF16)


def _s5_mix(ut, ncl, mt, bend, cp, lam16, bsz, layer):
    rows = ut.shape[2]
    kk = CH * S5_H
    nst = 2 * S5_P
    gs = 6
    off = layer * (S5_G // gs)

    def gspec(r, c):
        return pl.BlockSpec((gs, r, c), lambda g: (g, 0, 0))

    def tspec(r, c):
        return pl.BlockSpec((gs, r, c), lambda g: (g + off, 0, 0))

    return pl.pallas_call(
        functools.partial(_s5_kernel, bsz=bsz, ncl=ncl, ncc=rows // bsz - ncl, gs=gs),
        grid=(S5_G // gs,),
        in_specs=[gspec(kk, rows), tspec(kk, kk), tspec(kk, 2 * nst), tspec(kk, 2 * nst), tspec(2, nst)],
        out_specs=gspec(kk, rows),
        out_shape=jax.ShapeDtypeStruct((S5_G, kk, rows), BF16),
        scratch_shapes=[pltpu.VMEM((gs, rows, nst), F32)] * 6,
        compiler_params=_cparams(("arbitrary",)),
    )(ut, mt, bend, cp, lam16)


def _fnet_weights_kernel(ccs_ref, fw_ref, o_ref):
    fw = fw_ref[...]
    ccs = ccs_ref[...]
    wc = jnp.dot(ccs[:, :FN_W], fw, precision=HI, preferred_element_type=F32)
    ws = jnp.dot(ccs[:, FN_W:], fw, precision=HI, preferred_element_type=F32)
    o_ref[...] = jnp.concatenate([wc, ws], axis=1).astype(BF16)


def _fnet_weights(fw_bd):
    return pl.pallas_call(
        _fnet_weights_kernel,
        out_shape=jax.ShapeDtypeStruct((FN_W, 2 * FN_W), BF16),
    )(_dft_channel_matrix(), fw_bd)


def _fnet_kernel(tab_ref, v_ref, jr_ref, o_ref, *, ipt):
    r = pl.program_id(1)
    tab = tab_ref[...].astype(BF16)
    v = v_ref[...]
    nc, n = v.shape[2], v.shape[3]
    t = CH * nc
    a = jnp.dot(tab[:, :t], v[0].reshape(t, n), preferred_element_type=F32)
    b = jnp.dot(tab[:, t:], v[1].reshape(t, n), preferred_element_type=F32)
    o_ref[pl.ds(r * ipt, ipt)] = (a - b).reshape(ipt, nc, n).astype(BF16)
    m = (a + b).astype(BF16)
    for k in range(ipt):
        i = r * ipt + k

        @pl.when(jnp.logical_and(i >= 1, i <= CH // 2 - 1))
        def _():
            o_ref[CH - i] = jnp.dot(jr_ref[...], m[k * nc:(k + 1) * nc],
                                    preferred_element_type=F32).astype(BF16)


def _dft_half_table(nc):
    t = nc * CH
    cols = (np.arange(nc)[None, :] * CH + np.arange(CH)[:, None]).reshape(-1)
    rows = (np.arange(nc)[None, :] * CH + np.arange(CH // 2 + 1)[:, None]).reshape(-1)
    prod = (rows[:, None].astype(np.int64) * cols[None, :].astype(np.int64)) % t
    ang = prod.astype(np.float64) * (2.0 * np.pi / t)
    scale = 1.0 / math.sqrt(t * FN_GW)
    return jnp.asarray(np.concatenate([np.cos(ang), np.sin(ang)], axis=1) * scale, dtype=F32)


def _dft_channel_matrix():
    c = np.arange(FN_GW)
    ang = (c[:, None] * c[None, :] % FN_GW).astype(np.float64) * (2.0 * np.pi / FN_GW)
    eye = np.eye(FN_G)
    return jnp.asarray(np.concatenate([np.kron(eye, np.cos(ang)), np.kron(eye, np.sin(ang))], axis=1),
                       dtype=F32)


def _fnet(xw, c0, nc, bsz):
    t = nc * CH
    nhb = CH // 2 + 1
    ipt = 3
    bpb = 4
    jr = jnp.asarray(np.eye(nc)[::-1], dtype=BF16)
    return pl.pallas_call(
        functools.partial(_fnet_kernel, ipt=ipt),
        grid=(bsz // bpb, nhb // ipt),
        in_specs=[pl.BlockSpec((ipt * nc, 2 * t), lambda b, r: (r, 0)),
                  pl.BlockSpec((2, CH, nc, bpb * FN_W), lambda b, r: (0, 0, c0 // nc, b)),
                  pl.BlockSpec((nc, nc), lambda b, r: (0, 0))],
        out_specs=pl.BlockSpec((CH, nc, bpb * FN_W), lambda b, r: (0, 0, b)),
        out_shape=jax.ShapeDtypeStruct((CH, nc, bsz * FN_W), BF16),
        compiler_params=_cparams(("arbitrary", "arbitrary")),
    )(_dft_half_table(nc), xw, jr)


def _even_out_kernel(*refs, ncl, with_ctx):
    it = iter(refs)
    yt_ref, ybl_ref = next(it), next(it)
    ybc_ref = next(it) if with_ctx else None
    z_ref, x_ref, gate_ref, gwf_ref, gb_ref, wof_ref, o_ref = (next(it), next(it), next(it), next(it),
                                                               next(it), next(it), next(it))
    scr = [next(it), next(it)]
    gw_ref, wo_ref = next(it), next(it)

    @pl.when(pl.program_id(0) == 0)
    def _():
        gw_ref[...] = (0.5 * gwf_ref[...]).astype(BF16)
        wo_ref[...] = wof_ref[...].astype(BF16)

    rows = yt_ref.shape[2]
    nc, nb, _ = x_ref.shape
    for half in range(2):
        for bi in range(nb):
            lo = bi * FN_W + half * LANE
            scr[half][pl.ds(bi, ncl, stride=nb), :] = ybl_ref[:, lo:lo + LANE].astype(F32)
            if with_ctx:
                scr[half][pl.ds(ncl * nb + bi, nc - ncl, stride=nb), :] = (
                    ybc_ref[:, lo:lo + LANE].astype(F32))

    for r0 in range(0, rows, ROW_BLOCK):
        rb = min(ROW_BLOCK, rows - r0)
        yt = yt_ref[:, :, r0:r0 + rb]
        ya = _gelu_tanh(yt.astype(F32).reshape(S5_W, rb).T)
        half_g = jnp.dot(ya.astype(BF16), gw_ref[...], preferred_element_type=F32) + 0.5 * gb_ref[...]
        hy = 0.5 * ya
        ya = hy + hy * jnp.tanh(half_g)
        yb = jnp.concatenate([scr[0][r0:r0 + rb, :], scr[1][r0:r0 + rb, :]], axis=1)
        sz = z_ref[r0:r0 + rb, :]
        ma = (ya * sz[:, :S5_W]).astype(BF16)
        mb = (yb * sz[:, S5_W:]).astype(BF16)
        out = (jnp.dot(ma, wo_ref[0:S5_W, :], preferred_element_type=F32)
               + jnp.dot(mb, wo_ref[S5_W:MIX, :], preferred_element_type=F32))
        ca, cb = r0 // nb, (r0 + rb) // nb
        out = out.reshape(cb - ca, nb, D)
        cm = min(cb, ncl)
        if ca < cm:
            o_ref[ca:cm] = x_ref[ca:cm] + gate_ref[0:nb][None] * out[0:cm - ca]
        if cm < cb:
            lo = max(ca, ncl)
            o_ref[lo:cb] = x_ref[lo:cb] + gate_ref[nb:nb + 1][None] * out[lo - ca:cb - ca]


def _even_out(yt, ybl, ybc, z, xs, ncl, ada, l, glu_w, glu_b, w_out, layer):
    bsz = xs.shape[1]
    with_ctx = ybc is not None
    nc = xs.shape[0] if with_ctx else ncl
    rows = nc * bsz
    args = [yt, ybl]
    specs = [pl.BlockSpec((S5_G, S5_H, rows), lambda j: (0, j, 0)),
             pl.BlockSpec((None, ncl, bsz * FN_W), lambda j: (j, 0, 0))]
    if with_ctx:
        args.append(ybc)
        specs.append(pl.BlockSpec((None, nc - ncl, bsz * FN_W), lambda j: (j, 0, 0)))
    args += [z, xs, ada, glu_w, glu_b, w_out]
    specs += [pl.BlockSpec((None, rows, MIX), lambda j: (j, 0, 0)),
              pl.BlockSpec((nc, bsz, D), lambda j: (0, 0, j)),
              pl.BlockSpec((None, None) + ada.shape[2:], lambda j: (l, 2, 0, 0)),
              pl.BlockSpec((None, S5_W, S5_W), lambda j: (layer, 0, 0), pipeline_mode=pl.Buffered(1)),
              pl.BlockSpec((None, 1, S5_W), lambda j: (layer, 0, 0)),
              pl.BlockSpec((None, MIX, D), lambda j: (layer, 0, 0), pipeline_mode=pl.Buffered(1))]
    return pl.pallas_call(
        functools.partial(_even_out_kernel, ncl=ncl, with_ctx=with_ctx),
        grid=(CH,),
        in_specs=specs,
        out_specs=pl.BlockSpec((nc, bsz, D), lambda j: (0, 0, j)),
        out_shape=jax.ShapeDtypeStruct((nc, bsz, CH * D), F32),
        scratch_shapes=[pltpu.VMEM((rows, LANE), F32)] * 2 + [pltpu.VMEM((S5_W, S5_W), BF16),
                                                             pltpu.VMEM((MIX, D), BF16)],
        compiler_params=_cparams(("arbitrary",)),
    )(*args)


def _odd_kernel(*refs, ct, ntl, t_lat, t_ctx, final):
    it = iter(refs)
    xm_ref, xn_ref = next(it), next(it)
    shift_ref, scale_ref, gate_ref, g_ref = next(it), next(it), next(it), next(it)
    wif_ref, wof_ref, pwf_ref, ps_ref, cw_ref = next(it), next(it), next(it), next(it), next(it)
    fg_ref = next(it) if final else None
    o_ref = next(it)
    h_ref = next(it)
    pe_ref = next(it)
    tail_ref = next(it)
    wi_ref, wo_ref, pw_ref = next(it), next(it), next(it)
    fin_ref = next(it) if final else None

    ti = pl.program_id(0)
    is_ctx = ti >= ntl
    t0 = jnp.where(is_ctx, ti - ntl, ti) * (ct * CH)
    t_total = jnp.where(is_ctx, t_ctx, t_lat)

    @pl.when(ti == 0)
    def _():
        wi_ref[...] = wif_ref[...].astype(BF16)
        wo_ref[...] = wof_ref[...].astype(BF16)
        pw_ref[...] = pwf_ref[...].astype(BF16)
        tail_ref[...] = jnp.zeros(tail_ref.shape, F32)

    n1 = POOL_W + 2 * CONV_W
    nb = xm_ref.shape[1]
    tm = ct * CH
    ne = tm + 2 * HALO

    def rows_of(ref):
        return jnp.where(is_ctx, ref[nb:nb + 1], ref[0:nb])

    gain = g_ref[...] * (1.0 + rows_of(scale_ref))
    shift = rows_of(shift_ref)

    def hn(xv):
        return _mod_norm(xv, gain, shift)

    for i in range(CH):
        h_ref[0:ct, i] = hn(xm_ref[:, :, i * D:(i + 1) * D])
    for i in range(HALO):
        h_ref[ct, i] = hn(xn_ref[:, :, i * D:(i + 1) * D])[0]

    m0 = HALO * nb
    mr = tm * nb
    he = h_ref[...].reshape((ct + 1) * CH * nb, D)[0:mr + m0].astype(BF16)
    pe = jnp.dot(he, wi_ref[:, 0:n1], preferred_element_type=F32)
    uc = pe[:, :POOL_W]
    v = pe[:, POOL_W:POOL_W + CONV_W] * pe[:, POOL_W + CONV_W:]
    pe_ref[m0:m0 + mr, 0:POOL_W] = uc[0:mr]
    pe_ref[m0:m0 + mr, POOL_W:POOL_W + CONV_W] = v[0:mr]
    valid = (t0 + tm) < t_total
    pe_ref[m0 + mr:, 0:POOL_W] = jnp.where(valid, uc[mr:], 0.0)
    pe_ref[m0 + mr:, POOL_W:POOL_W + CONV_W] = jnp.where(valid, v[mr:], 0.0)
    pe_ref[0:m0, :] = jnp.where(t0 > 0, tail_ref[...], 0.0)

    p2 = jnp.dot(he[0:mr], wi_ref[:, n1:], preferred_element_type=F32)
    b_gate = p2[:, :CONV_W]
    sz = _silu(p2[:, CONV_W:])

    tpos = t0 + lax.broadcasted_iota(jnp.int32, (mr, 1), 0) // nb
    pooled = []
    for gi, w in enumerate(POOL_WINDOWS):
        c0 = gi * POOL_GW
        s = pe_ref[:, c0:c0 + POOL_GW]
        n = ne
        width = 1
        while width < w:
            s = s[0:(n - width) * nb] + s[width * nb:n * nb]
            n -= width
            width *= 2
        start = (HALO - w // 2) * nb
        total = s[start:start + mr]
        hi = jnp.minimum(tpos + w // 2, t_total)
        lo = jnp.maximum(tpos - w // 2, 0)
        inv_cnt = 1.0 / (hi - lo).astype(F32)
        centre = pe_ref[m0:m0 + mr, c0:c0 + POOL_GW]
        pg = total * inv_cnt - centre
        pooled.append(jnp.dot(pg.astype(BF16), pw_ref[gi], preferred_element_type=F32))
    y_c = jnp.concatenate(pooled, axis=1) * ps_ref[...]

    cwt = cw_ref[...]
    vm = pe_ref[m0 - nb:m0 - nb + mr, POOL_W:POOL_W + CONV_W]
    v0 = pe_ref[m0:m0 + mr, POOL_W:POOL_W + CONV_W]
    vp = pe_ref[m0 + nb:m0 + nb + mr, POOL_W:POOL_W + CONV_W]
    y_d = b_gate * (vm * cwt[0:1, :] + v0 * cwt[1:2, :] + vp * cwt[2:3, :])

    y = (jnp.concatenate([y_c, y_d], axis=1) * sz).astype(BF16)
    out = jnp.dot(y, wo_ref[...], preferred_element_type=F32)
    go = (rows_of(gate_ref) * out.reshape(tm, nb, D)).reshape(ct, CH, nb, D)
    for i in range(CH):
        xo = xm_ref[:, :, i * D:(i + 1) * D] + go[:, i]
        if not final:
            o_ref[:, :, i * D:(i + 1) * D] = xo
            continue
        ms = jnp.mean(xo * xo, axis=-1, keepdims=True)
        fin_ref[:, :, i * D:(i + 1) * D] = xo * lax.rsqrt(ms + EPS) * fg_ref[...]
    tail_ref[...] = pe_ref[mr:mr + m0, :]
    if final:
        o_ref[...] = pltpu.einshape("cb(id)->b(ci)d", fin_ref[...], i=CH)


def _odd_layer(xs, ncl, with_ctx, ada, l, g, w_in, w_out, pool_w, layer, pool_scale, conv_w, final_g):
    nc_all, bsz, _ = xs.shape
    ct = 8
    ntl = ncl // ct
    nc = nc_all if with_ctx else ncl
    final = final_g is not None
    args = [xs, xs, ada, ada, ada, g, w_in, w_out, pool_w, pool_scale, conv_w]

    def vec(j):
        return pl.BlockSpec((None, None) + ada.shape[2:], lambda i: (l, j, 0, 0))

    def const(*shape):
        return pl.BlockSpec(shape, lambda i: (0,) * len(shape), pipeline_mode=pl.Buffered(1))

    def stacked(*shape):
        return pl.BlockSpec((None,) + shape, lambda i: (layer,) + (0,) * len(shape),
                            pipeline_mode=pl.Buffered(1))

    specs = [pl.BlockSpec((ct, bsz, CH * D), lambda i: (i, 0, 0)),
             pl.BlockSpec((1, bsz, CH * D), lambda i: (jnp.minimum((i + 1) * ct, nc_all - 1), 0, 0)),
             vec(0), vec(1), vec(2),
             pl.BlockSpec((None, 1, D), lambda i: (l, 0, 0)), stacked(*w_in.shape[1:]), stacked(MIX, D),
             stacked(len(POOL_WINDOWS), POOL_GW, POOL_GW), stacked(1, POOL_W), stacked(3, CONV_W)]
    scratch = [pltpu.VMEM((ct + 1, CH, bsz, D), F32),
               pltpu.VMEM(((ct + 1) * CH * bsz, POOL_W + CONV_W), F32),
               pltpu.VMEM((HALO * bsz, POOL_W + CONV_W), F32),
               pltpu.VMEM(w_in.shape[1:], BF16), pltpu.VMEM((MIX, D), BF16),
               pltpu.VMEM((len(POOL_WINDOWS), POOL_GW, POOL_GW), BF16)]
    if final:
        args.append(final_g.reshape(1, D))
        specs.append(const(1, D))
        scratch.append(pltpu.VMEM((ct, bsz, CH * D), F32))
        out_spec = pl.BlockSpec((bsz, ct * CH, D), lambda i: (0, i, 0))
        out_shape = jax.ShapeDtypeStruct((bsz, nc * CH, D), F32)
    else:
        out_spec = pl.BlockSpec((ct, bsz, CH * D), lambda i: (i, 0, 0))
        out_shape = jax.ShapeDtypeStruct((nc, bsz, CH * D), F32)
    return pl.pallas_call(
        functools.partial(_odd_kernel, ct=ct, ntl=ntl, t_lat=ncl * CH, t_ctx=(nc_all - ncl) * CH,
                          final=final),
        grid=(nc // ct,),
        in_specs=specs,
        out_specs=out_spec,
        out_shape=out_shape,
        scratch_shapes=scratch,
        compiler_params=_cparams(("arbitrary",)),
    )(*args)


def _sincos_table(n_tok, dim):
    rows = n_tok // GRID_W
    rr, cc = np.meshgrid(np.arange(rows, dtype=np.float64), np.arange(GRID_W, dtype=np.float64),
                         indexing='ij')
    rr = rr.reshape(-1, 1)
    cc = cc.reshape(-1, 1)
    quarter = dim // 4
    omega = POS_BASE ** (-np.arange(quarter, dtype=np.float64) / quarter)
    tab = np.concatenate([np.sin(rr * omega), np.cos(rr * omega), np.sin(cc * omega), np.cos(cc * omega)],
                         axis=-1)
    return jnp.asarray(tab, dtype=F32)


def _block_diag(w):
    g, c, _ = w.shape
    eye = jnp.eye(g, dtype=w.dtype)
    return (eye[:, None, :, None] * w[:, :, None, :]).reshape(g * c, g * c)


def kernel(x, c, ctx, c_ctx, norm_g, ada_w, ada_b, even_w_in, even_w_out, s5_lam_re, s5_lam_im, s5_log_step, s5_b_re, s5_b_im, s5_c_re, s5_c_im, s5_d, s5_glu_w, s5_glu_b, fnet_w, odd_w_in, odd_w_out, pool_w, pool_scale, conv_w, final_g):
    bsz, n_tok, _ = x.shape
    depth = norm_g.shape[0]
    ncl = n_tok // CH
    ncc = ctx.shape[1] // CH

    cond = jnp.concatenate([c, jnp.broadcast_to(c_ctx[None], (16 - bsz, D))], axis=0)
    ada, mt, bend, cp, lam16, xs = _prologue(cond, ada_w, ada_b, s5_lam_re, s5_lam_im, s5_log_step, s5_b_re,
                                             s5_b_im, s5_c_re, s5_c_im, s5_d, x, _sincos_table(n_tok, D), ctx)

    need_ctx = [any(j % 2 == 0 for j in range(l + 1, depth)) for l in range(depth)]
    norm_g3 = norm_g.reshape(depth, 1, D)
    glu_b3 = s5_glu_b.reshape(-1, 1, S5_W)
    pool_scale3 = pool_scale.reshape(-1, 1, POOL_W)

    for l in range(depth):
        i = l // 2
        last = l == depth - 1
        if l % 2 == 0:
            wcs = _fnet_weights(_block_diag(fnet_w[i]))
            ut, xw, z = _even_in(xs, ncl, ada, l, norm_g3, even_w_in, i, wcs)
            yt = _s5_mix(ut, ncl, mt, bend, cp, lam16, bsz, i)
            ybl = _fnet(xw, 0, ncl, bsz)
            ybc = _fnet(xw, ncl, ncc, bsz) if need_ctx[l] else None
            xs = _even_out(yt, ybl, ybc, z, xs, ncl, ada, l, s5_glu_w, glu_b3, even_w_out, i)
        else:
            xs = _odd_layer(xs, ncl, need_ctx[l], ada, l, norm_g3, odd_w_in, odd_w_out, pool_w, i,
                            pool_scale3, conv_w, final_g if last else None)
    if depth % 2 == 1:
        raise NotImplementedError("final norm and (B, T, D) order are produced by the last (odd) layer")
    return xs
```

```python
import functools
import math

import numpy as np
import jax
import jax.numpy as jnp
from jax import lax
from jax.experimental import pallas as pl
from jax.experimental.pallas import tpu as pltpu

D = 1024
MIX = 1024
S5_W = 768
FN_W = 256
S5_H = 16
S5_G = 48
S5_P = 64
FN_G = 4
FN_GW = 64
POOL_W = 512
CONV_W = 512
POOL_WINDOWS = (2, 4, 8, 16)
POOL_GW = 128
GRID_W = 64
EPS = 1e-6
POS_BASE = 10000.0
CH = 16
HALO = 8
LANE = 128
ROW_BLOCK = 256
VMEM_LIMIT = 56 * 1024 * 1024

F32 = jnp.float32
BF16 = jnp.bfloat16
HI = lax.Precision.HIGHEST


def _cparams(sem):
    return pltpu.CompilerParams(dimension_semantics=sem, vmem_limit_bytes=VMEM_LIMIT)


def _silu(v):
    h = 0.5 * v
    return h + h * jnp.tanh(h)


def _gelu_tanh(v):
    c = math.sqrt(2.0 / math.pi)
    h = 0.5 * v
    return h + h * jnp.tanh(v * (c + (c * 0.044715) * (v * v)))


def _mod_norm(x, gain, shift):
    ms = jnp.mean(x * x, axis=-1, keepdims=True)
    return x * lax.rsqrt(ms + EPS) * gain + shift


def _split3(v):
    hi = v.astype(BF16)
    lo = (v - hi.astype(F32)).astype(BF16)
    return hi, lo


def _ada_item(c_ref, w_ref, b_ref, ada_ref):
    s_hi, s_lo = _split3(_silu(c_ref[...]))
    w_hi, w_lo = _split3(w_ref[...])
    ada_ref[...] = (jnp.dot(s_hi, w_hi, preferred_element_type=F32)
                    + jnp.dot(s_hi, w_lo, preferred_element_type=F32)
                    + jnp.dot(s_lo, w_hi, preferred_element_type=F32) + b_ref[...])


def _stream_item(x_ref, pos_ref, c_ref, o_ref, step, ntl):
    @pl.when(step < ntl)
    def _():
        o_ref[...] = pltpu.einshape("b(ci)d->cb(id)", x_ref[...] + pos_ref[...][None], i=CH)

    @pl.when(step >= ntl)
    def _():
        o_ref[...] = pltpu.einshape("b(ci)d->cb(id)", c_ref[...], i=CH)


def _prologue_kernel(c_ref, w_ref, b_ref, lr_ref, li_ref, ls_ref, btr_ref, bti_ref, cr_ref, ci_ref, d_ref,
                     x_ref, pos_ref, ctx_ref, ada_ref, mt_ref, be_ref, cp_ref, l16_ref, xs_ref,
                     ere_ref, eim_ref, *, gb, n_ada, n_tab, ntl):
    step = pl.program_id(0)

    @pl.when(step < n_ada)
    def _():
        _ada_item(c_ref, w_ref, b_ref, ada_ref)

    @pl.when(step < n_tab)
    def _():
        _tables_item(lr_ref, li_ref, ls_ref, btr_ref, bti_ref, cr_ref, ci_ref, d_ref,
                     mt_ref, be_ref, cp_ref, l16_ref, ere_ref, eim_ref, gb)

    _stream_item(x_ref, pos_ref, ctx_ref, xs_ref, step, ntl)


def _tables_item(lr_ref, li_ref, ls_ref, btr_ref, bti_ref, cr_ref, ci_ref, d_ref,
                 mt_ref, be_ref, cp_ref, l16_ref, ere_ref, eim_ref, gb):
    nst = 2 * S5_P
    kk = CH * S5_H
    step = jnp.exp(ls_ref[...])
    lr = lr_ref[...]
    li = li_ref[...]
    a = lr * step
    b = li * step

    mag = jnp.exp(a)
    l1re = mag * jnp.cos(b)
    l1im = mag * jnp.sin(b)
    squares = [(l1re, l1im)]
    for _ in range(4):
        sr, si = squares[-1]
        squares.append((sr * sr - si * si, 2.0 * (sr * si)))

    def powers(expo):
        pr = jnp.ones((gb,) + expo.shape[1:], F32)
        pi = jnp.zeros((gb,) + expo.shape[1:], F32)
        for bit, (sr, si) in enumerate(squares):
            on = ((expo >> bit) & 1) == 1
            pr, pi = jnp.where(on, pr * sr - pi * si, pr), jnp.where(on, pr * si + pi * sr, pi)
        return pr, pi

    row = lax.broadcasted_iota(jnp.int32, (1, CH, nst), 1)
    fwd = lax.broadcasted_iota(jnp.int32, (1, CH, nst), 2) < S5_P

    n_re = l1re - 1.0
    den = lr * lr + li * li
    co_re = (n_re * lr + l1im * li) / den
    co_im = (l1im * lr - n_re * li) / den
    btr = btr_ref[...]
    bti = bti_ref[...]
    bb_re = co_re * btr - co_im * bti
    bb_im = co_re * bti + co_im * btr

    pe_re, pe_im = powers(jnp.where(fwd, (CH - 1) - row, row))
    for l in range(CH):
        pr = pe_re[:, l:l + 1, :]
        pi = pe_im[:, l:l + 1, :]
        ere_ref[:, l * S5_H:(l + 1) * S5_H, :] = pr * bb_re - pi * bb_im
        eim_ref[:, l * S5_H:(l + 1) * S5_H, :] = pr * bb_im + pi * bb_re

    cr = cr_ref[...]
    ci = ci_ref[...]
    pc_re, pc_im = powers(jnp.where(fwd, row + 1, CH - row))
    for j in range(CH):
        pr = pc_re[:, j:j + 1, :]
        pi = pc_im[:, j:j + 1, :]
        w_re = cr * pr - ci * pi
        w_im = cr * pi + ci * pr
        cp_ref[:, j * S5_H:(j + 1) * S5_H, :] = jnp.concatenate([w_re, -w_im], axis=2).astype(BF16)

    l16_ref[...] = jnp.concatenate(squares[4], axis=1)

    fwd2 = lax.broadcasted_iota(jnp.int32, (S5_H, nst), 1) < S5_P
    lane = lax.broadcasted_iota(jnp.int32, (S5_H, kk), 1)
    iblk = lane // S5_H
    hrow = lax.broadcasted_iota(jnp.int32, (S5_H, kk), 0)
    nt = (((1,), (1,)), ((), ()))
    for g in range(gb):
        ere = ere_ref[g]
        eim = eim_ref[g]
        e2 = jnp.concatenate([ere, eim], axis=1)
        be_ref[g] = e2.astype(BF16)
        e_hi, e_lo = _split3(e2)
        rhs = jnp.concatenate([e_hi, e_hi, e_lo], axis=1)
        crg = cr[g]
        cig = ci[g]

        c2 = jnp.concatenate([
            jnp.concatenate([jnp.where(fwd2, crg, 0.0), jnp.where(fwd2, -cig, 0.0)], axis=1),
            jnp.concatenate([jnp.where(fwd2, 0.0, crg), jnp.where(fwd2, 0.0, -cig)], axis=1)], axis=0)
        c_hi, c_lo = _split3(c2)
        lhs = jnp.concatenate([c_hi, c_lo, c_hi], axis=1)
        kfb = lax.dot_general(lhs, rhs, nt, preferred_element_type=F32)
        kf = kfb[0:S5_H]
        kb = kfb[S5_H:2 * S5_H]
        dg = d_ref[g]
        for j in range(CH):
            sf = (kk - (CH - 1 - j) * S5_H) % kk
            rf = pltpu.roll(kf, sf, 1) if sf else kf
            rb = pltpu.roll(kb, j * S5_H, 1) if j else kb
            blk = (jnp.where(iblk <= j, rf, 0.0) + jnp.where(iblk >= j, rb, 0.0)
                   + jnp.where(lane == j * S5_H + hrow, dg, 0.0))
            mt_ref[g, j * S5_H:(j + 1) * S5_H, :] = blk.astype(BF16)


def _prologue(cond, ada_w, ada_b, lam_re, lam_im, log_step, b_re, b_im, c_re, c_im, d_skip, x, pos, ctx):
    n = lam_re.shape[0] * S5_G
    gb = 8
    nst = 2 * S5_P
    kk = CH * S5_H

    def fb(v):
        return jnp.concatenate([v[:, 0], v[:, 1]], axis=-1).reshape(n, v.shape[3], nst)

    lr = fb(lam_re[:, :, :, None, :])
    li = fb(lam_im[:, :, :, None, :])
    ls = fb(jnp.broadcast_to(log_step[:, :, :, None, None], log_step.shape + (1, S5_P)))
    btr = fb(jnp.swapaxes(b_re, -1, -2))
    bti = fb(jnp.swapaxes(b_im, -1, -2))
    cr = fb(c_re)
    ci = fb(c_im)
    d = d_skip.reshape(n, S5_H, 1)

    depth = ada_w.shape[0]
    n_ada = depth * 3
    n_tab = n // gb
    bsz, t, _ = x.shape
    tc = ctx.shape[1]
    ct = 8
    tm = ct * CH
    ntl = t // tm
    n_str = (t + tc) // tm

    def spec(r, c):
        return pl.BlockSpec((gb, r, c), lambda s: (jnp.minimum(s, n_tab - 1), 0, 0))

    def ada_idx(s):
        s = jnp.minimum(s, n_ada - 1)
        return s // 3, s % 3

    def lat_tile(s):
        return jnp.minimum(s, ntl - 1)

    return pl.pallas_call(
        functools.partial(_prologue_kernel, gb=gb, n_ada=n_ada, n_tab=n_tab, ntl=ntl),
        grid=(max(n_ada, n_tab, n_str),),
        in_specs=[pl.BlockSpec((16, D), lambda s: (0, 0)),
                  pl.BlockSpec((None, D, D), lambda s: (ada_idx(s)[0], 0, ada_idx(s)[1])),
                  pl.BlockSpec((None, None, 1, D), lambda s: ada_idx(s) + (0, 0)),
                  spec(1, nst), spec(1, nst), spec(1, nst), spec(S5_H, nst), spec(S5_H, nst),
                  spec(S5_H, nst), spec(S5_H, nst), spec(S5_H, 1),
                  pl.BlockSpec((bsz, tm, D), lambda s: (0, lat_tile(s), 0)),
                  pl.BlockSpec((tm, D), lambda s: (lat_tile(s), 0)),
                  pl.BlockSpec((bsz, tm, D), lambda s: (0, jnp.clip(s - ntl, 0, tc // tm - 1), 0),
                               pipeline_mode=pl.Buffered(1))],
        out_specs=[pl.BlockSpec((None, None, 16, D), lambda s: ada_idx(s) + (0, 0)),
                   spec(kk, kk), spec(kk, 2 * nst), spec(kk, 2 * nst), spec(2, nst),
                   pl.BlockSpec((ct, bsz, CH * D), lambda s: (jnp.minimum(s, n_str - 1), 0, 0))],
        out_shape=[jax.ShapeDtypeStruct((depth, 3, 16, D), F32),
                   jax.ShapeDtypeStruct((n, kk, kk), BF16),
                   jax.ShapeDtypeStruct((n, kk, 2 * nst), BF16),
                   jax.ShapeDtypeStruct((n, kk, 2 * nst), BF16),
                   jax.ShapeDtypeStruct((n, 2, nst), F32),
                   jax.ShapeDtypeStruct(((t + tc) // CH, bsz, CH * D), F32)],
        scratch_shapes=[pltpu.VMEM((gb, kk, nst), F32)] * 2,
        compiler_params=_cparams(("arbitrary",)),
    )(cond, ada_w, ada_b.reshape(depth, 3, 1, D), lr, li, ls, btr, bti, cr, ci, d, x, pos, ctx)


def _even_in_kernel(x_ref, shift_ref, scale_ref, g_ref, w_ref, wcs_ref, ut_ref, xw_ref, z_ref,
                    wat_ref, wbz_ref, *scr, ncl):
    @pl.when(pl.program_id(0) == 0)
    def _():
        wat_ref[...] = w_ref[:, :S5_W].T.astype(BF16)
        wbz_ref[...] = w_ref[:, S5_W:].astype(BF16)

    x = x_ref[...]
    nc, nb, _ = x.shape
    rows = nc * nb
    gain = g_ref[...] * (1.0 + scale_ref[...])
    h = jnp.concatenate([_mod_norm(x[0:ncl], gain[0:nb][None], shift_ref[0:nb][None]),
                         _mod_norm(x[ncl:nc], gain[nb:nb + 1][None], shift_ref[nb:nb + 1][None])], axis=0)
    hb = h.reshape(rows, D).astype(BF16)
    pt = lax.dot_general(wat_ref[...], hb, (((1,), (1,)), ((), ())), preferred_element_type=F32)
    ut_ref[...] = pt.astype(BF16).reshape(S5_G, S5_H, rows)
    p = jnp.dot(hb, wbz_ref[...], preferred_element_type=F32)
    z_ref[...] = _silu(p[:, FN_W:]).astype(BF16)
    xw = jnp.dot(p[:, :FN_W].astype(BF16), wcs_ref[...], preferred_element_type=F32)
    for q in range(4):
        scr[q][...] = xw[:, q * LANE:(q + 1) * LANE]
    for q in range(4):
        part, half = divmod(q, 2)
        for bi in range(nb):
            piece = scr[q][pl.ds(bi, nc, stride=nb), :]
            lo = bi * FN_W + half * LANE
            xw_ref[part, :, lo:lo + LANE] = piece.astype(BF16)


def _even_in(xs, ncl, ada, l, g, w_in, layer, wcs):
    nc, bsz, _ = xs.shape
    rows = nc * bsz
    return pl.pallas_call(
        functools.partial(_even_in_kernel, ncl=ncl),
        grid=(CH,),
        in_specs=[pl.BlockSpec((nc, bsz, D), lambda i: (0, 0, i)),
                  pl.BlockSpec((None, None) + ada.shape[2:], lambda i: (l, 0, 0, 0)),
                  pl.BlockSpec((None, None) + ada.shape[2:], lambda i: (l, 1, 0, 0)),
                  pl.BlockSpec((None, 1, D), lambda i: (l, 0, 0)),
                  pl.BlockSpec((None, D, w_in.shape[2]), lambda i: (layer, 0, 0),
                               pipeline_mode=pl.Buffered(1)),
                  pl.BlockSpec((FN_W, 2 * FN_W), lambda i: (0, 0))],
        out_specs=[pl.BlockSpec((S5_G, S5_H, rows), lambda i: (0, i, 0)),
                   pl.BlockSpec((2, None, nc, bsz * FN_W), lambda i: (0, i, 0, 0)),
                   pl.BlockSpec((None, rows, MIX), lambda i: (i, 0, 0))],
        out_shape=[jax.ShapeDtypeStruct((S5_G, CH * S5_H, rows), BF16),
                   jax.ShapeDtypeStruct((2, CH, nc, bsz * FN_W), BF16),
                   jax.ShapeDtypeStruct((CH, rows, MIX), BF16)],
        scratch_shapes=[pltpu.VMEM((S5_W, D), BF16), pltpu.VMEM((D, FN_W + MIX), BF16)]
        + [pltpu.VMEM((rows, LANE), F32)] * 4,
        compiler_params=_cparams(("arbitrary",)),
    )(xs, ada, ada, g, w_in, wcs)


def _s5_kernel(ut_ref, mt_ref, be_ref, cp_ref, l16_ref, yt_ref,
               sre_ref, sim_ref, are_ref, aim_ref, bre_ref, bim_ref, *, bsz, ncl, ncc, gs):
    nl = bsz * ncl
    nst = 2 * S5_P
    for g in range(gs):
        st = lax.dot_general(ut_ref[g], be_ref[g], (((0,), (0,)), ((), ())), preferred_element_type=F32)
        sre_ref[g] = st[:, :nst]
        sim_ref[g] = st[:, nst:]

    lam = [l16_ref[g] for g in range(gs)]
    is_fwd = lax.broadcasted_iota(jnp.int32, (bsz, nst), 1) < S5_P

    def make_step(base, nchunk):
        def step(c, carry):
            rf = pl.ds(pl.multiple_of(base + c * bsz, bsz), bsz)
            rb = pl.ds(pl.multiple_of(base + (nchunk - 1 - c) * bsz, bsz), bsz)
            out = []
            for g in range(gs):
                sre, sim = carry[2 * g], carry[2 * g + 1]
                lre = lam[g][0:1, :]
                lim = lam[g][1:2, :]
                are_ref[g, rf, :] = sre
                aim_ref[g, rf, :] = sim
                bre_ref[g, rb, :] = sre
                bim_ref[g, rb, :] = sim
                in_re = jnp.where(is_fwd, sre_ref[g, rf, :], sre_ref[g, rb, :])
                in_im = jnp.where(is_fwd, sim_ref[g, rf, :], sim_ref[g, rb, :])
                out.append(lre * sre - lim * sim + in_re)
                out.append(lre * sim + lim * sre + in_im)
            return tuple(out)
        return step

    zero = jnp.zeros((bsz, nst), F32)
    carry = lax.fori_loop(0, ncc, make_step(nl, ncc), (zero,) * (2 * gs))
    lax.fori_loop(0, ncl, make_step(0, ncl), carry)

    nt = (((1,), (1,)), ((), ()))
    fwd_rows = lax.broadcasted_iota(jnp.int32, (bsz * (ncl + ncc), nst), 1) < S5_P
    for g in range(gs):
        s0 = jnp.concatenate([jnp.where(fwd_rows, are_ref[g], bre_ref[g]),
                              jnp.where(fwd_rows, aim_ref[g], bim_ref[g])], axis=1).astype(BF16)
        yt_ref[g] = (jnp.dot(mt_ref[g], ut_ref[g], preferred_element_type=F32)
                     + lax.dot_general(cp_ref[g], s0, nt, preferred_element_type=F32)).astype(BF16)


def _s5_mix(ut, ncl, mt, bend, cp, lam16, bsz, layer):
    rows = ut.shape[2]
    kk = CH * S5_H
    nst = 2 * S5_P
    gs = 6
    off = layer * (S5_G // gs)

    def gspec(r, c):
        return pl.BlockSpec((gs, r, c), lambda g: (g, 0, 0))

    def tspec(r, c):
        return pl.BlockSpec((gs, r, c), lambda g: (g + off, 0, 0))

    return pl.pallas_call(
        functools.partial(_s5_kernel, bsz=bsz, ncl=ncl, ncc=rows // bsz - ncl, gs=gs),
        grid=(S5_G // gs,),
        in_specs=[gspec(kk, rows), tspec(kk, kk), tspec(kk, 2 * nst), tspec(kk, 2 * nst), tspec(2, nst)],
        out_specs=gspec(kk, rows),
        out_shape=jax.ShapeDtypeStruct((S5_G, kk, rows), BF16),
        scratch_shapes=[pltpu.VMEM((gs, rows, nst), F32)] * 6,
        compiler_params=_cparams(("arbitrary",)),
    )(ut, mt, bend, cp, lam16)


def _fnet_weights_kernel(ccs_ref, fw_ref, o_ref):
    fw = fw_ref[...]
    ccs = ccs_ref[...]
    wc = jnp.dot(ccs[:, :FN_W], fw, precision=HI, preferred_element_type=F32)
    ws = jnp.dot(ccs[:, FN_W:], fw, precision=HI, preferred_element_type=F32)
    o_ref[...] = jnp.concatenate([wc, ws], axis=1).astype(BF16)


def _fnet_weights(fw_bd):
    return pl.pallas_call(
        _fnet_weights_kernel,
        out_shape=jax.ShapeDtypeStruct((FN_W, 2 * FN_W), BF16),
    )(_dft_channel_matrix(), fw_bd)


def _fnet_kernel(tab_ref, v_ref, jr_ref, o_ref, *, ipt):
    r = pl.program_id(1)
    tab = tab_ref[...].astype(BF16)
    v = v_ref[...]
    nc, n = v.shape[2], v.shape[3]
    t = CH * nc
    a = jnp.dot(tab[:, :t], v[0].reshape(t, n), preferred_element_type=F32)
    b = jnp.dot(tab[:, t:], v[1].reshape(t, n), preferred_element_type=F32)
    o_ref[pl.ds(r * ipt, ipt)] = (a - b).reshape(ipt, nc, n).astype(BF16)
    m = (a + b).astype(BF16)
    for k in range(ipt):
        i = r * ipt + k

        @pl.when(jnp.logical_and(i >= 1, i <= CH // 2 - 1))
        def _():
            o_ref[CH - i] = jnp.dot(jr_ref[...], m[k * nc:(k + 1) * nc],
                                    preferred_element_type=F32).astype(BF16)


def _dft_half_table(nc):
    t = nc * CH
    cols = (np.arange(nc)[None, :] * CH + np.arange(CH)[:, None]).reshape(-1)
    rows = (np.arange(nc)[None, :] * CH + np.arange(CH // 2 + 1)[:, None]).reshape(-1)
    prod = (rows[:, None].astype(np.int64) * cols[None, :].astype(np.int64)) % t
    ang = prod.astype(np.float64) * (2.0 * np.pi / t)
    scale = 1.0 / math.sqrt(t * FN_GW)
    return jnp.asarray(np.concatenate([np.cos(ang), np.sin(ang)], axis=1) * scale, dtype=F32)


def _dft_channel_matrix():
    c = np.arange(FN_GW)
    ang = (c[:, None] * c[None, :] % FN_GW).astype(np.float64) * (2.0 * np.pi / FN_GW)
    eye = np.eye(FN_G)
    return jnp.asarray(np.concatenate([np.kron(eye, np.cos(ang)), np.kron(eye, np.sin(ang))], axis=1),
                       dtype=F32)


def _fnet(xw, c0, nc, bsz):
    t = nc * CH
    nhb = CH // 2 + 1
    ipt = 3
    bpb = 4
    jr = jnp.asarray(np.eye(nc)[::-1], dtype=BF16)
    return pl.pallas_call(
        functools.partial(_fnet_kernel, ipt=ipt),
        grid=(bsz // bpb, nhb // ipt),
        in_specs=[pl.BlockSpec((ipt * nc, 2 * t), lambda b, r: (r, 0)),
                  pl.BlockSpec((2, CH, nc, bpb * FN_W), lambda b, r: (0, 0, c0 // nc, b)),
                  pl.BlockSpec((nc, nc), lambda b, r: (0, 0))],
        out_specs=pl.BlockSpec((CH, nc, bpb * FN_W), lambda b, r: (0, 0, b)),
        out_shape=jax.ShapeDtypeStruct((CH, nc, bsz * FN_W), BF16),
        compiler_params=_cparams(("arbitrary", "arbitrary")),
    )(_dft_half_table(nc), xw, jr)


def _even_out_kernel(*refs, ncl, with_ctx):
    it = iter(refs)
    yt_ref, ybl_ref = next(it), next(it)
    ybc_ref = next(it) if with_ctx else None
    z_ref, x_ref, gate_ref, gwf_ref, gb_ref, wof_ref, o_ref = (next(it), next(it), next(it), next(it),
                                                               next(it), next(it), next(it))
    scr = [next(it), next(it)]
    gw_ref, wo_ref = next(it), next(it)

    @pl.when(pl.program_id(0) == 0)
    def _():
        gw_ref[...] = (0.5 * gwf_ref[...]).astype(BF16)
        wo_ref[...] = wof_ref[...].astype(BF16)

    rows = yt_ref.shape[2]
    nc, nb, _ = x_ref.shape
    for half in range(2):
        for bi in range(nb):
            lo = bi * FN_W + half * LANE
            scr[half][pl.ds(bi, ncl, stride=nb), :] = ybl_ref[:, lo:lo + LANE].astype(F32)
            if with_ctx:
                scr[half][pl.ds(ncl * nb + bi, nc - ncl, stride=nb), :] = (
                    ybc_ref[:, lo:lo + LANE].astype(F32))

    for r0 in range(0, rows, ROW_BLOCK):
        rb = min(ROW_BLOCK, rows - r0)
        yt = yt_ref[:, :, r0:r0 + rb]
        ya = _gelu_tanh(yt.astype(F32).reshape(S5_W, rb).T)
        half_g = jnp.dot(ya.astype(BF16), gw_ref[...], preferred_element_type=F32) + 0.5 * gb_ref[...]
        hy = 0.5 * ya
        ya = hy + hy * jnp.tanh(half_g)
        yb = jnp.concatenate([scr[0][r0:r0 + rb, :], scr[1][r0:r0 + rb, :]], axis=1)
        sz = z_ref[r0:r0 + rb, :]
        ma = (ya * sz[:, :S5_W]).astype(BF16)
        mb = (yb * sz[:, S5_W:]).astype(BF16)
        out = (jnp.dot(ma, wo_ref[0:S5_W, :], preferred_element_type=F32)
               + jnp.dot(mb, wo_ref[S5_W:MIX, :], preferred_element_type=F32))
        ca, cb = r0 // nb, (r0 + rb) // nb
        out = out.reshape(cb - ca, nb, D)
        cm = min(cb, ncl)
        if ca < cm:
            o_ref[ca:cm] = x_ref[ca:cm] + gate_ref[0:nb][None] * out[0:cm - ca]
        if cm < cb:
            lo = max(ca, ncl)
            o_ref[lo:cb] = x_ref[lo:cb] + gate_ref[nb:nb + 1][None] * out[lo - ca:cb - ca]


def _even_out(yt, ybl, ybc, z, xs, ncl, ada, l, glu_w, glu_b, w_out, layer):
    bsz = xs.shape[1]
    with_ctx = ybc is not None
    nc = xs.shape[0] if with_ctx else ncl
    rows = nc * bsz
    args = [yt, ybl]
    specs = [pl.BlockSpec((S5_G, S5_H, rows), lambda j: (0, j, 0)),
             pl.BlockSpec((None, ncl, bsz * FN_W), lambda j: (j, 0, 0))]
    if with_ctx:
        args.append(ybc)
        specs.append(pl.BlockSpec((None, nc - ncl, bsz * FN_W), lambda j: (j, 0, 0)))
    args += [z, xs, ada, glu_w, glu_b, w_out]
    specs += [pl.BlockSpec((None, rows, MIX), lambda j: (j, 0, 0)),
              pl.BlockSpec((nc, bsz, D), lambda j: (0, 0, j)),
              pl.BlockSpec((None, None) + ada.shape[2:], lambda j: (l, 2, 0, 0)),
              pl.BlockSpec((None, S5_W, S5_W), lambda j: (layer, 0, 0), pipeline_mode=pl.Buffered(1)),
              pl.BlockSpec((None, 1, S5_W), lambda j: (layer, 0, 0)),
              pl.BlockSpec((None, MIX, D), lambda j: (layer, 0, 0), pipeline_mode=pl.Buffered(1))]
    return pl.pallas_call(
        functools.partial(_even_out_kernel, ncl=ncl, with_ctx=with_ctx),
        grid=(CH,),
        in_specs=specs,
        out_specs=pl.BlockSpec((nc, bsz, D), lambda j: (0, 0, j)),
        out_shape=jax.ShapeDtypeStruct((nc, bsz, CH * D), F32),
        scratch_shapes=[pltpu.VMEM((rows, LANE), F32)] * 2 + [pltpu.VMEM((S5_W, S5_W), BF16),
                                                             pltpu.VMEM((MIX, D), BF16)],
        compiler_params=_cparams(("arbitrary",)),
    )(*args)


def _odd_kernel(*refs, ct, ntl, t_lat, t_ctx, final):
    it = iter(refs)
    xm_ref, xn_ref = next(it), next(it)
    shift_ref, scale_ref, gate_ref, g_ref = next(it), next(it), next(it), next(it)
    wif_ref, wof_ref, pwf_ref, ps_ref, cw_ref = next(it), next(it), next(it), next(it), next(it)
    fg_ref = next(it) if final else None
    o_ref = next(it)
    h_ref = next(it)
    pe_ref = next(it)
    tail_ref = next(it)
    wi_ref, wo_ref, pw_ref = next(it), next(it), next(it)
    fin_ref = next(it) if final else None

    ti = pl.program_id(0)
    is_ctx = ti >= ntl
    t0 = jnp.where(is_ctx, ti - ntl, ti) * (ct * CH)
    t_total = jnp.where(is_ctx, t_ctx, t_lat)

    @pl.when(ti == 0)
    def _():
        wi_ref[...] = wif_ref[...].astype(BF16)
        wo_ref[...] = wof_ref[...].astype(BF16)
        pw_ref[...] = pwf_ref[...].astype(BF16)
        tail_ref[...] = jnp.zeros(tail_ref.shape, F32)

    n1 = POOL_W + 2 * CONV_W
    nb = xm_ref.shape[1]
    tm = ct * CH
    ne = tm + 2 * HALO

    def rows_of(ref):
        return jnp.where(is_ctx, ref[nb:nb + 1], ref[0:nb])

    gain = g_ref[...] * (1.0 + rows_of(scale_ref))
    shift = rows_of(shift_ref)

    def hn(xv):
        return _mod_norm(xv, gain, shift)

    for i in range(CH):
        h_ref[0:ct, i] = hn(xm_ref[:, :, i * D:(i + 1) * D])
    for i in range(HALO):
        h_ref[ct, i] = hn(xn_ref[:, :, i * D:(i + 1) * D])[0]

    m0 = HALO * nb
    mr = tm * nb
    he = h_ref[...].reshape((ct + 1) * CH * nb, D)[0:mr + m0].astype(BF16)
    pe = jnp.dot(he, wi_ref[:, 0:n1], preferred_element_type=F32)
    uc = pe[:, :POOL_W]
    v = pe[:, POOL_W:POOL_W + CONV_W] * pe[:, POOL_W + CONV_W:]
    pe_ref[m0:m0 + mr, 0:POOL_W] = uc[0:mr]
    pe_ref[m0:m0 + mr, POOL_W:POOL_W + CONV_W] = v[0:mr]
    valid = (t0 + tm) < t_total
    pe_ref[m0 + mr:, 0:POOL_W] = jnp.where(valid, uc[mr:], 0.0)
    pe_ref[m0 + mr:, POOL_W:POOL_W + CONV_W] = jnp.where(valid, v[mr:], 0.0)
    pe_ref[0:m0, :] = jnp.where(t0 > 0, tail_ref[...], 0.0)

    cwt = cw_ref[...]
    gate = rows_of(gate_ref)
    blk = ROW_BLOCK if final else mr
    tb = blk // nb
    for r0 in range(0, mr, blk):
        p2 = jnp.dot(he[r0:r0 + blk], wi_ref[:, n1:], preferred_element_type=F32)
        b_gate = p2[:, :CONV_W]
        sz = _silu(p2[:, CONV_W:])

        tpos = t0 + r0 // nb + lax.broadcasted_iota(jnp.int32, (blk, 1), 0) // nb
        pooled = []
        for gi, w in enumerate(POOL_WINDOWS):
            c0 = gi * POOL_GW
            s = pe_ref[r0:r0 + blk + 2 * m0, c0:c0 + POOL_GW]
            n = tb + 2 * HALO
            width = 1
            while width < w:
                s = s[0:(n - width) * nb] + s[width * nb:n * nb]
                n -= width
                width *= 2
            start = (HALO - w // 2) * nb
            total = s[start:start + blk]
            hi = jnp.minimum(tpos + w // 2, t_total)
            lo = jnp.maximum(tpos - w // 2, 0)
            inv_cnt = 1.0 / (hi - lo).astype(F32)
            centre = pe_ref[m0 + r0:m0 + r0 + blk, c0:c0 + POOL_GW]
            pg = total * inv_cnt - centre
            pooled.append(jnp.dot(pg.astype(BF16), pw_ref[gi], preferred_element_type=F32))
        y_c = jnp.concatenate(pooled, axis=1) * ps_ref[...]

        q0 = m0 + r0
        vm = pe_ref[q0 - nb:q0 - nb + blk, POOL_W:POOL_W + CONV_W]
        v0 = pe_ref[q0:q0 + blk, POOL_W:POOL_W + CONV_W]
        vp = pe_ref[q0 + nb:q0 + nb + blk, POOL_W:POOL_W + CONV_W]
        y_d = b_gate * (vm * cwt[0:1, :] + v0 * cwt[1:2, :] + vp * cwt[2:3, :])

        y = (jnp.concatenate([y_c, y_d], axis=1) * sz).astype(BF16)
        out = jnp.dot(y, wo_ref[...], preferred_element_type=F32)
        ca, cb = r0 // (CH * nb), (r0 + blk) // (CH * nb)
        go = (gate * out.reshape(tb, nb, D)).reshape(cb - ca, CH, nb, D)
        for i in range(CH):
            xo = xm_ref[ca:cb, :, i * D:(i + 1) * D] + go[:, i]
            if not final:
                o_ref[ca:cb, :, i * D:(i + 1) * D] = xo
                continue
            ms = jnp.mean(xo * xo, axis=-1, keepdims=True)
            fin_ref[ca:cb, :, i * D:(i + 1) * D] = xo * lax.rsqrt(ms + EPS) * fg_ref[...]
        if final:
            o_ref[:, ca * CH:cb * CH, :] = pltpu.einshape("cb(id)->b(ci)d", fin_ref[ca:cb], i=CH)
    tail_ref[...] = pe_ref[mr:mr + m0, :]


def _odd_layer(xs, ncl, with_ctx, ada, l, g, w_in, w_out, pool_w, layer, pool_scale, conv_w, final_g):
    nc_all, bsz, _ = xs.shape
    ct = 8
    ntl = ncl // ct
    nc = nc_all if with_ctx else ncl
    final = final_g is not None
    args = [xs, xs, ada, ada, ada, g, w_in, w_out, pool_w, pool_scale, conv_w]

    def vec(j):
        return pl.BlockSpec((None, None) + ada.shape[2:], lambda i: (l, j, 0, 0))

    def const(*shape):
        return pl.BlockSpec(shape, lambda i: (0,) * len(shape), pipeline_mode=pl.Buffered(1))

    def stacked(*shape):
        return pl.BlockSpec((None,) + shape, lambda i: (layer,) + (0,) * len(shape),
                            pipeline_mode=pl.Buffered(1))

    specs = [pl.BlockSpec((ct, bsz, CH * D), lambda i: (i, 0, 0)),
             pl.BlockSpec((1, bsz, CH * D), lambda i: (jnp.minimum((i + 1) * ct, nc_all - 1), 0, 0)),
             vec(0), vec(1), vec(2),
             pl.BlockSpec((None, 1, D), lambda i: (l, 0, 0)), stacked(*w_in.shape[1:]), stacked(MIX, D),
             stacked(len(POOL_WINDOWS), POOL_GW, POOL_GW), stacked(1, POOL_W), stacked(3, CONV_W)]
    scratch = [pltpu.VMEM((ct + 1, CH, bsz, D), F32),
               pltpu.VMEM(((ct + 1) * CH * bsz, POOL_W + CONV_W), F32),
               pltpu.VMEM((HALO * bsz, POOL_W + CONV_W), F32),
               pltpu.VMEM(w_in.shape[1:], BF16), pltpu.VMEM((MIX, D), BF16),
               pltpu.VMEM((len(POOL_WINDOWS), POOL_GW, POOL_GW), BF16)]
    if final:
        args.append(final_g.reshape(1, D))
        specs.append(const(1, D))
        scratch.append(pltpu.VMEM((ct, bsz, CH * D), F32))
        out_spec = pl.BlockSpec((bsz, ct * CH, D), lambda i: (0, i, 0))
        out_shape = jax.ShapeDtypeStruct((bsz, nc * CH, D), F32)
    else:
        out_spec = pl.BlockSpec((ct, bsz, CH * D), lambda i: (i, 0, 0))
        out_shape = jax.ShapeDtypeStruct((nc, bsz, CH * D), F32)
    return pl.pallas_call(
        functools.partial(_odd_kernel, ct=ct, ntl=ntl, t_lat=ncl * CH, t_ctx=(nc_all - ncl) * CH,
                          final=final),
        grid=(nc // ct,),
        in_specs=specs,
        out_specs=out_spec,
        out_shape=out_shape,
        scratch_shapes=scratch,
        compiler_params=_cparams(("arbitrary",)),
    )(*args)


def _sincos_table(n_tok, dim):
    rows = n_tok // GRID_W
    rr, cc = np.meshgrid(np.arange(rows, dtype=np.float64), np.arange(GRID_W, dtype=np.float64),
                         indexing='ij')
    rr = rr.reshape(-1, 1)
    cc = cc.reshape(-1, 1)
    quarter = dim // 4
    omega = POS_BASE ** (-np.arange(quarter, dtype=np.float64) / quarter)
    tab = np.concatenate([np.sin(rr * omega), np.cos(rr * omega), np.sin(cc * omega), np.cos(cc * omega)],
                         axis=-1)
    return jnp.asarray(tab, dtype=F32)


def _block_diag(w):
    g, c, _ = w.shape
    eye = jnp.eye(g, dtype=w.dtype)
    return (eye[:, None, :, None] * w[:, :, None, :]).reshape(g * c, g * c)


def kernel(x, c, ctx, c_ctx, norm_g, ada_w, ada_b, even_w_in, even_w_out, s5_lam_re, s5_lam_im, s5_log_step, s5_b_re, s5_b_im, s5_c_re, s5_c_im, s5_d, s5_glu_w, s5_glu_b, fnet_w, odd_w_in, odd_w_out, pool_w, pool_scale, conv_w, final_g):
    bsz, n_tok, _ = x.shape
    depth = norm_g.shape[0]
    ncl = n_tok // CH
    ncc = ctx.shape[1] // CH

    cond = jnp.concatenate([c, jnp.broadcast_to(c_ctx[None], (16 - bsz, D))], axis=0)
    ada, mt, bend, cp, lam16, xs = _prologue(cond, ada_w, ada_b, s5_lam_re, s5_lam_im, s5_log_step, s5_b_re,
                                             s5_b_im, s5_c_re, s5_c_im, s5_d, x, _sincos_table(n_tok, D), ctx)

    need_ctx = [any(j % 2 == 0 for j in range(l + 1, depth)) for l in range(depth)]
    norm_g3 = norm_g.reshape(depth, 1, D)
    glu_b3 = s5_glu_b.reshape(-1, 1, S5_W)
    pool_scale3 = pool_scale.reshape(-1, 1, POOL_W)

    for l in range(depth):
        i = l // 2
        last = l == depth - 1
        if l % 2 == 0:
            wcs = _fnet_weights(_block_diag(fnet_w[i]))
            ut, xw, z = _even_in(xs, ncl, ada, l, norm_g3, even_w_in, i, wcs)
            yt = _s5_mix(ut, ncl, mt, bend, cp, lam16, bsz, i)
            ybl = _fnet(xw, 0, ncl, bsz)
            ybc = _fnet(xw, ncl, ncc, bsz) if need_ctx[l] else None
            xs = _even_out(yt, ybl, ybc, z, xs, ncl, ada, l, s5_glu_w, glu_b3, even_w_out, i)
        else:
            xs = _odd_layer(xs, ncl, need_ctx[l], ada, l, norm_g3, odd_w_in, odd_w_out, pool_w, i,
                            pool_scale3, conv_w, final_g if last else None)
    if depth % 2 == 1:
        raise NotImplementedError("final norm and (B, T, D) order are produced by the last (odd) layer")
    return xs
```

```python
import functools
import math

import numpy as np
import jax
import jax.numpy as jnp
from jax import lax
from jax.experimental import pallas as pl
from jax.experimental.pallas import tpu as pltpu

D = 1024
MIX = 1024
S5_W = 768
FN_W = 256
S5_H = 16
S5_G = 48
S5_P = 64
FN_G = 4
FN_GW = 64
POOL_W = 512
CONV_W = 512
POOL_WINDOWS = (2, 4, 8, 16)
POOL_GW = 128
GRID_W = 64
EPS = 1e-6
POS_BASE = 10000.0
CH = 16
HALO = 8
LANE = 128
ROW_BLOCK = 256
VMEM_LIMIT = 56 * 1024 * 1024

F32 = jnp.float32
BF16 = jnp.bfloat16
HI = lax.Precision.HIGHEST


def _cparams(sem):
    return pltpu.CompilerParams(dimension_semantics=sem, vmem_limit_bytes=VMEM_LIMIT)


def _silu(v):
    h = 0.5 * v
    return h + h * jnp.tanh(h)


def _gelu_tanh(v):
    c = math.sqrt(2.0 / math.pi)
    h = 0.5 * v
    return h + h * jnp.tanh(v * (c + (c * 0.044715) * (v * v)))


def _mod_norm(x, gain, shift):
    ms = jnp.mean(x * x, axis=-1, keepdims=True)
    return x * lax.rsqrt(ms + EPS) * gain + shift


def _split3(v):
    hi = v.astype(BF16)
    lo = (v - hi.astype(F32)).astype(BF16)
    return hi, lo


def _ada_item(c_ref, w_ref, b_ref, ada_ref):
    s_hi, s_lo = _split3(_silu(c_ref[...]))
    w_hi, w_lo = _split3(w_ref[...])
    ada_ref[...] = (jnp.dot(s_hi, w_hi, preferred_element_type=F32)
                    + jnp.dot(s_hi, w_lo, preferred_element_type=F32)
                    + jnp.dot(s_lo, w_hi, preferred_element_type=F32) + b_ref[...])


def _stream_item(x_ref, pos_ref, c_ref, o_ref, step, ntl):
    @pl.when(step < ntl)
    def _():
        o_ref[...] = pltpu.einshape("b(ci)d->cb(id)", x_ref[...] + pos_ref[...][None], i=CH)

    @pl.when(step >= ntl)
    def _():
        o_ref[...] = pltpu.einshape("b(ci)d->cb(id)", c_ref[...], i=CH)


def _prologue_kernel(c_ref, w_ref, b_ref, lr_ref, li_ref, ls_ref, btr_ref, bti_ref, cr_ref, ci_ref, d_ref,
                     x_ref, pos_ref, ctx_ref, ada_ref, mt_ref, be_ref, cp_ref, l16_ref, xs_ref,
                     ere_ref, eim_ref, *, gb, n_ada, n_tab, ntl):
    step = pl.program_id(0)

    @pl.when(step < n_ada)
    def _():
        _ada_item(c_ref, w_ref, b_ref, ada_ref)

    @pl.when(step < n_tab)
    def _():
        _tables_item(lr_ref, li_ref, ls_ref, btr_ref, bti_ref, cr_ref, ci_ref, d_ref,
                     mt_ref, be_ref, cp_ref, l16_ref, ere_ref, eim_ref, gb)

    _stream_item(x_ref, pos_ref, ctx_ref, xs_ref, step, ntl)


def _tables_item(lr_ref, li_ref, ls_ref, btr_ref, bti_ref, cr_ref, ci_ref, d_ref,
                 mt_ref, be_ref, cp_ref, l16_ref, ere_ref, eim_ref, gb):
    nst = 2 * S5_P
    kk = CH * S5_H
    step = jnp.exp(ls_ref[...])
    lr = lr_ref[...]
    li = li_ref[...]
    a = lr * step
    b = li * step

    mag = jnp.exp(a)
    l1re = mag * jnp.cos(b)
    l1im = mag * jnp.sin(b)
    squares = [(l1re, l1im)]
    for _ in range(4):
        sr, si = squares[-1]
        squares.append((sr * sr - si * si, 2.0 * (sr * si)))

    def powers(expo):
        pr = jnp.ones((gb,) + expo.shape[1:], F32)
        pi = jnp.zeros((gb,) + expo.shape[1:], F32)
        for bit, (sr, si) in enumerate(squares):
            on = ((expo >> bit) & 1) == 1
            pr, pi = jnp.where(on, pr * sr - pi * si, pr), jnp.where(on, pr * si + pi * sr, pi)
        return pr, pi

    row = lax.broadcasted_iota(jnp.int32, (1, CH, nst), 1)
    fwd = lax.broadcasted_iota(jnp.int32, (1, CH, nst), 2) < S5_P

    n_re = l1re - 1.0
    den = lr * lr + li * li
    co_re = (n_re * lr + l1im * li) / den
    co_im = (l1im * lr - n_re * li) / den
    btr = btr_ref[...]
    bti = bti_ref[...]
    bb_re = co_re * btr - co_im * bti
    bb_im = co_re * bti + co_im * btr

    pe_re, pe_im = powers(jnp.where(fwd, (CH - 1) - row, row))
    for l in range(CH):
        pr = pe_re[:, l:l + 1, :]
        pi = pe_im[:, l:l + 1, :]
        ere_ref[:, l * S5_H:(l + 1) * S5_H, :] = pr * bb_re - pi * bb_im
        eim_ref[:, l * S5_H:(l + 1) * S5_H, :] = pr * bb_im + pi * bb_re

    cr = cr_ref[...]
    ci = ci_ref[...]
    pc_re, pc_im = powers(jnp.where(fwd, row + 1, CH - row))
    for j in range(CH):
        pr = pc_re[:, j:j + 1, :]
        pi = pc_im[:, j:j + 1, :]
        w_re = cr * pr - ci * pi
        w_im = cr * pi + ci * pr
        cp_ref[:, j * S5_H:(j + 1) * S5_H, :] = jnp.concatenate([w_re, -w_im], axis=2).astype(BF16)

    l16_ref[...] = jnp.concatenate(squares[4], axis=1)

    fwd2 = lax.broadcasted_iota(jnp.int32, (S5_H, nst), 1) < S5_P
    lane = lax.broadcasted_iota(jnp.int32, (S5_H, kk), 1)
    iblk = lane // S5_H
    hrow = lax.broadcasted_iota(jnp.int32, (S5_H, kk), 0)
    nt = (((1,), (1,)), ((), ()))
    for g in range(gb):
        ere = ere_ref[g]
        eim = eim_ref[g]
        e2 = jnp.concatenate([ere, eim], axis=1)
        be_ref[g] = e2.astype(BF16)
        e_hi, e_lo = _split3(e2)
        rhs = jnp.concatenate([e_hi, e_hi, e_lo], axis=1)
        crg = cr[g]
        cig = ci[g]

        c2 = jnp.concatenate([
            jnp.concatenate([jnp.where(fwd2, crg, 0.0), jnp.where(fwd2, -cig, 0.0)], axis=1),
            jnp.concatenate([jnp.where(fwd2, 0.0, crg), jnp.where(fwd2, 0.0, -cig)], axis=1)], axis=0)
        c_hi, c_lo = _split3(c2)
        lhs = jnp.concatenate([c_hi, c_lo, c_hi], axis=1)
        kfb = lax.dot_general(lhs, rhs, nt, preferred_element_type=F32)
        kf = kfb[0:S5_H]
        kb = kfb[S5_H:2 * S5_H]
        dg = d_ref[g]
        for j in range(CH):
            sf = (kk - (CH - 1 - j) * S5_H) % kk
            rf = pltpu.roll(kf, sf, 1) if sf else kf
            rb = pltpu.roll(kb, j * S5_H, 1) if j else kb
            blk = (jnp.where(iblk <= j, rf, 0.0) + jnp.where(iblk >= j, rb, 0.0)
                   + jnp.where(lane == j * S5_H + hrow, dg, 0.0))
            mt_ref[g, j * S5_H:(j + 1) * S5_H, :] = blk.astype(BF16)


def _prologue(cond, ada_w, ada_b, lam_re, lam_im, log_step, b_re, b_im, c_re, c_im, d_skip, x, pos, ctx):
    n = lam_re.shape[0] * S5_G
    gb = 8
    nst = 2 * S5_P
    kk = CH * S5_H

    def fb(v):
        return jnp.concatenate([v[:, 0], v[:, 1]], axis=-1).reshape(n, v.shape[3], nst)

    lr = fb(lam_re[:, :, :, None, :])
    li = fb(lam_im[:, :, :, None, :])
    ls = fb(jnp.broadcast_to(log_step[:, :, :, None, None], log_step.shape + (1, S5_P)))
    btr = fb(jnp.swapaxes(b_re, -1, -2))
    bti = fb(jnp.swapaxes(b_im, -1, -2))
    cr = fb(c_re)
    ci = fb(c_im)
    d = d_skip.reshape(n, S5_H, 1)

    depth = ada_w.shape[0]
    n_ada = depth * 3
    n_tab = n // gb
    bsz, t, _ = x.shape
    tc = ctx.shape[1]
    ct = 8
    tm = ct * CH
    ntl = t // tm
    n_str = (t + tc) // tm

    def spec(r, c):
        return pl.BlockSpec((gb, r, c), lambda s: (jnp.minimum(s, n_tab - 1), 0, 0))

    def ada_idx(s):
        s = jnp.minimum(s, n_ada - 1)
        return s // 3, s % 3

    def lat_tile(s):
        return jnp.minimum(s, ntl - 1)

    return pl.pallas_call(
        functools.partial(_prologue_kernel, gb=gb, n_ada=n_ada, n_tab=n_tab, ntl=ntl),
        grid=(max(n_ada, n_tab, n_str),),
        in_specs=[pl.BlockSpec((16, D), lambda s: (0, 0)),
                  pl.BlockSpec((None, D, D), lambda s: (ada_idx(s)[0], 0, ada_idx(s)[1])),
                  pl.BlockSpec((None, None, 1, D), lambda s: ada_idx(s) + (0, 0)),
                  spec(1, nst), spec(1, nst), spec(1, nst), spec(S5_H, nst), spec(S5_H, nst),
                  spec(S5_H, nst), spec(S5_H, nst), spec(S5_H, 1),
                  pl.BlockSpec((bsz, tm, D), lambda s: (0, lat_tile(s), 0)),
                  pl.BlockSpec((tm, D), lambda s: (lat_tile(s), 0)),
                  pl.BlockSpec((bsz, tm, D), lambda s: (0, jnp.clip(s - ntl, 0, tc // tm - 1), 0),
                               pipeline_mode=pl.Buffered(1))],
        out_specs=[pl.BlockSpec((None, None, 16, D), lambda s: ada_idx(s) + (0, 0)),
                   spec(kk, kk), spec(kk, 2 * nst), spec(kk, 2 * nst), spec(2, nst),
                   pl.BlockSpec((ct, bsz, CH * D), lambda s: (jnp.minimum(s, n_str - 1), 0, 0))],
        out_shape=[jax.ShapeDtypeStruct((depth, 3, 16, D), F32),
                   jax.ShapeDtypeStruct((n, kk, kk), BF16),
                   jax.ShapeDtypeStruct((n, kk, 2 * nst), BF16),
                   jax.ShapeDtypeStruct((n, kk, 2 * nst), BF16),
                   jax.ShapeDtypeStruct((n, 2, nst), F32),
                   jax.ShapeDtypeStruct(((t + tc) // CH, bsz, CH * D), F32)],
        scratch_shapes=[pltpu.VMEM((gb, kk, nst), F32)] * 2,
        compiler_params=_cparams(("arbitrary",)),
    )(cond, ada_w, ada_b.reshape(depth, 3, 1, D), lr, li, ls, btr, bti, cr, ci, d, x, pos, ctx)


def _even_in_kernel(x_ref, shift_ref, scale_ref, g_ref, w_ref, wcs_ref, ut_ref, xw_ref, z_ref,
                    wat_ref, wbz_ref, *scr, ncl):
    @pl.when(pl.program_id(0) == 0)
    def _():
        wat_ref[...] = w_ref[:, :S5_W].T.astype(BF16)
        wbz_ref[...] = w_ref[:, S5_W:].astype(BF16)

    x = x_ref[...]
    nc, nb, _ = x.shape
    rows = nc * nb
    gain = g_ref[...] * (1.0 + scale_ref[...])
    h = jnp.concatenate([_mod_norm(x[0:ncl], gain[0:nb][None], shift_ref[0:nb][None]),
                         _mod_norm(x[ncl:nc], gain[nb:nb + 1][None], shift_ref[nb:nb + 1][None])], axis=0)
    hb = h.reshape(rows, D).astype(BF16)
    pt = lax.dot_general(wat_ref[...], hb, (((1,), (1,)), ((), ())), preferred_element_type=F32)
    ut_ref[...] = pt.astype(BF16).reshape(S5_G, S5_H, rows)
    p = jnp.dot(hb, wbz_ref[...], preferred_element_type=F32)
    z_ref[...] = _silu(p[:, FN_W:]).astype(BF16)
    xw = jnp.dot(p[:, :FN_W].astype(BF16), wcs_ref[...], preferred_element_type=F32)
    for q in range(4):
        scr[q][...] = xw[:, q * LANE:(q + 1) * LANE]
    for q in range(4):
        part, half = divmod(q, 2)
        for bi in range(nb):
            piece = scr[q][pl.ds(bi, nc, stride=nb), :]
            lo = bi * FN_W + half * LANE
            xw_ref[part, :, lo:lo + LANE] = piece.astype(BF16)


def _even_in(xs, ncl, ada, l, g, w_in, layer, wcs):
    nc, bsz, _ = xs.shape
    rows = nc * bsz
    return pl.pallas_call(
        functools.partial(_even_in_kernel, ncl=ncl),
        grid=(CH,),
        in_specs=[pl.BlockSpec((nc, bsz, D), lambda i: (0, 0, i)),
                  pl.BlockSpec((None, None) + ada.shape[2:], lambda i: (l, 0, 0, 0)),
                  pl.BlockSpec((None, None) + ada.shape[2:], lambda i: (l, 1, 0, 0)),
                  pl.BlockSpec((None, 1, D), lambda i: (l, 0, 0)),
                  pl.BlockSpec((None, D, w_in.shape[2]), lambda i: (layer, 0, 0),
                               pipeline_mode=pl.Buffered(1)),
                  pl.BlockSpec((FN_W, 2 * FN_W), lambda i: (0, 0))],
        out_specs=[pl.BlockSpec((S5_G, S5_H, rows), lambda i: (0, i, 0)),
                   pl.BlockSpec((2, None, nc, bsz * FN_W), lambda i: (0, i, 0, 0)),
                   pl.BlockSpec((None, rows, MIX), lambda i: (i, 0, 0))],
        out_shape=[jax.ShapeDtypeStruct((S5_G, CH * S5_H, rows), BF16),
                   jax.ShapeDtypeStruct((2, CH, nc, bsz * FN_W), BF16),
                   jax.ShapeDtypeStruct((CH, rows, MIX), BF16)],
        scratch_shapes=[pltpu.VMEM((S5_W, D), BF16), pltpu.VMEM((D, FN_W + MIX), BF16)]
        + [pltpu.VMEM((rows, LANE), F32)] * 4,
        compiler_params=_cparams(("arbitrary",)),
    )(xs, ada, ada, g, w_in, wcs)


def _s5_kernel(ut_ref, mt_ref, be_ref, cp_ref, l16_ref, yt_ref,
               sre_ref, sim_ref, are_ref, aim_ref, bre_ref, bim_ref, *, bsz, ncl, ncc, gs):
    nl = bsz * ncl
    nst = 2 * S5_P
    for g in range(gs):
        st = lax.dot_general(ut_ref[g], be_ref[g], (((0,), (0,)), ((), ())), preferred_element_type=F32)
        sre_ref[g] = st[:, :nst]
        sim_ref[g] = st[:, nst:]

    lam = [l16_ref[g] for g in range(gs)]
    is_fwd = lax.broadcasted_iota(jnp.int32, (bsz, nst), 1) < S5_P

    def make_step(base, nchunk):
        def step(c, carry):
            rf = pl.ds(pl.multiple_of(base + c * bsz, bsz), bsz)
            rb = pl.ds(pl.multiple_of(base + (nchunk - 1 - c) * bsz, bsz), bsz)
            out = []
            for g in range(gs):
                sre, sim = carry[2 * g], carry[2 * g + 1]
                lre = lam[g][0:1, :]
                lim = lam[g][1:2, :]
                are_ref[g, rf, :] = sre
                aim_ref[g, rf, :] = sim
                bre_ref[g, rb, :] = sre
                bim_ref[g, rb, :] = sim
                in_re = jnp.where(is_fwd, sre_ref[g, rf, :], sre_ref[g, rb, :])
                in_im = jnp.where(is_fwd, sim_ref[g, rf, :], sim_ref[g, rb, :])
                out.append(lre * sre - lim * sim + in_re)
                out.append(lre * sim + lim * sre + in_im)
            return tuple(out)
        return step

    zero = jnp.zeros((bsz, nst), F32)
    carry = lax.fori_loop(0, ncc, make_step(nl, ncc), (zero,) * (2 * gs))
    lax.fori_loop(0, ncl, make_step(0, ncl), carry)

    nt = (((1,), (1,)), ((), ()))
    fwd_rows = lax.broadcasted_iota(jnp.int32, (bsz * (ncl + ncc), nst), 1) < S5_P
    for g in range(gs):
        s0 = jnp.concatenate([jnp.where(fwd_rows, are_ref[g], bre_ref[g]),
                              jnp.where(fwd_rows, aim_ref[g], bim_ref[g])], axis=1).astype(BF16)
        yt_ref[g] = (jnp.dot(mt_ref[g], ut_ref[g], preferred_element_type=F32)
                     + lax.dot_general(cp_ref[g], s0, nt, preferred_element_type=F32)).astype(BF16)


def _s5_mix(ut, ncl, mt, bend, cp, lam16, bsz, layer):
    rows = ut.shape[2]
    kk = CH * S5_H
    nst = 2 * S5_P
    gs = 8
    off = layer * (S5_G // gs)

    def gspec(r, c):
        return pl.BlockSpec((gs, r, c), lambda g: (g, 0, 0))

    def tspec(r, c):
        return pl.BlockSpec((gs, r, c), lambda g: (g + off, 0, 0), pipeline_mode=pl.Buffered(1))

    return pl.pallas_call(
        functools.partial(_s5_kernel, bsz=bsz, ncl=ncl, ncc=rows // bsz - ncl, gs=gs),
        grid=(S5_G // gs,),
        in_specs=[gspec(kk, rows), tspec(kk, kk), tspec(kk, 2 * nst), tspec(kk, 2 * nst), tspec(2, nst)],
        out_specs=gspec(kk, rows),
        out_shape=jax.ShapeDtypeStruct((S5_G, kk, rows), BF16),
        scratch_shapes=[pltpu.VMEM((gs, rows, nst), F32)] * 6,
        compiler_params=_cparams(("arbitrary",)),
    )(ut, mt, bend, cp, lam16)


def _fnet_weights_kernel(ccs_ref, fw_ref, o_ref):
    fw = fw_ref[...]
    ccs = ccs_ref[...]
    wc = jnp.dot(ccs[:, :FN_W], fw, precision=HI, preferred_element_type=F32)
    ws = jnp.dot(ccs[:, FN_W:], fw, precision=HI, preferred_element_type=F32)
    o_ref[...] = jnp.concatenate([wc, ws], axis=1).astype(BF16)


def _fnet_weights(fw_bd):
    return pl.pallas_call(
        _fnet_weights_kernel,
        out_shape=jax.ShapeDtypeStruct((FN_W, 2 * FN_W), BF16),
    )(_dft_channel_matrix(), fw_bd)


def _fnet_kernel(tab_ref, v_ref, jr_ref, o_ref, *, ipt):
    r = pl.program_id(1)
    tab = tab_ref[...].astype(BF16)
    v = v_ref[...]
    nc, n = v.shape[2], v.shape[3]
    t = CH * nc
    a = jnp.dot(tab[:, :t], v[0].reshape(t, n), preferred_element_type=F32)
    b = jnp.dot(tab[:, t:], v[1].reshape(t, n), preferred_element_type=F32)
    o_ref[pl.ds(r * ipt, ipt)] = (a - b).reshape(ipt, nc, n).astype(BF16)
    m = (a + b).astype(BF16)
    for k in range(ipt):
        i = r * ipt + k

        @pl.when(jnp.logical_and(i >= 1, i <= CH // 2 - 1))
        def _():
            o_ref[CH - i] = jnp.dot(jr_ref[...], m[k * nc:(k + 1) * nc],
                                    preferred_element_type=F32).astype(BF16)


def _dft_half_table(nc):
    t = nc * CH
    cols = (np.arange(nc)[None, :] * CH + np.arange(CH)[:, None]).reshape(-1)
    rows = (np.arange(nc)[None, :] * CH + np.arange(CH // 2 + 1)[:, None]).reshape(-1)
    prod = (rows[:, None].astype(np.int64) * cols[None, :].astype(np.int64)) % t
    ang = prod.astype(np.float64) * (2.0 * np.pi / t)
    scale = 1.0 / math.sqrt(t * FN_GW)
    return jnp.asarray(np.concatenate([np.cos(ang), np.sin(ang)], axis=1) * scale, dtype=F32)


def _dft_channel_matrix():
    c = np.arange(FN_GW)
    ang = (c[:, None] * c[None, :] % FN_GW).astype(np.float64) * (2.0 * np.pi / FN_GW)
    eye = np.eye(FN_G)
    return jnp.asarray(np.concatenate([np.kron(eye, np.cos(ang)), np.kron(eye, np.sin(ang))], axis=1),
                       dtype=F32)


def _fnet(xw, c0, nc, bsz):
    t = nc * CH
    nhb = CH // 2 + 1
    ipt = 3
    bpb = 4
    jr = jnp.asarray(np.eye(nc)[::-1], dtype=BF16)
    return pl.pallas_call(
        functools.partial(_fnet_kernel, ipt=ipt),
        grid=(bsz // bpb, nhb // ipt),
        in_specs=[pl.BlockSpec((ipt * nc, 2 * t), lambda b, r: (r, 0)),
                  pl.BlockSpec((2, CH, nc, bpb * FN_W), lambda b, r: (0, 0, c0 // nc, b)),
                  pl.BlockSpec((nc, nc), lambda b, r: (0, 0))],
        out_specs=pl.BlockSpec((CH, nc, bpb * FN_W), lambda b, r: (0, 0, b)),
        out_shape=jax.ShapeDtypeStruct((CH, nc, bsz * FN_W), BF16),
        compiler_params=_cparams(("arbitrary", "arbitrary")),
    )(_dft_half_table(nc), xw, jr)


def _even_out_kernel(*refs, ncl, with_ctx):
    it = iter(refs)
    yt_ref, ybl_ref = next(it), next(it)
    ybc_ref = next(it) if with_ctx else None
    z_ref, x_ref, gate_ref, gwf_ref, gb_ref, wof_ref, o_ref = (next(it), next(it), next(it), next(it),
                                                               next(it), next(it), next(it))
    scr = [next(it), next(it)]
    gw_ref, wo_ref = next(it), next(it)

    @pl.when(pl.program_id(0) == 0)
    def _():
        gw_ref[...] = (0.5 * gwf_ref[...]).astype(BF16)
        wo_ref[...] = wof_ref[...].astype(BF16)

    rows = yt_ref.shape[2]
    nc, nb, _ = x_ref.shape
    for half in range(2):
        for bi in range(nb):
            lo = bi * FN_W + half * LANE
            scr[half][pl.ds(bi, ncl, stride=nb), :] = ybl_ref[:, lo:lo + LANE].astype(F32)
            if with_ctx:
                scr[half][pl.ds(ncl * nb + bi, nc - ncl, stride=nb), :] = (
                    ybc_ref[:, lo:lo + LANE].astype(F32))

    for r0 in range(0, rows, ROW_BLOCK):
        rb = min(ROW_BLOCK, rows - r0)
        yt = yt_ref[:, :, r0:r0 + rb]
        ya = _gelu_tanh(yt.astype(F32).reshape(S5_W, rb).T)
        half_g = jnp.dot(ya.astype(BF16), gw_ref[...], preferred_element_type=F32) + 0.5 * gb_ref[...]
        hy = 0.5 * ya
        ya = hy + hy * jnp.tanh(half_g)
        yb = jnp.concatenate([scr[0][r0:r0 + rb, :], scr[1][r0:r0 + rb, :]], axis=1)
        sz = z_ref[r0:r0 + rb, :]
        ma = (ya * sz[:, :S5_W]).astype(BF16)
        mb = (yb * sz[:, S5_W:]).astype(BF16)
        out = (jnp.dot(ma, wo_ref[0:S5_W, :], preferred_element_type=F32)
               + jnp.dot(mb, wo_ref[S5_W:MIX, :], preferred_element_type=F32))
        ca, cb = r0 // nb, (r0 + rb) // nb
        out = out.reshape(cb - ca, nb, D)
        cm = min(cb, ncl)
        if ca < cm:
            o_ref[ca:cm] = x_ref[ca:cm] + gate_ref[0:nb][None] * out[0:cm - ca]
        if cm < cb:
            lo = max(ca, ncl)
            o_ref[lo:cb] = x_ref[lo:cb] + gate_ref[nb:nb + 1][None] * out[lo - ca:cb - ca]


def _even_out(yt, ybl, ybc, z, xs, ncl, ada, l, glu_w, glu_b, w_out, layer):
    bsz = xs.shape[1]
    with_ctx = ybc is not None
    nc = xs.shape[0] if with_ctx else ncl
    rows = nc * bsz
    args = [yt, ybl]
    specs = [pl.BlockSpec((S5_G, S5_H, rows), lambda j: (0, j, 0)),
             pl.BlockSpec((None, ncl, bsz * FN_W), lambda j: (j, 0, 0))]
    if with_ctx:
        args.append(ybc)
        specs.append(pl.BlockSpec((None, nc - ncl, bsz * FN_W), lambda j: (j, 0, 0)))
    args += [z, xs, ada, glu_w, glu_b, w_out]
    specs += [pl.BlockSpec((None, rows, MIX), lambda j: (j, 0, 0)),
              pl.BlockSpec((nc, bsz, D), lambda j: (0, 0, j)),
              pl.BlockSpec((None, None) + ada.shape[2:], lambda j: (l, 2, 0, 0)),
              pl.BlockSpec((None, S5_W, S5_W), lambda j: (layer, 0, 0), pipeline_mode=pl.Buffered(1)),
              pl.BlockSpec((None, 1, S5_W), lambda j: (layer, 0, 0)),
              pl.BlockSpec((None, MIX, D), lambda j: (layer, 0, 0), pipeline_mode=pl.Buffered(1))]
    return pl.pallas_call(
        functools.partial(_even_out_kernel, ncl=ncl, with_ctx=with_ctx),
        grid=(CH,),
        in_specs=specs,
        out_specs=pl.BlockSpec((nc, bsz, D), lambda j: (0, 0, j)),
        out_shape=jax.ShapeDtypeStruct((nc, bsz, CH * D), F32),
        scratch_shapes=[pltpu.VMEM((rows, LANE), F32)] * 2 + [pltpu.VMEM((S5_W, S5_W), BF16),
                                                             pltpu.VMEM((MIX, D), BF16)],
        compiler_params=_cparams(("arbitrary",)),
    )(*args)


def _odd_kernel(*refs, ct, ntl, t_lat, t_ctx, final):
    it = iter(refs)
    xm_ref, xn_ref = next(it), next(it)
    shift_ref, scale_ref, gate_ref, g_ref = next(it), next(it), next(it), next(it)
    wif_ref, wof_ref, pwf_ref, ps_ref, cw_ref = next(it), next(it), next(it), next(it), next(it)
    fg_ref = next(it) if final else None
    o_ref = next(it)
    h_ref = next(it)
    pe_ref = next(it)
    tail_ref = next(it)
    wi_ref, wo_ref, pw_ref = next(it), next(it), next(it)
    fin_ref = next(it) if final else None

    ti = pl.program_id(0)
    is_ctx = ti >= ntl
    t0 = jnp.where(is_ctx, ti - ntl, ti) * (ct * CH)
    t_total = jnp.where(is_ctx, t_ctx, t_lat)

    @pl.when(ti == 0)
    def _():
        wi_ref[...] = wif_ref[...].astype(BF16)
        wo_ref[...] = wof_ref[...].astype(BF16)
        pw_ref[...] = pwf_ref[...].astype(BF16)
        tail_ref[...] = jnp.zeros(tail_ref.shape, F32)

    n1 = POOL_W + 2 * CONV_W
    nb = xm_ref.shape[1]
    tm = ct * CH
    ne = tm + 2 * HALO

    def rows_of(ref):
        return jnp.where(is_ctx, ref[nb:nb + 1], ref[0:nb])

    gain = g_ref[...] * (1.0 + rows_of(scale_ref))
    shift = rows_of(shift_ref)

    def hn(xv):
        return _mod_norm(xv, gain, shift)

    for i in range(CH):
        h_ref[0:ct, i] = hn(xm_ref[:, :, i * D:(i + 1) * D])
    for i in range(HALO):
        h_ref[ct, i] = hn(xn_ref[:, :, i * D:(i + 1) * D])[0]

    m0 = HALO * nb
    mr = tm * nb
    he = h_ref[...].reshape((ct + 1) * CH * nb, D)[0:mr + m0].astype(BF16)
    pe = jnp.dot(he, wi_ref[:, 0:n1], preferred_element_type=F32)
    uc = pe[:, :POOL_W]
    v = pe[:, POOL_W:POOL_W + CONV_W] * pe[:, POOL_W + CONV_W:]
    pe_ref[m0:m0 + mr, 0:POOL_W] = uc[0:mr]
    pe_ref[m0:m0 + mr, POOL_W:POOL_W + CONV_W] = v[0:mr]
    valid = (t0 + tm) < t_total
    pe_ref[m0 + mr:, 0:POOL_W] = jnp.where(valid, uc[mr:], 0.0)
    pe_ref[m0 + mr:, POOL_W:POOL_W + CONV_W] = jnp.where(valid, v[mr:], 0.0)
    pe_ref[0:m0, :] = jnp.where(t0 > 0, tail_ref[...], 0.0)

    p2 = jnp.dot(he[0:mr], wi_ref[:, n1:], preferred_element_type=F32)
    b_gate = p2[:, :CONV_W]
    sz = _silu(p2[:, CONV_W:])

    tpos = t0 + lax.broadcasted_iota(jnp.int32, (mr, 1), 0) // nb
    pooled = []
    for gi, w in enumerate(POOL_WINDOWS):
        c0 = gi * POOL_GW
        s = pe_ref[:, c0:c0 + POOL_GW]
        n = ne
        width = 1
        while width < w:
            s = s[0:(n - width) * nb] + s[width * nb:n * nb]
            n -= width
            width *= 2
        start = (HALO - w // 2) * nb
        total = s[start:start + mr]
        hi = jnp.minimum(tpos + w // 2, t_total)
        lo = jnp.maximum(tpos - w // 2, 0)
        inv_cnt = 1.0 / (hi - lo).astype(F32)
        centre = pe_ref[m0:m0 + mr, c0:c0 + POOL_GW]
        pg = total * inv_cnt - centre
        pooled.append(jnp.dot(pg.astype(BF16), pw_ref[gi], preferred_element_type=F32))
    y_c = jnp.concatenate(pooled, axis=1) * ps_ref[...]

    cwt = cw_ref[...]
    vm = pe_ref[m0 - nb:m0 - nb + mr, POOL_W:POOL_W + CONV_W]
    v0 = pe_ref[m0:m0 + mr, POOL_W:POOL_W + CONV_W]
    vp = pe_ref[m0 + nb:m0 + nb + mr, POOL_W:POOL_W + CONV_W]
    y_d = b_gate * (vm * cwt[0:1, :] + v0 * cwt[1:2, :] + vp * cwt[2:3, :])

    y = (jnp.concatenate([y_c, y_d], axis=1) * sz).astype(BF16)
    out = jnp.dot(y, wo_ref[...], preferred_element_type=F32)
    go = (rows_of(gate_ref) * out.reshape(tm, nb, D)).reshape(ct, CH, nb, D)
    for i in range(CH):
        xo = xm_ref[:, :, i * D:(i + 1) * D] + go[:, i]
        if not final:
            o_ref[:, :, i * D:(i + 1) * D] = xo
            continue
        ms = jnp.mean(xo * xo, axis=-1, keepdims=True)
        fin_ref[:, :, i * D:(i + 1) * D] = xo * lax.rsqrt(ms + EPS) * fg_ref[...]
    tail_ref[...] = pe_ref[mr:mr + m0, :]
    if final:
        o_ref[...] = pltpu.einshape("cb(id)->b(ci)d", fin_ref[...], i=CH)


def _odd_layer(xs, ncl, with_ctx, ada, l, g, w_in, w_out, pool_w, layer, pool_scale, conv_w, final_g):
    nc_all, bsz, _ = xs.shape
    ct = 8
    ntl = ncl // ct
    nc = nc_all if with_ctx else ncl
    final = final_g is not None
    args = [xs, xs, ada, ada, ada, g, w_in, w_out, pool_w, pool_scale, conv_w]

    def vec(j):
        return pl.BlockSpec((None, None) + ada.shape[2:], lambda i: (l, j, 0, 0))

    def const(*shape):
        return pl.BlockSpec(shape, lambda i: (0,) * len(shape), pipeline_mode=pl.Buffered(1))

    def stacked(*shape):
        return pl.BlockSpec((None,) + shape, lambda i: (layer,) + (0,) * len(shape),
                            pipeline_mode=pl.Buffered(1))

    specs = [pl.BlockSpec((ct, bsz, CH * D), lambda i: (i, 0, 0)),
             pl.BlockSpec((1, bsz, CH * D), lambda i: (jnp.minimum((i + 1) * ct, nc_all - 1), 0, 0)),
             vec(0), vec(1), vec(2),
             pl.BlockSpec((None, 1, D), lambda i: (l, 0, 0)), stacked(*w_in.shape[1:]), stacked(MIX, D),
             stacked(len(POOL_WINDOWS), POOL_GW, POOL_GW), stacked(1, POOL_W), stacked(3, CONV_W)]
    scratch = [pltpu.VMEM((ct + 1, CH, bsz, D), F32),
               pltpu.VMEM(((ct + 1) * CH * bsz, POOL_W + CONV_W), F32),
               pltpu.VMEM((HALO * bsz, POOL_W + CONV_W), F32),
               pltpu.VMEM(w_in.shape[1:], BF16), pltpu.VMEM((MIX, D), BF16),
               pltpu.VMEM((len(POOL_WINDOWS), POOL_GW, POOL_GW), BF16)]
    if final:
        args.append(final_g.reshape(1, D))
        specs.append(const(1, D))
        scratch.append(pltpu.VMEM((ct, bsz, CH * D), F32))
        out_spec = pl.BlockSpec((bsz, ct * CH, D), lambda i: (0, i, 0))
        out_shape = jax.ShapeDtypeStruct((bsz, nc * CH, D), F32)
    else:
        out_spec = pl.BlockSpec((ct, bsz, CH * D), lambda i: (i, 0, 0))
        out_shape = jax.ShapeDtypeStruct((nc, bsz, CH * D), F32)
    return pl.pallas_call(
        functools.partial(_odd_kernel, ct=ct, ntl=ntl, t_lat=ncl * CH, t_ctx=(nc_all - ncl) * CH,
                          final=final),
        grid=(nc // ct,),
        in_specs=specs,
        out_specs=out_spec,
        out_shape=out_shape,
        scratch_shapes=scratch,
        compiler_params=_cparams(("arbitrary",)),
    )(*args)


def _sincos_table(n_tok, dim):
    rows = n_tok // GRID_W
    rr, cc = np.meshgrid(np.arange(rows, dtype=np.float64), np.arange(GRID_W, dtype=np.float64),
                         indexing='ij')
    rr = rr.reshape(-1, 1)
    cc = cc.reshape(-1, 1)
    quarter = dim // 4
    omega = POS_BASE ** (-np.arange(quarter, dtype=np.float64) / quarter)
    tab = np.concatenate([np.sin(rr * omega), np.cos(rr * omega), np.sin(cc * omega), np.cos(cc * omega)],
                         axis=-1)
    return jnp.asarray(tab, dtype=F32)


def _block_diag(w):
    g, c, _ = w.shape
    eye = jnp.eye(g, dtype=w.dtype)
    return (eye[:, None, :, None] * w[:, :, None, :]).reshape(g * c, g * c)


def kernel(x, c, ctx, c_ctx, norm_g, ada_w, ada_b, even_w_in, even_w_out, s5_lam_re, s5_lam_im, s5_log_step, s5_b_re, s5_b_im, s5_c_re, s5_c_im, s5_d, s5_glu_w, s5_glu_b, fnet_w, odd_w_in, odd_w_out, pool_w, pool_scale, conv_w, final_g):
    bsz, n_tok, _ = x.shape
    depth = norm_g.shape[0]
    ncl = n_tok // CH
    ncc = ctx.shape[1] // CH

    cond = jnp.concatenate([c, jnp.broadcast_to(c_ctx[None], (16 - bsz, D))], axis=0)
    ada, mt, bend, cp, lam16, xs = _prologue(cond, ada_w, ada_b, s5_lam_re, s5_lam_im, s5_log_step, s5_b_re,
                                             s5_b_im, s5_c_re, s5_c_im, s5_d, x, _sincos_table(n_tok, D), ctx)

    need_ctx = [any(j % 2 == 0 for j in range(l + 1, depth)) for l in range(depth)]
    norm_g3 = norm_g.reshape(depth, 1, D)
    glu_b3 = s5_glu_b.reshape(-1, 1, S5_W)
    pool_scale3 = pool_scale.reshape(-1, 1, POOL_W)

    for l in range(depth):
        i = l // 2
        last = l == depth - 1
        if l % 2 == 0:
            wcs = _fnet_weights(_block_diag(fnet_w[i]))
            ut, xw, z = _even_in(xs, ncl, ada, l, norm_g3, even_w_in, i, wcs)
            yt = _s5_mix(ut, ncl, mt, bend, cp, lam16, bsz, i)
            ybl = _fnet(xw, 0, ncl, bsz)
            ybc = _fnet(xw, ncl, ncc, bsz) if need_ctx[l] else None
            xs = _even_out(yt, ybl, ybc, z, xs, ncl, ada, l, s5_glu_w, glu_b3, even_w_out, i)
        else:
            xs = _odd_layer(xs, ncl, need_ctx[l], ada, l, norm_g3, odd_w_in, odd_w_out, pool_w, i,
                            pool_scale3, conv_w, final_g if last else None)
    if depth % 2 == 1:
        raise NotImplementedError("final norm and (B, T, D) order are produced by the last (odd) layer")
    return xs
```

```python
import functools
import math

import numpy as np
import jax
import jax.numpy as jnp
from jax import lax
from jax.experimental import pallas as pl
from jax.experimental.pallas import tpu as pltpu

D = 1024
MIX = 1024
S5_W = 768
FN_W = 256
S5_H = 16
S5_G = 48
S5_P = 64
FN_G = 4
FN_GW = 64
POOL_W = 512
CONV_W = 512
POOL_WINDOWS = (2, 4, 8, 16)
POOL_GW = 128
GRID_W = 64
EPS = 1e-6
POS_BASE = 10000.0
CH = 16
HALO = 8
LANE = 128
ROW_BLOCK = 256
VMEM_LIMIT = 56 * 1024 * 1024

F32 = jnp.float32
BF16 = jnp.bfloat16
HI = lax.Precision.HIGHEST


def _cparams(sem):
    return pltpu.CompilerParams(dimension_semantics=sem, vmem_limit_bytes=VMEM_LIMIT)


def _silu(v):
    h = 0.5 * v
    return h + h * jnp.tanh(h)


def _gelu_tanh(v):
    c = math.sqrt(2.0 / math.pi)
    h = 0.5 * v
    return h + h * jnp.tanh(v * (c + (c * 0.044715) * (v * v)))


def _mod_norm(x, gain, shift):
    ms = jnp.mean(x * x, axis=-1, keepdims=True)
    return x * lax.rsqrt(ms + EPS) * gain + shift


def _split3(v):
    hi = v.astype(BF16)
    lo = (v - hi.astype(F32)).astype(BF16)
    return hi, lo


def _ada_item(c_ref, w_ref, b_ref, ada_ref):
    s_hi, s_lo = _split3(_silu(c_ref[...]))
    w_hi, w_lo = _split3(w_ref[...])
    ada_ref[...] = (jnp.dot(s_hi, w_hi, preferred_element_type=F32)
                    + jnp.dot(s_hi, w_lo, preferred_element_type=F32)
                    + jnp.dot(s_lo, w_hi, preferred_element_type=F32) + b_ref[...])


def _stream_item(x_ref, pos_ref, c_ref, o_ref, step, ntl):
    @pl.when(step < ntl)
    def _():
        o_ref[...] = pltpu.einshape("b(ci)d->cb(id)", x_ref[...] + pos_ref[...][None], i=CH)

    @pl.when(step >= ntl)
    def _():
        o_ref[...] = pltpu.einshape("b(ci)d->cb(id)", c_ref[...], i=CH)


def _prologue_kernel(c_ref, w_ref, b_ref, lr_ref, li_ref, ls_ref, btr_ref, bti_ref, cr_ref, ci_ref, d_ref,
                     x_ref, pos_ref, ctx_ref, ada_ref, mt_ref, be_ref, cp_ref, l16_ref, xs_ref,
                     ere_ref, eim_ref, *, gb, n_ada, n_tab, ntl):
    step = pl.program_id(0)

    @pl.when(step < n_ada)
    def _():
        _ada_item(c_ref, w_ref, b_ref, ada_ref)

    @pl.when(step < n_tab)
    def _():
        _tables_item(lr_ref, li_ref, ls_ref, btr_ref, bti_ref, cr_ref, ci_ref, d_ref,
                     mt_ref, be_ref, cp_ref, l16_ref, ere_ref, eim_ref, gb)

    _stream_item(x_ref, pos_ref, ctx_ref, xs_ref, step, ntl)


def _tables_item(lr_ref, li_ref, ls_ref, btr_ref, bti_ref, cr_ref, ci_ref, d_ref,
                 mt_ref, be_ref, cp_ref, l16_ref, ere_ref, eim_ref, gb):
    nst = 2 * S5_P
    kk = CH * S5_H
    step = jnp.exp(ls_ref[...])
    lr = lr_ref[...]
    li = li_ref[...]
    a = lr * step
    b = li * step

    mag = jnp.exp(a)
    l1re = mag * jnp.cos(b)
    l1im = mag * jnp.sin(b)
    squares = [(l1re, l1im)]
    for _ in range(4):
        sr, si = squares[-1]
        squares.append((sr * sr - si * si, 2.0 * (sr * si)))

    def powers(expo):
        pr = jnp.ones((gb,) + expo.shape[1:], F32)
        pi = jnp.zeros((gb,) + expo.shape[1:], F32)
        for bit, (sr, si) in enumerate(squares):
            on = ((expo >> bit) & 1) == 1
            pr, pi = jnp.where(on, pr * sr - pi * si, pr), jnp.where(on, pr * si + pi * sr, pi)
        return pr, pi

    row = lax.broadcasted_iota(jnp.int32, (1, CH, nst), 1)
    fwd = lax.broadcasted_iota(jnp.int32, (1, CH, nst), 2) < S5_P

    n_re = l1re - 1.0
    den = lr * lr + li * li
    co_re = (n_re * lr + l1im * li) / den
    co_im = (l1im * lr - n_re * li) / den
    btr = btr_ref[...]
    bti = bti_ref[...]
    bb_re = co_re * btr - co_im * bti
    bb_im = co_re * bti + co_im * btr

    pe_re, pe_im = powers(jnp.where(fwd, (CH - 1) - row, row))
    for l in range(CH):
        pr = pe_re[:, l:l + 1, :]
        pi = pe_im[:, l:l + 1, :]
        ere_ref[:, l * S5_H:(l + 1) * S5_H, :] = pr * bb_re - pi * bb_im
        eim_ref[:, l * S5_H:(l + 1) * S5_H, :] = pr * bb_im + pi * bb_re

    cr = cr_ref[...]
    ci = ci_ref[...]
    pc_re, pc_im = powers(jnp.where(fwd, row + 1, CH - row))
    for j in range(CH):
        pr = pc_re[:, j:j + 1, :]
        pi = pc_im[:, j:j + 1, :]
        w_re = cr * pr - ci * pi
        w_im = cr * pi + ci * pr
        cp_ref[:, j * S5_H:(j + 1) * S5_H, :] = jnp.concatenate([w_re, -w_im], axis=2).astype(BF16)

    l16_ref[...] = jnp.concatenate(squares[4], axis=1)

    fwd2 = lax.broadcasted_iota(jnp.int32, (S5_H, nst), 1) < S5_P
    lane = lax.broadcasted_iota(jnp.int32, (S5_H, kk), 1)
    iblk = lane // S5_H
    hrow = lax.broadcasted_iota(jnp.int32, (S5_H, kk), 0)
    nt = (((1,), (1,)), ((), ()))
    for g in range(gb):
        ere = ere_ref[g]
        eim = eim_ref[g]
        e2 = jnp.concatenate([ere, eim], axis=1)
        be_ref[g] = e2.astype(BF16)
        e_hi, e_lo = _split3(e2)
        rhs = jnp.concatenate([e_hi, e_hi, e_lo], axis=1)
        crg = cr[g]
        cig = ci[g]

        c2 = jnp.concatenate([
            jnp.concatenate([jnp.where(fwd2, crg, 0.0), jnp.where(fwd2, -cig, 0.0)], axis=1),
            jnp.concatenate([jnp.where(fwd2, 0.0, crg), jnp.where(fwd2, 0.0, -cig)], axis=1)], axis=0)
        c_hi, c_lo = _split3(c2)
        lhs = jnp.concatenate([c_hi, c_lo, c_hi], axis=1)
        kfb = lax.dot_general(lhs, rhs, nt, preferred_element_type=F32)
        kf = kfb[0:S5_H]
        kb = kfb[S5_H:2 * S5_H]
        dg = d_ref[g]
        for j in range(CH):
            sf = (kk - (CH - 1 - j) * S5_H) % kk
            rf = pltpu.roll(kf, sf, 1) if sf else kf
            rb = pltpu.roll(kb, j * S5_H, 1) if j else kb
            blk = (jnp.where(iblk <= j, rf, 0.0) + jnp.where(iblk >= j, rb, 0.0)
                   + jnp.where(lane == j * S5_H + hrow, dg, 0.0))
            mt_ref[g, j * S5_H:(j + 1) * S5_H, :] = blk.astype(BF16)


def _prologue(cond, ada_w, ada_b, lam_re, lam_im, log_step, b_re, b_im, c_re, c_im, d_skip, x, pos, ctx):
    n = lam_re.shape[0] * S5_G
    gb = 8
    nst = 2 * S5_P
    kk = CH * S5_H

    def fb(v):
        return jnp.concatenate([v[:, 0], v[:, 1]], axis=-1).reshape(n, v.shape[3], nst)

    lr = fb(lam_re[:, :, :, None, :])
    li = fb(lam_im[:, :, :, None, :])
    ls = fb(jnp.broadcast_to(log_step[:, :, :, None, None], log_step.shape + (1, S5_P)))
    btr = fb(jnp.swapaxes(b_re, -1, -2))
    bti = fb(jnp.swapaxes(b_im, -1, -2))
    cr = fb(c_re)
    ci = fb(c_im)
    d = d_skip.reshape(n, S5_H, 1)

    depth = ada_w.shape[0]
    n_ada = depth * 3
    n_tab = n // gb
    bsz, t, _ = x.shape
    tc = ctx.shape[1]
    ct = 8
    tm = ct * CH
    ntl = t // tm
    n_str = (t + tc) // tm

    def spec(r, c):
        return pl.BlockSpec((gb, r, c), lambda s: (jnp.minimum(s, n_tab - 1), 0, 0))

    def ada_idx(s):
        s = jnp.minimum(s, n_ada - 1)
        return s // 3, s % 3

    def lat_tile(s):
        return jnp.minimum(s, ntl - 1)

    return pl.pallas_call(
        functools.partial(_prologue_kernel, gb=gb, n_ada=n_ada, n_tab=n_tab, ntl=ntl),
        grid=(max(n_ada, n_tab, n_str),),
        in_specs=[pl.BlockSpec((16, D), lambda s: (0, 0)),
                  pl.BlockSpec((None, D, D), lambda s: (ada_idx(s)[0], 0, ada_idx(s)[1])),
                  pl.BlockSpec((None, None, 1, D), lambda s: ada_idx(s) + (0, 0)),
                  spec(1, nst), spec(1, nst), spec(1, nst), spec(S5_H, nst), spec(S5_H, nst),
                  spec(S5_H, nst), spec(S5_H, nst), spec(S5_H, 1),
                  pl.BlockSpec((bsz, tm, D), lambda s: (0, lat_tile(s), 0)),
                  pl.BlockSpec((tm, D), lambda s: (lat_tile(s), 0)),
                  pl.BlockSpec((bsz, tm, D), lambda s: (0, jnp.clip(s - ntl, 0, tc // tm - 1), 0),
                               pipeline_mode=pl.Buffered(1))],
        out_specs=[pl.BlockSpec((None, None, 16, D), lambda s: ada_idx(s) + (0, 0)),
                   spec(kk, kk), spec(kk, 2 * nst), spec(kk, 2 * nst), spec(2, nst),
                   pl.BlockSpec((ct, bsz, CH * D), lambda s: (jnp.minimum(s, n_str - 1), 0, 0))],
        out_shape=[jax.ShapeDtypeStruct((depth, 3, 16, D), F32),
                   jax.ShapeDtypeStruct((n, kk, kk), BF16),
                   jax.ShapeDtypeStruct((n, kk, 2 * nst), BF16),
                   jax.ShapeDtypeStruct((n, kk, 2 * nst), BF16),
                   jax.ShapeDtypeStruct((n, 2, nst), F32),
                   jax.ShapeDtypeStruct(((t + tc) // CH, bsz, CH * D), F32)],
        scratch_shapes=[pltpu.VMEM((gb, kk, nst), F32)] * 2,
        compiler_params=_cparams(("arbitrary",)),
    )(cond, ada_w, ada_b.reshape(depth, 3, 1, D), lr, li, ls, btr, bti, cr, ci, d, x, pos, ctx)


def _even_in_kernel(x_ref, shift_ref, scale_ref, g_ref, w_ref, wcs_ref, ut_ref, xw_ref, z_ref,
                    wat_ref, wbz_ref, *scr, ncl):
    @pl.when(pl.program_id(0) == 0)
    def _():
        wat_ref[...] = w_ref[:, :S5_W].T.astype(BF16)
        wbz_ref[...] = w_ref[:, S5_W:].astype(BF16)

    x = x_ref[...]
    nc, nb, _ = x.shape
    rows = nc * nb
    gain = g_ref[...] * (1.0 + scale_ref[...])
    h = jnp.concatenate([_mod_norm(x[0:ncl], gain[0:nb][None], shift_ref[0:nb][None]),
                         _mod_norm(x[ncl:nc], gain[nb:nb + 1][None], shift_ref[nb:nb + 1][None])], axis=0)
    hb = h.reshape(rows, D).astype(BF16)
    pt = lax.dot_general(wat_ref[...], hb, (((1,), (1,)), ((), ())), preferred_element_type=F32)
    ut_ref[...] = pt.astype(BF16).reshape(S5_G, S5_H, rows)
    p = jnp.dot(hb, wbz_ref[...], preferred_element_type=F32)
    z_ref[...] = _silu(p[:, FN_W:]).astype(BF16)
    xw = jnp.dot(p[:, :FN_W].astype(BF16), wcs_ref[...], preferred_element_type=F32)
    for q in range(4):
        scr[q][...] = xw[:, q * LANE:(q + 1) * LANE]
    for q in range(4):
        part, half = divmod(q, 2)
        for bi in range(nb):
            piece = scr[q][pl.ds(bi, nc, stride=nb), :]
            lo = bi * FN_W + half * LANE
            xw_ref[part, :, lo:lo + LANE] = piece.astype(BF16)


def _even_in(xs, ncl, ada, l, g, w_in, layer, wcs):
    nc, bsz, _ = xs.shape
    rows = nc * bsz
    return pl.pallas_call(
        functools.partial(_even_in_kernel, ncl=ncl),
        grid=(CH,),
        in_specs=[pl.BlockSpec((nc, bsz, D), lambda i: (0, 0, i)),
                  pl.BlockSpec((None, None) + ada.shape[2:], lambda i: (l, 0, 0, 0)),
                  pl.BlockSpec((None, None) + ada.shape[2:], lambda i: (l, 1, 0, 0)),
                  pl.BlockSpec((None, 1, D), lambda i: (l, 0, 0)),
                  pl.BlockSpec((None, D, w_in.shape[2]), lambda i: (layer, 0, 0),
                               pipeline_mode=pl.Buffered(1)),
                  pl.BlockSpec((FN_W, 2 * FN_W), lambda i: (0, 0))],
        out_specs=[pl.BlockSpec((S5_G, S5_H, rows), lambda i: (0, i, 0)),
                   pl.BlockSpec((2, None, nc, bsz * FN_W), lambda i: (0, i, 0, 0)),
                   pl.BlockSpec((None, rows, MIX), lambda i: (i, 0, 0))],
        out_shape=[jax.ShapeDtypeStruct((S5_G, CH * S5_H, rows), BF16),
                   jax.ShapeDtypeStruct((2, CH, nc, bsz * FN_W), BF16),
                   jax.ShapeDtypeStruct((CH, rows, MIX), BF16)],
        scratch_shapes=[pltpu.VMEM((S5_W, D), BF16), pltpu.VMEM((D, FN_W + MIX), BF16)]
        + [pltpu.VMEM((rows, LANE), F32)] * 4,
        compiler_params=_cparams(("arbitrary",)),
    )(xs, ada, ada, g, w_in, wcs)


def _s5_kernel(ut_ref, mt_ref, be_ref, cp_ref, l16_ref, yt_ref,
               sre_ref, sim_ref, are_ref, aim_ref, bre_ref, bim_ref, *, bsz, ncl, ncc, gs):
    nl = bsz * ncl
    nst = 2 * S5_P
    for g in range(gs):
        st = lax.dot_general(ut_ref[g], be_ref[g], (((0,), (0,)), ((), ())), preferred_element_type=F32)
        sre_ref[g] = st[:, :nst]
        sim_ref[g] = st[:, nst:]

    lam = [l16_ref[g] for g in range(gs)]
    is_fwd = lax.broadcasted_iota(jnp.int32, (bsz, nst), 1) < S5_P

    def make_step(base, nchunk):
        def step(c, carry):
            rf = pl.ds(pl.multiple_of(base + c * bsz, bsz), bsz)
            rb = pl.ds(pl.multiple_of(base + (nchunk - 1 - c) * bsz, bsz), bsz)
            out = []
            for g in range(gs):
                sre, sim = carry[2 * g], carry[2 * g + 1]
                lre = lam[g][0:1, :]
                lim = lam[g][1:2, :]
                are_ref[g, rf, :] = sre
                aim_ref[g, rf, :] = sim
                bre_ref[g, rb, :] = sre
                bim_ref[g, rb, :] = sim
                in_re = jnp.where(is_fwd, sre_ref[g, rf, :], sre_ref[g, rb, :])
                in_im = jnp.where(is_fwd, sim_ref[g, rf, :], sim_ref[g, rb, :])
                out.append(lre * sre - lim * sim + in_re)
                out.append(lre * sim + lim * sre + in_im)
            return tuple(out)
        return step

    zero = jnp.zeros((bsz, nst), F32)
    carry = lax.fori_loop(0, ncc, make_step(nl, ncc), (zero,) * (2 * gs))
    lax.fori_loop(0, ncl, make_step(0, ncl), carry)

    nt = (((1,), (1,)), ((), ()))
    fwd_rows = lax.broadcasted_iota(jnp.int32, (bsz * (ncl + ncc), nst), 1) < S5_P
    for g in range(gs):
        s0 = jnp.concatenate([jnp.where(fwd_rows, are_ref[g], bre_ref[g]),
                              jnp.where(fwd_rows, aim_ref[g], bim_ref[g])], axis=1).astype(BF16)
        yt_ref[g] = (jnp.dot(mt_ref[g], ut_ref[g], preferred_element_type=F32)
                     + lax.dot_general(cp_ref[g], s0, nt, preferred_element_type=F32)).astype(BF16)


def _s5_mix(ut, ncl, mt, bend, cp, lam16, bsz, layer):
    rows = ut.shape[2]
    kk = CH * S5_H
    nst = 2 * S5_P
    gs = 6
    off = layer * (S5_G // gs)

    def gspec(r, c):
        return pl.BlockSpec((gs, r, c), lambda g: (g, 0, 0))

    def tspec(r, c):
        return pl.BlockSpec((gs, r, c), lambda g: (g + off, 0, 0))

    return pl.pallas_call(
        functools.partial(_s5_kernel, bsz=bsz, ncl=ncl, ncc=rows // bsz - ncl, gs=gs),
        grid=(S5_G // gs,),
        in_specs=[gspec(kk, rows), tspec(kk, kk), tspec(kk, 2 * nst), tspec(kk, 2 * nst), tspec(2, nst)],
        out_specs=gspec(kk, rows),
        out_shape=jax.ShapeDtypeStruct((S5_G, kk, rows), BF16),
        scratch_shapes=[pltpu.VMEM((gs, rows, nst), F32)] * 6,
        compiler_params=_cparams(("arbitrary",)),
    )(ut, mt, bend, cp, lam16)


def _fnet_weights_kernel(ccs_ref, fw_ref, o_ref):
    fw = fw_ref[...]
    ccs = ccs_ref[...]
    wc = jnp.dot(ccs[:, :FN_W], fw, precision=HI, preferred_element_type=F32)
    ws = jnp.dot(ccs[:, FN_W:], fw, precision=HI, preferred_element_type=F32)
    o_ref[...] = jnp.concatenate([wc, ws], axis=1).astype(BF16)


def _fnet_weights(fw_bd):
    return pl.pallas_call(
        _fnet_weights_kernel,
        out_shape=jax.ShapeDtypeStruct((FN_W, 2 * FN_W), BF16),
    )(_dft_channel_matrix(), fw_bd)


def _fnet_kernel(tab_ref, v_ref, jr_ref, o_ref, *, ipt):
    r = pl.program_id(1)
    tab = tab_ref[...].astype(BF16)
    v = v_ref[...]
    nc, n = v.shape[2], v.shape[3]
    t = CH * nc
    a = jnp.dot(tab[:, :t], v[0].reshape(t, n), preferred_element_type=F32)
    b = jnp.dot(tab[:, t:], v[1].reshape(t, n), preferred_element_type=F32)
    o_ref[pl.ds(r * ipt, ipt)] = (a - b).reshape(ipt, nc, n).astype(BF16)
    m = (a + b).astype(BF16)
    for k in range(ipt):
        i = r * ipt + k

        @pl.when(jnp.logical_and(i >= 1, i <= CH // 2 - 1))
        def _():
            o_ref[CH - i] = jnp.dot(jr_ref[...], m[k * nc:(k + 1) * nc],
                                    preferred_element_type=F32).astype(BF16)


def _dft_half_table(nc):
    t = nc * CH
    cols = (np.arange(nc)[None, :] * CH + np.arange(CH)[:, None]).reshape(-1)
    rows = (np.arange(nc)[None, :] * CH + np.arange(CH // 2 + 1)[:, None]).reshape(-1)
    prod = (rows[:, None].astype(np.int64) * cols[None, :].astype(np.int64)) % t
    ang = prod.astype(np.float64) * (2.0 * np.pi / t)
    scale = 1.0 / math.sqrt(t * FN_GW)
    return jnp.asarray(np.concatenate([np.cos(ang), np.sin(ang)], axis=1) * scale, dtype=F32)


def _dft_channel_matrix():
    c = np.arange(FN_GW)
    ang = (c[:, None] * c[None, :] % FN_GW).astype(np.float64) * (2.0 * np.pi / FN_GW)
    eye = np.eye(FN_G)
    return jnp.asarray(np.concatenate([np.kron(eye, np.cos(ang)), np.kron(eye, np.sin(ang))], axis=1),
                       dtype=F32)


def _fnet(xw, c0, nc, bsz):
    t = nc * CH
    nhb = CH // 2 + 1
    ipt = 3
    bpb = 4
    jr = jnp.asarray(np.eye(nc)[::-1], dtype=BF16)
    return pl.pallas_call(
        functools.partial(_fnet_kernel, ipt=ipt),
        grid=(bsz // bpb, nhb // ipt),
        in_specs=[pl.BlockSpec((ipt * nc, 2 * t), lambda b, r: (r, 0)),
                  pl.BlockSpec((2, CH, nc, bpb * FN_W), lambda b, r: (0, 0, c0 // nc, b)),
                  pl.BlockSpec((nc, nc), lambda b, r: (0, 0))],
        out_specs=pl.BlockSpec((CH, nc, bpb * FN_W), lambda b, r: (0, 0, b)),
        out_shape=jax.ShapeDtypeStruct((CH, nc, bsz * FN_W), BF16),
        compiler_params=_cparams(("arbitrary", "arbitrary")),
    )(_dft_half_table(nc), xw, jr)


def _even_out_kernel(*refs, ncl, with_ctx):
    it = iter(refs)
    yt_ref, ybl_ref = next(it), next(it)
    ybc_ref = next(it) if with_ctx else None
    z_ref, x_ref, gate_ref, gwf_ref, gb_ref, wof_ref, o_ref = (next(it), next(it), next(it), next(it),
                                                               next(it), next(it), next(it))
    scr = [next(it), next(it)]
    gw_ref, wo_ref = next(it), next(it)

    @pl.when(pl.program_id(0) == 0)
    def _():
        gw_ref[...] = (0.5 * gwf_ref[...]).astype(BF16)
        wo_ref[...] = wof_ref[...].astype(BF16)

    rows = yt_ref.shape[2]
    nc, nb, _ = x_ref.shape
    for half in range(2):
        for bi in range(nb):
            lo = bi * FN_W + half * LANE
            scr[half][pl.ds(bi, ncl, stride=nb), :] = ybl_ref[:, lo:lo + LANE].astype(F32)
            if with_ctx:
                scr[half][pl.ds(ncl * nb + bi, nc - ncl, stride=nb), :] = (
                    ybc_ref[:, lo:lo + LANE].astype(F32))

    blk = ROW_BLOCK if with_ctx else ROW_BLOCK // 2
    for r0 in range(0, rows, blk):
        rb = min(blk, rows - r0)
        yt = yt_ref[:, :, r0:r0 + rb]
        ya = _gelu_tanh(yt.astype(F32).reshape(S5_W, rb).T)
        half_g = jnp.dot(ya.astype(BF16), gw_ref[...], preferred_element_type=F32) + 0.5 * gb_ref[...]
        hy = 0.5 * ya
        ya = hy + hy * jnp.tanh(half_g)
        yb = jnp.concatenate([scr[0][r0:r0 + rb, :], scr[1][r0:r0 + rb, :]], axis=1)
        sz = z_ref[r0:r0 + rb, :]
        ma = (ya * sz[:, :S5_W]).astype(BF16)
        mb = (yb * sz[:, S5_W:]).astype(BF16)
        out = (jnp.dot(ma, wo_ref[0:S5_W, :], preferred_element_type=F32)
               + jnp.dot(mb, wo_ref[S5_W:MIX, :], preferred_element_type=F32))
        ca, cb = r0 // nb, (r0 + rb) // nb
        out = out.reshape(cb - ca, nb, D)
        cm = min(cb, ncl)
        if ca < cm:
            o_ref[ca:cm] = x_ref[ca:cm] + gate_ref[0:nb][None] * out[0:cm - ca]
        if cm < cb:
            lo = max(ca, ncl)
            o_ref[lo:cb] = x_ref[lo:cb] + gate_ref[nb:nb + 1][None] * out[lo - ca:cb - ca]


def _even_out(yt, ybl, ybc, z, xs, ncl, ada, l, glu_w, glu_b, w_out, layer):
    bsz = xs.shape[1]
    with_ctx = ybc is not None
    nc = xs.shape[0] if with_ctx else ncl
    rows = nc * bsz
    args = [yt, ybl]
    specs = [pl.BlockSpec((S5_G, S5_H, rows), lambda j: (0, j, 0)),
             pl.BlockSpec((None, ncl, bsz * FN_W), lambda j: (j, 0, 0))]
    if with_ctx:
        args.append(ybc)
        specs.append(pl.BlockSpec((None, nc - ncl, bsz * FN_W), lambda j: (j, 0, 0)))
    args += [z, xs, ada, glu_w, glu_b, w_out]
    specs += [pl.BlockSpec((None, rows, MIX), lambda j: (j, 0, 0)),
              pl.BlockSpec((nc, bsz, D), lambda j: (0, 0, j)),
              pl.BlockSpec((None, None) + ada.shape[2:], lambda j: (l, 2, 0, 0)),
              pl.BlockSpec((None, S5_W, S5_W), lambda j: (layer, 0, 0), pipeline_mode=pl.Buffered(1)),
              pl.BlockSpec((None, 1, S5_W), lambda j: (layer, 0, 0)),
              pl.BlockSpec((None, MIX, D), lambda j: (layer, 0, 0), pipeline_mode=pl.Buffered(1))]
    return pl.pallas_call(
        functools.partial(_even_out_kernel, ncl=ncl, with_ctx=with_ctx),
        grid=(CH,),
        in_specs=specs,
        out_specs=pl.BlockSpec((nc, bsz, D), lambda j: (0, 0, j)),
        out_shape=jax.ShapeDtypeStruct((nc, bsz, CH * D), F32),
        scratch_shapes=[pltpu.VMEM((rows, LANE), F32)] * 2 + [pltpu.VMEM((S5_W, S5_W), BF16),
                                                             pltpu.VMEM((MIX, D), BF16)],
        compiler_params=_cparams(("arbitrary",)),
    )(*args)


def _odd_kernel(*refs, ct, ntl, t_lat, t_ctx, final):
    it = iter(refs)
    xm_ref, xn_ref = next(it), next(it)
    shift_ref, scale_ref, gate_ref, g_ref = next(it), next(it), next(it), next(it)
    wif_ref, wof_ref, pwf_ref, ps_ref, cw_ref = next(it), next(it), next(it), next(it), next(it)
    fg_ref = next(it) if final else None
    o_ref = next(it)
    h_ref = next(it)
    pe_ref = next(it)
    tail_ref = next(it)
    wi_ref, wo_ref, pw_ref = next(it), next(it), next(it)
    fin_ref = next(it) if final else None

    ti = pl.program_id(0)
    is_ctx = ti >= ntl
    t0 = jnp.where(is_ctx, ti - ntl, ti) * (ct * CH)
    t_total = jnp.where(is_ctx, t_ctx, t_lat)

    @pl.when(ti == 0)
    def _():
        wi_ref[...] = wif_ref[...].astype(BF16)
        wo_ref[...] = wof_ref[...].astype(BF16)
        pw_ref[...] = pwf_ref[...].astype(BF16)
        tail_ref[...] = jnp.zeros(tail_ref.shape, F32)

    n1 = POOL_W + 2 * CONV_W
    nb = xm_ref.shape[1]
    tm = ct * CH
    ne = tm + 2 * HALO

    def rows_of(ref):
        return jnp.where(is_ctx, ref[nb:nb + 1], ref[0:nb])

    gain = g_ref[...] * (1.0 + rows_of(scale_ref))
    shift = rows_of(shift_ref)

    def hn(xv):
        return _mod_norm(xv, gain, shift)

    for i in range(CH):
        h_ref[0:ct, i] = hn(xm_ref[:, :, i * D:(i + 1) * D])
    for i in range(HALO):
        h_ref[ct, i] = hn(xn_ref[:, :, i * D:(i + 1) * D])[0]

    m0 = HALO * nb
    mr = tm * nb
    he = h_ref[...].reshape((ct + 1) * CH * nb, D)[0:mr + m0].astype(BF16)
    pe = jnp.dot(he, wi_ref[:, 0:n1], preferred_element_type=F32)
    uc = pe[:, :POOL_W]
    v = pe[:, POOL_W:POOL_W + CONV_W] * pe[:, POOL_W + CONV_W:]
    pe_ref[m0:m0 + mr, 0:POOL_W] = uc[0:mr]
    pe_ref[m0:m0 + mr, POOL_W:POOL_W + CONV_W] = v[0:mr]
    valid = (t0 + tm) < t_total
    pe_ref[m0 + mr:, 0:POOL_W] = jnp.where(valid, uc[mr:], 0.0)
    pe_ref[m0 + mr:, POOL_W:POOL_W + CONV_W] = jnp.where(valid, v[mr:], 0.0)
    pe_ref[0:m0, :] = jnp.where(t0 > 0, tail_ref[...], 0.0)

    p2 = jnp.dot(he[0:mr], wi_ref[:, n1:], preferred_element_type=F32)
    b_gate = p2[:, :CONV_W]
    sz = _silu(p2[:, CONV_W:])

    tpos = t0 + lax.broadcasted_iota(jnp.int32, (mr, 1), 0) // nb
    pooled = []
    for gi, w in enumerate(POOL_WINDOWS):
        c0 = gi * POOL_GW
        s = pe_ref[:, c0:c0 + POOL_GW]
        n = ne
        width = 1
        while width < w:
            s = s[0:(n - width) * nb] + s[width * nb:n * nb]
            n -= width
            width *= 2
        start = (HALO - w // 2) * nb
        total = s[start:start + mr]
        hi = jnp.minimum(tpos + w // 2, t_total)
        lo = jnp.maximum(tpos - w // 2, 0)
        inv_cnt = 1.0 / (hi - lo).astype(F32)
        centre = pe_ref[m0:m0 + mr, c0:c0 + POOL_GW]
        pg = total * inv_cnt - centre
        pooled.append(jnp.dot(pg.astype(BF16), pw_ref[gi], preferred_element_type=F32))
    y_c = jnp.concatenate(pooled, axis=1) * ps_ref[...]

    cwt = cw_ref[...]
    vm = pe_ref[m0 - nb:m0 - nb + mr, POOL_W:POOL_W + CONV_W]
    v0 = pe_ref[m0:m0 + mr, POOL_W:POOL_W + CONV_W]
    vp = pe_ref[m0 + nb:m0 + nb + mr, POOL_W:POOL_W + CONV_W]
    y_d = b_gate * (vm * cwt[0:1, :] + v0 * cwt[1:2, :] + vp * cwt[2:3, :])

    y = (jnp.concatenate([y_c, y_d], axis=1) * sz).astype(BF16)
    out = jnp.dot(y, wo_ref[...], preferred_element_type=F32)
    go = (rows_of(gate_ref) * out.reshape(tm, nb, D)).reshape(ct, CH, nb, D)
    for i in range(CH):
        xo = xm_ref[:, :, i * D:(i + 1) * D] + go[:, i]
        if not final:
            o_ref[:, :, i * D:(i + 1) * D] = xo
            continue
        ms = jnp.mean(xo * xo, axis=-1, keepdims=True)
        fin_ref[:, :, i * D:(i + 1) * D] = xo * lax.rsqrt(ms + EPS) * fg_ref[...]
    tail_ref[...] = pe_ref[mr:mr + m0, :]
    if final:
        o_ref[...] = pltpu.einshape("cb(id)->b(ci)d", fin_ref[...], i=CH)


def _odd_layer(xs, ncl, with_ctx, ada, l, g, w_in, w_out, pool_w, layer, pool_scale, conv_w, final_g):
    nc_all, bsz, _ = xs.shape
    ct = 8
    ntl = ncl // ct
    nc = nc_all if with_ctx else ncl
    final = final_g is not None
    args = [xs, xs, ada, ada, ada, g, w_in, w_out, pool_w, pool_scale, conv_w]

    def vec(j):
        return pl.BlockSpec((None, None) + ada.shape[2:], lambda i: (l, j, 0, 0))

    def const(*shape):
        return pl.BlockSpec(shape, lambda i: (0,) * len(shape), pipeline_mode=pl.Buffered(1))

    def stacked(*shape):
        return pl.BlockSpec((None,) + shape, lambda i: (layer,) + (0,) * len(shape),
                            pipeline_mode=pl.Buffered(1))

    specs = [pl.BlockSpec((ct, bsz, CH * D), lambda i: (i, 0, 0)),
             pl.BlockSpec((1, bsz, CH * D), lambda i: (jnp.minimum((i + 1) * ct, nc_all - 1), 0, 0)),
             vec(0), vec(1), vec(2),
             pl.BlockSpec((None, 1, D), lambda i: (l, 0, 0)), stacked(*w_in.shape[1:]), stacked(MIX, D),
             stacked(len(POOL_WINDOWS), POOL_GW, POOL_GW), stacked(1, POOL_W), stacked(3, CONV_W)]
    scratch = [pltpu.VMEM((ct + 1, CH, bsz, D), F32),
               pltpu.VMEM(((ct + 1) * CH * bsz, POOL_W + CONV_W), F32),
               pltpu.VMEM((HALO * bsz, POOL_W + CONV_W), F32),
               pltpu.VMEM(w_in.shape[1:], BF16), pltpu.VMEM((MIX, D), BF16),
               pltpu.VMEM((len(POOL_WINDOWS), POOL_GW, POOL_GW), BF16)]
    if final:
        args.append(final_g.reshape(1, D))
        specs.append(const(1, D))
        scratch.append(pltpu.VMEM((ct, bsz, CH * D), F32))
        out_spec = pl.BlockSpec((bsz, ct * CH, D), lambda i: (0, i, 0))
        out_shape = jax.ShapeDtypeStruct((bsz, nc * CH, D), F32)
    else:
        out_spec = pl.BlockSpec((ct, bsz, CH * D), lambda i: (i, 0, 0))
        out_shape = jax.ShapeDtypeStruct((nc, bsz, CH * D), F32)
    return pl.pallas_call(
        functools.partial(_odd_kernel, ct=ct, ntl=ntl, t_lat=ncl * CH, t_ctx=(nc_all - ncl) * CH,
                          final=final),
        grid=(nc // ct,),
        in_specs=specs,
        out_specs=out_spec,
        out_shape=out_shape,
        scratch_shapes=scratch,
        compiler_params=_cparams(("arbitrary",)),
    )(*args)


def _sincos_table(n_tok, dim):
    rows = n_tok // GRID_W
    rr, cc = np.meshgrid(np.arange(rows, dtype=np.float64), np.arange(GRID_W, dtype=np.float64),
                         indexing='ij')
    rr = rr.reshape(-1, 1)
    cc = cc.reshape(-1, 1)
    quarter = dim // 4
    omega = POS_BASE ** (-np.arange(quarter, dtype=np.float64) / quarter)
    tab = np.concatenate([np.sin(rr * omega), np.cos(rr * omega), np.sin(cc * omega), np.cos(cc * omega)],
                         axis=-1)
    return jnp.asarray(tab, dtype=F32)


def _block_diag(w):
    g, c, _ = w.shape
    eye = jnp.eye(g, dtype=w.dtype)
    return (eye[:, None, :, None] * w[:, :, None, :]).reshape(g * c, g * c)


def kernel(x, c, ctx, c_ctx, norm_g, ada_w, ada_b, even_w_in, even_w_out, s5_lam_re, s5_lam_im, s5_log_step, s5_b_re, s5_b_im, s5_c_re, s5_c_im, s5_d, s5_glu_w, s5_glu_b, fnet_w, odd_w_in, odd_w_out, pool_w, pool_scale, conv_w, final_g):
    bsz, n_tok, _ = x.shape
    depth = norm_g.shape[0]
    ncl = n_tok // CH
    ncc = ctx.shape[1] // CH

    cond = jnp.concatenate([c, jnp.broadcast_to(c_ctx[None], (16 - bsz, D))], axis=0)
    ada, mt, bend, cp, lam16, xs = _prologue(cond, ada_w, ada_b, s5_lam_re, s5_lam_im, s5_log_step, s5_b_re,
                                             s5_b_im, s5_c_re, s5_c_im, s5_d, x, _sincos_table(n_tok, D), ctx)

    need_ctx = [any(j % 2 == 0 for j in range(l + 1, depth)) for l in range(depth)]
    norm_g3 = norm_g.reshape(depth, 1, D)
    glu_b3 = s5_glu_b.reshape(-1, 1, S5_W)
    pool_scale3 = pool_scale.reshape(-1, 1, POOL_W)

    for l in range(depth):
        i = l // 2
        last = l == depth - 1
        if l % 2 == 0:
            wcs = _fnet_weights(_block_diag(fnet_w[i]))
            ut, xw, z = _even_in(xs, ncl, ada, l, norm_g3, even_w_in, i, wcs)
            yt = _s5_mix(ut, ncl, mt, bend, cp, lam16, bsz, i)
            ybl = _fnet(xw, 0, ncl, bsz)
            ybc = _fnet(xw, ncl, ncc, bsz) if need_ctx[l] else None
            xs = _even_out(yt, ybl, ybc, z, xs, ncl, ada, l, s5_glu_w, glu_b3, even_w_out, i)
        else:
            xs = _odd_layer(xs, ncl, need_ctx[l], ada, l, norm_g3, odd_w_in, odd_w_out, pool_w, i,
                            pool_scale3, conv_w, final_g if last else None)
    if depth % 2 == 1:
        raise NotImplementedError("final norm and (B, T, D) order are produced by the last (odd) layer")
    return xs
```

```python
import functools
import math

import numpy as np
import jax
import jax.numpy as jnp
from jax import lax
from jax.experimental import pallas as pl
from jax.experimental.pallas import tpu as pltpu

D = 1024
MIX = 1024
S5_W = 768
FN_W = 256
S5_H = 16
S5_G = 48
S5_P = 64
FN_G = 4
FN_GW = 64
POOL_W = 512
CONV_W = 512
POOL_WINDOWS = (2, 4, 8, 16)
POOL_GW = 128
GRID_W = 64
EPS = 1e-6
POS_BASE = 10000.0
CH = 16
HALO = 8
LANE = 128
ROW_BLOCK = 256
VMEM_LIMIT = 56 * 1024 * 1024

F32 = jnp.float32
BF16 = jnp.bfloat16
HI = lax.Precision.HIGHEST


def _cparams(sem):
    return pltpu.CompilerParams(dimension_semantics=sem, vmem_limit_bytes=VMEM_LIMIT)


def _silu(v):
    h = 0.5 * v
    return h + h * jnp.tanh(h)


def _gelu_tanh(v):
    c = math.sqrt(2.0 / math.pi)
    h = 0.5 * v
    return h + h * jnp.tanh(v * (c + (c * 0.044715) * (v * v)))


def _mod_norm(x, gain, shift):
    ms = jnp.mean(x * x, axis=-1, keepdims=True)
    return x * lax.rsqrt(ms + EPS) * gain + shift


def _split3(v):
    hi = v.astype(BF16)
    lo = (v - hi.astype(F32)).astype(BF16)
    return hi, lo


def _ada_item(c_ref, w_ref, b_ref, ada_ref):
    s_hi, s_lo = _split3(_silu(c_ref[...]))
    w_hi, w_lo = _split3(w_ref[...])
    ada_ref[...] = (jnp.dot(s_hi, w_hi, preferred_element_type=F32)
                    + jnp.dot(s_hi, w_lo, preferred_element_type=F32)
                    + jnp.dot(s_lo, w_hi, preferred_element_type=F32) + b_ref[...])


def _stream_item(x_ref, pos_ref, c_ref, o_ref, step, ntl):
    @pl.when(step < ntl)
    def _():
        o_ref[...] = pltpu.einshape("b(ci)d->cb(id)", x_ref[...] + pos_ref[...][None], i=CH)

    @pl.when(step >= ntl)
    def _():
        o_ref[...] = pltpu.einshape("b(ci)d->cb(id)", c_ref[...], i=CH)


def _prologue_kernel(c_ref, w_ref, b_ref, lr_ref, li_ref, ls_ref, btr_ref, bti_ref, cr_ref, ci_ref, d_ref,
                     x_ref, pos_ref, ctx_ref, ada_ref, mt_ref, be_ref, cp_ref, l16_ref, xs_ref,
                     ere_ref, eim_ref, *, gb, n_ada, n_tab, ntl):
    step = pl.program_id(0)

    @pl.when(step < n_ada)
    def _():
        _ada_item(c_ref, w_ref, b_ref, ada_ref)

    @pl.when(step < n_tab)
    def _():
        _tables_item(lr_ref, li_ref, ls_ref, btr_ref, bti_ref, cr_ref, ci_ref, d_ref,
                     mt_ref, be_ref, cp_ref, l16_ref, ere_ref, eim_ref, gb)

    _stream_item(x_ref, pos_ref, ctx_ref, xs_ref, step, ntl)


def _tables_item(lr_ref, li_ref, ls_ref, btr_ref, bti_ref, cr_ref, ci_ref, d_ref,
                 mt_ref, be_ref, cp_ref, l16_ref, ere_ref, eim_ref, gb):
    nst = 2 * S5_P
    kk = CH * S5_H
    step = jnp.exp(ls_ref[...])
    lr = lr_ref[...]
    li = li_ref[...]
    a = lr * step
    b = li * step

    mag = jnp.exp(a)
    l1re = mag * jnp.cos(b)
    l1im = mag * jnp.sin(b)
    squares = [(l1re, l1im)]
    for _ in range(4):
        sr, si = squares[-1]
        squares.append((sr * sr - si * si, 2.0 * (sr * si)))

    def powers(expo):
        pr = jnp.ones((gb,) + expo.shape[1:], F32)
        pi = jnp.zeros((gb,) + expo.shape[1:], F32)
        for bit, (sr, si) in enumerate(squares):
            on = ((expo >> bit) & 1) == 1
            pr, pi = jnp.where(on, pr * sr - pi * si, pr), jnp.where(on, pr * si + pi * sr, pi)
        return pr, pi

    row = lax.broadcasted_iota(jnp.int32, (1, CH, nst), 1)
    fwd = lax.broadcasted_iota(jnp.int32, (1, CH, nst), 2) < S5_P

    n_re = l1re - 1.0
    den = lr * lr + li * li
    co_re = (n_re * lr + l1im * li) / den
    co_im = (l1im * lr - n_re * li) / den
    btr = btr_ref[...]
    bti = bti_ref[...]
    bb_re = co_re * btr - co_im * bti
    bb_im = co_re * bti + co_im * btr

    pe_re, pe_im = powers(jnp.where(fwd, (CH - 1) - row, row))
    for l in range(CH):
        pr = pe_re[:, l:l + 1, :]
        pi = pe_im[:, l:l + 1, :]
        ere_ref[:, l * S5_H:(l + 1) * S5_H, :] = pr * bb_re - pi * bb_im
        eim_ref[:, l * S5_H:(l + 1) * S5_H, :] = pr * bb_im + pi * bb_re

    cr = cr_ref[...]
    ci = ci_ref[...]
    pc_re, pc_im = powers(jnp.where(fwd, row + 1, CH - row))
    for j in range(CH):
        pr = pc_re[:, j:j + 1, :]
        pi = pc_im[:, j:j + 1, :]
        w_re = cr * pr - ci * pi
        w_im = cr * pi + ci * pr
        cp_ref[:, j * S5_H:(j + 1) * S5_H, :] = jnp.concatenate([w_re, -w_im], axis=2).astype(BF16)

    l16_ref[...] = jnp.concatenate(squares[4], axis=1)

    fwd2 = lax.broadcasted_iota(jnp.int32, (S5_H, nst), 1) < S5_P
    lane = lax.broadcasted_iota(jnp.int32, (S5_H, kk), 1)
    iblk = lane // S5_H
    hrow = lax.broadcasted_iota(jnp.int32, (S5_H, kk), 0)
    nt = (((1,), (1,)), ((), ()))
    for g in range(gb):
        ere = ere_ref[g]
        eim = eim_ref[g]
        e2 = jnp.concatenate([ere, eim], axis=1)
        be_ref[g] = e2.astype(BF16)
        e_hi, e_lo = _split3(e2)
        rhs = jnp.concatenate([e_hi, e_hi, e_lo], axis=1)
        crg = cr[g]
        cig = ci[g]

        c2 = jnp.concatenate([
            jnp.concatenate([jnp.where(fwd2, crg, 0.0), jnp.where(fwd2, -cig, 0.0)], axis=1),
            jnp.concatenate([jnp.where(fwd2, 0.0, crg), jnp.where(fwd2, 0.0, -cig)], axis=1)], axis=0)
        c_hi, c_lo = _split3(c2)
        lhs = jnp.concatenate([c_hi, c_lo, c_hi], axis=1)
        kfb = lax.dot_general(lhs, rhs, nt, preferred_element_type=F32)
        kf = kfb[0:S5_H]
        kb = kfb[S5_H:2 * S5_H]
        dg = d_ref[g]
        for j in range(CH):
            sf = (kk - (CH - 1 - j) * S5_H) % kk
            rf = pltpu.roll(kf, sf, 1) if sf else kf
            rb = pltpu.roll(kb, j * S5_H, 1) if j else kb
            blk = (jnp.where(iblk <= j, rf, 0.0) + jnp.where(iblk >= j, rb, 0.0)
                   + jnp.where(lane == j * S5_H + hrow, dg, 0.0))
            mt_ref[g, j * S5_H:(j + 1) * S5_H, :] = blk.astype(BF16)


def _prologue(cond, ada_w, ada_b, lam_re, lam_im, log_step, b_re, b_im, c_re, c_im, d_skip, x, pos, ctx):
    n = lam_re.shape[0] * S5_G
    gb = 8
    nst = 2 * S5_P
    kk = CH * S5_H

    def fb(v):
        return jnp.concatenate([v[:, 0], v[:, 1]], axis=-1).reshape(n, v.shape[3], nst)

    lr = fb(lam_re[:, :, :, None, :])
    li = fb(lam_im[:, :, :, None, :])
    ls = fb(jnp.broadcast_to(log_step[:, :, :, None, None], log_step.shape + (1, S5_P)))
    btr = fb(jnp.swapaxes(b_re, -1, -2))
    bti = fb(jnp.swapaxes(b_im, -1, -2))
    cr = fb(c_re)
    ci = fb(c_im)
    d = d_skip.reshape(n, S5_H, 1)

    depth = ada_w.shape[0]
    n_ada = depth * 3
    n_tab = n // gb
    bsz, t, _ = x.shape
    tc = ctx.shape[1]
    ct = 8
    tm = ct * CH
    ntl = t // tm
    n_str = (t + tc) // tm

    def spec(r, c):
        return pl.BlockSpec((gb, r, c), lambda s: (jnp.minimum(s, n_tab - 1), 0, 0))

    def ada_idx(s):
        s = jnp.minimum(s, n_ada - 1)
        return s // 3, s % 3

    def lat_tile(s):
        return jnp.minimum(s, ntl - 1)

    return pl.pallas_call(
        functools.partial(_prologue_kernel, gb=gb, n_ada=n_ada, n_tab=n_tab, ntl=ntl),
        grid=(max(n_ada, n_tab, n_str),),
        in_specs=[pl.BlockSpec((16, D), lambda s: (0, 0)),
                  pl.BlockSpec((None, D, D), lambda s: (ada_idx(s)[0], 0, ada_idx(s)[1])),
                  pl.BlockSpec((None, None, 1, D), lambda s: ada_idx(s) + (0, 0)),
                  spec(1, nst), spec(1, nst), spec(1, nst), spec(S5_H, nst), spec(S5_H, nst),
                  spec(S5_H, nst), spec(S5_H, nst), spec(S5_H, 1),
                  pl.BlockSpec((bsz, tm, D), lambda s: (0, lat_tile(s), 0)),
                  pl.BlockSpec((tm, D), lambda s: (lat_tile(s), 0)),
                  pl.BlockSpec((bsz, tm, D), lambda s: (0, jnp.clip(s - ntl, 0, tc // tm - 1), 0),
                               pipeline_mode=pl.Buffered(1))],
        out_specs=[pl.BlockSpec((None, None, 16, D), lambda s: ada_idx(s) + (0, 0)),
                   spec(kk, kk), spec(kk, 2 * nst), spec(kk, 2 * nst), spec(2, nst),
                   pl.BlockSpec((ct, bsz, CH * D), lambda s: (jnp.minimum(s, n_str - 1), 0, 0))],
        out_shape=[jax.ShapeDtypeStruct((depth, 3, 16, D), F32),
                   jax.ShapeDtypeStruct((n, kk, kk), BF16),
                   jax.ShapeDtypeStruct((n, kk, 2 * nst), BF16),
                   jax.ShapeDtypeStruct((n, kk, 2 * nst), BF16),
                   jax.ShapeDtypeStruct((n, 2, nst), F32),
                   jax.ShapeDtypeStruct(((t + tc) // CH, bsz, CH * D), F32)],
        scratch_shapes=[pltpu.VMEM((gb, kk, nst), F32)] * 2,
        compiler_params=_cparams(("arbitrary",)),
    )(cond, ada_w, ada_b.reshape(depth, 3, 1, D), lr, li, ls, btr, bti, cr, ci, d, x, pos, ctx)


def _even_in_kernel(x_ref, shift_ref, scale_ref, g_ref, w_ref, wcs_ref, ut_ref, xw_ref, z_ref,
                    wat_ref, wbz_ref, *scr, ncl):
    @pl.when(pl.program_id(0) == 0)
    def _():
        wat_ref[...] = w_ref[:, :S5_W].T.astype(BF16)
        wbz_ref[...] = w_ref[:, S5_W:].astype(BF16)

    x = x_ref[...]
    nc, nb, _ = x.shape
    rows = nc * nb
    gain = g_ref[...] * (1.0 + scale_ref[...])
    h = jnp.concatenate([_mod_norm(x[0:ncl], gain[0:nb][None], shift_ref[0:nb][None]),
                         _mod_norm(x[ncl:nc], gain[nb:nb + 1][None], shift_ref[nb:nb + 1][None])], axis=0)
    hb = h.reshape(rows, D).astype(BF16)
    pt = lax.dot_general(wat_ref[...], hb, (((1,), (1,)), ((), ())), preferred_element_type=F32)
    ut_ref[...] = pt.astype(BF16).reshape(S5_G, S5_H, rows)
    p = jnp.dot(hb, wbz_ref[...], preferred_element_type=F32)
    z_ref[...] = _silu(p[:, FN_W:]).astype(BF16)
    xw = jnp.dot(p[:, :FN_W].astype(BF16), wcs_ref[...], preferred_element_type=F32)
    for q in range(4):
        scr[q][...] = xw[:, q * LANE:(q + 1) * LANE]
    for q in range(4):
        part, half = divmod(q, 2)
        for bi in range(nb):
            piece = scr[q][pl.ds(bi, nc, stride=nb), :]
            lo = bi * FN_W + half * LANE
            xw_ref[part, :, lo:lo + LANE] = piece.astype(BF16)


def _even_in(xs, ncl, ada, l, g, w_in, layer, wcs):
    nc, bsz, _ = xs.shape
    rows = nc * bsz
    return pl.pallas_call(
        functools.partial(_even_in_kernel, ncl=ncl),
        grid=(CH,),
        in_specs=[pl.BlockSpec((nc, bsz, D), lambda i: (0, 0, i)),
                  pl.BlockSpec((None, None) + ada.shape[2:], lambda i: (l, 0, 0, 0)),
                  pl.BlockSpec((None, None) + ada.shape[2:], lambda i: (l, 1, 0, 0)),
                  pl.BlockSpec((None, 1, D), lambda i: (l, 0, 0)),
                  pl.BlockSpec((None, D, w_in.shape[2]), lambda i: (layer, 0, 0),
                               pipeline_mode=pl.Buffered(1)),
                  pl.BlockSpec((FN_W, 2 * FN_W), lambda i: (0, 0))],
        out_specs=[pl.BlockSpec((S5_G, S5_H, rows), lambda i: (0, i, 0)),
                   pl.BlockSpec((2, None, nc, bsz * FN_W), lambda i: (0, i, 0, 0)),
                   pl.BlockSpec((None, rows, MIX), lambda i: (i, 0, 0))],
        out_shape=[jax.ShapeDtypeStruct((S5_G, CH * S5_H, rows), BF16),
                   jax.ShapeDtypeStruct((2, CH, nc, bsz * FN_W), BF16),
                   jax.ShapeDtypeStruct((CH, rows, MIX), BF16)],
        scratch_shapes=[pltpu.VMEM((S5_W, D), BF16), pltpu.VMEM((D, FN_W + MIX), BF16)]
        + [pltpu.VMEM((rows, LANE), F32)] * 4,
        compiler_params=_cparams(("arbitrary",)),
    )(xs, ada, ada, g, w_in, wcs)


def _s5_kernel(ut_ref, mt_ref, be_ref, cp_ref, l16_ref, yt_ref,
               sre_ref, sim_ref, are_ref, aim_ref, bre_ref, bim_ref, *, bsz, ncl, ncc, gs):
    nl = bsz * ncl
    nst = 2 * S5_P
    for g in range(gs):
        st = lax.dot_general(ut_ref[g], be_ref[g], (((0,), (0,)), ((), ())), preferred_element_type=F32)
        sre_ref[g] = st[:, :nst]
        sim_ref[g] = st[:, nst:]

    lam = [l16_ref[g] for g in range(gs)]
    is_fwd = lax.broadcasted_iota(jnp.int32, (bsz, nst), 1) < S5_P

    def make_step(base, nchunk):
        def step(c, carry):
            rf = pl.ds(pl.multiple_of(base + c * bsz, bsz), bsz)
            rb = pl.ds(pl.multiple_of(base + (nchunk - 1 - c) * bsz, bsz), bsz)
            out = []
            for g in range(gs):
                sre, sim = carry[2 * g], carry[2 * g + 1]
                lre = lam[g][0:1, :]
                lim = lam[g][1:2, :]
                are_ref[g, rf, :] = sre
                aim_ref[g, rf, :] = sim
                bre_ref[g, rb, :] = sre
                bim_ref[g, rb, :] = sim
                in_re = jnp.where(is_fwd, sre_ref[g, rf, :], sre_ref[g, rb, :])
                in_im = jnp.where(is_fwd, sim_ref[g, rf, :], sim_ref[g, rb, :])
                out.append(lre * sre - lim * sim + in_re)
                out.append(lre * sim + lim * sre + in_im)
            return tuple(out)
        return step

    zero = jnp.zeros((bsz, nst), F32)
    carry = lax.fori_loop(0, ncc, make_step(nl, ncc), (zero,) * (2 * gs))
    lax.fori_loop(0, ncl, make_step(0, ncl), carry)

    nt = (((1,), (1,)), ((), ()))
    fwd_rows = lax.broadcasted_iota(jnp.int32, (bsz * (ncl + ncc), nst), 1) < S5_P
    for g in range(gs):
        s0 = jnp.concatenate([jnp.where(fwd_rows, are_ref[g], bre_ref[g]),
                              jnp.where(fwd_rows, aim_ref[g], bim_ref[g])], axis=1).astype(BF16)
        yt_ref[g] = (jnp.dot(mt_ref[g], ut_ref[g], preferred_element_type=F32)
                     + lax.dot_general(cp_ref[g], s0, nt, preferred_element_type=F32)).astype(BF16)


def _s5_mix(ut, ncl, mt, bend, cp, lam16, bsz, layer):
    rows = ut.shape[2]
    kk = CH * S5_H
    nst = 2 * S5_P
    gs = 6
    off = layer * (S5_G // gs)

    def gspec(r, c):
        return pl.BlockSpec((gs, r, c), lambda g: (g, 0, 0))

    def tspec(r, c):
        return pl.BlockSpec((gs, r, c), lambda g: (g + off, 0, 0))

    return pl.pallas_call(
        functools.partial(_s5_kernel, bsz=bsz, ncl=ncl, ncc=rows // bsz - ncl, gs=gs),
        grid=(S5_G // gs,),
        in_specs=[gspec(kk, rows), tspec(kk, kk), tspec(kk, 2 * nst), tspec(kk, 2 * nst), tspec(2, nst)],
        out_specs=gspec(kk, rows),
        out_shape=jax.ShapeDtypeStruct((S5_G, kk, rows), BF16),
        scratch_shapes=[pltpu.VMEM((gs, rows, nst), F32)] * 6,
        compiler_params=_cparams(("arbitrary",)),
    )(ut, mt, bend, cp, lam16)


def _fnet_weights_kernel(ccs_ref, fw_ref, o_ref):
    fw = fw_ref[...]
    ccs = ccs_ref[...]
    wc = jnp.dot(ccs[:, :FN_W], fw, precision=HI, preferred_element_type=F32)
    ws = jnp.dot(ccs[:, FN_W:], fw, precision=HI, preferred_element_type=F32)
    o_ref[...] = jnp.concatenate([wc, ws], axis=1).astype(BF16)


def _fnet_weights(fw_bd):
    return pl.pallas_call(
        _fnet_weights_kernel,
        out_shape=jax.ShapeDtypeStruct((FN_W, 2 * FN_W), BF16),
    )(_dft_channel_matrix(), fw_bd)


def _fnet_kernel(tab_ref, v_ref, jr_ref, o_ref, *, ipt):
    r = pl.program_id(1)
    tab = tab_ref[...].astype(BF16)
    v = v_ref[...]
    nc, n = v.shape[2], v.shape[3]
    t = CH * nc
    a = jnp.dot(tab[:, :t], v[0].reshape(t, n), preferred_element_type=F32)
    b = jnp.dot(tab[:, t:], v[1].reshape(t, n), preferred_element_type=F32)
    o_ref[pl.ds(r * ipt, ipt)] = (a - b).reshape(ipt, nc, n).astype(BF16)
    m = (a + b).astype(BF16)
    for k in range(ipt):
        i = r * ipt + k

        @pl.when(jnp.logical_and(i >= 1, i <= CH // 2 - 1))
        def _():
            o_ref[CH - i] = jnp.dot(jr_ref[...], m[k * nc:(k + 1) * nc],
                                    preferred_element_type=F32).astype(BF16)


def _dft_half_table(nc):
    t = nc * CH
    cols = (np.arange(nc)[None, :] * CH + np.arange(CH)[:, None]).reshape(-1)
    rows = (np.arange(nc)[None, :] * CH + np.arange(CH // 2 + 1)[:, None]).reshape(-1)
    prod = (rows[:, None].astype(np.int64) * cols[None, :].astype(np.int64)) % t
    ang = prod.astype(np.float64) * (2.0 * np.pi / t)
    scale = 1.0 / math.sqrt(t * FN_GW)
    return jnp.asarray(np.concatenate([np.cos(ang), np.sin(ang)], axis=1) * scale, dtype=F32)


def _dft_channel_matrix():
    c = np.arange(FN_GW)
    ang = (c[:, None] * c[None, :] % FN_GW).astype(np.float64) * (2.0 * np.pi / FN_GW)
    eye = np.eye(FN_G)
    return jnp.asarray(np.concatenate([np.kron(eye, np.cos(ang)), np.kron(eye, np.sin(ang))], axis=1),
                       dtype=F32)


def _fnet(xw, c0, nc, bsz):
    t = nc * CH
    nhb = CH // 2 + 1
    ipt = 3
    bpb = 4
    jr = jnp.asarray(np.eye(nc)[::-1], dtype=BF16)
    return pl.pallas_call(
        functools.partial(_fnet_kernel, ipt=ipt),
        grid=(bsz // bpb, nhb // ipt),
        in_specs=[pl.BlockSpec((ipt * nc, 2 * t), lambda b, r: (r, 0)),
                  pl.BlockSpec((2, CH, nc, bpb * FN_W), lambda b, r: (0, 0, c0 // nc, b)),
                  pl.BlockSpec((nc, nc), lambda b, r: (0, 0))],
        out_specs=pl.BlockSpec((CH, nc, bpb * FN_W), lambda b, r: (0, 0, b)),
        out_shape=jax.ShapeDtypeStruct((CH, nc, bsz * FN_W), BF16),
        compiler_params=_cparams(("arbitrary", "arbitrary")),
    )(_dft_half_table(nc), xw, jr)


def _even_out_kernel(*refs, ncl, with_ctx):
    it = iter(refs)
    yt_ref, ybl_ref = next(it), next(it)
    ybc_ref = next(it) if with_ctx else None
    z_ref, x_ref, gate_ref, gwf_ref, gb_ref, wof_ref, o_ref = (next(it), next(it), next(it), next(it),
                                                               next(it), next(it), next(it))
    scr = [next(it), next(it)]
    gw_ref, wo_ref = next(it), next(it)

    @pl.when(pl.program_id(0) == 0)
    def _():
        gw_ref[...] = (0.5 * gwf_ref[...]).astype(BF16)
        wo_ref[...] = wof_ref[...].astype(BF16)

    rows = yt_ref.shape[2]
    nc, nb, _ = x_ref.shape
    for half in range(2):
        for bi in range(nb):
            lo = bi * FN_W + half * LANE
            scr[half][pl.ds(bi, ncl, stride=nb), :] = ybl_ref[:, lo:lo + LANE].astype(F32)
            if with_ctx:
                scr[half][pl.ds(ncl * nb + bi, nc - ncl, stride=nb), :] = (
                    ybc_ref[:, lo:lo + LANE].astype(F32))

    for r0 in range(0, rows, ROW_BLOCK):
        rb = min(ROW_BLOCK, rows - r0)
        yt = yt_ref[:, :, r0:r0 + rb]
        ya = _gelu_tanh(yt.astype(F32).reshape(S5_W, rb).T)
        half_g = jnp.dot(ya.astype(BF16), gw_ref[...], preferred_element_type=F32) + 0.5 * gb_ref[...]
        hy = 0.5 * ya
        ya = hy + hy * jnp.tanh(half_g)
        yb = jnp.concatenate([scr[0][r0:r0 + rb, :], scr[1][r0:r0 + rb, :]], axis=1)
        sz = z_ref[r0:r0 + rb, :]
        ma = (ya * sz[:, :S5_W]).astype(BF16)
        mb = (yb * sz[:, S5_W:]).astype(BF16)
        out = (jnp.dot(ma, wo_ref[0:S5_W, :], preferred_element_type=F32)
               + jnp.dot(mb, wo_ref[S5_W:MIX, :], preferred_element_type=F32))
        ca, cb = r0 // nb, (r0 + rb) // nb
        out = out.reshape(cb - ca, nb, D)
        cm = min(cb, ncl)
        if ca < cm:
            o_ref[ca:cm] = x_ref[ca:cm] + gate_ref[0:nb][None] * out[0:cm - ca]
        if cm < cb:
            lo = max(ca, ncl)
            o_ref[lo:cb] = x_ref[lo:cb] + gate_ref[nb:nb + 1][None] * out[lo - ca:cb - ca]


def _even_out(yt, ybl, ybc, z, xs, ncl, ada, l, glu_w, glu_b, w_out, layer):
    bsz = xs.shape[1]
    with_ctx = ybc is not None
    nc = xs.shape[0] if with_ctx else ncl
    rows = nc * bsz
    args = [yt, ybl]
    specs = [pl.BlockSpec((S5_G, S5_H, rows), lambda j: (0, j, 0)),
             pl.BlockSpec((None, ncl, bsz * FN_W), lambda j: (j, 0, 0))]
    if with_ctx:
        args.append(ybc)
        specs.append(pl.BlockSpec((None, nc - ncl, bsz * FN_W), lambda j: (j, 0, 0)))
    args += [z, xs, ada, glu_w, glu_b, w_out]
    specs += [pl.BlockSpec((None, rows, MIX), lambda j: (j, 0, 0)),
              pl.BlockSpec((nc, bsz, D), lambda j: (0, 0, j)),
              pl.BlockSpec((None, None) + ada.shape[2:], lambda j: (l, 2, 0, 0)),
              pl.BlockSpec((None, S5_W, S5_W), lambda j: (layer, 0, 0), pipeline_mode=pl.Buffered(1)),
              pl.BlockSpec((None, 1, S5_W), lambda j: (layer, 0, 0)),
              pl.BlockSpec((None, MIX, D), lambda j: (layer, 0, 0), pipeline_mode=pl.Buffered(1))]
    return pl.pallas_call(
        functools.partial(_even_out_kernel, ncl=ncl, with_ctx=with_ctx),
        grid=(CH,),
        in_specs=specs,
        out_specs=pl.BlockSpec((nc, bsz, D), lambda j: (0, 0, j)),
        out_shape=jax.ShapeDtypeStruct((nc, bsz, CH * D), F32),
        input_output_aliases={4: 0} if with_ctx else {},
        scratch_shapes=[pltpu.VMEM((rows, LANE), F32)] * 2 + [pltpu.VMEM((S5_W, S5_W), BF16),
                                                             pltpu.VMEM((MIX, D), BF16)],
        compiler_params=_cparams(("arbitrary",)),
    )(*args)


def _odd_kernel(*refs, ct, ntl, t_lat, t_ctx, final):
    it = iter(refs)
    xm_ref, xn_ref = next(it), next(it)
    shift_ref, scale_ref, gate_ref, g_ref = next(it), next(it), next(it), next(it)
    wif_ref, wof_ref, pwf_ref, ps_ref, cw_ref = next(it), next(it), next(it), next(it), next(it)
    fg_ref = next(it) if final else None
    o_ref = next(it)
    h_ref = next(it)
    pe_ref = next(it)
    tail_ref = next(it)
    wi_ref, wo_ref, pw_ref = next(it), next(it), next(it)
    fin_ref = next(it) if final else None

    ti = pl.program_id(0)
    is_ctx = ti >= ntl
    t0 = jnp.where(is_ctx, ti - ntl, ti) * (ct * CH)
    t_total = jnp.where(is_ctx, t_ctx, t_lat)

    @pl.when(ti == 0)
    def _():
        wi_ref[...] = wif_ref[...].astype(BF16)
        wo_ref[...] = wof_ref[...].astype(BF16)
        pw_ref[...] = pwf_ref[...].astype(BF16)
        tail_ref[...] = jnp.zeros(tail_ref.shape, F32)

    n1 = POOL_W + 2 * CONV_W
    nb = xm_ref.shape[1]
    tm = ct * CH
    ne = tm + 2 * HALO

    def rows_of(ref):
        return jnp.where(is_ctx, ref[nb:nb + 1], ref[0:nb])

    gain = g_ref[...] * (1.0 + rows_of(scale_ref))
    shift = rows_of(shift_ref)

    def hn(xv):
        return _mod_norm(xv, gain, shift)

    for i in range(CH):
        h_ref[0:ct, i] = hn(xm_ref[:, :, i * D:(i + 1) * D])
    for i in range(HALO):
        h_ref[ct, i] = hn(xn_ref[:, :, i * D:(i + 1) * D])[0]

    m0 = HALO * nb
    mr = tm * nb
    he = h_ref[...].reshape((ct + 1) * CH * nb, D)[0:mr + m0].astype(BF16)
    pe = jnp.dot(he, wi_ref[:, 0:n1], preferred_element_type=F32)
    uc = pe[:, :POOL_W]
    v = pe[:, POOL_W:POOL_W + CONV_W] * pe[:, POOL_W + CONV_W:]
    pe_ref[m0:m0 + mr, 0:POOL_W] = uc[0:mr]
    pe_ref[m0:m0 + mr, POOL_W:POOL_W + CONV_W] = v[0:mr]
    valid = (t0 + tm) < t_total
    pe_ref[m0 + mr:, 0:POOL_W] = jnp.where(valid, uc[mr:], 0.0)
    pe_ref[m0 + mr:, POOL_W:POOL_W + CONV_W] = jnp.where(valid, v[mr:], 0.0)
    pe_ref[0:m0, :] = jnp.where(t0 > 0, tail_ref[...], 0.0)

    p2 = jnp.dot(he[0:mr], wi_ref[:, n1:], preferred_element_type=F32)
    b_gate = p2[:, :CONV_W]
    sz = _silu(p2[:, CONV_W:])

    tpos = t0 + lax.broadcasted_iota(jnp.int32, (mr, 1), 0) // nb
    pooled = []
    for gi, w in enumerate(POOL_WINDOWS):
        c0 = gi * POOL_GW
        s = pe_ref[:, c0:c0 + POOL_GW]
        n = ne
        width = 1
        while width < w:
            s = s[0:(n - width) * nb] + s[width * nb:n * nb]
            n -= width
            width *= 2
        start = (HALO - w // 2) * nb
        total = s[start:start + mr]
        hi = jnp.minimum(tpos + w // 2, t_total)
        lo = jnp.maximum(tpos - w // 2, 0)
        inv_cnt = 1.0 / (hi - lo).astype(F32)
        centre = pe_ref[m0:m0 + mr, c0:c0 + POOL_GW]
        pg = total * inv_cnt - centre
        pooled.append(jnp.dot(pg.astype(BF16), pw_ref[gi], preferred_element_type=F32))
    y_c = jnp.concatenate(pooled, axis=1) * ps_ref[...]

    cwt = cw_ref[...]
    vm = pe_ref[m0 - nb:m0 - nb + mr, POOL_W:POOL_W + CONV_W]
    v0 = pe_ref[m0:m0 + mr, POOL_W:POOL_W + CONV_W]
    vp = pe_ref[m0 + nb:m0 + nb + mr, POOL_W:POOL_W + CONV_W]
    y_d = b_gate * (vm * cwt[0:1, :] + v0 * cwt[1:2, :] + vp * cwt[2:3, :])

    y = (jnp.concatenate([y_c, y_d], axis=1) * sz).astype(BF16)
    out = jnp.dot(y, wo_ref[...], preferred_element_type=F32)
    go = (rows_of(gate_ref) * out.reshape(tm, nb, D)).reshape(ct, CH, nb, D)
    for i in range(CH):
        xo = xm_ref[:, :, i * D:(i + 1) * D] + go[:, i]
        if not final:
            o_ref[:, :, i * D:(i + 1) * D] = xo
            continue
        ms = jnp.mean(xo * xo, axis=-1, keepdims=True)
        fin_ref[:, :, i * D:(i + 1) * D] = xo * lax.rsqrt(ms + EPS) * fg_ref[...]
    tail_ref[...] = pe_ref[mr:mr + m0, :]
    if final:
        o_ref[...] = pltpu.einshape("cb(id)->b(ci)d", fin_ref[...], i=CH)


def _odd_layer(xs, ncl, with_ctx, ada, l, g, w_in, w_out, pool_w, layer, pool_scale, conv_w, final_g):
    nc_all, bsz, _ = xs.shape
    ct = 8
    ntl = ncl // ct
    nc = nc_all if with_ctx else ncl
    final = final_g is not None
    args = [xs, xs, ada, ada, ada, g, w_in, w_out, pool_w, pool_scale, conv_w]

    def vec(j):
        return pl.BlockSpec((None, None) + ada.shape[2:], lambda i: (l, j, 0, 0))

    def const(*shape):
        return pl.BlockSpec(shape, lambda i: (0,) * len(shape), pipeline_mode=pl.Buffered(1))

    def stacked(*shape):
        return pl.BlockSpec((None,) + shape, lambda i: (layer,) + (0,) * len(shape),
                            pipeline_mode=pl.Buffered(1))

    specs = [pl.BlockSpec((ct, bsz, CH * D), lambda i: (i, 0, 0)),
             pl.BlockSpec((1, bsz, CH * D), lambda i: (jnp.minimum((i + 1) * ct, nc_all - 1), 0, 0)),
             vec(0), vec(1), vec(2),
             pl.BlockSpec((None, 1, D), lambda i: (l, 0, 0)), stacked(*w_in.shape[1:]), stacked(MIX, D),
             stacked(len(POOL_WINDOWS), POOL_GW, POOL_GW), stacked(1, POOL_W), stacked(3, CONV_W)]
    scratch = [pltpu.VMEM((ct + 1, CH, bsz, D), F32),
               pltpu.VMEM(((ct + 1) * CH * bsz, POOL_W + CONV_W), F32),
               pltpu.VMEM((HALO * bsz, POOL_W + CONV_W), F32),
               pltpu.VMEM(w_in.shape[1:], BF16), pltpu.VMEM((MIX, D), BF16),
               pltpu.VMEM((len(POOL_WINDOWS), POOL_GW, POOL_GW), BF16)]
    if final:
        args.append(final_g.reshape(1, D))
        specs.append(const(1, D))
        scratch.append(pltpu.VMEM((ct, bsz, CH * D), F32))
        out_spec = pl.BlockSpec((bsz, ct * CH, D), lambda i: (0, i, 0))
        out_shape = jax.ShapeDtypeStruct((bsz, nc * CH, D), F32)
    else:
        out_spec = pl.BlockSpec((ct, bsz, CH * D), lambda i: (i, 0, 0))
        out_shape = jax.ShapeDtypeStruct((nc, bsz, CH * D), F32)
    return pl.pallas_call(
        functools.partial(_odd_kernel, ct=ct, ntl=ntl, t_lat=ncl * CH, t_ctx=(nc_all - ncl) * CH,
                          final=final),
        grid=(nc // ct,),
        in_specs=specs,
        out_specs=out_spec,
        out_shape=out_shape,
        scratch_shapes=scratch,
        compiler_params=_cparams(("arbitrary",)),
    )(*args)


def _sincos_table(n_tok, dim):
    rows = n_tok // GRID_W
    rr, cc = np.meshgrid(np.arange(rows, dtype=np.float64), np.arange(GRID_W, dtype=np.float64),
                         indexing='ij')
    rr = rr.reshape(-1, 1)
    cc = cc.reshape(-1, 1)
    quarter = dim // 4
    omega = POS_BASE ** (-np.arange(quarter, dtype=np.float64) / quarter)
    tab = np.concatenate([np.sin(rr * omega), np.cos(rr * omega), np.sin(cc * omega), np.cos(cc * omega)],
                         axis=-1)
    return jnp.asarray(tab, dtype=F32)


def _block_diag(w):
    g, c, _ = w.shape
    eye = jnp.eye(g, dtype=w.dtype)
    return (eye[:, None, :, None] * w[:, :, None, :]).reshape(g * c, g * c)


def kernel(x, c, ctx, c_ctx, norm_g, ada_w, ada_b, even_w_in, even_w_out, s5_lam_re, s5_lam_im, s5_log_step, s5_b_re, s5_b_im, s5_c_re, s5_c_im, s5_d, s5_glu_w, s5_glu_b, fnet_w, odd_w_in, odd_w_out, pool_w, pool_scale, conv_w, final_g):
    bsz, n_tok, _ = x.shape
    depth = norm_g.shape[0]
    ncl = n_tok // CH
    ncc = ctx.shape[1] // CH

    cond = jnp.concatenate([c, jnp.broadcast_to(c_ctx[None], (16 - bsz, D))], axis=0)
    ada, mt, bend, cp, lam16, xs = _prologue(cond, ada_w, ada_b, s5_lam_re, s5_lam_im, s5_log_step, s5_b_re,
                                             s5_b_im, s5_c_re, s5_c_im, s5_d, x, _sincos_table(n_tok, D), ctx)

    need_ctx = [any(j % 2 == 0 for j in range(l + 1, depth)) for l in range(depth)]
    norm_g3 = norm_g.reshape(depth, 1, D)
    glu_b3 = s5_glu_b.reshape(-1, 1, S5_W)
    pool_scale3 = pool_scale.reshape(-1, 1, POOL_W)

    for l in range(depth):
        i = l // 2
        last = l == depth - 1
        if l % 2 == 0:
            wcs = _fnet_weights(_block_diag(fnet_w[i]))
            ut, xw, z = _even_in(xs, ncl, ada, l, norm_g3, even_w_in, i, wcs)
            yt = _s5_mix(ut, ncl, mt, bend, cp, lam16, bsz, i)
            ybl = _fnet(xw, 0, ncl, bsz)
            ybc = _fnet(xw, ncl, ncc, bsz) if need_ctx[l] else None
            xs = _even_out(yt, ybl, ybc, z, xs, ncl, ada, l, s5_glu_w, glu_b3, even_w_out, i)
        else:
            xs = _odd_layer(xs, ncl, need_ctx[l], ada, l, norm_g3, odd_w_in, odd_w_out, pool_w, i,
                            pool_scale3, conv_w, final_g if last else None)
    if depth % 2 == 1:
        raise NotImplementedError("final norm and (B, T, D) order are produced by the last (odd) layer")
    return xs
```
